```python
import math
import jax, jax.numpy as jnp
from jax import lax
import numpy as np


D_MODEL = 1024
BATCH = 16
SEQ = 256
DEPTH = 2
DEC_BATCH = 4
DEC_SEQ = 2048
PAST_LEN = 512

GRID_W = 64
N_EVEN = (DEPTH + 1) // 2
N_ODD = DEPTH // 2
H_RET = 4
RET_DK = 128
RET_DV = 128
RET_W = H_RET * RET_DV
RET_CHUNK = 128
H_DIFF = 4
DIFF_QK = 64
DIFF_V = 128
DIFF_W = H_DIFF * DIFF_V
Q_BLOCK = 128
ROPE_BASE = 10000.0
HY_BANDS = 16
HY_EMB = 1 + 2 * HY_BANDS
HY_FH = 64
HY_DECAY_TARGET = 1e-2
HY_FAST = 0.3
HY_SLOW = 1.5
HY_SHIFT = 0.05
N_EXPERTS = 16
N_GROUPS = 4
TOP_K = 2
D_FF_EXPERT = 512
ALPHA = (2 * DEPTH) ** 0.25
BETA = (8 * DEPTH) ** -0.25
LN_EPS = 1e-5

kernel_name = 'hybrid_retention_diffattn_hyena_moe_diffusion_step'


def layer_norm(x, w, b):
    xf = x.astype(jnp.float32)
    mu = xf.mean(-1, keepdims=True)
    var = jnp.square(xf - mu).mean(-1, keepdims=True)
    y = (xf - mu) * lax.rsqrt(var + LN_EPS) * w.astype(jnp.float32) + b.astype(jnp.float32)
    return y.astype(x.dtype)


def axial_rope(x):
    L = x.shape[1]
    d = x.shape[-1]
    half = d // 2
    quarter = half // 2
    rows = L // GRID_W
    t = jnp.arange(rows * GRID_W)
    row = (t // GRID_W).astype(jnp.float32)
    col = (t % GRID_W).astype(jnp.float32)
    inv = ROPE_BASE ** (-jnp.arange(quarter, dtype=jnp.float32) / quarter)
    bshape = (L,) + (1,) * (x.ndim - 3) + (quarter,)

    def rot(xh, pos):
        ang = (pos[:, None] * inv[None, :]).reshape(bshape)
        cos, sin = jnp.cos(ang), jnp.sin(ang)
        x1, x2 = xh[..., :quarter], xh[..., quarter:]
        return jnp.concatenate([x1 * cos - x2 * sin, x1 * sin + x2 * cos], axis=-1)

    xf = x.astype(jnp.float32)
    return jnp.concatenate([rot(xf[..., :half], row), rot(xf[..., half:], col)], axis=-1).astype(x.dtype)


def retention_dir(q, k, v, log_gamma, s0):
    B, L, H, dk = q.shape
    dv = v.shape[-1]
    n = L // RET_CHUNK
    qc = q.reshape(B, n, RET_CHUNK, H, dk)
    kc = k.reshape(B, n, RET_CHUNK, H, dk)
    vc = v.reshape(B, n, RET_CHUNK, H, dv)
    idx = jnp.arange(RET_CHUNK, dtype=jnp.float32)
    rel = idx[:, None] - idx[None, :]
    dmat = jnp.where(rel >= 0, jnp.exp(jnp.maximum(rel, 0.0)[None] * log_gamma[:, None, None]), 0.0)
    xi = jnp.exp((idx + 1.0)[None, :] * log_gamma[:, None])
    zeta = jnp.exp((RET_CHUNK - 1.0 - idx)[None, :] * log_gamma[:, None])
    g_chunk = jnp.exp(RET_CHUNK * log_gamma)
    scores = jnp.einsum('bnihd,bnjhd->bnhij', qc, kc) * dmat
    inner = jnp.einsum('bnhij,bnjhe->bnihe', scores, vc)
    kv = jnp.einsum('bnjhd,hj,bnjhe->nbhde', kc, zeta, vc)

    def step(s, kv_i):
        return g_chunk[:, None, None] * s + kv_i, s

    s_final, s_prev = lax.scan(step, s0, kv)
    cross = jnp.einsum('bnihd,nbhde,hi->bnihe', qc, s_prev, xi)
    return (inner + cross).reshape(B, L, H, dv), s_final


def diff_attention(q, k, v, lam):
    B, Lq, H, _, dq = q.shape
    nb = Lq // Q_BLOCK
    qb = jnp.moveaxis(q.reshape(B, nb, Q_BLOCK, H, 2, dq), 1, 0)
    scale = dq ** -0.5
    vf = v.astype(jnp.float32)

    def block(qi):
        s = jnp.einsum('bqhrd,bkhrd->bhrqk', qi, k).astype(jnp.float32) * scale
        p = jax.nn.softmax(s, axis=-1)
        a = p[:, :, 0] - lam * p[:, :, 1]
        return jnp.einsum('bhqk,bkhd->bqhd', a, vf)

    o = lax.map(block, qb)
    return jnp.moveaxis(o, 0, 1).reshape(B, Lq, H, v.shape[-1])


def implicit_filter(L, w1, b1, w2, b2, w3, freq):
    f32 = jnp.float32
    t = jnp.linspace(0.0, 1.0, L, dtype=f32)[:, None]
    bands = jnp.linspace(1e-4, HY_BANDS - 1, HY_BANDS, dtype=f32)
    ang = 2.0 * math.pi * bands[None, :] * jnp.arange(L, dtype=f32)[:, None] / L
    z = jnp.concatenate([t, jnp.cos(ang), -jnp.sin(ang)], axis=-1)
    fr = freq.astype(f32)
    a = jnp.sin(fr * (z @ w1.astype(f32) + b1.astype(f32)))
    a = jnp.sin(fr * (a @ w2.astype(f32) + b2.astype(f32)))
    filt = a @ w3.astype(f32)
    deltas = jnp.abs(jnp.linspace(math.log(HY_DECAY_TARGET) / HY_SLOW,
                                  math.log(HY_DECAY_TARGET) / HY_FAST, D_MODEL, dtype=f32))
    window = jnp.exp(-t * deltas[None, :]) + HY_SHIFT
    return filt[:, :D_MODEL] * window, filt[:, D_MODEL:] * window


def bidir_fft_conv(u, h_fwd, h_bwd, bias):
    L = u.shape[1]
    D = u.shape[2]
    uf = u.astype(jnp.float32)
    taps = jnp.concatenate([h_fwd, jnp.zeros((1, D), jnp.float32), h_bwd[1:][::-1]], axis=0)
    spec = jnp.fft.rfft(uf, n=2 * L, axis=1) * jnp.fft.rfft(taps, n=2 * L, axis=0)[None]
    y = jnp.fft.irfft(spec, n=2 * L, axis=1)[:, :L]
    return (y + uf * bias.astype(jnp.float32)).astype(u.dtype)


def moe_ffn(h, router_w, router_bias, w_gate, w_up, w_down):
    B, L, D = h.shape
    t = h.reshape(B * L, D)
    T = t.shape[0]
    scores = jax.nn.sigmoid((t @ router_w).astype(jnp.float32))
    sel = scores + router_bias.astype(jnp.float32)
    grp = sel.reshape(T, N_GROUPS, N_EXPERTS // N_GROUPS)
    gscore = lax.top_k(grp, TOP_K)[0].sum(-1)
    gmask = jax.nn.one_hot(jnp.argmax(gscore, axis=-1), N_GROUPS) > 0
    emask = jnp.repeat(gmask, N_EXPERTS // N_GROUPS, axis=-1)
    _, eidx = lax.top_k(jnp.where(emask, sel, -jnp.inf), TOP_K)
    wsel = jnp.take_along_axis(scores, eidx, axis=-1)
    wsel = wsel / wsel.sum(-1, keepdims=True)
    comb = jnp.sum(jax.nn.one_hot(eidx, N_EXPERTS, dtype=jnp.float32) * wsel[..., None], axis=1)
    hg = jnp.einsum('td,edf->tef', t, w_gate)
    hu = jnp.einsum('td,edf->tef', t, w_up)
    act = jax.nn.silu(hg) * hu * comb[:, :, None].astype(t.dtype)
    return jnp.einsum('tef,efd->td', act, w_down).reshape(B, L, D)


def even_mixer(h, P, e, l, ctx):
    B, L, _ = h.shape
    f32 = jnp.float32
    proj = h @ P['ev_w_in'][e]
    widths = [H_RET * RET_DK, H_RET * RET_DK, RET_W, RET_W, H_DIFF * 2 * DIFF_QK, H_DIFF * 2 * DIFF_QK, DIFF_W]
    q_r, k_r, v_r, g_r, q_d, k_d, v_d = jnp.split(proj, np.cumsum(widths)[:-1].tolist(), axis=-1)
    q_r = q_r.reshape(B, L, H_RET, RET_DK)
    k_r = k_r.reshape(B, L, H_RET, RET_DK) * (RET_DK ** -0.5)
    v_r = v_r.reshape(B, L, H_RET, RET_DV)
    q_d = q_d.reshape(B, L, H_DIFF, 2, DIFF_QK)
    k_d = k_d.reshape(B, L, H_DIFF, 2, DIFF_QK)
    v_d = v_d.reshape(B, L, H_DIFF, DIFF_V)
    k_d_ctx = k_d.reshape(B, L, H_DIFF, 2 * DIFF_QK)
    if ctx is None:
        s0f = jnp.zeros((B, H_RET, RET_DK, RET_DV), f32)
        s0b = s0f
        k_all, v_all = k_d, v_d
    else:
        ck, cv, s0f, s0b = ctx
        q_r, k_r = axial_rope(q_r), axial_rope(k_r)
        q_d, k_d = axial_rope(q_d), axial_rope(k_d)
        k_all = jnp.concatenate([k_d, ck.reshape(B, -1, H_DIFF, 2, DIFF_QK).astype(k_d.dtype)], axis=1)
        v_all = jnp.concatenate([v_d, cv.astype(v_d.dtype)], axis=1)
    lgf = jnp.log1p(-jnp.exp2(P['ret_decay_fwd'][e].astype(f32)))
    lgb = jnp.log1p(-jnp.exp2(P['ret_decay_bwd'][e].astype(f32)))
    qf, kf, vf = q_r.astype(f32), k_r.astype(f32), v_r.astype(f32)
    of, sf = retention_dir(qf, kf, vf, lgf, s0f.astype(f32))
    ob, sb = retention_dir(qf[:, ::-1], kf[:, ::-1], vf[:, ::-1], lgb, s0b.astype(f32))
    ret = of + ob[:, ::-1]
    mu = ret.mean(-1, keepdims=True)
    var = jnp.square(ret - mu).mean(-1, keepdims=True)
    ret = ((ret - mu) * lax.rsqrt(var + LN_EPS)).reshape(B, L, RET_W) * P['ret_gn_w'][e].astype(f32)
    ret = jax.nn.silu(g_r.astype(f32)) * ret
    lam_init = 0.8 - 0.6 * math.exp(-0.3 * l)
    lq1, lk1, lq2, lk2 = P['diff_lambda'][e].astype(f32)
    lam = jnp.exp(jnp.sum(lq1 * lk1)) - jnp.exp(jnp.sum(lq2 * lk2)) + lam_init
    att = diff_attention(q_d, k_all, v_all, lam)
    att = att * lax.rsqrt(jnp.mean(jnp.square(att), -1, keepdims=True) + LN_EPS)
    att = (att * P['diff_subln_w'][e].astype(f32) * (1.0 - lam_init)).reshape(B, L, DIFF_W)
    out = jnp.concatenate([ret, att], axis=-1).astype(h.dtype) @ P['ev_w_out'][e]
    return out, (k_d_ctx, v_d, sf.astype(h.dtype), sb.astype(h.dtype))


def hyena_mixer(h, P, o):
    B, L, D = h.shape
    u = h @ P['hy_w_in'][o]
    cw = P['hy_conv_w'][o]
    up = jnp.pad(u, ((0, 0), (1, 1), (0, 0)))
    u = up[:, :-2] * cw[0] + up[:, 1:-1] * cw[1] + up[:, 2:] * cw[2] + P['hy_conv_b'][o]
    x0, x1, v = jnp.split(u, 3, axis=-1)
    hf, hb = implicit_filter(L, P['hy_ffn_w1'][o], P['hy_ffn_b1'][o], P['hy_ffn_w2'][o],
                             P['hy_ffn_b2'][o], P['hy_ffn_w3'][o], P['hy_freq'][o])
    z = bidir_fft_conv(x1 * v, hf, hb, P['hy_filter_bias'][o])
    return (x0 * z) @ P['hy_w_out'][o]


def trunk(x, cond, P, caches):
    produced = []
    for l in range(DEPTH):
        mod = jax.nn.silu(cond) @ P['ada_w'][l] + P['ada_b'][l]
        sh1, sc1, g1, sh2, sc2, g2 = jnp.split(mod[:, None, :], 6, axis=-1)
        h = x * (1.0 + sc1) + sh1
        if l % 2 == 0:
            e = l // 2
            ctx = None if caches is None else caches[e]
            out, ctx_new = even_mixer(h, P, e, l, ctx)
            if caches is None:
                produced.append(ctx_new)
        else:
            out = hyena_mixer(h, P, l // 2)
        x = layer_norm(ALPHA * x + g1 * out, P['ln_w'][l, 0], P['ln_b'][l, 0])
        h = x * (1.0 + sc2) + sh2
        y = moe_ffn(h, P['router_w'], P['router_bias'], P['moe_w_gate'][l], P['moe_w_up'][l], P['moe_w_down'][l])
        x = layer_norm(ALPHA * x + g2 * y, P['ln_w'][l, 1], P['ln_b'][l, 1])
    return x, produced


def setup_inputs(seed: int = 0) -> dict:
    key = jax.random.key(seed)
    ks = jax.random.split(key, 36)
    f32 = jnp.float32
    D = D_MODEL

    def nrm(k, shape, s):
        return jax.random.normal(k, shape, f32) * s

    ev_in = 2 * H_RET * RET_DK + 2 * RET_W + 2 * H_DIFF * 2 * DIFF_QK + DIFF_W
    decay_base = -(5.0 + jnp.arange(H_RET, dtype=f32))[None, :]
    return {
        'x_prompt': nrm(ks[0], (BATCH, SEQ, D), 1.0),
        'x_sample': nrm(ks[1], (DEC_BATCH, DEC_SEQ, D), 1.0),
        'cache_diff_k': nrm(ks[2], (DEC_BATCH, N_EVEN, PAST_LEN, H_DIFF, 2 * DIFF_QK), 1.0),
        'cache_diff_v': nrm(ks[3], (DEC_BATCH, N_EVEN, PAST_LEN, H_DIFF, DIFF_V), 1.0),
        'state_ret_fwd': nrm(ks[4], (DEC_BATCH, N_EVEN, H_RET, RET_DK, RET_DV), 0.3),
        'state_ret_bwd': nrm(ks[5], (DEC_BATCH, N_EVEN, H_RET, RET_DK, RET_DV), 0.3),
        'c': nrm(ks[6], (DEC_BATCH, D), 1.0),
        'c_ctx': nrm(ks[7], (D,), 1.0),
        'ada_w': nrm(ks[8], (DEPTH, D, 6 * D), D ** -0.5),
        'ada_b': nrm(ks[9], (DEPTH, 6 * D), 0.01),
        'ln_w': 1.0 + nrm(ks[10], (DEPTH, 2, D), 0.01),
        'ln_b': nrm(ks[11], (DEPTH, 2, D), 0.01),
        'ev_w_in': nrm(ks[12], (N_EVEN, D, ev_in), D ** -0.5),
        'ev_w_out': nrm(ks[13], (N_EVEN, RET_W + DIFF_W, D), BETA * (RET_W + DIFF_W) ** -0.5),
        'ret_decay_fwd': decay_base + nrm(ks[14], (N_EVEN, H_RET), 0.1),
        'ret_decay_bwd': decay_base + nrm(ks[15], (N_EVEN, H_RET), 0.1),
        'ret_gn_w': 1.0 + nrm(ks[16], (N_EVEN, RET_W), 0.01),
        'diff_lambda': nrm(ks[17], (N_EVEN, 4, DIFF_QK), 0.1),
        'diff_subln_w': 1.0 + nrm(ks[18], (N_EVEN, DIFF_V), 0.01),
        'hy_w_in': nrm(ks[19], (N_ODD, D, 3 * D), D ** -0.5),
        'hy_conv_w': nrm(ks[20], (N_ODD, 3, 3 * D), 3 ** -0.5),
        'hy_conv_b': nrm(ks[21], (N_ODD, 3 * D), 0.01),
        'hy_ffn_w1': nrm(ks[22], (N_ODD, HY_EMB, HY_FH), HY_EMB ** -0.5),
        'hy_ffn_b1': nrm(ks[23], (N_ODD, HY_FH), 0.01),
        'hy_ffn_w2': nrm(ks[24], (N_ODD, HY_FH, HY_FH), HY_FH ** -0.5),
        'hy_ffn_b2': nrm(ks[25], (N_ODD, HY_FH), 0.01),
        'hy_ffn_w3': nrm(ks[26], (N_ODD, HY_FH, 2 * D), 0.1 * HY_FH ** -0.5),
        'hy_freq': 1.0 + nrm(ks[27], (N_ODD, HY_FH), 0.01),
        'hy_filter_bias': nrm(ks[28], (N_ODD, D), 0.5),
        'hy_w_out': nrm(ks[29], (N_ODD, D, D), BETA * D ** -0.5),
        'router_w': nrm(ks[30], (D, N_EXPERTS), D ** -0.5),
        'router_bias': nrm(ks[31], (N_EXPERTS,), 0.01),
        'moe_w_gate': nrm(ks[32], (DEPTH, N_EXPERTS, D, D_FF_EXPERT), D ** -0.5),
        'moe_w_up': nrm(ks[33], (DEPTH, N_EXPERTS, D, D_FF_EXPERT), D ** -0.5),
        'moe_w_down': nrm(ks[34], (DEPTH, N_EXPERTS, D_FF_EXPERT, D), BETA * D_FF_EXPERT ** -0.5),
    }


def reference(x_prompt, x_sample, cache_diff_k, cache_diff_v, state_ret_fwd, state_ret_bwd, c, c_ctx,
              ada_w, ada_b, ln_w, ln_b, ev_w_in, ev_w_out, ret_decay_fwd, ret_decay_bwd, ret_gn_w,
              diff_lambda, diff_subln_w, hy_w_in, hy_conv_w, hy_conv_b, hy_ffn_w1, hy_ffn_b1, hy_ffn_w2,
              hy_ffn_b2, hy_ffn_w3, hy_freq, hy_filter_bias, hy_w_out, router_w, router_bias,
              moe_w_gate, moe_w_up, moe_w_down):
    P = {
        'ada_w': ada_w, 'ada_b': ada_b, 'ln_w': ln_w, 'ln_b': ln_b,
        'ev_w_in': ev_w_in, 'ev_w_out': ev_w_out, 'ret_decay_fwd': ret_decay_fwd,
        'ret_decay_bwd': ret_decay_bwd, 'ret_gn_w': ret_gn_w, 'diff_lambda': diff_lambda,
        'diff_subln_w': diff_subln_w, 'hy_w_in': hy_w_in, 'hy_conv_w': hy_conv_w, 'hy_conv_b': hy_conv_b,
        'hy_ffn_w1': hy_ffn_w1, 'hy_ffn_b1': hy_ffn_b1, 'hy_ffn_w2': hy_ffn_w2, 'hy_ffn_b2': hy_ffn_b2,
        'hy_ffn_w3': hy_ffn_w3, 'hy_freq': hy_freq, 'hy_filter_bias': hy_filter_bias, 'hy_w_out': hy_w_out,
        'router_w': router_w, 'router_bias': router_bias,
        'moe_w_gate': moe_w_gate, 'moe_w_up': moe_w_up, 'moe_w_down': moe_w_down,
    }
    y_prompt, produced = trunk(x_prompt, c_ctx[None, :], P, None)
    new_diff_k = jnp.stack([p[0] for p in produced], axis=1)
    new_diff_v = jnp.stack([p[1] for p in produced], axis=1)
    new_ret_state_fwd = jnp.stack([p[2] for p in produced], axis=1)
    new_ret_state_bwd = jnp.stack([p[3] for p in produced], axis=1)
    caches = [(cache_diff_k[:, e], cache_diff_v[:, e], state_ret_fwd[:, e], state_ret_bwd[:, e])
              for e in range(N_EVEN)]
    y_sample, _ = trunk(x_sample, c, P, caches)
    return (y_prompt, y_sample, new_diff_k, new_diff_v, new_ret_state_fwd, new_ret_state_bwd)
```

```python
import functools
import math

import numpy as np
import jax
import jax.numpy as jnp
from jax import lax
from jax.experimental import pallas as pl
from jax.experimental.pallas import tpu as pltpu

F32 = jnp.float32
BF16 = jnp.bfloat16
HIGHEST = lax.Precision.HIGHEST

D_MODEL = 1024
DEPTH = 2
GRID_W = 64
H_RET = 4
RET_DK = 128
RET_DV = 128
RET_CHUNK = 128
H_DIFF = 4
DIFF_QK = 64
DIFF_V = 128
ROPE_BASE = 10000.0
HY_BANDS = 16
HY_FH = 64
HY_DECAY_TARGET = 1e-2
HY_FAST = 0.3
HY_SLOW = 1.5
HY_SHIFT = 0.05
N_EXPERTS = 16
N_GROUPS = 4
GROUP_SIZE = N_EXPERTS // N_GROUPS
D_FF_EXPERT = 512
ALPHA = (2 * DEPTH) ** 0.25
LN_EPS = 1e-5

LANES = 128
SEC = 512
COND_ROWS = 8
N_MOD = 6
VMEM_LIMIT = 50 * 1024 * 1024

PAIR_SLOTS = ((0, 1), (2, 1), (2, 3), (0, 3), (0, 2), (1, 3))
N_PAIRS = len(PAIR_SLOTS)
N_CLASSES = N_GROUPS * N_PAIRS
MOE_TM = 256


def _cparams(sem):
    return pltpu.CompilerParams(dimension_semantics=sem, vmem_limit_bytes=VMEM_LIMIT)


def _silu(x):
    return x * jax.nn.sigmoid(x)


def _layer_norm(x, w, b):
    mu = jnp.mean(x, axis=-1, keepdims=True)
    xc = x - mu
    var = jnp.mean(xc * xc, axis=-1, keepdims=True)
    return xc * lax.rsqrt(var + LN_EPS) * w + b


def _ada_kernel(c_ref, w_ref, b_ref, o_ref):
    c = c_ref[...]
    o_ref[...] = jnp.dot(_silu(c), w_ref[...], precision=HIGHEST, preferred_element_type=F32) + b_ref[...]


def _ada_mod(cond8, ada_w, ada_b):
    tn = 1024
    nj = ada_w.shape[2] // tn
    return pl.pallas_call(
        _ada_kernel,
        grid=(DEPTH, nj),
        in_specs=[
            pl.BlockSpec((COND_ROWS, D_MODEL), lambda l, j: (0, 0)),
            pl.BlockSpec((None, D_MODEL, tn), lambda l, j: (l, 0, j)),
            pl.BlockSpec((None, 1, tn), lambda l, j: (l, 0, j)),
        ],
        out_specs=pl.BlockSpec((None, COND_ROWS, tn), lambda l, j: (l, 0, j)),
        out_shape=jax.ShapeDtypeStruct((DEPTH, COND_ROWS, ada_w.shape[2]), F32),
        compiler_params=_cparams(("arbitrary", "arbitrary")),
        name="ada_mod",
    )(cond8, ada_w, ada_b.reshape(DEPTH, 1, -1))


def _mod_spec(layer, chunk, row_of_tile):
    def imap(i, *_):
        return ((layer * COND_ROWS + row_of_tile(i)) * N_MOD + chunk, 0, 0)
    return pl.BlockSpec((None, 1, D_MODEL), imap)


def _rope(a, tabs, quarter):
    c, sa, sb = tabs
    out = []
    for hb in range(a.shape[1] // LANES):
        blk = a[:, hb * LANES:(hb + 1) * LANES]
        up = pltpu.roll(blk, LANES - quarter, axis=1)
        dn = pltpu.roll(blk, quarter, axis=1)
        out.append(blk * c + up * sa + dn * sb)
    return jnp.concatenate(out, axis=1)


def _in_kernel(*refs, secs, n_f32_out):
    x_ref, sh_ref, sc_ref, w_ref = refs[:4]
    pos = 4
    tabs = {}
    for kind in ("ret", "diff"):
        if any(s[0] == kind for s in secs):
            tabs[kind] = tuple(r[...] for r in refs[pos:pos + 3])
            pos += 3
    o_ref = refs[pos]
    f32_refs = refs[pos + 1:]
    h = (x_ref[...] * (1.0 + sc_ref[...]) + sh_ref[...]).astype(BF16)
    k32 = 0
    for s, (kind, scale, want_f32) in enumerate(secs):
        acc = jnp.dot(h, w_ref[:, s * SEC:(s + 1) * SEC], preferred_element_type=F32)
        if scale != 1.0:
            acc = acc * scale
        if kind == "ret":
            acc = _rope(acc, tabs["ret"], RET_DK // 4)
        elif kind == "diff":
            acc = _rope(acc, tabs["diff"], DIFF_QK // 4)
        o_ref[:, s * SEC:(s + 1) * SEC] = acc.astype(BF16)
        if want_f32:
            f32_refs[k32][...] = acc
            k32 += 1
    assert k32 == n_f32_out


def _in_proj(x2d, mod, layer, w_bf16, secs, row_of_tile, seq_len, rope_tabs=None, tm=512):
    T = x2d.shape[0]
    N = w_bf16.shape[1]
    assert N == SEC * len(secs) and T % tm == 0 and seq_len % tm == 0
    tiles_per_seq = seq_len // tm
    in_specs = [
        pl.BlockSpec((tm, D_MODEL), lambda i: (i, 0)),
        _mod_spec(layer, 0, row_of_tile),
        _mod_spec(layer, 1, row_of_tile),
        pl.BlockSpec((D_MODEL, N), lambda i: (0, 0)),
    ]
    args = [x2d, mod, mod, w_bf16]
    for kind in ("ret", "diff"):
        if any(s[0] == kind for s in secs):
            for t in rope_tabs[kind]:
                in_specs.append(pl.BlockSpec((tm, LANES), lambda i: (i % tiles_per_seq, 0)))
                args.append(t)
    n_f32 = sum(1 for s in secs if s[2])
    out_shape = [jax.ShapeDtypeStruct((T, N), BF16)] + [jax.ShapeDtypeStruct((T, SEC), F32)] * n_f32
    out_specs = [pl.BlockSpec((tm, N), lambda i: (i, 0))] + [pl.BlockSpec((tm, SEC), lambda i: (i, 0))] * n_f32
    return pl.pallas_call(
        functools.partial(_in_kernel, secs=secs, n_f32_out=n_f32),
        grid=(T // tm,),
        in_specs=in_specs,
        out_specs=out_specs,
        out_shape=out_shape,
        compiler_params=_cparams(("arbitrary",)),
        name="in_proj",
    )(*args)


def _rope_tables(seq_len, d):
    half = d // 2
    quarter = half // 2
    t = jnp.arange(seq_len)
    row = (t // GRID_W).astype(F32)
    col = (t % GRID_W).astype(F32)
    inv = ROPE_BASE ** (-jnp.arange(quarter, dtype=F32) / quarter)
    ang_r = row[:, None] * inv[None, :]
    ang_c = col[:, None] * inv[None, :]
    zero = jnp.zeros_like(ang_r)
    cos = jnp.concatenate([jnp.cos(ang_r)] * 2 + [jnp.cos(ang_c)] * 2, axis=1)
    sa = jnp.concatenate([-jnp.sin(ang_r), zero, -jnp.sin(ang_c), zero], axis=1)
    sb = jnp.concatenate([zero, jnp.sin(ang_r), zero, jnp.sin(ang_c)], axis=1)
    reps = LANES // d
    return tuple(jnp.tile(a, (1, reps)) for a in (cos, sa, sb))


def _ret_kernel(*refs, n_chunks, has_state, write_state):
    lg_ref, q_ref, k_ref, v_ref, g_ref, gnw_ref = refs[:6]
    pos = 6
    if has_state:
        s0f_ref, s0b_ref = refs[pos:pos + 2]
        pos += 2
    o_ref = refs[pos]
    pos += 1
    if write_state:
        sf_ref, sb_ref = refs[pos:pos + 2]
        pos += 2
    acc_ref = refs[pos]

    C = RET_CHUNK
    hd = pl.program_id(1)
    lgf = lg_ref[0, hd]
    lgb = lg_ref[1, hd]
    ii = lax.broadcasted_iota(jnp.int32, (C, C), 0).astype(F32)
    jj = lax.broadcasted_iota(jnp.int32, (C, C), 1).astype(F32)
    rel = ii - jj
    d_f = jnp.where(rel >= 0, jnp.exp(jnp.maximum(rel, 0.0) * lgf), 0.0)
    d_b = jnp.where(rel <= 0, jnp.exp(jnp.maximum(-rel, 0.0) * lgb), 0.0)
    d_sum = d_f + d_b
    idx = lax.broadcasted_iota(jnp.int32, (C, 1), 0).astype(F32)
    xi_f = jnp.exp((idx + 1.0) * lgf)
    zeta_f = jnp.exp((C - 1.0 - idx) * lgf)
    xi_b = jnp.exp((C - idx) * lgb)
    zeta_b = jnp.exp(idx * lgb)
    one = jnp.ones((1, 1), F32)
    gc_f = jnp.exp(one * (C * lgf))
    gc_b = jnp.exp(one * (C * lgb))

    nt = (((1,), (1,)), ((), ()))
    tn = (((0,), (0,)), ((), ()))

    if has_state:
        s_f = s0f_ref[...]
        s_b = s0b_ref[...]
    else:
        s_f = jnp.zeros((RET_DK, RET_DV), F32)
        s_b = jnp.zeros((RET_DK, RET_DV), F32)

    for n in range(n_chunks):
        sl = slice(n * C, (n + 1) * C)
        qc, kc, vc = q_ref[sl, :], k_ref[sl, :], v_ref[sl, :]
        scores = lax.dot_general(qc, kc, nt, preferred_element_type=F32) * d_sum
        inner = jnp.dot(scores.astype(BF16), vc, preferred_element_type=F32)
        cross = jnp.dot((qc.astype(F32) * xi_f).astype(BF16), s_f.astype(BF16), preferred_element_type=F32)
        acc_ref[sl, :] = inner + cross
        kz = (kc.astype(F32) * zeta_f).astype(BF16)
        s_f = gc_f * s_f + lax.dot_general(kz, vc, tn, preferred_element_type=F32)

    gnw = gnw_ref[...]
    for n in reversed(range(n_chunks)):
        sl = slice(n * C, (n + 1) * C)
        qc, kc, vc = q_ref[sl, :], k_ref[sl, :], v_ref[sl, :]
        cross = jnp.dot((qc.astype(F32) * xi_b).astype(BF16), s_b.astype(BF16), preferred_element_type=F32)
        r = acc_ref[sl, :] + cross
        mu = jnp.mean(r, axis=-1, keepdims=True)
        rc = r - mu
        var = jnp.mean(rc * rc, axis=-1, keepdims=True)
        rn = rc * lax.rsqrt(var + LN_EPS) * gnw
        o_ref[sl, :] = (_silu(g_ref[sl, :].astype(F32)) * rn).astype(BF16)
        kz = (kc.astype(F32) * zeta_b).astype(BF16)
        s_b = gc_b * s_b + lax.dot_general(kz, vc, tn, preferred_element_type=F32)

    if write_state:
        sf_ref[...] = s_f
        sb_ref[...] = s_b


def _retention(proj, lg, gn_w, batch, seq_len, states=None, write_state=False):
    T = proj.shape[0]
    hsec = SEC // LANES
    blk = lambda sec: pl.BlockSpec((seq_len, LANES), lambda b, h: (b, sec * hsec + h))
    in_specs = [pl.BlockSpec(memory_space=pltpu.SMEM), blk(0), blk(1), blk(2), blk(3),
                pl.BlockSpec((1, LANES), lambda b, h: (0, h))]
    args = [lg, proj, proj, proj, proj, gn_w.reshape(1, -1)]
    if states is not None:
        st = pl.BlockSpec((None, None, RET_DK, RET_DV), lambda b, h: (b, h, 0, 0))
        in_specs += [st, st]
        args += list(states)
    out_shape = [jax.ShapeDtypeStruct((T, SEC), BF16)]
    out_specs = [pl.BlockSpec((seq_len, LANES), lambda b, h: (b, h))]
    if write_state:
        st_o = pl.BlockSpec((None, None, RET_DK, RET_DV), lambda b, h: (b, h, 0, 0))
        out_shape += [jax.ShapeDtypeStruct((batch, H_RET, RET_DK, RET_DV), F32)] * 2
        out_specs += [st_o, st_o]
    return pl.pallas_call(
        functools.partial(_ret_kernel, n_chunks=seq_len // RET_CHUNK, has_state=states is not None,
                          write_state=write_state),
        grid=(batch, H_RET),
        in_specs=in_specs,
        out_specs=out_specs,
        out_shape=out_shape,
        scratch_shapes=[pltpu.VMEM((seq_len, RET_DV), F32)],
        compiler_params=_cparams(("arbitrary", "arbitrary")),
        name="retention",
    )(*args)


def _att_kernel(*refs, has_ctx, out_scale):
    lam_ref, q_ref, k_ref, v_ref = refs[:4]
    pos = 4
    if has_ctx:
        ck_ref, cv_ref = refs[pos:pos + 2]
        pos += 2
    w_ref, o_ref = refs[pos:pos + 2]
    lam = lam_ref[0]
    q = q_ref[...]
    tq = q.shape[0]
    lane = lax.broadcasted_iota(jnp.int32, q.shape, 1)
    zero = jnp.zeros_like(q)
    qq = jnp.concatenate([jnp.where(lane < DIFF_QK, q, zero), jnp.where(lane >= DIFF_QK, q, zero)], axis=0)
    qq = qq * jnp.asarray(DIFF_QK ** -0.5, BF16)
    nt = (((1,), (1,)), ((), ()))
    s = lax.dot_general(qq, k_ref[...], nt, preferred_element_type=F32)
    m = jnp.max(s, axis=-1, keepdims=True)
    if has_ctx:
        s2 = lax.dot_general(qq, ck_ref[...].astype(BF16), nt, preferred_element_type=F32)
        m = jnp.maximum(m, jnp.max(s2, axis=-1, keepdims=True))
    p = jnp.exp(s - m)
    l = jnp.sum(p, axis=-1, keepdims=True)
    o = jnp.dot(p.astype(BF16), v_ref[...], preferred_element_type=F32)
    if has_ctx:
        p2 = jnp.exp(s2 - m)
        l = l + jnp.sum(p2, axis=-1, keepdims=True)
        o = o + jnp.dot(p2.astype(BF16), cv_ref[...].astype(BF16), preferred_element_type=F32)
    o = o / l
    att = o[:tq] - lam * o[tq:]
    att = att * lax.rsqrt(jnp.mean(att * att, axis=-1, keepdims=True) + LN_EPS)
    o_ref[...] = (att * w_ref[...] * out_scale).astype(BF16)


def _diff_attention(proj, lam, subln_w, lam_init, batch, seq_len, ctx=None, tq=256):
    T = proj.shape[0]
    hsec = SEC // LANES
    nq = seq_len // tq
    in_specs = [
        pl.BlockSpec(memory_space=pltpu.SMEM),
        pl.BlockSpec((tq, LANES), lambda b, h, i: (b * nq + i, 4 * hsec + h)),
        pl.BlockSpec((seq_len, LANES), lambda b, h, i: (b, 5 * hsec + h)),
        pl.BlockSpec((seq_len, LANES), lambda b, h, i: (b, 6 * hsec + h)),
    ]
    args = [lam, proj, proj, proj]
    if ctx is not None:
        ck, cv = ctx
        past = ck.shape[1]
        cspec = pl.BlockSpec((None, past, LANES), lambda b, h, i: (b, 0, h))
        in_specs += [cspec, cspec]
        args += [ck, cv]
    in_specs.append(pl.BlockSpec((1, LANES), lambda b, h, i: (0, 0)))
    args.append(subln_w.reshape(1, -1))
    return pl.pallas_call(
        functools.partial(_att_kernel, has_ctx=ctx is not None, out_scale=1.0 - lam_init),
        grid=(batch, H_DIFF, nq),
        in_specs=in_specs,
        out_specs=pl.BlockSpec((tq, LANES), lambda b, h, i: (b * nq + i, h)),
        out_shape=jax.ShapeDtypeStruct((T, SEC), BF16),
        compiler_params=_cparams(("arbitrary", "arbitrary", "arbitrary")),
        name="diff_attention",
    )(*args)


def _route(lt, rb_ref):
    score = [jax.nn.sigmoid(lt[e:e + 1, :]) for e in range(N_EXPERTS)]
    sel = [score[e] + rb_ref[e] for e in range(N_EXPERTS)]
    gscore = []
    for g in range(N_GROUPS):
        mem = sel[g * GROUP_SIZE:(g + 1) * GROUP_SIZE]
        best = None
        for a in range(GROUP_SIZE):
            for b in range(a + 1, GROUP_SIZE):
                pair = mem[a] + mem[b]
                best = pair if best is None else jnp.maximum(best, pair)
        gscore.append(best)
    gbest = gscore[0]
    gidx = jnp.zeros_like(gbest)
    for g in range(1, N_GROUPS):
        upd = gscore[g] > gbest
        gidx = jnp.where(upd, float(g), gidx)
        gbest = jnp.where(upd, gscore[g], gbest)

    def pick(vals, j):
        out = vals[j]
        for g in range(1, N_GROUPS):
            out = jnp.where(gidx == float(g), vals[g * GROUP_SIZE + j], out)
        return out

    msel = [pick(sel, j) for j in range(GROUP_SIZE)]
    msco = [pick(score, j) for j in range(GROUP_SIZE)]
    one = jnp.ones_like(gbest)
    zero = jnp.zeros_like(gbest)
    chosen = []
    for j in range(GROUP_SIZE):
        rank = zero
        for k in range(GROUP_SIZE):
            if k < j:
                rank = rank + jnp.where(msel[k] >= msel[j], one, zero)
            elif k > j:
                rank = rank + jnp.where(msel[k] > msel[j], one, zero)
        chosen.append(jnp.where(rank < 2.0, one, zero))
    c0, c1, c2, c3 = chosen
    s_low = jnp.where(c0 > 0, msco[0], jnp.where(c1 > 0, msco[1], msco[2]))
    s_high = jnp.where(c3 > 0, msco[3], jnp.where(c2 > 0, msco[2], msco[1]))
    tot = s_low + s_high
    order = jnp.where(c0 * c1 > 0, 0.0, jnp.where(c1 * c2 > 0, 1.0, jnp.where(c2 * c3 > 0, 2.0,
            jnp.where(c0 * c3 > 0, 3.0, jnp.where(c0 * c2 > 0, 4.0, 5.0)))))
    return s_low / tot, s_high / tot, gidx * float(N_PAIRS) + order


def _post_kernel(*refs, n_mix):
    mix_refs = refs[:n_mix]
    (w_ref, x_ref, g_ref, sh_ref, sc_ref, lnw_ref, lnb_ref, rw_ref, rb_ref,
     x1_ref, h2_ref, rec_ref) = refs[n_mix:]
    out = None
    off = 0
    for m_ref in mix_refs:
        width = m_ref.shape[1]
        part = jnp.dot(m_ref[...], w_ref[off:off + width, :], preferred_element_type=F32)
        out = part if out is None else out + part
        off += width
    x1 = _layer_norm(ALPHA * x_ref[...] + g_ref[...] * out, lnw_ref[...], lnb_ref[...])
    x1_ref[...] = x1
    h2 = x1 * (1.0 + sc_ref[...]) + sh_ref[...]
    h2_ref[...] = h2.astype(BF16)
    logits = jnp.dot(h2, rw_ref[...], precision=HIGHEST, preferred_element_type=F32)
    w_low, w_high, cls = _route(logits.T, rb_ref)
    tm = h2.shape[0]
    row = lax.broadcasted_iota(jnp.int32, (8, tm), 0)
    rec8 = jnp.where(row == 0, w_low, jnp.where(row == 1, w_high, jnp.where(row == 2, cls, 0.0)))
    rec_t = jnp.concatenate([rec8, jnp.zeros((LANES - 8, tm), F32)], axis=0)
    rec_ref[...] = rec_t.T


def _post_mix(mixes, w_out_bf16, x2d, mod, layer, ln_w, ln_b, router_w_pad, router_bias, row_of_tile, tm=512):
    T = x2d.shape[0]
    row = pl.BlockSpec((tm, D_MODEL), lambda i: (i, 0))
    vec = pl.BlockSpec((1, D_MODEL), lambda i: (0, 0))
    in_specs = [pl.BlockSpec((tm, m.shape[1]), lambda i: (i, 0)) for m in mixes]
    in_specs += [
        pl.BlockSpec((D_MODEL, D_MODEL), lambda i: (0, 0)),
        row,
        _mod_spec(layer, 2, row_of_tile), _mod_spec(layer, 3, row_of_tile), _mod_spec(layer, 4, row_of_tile),
        vec, vec,
        pl.BlockSpec((D_MODEL, LANES), lambda i: (0, 0)),
        pl.BlockSpec(memory_space=pltpu.SMEM),
    ]
    return pl.pallas_call(
        functools.partial(_post_kernel, n_mix=len(mixes)),
        grid=(T // tm,),
        in_specs=in_specs,
        out_specs=[row, row, pl.BlockSpec((tm, LANES), lambda i: (i, 0))],
        out_shape=[jax.ShapeDtypeStruct((T, D_MODEL), F32), jax.ShapeDtypeStruct((T, D_MODEL), BF16),
                   jax.ShapeDtypeStruct((T, LANES), F32)],
        compiler_params=_cparams(("arbitrary",)),
        name="post_mix",
    )(*mixes, w_out_bf16, x2d, mod, mod, mod, ln_w.reshape(1, -1), ln_b.reshape(1, -1), router_w_pad,
      router_bias)


def _moe_kernel(ea_ref, eb_ref, fl_ref, h_ref, rec_ref, wga, wua, wda, wgb, wub, wdb, o_ref,
                sga, sua, sda, sgb, sub, sdb):
    t = pl.program_id(0)
    flags = fl_ref[t]
    valid = (flags & 1) != 0
    swap = (flags & 2) != 0

    @pl.when((flags & 4) != 0)
    def _():
        sga[...] = wga[...].astype(BF16)
        sua[...] = wua[...].astype(BF16)
        sda[...] = wda[...].astype(BF16)

    @pl.when((flags & 8) != 0)
    def _():
        sgb[...] = wgb[...].astype(BF16)
        sub[...] = wub[...].astype(BF16)
        sdb[...] = wdb[...].astype(BF16)

    @pl.when(valid)
    def _():
        h = h_ref[...]
        w_low = rec_ref[:, 0:1]
        w_high = rec_ref[:, 1:2]
        w_a = jnp.where(swap, w_high, w_low)
        w_b = jnp.where(swap, w_low, w_high)
        act_a = _silu(jnp.dot(h, sga[...], preferred_element_type=F32)) \
            * jnp.dot(h, sua[...], preferred_element_type=F32) * w_a
        act_b = _silu(jnp.dot(h, sgb[...], preferred_element_type=F32)) \
            * jnp.dot(h, sub[...], preferred_element_type=F32) * w_b
        o_ref[...] = (jnp.dot(act_a.astype(BF16), sda[...], preferred_element_type=F32)
                      + jnp.dot(act_b.astype(BF16), sdb[...], preferred_element_type=F32))

    @pl.when(jnp.logical_not(valid))
    def _():
        o_ref[...] = jnp.zeros_like(o_ref)


def _moe_plan(cls, n_tiles):
    T = cls.shape[0]
    counts = jnp.sum(cls[:, None] == jnp.arange(N_CLASSES, dtype=jnp.int32)[None, :], axis=0, dtype=jnp.int32)
    tiles = (counts + MOE_TM - 1) // MOE_TM
    tile_end = jnp.cumsum(tiles)
    tile_start = tile_end - tiles
    tok_start = jnp.cumsum(counts) - counts
    order = jnp.argsort(cls, stable=True).astype(jnp.int32)
    total = tile_end[-1]
    t_ids = jnp.arange(n_tiles, dtype=jnp.int32)
    valid = t_ids < total
    t_cls = jnp.minimum(jnp.searchsorted(tile_end, t_ids, side="right").astype(jnp.int32), N_CLASSES - 1)
    last_cls = t_cls[jnp.maximum(total - 1, 0)]
    t_cls = jnp.where(valid, t_cls, last_cls)
    slots = jnp.asarray(PAIR_SLOTS, jnp.int32)
    grp = t_cls // N_PAIRS
    pair = t_cls % N_PAIRS
    ea = grp * GROUP_SIZE + slots[pair, 0]
    eb = grp * GROUP_SIZE + slots[pair, 1]
    swap = slots[pair, 0] > slots[pair, 1]
    first = t_ids == 0
    load_a = first | (ea != jnp.roll(ea, 1))
    load_b = first | (eb != jnp.roll(eb, 1))
    flags = (valid.astype(jnp.int32) + 2 * swap.astype(jnp.int32) + 4 * load_a.astype(jnp.int32)
             + 8 * load_b.astype(jnp.int32))
    rows = jnp.arange(n_tiles * MOE_TM, dtype=jnp.int32)
    r_cls = t_cls[rows // MOE_TM]
    rank = rows - tile_start[r_cls] * MOE_TM
    r_valid = valid[rows // MOE_TM] & (rank < counts[r_cls])
    src = jnp.where(r_valid, order[jnp.clip(tok_start[r_cls] + rank, 0, T - 1)], 0)
    sorted_pos = jnp.zeros((T,), jnp.int32).at[order].set(jnp.arange(T, dtype=jnp.int32))
    dst = tile_start[cls] * MOE_TM + sorted_pos - tok_start[cls]
    return ea, eb, flags, src, dst


def _moe(h2, rec, layer, w_gate, w_up, w_down):
    T = h2.shape[0]
    n_tiles = T // MOE_TM + N_CLASSES
    cls = rec[:, 2].astype(jnp.int32)
    ea, eb, flags, src, dst = _moe_plan(cls, n_tiles)
    hs = jnp.take(h2, src, axis=0)
    recs = jnp.take(rec, src, axis=0)
    wspec = lambda shape, which: pl.BlockSpec(
        (None, None) + shape, (lambda t, ea, eb, fl: (layer, ea[t], 0, 0)) if which == 0
        else (lambda t, ea, eb, fl: (layer, eb[t], 0, 0)))
    up = (D_MODEL, D_FF_EXPERT)
    dn = (D_FF_EXPERT, D_MODEL)
    grid_spec = pltpu.PrefetchScalarGridSpec(
        num_scalar_prefetch=3,
        grid=(n_tiles,),
        in_specs=[
            pl.BlockSpec((MOE_TM, D_MODEL), lambda t, ea, eb, fl: (t, 0)),
            pl.BlockSpec((MOE_TM, LANES), lambda t, ea, eb, fl: (t, 0)),
            wspec(up, 0), wspec(up, 0), wspec(dn, 0), wspec(up, 1), wspec(up, 1), wspec(dn, 1),
        ],
        out_specs=pl.BlockSpec((MOE_TM, D_MODEL), lambda t, ea, eb, fl: (t, 0)),
        scratch_shapes=[pltpu.VMEM(up, BF16), pltpu.VMEM(up, BF16), pltpu.VMEM(dn, BF16),
                        pltpu.VMEM(up, BF16), pltpu.VMEM(up, BF16), pltpu.VMEM(dn, BF16)],
    )
    ys = pl.pallas_call(
        _moe_kernel,
        grid_spec=grid_spec,
        out_shape=jax.ShapeDtypeStruct((n_tiles * MOE_TM, D_MODEL), F32),
        compiler_params=_cparams(("arbitrary",)),
        name="moe",
    )(ea, eb, flags, hs, recs, w_gate, w_up, w_down, w_gate, w_up, w_down)
    return jnp.take(ys, dst, axis=0)


def _fin_kernel(x_ref, y_ref, g_ref, lnw_ref, lnb_ref, o_ref):
    o_ref[...] = _layer_norm(ALPHA * x_ref[...] + g_ref[...] * y_ref[...], lnw_ref[...], lnb_ref[...])


def _fin(x1, y, mod, layer, ln_w, ln_b, row_of_tile, tm=512):
    T = x1.shape[0]
    row = pl.BlockSpec((tm, D_MODEL), lambda i: (i, 0))
    vec = pl.BlockSpec((1, D_MODEL), lambda i: (0, 0))
    return pl.pallas_call(
        _fin_kernel,
        grid=(T // tm,),
        in_specs=[row, row, _mod_spec(layer, 5, row_of_tile), vec, vec],
        out_specs=row,
        out_shape=jax.ShapeDtypeStruct((T, D_MODEL), F32),
        compiler_params=_cparams(("arbitrary",)),
        name="post_moe",
    )(x1, y, mod, ln_w.reshape(1, -1), ln_b.reshape(1, -1))


def _filt_kernel(z_ref, w1_ref, b1_ref, w2_ref, b2_ref, fr_ref, w3_ref, dl_ref, o_ref):
    i = pl.program_id(0)
    z = z_ref[...]
    fr = fr_ref[...]
    a = jnp.sin(fr * (jnp.dot(z, w1_ref[...], precision=HIGHEST, preferred_element_type=F32) + b1_ref[...]))
    a = jnp.sin(fr * (jnp.dot(a, w2_ref[...], precision=HIGHEST, preferred_element_type=F32) + b2_ref[...]))
    filt = jnp.dot(a, w3_ref[...], precision=HIGHEST, preferred_element_type=F32)
    window = jnp.exp(-z[:, 0:1] * dl_ref[...]) + HY_SHIFT
    rows = i * z.shape[0] + lax.broadcasted_iota(jnp.int32, filt.shape, 0)
    o_ref[...] = jnp.where(rows == 0, 0.0, filt * window)


def _hyena_taps(seq_len, w1, b1, w2, b2, w3, freq):
    L = seq_len
    t = jnp.linspace(0.0, 1.0, L, dtype=F32)[:, None]
    bands = jnp.linspace(1e-4, HY_BANDS - 1, HY_BANDS, dtype=F32)
    ang = 2.0 * math.pi * bands[None, :] * jnp.arange(L, dtype=F32)[:, None] / L
    z = jnp.concatenate([t, jnp.cos(ang), -jnp.sin(ang)], axis=-1)
    offs = np.minimum(np.abs(np.arange(2 * L) - L), L - 1)
    emb = z.shape[1]
    z2 = jnp.zeros((2 * L, LANES), F32).at[:, :emb].set(z[offs])
    pad_c = lambda a: jnp.zeros((a.shape[0], LANES), F32).at[:, :a.shape[1]].set(a)
    pad_r = lambda a: jnp.zeros((LANES, a.shape[1]), F32).at[:a.shape[0], :].set(a)
    w1p = pad_r(pad_c(w1))
    w2p = pad_r(pad_c(w2))
    w3p = pad_r(w3)
    b1p, b2p, frp = pad_c(b1[None, :]), pad_c(b2[None, :]), pad_c(freq[None, :])
    deltas = jnp.abs(jnp.linspace(math.log(HY_DECAY_TARGET) / HY_SLOW, math.log(HY_DECAY_TARGET) / HY_FAST,
                                  D_MODEL, dtype=F32))[None, :]
    rb = min(512, L)
    cbf = 512
    ncb = D_MODEL // cbf
    nrb_back = L // rb
    sq = pl.BlockSpec((LANES, LANES), lambda i, j: (0, 0))
    vec = pl.BlockSpec((1, LANES), lambda i, j: (0, 0))
    return pl.pallas_call(
        _filt_kernel,
        grid=(2 * L // rb, ncb),
        in_specs=[
            pl.BlockSpec((rb, LANES), lambda i, j: (i, 0)),
            sq, vec, sq, vec, vec,
            pl.BlockSpec((LANES, cbf), lambda i, j: (0, jnp.where(i < nrb_back, ncb + j, j))),
            pl.BlockSpec((1, cbf), lambda i, j: (0, j)),
        ],
        out_specs=pl.BlockSpec((rb, cbf), lambda i, j: (i, j)),
        out_shape=jax.ShapeDtypeStruct((2 * L, D_MODEL), F32),
        compiler_params=_cparams(("arbitrary", "arbitrary")),
        name="hyena_taps",
    )(z2, w1p, b1p, w2p, b2p, frp, w3p, deltas)


def _dft_mats(cb):
    n = 2 * cb
    m = np.arange(cb)
    f = np.arange(cb)
    ang = 2.0 * np.pi * ((f[:, None] * m[None, :]) % n) / n
    fwd = np.concatenate([np.cos(ang), -np.sin(ang)], axis=0)
    fwd[cb, :] = np.where(m % 2 == 0, 1.0, -1.0)
    coef = np.where(f == 0, 1.0, 2.0)[None, :] / n
    inv = np.concatenate([coef * np.cos(ang.T), -coef * np.sin(ang.T)], axis=1)
    inv[:, cb] = np.where(m % 2 == 0, 1.0, -1.0) / n
    return fwd.astype(np.float32), inv.astype(np.float32)


def _hconv_kernel(x0_ref, x1_ref, v_ref, cw0_ref, cw1_ref, cw2_ref, cb0_ref, cb1_ref, cb2_ref,
                  taps_ref, fb_ref, fwd_ref, inv_ref, o_ref,
                  hs_ref, stage_ref, w32_ref, w_ref, x0c_ref, u_ref, y_ref, *, seq_len, blk, n_seq):
    L = seq_len
    nb = L // blk
    cw = x0_ref.shape[1]
    RC = 256
    bi = pl.program_id(1)
    row0 = lax.broadcasted_iota(jnp.int32, (RC, cw), 0) == 0

    @pl.when(bi == 0)
    def _():
        sign = jnp.where(lax.broadcasted_iota(jnp.int32, (2 * blk, 1), 0) % 2 == 0, 1.0, -1.0)
        for k in range(2 * nb - 1):
            base = L + (k - nb + 1) * blk
            head = taps_ref[base:base + blk, :].astype(BF16)
            tail = taps_ref[base - blk:base, :]
            tail = jnp.where(lax.broadcasted_iota(jnp.int32, tail.shape, 0) == 0, 0.0, tail).astype(BF16)
            hs_ref[k] = (jnp.dot(fwd_ref[...], head, preferred_element_type=F32)
                         + sign * jnp.dot(fwd_ref[...], tail, preferred_element_type=F32))

    zeros8 = jnp.zeros((8, cw), F32)
    for s in range(n_seq):
        r0 = s * L

        def short_conv(src_ref, cw_ref, cb_ref, store):
            stage_ref[0:8, :] = zeros8
            stage_ref[8 + L:16 + L, :] = zeros8
            for c in range(L // RC):
                stage_ref[8 + c * RC:8 + (c + 1) * RC, :] = src_ref[r0 + c * RC:r0 + (c + 1) * RC, :].astype(F32)
            w = cw_ref[...]
            for c in range(L // RC):
                lo = stage_ref[7 + c * RC:7 + (c + 1) * RC, :]
                mid = stage_ref[8 + c * RC:8 + (c + 1) * RC, :]
                hi = stage_ref[9 + c * RC:9 + (c + 1) * RC, :]
                store(c, lo * w[0:1, :] + mid * w[1:2, :] + hi * w[2:3, :] + cb_ref[...])

        def st_x1(c, val):
            w32_ref[c * RC:(c + 1) * RC, :] = val

        def st_v(c, val):
            w_ref[c * RC:(c + 1) * RC, :] = (w32_ref[c * RC:(c + 1) * RC, :] * val).astype(BF16)

        def st_x0(c, val):
            x0c_ref[c * RC:(c + 1) * RC, :] = val.astype(BF16)

        short_conv(x1_ref, cw1_ref, cb1_ref, st_x1)
        short_conv(v_ref, cw2_ref, cb2_ref, st_v)
        short_conv(x0_ref, cw0_ref, cb0_ref, st_x0)

        for j in range(nb):
            u_ref[j] = jnp.dot(fwd_ref[...], w_ref[j * blk:(j + 1) * blk, :], preferred_element_type=F32)

        for i in range(nb):
            for c in range(blk // RC):
                re = None
                im = None
                for j in range(nb):
                    k = i - j + nb - 1
                    a = u_ref[j, c * RC:(c + 1) * RC, :]
                    b = u_ref[j, blk + c * RC:blk + (c + 1) * RC, :]
                    hr = hs_ref[k, c * RC:(c + 1) * RC, :]
                    hi = hs_ref[k, blk + c * RC:blk + (c + 1) * RC, :]
                    bb = b * hi
                    if c == 0:
                        t_re = a * hr - jnp.where(row0, 0.0, bb)
                        t_im = jnp.where(row0, bb, a * hi + b * hr)
                    else:
                        t_re = a * hr - bb
                        t_im = a * hi + b * hr
                    re = t_re if re is None else re + t_re
                    im = t_im if im is None else im + t_im
                y_ref[c * RC:(c + 1) * RC, :] = re.astype(BF16)
                y_ref[blk + c * RC:blk + (c + 1) * RC, :] = im.astype(BF16)
            conv = jnp.dot(inv_ref[...], y_ref[...], preferred_element_type=F32)
            sl = slice(i * blk, (i + 1) * blk)
            z = x0c_ref[sl, :].astype(F32) * (conv + w_ref[sl, :].astype(F32) * fb_ref[...])
            o_ref[r0 + i * blk:r0 + (i + 1) * blk, :] = z.astype(BF16)


def _hyena_conv(u, conv_w, conv_b, taps, filt_bias, batch, seq_len, blk, n_seq, cw=256):
    T = u.shape[0]
    L = seq_len
    nb = L // blk
    ncw = D_MODEL // cw
    fwd_np, inv_np = _dft_mats(blk)
    fwd = jnp.asarray(fwd_np).astype(BF16)
    inv = jnp.asarray(inv_np).astype(BF16)
    rows = n_seq * L
    sec = lambda s: pl.BlockSpec((rows, cw), lambda c, b: (b, s * ncw + c))
    cws = lambda s: pl.BlockSpec((3, cw), lambda c, b: (0, s * ncw + c))
    cbs = lambda s: pl.BlockSpec((1, cw), lambda c, b: (0, s * ncw + c))
    return pl.pallas_call(
        functools.partial(_hconv_kernel, seq_len=L, blk=blk, n_seq=n_seq),
        grid=(ncw, batch // n_seq),
        in_specs=[sec(0), sec(1), sec(2), cws(0), cws(1), cws(2), cbs(0), cbs(1), cbs(2),
                  pl.BlockSpec((2 * L, cw), lambda c, b: (0, c)),
                  pl.BlockSpec((1, cw), lambda c, b: (0, c)),
                  pl.BlockSpec((2 * blk, blk), lambda c, b: (0, 0)),
                  pl.BlockSpec((blk, 2 * blk), lambda c, b: (0, 0))],
        out_specs=pl.BlockSpec((rows, cw), lambda c, b: (b, c)),
        out_shape=jax.ShapeDtypeStruct((T, D_MODEL), BF16),
        scratch_shapes=[
            pltpu.VMEM((2 * nb - 1, 2 * blk, cw), F32),
            pltpu.VMEM((L + 16, cw), F32),
            pltpu.VMEM((L, cw), F32),
            pltpu.VMEM((L, cw), BF16),
            pltpu.VMEM((L, cw), BF16),
            pltpu.VMEM((nb, 2 * blk, cw), F32),
            pltpu.VMEM((2 * blk, cw), BF16),
        ],
        compiler_params=_cparams(("arbitrary", "arbitrary")),
        name="hyena_conv",
    )(u, u, u, conv_w, conv_w, conv_w, conv_b.reshape(1, -1), conv_b.reshape(1, -1), conv_b.reshape(1, -1),
      taps, filt_bias.reshape(1, -1), fwd, inv)


def kernel(x_prompt, x_sample, cache_diff_k, cache_diff_v, state_ret_fwd, state_ret_bwd, c, c_ctx, ada_w, ada_b, ln_w, ln_b, ev_w_in, ev_w_out, ret_decay_fwd, ret_decay_bwd, ret_gn_w, diff_lambda, diff_subln_w, hy_w_in, hy_conv_w, hy_conv_b, hy_ffn_w1, hy_ffn_b1, hy_ffn_w2, hy_ffn_b2, hy_ffn_w3, hy_freq, hy_filter_bias, hy_w_out, router_w, router_bias, moe_w_gate, moe_w_up, moe_w_down):
    B, S, D = x_prompt.shape
    DB, DS, _ = x_sample.shape
    PAST = cache_diff_k.shape[2]
    TP, TS = B * S, DB * DS
    assert D == D_MODEL and 1 + DB <= COND_ROWS

    cond8 = jnp.zeros((COND_ROWS, D), F32).at[0].set(c_ctx).at[1:1 + DB].set(c)
    mod = _ada_mod(cond8, ada_w, ada_b).reshape(DEPTH * COND_ROWS * N_MOD, 1, D)

    tm = 512
    groups = [
        dict(x=x_prompt.reshape(TP, D), batch=B, seq=S, row_of_tile=lambda i: 0),
        dict(x=x_sample.reshape(TS, D), batch=DB, seq=DS, row_of_tile=lambda i: 1 + (i * tm) // DS),
    ]
    router_w_pad = jnp.zeros((D, LANES), F32).at[:, :N_EXPERTS].set(router_w)
    outs = {}

    for l in range(DEPTH):
        mixes = []
        if l % 2 == 0:
            e = l // 2
            w_in = ev_w_in[e].astype(BF16)
            w_out = ev_w_out[e].astype(BF16)
            lg = jnp.stack([jnp.log1p(-jnp.exp2(ret_decay_fwd[e].astype(F32))),
                            jnp.log1p(-jnp.exp2(ret_decay_bwd[e].astype(F32)))])
            lam_init = 0.8 - 0.6 * math.exp(-0.3 * l)
            lq1, lk1, lq2, lk2 = diff_lambda[e].astype(F32)
            lam = (jnp.exp(jnp.sum(lq1 * lk1)) - jnp.exp(jnp.sum(lq2 * lk2)) + lam_init).reshape(1)
            kscale = RET_DK ** -0.5
            for gi, g in enumerate(groups):
                is_ctx_pass = gi == 0
                if is_ctx_pass:
                    secs = (("none", 1.0, False), ("none", kscale, False), ("none", 1.0, False),
                            ("none", 1.0, False), ("none", 1.0, False), ("none", 1.0, True),
                            ("none", 1.0, True))
                    proj, kd, vd = _in_proj(g["x"], mod, l, w_in, secs, g["row_of_tile"], g["seq"], tm=min(tm, g["seq"]))
                    outs.setdefault("kd", []).append(kd.reshape(B, S, H_DIFF, 2 * DIFF_QK))
                    outs.setdefault("vd", []).append(vd.reshape(B, S, H_DIFF, DIFF_V))
                    ret, sf, sb = _retention(proj, lg, ret_gn_w[e], g["batch"], g["seq"], write_state=True)
                    outs.setdefault("sf", []).append(sf)
                    outs.setdefault("sb", []).append(sb)
                    att = _diff_attention(proj, lam, diff_subln_w[e], lam_init, g["batch"], g["seq"],
                                          tq=min(256, g["seq"]))
                else:
                    secs = (("ret", 1.0, False), ("ret", kscale, False), ("none", 1.0, False),
                            ("none", 1.0, False), ("diff", 1.0, False), ("diff", 1.0, False),
                            ("none", 1.0, False))
                    tabs = {"ret": _rope_tables(g["seq"], RET_DK), "diff": _rope_tables(g["seq"], DIFF_QK)}
                    (proj,) = _in_proj(g["x"], mod, l, w_in, secs, g["row_of_tile"], g["seq"], rope_tabs=tabs, tm=tm)
                    ret, = _retention(proj, lg, ret_gn_w[e], g["batch"], g["seq"],
                                      states=(state_ret_fwd[:, e], state_ret_bwd[:, e]))
                    ctx = (cache_diff_k[:, e].reshape(DB, PAST, H_DIFF * 2 * DIFF_QK),
                           cache_diff_v[:, e].reshape(DB, PAST, H_DIFF * DIFF_V))
                    att = _diff_attention(proj, lam, diff_subln_w[e], lam_init, g["batch"], g["seq"], ctx=ctx)
                mixes.append((ret, att))
        else:
            o = l // 2
            w_in = hy_w_in[o].astype(BF16)
            w_out = hy_w_out[o].astype(BF16)
            secs = (("none", 1.0, False),) * (3 * D // SEC)
            for gi, g in enumerate(groups):
                (u,) = _in_proj(g["x"], mod, l, w_in, secs, g["row_of_tile"], g["seq"], tm=min(tm, g["seq"]))
                taps = _hyena_taps(g["seq"], hy_ffn_w1[o], hy_ffn_b1[o], hy_ffn_w2[o], hy_ffn_b2[o],
                                   hy_ffn_w3[o], hy_freq[o])
                blk = min(g["seq"], 1024)
                n_seq = max(1, 1024 // g["seq"])
                z = _hyena_conv(u, hy_conv_w[o], hy_conv_b[o], taps, hy_filter_bias[o], g["batch"], g["seq"],
                                blk, n_seq)
                mixes.append((z,))

        x1s, h2s, recs = [], [], []
        for g, mix in zip(groups, mixes):
            x1, h2, rec = _post_mix(mix, w_out, g["x"], mod, l, ln_w[l, 0], ln_b[l, 0], router_w_pad,
                                    router_bias.astype(F32), g["row_of_tile"], tm=min(tm, g["seq"]))
            x1s.append(x1)
            h2s.append(h2)
            recs.append(rec)
        y = _moe(jnp.concatenate(h2s, axis=0), jnp.concatenate(recs, axis=0), l, moe_w_gate, moe_w_up, moe_w_down)
        off = 0
        for g, x1 in zip(groups, x1s):
            n = x1.shape[0]
            g["x"] = _fin(x1, y[off:off + n], mod, l, ln_w[l, 1], ln_b[l, 1], g["row_of_tile"], tm=min(tm, g["seq"]))
            off += n

    y_prompt = groups[0]["x"].reshape(B, S, D)
    y_sample = groups[1]["x"].reshape(DB, DS, D)
    new_diff_k = jnp.stack(outs["kd"], axis=1)
    new_diff_v = jnp.stack(outs["vd"], axis=1)
    new_sf = jnp.stack(outs["sf"], axis=1)
    new_sb = jnp.stack(outs["sb"], axis=1)
    return (y_prompt, y_sample, new_diff_k, new_diff_v, new_sf, new_sb)
```

```python
import functools
import math

import numpy as np
import jax
import jax.numpy as jnp
from jax import lax
from jax.experimental import pallas as pl
from jax.experimental.pallas import tpu as pltpu

F32 = jnp.float32
BF16 = jnp.bfloat16
HIGHEST = lax.Precision.HIGHEST

D_MODEL = 1024
DEPTH = 2
GRID_W = 64
H_RET = 4
RET_DK = 128
RET_DV = 128
RET_CHUNK = 128
H_DIFF = 4
DIFF_QK = 64
DIFF_V = 128
ROPE_BASE = 10000.0
HY_BANDS = 16
HY_FH = 64
HY_DECAY_TARGET = 1e-2
HY_FAST = 0.3
HY_SLOW = 1.5
HY_SHIFT = 0.05
N_EXPERTS = 16
N_GROUPS = 4
GROUP_SIZE = N_EXPERTS // N_GROUPS
D_FF_EXPERT = 512
ALPHA = (2 * DEPTH) ** 0.25
LN_EPS = 1e-5

LANES = 128
SEC = 512
COND_ROWS = 8
N_MOD = 6
VMEM_LIMIT = 50 * 1024 * 1024

PAIR_SLOTS = ((0, 1), (2, 1), (2, 3), (0, 3), (0, 2), (1, 3))
N_PAIRS = len(PAIR_SLOTS)
N_CLASSES = N_GROUPS * N_PAIRS
MOE_TM = 256
MOE_SHIFT = MOE_TM.bit_length() - 1


def _cparams(sem):
    return pltpu.CompilerParams(dimension_semantics=sem, vmem_limit_bytes=VMEM_LIMIT)


def _silu(x):
    return x * jax.nn.sigmoid(x)


def _layer_norm(x, w, b):
    mu = jnp.mean(x, axis=-1, keepdims=True)
    xc = x - mu
    var = jnp.mean(xc * xc, axis=-1, keepdims=True)
    return xc * lax.rsqrt(var + LN_EPS) * w + b


def _ada_kernel(c_ref, w_ref, b_ref, o_ref):
    c = c_ref[...]
    o_ref[...] = jnp.dot(_silu(c), w_ref[...], precision=HIGHEST, preferred_element_type=F32) + b_ref[...]


def _ada_mod(cond8, ada_w, ada_b):
    tn = 1024
    nj = ada_w.shape[2] // tn
    return pl.pallas_call(
        _ada_kernel,
        grid=(DEPTH, nj),
        in_specs=[
            pl.BlockSpec((COND_ROWS, D_MODEL), lambda l, j: (0, 0)),
            pl.BlockSpec((None, D_MODEL, tn), lambda l, j: (l, 0, j)),
            pl.BlockSpec((None, 1, tn), lambda l, j: (l, 0, j)),
        ],
        out_specs=pl.BlockSpec((None, COND_ROWS, tn), lambda l, j: (l, 0, j)),
        out_shape=jax.ShapeDtypeStruct((DEPTH, COND_ROWS, ada_w.shape[2]), F32),
        compiler_params=_cparams(("arbitrary", "arbitrary")),
        name="ada_mod",
    )(cond8, ada_w, ada_b.reshape(DEPTH, 1, -1))


def _mod_spec(layer, chunk, row_of_tile):
    def imap(i, *_):
        return ((layer * COND_ROWS + row_of_tile(i)) * N_MOD + chunk, 0, 0)
    return pl.BlockSpec((None, 1, D_MODEL), imap)


def _rope(a, tabs, quarter):
    c, sa, sb = tabs
    out = []
    for hb in range(a.shape[1] // LANES):
        blk = a[:, hb * LANES:(hb + 1) * LANES]
        up = pltpu.roll(blk, LANES - quarter, axis=1)
        dn = pltpu.roll(blk, quarter, axis=1)
        out.append(blk * c + up * sa + dn * sb)
    return jnp.concatenate(out, axis=1)


def _in_kernel(*refs, secs, n_f32_out):
    x_ref, sh_ref, sc_ref, w_ref = refs[:4]
    pos = 4
    tabs = {}
    for kind in ("ret", "diff"):
        if any(s[0] == kind for s in secs):
            tabs[kind] = tuple(r[...] for r in refs[pos:pos + 3])
            pos += 3
    o_ref = refs[pos]
    f32_refs = refs[pos + 1:]
    h = (x_ref[...] * (1.0 + sc_ref[...]) + sh_ref[...]).astype(BF16)
    k32 = 0
    for s, (kind, scale, want_f32) in enumerate(secs):
        acc = jnp.dot(h, w_ref[:, s * SEC:(s + 1) * SEC], preferred_element_type=F32)
        if scale != 1.0:
            acc = acc * scale
        if kind == "ret":
            acc = _rope(acc, tabs["ret"], RET_DK // 4)
        elif kind == "diff":
            acc = _rope(acc, tabs["diff"], DIFF_QK // 4)
        o_ref[:, s * SEC:(s + 1) * SEC] = acc.astype(BF16)
        if want_f32:
            f32_refs[k32][...] = acc
            k32 += 1
    assert k32 == n_f32_out


def _in_proj(x2d, mod, layer, w_bf16, secs, row_of_tile, seq_len, rope_tabs=None, tm=512):
    T = x2d.shape[0]
    N = w_bf16.shape[1]
    assert N == SEC * len(secs) and T % tm == 0 and seq_len % tm == 0
    tiles_per_seq = seq_len // tm
    in_specs = [
        pl.BlockSpec((tm, D_MODEL), lambda i: (i, 0)),
        _mod_spec(layer, 0, row_of_tile),
        _mod_spec(layer, 1, row_of_tile),
        pl.BlockSpec((D_MODEL, N), lambda i: (0, 0)),
    ]
    args = [x2d, mod, mod, w_bf16]
    for kind in ("ret", "diff"):
        if any(s[0] == kind for s in secs):
            for t in rope_tabs[kind]:
                in_specs.append(pl.BlockSpec((tm, LANES), lambda i: (i % tiles_per_seq, 0)))
                args.append(t)
    n_f32 = sum(1 for s in secs if s[2])
    out_shape = [jax.ShapeDtypeStruct((T, N), BF16)] + [jax.ShapeDtypeStruct((T, SEC), F32)] * n_f32
    out_specs = [pl.BlockSpec((tm, N), lambda i: (i, 0))] + [pl.BlockSpec((tm, SEC), lambda i: (i, 0))] * n_f32
    return pl.pallas_call(
        functools.partial(_in_kernel, secs=secs, n_f32_out=n_f32),
        grid=(T // tm,),
        in_specs=in_specs,
        out_specs=out_specs,
        out_shape=out_shape,
        compiler_params=_cparams(("arbitrary",)),
        name="in_proj",
    )(*args)


def _rope_tables(seq_len, d):
    half = d // 2
    quarter = half // 2
    t = jnp.arange(seq_len)
    row = (t // GRID_W).astype(F32)
    col = (t % GRID_W).astype(F32)
    inv = ROPE_BASE ** (-jnp.arange(quarter, dtype=F32) / quarter)
    ang_r = row[:, None] * inv[None, :]
    ang_c = col[:, None] * inv[None, :]
    zero = jnp.zeros_like(ang_r)
    cos = jnp.concatenate([jnp.cos(ang_r)] * 2 + [jnp.cos(ang_c)] * 2, axis=1)
    sa = jnp.concatenate([-jnp.sin(ang_r), zero, -jnp.sin(ang_c), zero], axis=1)
    sb = jnp.concatenate([zero, jnp.sin(ang_r), zero, jnp.sin(ang_c)], axis=1)
    reps = LANES // d
    return tuple(jnp.tile(a, (1, reps)) for a in (cos, sa, sb))


def _ret_kernel(*refs, n_chunks, has_state, write_state):
    lg_ref, q_ref, k_ref, v_ref, g_ref, gnw_ref = refs[:6]
    pos = 6
    if has_state:
        s0f_ref, s0b_ref = refs[pos:pos + 2]
        pos += 2
    o_ref = refs[pos]
    pos += 1
    if write_state:
        sf_ref, sb_ref = refs[pos:pos + 2]
        pos += 2
    acc_ref = refs[pos]

    C = RET_CHUNK
    hd = pl.program_id(1)
    lgf = lg_ref[0, hd]
    lgb = lg_ref[1, hd]
    ii = lax.broadcasted_iota(jnp.int32, (C, C), 0).astype(F32)
    jj = lax.broadcasted_iota(jnp.int32, (C, C), 1).astype(F32)
    rel = ii - jj
    d_f = jnp.where(rel >= 0, jnp.exp(jnp.maximum(rel, 0.0) * lgf), 0.0)
    d_b = jnp.where(rel <= 0, jnp.exp(jnp.maximum(-rel, 0.0) * lgb), 0.0)
    d_sum = d_f + d_b
    idx = lax.broadcasted_iota(jnp.int32, (C, 1), 0).astype(F32)
    xi_f = jnp.exp((idx + 1.0) * lgf)
    zeta_f = jnp.exp((C - 1.0 - idx) * lgf)
    xi_b = jnp.exp((C - idx) * lgb)
    zeta_b = jnp.exp(idx * lgb)
    one = jnp.ones((1, 1), F32)
    gc_f = jnp.exp(one * (C * lgf))
    gc_b = jnp.exp(one * (C * lgb))

    nt = (((1,), (1,)), ((), ()))
    tn = (((0,), (0,)), ((), ()))

    if has_state:
        s_f = s0f_ref[...]
        s_b = s0b_ref[...]
    else:
        s_f = jnp.zeros((RET_DK, RET_DV), F32)
        s_b = jnp.zeros((RET_DK, RET_DV), F32)

    for n in range(n_chunks):
        sl = slice(n * C, (n + 1) * C)
        qc, kc, vc = q_ref[sl, :], k_ref[sl, :], v_ref[sl, :]
        scores = lax.dot_general(qc, kc, nt, preferred_element_type=F32) * d_sum
        inner = jnp.dot(scores.astype(BF16), vc, preferred_element_type=F32)
        cross = jnp.dot((qc.astype(F32) * xi_f).astype(BF16), s_f.astype(BF16), preferred_element_type=F32)
        acc_ref[sl, :] = inner + cross
        kz = (kc.astype(F32) * zeta_f).astype(BF16)
        s_f = gc_f * s_f + lax.dot_general(kz, vc, tn, preferred_element_type=F32)

    gnw = gnw_ref[...]
    for n in reversed(range(n_chunks)):
        sl = slice(n * C, (n + 1) * C)
        qc, kc, vc = q_ref[sl, :], k_ref[sl, :], v_ref[sl, :]
        cross = jnp.dot((qc.astype(F32) * xi_b).astype(BF16), s_b.astype(BF16), preferred_element_type=F32)
        r = acc_ref[sl, :] + cross
        mu = jnp.mean(r, axis=-1, keepdims=True)
        rc = r - mu
        var = jnp.mean(rc * rc, axis=-1, keepdims=True)
        rn = rc * lax.rsqrt(var + LN_EPS) * gnw
        o_ref[sl, :] = (_silu(g_ref[sl, :].astype(F32)) * rn).astype(BF16)
        kz = (kc.astype(F32) * zeta_b).astype(BF16)
        s_b = gc_b * s_b + lax.dot_general(kz, vc, tn, preferred_element_type=F32)

    if write_state:
        sf_ref[...] = s_f
        sb_ref[...] = s_b


def _retention(proj, lg, gn_w, batch, seq_len, states=None, write_state=False):
    T = proj.shape[0]
    hsec = SEC // LANES
    blk = lambda sec: pl.BlockSpec((seq_len, LANES), lambda b, h: (b, sec * hsec + h))
    in_specs = [pl.BlockSpec(memory_space=pltpu.SMEM), blk(0), blk(1), blk(2), blk(3),
                pl.BlockSpec((1, LANES), lambda b, h: (0, h))]
    args = [lg, proj, proj, proj, proj, gn_w.reshape(1, -1)]
    if states is not None:
        st = pl.BlockSpec((None, None, RET_DK, RET_DV), lambda b, h: (b, h, 0, 0))
        in_specs += [st, st]
        args += list(states)
    out_shape = [jax.ShapeDtypeStruct((T, SEC), BF16)]
    out_specs = [pl.BlockSpec((seq_len, LANES), lambda b, h: (b, h))]
    if write_state:
        st_o = pl.BlockSpec((None, None, RET_DK, RET_DV), lambda b, h: (b, h, 0, 0))
        out_shape += [jax.ShapeDtypeStruct((batch, H_RET, RET_DK, RET_DV), F32)] * 2
        out_specs += [st_o, st_o]
    return pl.pallas_call(
        functools.partial(_ret_kernel, n_chunks=seq_len // RET_CHUNK, has_state=states is not None,
                          write_state=write_state),
        grid=(batch, H_RET),
        in_specs=in_specs,
        out_specs=out_specs,
        out_shape=out_shape,
        scratch_shapes=[pltpu.VMEM((seq_len, RET_DV), F32)],
        compiler_params=_cparams(("arbitrary", "arbitrary")),
        name="retention",
    )(*args)


def _att_kernel(*refs, has_ctx, out_scale, key_chunk):
    lam_ref, q_ref, k_ref, v_ref = refs[:4]
    pos = 4
    if has_ctx:
        ck_ref, cv_ref = refs[pos:pos + 2]
        pos += 2
    w_ref, o_ref = refs[pos:pos + 2]
    lam = lam_ref[0]
    q = q_ref[...]
    tq = q.shape[0]
    lane = lax.broadcasted_iota(jnp.int32, q.shape, 1)
    zero = jnp.zeros_like(q)
    qq = jnp.concatenate([jnp.where(lane < DIFF_QK, q, zero), jnp.where(lane >= DIFF_QK, q, zero)], axis=0)
    qq = qq * jnp.asarray(DIFF_QK ** -0.5, BF16)
    nt = (((1,), (1,)), ((), ()))
    chunks = [(k_ref, v_ref, c * key_chunk, key_chunk) for c in range(k_ref.shape[0] // key_chunk)]
    if has_ctx:
        chunks.append((ck_ref, cv_ref, 0, ck_ref.shape[0]))
    m = l = acc = None
    for kr, vr, off, n in chunks:
        kch = kr[off:off + n, :].astype(BF16)
        vch = vr[off:off + n, :].astype(BF16)
        s = lax.dot_general(qq, kch, nt, preferred_element_type=F32)
        cm = jnp.max(s, axis=-1, keepdims=True)
        m_new = cm if m is None else jnp.maximum(m, cm)
        p = jnp.exp(s - m_new)
        ps = jnp.sum(p, axis=-1, keepdims=True)
        pv = jnp.dot(p.astype(BF16), vch, preferred_element_type=F32)
        if m is None:
            l, acc = ps, pv
        else:
            alpha = jnp.exp(m - m_new)
            l = alpha * l + ps
            acc = alpha * acc + pv
        m = m_new
    o = acc / l
    att = o[:tq] - lam * o[tq:]
    att = att * lax.rsqrt(jnp.mean(att * att, axis=-1, keepdims=True) + LN_EPS)
    o_ref[...] = (att * w_ref[...] * out_scale).astype(BF16)


def _diff_attention(proj, lam, subln_w, lam_init, batch, seq_len, ctx=None, tq=256):
    T = proj.shape[0]
    hsec = SEC // LANES
    nq = seq_len // tq
    in_specs = [
        pl.BlockSpec(memory_space=pltpu.SMEM),
        pl.BlockSpec((tq, LANES), lambda b, h, i: (b * nq + i, 4 * hsec + h)),
        pl.BlockSpec((seq_len, LANES), lambda b, h, i: (b, 5 * hsec + h)),
        pl.BlockSpec((seq_len, LANES), lambda b, h, i: (b, 6 * hsec + h)),
    ]
    args = [lam, proj, proj, proj]
    if ctx is not None:
        ck, cv = ctx
        past = ck.shape[1]
        cspec = pl.BlockSpec((None, past, LANES), lambda b, h, i: (b, 0, h))
        in_specs += [cspec, cspec]
        args += [ck, cv]
    in_specs.append(pl.BlockSpec((1, LANES), lambda b, h, i: (0, 0)))
    args.append(subln_w.reshape(1, -1))
    return pl.pallas_call(
        functools.partial(_att_kernel, has_ctx=ctx is not None, out_scale=1.0 - lam_init,
                          key_chunk=min(512, seq_len)),
        grid=(batch, H_DIFF, nq),
        in_specs=in_specs,
        out_specs=pl.BlockSpec((tq, LANES), lambda b, h, i: (b * nq + i, h)),
        out_shape=jax.ShapeDtypeStruct((T, SEC), BF16),
        compiler_params=_cparams(("arbitrary", "arbitrary", "arbitrary")),
        name="diff_attention",
    )(*args)


def _route_class(lt, rb_ref):
    sel = [jax.nn.sigmoid(lt[e:e + 1, :]) + rb_ref[e] for e in range(N_EXPERTS)]
    gscore = []
    for g in range(N_GROUPS):
        mem = sel[g * GROUP_SIZE:(g + 1) * GROUP_SIZE]
        best = None
        for a in range(GROUP_SIZE):
            for b in range(a + 1, GROUP_SIZE):
                pair = mem[a] + mem[b]
                best = pair if best is None else jnp.maximum(best, pair)
        gscore.append(best)
    gbest = gscore[0]
    gidx = jnp.zeros_like(gbest)
    for g in range(1, N_GROUPS):
        upd = gscore[g] > gbest
        gidx = jnp.where(upd, float(g), gidx)
        gbest = jnp.where(upd, gscore[g], gbest)
    msel = []
    for j in range(GROUP_SIZE):
        out = sel[j]
        for g in range(1, N_GROUPS):
            out = jnp.where(gidx == float(g), sel[g * GROUP_SIZE + j], out)
        msel.append(out)
    one = jnp.ones_like(gbest)
    zero = jnp.zeros_like(gbest)
    chosen = []
    for j in range(GROUP_SIZE):
        rank = zero
        for k in range(GROUP_SIZE):
            if k < j:
                rank = rank + jnp.where(msel[k] >= msel[j], one, zero)
            elif k > j:
                rank = rank + jnp.where(msel[k] > msel[j], one, zero)
        chosen.append(jnp.where(rank < 2.0, one, zero))
    c0, c1, c2, c3 = chosen
    order = jnp.where(c0 * c1 > 0, 0.0, jnp.where(c1 * c2 > 0, 1.0, jnp.where(c2 * c3 > 0, 2.0,
            jnp.where(c0 * c3 > 0, 3.0, jnp.where(c0 * c2 > 0, 4.0, 5.0)))))
    return gidx * float(N_PAIRS) + order


def _post_kernel(*refs, n_mix, group_tiles):
    n_groups = len(group_tiles)
    per_group = n_mix + 1
    group_refs = [refs[g * per_group:(g + 1) * per_group] for g in range(n_groups)]
    refs = refs[n_groups * per_group:]
    (w_ref, g_ref, sh_ref, sc_ref, lnw_ref, lnb_ref, rwh_ref, rwl_ref, rb_ref, tri_ref,
     x1_ref, h2_ref, info_ref, cout_ref, cnt_ref, pre_ref) = refs
    i = pl.program_id(0)

    @pl.when(i == 0)
    def _():
        cnt_ref[...] = jnp.zeros_like(cnt_ref)

    first = 0
    for g in range(n_groups):
        @pl.when(jnp.logical_and(i >= first, i < first + group_tiles[g]))
        def _(g=g):
            out = None
            off = 0
            for m_ref in group_refs[g][:n_mix]:
                width = m_ref.shape[1]
                part = jnp.dot(m_ref[...], w_ref[off:off + width, :], preferred_element_type=F32)
                out = part if out is None else out + part
                off += width
            pre_ref[...] = ALPHA * group_refs[g][n_mix][...] + g_ref[...] * out
        first += group_tiles[g]

    x1 = _layer_norm(pre_ref[...], lnw_ref[...], lnb_ref[...])
    x1_ref[...] = x1
    h2 = x1 * (1.0 + sc_ref[...]) + sh_ref[...]
    h2_ref[...] = h2
    h_hi = h2.astype(BF16)
    h_lo = (h2 - h_hi.astype(F32)).astype(BF16)
    logits = (jnp.dot(h_hi, rwh_ref[...], preferred_element_type=F32)
              + jnp.dot(h_lo, rwh_ref[...], preferred_element_type=F32)
              + jnp.dot(h_hi, rwl_ref[...], preferred_element_type=F32))
    cls = _route_class(logits.T, rb_ref)
    tm = h2.shape[0]
    crow = lax.broadcasted_iota(jnp.int32, (32, tm), 0).astype(F32)
    onehot = jnp.where(crow == cls, 1.0, 0.0)
    prefix = jnp.dot(onehot.astype(BF16), tri_ref[...], preferred_element_type=F32)
    base = cnt_ref[:, 0:1]
    rank = jnp.sum(onehot * (prefix - 1.0 + base), axis=0, keepdims=True)
    cnt_ref[...] = cnt_ref[...] + jnp.sum(onehot, axis=1, keepdims=True)
    packed = cls.astype(jnp.int32) * 65536 + rank.astype(jnp.int32)
    row = lax.broadcasted_iota(jnp.int32, (8, tm), 0)
    info_ref[...] = jnp.where(row == 0, packed, 0)

    @pl.when(i == pl.num_programs(0) - 1)
    def _():
        cout_ref[...] = cnt_ref[...]


def _tri(tm):
    return jnp.asarray(np.triu(np.ones((tm, tm), np.float32))).astype(BF16)


def _group_spec(width, tm, first_tile, n_tiles):
    return pl.BlockSpec((tm, width), lambda i, *_: (jnp.clip(i - first_tile, 0, n_tiles - 1), 0))


def _post_mix(group_mixes, group_x, w_out_bf16, mod, layer, ln_w, ln_b, rw_hi, rw_lo, router_bias, cond_row, tm):
    group_tiles = tuple(x.shape[0] // tm for x in group_x)
    n_tiles = sum(group_tiles)
    total_rows = n_tiles * tm
    row = pl.BlockSpec((tm, D_MODEL), lambda i: (i, 0))
    vec = pl.BlockSpec((1, D_MODEL), lambda i: (0, 0))
    cnt = pl.BlockSpec((32, LANES), lambda i: (0, 0))
    in_specs, args = [], []
    first = 0
    for mixes, x, nt in zip(group_mixes, group_x, group_tiles):
        for m in mixes:
            in_specs.append(_group_spec(m.shape[1], tm, first, nt))
            args.append(m)
        in_specs.append(_group_spec(D_MODEL, tm, first, nt))
        args.append(x)
        first += nt
    in_specs += [
        pl.BlockSpec((D_MODEL, D_MODEL), lambda i: (0, 0)),
        _mod_spec(layer, 2, lambda i: cond_row(i, tm)), _mod_spec(layer, 3, lambda i: cond_row(i, tm)),
        _mod_spec(layer, 4, lambda i: cond_row(i, tm)),
        vec, vec,
        pl.BlockSpec((D_MODEL, LANES), lambda i: (0, 0)),
        pl.BlockSpec((D_MODEL, LANES), lambda i: (0, 0)),
        pl.BlockSpec(memory_space=pltpu.SMEM),
        pl.BlockSpec((tm, tm), lambda i: (0, 0)),
    ]
    args += [w_out_bf16, mod, mod, mod, ln_w.reshape(1, -1), ln_b.reshape(1, -1), rw_hi, rw_lo, router_bias,
             _tri(tm)]
    return pl.pallas_call(
        functools.partial(_post_kernel, n_mix=len(group_mixes[0]), group_tiles=group_tiles),
        grid=(n_tiles,),
        in_specs=in_specs,
        out_specs=[row, row, pl.BlockSpec((8, tm), lambda i: (0, i)), cnt],
        out_shape=[jax.ShapeDtypeStruct((total_rows, D_MODEL), F32), jax.ShapeDtypeStruct((total_rows, D_MODEL), F32),
                   jax.ShapeDtypeStruct((8, total_rows), jnp.int32), jax.ShapeDtypeStruct((32, LANES), F32)],
        scratch_shapes=[pltpu.VMEM((32, LANES), F32), pltpu.VMEM((tm, D_MODEL), F32)],
        compiler_params=_cparams(("arbitrary",)),
        name="post_mix",
    )(*args)


def _plan_kernel(packed_ref, cnt_ref, src_ref, ea_ref, eb_ref, fl_ref, rs_ref, *, n_tokens, n_tiles):
    start = jnp.int32(0)
    prev_a = jnp.int32(-1)
    prev_b = jnp.int32(-1)
    for c in range(N_CLASSES):
        n = cnt_ref[c]
        tiles = lax.shift_right_logical(n + (MOE_TM - 1), MOE_SHIFT)
        row0 = start * MOE_TM
        rs_ref[c] = row0
        g, pr = divmod(c, N_PAIRS)
        a = g * GROUP_SIZE + PAIR_SLOTS[pr][0]
        b = g * GROUP_SIZE + PAIR_SLOTS[pr][1]
        first = 1 + 4 * (prev_a != a).astype(jnp.int32) + 8 * (prev_b != b).astype(jnp.int32)

        def tile_body(k, _, start=start, a=a, b=b, first=first):
            t = start + k
            ea_ref[t] = a
            eb_ref[t] = b
            fl_ref[t] = jnp.where(k == 0, first, 1)
            return 0

        lax.fori_loop(0, tiles, tile_body, 0)

        def pad_body(r, _):
            src_ref[r] = 0
            return 0

        lax.fori_loop(row0 + n, row0 + tiles * MOE_TM, pad_body, 0)
        has = tiles > 0
        prev_a = jnp.where(has, a, prev_a)
        prev_b = jnp.where(has, b, prev_b)
        start = start + tiles
    for c in range(N_CLASSES, 32):
        rs_ref[c] = 0

    def idle_body(t, _):
        ea_ref[t] = prev_a
        eb_ref[t] = prev_b
        fl_ref[t] = 0
        return 0

    lax.fori_loop(start, n_tiles, idle_body, 0)

    def tail_body(r, _):
        src_ref[r] = 0
        return 0

    lax.fori_loop(start * MOE_TM, n_tiles * MOE_TM, tail_body, 0)

    def tok_body(i, _):
        p = packed_ref[i]
        src_ref[rs_ref[lax.shift_right_logical(p, 16)] + (p & 0xFFFF)] = i
        return 0

    lax.fori_loop(0, n_tokens, tok_body, 0)


def _moe_plan(packed, counts, n_tiles):
    T = packed.shape[0]
    smem = pl.BlockSpec(memory_space=pltpu.SMEM)
    i32 = lambda n: jax.ShapeDtypeStruct((n,), jnp.int32)
    return pl.pallas_call(
        functools.partial(_plan_kernel, n_tokens=T, n_tiles=n_tiles),
        in_specs=[smem, smem],
        out_specs=[smem] * 5,
        out_shape=[i32(n_tiles * MOE_TM), i32(n_tiles), i32(n_tiles), i32(n_tiles), i32(32)],
        name="moe_plan",
    )(packed, counts)


def _moe_kernel(ea_ref, eb_ref, fl_ref, src_ref, h2_ref, rwt_ref, wga, wua, wda, wgb, wub, wdb, o_ref,
                hbuf, sem, sga, sua, sda, sgb, sub, sdb):
    t = pl.program_id(0)
    n_t = pl.num_programs(0)
    flags = fl_ref[t]
    valid = (flags & 1) != 0
    slot = t % 2

    def gather(tile, buf_slot):
        def body(r, _):
            tok = src_ref[tile * MOE_TM + r]
            pltpu.make_async_copy(h2_ref.at[pl.ds(tok, 1)], hbuf.at[buf_slot, pl.ds(r, 1)],
                                  sem.at[buf_slot]).start()
            return 0
        lax.fori_loop(0, MOE_TM, body, 0, unroll=8)

    @pl.when(t == 0)
    def _():
        gather(0, 0)

    nxt = jnp.minimum(t + 1, n_t - 1)

    @pl.when(jnp.logical_and(t + 1 < n_t, (fl_ref[nxt] & 1) != 0))
    def _():
        gather(nxt, 1 - slot)

    @pl.when((flags & 4) != 0)
    def _():
        sga[...] = wga[...].astype(BF16)
        sua[...] = wua[...].astype(BF16)
        sda[...] = wda[...].astype(BF16)

    @pl.when((flags & 8) != 0)
    def _():
        sgb[...] = wgb[...].astype(BF16)
        sub[...] = wub[...].astype(BF16)
        sdb[...] = wdb[...].astype(BF16)

    @pl.when(valid)
    def _():
        pltpu.make_async_copy(h2_ref.at[pl.ds(0, MOE_TM)], hbuf.at[slot], sem.at[slot]).wait()
        h = hbuf[slot].astype(BF16)
        nt = (((1,), (1,)), ((), ()))
        score = jax.nn.sigmoid(lax.dot_general(h, rwt_ref[...], nt, preferred_element_type=F32))
        lane = lax.broadcasted_iota(jnp.int32, score.shape, 1)
        s_a = jnp.sum(jnp.where(lane == ea_ref[t], score, 0.0), axis=1, keepdims=True)
        s_b = jnp.sum(jnp.where(lane == eb_ref[t], score, 0.0), axis=1, keepdims=True)
        tot = s_a + s_b
        act_a = _silu(jnp.dot(h, sga[...], preferred_element_type=F32)) \
            * jnp.dot(h, sua[...], preferred_element_type=F32) * (s_a / tot)
        act_b = _silu(jnp.dot(h, sgb[...], preferred_element_type=F32)) \
            * jnp.dot(h, sub[...], preferred_element_type=F32) * (s_b / tot)
        o_ref[...] = (jnp.dot(act_a.astype(BF16), sda[...], preferred_element_type=F32)
                      + jnp.dot(act_b.astype(BF16), sdb[...], preferred_element_type=F32))

    @pl.when(jnp.logical_not(valid))
    def _():
        o_ref[...] = jnp.zeros_like(o_ref)


def _moe(h2_all, plan, layer, rwt_bf16, w_gate, w_up, w_down):
    src, ea, eb, flags, _ = plan
    n_tiles = ea.shape[0]
    wspec = lambda shape, which: pl.BlockSpec(
        (None, None) + shape, (lambda t, ea, eb, fl, src: (layer, ea[t], 0, 0)) if which == 0
        else (lambda t, ea, eb, fl, src: (layer, eb[t], 0, 0)))
    up = (D_MODEL, D_FF_EXPERT)
    dn = (D_FF_EXPERT, D_MODEL)
    grid_spec = pltpu.PrefetchScalarGridSpec(
        num_scalar_prefetch=4,
        grid=(n_tiles,),
        in_specs=[
            pl.BlockSpec(memory_space=pl.ANY),
            pl.BlockSpec((N_EXPERTS, D_MODEL), lambda t, ea, eb, fl, src: (0, 0)),
            wspec(up, 0), wspec(up, 0), wspec(dn, 0), wspec(up, 1), wspec(up, 1), wspec(dn, 1),
        ],
        out_specs=pl.BlockSpec((MOE_TM, D_MODEL), lambda t, ea, eb, fl, src: (t, 0)),
        scratch_shapes=[pltpu.VMEM((2, MOE_TM, D_MODEL), F32), pltpu.SemaphoreType.DMA((2,)),
                        pltpu.VMEM(up, BF16), pltpu.VMEM(up, BF16), pltpu.VMEM(dn, BF16),
                        pltpu.VMEM(up, BF16), pltpu.VMEM(up, BF16), pltpu.VMEM(dn, BF16)],
    )
    return pl.pallas_call(
        _moe_kernel,
        grid_spec=grid_spec,
        out_shape=jax.ShapeDtypeStruct((n_tiles * MOE_TM, D_MODEL), F32),
        compiler_params=_cparams(("arbitrary",)),
        name="moe",
    )(ea, eb, flags, src, h2_all, rwt_bf16, w_gate, w_up, w_down, w_gate, w_up, w_down)


def _fin_kernel(packed_ref, rs_ref, x_ref, ys_ref, g_ref, lnw_ref, lnb_ref, *rest, tm, group_tiles):
    o_refs = rest[:len(group_tiles)]
    ybuf, sem = rest[len(group_tiles):]
    i = pl.program_id(0)
    n_i = pl.num_programs(0)
    slot = i % 2

    def gather(tile, buf_slot):
        def body(r, _):
            p = packed_ref[tile * tm + r]
            d = rs_ref[lax.shift_right_logical(p, 16)] + (p & 0xFFFF)
            pltpu.make_async_copy(ys_ref.at[pl.ds(d, 1)], ybuf.at[buf_slot, pl.ds(r, 1)], sem.at[buf_slot]).start()
            return 0
        lax.fori_loop(0, tm, body, 0, unroll=8)

    @pl.when(i == 0)
    def _():
        gather(0, 0)

    @pl.when(i + 1 < n_i)
    def _():
        gather(jnp.minimum(i + 1, n_i - 1), 1 - slot)

    pltpu.make_async_copy(ys_ref.at[pl.ds(0, tm)], ybuf.at[slot], sem.at[slot]).wait()
    y = _layer_norm(ALPHA * x_ref[...] + g_ref[...] * ybuf[slot], lnw_ref[...], lnb_ref[...])
    first = 0
    for o_ref, nt in zip(o_refs, group_tiles):
        @pl.when(jnp.logical_and(i >= first, i < first + nt))
        def _(o_ref=o_ref):
            o_ref[...] = y
        first += nt


def _fin(x1, ys, packed, row_start, mod, layer, ln_w, ln_b, cond_row, group_rows, tm=256):
    group_tiles = tuple(n // tm for n in group_rows)
    row = pl.BlockSpec((tm, D_MODEL), lambda i, *_: (i, 0))
    vec = pl.BlockSpec((1, D_MODEL), lambda i, *_: (0, 0))
    out_specs, first = [], 0
    for nt in group_tiles:
        out_specs.append(_group_spec(D_MODEL, tm, first, nt))
        first += nt
    grid_spec = pltpu.PrefetchScalarGridSpec(
        num_scalar_prefetch=2,
        grid=(sum(group_tiles),),
        in_specs=[row, pl.BlockSpec(memory_space=pl.ANY), _mod_spec(layer, 5, lambda i: cond_row(i, tm)), vec, vec],
        out_specs=out_specs,
        scratch_shapes=[pltpu.VMEM((2, tm, D_MODEL), F32), pltpu.SemaphoreType.DMA((2,))],
    )
    return pl.pallas_call(
        functools.partial(_fin_kernel, tm=tm, group_tiles=group_tiles),
        grid_spec=grid_spec,
        out_shape=[jax.ShapeDtypeStruct((n, D_MODEL), F32) for n in group_rows],
        compiler_params=_cparams(("arbitrary",)),
        name="post_moe",
    )(packed, row_start, x1, ys, mod, ln_w.reshape(1, -1), ln_b.reshape(1, -1))


def _filt_kernel(z_ref, w1_ref, b1_ref, w2_ref, b2_ref, fr_ref, w3_ref, dl_ref, o_ref):
    i = pl.program_id(0)
    z = z_ref[...]
    fr = fr_ref[...]
    a = jnp.sin(fr * (jnp.dot(z, w1_ref[...], precision=HIGHEST, preferred_element_type=F32) + b1_ref[...]))
    a = jnp.sin(fr * (jnp.dot(a, w2_ref[...], precision=HIGHEST, preferred_element_type=F32) + b2_ref[...]))
    filt = jnp.dot(a, w3_ref[...], precision=HIGHEST, preferred_element_type=F32)
    window = jnp.exp(-z[:, 0:1] * dl_ref[...]) + HY_SHIFT
    rows = i * z.shape[0] + lax.broadcasted_iota(jnp.int32, filt.shape, 0)
    o_ref[...] = jnp.where(rows == 0, 0.0, filt * window)


def _hyena_taps(seq_len, w1, b1, w2, b2, w3, freq):
    L = seq_len
    t = jnp.linspace(0.0, 1.0, L, dtype=F32)[:, None]
    bands = jnp.linspace(1e-4, HY_BANDS - 1, HY_BANDS, dtype=F32)
    ang = 2.0 * math.pi * bands[None, :] * jnp.arange(L, dtype=F32)[:, None] / L
    z = jnp.concatenate([t, jnp.cos(ang), -jnp.sin(ang)], axis=-1)
    offs = np.minimum(np.abs(np.arange(2 * L) - L), L - 1)
    emb = z.shape[1]
    z2 = jnp.zeros((2 * L, LANES), F32).at[:, :emb].set(z[offs])
    pad_c = lambda a: jnp.zeros((a.shape[0], LANES), F32).at[:, :a.shape[1]].set(a)
    pad_r = lambda a: jnp.zeros((LANES, a.shape[1]), F32).at[:a.shape[0], :].set(a)
    w1p = pad_r(pad_c(w1))
    w2p = pad_r(pad_c(w2))
    w3p = pad_r(w3)
    b1p, b2p, frp = pad_c(b1[None, :]), pad_c(b2[None, :]), pad_c(freq[None, :])
    deltas = jnp.abs(jnp.linspace(math.log(HY_DECAY_TARGET) / HY_SLOW, math.log(HY_DECAY_TARGET) / HY_FAST,
                                  D_MODEL, dtype=F32))[None, :]
    rb = min(512, L)
    cbf = 512
    ncb = D_MODEL // cbf
    nrb_back = L // rb
    sq = pl.BlockSpec((LANES, LANES), lambda i, j: (0, 0))
    vec = pl.BlockSpec((1, LANES), lambda i, j: (0, 0))
    return pl.pallas_call(
        _filt_kernel,
        grid=(2 * L // rb, ncb),
        in_specs=[
            pl.BlockSpec((rb, LANES), lambda i, j: (i, 0)),
            sq, vec, sq, vec, vec,
            pl.BlockSpec((LANES, cbf), lambda i, j: (0, jnp.where(i < nrb_back, ncb + j, j))),
            pl.BlockSpec((1, cbf), lambda i, j: (0, j)),
        ],
        out_specs=pl.BlockSpec((rb, cbf), lambda i, j: (i, j)),
        out_shape=jax.ShapeDtypeStruct((2 * L, D_MODEL), F32),
        compiler_params=_cparams(("arbitrary", "arbitrary")),
        name="hyena_taps",
    )(z2, w1p, b1p, w2p, b2p, frp, w3p, deltas)


def _dft_mats(cb):
    n = 2 * cb
    m = np.arange(cb)
    f = np.arange(cb)
    ang = 2.0 * np.pi * ((f[:, None] * m[None, :]) % n) / n
    fwd = np.concatenate([np.cos(ang), -np.sin(ang)], axis=0)
    fwd[cb, :] = np.where(m % 2 == 0, 1.0, -1.0)
    coef = np.where(f == 0, 1.0, 2.0)[None, :] / n
    inv = np.concatenate([coef * np.cos(ang.T), -coef * np.sin(ang.T)], axis=1)
    inv[:, cb] = np.where(m % 2 == 0, 1.0, -1.0) / n
    return fwd.astype(np.float32), inv.astype(np.float32)


def _hconv_kernel(x0_ref, x1_ref, v_ref, cw0_ref, cw1_ref, cw2_ref, cb0_ref, cb1_ref, cb2_ref,
                  taps_ref, fb_ref, fwd_ref, inv_ref, o_ref,
                  hs_ref, stage_ref, w32_ref, w_ref, x0c_ref, u_ref, y_ref, *, seq_len, blk, n_seq):
    L = seq_len
    nb = L // blk
    cw = x0_ref.shape[1]
    RC = 256
    bi = pl.program_id(1)
    row0 = lax.broadcasted_iota(jnp.int32, (RC, cw), 0) == 0

    @pl.when(bi == 0)
    def _():
        sign = jnp.where(lax.broadcasted_iota(jnp.int32, (2 * blk, 1), 0) % 2 == 0, 1.0, -1.0)
        for k in range(2 * nb - 1):
            base = L + (k - nb + 1) * blk
            head = taps_ref[base:base + blk, :].astype(BF16)
            tail = taps_ref[base - blk:base, :]
            tail = jnp.where(lax.broadcasted_iota(jnp.int32, tail.shape, 0) == 0, 0.0, tail).astype(BF16)
            hs_ref[k] = (jnp.dot(fwd_ref[...], head, preferred_element_type=F32)
                         + sign * jnp.dot(fwd_ref[...], tail, preferred_element_type=F32))

    zeros8 = jnp.zeros((8, cw), F32)
    for s in range(n_seq):
        r0 = s * L

        def short_conv(src_ref, cw_ref, cb_ref, store):
            stage_ref[0:8, :] = zeros8
            stage_ref[8 + L:16 + L, :] = zeros8
            for c in range(L // RC):
                stage_ref[8 + c * RC:8 + (c + 1) * RC, :] = src_ref[r0 + c * RC:r0 + (c + 1) * RC, :].astype(F32)
            w = cw_ref[...]
            for c in range(L // RC):
                lo = stage_ref[7 + c * RC:7 + (c + 1) * RC, :]
                mid = stage_ref[8 + c * RC:8 + (c + 1) * RC, :]
                hi = stage_ref[9 + c * RC:9 + (c + 1) * RC, :]
                store(c, lo * w[0:1, :] + mid * w[1:2, :] + hi * w[2:3, :] + cb_ref[...])

        def st_x1(c, val):
            w32_ref[c * RC:(c + 1) * RC, :] = val

        def st_v(c, val):
            w_ref[c * RC:(c + 1) * RC, :] = (w32_ref[c * RC:(c + 1) * RC, :] * val).astype(BF16)

        def st_x0(c, val):
            x0c_ref[c * RC:(c + 1) * RC, :] = val.astype(BF16)

        short_conv(x1_ref, cw1_ref, cb1_ref, st_x1)
        short_conv(v_ref, cw2_ref, cb2_ref, st_v)
        short_conv(x0_ref, cw0_ref, cb0_ref, st_x0)

        for j in range(nb):
            u_ref[j] = jnp.dot(fwd_ref[...], w_ref[j * blk:(j + 1) * blk, :], preferred_element_type=F32)

        for i in range(nb):
            for c in range(blk // RC):
                re = None
                im = None
                for j in range(nb):
                    k = i - j + nb - 1
                    a = u_ref[j, c * RC:(c + 1) * RC, :]
                    b = u_ref[j, blk + c * RC:blk + (c + 1) * RC, :]
                    hr = hs_ref[k, c * RC:(c + 1) * RC, :]
                    hi = hs_ref[k, blk + c * RC:blk + (c + 1) * RC, :]
                    bb = b * hi
                    if c == 0:
                        t_re = a * hr - jnp.where(row0, 0.0, bb)
                        t_im = jnp.where(row0, bb, a * hi + b * hr)
                    else:
                        t_re = a * hr - bb
                        t_im = a * hi + b * hr
                    re = t_re if re is None else re + t_re
                    im = t_im if im is None else im + t_im
                y_ref[c * RC:(c + 1) * RC, :] = re.astype(BF16)
                y_ref[blk + c * RC:blk + (c + 1) * RC, :] = im.astype(BF16)
            conv = jnp.dot(inv_ref[...], y_ref[...], preferred_element_type=F32)
            sl = slice(i * blk, (i + 1) * blk)
            z = x0c_ref[sl, :].astype(F32) * (conv + w_ref[sl, :].astype(F32) * fb_ref[...])
            o_ref[r0 + i * blk:r0 + (i + 1) * blk, :] = z.astype(BF16)


def _hyena_conv(u, conv_w, conv_b, taps, filt_bias, batch, seq_len, blk, n_seq, cw=256):
    T = u.shape[0]
    L = seq_len
    nb = L // blk
    ncw = D_MODEL // cw
    fwd_np, inv_np = _dft_mats(blk)
    fwd = jnp.asarray(fwd_np).astype(BF16)
    inv = jnp.asarray(inv_np).astype(BF16)
    rows = n_seq * L
    sec = lambda s: pl.BlockSpec((rows, cw), lambda c, b: (b, s * ncw + c))
    cws = lambda s: pl.BlockSpec((3, cw), lambda c, b: (0, s * ncw + c))
    cbs = lambda s: pl.BlockSpec((1, cw), lambda c, b: (0, s * ncw + c))
    return pl.pallas_call(
        functools.partial(_hconv_kernel, seq_len=L, blk=blk, n_seq=n_seq),
        grid=(ncw, batch // n_seq),
        in_specs=[sec(0), sec(1), sec(2), cws(0), cws(1), cws(2), cbs(0), cbs(1), cbs(2),
                  pl.BlockSpec((2 * L, cw), lambda c, b: (0, c)),
                  pl.BlockSpec((1, cw), lambda c, b: (0, c)),
                  pl.BlockSpec((2 * blk, blk), lambda c, b: (0, 0)),
                  pl.BlockSpec((blk, 2 * blk), lambda c, b: (0, 0))],
        out_specs=pl.BlockSpec((rows, cw), lambda c, b: (b, c)),
        out_shape=jax.ShapeDtypeStruct((T, D_MODEL), BF16),
        scratch_shapes=[
            pltpu.VMEM((2 * nb - 1, 2 * blk, cw), F32),
            pltpu.VMEM((L + 16, cw), F32),
            pltpu.VMEM((L, cw), F32),
            pltpu.VMEM((L, cw), BF16),
            pltpu.VMEM((L, cw), BF16),
            pltpu.VMEM((nb, 2 * blk, cw), F32),
            pltpu.VMEM((2 * blk, cw), BF16),
        ],
        compiler_params=_cparams(("arbitrary", "arbitrary")),
        name="hyena_conv",
    )(u, u, u, conv_w, conv_w, conv_w, conv_b.reshape(1, -1), conv_b.reshape(1, -1), conv_b.reshape(1, -1),
      taps, filt_bias.reshape(1, -1), fwd, inv)


def kernel(x_prompt, x_sample, cache_diff_k, cache_diff_v, state_ret_fwd, state_ret_bwd, c, c_ctx, ada_w, ada_b, ln_w, ln_b, ev_w_in, ev_w_out, ret_decay_fwd, ret_decay_bwd, ret_gn_w, diff_lambda, diff_subln_w, hy_w_in, hy_conv_w, hy_conv_b, hy_ffn_w1, hy_ffn_b1, hy_ffn_w2, hy_ffn_b2, hy_ffn_w3, hy_freq, hy_filter_bias, hy_w_out, router_w, router_bias, moe_w_gate, moe_w_up, moe_w_down):
    B, S, D = x_prompt.shape
    DB, DS, _ = x_sample.shape
    PAST = cache_diff_k.shape[2]
    TP, TS = B * S, DB * DS
    T_ALL = TP + TS
    assert D == D_MODEL and 1 + DB <= COND_ROWS and T_ALL < 65536

    cond8 = jnp.zeros((COND_ROWS, D), F32).at[0].set(c_ctx).at[1:1 + DB].set(c)
    mod = _ada_mod(cond8, ada_w, ada_b).reshape(DEPTH * COND_ROWS * N_MOD, 1, D)

    tm = 512
    groups = [
        dict(x=x_prompt.reshape(TP, D), batch=B, seq=S, off=0, cond_row=lambda i, tm: 0),
        dict(x=x_sample.reshape(TS, D), batch=DB, seq=DS, off=TP, cond_row=lambda i, tm: 1 + (i * tm) // DS),
    ]

    def cond_row_all(i, tm):
        return jnp.where(i * tm < TP, 0, 1 + (i * tm - TP) // DS)

    rw_pad = jnp.zeros((D, LANES), F32).at[:, :N_EXPERTS].set(router_w.astype(F32))
    rw_hi = rw_pad.astype(BF16)
    rw_lo = (rw_pad - rw_hi.astype(F32)).astype(BF16)
    rwt = router_w.T.astype(BF16)
    rbias = router_bias.astype(F32)
    n_tiles = T_ALL // MOE_TM + N_CLASSES
    outs = {}

    for l in range(DEPTH):
        mixes = []
        if l % 2 == 0:
            e = l // 2
            w_in = ev_w_in[e].astype(BF16)
            w_out = ev_w_out[e].astype(BF16)
            lg = jnp.stack([jnp.log1p(-jnp.exp2(ret_decay_fwd[e].astype(F32))),
                            jnp.log1p(-jnp.exp2(ret_decay_bwd[e].astype(F32)))])
            lam_init = 0.8 - 0.6 * math.exp(-0.3 * l)
            lq1, lk1, lq2, lk2 = diff_lambda[e].astype(F32)
            lam = (jnp.exp(jnp.sum(lq1 * lk1)) - jnp.exp(jnp.sum(lq2 * lk2)) + lam_init).reshape(1)
            kscale = RET_DK ** -0.5
            for gi, g in enumerate(groups):
                crow = functools.partial(g["cond_row"], tm=tm)
                if gi == 0:
                    secs = (("none", 1.0, False), ("none", kscale, False), ("none", 1.0, False),
                            ("none", 1.0, False), ("none", 1.0, False), ("none", 1.0, True),
                            ("none", 1.0, True))
                    proj, kd, vd = _in_proj(g["x"], mod, l, w_in, secs, crow, tm, tm=tm)
                    outs.setdefault("kd", []).append(kd.reshape(B, 1, S, H_DIFF, 2 * DIFF_QK))
                    outs.setdefault("vd", []).append(vd.reshape(B, 1, S, H_DIFF, DIFF_V))
                    ret, sf, sb = _retention(proj, lg, ret_gn_w[e], g["batch"], g["seq"], write_state=True)
                    outs.setdefault("sf", []).append(sf.reshape(B, 1, H_RET, RET_DK, RET_DV))
                    outs.setdefault("sb", []).append(sb.reshape(B, 1, H_RET, RET_DK, RET_DV))
                    att = _diff_attention(proj, lam, diff_subln_w[e], lam_init, g["batch"], g["seq"],
                                          tq=min(256, g["seq"]))
                else:
                    secs = (("ret", 1.0, False), ("ret", kscale, False), ("none", 1.0, False),
                            ("none", 1.0, False), ("diff", 1.0, False), ("diff", 1.0, False),
                            ("none", 1.0, False))
                    tabs = {"ret": _rope_tables(g["seq"], RET_DK), "diff": _rope_tables(g["seq"], DIFF_QK)}
                    (proj,) = _in_proj(g["x"], mod, l, w_in, secs, crow, g["seq"], rope_tabs=tabs, tm=tm)
                    ret, = _retention(proj, lg, ret_gn_w[e], g["batch"], g["seq"],
                                      states=(state_ret_fwd[:, e], state_ret_bwd[:, e]))
                    ctx = (cache_diff_k[:, e].reshape(DB, PAST, H_DIFF * 2 * DIFF_QK),
                           cache_diff_v[:, e].reshape(DB, PAST, H_DIFF * DIFF_V))
                    att = _diff_attention(proj, lam, diff_subln_w[e], lam_init, g["batch"], g["seq"], ctx=ctx)
                mixes.append((ret, att))
        else:
            o = l // 2
            w_in = hy_w_in[o].astype(BF16)
            w_out = hy_w_out[o].astype(BF16)
            secs = (("none", 1.0, False),) * (3 * D // SEC)
            for gi, g in enumerate(groups):
                crow = functools.partial(g["cond_row"], tm=tm)
                (u,) = _in_proj(g["x"], mod, l, w_in, secs, crow, tm, tm=tm)
                taps = _hyena_taps(g["seq"], hy_ffn_w1[o], hy_ffn_b1[o], hy_ffn_w2[o], hy_ffn_b2[o],
                                   hy_ffn_w3[o], hy_freq[o])
                blk = min(g["seq"], 1024)
                n_seq = max(1, 1024 // g["seq"])
                z = _hyena_conv(u, hy_conv_w[o], hy_conv_b[o], taps, hy_filter_bias[o], g["batch"], g["seq"],
                                blk, n_seq)
                mixes.append((z,))

        x1_all, h2_all, info_all, counts = _post_mix(mixes, [g["x"] for g in groups], w_out, mod, l, ln_w[l, 0],
                                                     ln_b[l, 0], rw_hi, rw_lo, rbias, cond_row_all, tm)
        packed = info_all[0]
        plan = _moe_plan(packed, counts[:, 0].astype(jnp.int32), n_tiles)
        ys = _moe(h2_all, plan, l, rwt, moe_w_gate, moe_w_up, moe_w_down)
        xs = _fin(x1_all, ys, packed, plan[4], mod, l, ln_w[l, 1], ln_b[l, 1], cond_row_all, (TP, TS))
        for g, x in zip(groups, xs):
            g["x"] = x

    y_prompt = groups[0]["x"].reshape(B, S, D)
    y_sample = groups[1]["x"].reshape(DB, DS, D)
    cat = lambda xs: xs[0] if len(xs) == 1 else jnp.concatenate(xs, axis=1)
    return (y_prompt, y_sample, cat(outs["kd"]), cat(outs["vd"]), cat(outs["sf"]), cat(outs["sb"]))
```

```python
import functools
import math

import numpy as np
import jax
import jax.numpy as jnp
from jax import lax
from jax.experimental import pallas as pl
from jax.experimental.pallas import tpu as pltpu

F32 = jnp.float32
BF16 = jnp.bfloat16
HIGHEST = lax.Precision.HIGHEST

D_MODEL = 1024
DEPTH = 2
GRID_W = 64
H_RET = 4
RET_DK = 128
RET_DV = 128
RET_CHUNK = 128
H_DIFF = 4
DIFF_QK = 64
DIFF_V = 128
ROPE_BASE = 10000.0
HY_BANDS = 16
HY_FH = 64
HY_DECAY_TARGET = 1e-2
HY_FAST = 0.3
HY_SLOW = 1.5
HY_SHIFT = 0.05
N_EXPERTS = 16
N_GROUPS = 4
GROUP_SIZE = N_EXPERTS // N_GROUPS
D_FF_EXPERT = 512
ALPHA = (2 * DEPTH) ** 0.25
LN_EPS = 1e-5

LANES = 128
SEC = 512
COND_ROWS = 8
N_MOD = 6
VMEM_LIMIT = 50 * 1024 * 1024

PAIR_SLOTS = ((0, 1), (2, 1), (2, 3), (0, 3), (0, 2), (1, 3))
N_PAIRS = len(PAIR_SLOTS)
N_CLASSES = N_GROUPS * N_PAIRS
MOE_TM = 256
MOE_SHIFT = MOE_TM.bit_length() - 1
TOK_ROWS = D_MODEL // LANES


def _cparams(sem):
    return pltpu.CompilerParams(dimension_semantics=sem, vmem_limit_bytes=VMEM_LIMIT)


def _silu(x):
    return x * jax.nn.sigmoid(x)


def _store_token_tiles(ref, x):
    n = x.shape[0]
    for j in range(TOK_ROWS):
        ref[pl.ds(j, n, stride=TOK_ROWS), :] = x[:, j * LANES:(j + 1) * LANES]


def _load_token_tiles(ref, n, slot=None):
    idx = () if slot is None else (slot,)
    return jnp.concatenate([ref[idx + (pl.ds(j, n, stride=TOK_ROWS), slice(None))] for j in range(TOK_ROWS)],
                           axis=1)


def _layer_norm(x, w, b):
    mu = jnp.mean(x, axis=-1, keepdims=True)
    xc = x - mu
    var = jnp.mean(xc * xc, axis=-1, keepdims=True)
    return xc * lax.rsqrt(var + LN_EPS) * w + b


def _ada_kernel(c_ref, w_ref, b_ref, o_ref):
    c = c_ref[...]
    o_ref[...] = jnp.dot(_silu(c), w_ref[...], precision=HIGHEST, preferred_element_type=F32) + b_ref[...]


def _ada_mod(cond8, ada_w, ada_b):
    tn = 1024
    nj = ada_w.shape[2] // tn
    return pl.pallas_call(
        _ada_kernel,
        grid=(DEPTH, nj),
        in_specs=[
            pl.BlockSpec((COND_ROWS, D_MODEL), lambda l, j: (0, 0)),
            pl.BlockSpec((None, D_MODEL, tn), lambda l, j: (l, 0, j)),
            pl.BlockSpec((None, 1, tn), lambda l, j: (l, 0, j)),
        ],
        out_specs=pl.BlockSpec((None, COND_ROWS, tn), lambda l, j: (l, 0, j)),
        out_shape=jax.ShapeDtypeStruct((DEPTH, COND_ROWS, ada_w.shape[2]), F32),
        compiler_params=_cparams(("arbitrary", "arbitrary")),
        name="ada_mod",
    )(cond8, ada_w, ada_b.reshape(DEPTH, 1, -1))


def _mod_spec(layer, chunk, row_of_tile):
    def imap(i, *_):
        return ((layer * COND_ROWS + row_of_tile(i)) * N_MOD + chunk, 0, 0)
    return pl.BlockSpec((None, 1, D_MODEL), imap)


def _rope(a, tabs, quarter):
    c, sa, sb = tabs
    out = []
    for hb in range(a.shape[1] // LANES):
        blk = a[:, hb * LANES:(hb + 1) * LANES]
        up = pltpu.roll(blk, LANES - quarter, axis=1)
        dn = pltpu.roll(blk, quarter, axis=1)
        out.append(blk * c + up * sa + dn * sb)
    return jnp.concatenate(out, axis=1)


def _in_kernel(*refs, secs, n_f32_out):
    x_ref, sh_ref, sc_ref, w_ref = refs[:4]
    pos = 4
    tabs = {}
    for kind in ("ret", "diff"):
        if any(s[0] == kind for s in secs):
            tabs[kind] = tuple(r[...] for r in refs[pos:pos + 3])
            pos += 3
    o_ref = refs[pos]
    f32_refs = refs[pos + 1:]
    h = (x_ref[...] * (1.0 + sc_ref[...]) + sh_ref[...]).astype(BF16)
    k32 = 0
    for s, (kind, scale, want_f32) in enumerate(secs):
        acc = jnp.dot(h, w_ref[:, s * SEC:(s + 1) * SEC], preferred_element_type=F32)
        if scale != 1.0:
            acc = acc * scale
        if kind == "ret":
            acc = _rope(acc, tabs["ret"], RET_DK // 4)
        elif kind == "diff":
            acc = _rope(acc, tabs["diff"], DIFF_QK // 4)
        o_ref[:, s * SEC:(s + 1) * SEC] = acc.astype(BF16)
        if want_f32:
            f32_refs[k32][...] = acc
            k32 += 1
    assert k32 == n_f32_out


def _in_proj(x2d, mod, layer, w_bf16, secs, row_of_tile, seq_len, rope_tabs=None, tm=512):
    T = x2d.shape[0]
    N = w_bf16.shape[1]
    assert N == SEC * len(secs) and T % tm == 0 and seq_len % tm == 0
    tiles_per_seq = seq_len // tm
    in_specs = [
        pl.BlockSpec((tm, D_MODEL), lambda i: (i, 0)),
        _mod_spec(layer, 0, row_of_tile),
        _mod_spec(layer, 1, row_of_tile),
        pl.BlockSpec((D_MODEL, N), lambda i: (0, 0)),
    ]
    args = [x2d, mod, mod, w_bf16]
    for kind in ("ret", "diff"):
        if any(s[0] == kind for s in secs):
            for t in rope_tabs[kind]:
                in_specs.append(pl.BlockSpec((tm, LANES), lambda i: (i % tiles_per_seq, 0)))
                args.append(t)
    n_f32 = sum(1 for s in secs if s[2])
    out_shape = [jax.ShapeDtypeStruct((T, N), BF16)] + [jax.ShapeDtypeStruct((T, SEC), F32)] * n_f32
    out_specs = [pl.BlockSpec((tm, N), lambda i: (i, 0))] + [pl.BlockSpec((tm, SEC), lambda i: (i, 0))] * n_f32
    return pl.pallas_call(
        functools.partial(_in_kernel, secs=secs, n_f32_out=n_f32),
        grid=(T // tm,),
        in_specs=in_specs,
        out_specs=out_specs,
        out_shape=out_shape,
        compiler_params=_cparams(("arbitrary",)),
        name="in_proj",
    )(*args)


def _rope_tables(seq_len, d):
    half = d // 2
    quarter = half // 2
    t = jnp.arange(seq_len)
    row = (t // GRID_W).astype(F32)
    col = (t % GRID_W).astype(F32)
    inv = ROPE_BASE ** (-jnp.arange(quarter, dtype=F32) / quarter)
    ang_r = row[:, None] * inv[None, :]
    ang_c = col[:, None] * inv[None, :]
    zero = jnp.zeros_like(ang_r)
    cos = jnp.concatenate([jnp.cos(ang_r)] * 2 + [jnp.cos(ang_c)] * 2, axis=1)
    sa = jnp.concatenate([-jnp.sin(ang_r), zero, -jnp.sin(ang_c), zero], axis=1)
    sb = jnp.concatenate([zero, jnp.sin(ang_r), zero, jnp.sin(ang_c)], axis=1)
    reps = LANES // d
    return tuple(jnp.tile(a, (1, reps)) for a in (cos, sa, sb))


def _ret_kernel(*refs, n_chunks, has_state, write_state):
    lg_ref, q_ref, k_ref, v_ref, g_ref, gnw_ref = refs[:6]
    pos = 6
    if has_state:
        s0f_ref, s0b_ref = refs[pos:pos + 2]
        pos += 2
    o_ref = refs[pos]
    pos += 1
    if write_state:
        sf_ref, sb_ref = refs[pos:pos + 2]
        pos += 2
    acc_ref = refs[pos]

    C = RET_CHUNK
    hd = pl.program_id(1)
    lgf = lg_ref[0, hd]
    lgb = lg_ref[1, hd]
    ii = lax.broadcasted_iota(jnp.int32, (C, C), 0).astype(F32)
    jj = lax.broadcasted_iota(jnp.int32, (C, C), 1).astype(F32)
    rel = ii - jj
    d_f = jnp.where(rel >= 0, jnp.exp(jnp.maximum(rel, 0.0) * lgf), 0.0)
    d_b = jnp.where(rel <= 0, jnp.exp(jnp.maximum(-rel, 0.0) * lgb), 0.0)
    d_sum = d_f + d_b
    idx = lax.broadcasted_iota(jnp.int32, (C, 1), 0).astype(F32)
    xi_f = jnp.exp((idx + 1.0) * lgf)
    zeta_f = jnp.exp((C - 1.0 - idx) * lgf)
    xi_b = jnp.exp((C - idx) * lgb)
    zeta_b = jnp.exp(idx * lgb)
    one = jnp.ones((1, 1), F32)
    gc_f = jnp.exp(one * (C * lgf))
    gc_b = jnp.exp(one * (C * lgb))

    nt = (((1,), (1,)), ((), ()))
    tn = (((0,), (0,)), ((), ()))

    if has_state:
        s_f = s0f_ref[...]
        s_b = s0b_ref[...]
    else:
        s_f = jnp.zeros((RET_DK, RET_DV), F32)
        s_b = jnp.zeros((RET_DK, RET_DV), F32)

    for n in range(n_chunks):
        sl = slice(n * C, (n + 1) * C)
        qc, kc, vc = q_ref[sl, :], k_ref[sl, :], v_ref[sl, :]
        scores = lax.dot_general(qc, kc, nt, preferred_element_type=F32) * d_sum
        inner = jnp.dot(scores.astype(BF16), vc, preferred_element_type=F32)
        cross = jnp.dot((qc.astype(F32) * xi_f).astype(BF16), s_f.astype(BF16), preferred_element_type=F32)
        acc_ref[sl, :] = inner + cross
        kz = (kc.astype(F32) * zeta_f).astype(BF16)
        s_f = gc_f * s_f + lax.dot_general(kz, vc, tn, preferred_element_type=F32)

    gnw = gnw_ref[...]
    for n in reversed(range(n_chunks)):
        sl = slice(n * C, (n + 1) * C)
        qc, kc, vc = q_ref[sl, :], k_ref[sl, :], v_ref[sl, :]
        cross = jnp.dot((qc.astype(F32) * xi_b).astype(BF16), s_b.astype(BF16), preferred_element_type=F32)
        r = acc_ref[sl, :] + cross
        mu = jnp.mean(r, axis=-1, keepdims=True)
        rc = r - mu
        var = jnp.mean(rc * rc, axis=-1, keepdims=True)
        rn = rc * lax.rsqrt(var + LN_EPS) * gnw
        o_ref[sl, :] = (_silu(g_ref[sl, :].astype(F32)) * rn).astype(BF16)
        kz = (kc.astype(F32) * zeta_b).astype(BF16)
        s_b = gc_b * s_b + lax.dot_general(kz, vc, tn, preferred_element_type=F32)

    if write_state:
        sf_ref[...] = s_f
        sb_ref[...] = s_b


def _retention(proj, lg, gn_w, batch, seq_len, states=None, write_state=False):
    T = proj.shape[0]
    hsec = SEC // LANES
    blk = lambda sec: pl.BlockSpec((seq_len, LANES), lambda b, h: (b, sec * hsec + h))
    in_specs = [pl.BlockSpec(memory_space=pltpu.SMEM), blk(0), blk(1), blk(2), blk(3),
                pl.BlockSpec((1, LANES), lambda b, h: (0, h))]
    args = [lg, proj, proj, proj, proj, gn_w.reshape(1, -1)]
    if states is not None:
        st = pl.BlockSpec((None, None, RET_DK, RET_DV), lambda b, h: (b, h, 0, 0))
        in_specs += [st, st]
        args += list(states)
    out_shape = [jax.ShapeDtypeStruct((T, SEC), BF16)]
    out_specs = [pl.BlockSpec((seq_len, LANES), lambda b, h: (b, h))]
    if write_state:
        st_o = pl.BlockSpec((None, None, RET_DK, RET_DV), lambda b, h: (b, h, 0, 0))
        out_shape += [jax.ShapeDtypeStruct((batch, H_RET, RET_DK, RET_DV), F32)] * 2
        out_specs += [st_o, st_o]
    return pl.pallas_call(
        functools.partial(_ret_kernel, n_chunks=seq_len // RET_CHUNK, has_state=states is not None,
                          write_state=write_state),
        grid=(batch, H_RET),
        in_specs=in_specs,
        out_specs=out_specs,
        out_shape=out_shape,
        scratch_shapes=[pltpu.VMEM((seq_len, RET_DV), F32)],
        compiler_params=_cparams(("arbitrary", "arbitrary")),
        name="retention",
    )(*args)


def _att_kernel(*refs, has_ctx, out_scale, key_chunk):
    lam_ref, q_ref, k_ref, v_ref = refs[:4]
    pos = 4
    if has_ctx:
        ck_ref, cv_ref = refs[pos:pos + 2]
        pos += 2
    w_ref, o_ref = refs[pos:pos + 2]
    lam = lam_ref[0]
    q = q_ref[...]
    tq = q.shape[0]
    lane = lax.broadcasted_iota(jnp.int32, q.shape, 1)
    zero = jnp.zeros_like(q)
    qq = jnp.concatenate([jnp.where(lane < DIFF_QK, q, zero), jnp.where(lane >= DIFF_QK, q, zero)], axis=0)
    qq = qq * jnp.asarray(DIFF_QK ** -0.5, BF16)
    nt = (((1,), (1,)), ((), ()))
    chunks = [(k_ref, v_ref, c * key_chunk, key_chunk) for c in range(k_ref.shape[0] // key_chunk)]
    if has_ctx:
        chunks.append((ck_ref, cv_ref, 0, ck_ref.shape[0]))
    m = l = acc = None
    for kr, vr, off, n in chunks:
        kch = kr[off:off + n, :].astype(BF16)
        vch = vr[off:off + n, :].astype(BF16)
        s = lax.dot_general(qq, kch, nt, preferred_element_type=F32)
        cm = jnp.max(s, axis=-1, keepdims=True)
        m_new = cm if m is None else jnp.maximum(m, cm)
        p = jnp.exp(s - m_new)
        ps = jnp.sum(p, axis=-1, keepdims=True)
        pv = jnp.dot(p.astype(BF16), vch, preferred_element_type=F32)
        if m is None:
            l, acc = ps, pv
        else:
            alpha = jnp.exp(m - m_new)
            l = alpha * l + ps
            acc = alpha * acc + pv
        m = m_new
    o = acc / l
    att = o[:tq] - lam * o[tq:]
    att = att * lax.rsqrt(jnp.mean(att * att, axis=-1, keepdims=True) + LN_EPS)
    o_ref[...] = (att * w_ref[...] * out_scale).astype(BF16)


def _diff_attention(proj, lam, subln_w, lam_init, batch, seq_len, ctx=None, tq=256):
    T = proj.shape[0]
    hsec = SEC // LANES
    nq = seq_len // tq
    in_specs = [
        pl.BlockSpec(memory_space=pltpu.SMEM),
        pl.BlockSpec((tq, LANES), lambda b, h, i: (b * nq + i, 4 * hsec + h)),
        pl.BlockSpec((seq_len, LANES), lambda b, h, i: (b, 5 * hsec + h)),
        pl.BlockSpec((seq_len, LANES), lambda b, h, i: (b, 6 * hsec + h)),
    ]
    args = [lam, proj, proj, proj]
    if ctx is not None:
        ck, cv = ctx
        past = ck.shape[1]
        cspec = pl.BlockSpec((None, past, LANES), lambda b, h, i: (b, 0, h))
        in_specs += [cspec, cspec]
        args += [ck, cv]
    in_specs.append(pl.BlockSpec((1, LANES), lambda b, h, i: (0, 0)))
    args.append(subln_w.reshape(1, -1))
    return pl.pallas_call(
        functools.partial(_att_kernel, has_ctx=ctx is not None, out_scale=1.0 - lam_init,
                          key_chunk=min(512, seq_len)),
        grid=(batch, H_DIFF, nq),
        in_specs=in_specs,
        out_specs=pl.BlockSpec((tq, LANES), lambda b, h, i: (b * nq + i, h)),
        out_shape=jax.ShapeDtypeStruct((T, SEC), BF16),
        compiler_params=_cparams(("arbitrary", "arbitrary", "arbitrary")),
        name="diff_attention",
    )(*args)


def _route_class(lt, rb_ref):
    sel = [jax.nn.sigmoid(lt[e:e + 1, :]) + rb_ref[e] for e in range(N_EXPERTS)]
    gscore = []
    for g in range(N_GROUPS):
        mem = sel[g * GROUP_SIZE:(g + 1) * GROUP_SIZE]
        best = None
        for a in range(GROUP_SIZE):
            for b in range(a + 1, GROUP_SIZE):
                pair = mem[a] + mem[b]
                best = pair if best is None else jnp.maximum(best, pair)
        gscore.append(best)
    gbest = gscore[0]
    gidx = jnp.zeros_like(gbest)
    for g in range(1, N_GROUPS):
        upd = gscore[g] > gbest
        gidx = jnp.where(upd, float(g), gidx)
        gbest = jnp.where(upd, gscore[g], gbest)
    msel = []
    for j in range(GROUP_SIZE):
        out = sel[j]
        for g in range(1, N_GROUPS):
            out = jnp.where(gidx == float(g), sel[g * GROUP_SIZE + j], out)
        msel.append(out)
    one = jnp.ones_like(gbest)
    zero = jnp.zeros_like(gbest)
    chosen = []
    for j in range(GROUP_SIZE):
        rank = zero
        for k in range(GROUP_SIZE):
            if k < j:
                rank = rank + jnp.where(msel[k] >= msel[j], one, zero)
            elif k > j:
                rank = rank + jnp.where(msel[k] > msel[j], one, zero)
        chosen.append(jnp.where(rank < 2.0, one, zero))
    c0, c1, c2, c3 = chosen
    order = jnp.where(c0 * c1 > 0, 0.0, jnp.where(c1 * c2 > 0, 1.0, jnp.where(c2 * c3 > 0, 2.0,
            jnp.where(c0 * c3 > 0, 3.0, jnp.where(c0 * c2 > 0, 4.0, 5.0)))))
    return gidx * float(N_PAIRS) + order


def _post_kernel(*refs, n_mix, group_tiles):
    n_groups = len(group_tiles)
    per_group = n_mix + 1
    group_refs = [refs[g * per_group:(g + 1) * per_group] for g in range(n_groups)]
    refs = refs[n_groups * per_group:]
    (w_ref, g_ref, sh_ref, sc_ref, lnw_ref, lnb_ref, rwh_ref, rwl_ref, rb_ref, tri_ref,
     x1_ref, h2_ref, info_ref, cout_ref, cnt_ref, pre_ref) = refs
    i = pl.program_id(0)

    @pl.when(i == 0)
    def _():
        cnt_ref[...] = jnp.zeros_like(cnt_ref)

    first = 0
    for g in range(n_groups):
        @pl.when(jnp.logical_and(i >= first, i < first + group_tiles[g]))
        def _(g=g):
            out = None
            off = 0
            for m_ref in group_refs[g][:n_mix]:
                width = m_ref.shape[1]
                part = jnp.dot(m_ref[...], w_ref[off:off + width, :], preferred_element_type=F32)
                out = part if out is None else out + part
                off += width
            pre_ref[...] = ALPHA * group_refs[g][n_mix][...] + g_ref[...] * out
        first += group_tiles[g]

    x1 = _layer_norm(pre_ref[...], lnw_ref[...], lnb_ref[...])
    x1_ref[...] = x1
    h2 = x1 * (1.0 + sc_ref[...]) + sh_ref[...]
    _store_token_tiles(h2_ref, h2)
    h_hi = h2.astype(BF16)
    h_lo = (h2 - h_hi.astype(F32)).astype(BF16)
    logits = (jnp.dot(h_hi, rwh_ref[...], preferred_element_type=F32)
              + jnp.dot(h_lo, rwh_ref[...], preferred_element_type=F32)
              + jnp.dot(h_hi, rwl_ref[...], preferred_element_type=F32))
    cls = _route_class(logits.T, rb_ref)
    tm = h2.shape[0]
    crow = lax.broadcasted_iota(jnp.int32, (32, tm), 0).astype(F32)
    onehot = jnp.where(crow == cls, 1.0, 0.0)
    prefix = jnp.dot(onehot.astype(BF16), tri_ref[...], preferred_element_type=F32)
    base = cnt_ref[:, 0:1]
    rank = jnp.sum(onehot * (prefix - 1.0 + base), axis=0, keepdims=True)
    cnt_ref[...] = cnt_ref[...] + jnp.sum(onehot, axis=1, keepdims=True)
    packed = cls.astype(jnp.int32) * 65536 + rank.astype(jnp.int32)
    row = lax.broadcasted_iota(jnp.int32, (8, tm), 0)
    info_ref[...] = jnp.where(row == 0, packed, 0)

    @pl.when(i == pl.num_programs(0) - 1)
    def _():
        cout_ref[...] = cnt_ref[...]


def _tri(tm):
    return jnp.asarray(np.triu(np.ones((tm, tm), np.float32))).astype(BF16)


def _group_spec(width, tm, first_tile, n_tiles):
    return pl.BlockSpec((tm, width), lambda i, *_: (jnp.clip(i - first_tile, 0, n_tiles - 1), 0))


def _post_mix(group_mixes, group_x, w_out_bf16, mod, layer, ln_w, ln_b, rw_hi, rw_lo, router_bias, cond_row, tm):
    group_tiles = tuple(x.shape[0] // tm for x in group_x)
    n_tiles = sum(group_tiles)
    total_rows = n_tiles * tm
    row = pl.BlockSpec((tm, D_MODEL), lambda i: (i, 0))
    vec = pl.BlockSpec((1, D_MODEL), lambda i: (0, 0))
    cnt = pl.BlockSpec((32, LANES), lambda i: (0, 0))
    in_specs, args = [], []
    first = 0
    for mixes, x, nt in zip(group_mixes, group_x, group_tiles):
        for m in mixes:
            in_specs.append(_group_spec(m.shape[1], tm, first, nt))
            args.append(m)
        in_specs.append(_group_spec(D_MODEL, tm, first, nt))
        args.append(x)
        first += nt
    in_specs += [
        pl.BlockSpec((D_MODEL, D_MODEL), lambda i: (0, 0)),
        _mod_spec(layer, 2, lambda i: cond_row(i, tm)), _mod_spec(layer, 3, lambda i: cond_row(i, tm)),
        _mod_spec(layer, 4, lambda i: cond_row(i, tm)),
        vec, vec,
        pl.BlockSpec((D_MODEL, LANES), lambda i: (0, 0)),
        pl.BlockSpec((D_MODEL, LANES), lambda i: (0, 0)),
        pl.BlockSpec(memory_space=pltpu.SMEM),
        pl.BlockSpec((tm, tm), lambda i: (0, 0)),
    ]
    args += [w_out_bf16, mod, mod, mod, ln_w.reshape(1, -1), ln_b.reshape(1, -1), rw_hi, rw_lo, router_bias,
             _tri(tm)]
    return pl.pallas_call(
        functools.partial(_post_kernel, n_mix=len(group_mixes[0]), group_tiles=group_tiles),
        grid=(n_tiles,),
        in_specs=in_specs,
        out_specs=[row, pl.BlockSpec((tm * TOK_ROWS, LANES), lambda i: (i, 0)),
                   pl.BlockSpec((8, tm), lambda i: (0, i)), cnt],
        out_shape=[jax.ShapeDtypeStruct((total_rows, D_MODEL), F32),
                   jax.ShapeDtypeStruct((total_rows * TOK_ROWS, LANES), F32),
                   jax.ShapeDtypeStruct((8, total_rows), jnp.int32), jax.ShapeDtypeStruct((32, LANES), F32)],
        scratch_shapes=[pltpu.VMEM((32, LANES), F32), pltpu.VMEM((tm, D_MODEL), F32)],
        compiler_params=_cparams(("arbitrary",)),
        name="post_mix",
    )(*args)


def _plan_kernel(cnt_ref, ea_ref, eb_ref, fl_ref, rs_ref, *, n_tiles):
    start = jnp.int32(0)
    prev_a = jnp.int32(-1)
    prev_b = jnp.int32(-1)
    for c in range(N_CLASSES):
        n = cnt_ref[c]
        tiles = lax.shift_right_logical(n + (MOE_TM - 1), MOE_SHIFT)
        row0 = start * MOE_TM
        rs_ref[c] = row0
        g, pr = divmod(c, N_PAIRS)
        a = g * GROUP_SIZE + PAIR_SLOTS[pr][0]
        b = g * GROUP_SIZE + PAIR_SLOTS[pr][1]
        first = 1 + 4 * (prev_a != a).astype(jnp.int32) + 8 * (prev_b != b).astype(jnp.int32)

        def tile_body(k, _, start=start, a=a, b=b, first=first):
            t = start + k
            ea_ref[t] = a
            eb_ref[t] = b
            fl_ref[t] = jnp.where(k == 0, first, 1)
            return 0

        lax.fori_loop(0, tiles, tile_body, 0)
        has = tiles > 0
        prev_a = jnp.where(has, a, prev_a)
        prev_b = jnp.where(has, b, prev_b)
        start = start + tiles
    for c in range(N_CLASSES, 31):
        rs_ref[c] = 0
    rs_ref[31] = start

    def idle_body(t, _):
        ea_ref[t] = prev_a
        eb_ref[t] = prev_b
        fl_ref[t] = 0
        return 0

    lax.fori_loop(start, n_tiles, idle_body, 0)


def _moe_plan(counts, n_tiles):
    smem = pl.BlockSpec(memory_space=pltpu.SMEM)
    i32 = lambda n: jax.ShapeDtypeStruct((n,), jnp.int32)
    return pl.pallas_call(
        functools.partial(_plan_kernel, n_tiles=n_tiles),
        in_specs=[smem],
        out_specs=[smem] * 4,
        out_shape=[i32(n_tiles), i32(n_tiles), i32(n_tiles), i32(32)],
        name="moe_plan",
    )(counts)


def _class_row(packed, rs_ref):
    return rs_ref[lax.shift_right_logical(packed, 16)] + (packed & 0xFFFF)


def _dispatch_kernel(packed_ref, rs_ref, cnt_ref, h2_ref, hs_ref, sem, fill_sem, *, tm, n_tiles):
    i = pl.program_id(0)
    n_i = pl.num_programs(0)
    slot = i % 2
    used = rs_ref[31]

    def tile_rows(row, n=1):
        return pl.ds(pl.multiple_of(row * TOK_ROWS, TOK_ROWS), n * TOK_ROWS)

    def fill_copy(row):
        return pltpu.make_async_copy(h2_ref.at[tile_rows(0)], hs_ref.at[tile_rows(row)], fill_sem)

    def idle_copy(t):
        return pltpu.make_async_copy(h2_ref.at[tile_rows(0, MOE_TM)], hs_ref.at[tile_rows(t * MOE_TM, MOE_TM)],
                                     fill_sem)

    def class_pad(c):
        n = cnt_ref[c]
        first = rs_ref[c] + n
        last = rs_ref[c] + lax.shift_left(lax.shift_right_logical(n + (MOE_TM - 1), MOE_SHIFT), MOE_SHIFT)
        return first, last

    @pl.when(i == 0)
    def _():
        for c in range(N_CLASSES):
            first, last = class_pad(c)
            lax.fori_loop(first, last, lambda r, _: (fill_copy(r).start(), 0)[1], 0)
        lax.fori_loop(used, n_tiles, lambda t, _: (idle_copy(t).start(), 0)[1], 0)

    def body(r, _):
        tok = i * tm + r
        pltpu.make_async_copy(h2_ref.at[tile_rows(tok)], hs_ref.at[tile_rows(_class_row(packed_ref[tok], rs_ref))],
                              sem.at[slot]).start()
        return 0

    lax.fori_loop(0, tm, body, 0, unroll=8)

    def wait_tile_copies(s):
        pltpu.make_async_copy(h2_ref.at[tile_rows(0, tm)], hs_ref.at[tile_rows(0, tm)], sem.at[s]).wait()

    @pl.when(i > 0)
    def _():
        wait_tile_copies(1 - slot)

    @pl.when(i == n_i - 1)
    def _():
        wait_tile_copies(slot)
        for c in range(N_CLASSES):
            first, last = class_pad(c)
            lax.fori_loop(first, last, lambda r, _: (fill_copy(r).wait(), 0)[1], 0)
        lax.fori_loop(used, n_tiles, lambda t, _: (idle_copy(t).wait(), 0)[1], 0)


def _dispatch(h2_tiles, packed, plan, counts, n_tiles, tm=256):
    T = packed.shape[0]
    grid_spec = pltpu.PrefetchScalarGridSpec(
        num_scalar_prefetch=3,
        grid=(T // tm,),
        in_specs=[pl.BlockSpec(memory_space=pl.ANY)],
        out_specs=pl.BlockSpec(memory_space=pl.ANY),
        scratch_shapes=[pltpu.SemaphoreType.DMA((2,)), pltpu.SemaphoreType.DMA(())],
    )
    return pl.pallas_call(
        functools.partial(_dispatch_kernel, tm=tm, n_tiles=n_tiles),
        grid_spec=grid_spec,
        out_shape=jax.ShapeDtypeStruct((n_tiles * MOE_TM * TOK_ROWS, LANES), F32),
        compiler_params=_cparams(("arbitrary",)),
        name="moe_dispatch",
    )(packed, plan[3], counts, h2_tiles)


def _moe_kernel(ea_ref, eb_ref, fl_ref, h_ref, rwt_ref, wga, wua, wda, wgb, wub, wdb, o_ref,
                sga, sua, sda, sgb, sub, sdb):
    t = pl.program_id(0)
    flags = fl_ref[t]
    valid = (flags & 1) != 0

    @pl.when((flags & 4) != 0)
    def _():
        sga[...] = wga[...].astype(BF16)
        sua[...] = wua[...].astype(BF16)
        sda[...] = wda[...].astype(BF16)

    @pl.when((flags & 8) != 0)
    def _():
        sgb[...] = wgb[...].astype(BF16)
        sub[...] = wub[...].astype(BF16)
        sdb[...] = wdb[...].astype(BF16)

    @pl.when(valid)
    def _():
        h = _load_token_tiles(h_ref, MOE_TM).astype(BF16)
        nt = (((1,), (1,)), ((), ()))
        score = jax.nn.sigmoid(lax.dot_general(h, rwt_ref[...], nt, preferred_element_type=F32))
        lane = lax.broadcasted_iota(jnp.int32, score.shape, 1)
        s_a = jnp.sum(jnp.where(lane == ea_ref[t], score, 0.0), axis=1, keepdims=True)
        s_b = jnp.sum(jnp.where(lane == eb_ref[t], score, 0.0), axis=1, keepdims=True)
        tot = s_a + s_b
        act_a = _silu(jnp.dot(h, sga[...], preferred_element_type=F32)) \
            * jnp.dot(h, sua[...], preferred_element_type=F32) * (s_a / tot)
        act_b = _silu(jnp.dot(h, sgb[...], preferred_element_type=F32)) \
            * jnp.dot(h, sub[...], preferred_element_type=F32) * (s_b / tot)
        _store_token_tiles(o_ref, jnp.dot(act_a.astype(BF16), sda[...], preferred_element_type=F32)
                           + jnp.dot(act_b.astype(BF16), sdb[...], preferred_element_type=F32))

    @pl.when(jnp.logical_not(valid))
    def _():
        o_ref[...] = jnp.zeros_like(o_ref)


def _moe(hs, plan, layer, rwt_bf16, w_gate, w_up, w_down):
    ea, eb, flags, _ = plan
    n_tiles = ea.shape[0]
    wspec = lambda shape, which: pl.BlockSpec(
        (None, None) + shape, (lambda t, ea, eb, fl: (layer, ea[t], 0, 0)) if which == 0
        else (lambda t, ea, eb, fl: (layer, eb[t], 0, 0)))
    up = (D_MODEL, D_FF_EXPERT)
    dn = (D_FF_EXPERT, D_MODEL)
    tile = pl.BlockSpec((MOE_TM * TOK_ROWS, LANES), lambda t, ea, eb, fl: (t, 0))
    grid_spec = pltpu.PrefetchScalarGridSpec(
        num_scalar_prefetch=3,
        grid=(n_tiles,),
        in_specs=[
            tile,
            pl.BlockSpec((N_EXPERTS, D_MODEL), lambda t, ea, eb, fl: (0, 0)),
            wspec(up, 0), wspec(up, 0), wspec(dn, 0), wspec(up, 1), wspec(up, 1), wspec(dn, 1),
        ],
        out_specs=tile,
        scratch_shapes=[pltpu.VMEM(up, BF16), pltpu.VMEM(up, BF16), pltpu.VMEM(dn, BF16),
                        pltpu.VMEM(up, BF16), pltpu.VMEM(up, BF16), pltpu.VMEM(dn, BF16)],
    )
    return pl.pallas_call(
        _moe_kernel,
        grid_spec=grid_spec,
        out_shape=jax.ShapeDtypeStruct((n_tiles * MOE_TM * TOK_ROWS, LANES), F32),
        compiler_params=_cparams(("arbitrary",)),
        name="moe",
    )(ea, eb, flags, hs, rwt_bf16, w_gate, w_up, w_down, w_gate, w_up, w_down)


def _fin_kernel(packed_ref, rs_ref, x_ref, ys_ref, g_ref, lnw_ref, lnb_ref, *rest, tm, group_tiles):
    o_refs = rest[:len(group_tiles)]
    ybuf, sem = rest[len(group_tiles):]
    i = pl.program_id(0)
    n_i = pl.num_programs(0)
    slot = i % 2

    def gather(tile, buf_slot):
        def body(r, _):
            d = _class_row(packed_ref[tile * tm + r], rs_ref)
            pltpu.make_async_copy(ys_ref.at[pl.ds(pl.multiple_of(d * TOK_ROWS, TOK_ROWS), TOK_ROWS)],
                                  ybuf.at[buf_slot, pl.ds(pl.multiple_of(r * TOK_ROWS, TOK_ROWS), TOK_ROWS)],
                                  sem.at[buf_slot]).start()
            return 0
        lax.fori_loop(0, tm, body, 0, unroll=8)

    @pl.when(i == 0)
    def _():
        gather(0, 0)

    @pl.when(i + 1 < n_i)
    def _():
        gather(jnp.minimum(i + 1, n_i - 1), 1 - slot)

    pltpu.make_async_copy(ys_ref.at[pl.ds(0, tm * TOK_ROWS)], ybuf.at[slot], sem.at[slot]).wait()
    y = _layer_norm(ALPHA * x_ref[...] + g_ref[...] * _load_token_tiles(ybuf, tm, slot), lnw_ref[...], lnb_ref[...])
    first = 0
    for o_ref, nt in zip(o_refs, group_tiles):
        @pl.when(jnp.logical_and(i >= first, i < first + nt))
        def _(o_ref=o_ref):
            o_ref[...] = y
        first += nt


def _fin(x1, ys, packed, row_start, mod, layer, ln_w, ln_b, cond_row, group_rows, tm=256):
    group_tiles = tuple(n // tm for n in group_rows)
    row = pl.BlockSpec((tm, D_MODEL), lambda i, *_: (i, 0))
    vec = pl.BlockSpec((1, D_MODEL), lambda i, *_: (0, 0))
    out_specs, first = [], 0
    for nt in group_tiles:
        out_specs.append(_group_spec(D_MODEL, tm, first, nt))
        first += nt
    grid_spec = pltpu.PrefetchScalarGridSpec(
        num_scalar_prefetch=2,
        grid=(sum(group_tiles),),
        in_specs=[row, pl.BlockSpec(memory_space=pl.ANY), _mod_spec(layer, 5, lambda i: cond_row(i, tm)), vec, vec],
        out_specs=out_specs,
        scratch_shapes=[pltpu.VMEM((2, tm * TOK_ROWS, LANES), F32), pltpu.SemaphoreType.DMA((2,))],
    )
    return pl.pallas_call(
        functools.partial(_fin_kernel, tm=tm, group_tiles=group_tiles),
        grid_spec=grid_spec,
        out_shape=[jax.ShapeDtypeStruct((n, D_MODEL), F32) for n in group_rows],
        compiler_params=_cparams(("arbitrary",)),
        name="post_moe",
    )(packed, row_start, x1, ys, mod, ln_w.reshape(1, -1), ln_b.reshape(1, -1))


def _filt_kernel(z_ref, w1_ref, b1_ref, w2_ref, b2_ref, fr_ref, w3_ref, dl_ref, o_ref):
    i = pl.program_id(0)
    z = z_ref[...]
    fr = fr_ref[...]
    a = jnp.sin(fr * (jnp.dot(z, w1_ref[...], precision=HIGHEST, preferred_element_type=F32) + b1_ref[...]))
    a = jnp.sin(fr * (jnp.dot(a, w2_ref[...], precision=HIGHEST, preferred_element_type=F32) + b2_ref[...]))
    filt = jnp.dot(a, w3_ref[...], precision=HIGHEST, preferred_element_type=F32)
    window = jnp.exp(-z[:, 0:1] * dl_ref[...]) + HY_SHIFT
    rows = i * z.shape[0] + lax.broadcasted_iota(jnp.int32, filt.shape, 0)
    o_ref[...] = jnp.where(rows == 0, 0.0, filt * window)


def _hyena_taps(seq_len, w1, b1, w2, b2, w3, freq):
    L = seq_len
    t = jnp.linspace(0.0, 1.0, L, dtype=F32)[:, None]
    bands = jnp.linspace(1e-4, HY_BANDS - 1, HY_BANDS, dtype=F32)
    ang = 2.0 * math.pi * bands[None, :] * jnp.arange(L, dtype=F32)[:, None] / L
    z = jnp.concatenate([t, jnp.cos(ang), -jnp.sin(ang)], axis=-1)
    offs = np.minimum(np.abs(np.arange(2 * L) - L), L - 1)
    emb = z.shape[1]
    z2 = jnp.zeros((2 * L, LANES), F32).at[:, :emb].set(z[offs])
    pad_c = lambda a: jnp.zeros((a.shape[0], LANES), F32).at[:, :a.shape[1]].set(a)
    pad_r = lambda a: jnp.zeros((LANES, a.shape[1]), F32).at[:a.shape[0], :].set(a)
    w1p = pad_r(pad_c(w1))
    w2p = pad_r(pad_c(w2))
    w3p = pad_r(w3)
    b1p, b2p, frp = pad_c(b1[None, :]), pad_c(b2[None, :]), pad_c(freq[None, :])
    deltas = jnp.abs(jnp.linspace(math.log(HY_DECAY_TARGET) / HY_SLOW, math.log(HY_DECAY_TARGET) / HY_FAST,
                                  D_MODEL, dtype=F32))[None, :]
    rb = min(512, L)
    cbf = D_MODEL
    ncb = D_MODEL // cbf
    nrb_back = L // rb
    sq = pl.BlockSpec((LANES, LANES), lambda i, j: (0, 0))
    vec = pl.BlockSpec((1, LANES), lambda i, j: (0, 0))
    return pl.pallas_call(
        _filt_kernel,
        grid=(2 * L // rb, ncb),
        in_specs=[
            pl.BlockSpec((rb, LANES), lambda i, j: (i, 0)),
            sq, vec, sq, vec, vec,
            pl.BlockSpec((LANES, cbf), lambda i, j: (0, jnp.where(i < nrb_back, ncb + j, j))),
            pl.BlockSpec((1, cbf), lambda i, j: (0, j)),
        ],
        out_specs=pl.BlockSpec((rb, cbf), lambda i, j: (i, j)),
        out_shape=jax.ShapeDtypeStruct((2 * L, D_MODEL), F32),
        compiler_params=_cparams(("arbitrary", "arbitrary")),
        name="hyena_taps",
    )(z2, w1p, b1p, w2p, b2p, frp, w3p, deltas)


def _dft_mats(cb):
    n = 2 * cb
    m = np.arange(cb)
    f = np.arange(cb)
    ang = 2.0 * np.pi * ((f[:, None] * m[None, :]) % n) / n
    fwd = np.concatenate([np.cos(ang), -np.sin(ang)], axis=0)
    fwd[cb, :] = np.where(m % 2 == 0, 1.0, -1.0)
    coef = np.where(f == 0, 1.0, 2.0)[None, :] / n
    inv = np.concatenate([coef * np.cos(ang.T), -coef * np.sin(ang.T)], axis=1)
    inv[:, cb] = np.where(m % 2 == 0, 1.0, -1.0) / n
    return fwd.astype(np.float32), inv.astype(np.float32)


def _hconv_kernel(x0_ref, x1_ref, v_ref, cw0_ref, cw1_ref, cw2_ref, cb0_ref, cb1_ref, cb2_ref,
                  taps_ref, fb_ref, fwd_ref, inv_ref, o_ref,
                  hs_ref, stage_ref, w32_ref, w_ref, x0c_ref, u_ref, y_ref, *, seq_len, blk, n_seq):
    L = seq_len
    nb = L // blk
    cw = x0_ref.shape[1]
    RC = 256
    bi = pl.program_id(1)
    row0 = lax.broadcasted_iota(jnp.int32, (RC, cw), 0) == 0

    @pl.when(bi == 0)
    def _():
        sign = jnp.where(lax.broadcasted_iota(jnp.int32, (2 * blk, 1), 0) % 2 == 0, 1.0, -1.0)
        for k in range(2 * nb - 1):
            base = L + (k - nb + 1) * blk
            head = taps_ref[base:base + blk, :].astype(BF16)
            tail = taps_ref[base - blk:base, :]
            tail = jnp.where(lax.broadcasted_iota(jnp.int32, tail.shape, 0) == 0, 0.0, tail).astype(BF16)
            hs_ref[k] = (jnp.dot(fwd_ref[...], head, preferred_element_type=F32)
                         + sign * jnp.dot(fwd_ref[...], tail, preferred_element_type=F32))

    zeros8 = jnp.zeros((8, cw), F32)
    for s in range(n_seq):
        r0 = s * L

        def short_conv(src_ref, cw_ref, cb_ref, store):
            stage_ref[0:8, :] = zeros8
            stage_ref[8 + L:16 + L, :] = zeros8
            for c in range(L // RC):
                stage_ref[8 + c * RC:8 + (c + 1) * RC, :] = src_ref[r0 + c * RC:r0 + (c + 1) * RC, :].astype(F32)
            w = cw_ref[...]
            for c in range(L // RC):
                lo = stage_ref[7 + c * RC:7 + (c + 1) * RC, :]
                mid = stage_ref[8 + c * RC:8 + (c + 1) * RC, :]
                hi = stage_ref[9 + c * RC:9 + (c + 1) * RC, :]
                store(c, lo * w[0:1, :] + mid * w[1:2, :] + hi * w[2:3, :] + cb_ref[...])

        def st_x1(c, val):
            w32_ref[c * RC:(c + 1) * RC, :] = val

        def st_v(c, val):
            w_ref[c * RC:(c + 1) * RC, :] = (w32_ref[c * RC:(c + 1) * RC, :] * val).astype(BF16)

        def st_x0(c, val):
            x0c_ref[c * RC:(c + 1) * RC, :] = val.astype(BF16)

        short_conv(x1_ref, cw1_ref, cb1_ref, st_x1)
        short_conv(v_ref, cw2_ref, cb2_ref, st_v)
        short_conv(x0_ref, cw0_ref, cb0_ref, st_x0)

        for j in range(nb):
            u_ref[j] = jnp.dot(fwd_ref[...], w_ref[j * blk:(j + 1) * blk, :], preferred_element_type=F32)

        for i in range(nb):
            for c in range(blk // RC):
                re = None
                im = None
                for j in range(nb):
                    k = i - j + nb - 1
                    a = u_ref[j, c * RC:(c + 1) * RC, :]
                    b = u_ref[j, blk + c * RC:blk + (c + 1) * RC, :]
                    hr = hs_ref[k, c * RC:(c + 1) * RC, :]
                    hi = hs_ref[k, blk + c * RC:blk + (c + 1) * RC, :]
                    bb = b * hi
                    if c == 0:
                        t_re = a * hr - jnp.where(row0, 0.0, bb)
                        t_im = jnp.where(row0, bb, a * hi + b * hr)
                    else:
                        t_re = a * hr - bb
                        t_im = a * hi + b * hr
                    re = t_re if re is None else re + t_re
                    im = t_im if im is None else im + t_im
                y_ref[c * RC:(c + 1) * RC, :] = re.astype(BF16)
                y_ref[blk + c * RC:blk + (c + 1) * RC, :] = im.astype(BF16)
            conv = jnp.dot(inv_ref[...], y_ref[...], preferred_element_type=F32)
            sl = slice(i * blk, (i + 1) * blk)
            z = x0c_ref[sl, :].astype(F32) * (conv + w_ref[sl, :].astype(F32) * fb_ref[...])
            o_ref[r0 + i * blk:r0 + (i + 1) * blk, :] = z.astype(BF16)


def _hyena_conv(u, conv_w, conv_b, taps, filt_bias, batch, seq_len, blk, n_seq, cw=256):
    T = u.shape[0]
    L = seq_len
    nb = L // blk
    ncw = D_MODEL // cw
    fwd_np, inv_np = _dft_mats(blk)
    fwd = jnp.asarray(fwd_np).astype(BF16)
    inv = jnp.asarray(inv_np).astype(BF16)
    rows = n_seq * L
    sec = lambda s: pl.BlockSpec((rows, cw), lambda c, b: (b, s * ncw + c))
    cws = lambda s: pl.BlockSpec((3, cw), lambda c, b: (0, s * ncw + c))
    cbs = lambda s: pl.BlockSpec((1, cw), lambda c, b: (0, s * ncw + c))
    return pl.pallas_call(
        functools.partial(_hconv_kernel, seq_len=L, blk=blk, n_seq=n_seq),
        grid=(ncw, batch // n_seq),
        in_specs=[sec(0), sec(1), sec(2), cws(0), cws(1), cws(2), cbs(0), cbs(1), cbs(2),
                  pl.BlockSpec((2 * L, cw), lambda c, b: (0, c)),
                  pl.BlockSpec((1, cw), lambda c, b: (0, c)),
                  pl.BlockSpec((2 * blk, blk), lambda c, b: (0, 0)),
                  pl.BlockSpec((blk, 2 * blk), lambda c, b: (0, 0))],
        out_specs=pl.BlockSpec((rows, cw), lambda c, b: (b, c)),
        out_shape=jax.ShapeDtypeStruct((T, D_MODEL), BF16),
        scratch_shapes=[
            pltpu.VMEM((2 * nb - 1, 2 * blk, cw), F32),
            pltpu.VMEM((L + 16, cw), F32),
            pltpu.VMEM((L, cw), F32),
            pltpu.VMEM((L, cw), BF16),
            pltpu.VMEM((L, cw), BF16),
            pltpu.VMEM((nb, 2 * blk, cw), F32),
            pltpu.VMEM((2 * blk, cw), BF16),
        ],
        compiler_params=_cparams(("arbitrary", "arbitrary")),
        name="hyena_conv",
    )(u, u, u, conv_w, conv_w, conv_w, conv_b.reshape(1, -1), conv_b.reshape(1, -1), conv_b.reshape(1, -1),
      taps, filt_bias.reshape(1, -1), fwd, inv)


def kernel(x_prompt, x_sample, cache_diff_k, cache_diff_v, state_ret_fwd, state_ret_bwd, c, c_ctx, ada_w, ada_b, ln_w, ln_b, ev_w_in, ev_w_out, ret_decay_fwd, ret_decay_bwd, ret_gn_w, diff_lambda, diff_subln_w, hy_w_in, hy_conv_w, hy_conv_b, hy_ffn_w1, hy_ffn_b1, hy_ffn_w2, hy_ffn_b2, hy_ffn_w3, hy_freq, hy_filter_bias, hy_w_out, router_w, router_bias, moe_w_gate, moe_w_up, moe_w_down):
    B, S, D = x_prompt.shape
    DB, DS, _ = x_sample.shape
    PAST = cache_diff_k.shape[2]
    TP, TS = B * S, DB * DS
    T_ALL = TP + TS
    assert D == D_MODEL and 1 + DB <= COND_ROWS and T_ALL < 65536

    cond8 = jnp.zeros((COND_ROWS, D), F32).at[0].set(c_ctx).at[1:1 + DB].set(c)
    mod = _ada_mod(cond8, ada_w, ada_b).reshape(DEPTH * COND_ROWS * N_MOD, 1, D)

    tm = 512
    groups = [
        dict(x=x_prompt.reshape(TP, D), batch=B, seq=S, off=0, cond_row=lambda i, tm: 0),
        dict(x=x_sample.reshape(TS, D), batch=DB, seq=DS, off=TP, cond_row=lambda i, tm: 1 + (i * tm) // DS),
    ]

    def cond_row_all(i, tm):
        return jnp.where(i * tm < TP, 0, 1 + (i * tm - TP) // DS)

    rw_pad = jnp.zeros((D, LANES), F32).at[:, :N_EXPERTS].set(router_w.astype(F32))
    rw_hi = rw_pad.astype(BF16)
    rw_lo = (rw_pad - rw_hi.astype(F32)).astype(BF16)
    rwt = router_w.T.astype(BF16)
    rbias = router_bias.astype(F32)
    n_tiles = T_ALL // MOE_TM + N_CLASSES
    outs = {}

    for l in range(DEPTH):
        mixes = []
        if l % 2 == 0:
            e = l // 2
            w_in = ev_w_in[e].astype(BF16)
            w_out = ev_w_out[e].astype(BF16)
            lg = jnp.stack([jnp.log1p(-jnp.exp2(ret_decay_fwd[e].astype(F32))),
                            jnp.log1p(-jnp.exp2(ret_decay_bwd[e].astype(F32)))])
            lam_init = 0.8 - 0.6 * math.exp(-0.3 * l)
            lq1, lk1, lq2, lk2 = diff_lambda[e].astype(F32)
            lam = (jnp.exp(jnp.sum(lq1 * lk1)) - jnp.exp(jnp.sum(lq2 * lk2)) + lam_init).reshape(1)
            kscale = RET_DK ** -0.5
            for gi, g in enumerate(groups):
                crow = functools.partial(g["cond_row"], tm=tm)
                if gi == 0:
                    secs = (("none", 1.0, False), ("none", kscale, False), ("none", 1.0, False),
                            ("none", 1.0, False), ("none", 1.0, False), ("none", 1.0, True),
                            ("none", 1.0, True))
                    proj, kd, vd = _in_proj(g["x"], mod, l, w_in, secs, crow, tm, tm=tm)
                    outs.setdefault("kd", []).append(kd.reshape(B, 1, S, H_DIFF, 2 * DIFF_QK))
                    outs.setdefault("vd", []).append(vd.reshape(B, 1, S, H_DIFF, DIFF_V))
                    ret, sf, sb = _retention(proj, lg, ret_gn_w[e], g["batch"], g["seq"], write_state=True)
                    outs.setdefault("sf", []).append(sf.reshape(B, 1, H_RET, RET_DK, RET_DV))
                    outs.setdefault("sb", []).append(sb.reshape(B, 1, H_RET, RET_DK, RET_DV))
                    att = _diff_attention(proj, lam, diff_subln_w[e], lam_init, g["batch"], g["seq"],
                                          tq=min(256, g["seq"]))
                else:
                    secs = (("ret", 1.0, False), ("ret", kscale, False), ("none", 1.0, False),
                            ("none", 1.0, False), ("diff", 1.0, False), ("diff", 1.0, False),
                            ("none", 1.0, False))
                    tabs = {"ret": _rope_tables(g["seq"], RET_DK), "diff": _rope_tables(g["seq"], DIFF_QK)}
                    (proj,) = _in_proj(g["x"], mod, l, w_in, secs, crow, g["seq"], rope_tabs=tabs, tm=tm)
                    ret, = _retention(proj, lg, ret_gn_w[e], g["batch"], g["seq"],
                                      states=(state_ret_fwd[:, e], state_ret_bwd[:, e]))
                    ctx = (cache_diff_k[:, e].reshape(DB, PAST, H_DIFF * 2 * DIFF_QK),
                           cache_diff_v[:, e].reshape(DB, PAST, H_DIFF * DIFF_V))
                    att = _diff_attention(proj, lam, diff_subln_w[e], lam_init, g["batch"], g["seq"], ctx=ctx)
                mixes.append((ret, att))
        else:
            o = l // 2
            w_in = hy_w_in[o].astype(BF16)
            w_out = hy_w_out[o].astype(BF16)
            secs = (("none", 1.0, False),) * (3 * D // SEC)
            for gi, g in enumerate(groups):
                crow = functools.partial(g["cond_row"], tm=tm)
                (u,) = _in_proj(g["x"], mod, l, w_in, secs, crow, tm, tm=tm)
                taps = _hyena_taps(g["seq"], hy_ffn_w1[o], hy_ffn_b1[o], hy_ffn_w2[o], hy_ffn_b2[o],
                                   hy_ffn_w3[o], hy_freq[o])
                blk = min(g["seq"], 1024)
                n_seq = max(1, 1024 // g["seq"])
                z = _hyena_conv(u, hy_conv_w[o], hy_conv_b[o], taps, hy_filter_bias[o], g["batch"], g["seq"],
                                blk, n_seq)
                mixes.append((z,))

        x1_all, h2_all, info_all, counts = _post_mix(mixes, [g["x"] for g in groups], w_out, mod, l, ln_w[l, 0],
                                                     ln_b[l, 0], rw_hi, rw_lo, rbias, cond_row_all, tm)
        packed = info_all[0]
        counts_i = counts[:, 0].astype(jnp.int32)
        plan = _moe_plan(counts_i, n_tiles)
        hs = _dispatch(h2_all, packed, plan, counts_i, n_tiles)
        ys = _moe(hs, plan, l, rwt, moe_w_gate, moe_w_up, moe_w_down)
        xs = _fin(x1_all, ys, packed, plan[3], mod, l, ln_w[l, 1], ln_b[l, 1], cond_row_all, (TP, TS))
        for g, x in zip(groups, xs):
            g["x"] = x

    y_prompt = groups[0]["x"].reshape(B, S, D)
    y_sample = groups[1]["x"].reshape(DB, DS, D)
    cat = lambda xs: xs[0] if len(xs) == 1 else jnp.concatenate(xs, axis=1)
    return (y_prompt, y_sample, cat(outs["kd"]), cat(outs["vd"]), cat(outs["sf"]), cat(outs["sb"]))
```

```python
import functools
import math

import numpy as np
import jax
import jax.numpy as jnp
from jax import lax
from jax.experimental import pallas as pl
from jax.experimental.pallas import tpu as pltpu

F32 = jnp.float32
BF16 = jnp.bfloat16
HIGHEST = lax.Precision.HIGHEST

D_MODEL = 1024
DEPTH = 2
GRID_W = 64
H_RET = 4
RET_DK = 128
RET_DV = 128
RET_CHUNK = 128
H_DIFF = 4
DIFF_QK = 64
DIFF_V = 128
ROPE_BASE = 10000.0
HY_BANDS = 16
HY_FH = 64
HY_DECAY_TARGET = 1e-2
HY_FAST = 0.3
HY_SLOW = 1.5
HY_SHIFT = 0.05
N_EXPERTS = 16
N_GROUPS = 4
GROUP_SIZE = N_EXPERTS // N_GROUPS
D_FF_EXPERT = 512
ALPHA = (2 * DEPTH) ** 0.25
LN_EPS = 1e-5

LANES = 128
SEC = 512
COND_ROWS = 8
N_MOD = 6
VMEM_LIMIT = 50 * 1024 * 1024

PAIR_SLOTS = ((0, 1), (2, 1), (2, 3), (0, 3), (0, 2), (1, 3))
N_PAIRS = len(PAIR_SLOTS)
N_CLASSES = N_GROUPS * N_PAIRS
MOE_TM = 256
MOE_SHIFT = MOE_TM.bit_length() - 1
TOK_ROWS = D_MODEL // LANES


def _cparams(sem):
    return pltpu.CompilerParams(dimension_semantics=sem, vmem_limit_bytes=VMEM_LIMIT)


def _silu(x):
    return x * jax.nn.sigmoid(x)


def _store_token_tiles(ref, x):
    n = x.shape[0]
    for j in range(TOK_ROWS):
        ref[pl.ds(j, n, stride=TOK_ROWS), :] = x[:, j * LANES:(j + 1) * LANES]


def _load_token_tiles(ref, n, slot=None):
    idx = () if slot is None else (slot,)
    return jnp.concatenate([ref[idx + (pl.ds(j, n, stride=TOK_ROWS), slice(None))] for j in range(TOK_ROWS)],
                           axis=1)


def _layer_norm(x, w, b):
    mu = jnp.mean(x, axis=-1, keepdims=True)
    xc = x - mu
    var = jnp.mean(xc * xc, axis=-1, keepdims=True)
    return xc * lax.rsqrt(var + LN_EPS) * w + b


def _ada_kernel(c_ref, w_ref, b_ref, o_ref):
    c = c_ref[...]
    o_ref[...] = jnp.dot(_silu(c), w_ref[...], precision=HIGHEST, preferred_element_type=F32) + b_ref[...]


def _ada_mod(cond8, ada_w, ada_b):
    tn = 1024
    nj = ada_w.shape[2] // tn
    return pl.pallas_call(
        _ada_kernel,
        grid=(DEPTH, nj),
        in_specs=[
            pl.BlockSpec((COND_ROWS, D_MODEL), lambda l, j: (0, 0)),
            pl.BlockSpec((None, D_MODEL, tn), lambda l, j: (l, 0, j)),
            pl.BlockSpec((None, 1, tn), lambda l, j: (l, 0, j)),
        ],
        out_specs=pl.BlockSpec((None, COND_ROWS, tn), lambda l, j: (l, 0, j)),
        out_shape=jax.ShapeDtypeStruct((DEPTH, COND_ROWS, ada_w.shape[2]), F32),
        compiler_params=_cparams(("arbitrary", "arbitrary")),
        name="ada_mod",
    )(cond8, ada_w, ada_b.reshape(DEPTH, 1, -1))


def _mod_spec(layer, chunk, row_of_tile):
    def imap(i, *_):
        return ((layer * COND_ROWS + row_of_tile(i)) * N_MOD + chunk, 0, 0)
    return pl.BlockSpec((None, 1, D_MODEL), imap)


def _rope(a, tabs, quarter):
    c, sa, sb = tabs
    out = []
    for hb in range(a.shape[1] // LANES):
        blk = a[:, hb * LANES:(hb + 1) * LANES]
        up = pltpu.roll(blk, LANES - quarter, axis=1)
        dn = pltpu.roll(blk, quarter, axis=1)
        out.append(blk * c + up * sa + dn * sb)
    return jnp.concatenate(out, axis=1)


def _in_kernel(*refs, secs, n_f32_out):
    x_ref, sh_ref, sc_ref, w_ref = refs[:4]
    pos = 4
    tabs = {}
    for kind in ("ret", "diff"):
        if any(s[0] == kind for s in secs):
            tabs[kind] = tuple(r[...] for r in refs[pos:pos + 3])
            pos += 3
    o_ref = refs[pos]
    f32_refs = refs[pos + 1:]
    h = (x_ref[...] * (1.0 + sc_ref[...]) + sh_ref[...]).astype(BF16)
    k32 = 0
    for s, (kind, scale, want_f32) in enumerate(secs):
        acc = jnp.dot(h, w_ref[:, s * SEC:(s + 1) * SEC], preferred_element_type=F32)
        if scale != 1.0:
            acc = acc * scale
        if kind == "ret":
            acc = _rope(acc, tabs["ret"], RET_DK // 4)
        elif kind == "diff":
            acc = _rope(acc, tabs["diff"], DIFF_QK // 4)
        o_ref[:, s * SEC:(s + 1) * SEC] = acc.astype(BF16)
        if want_f32:
            f32_refs[k32][...] = acc
            k32 += 1
    assert k32 == n_f32_out


def _in_proj(x2d, mod, layer, w_bf16, secs, row_of_tile, seq_len, rope_tabs=None, tm=512):
    T = x2d.shape[0]
    N = w_bf16.shape[1]
    assert N == SEC * len(secs) and T % tm == 0 and seq_len % tm == 0
    tiles_per_seq = seq_len // tm
    in_specs = [
        pl.BlockSpec((tm, D_MODEL), lambda i: (i, 0)),
        _mod_spec(layer, 0, row_of_tile),
        _mod_spec(layer, 1, row_of_tile),
        pl.BlockSpec((D_MODEL, N), lambda i: (0, 0)),
    ]
    args = [x2d, mod, mod, w_bf16]
    for kind in ("ret", "diff"):
        if any(s[0] == kind for s in secs):
            for t in rope_tabs[kind]:
                in_specs.append(pl.BlockSpec((tm, LANES), lambda i: (i % tiles_per_seq, 0)))
                args.append(t)
    n_f32 = sum(1 for s in secs if s[2])
    out_shape = [jax.ShapeDtypeStruct((T, N), BF16)] + [jax.ShapeDtypeStruct((T, SEC), F32)] * n_f32
    out_specs = [pl.BlockSpec((tm, N), lambda i: (i, 0))] + [pl.BlockSpec((tm, SEC), lambda i: (i, 0))] * n_f32
    return pl.pallas_call(
        functools.partial(_in_kernel, secs=secs, n_f32_out=n_f32),
        grid=(T // tm,),
        in_specs=in_specs,
        out_specs=out_specs,
        out_shape=out_shape,
        compiler_params=_cparams(("arbitrary",)),
        name="in_proj",
    )(*args)


def _rope_tables(seq_len, d):
    half = d // 2
    quarter = half // 2
    t = jnp.arange(seq_len)
    row = (t // GRID_W).astype(F32)
    col = (t % GRID_W).astype(F32)
    inv = ROPE_BASE ** (-jnp.arange(quarter, dtype=F32) / quarter)
    ang_r = row[:, None] * inv[None, :]
    ang_c = col[:, None] * inv[None, :]
    zero = jnp.zeros_like(ang_r)
    cos = jnp.concatenate([jnp.cos(ang_r)] * 2 + [jnp.cos(ang_c)] * 2, axis=1)
    sa = jnp.concatenate([-jnp.sin(ang_r), zero, -jnp.sin(ang_c), zero], axis=1)
    sb = jnp.concatenate([zero, jnp.sin(ang_r), zero, jnp.sin(ang_c)], axis=1)
    reps = LANES // d
    return tuple(jnp.tile(a, (1, reps)) for a in (cos, sa, sb))


def _ret_kernel(*refs, n_chunks, has_state, write_state):
    lg_ref, q_ref, k_ref, v_ref, g_ref, gnw_ref = refs[:6]
    pos = 6
    if has_state:
        s0f_ref, s0b_ref = refs[pos:pos + 2]
        pos += 2
    o_ref = refs[pos]
    pos += 1
    if write_state:
        sf_ref, sb_ref = refs[pos:pos + 2]
        pos += 2
    acc_ref = refs[pos]

    C = RET_CHUNK
    hd = pl.program_id(1)
    lgf = lg_ref[0, hd]
    lgb = lg_ref[1, hd]
    ii = lax.broadcasted_iota(jnp.int32, (C, C), 0).astype(F32)
    jj = lax.broadcasted_iota(jnp.int32, (C, C), 1).astype(F32)
    rel = ii - jj
    d_f = jnp.where(rel >= 0, jnp.exp(jnp.maximum(rel, 0.0) * lgf), 0.0)
    d_b = jnp.where(rel <= 0, jnp.exp(jnp.maximum(-rel, 0.0) * lgb), 0.0)
    d_sum = d_f + d_b
    idx = lax.broadcasted_iota(jnp.int32, (C, 1), 0).astype(F32)
    xi_f = jnp.exp((idx + 1.0) * lgf)
    zeta_f = jnp.exp((C - 1.0 - idx) * lgf)
    xi_b = jnp.exp((C - idx) * lgb)
    zeta_b = jnp.exp(idx * lgb)
    one = jnp.ones((1, 1), F32)
    gc_f = jnp.exp(one * (C * lgf))
    gc_b = jnp.exp(one * (C * lgb))

    nt = (((1,), (1,)), ((), ()))
    tn = (((0,), (0,)), ((), ()))

    if has_state:
        s_f = s0f_ref[...]
        s_b = s0b_ref[...]
    else:
        s_f = jnp.zeros((RET_DK, RET_DV), F32)
        s_b = jnp.zeros((RET_DK, RET_DV), F32)

    for n in range(n_chunks):
        sl = slice(n * C, (n + 1) * C)
        qc, kc, vc = q_ref[sl, :], k_ref[sl, :], v_ref[sl, :]
        scores = lax.dot_general(qc, kc, nt, preferred_element_type=F32) * d_sum
        inner = jnp.dot(scores.astype(BF16), vc, preferred_element_type=F32)
        cross = jnp.dot((qc.astype(F32) * xi_f).astype(BF16), s_f.astype(BF16), preferred_element_type=F32)
        acc_ref[sl, :] = inner + cross
        kz = (kc.astype(F32) * zeta_f).astype(BF16)
        s_f = gc_f * s_f + lax.dot_general(kz, vc, tn, preferred_element_type=F32)

    gnw = gnw_ref[...]
    for n in reversed(range(n_chunks)):
        sl = slice(n * C, (n + 1) * C)
        qc, kc, vc = q_ref[sl, :], k_ref[sl, :], v_ref[sl, :]
        cross = jnp.dot((qc.astype(F32) * xi_b).astype(BF16), s_b.astype(BF16), preferred_element_type=F32)
        r = acc_ref[sl, :] + cross
        mu = jnp.mean(r, axis=-1, keepdims=True)
        rc = r - mu
        var = jnp.mean(rc * rc, axis=-1, keepdims=True)
        rn = rc * lax.rsqrt(var + LN_EPS) * gnw
        o_ref[sl, :] = (_silu(g_ref[sl, :].astype(F32)) * rn).astype(BF16)
        kz = (kc.astype(F32) * zeta_b).astype(BF16)
        s_b = gc_b * s_b + lax.dot_general(kz, vc, tn, preferred_element_type=F32)

    if write_state:
        sf_ref[...] = s_f
        sb_ref[...] = s_b


def _retention(proj, lg, gn_w, batch, seq_len, states=None, write_state=False):
    T = proj.shape[0]
    hsec = SEC // LANES
    blk = lambda sec: pl.BlockSpec((seq_len, LANES), lambda b, h: (b, sec * hsec + h))
    in_specs = [pl.BlockSpec(memory_space=pltpu.SMEM), blk(0), blk(1), blk(2), blk(3),
                pl.BlockSpec((1, LANES), lambda b, h: (0, h))]
    args = [lg, proj, proj, proj, proj, gn_w.reshape(1, -1)]
    if states is not None:
        st = pl.BlockSpec((None, None, RET_DK, RET_DV), lambda b, h: (b, h, 0, 0))
        in_specs += [st, st]
        args += list(states)
    out_shape = [jax.ShapeDtypeStruct((T, SEC), BF16)]
    out_specs = [pl.BlockSpec((seq_len, LANES), lambda b, h: (b, h))]
    if write_state:
        st_o = pl.BlockSpec((None, None, RET_DK, RET_DV), lambda b, h: (b, h, 0, 0))
        out_shape += [jax.ShapeDtypeStruct((batch, H_RET, RET_DK, RET_DV), F32)] * 2
        out_specs += [st_o, st_o]
    return pl.pallas_call(
        functools.partial(_ret_kernel, n_chunks=seq_len // RET_CHUNK, has_state=states is not None,
                          write_state=write_state),
        grid=(batch, H_RET),
        in_specs=in_specs,
        out_specs=out_specs,
        out_shape=out_shape,
        scratch_shapes=[pltpu.VMEM((seq_len, RET_DV), F32)],
        compiler_params=_cparams(("arbitrary", "arbitrary")),
        name="retention",
    )(*args)


def _att_kernel(*refs, has_ctx, out_scale, key_chunk):
    lam_ref, q_ref, k_ref, v_ref = refs[:4]
    pos = 4
    if has_ctx:
        ck_ref, cv_ref = refs[pos:pos + 2]
        pos += 2
    w_ref, o_ref = refs[pos:pos + 2]
    lam = lam_ref[0]
    q = q_ref[...]
    tq = q.shape[0]
    lane = lax.broadcasted_iota(jnp.int32, q.shape, 1)
    zero = jnp.zeros_like(q)
    qq = jnp.concatenate([jnp.where(lane < DIFF_QK, q, zero), jnp.where(lane >= DIFF_QK, q, zero)], axis=0)
    qq = qq * jnp.asarray(DIFF_QK ** -0.5, BF16)
    nt = (((1,), (1,)), ((), ()))
    chunks = [(k_ref, v_ref, c * key_chunk, key_chunk) for c in range(k_ref.shape[0] // key_chunk)]
    if has_ctx:
        chunks.append((ck_ref, cv_ref, 0, ck_ref.shape[0]))
    m = l = acc = None
    for kr, vr, off, n in chunks:
        kch = kr[off:off + n, :].astype(BF16)
        vch = vr[off:off + n, :].astype(BF16)
        s = lax.dot_general(qq, kch, nt, preferred_element_type=F32)
        cm = jnp.max(s, axis=-1, keepdims=True)
        m_new = cm if m is None else jnp.maximum(m, cm)
        p = jnp.exp(s - m_new)
        ps = jnp.sum(p, axis=-1, keepdims=True)
        pv = jnp.dot(p.astype(BF16), vch, preferred_element_type=F32)
        if m is None:
            l, acc = ps, pv
        else:
            alpha = jnp.exp(m - m_new)
            l = alpha * l + ps
            acc = alpha * acc + pv
        m = m_new
    o = acc / l
    att = o[:tq] - lam * o[tq:]
    att = att * lax.rsqrt(jnp.mean(att * att, axis=-1, keepdims=True) + LN_EPS)
    o_ref[...] = (att * w_ref[...] * out_scale).astype(BF16)


def _diff_attention(proj, lam, subln_w, lam_init, batch, seq_len, ctx=None, tq=256):
    T = proj.shape[0]
    hsec = SEC // LANES
    nq = seq_len // tq
    in_specs = [
        pl.BlockSpec(memory_space=pltpu.SMEM),
        pl.BlockSpec((tq, LANES), lambda b, h, i: (b * nq + i, 4 * hsec + h)),
        pl.BlockSpec((seq_len, LANES), lambda b, h, i: (b, 5 * hsec + h)),
        pl.BlockSpec((seq_len, LANES), lambda b, h, i: (b, 6 * hsec + h)),
    ]
    args = [lam, proj, proj, proj]
    if ctx is not None:
        ck, cv = ctx
        past = ck.shape[1]
        cspec = pl.BlockSpec((None, past, LANES), lambda b, h, i: (b, 0, h))
        in_specs += [cspec, cspec]
        args += [ck, cv]
    in_specs.append(pl.BlockSpec((1, LANES), lambda b, h, i: (0, 0)))
    args.append(subln_w.reshape(1, -1))
    return pl.pallas_call(
        functools.partial(_att_kernel, has_ctx=ctx is not None, out_scale=1.0 - lam_init,
                          key_chunk=min(512, seq_len)),
        grid=(batch, H_DIFF, nq),
        in_specs=in_specs,
        out_specs=pl.BlockSpec((tq, LANES), lambda b, h, i: (b * nq + i, h)),
        out_shape=jax.ShapeDtypeStruct((T, SEC), BF16),
        compiler_params=_cparams(("arbitrary", "arbitrary", "arbitrary")),
        name="diff_attention",
    )(*args)


def _route_class(lt, rb_ref):
    sel = [jax.nn.sigmoid(lt[e:e + 1, :]) + rb_ref[e] for e in range(N_EXPERTS)]
    gscore = []
    for g in range(N_GROUPS):
        mem = sel[g * GROUP_SIZE:(g + 1) * GROUP_SIZE]
        best = None
        for a in range(GROUP_SIZE):
            for b in range(a + 1, GROUP_SIZE):
                pair = mem[a] + mem[b]
                best = pair if best is None else jnp.maximum(best, pair)
        gscore.append(best)
    gbest = gscore[0]
    gidx = jnp.zeros_like(gbest)
    for g in range(1, N_GROUPS):
        upd = gscore[g] > gbest
        gidx = jnp.where(upd, float(g), gidx)
        gbest = jnp.where(upd, gscore[g], gbest)
    msel = []
    for j in range(GROUP_SIZE):
        out = sel[j]
        for g in range(1, N_GROUPS):
            out = jnp.where(gidx == float(g), sel[g * GROUP_SIZE + j], out)
        msel.append(out)
    one = jnp.ones_like(gbest)
    zero = jnp.zeros_like(gbest)
    chosen = []
    for j in range(GROUP_SIZE):
        rank = zero
        for k in range(GROUP_SIZE):
            if k < j:
                rank = rank + jnp.where(msel[k] >= msel[j], one, zero)
            elif k > j:
                rank = rank + jnp.where(msel[k] > msel[j], one, zero)
        chosen.append(jnp.where(rank < 2.0, one, zero))
    c0, c1, c2, c3 = chosen
    order = jnp.where(c0 * c1 > 0, 0.0, jnp.where(c1 * c2 > 0, 1.0, jnp.where(c2 * c3 > 0, 2.0,
            jnp.where(c0 * c3 > 0, 3.0, jnp.where(c0 * c2 > 0, 4.0, 5.0)))))
    return gidx * float(N_PAIRS) + order


def _post_kernel(*refs, n_mix, group_tiles):
    n_groups = len(group_tiles)
    per_group = n_mix + 1
    group_refs = [refs[g * per_group:(g + 1) * per_group] for g in range(n_groups)]
    refs = refs[n_groups * per_group:]
    (w_ref, g_ref, sh_ref, sc_ref, lnw_ref, lnb_ref, rwh_ref, rwl_ref, rb_ref, tri_ref,
     x1_ref, h2_ref, info_ref, cout_ref, cnt_ref, pre_ref) = refs
    i = pl.program_id(0)

    @pl.when(i == 0)
    def _():
        cnt_ref[...] = jnp.zeros_like(cnt_ref)

    first = 0
    for g in range(n_groups):
        @pl.when(jnp.logical_and(i >= first, i < first + group_tiles[g]))
        def _(g=g):
            out = None
            off = 0
            for m_ref in group_refs[g][:n_mix]:
                width = m_ref.shape[1]
                part = jnp.dot(m_ref[...], w_ref[off:off + width, :], preferred_element_type=F32)
                out = part if out is None else out + part
                off += width
            pre_ref[...] = ALPHA * group_refs[g][n_mix][...] + g_ref[...] * out
        first += group_tiles[g]

    x1 = _layer_norm(pre_ref[...], lnw_ref[...], lnb_ref[...])
    x1_ref[...] = x1
    h2 = x1 * (1.0 + sc_ref[...]) + sh_ref[...]
    _store_token_tiles(h2_ref, h2)
    h_hi = h2.astype(BF16)
    h_lo = (h2 - h_hi.astype(F32)).astype(BF16)
    logits = (jnp.dot(h_hi, rwh_ref[...], preferred_element_type=F32)
              + jnp.dot(h_lo, rwh_ref[...], preferred_element_type=F32)
              + jnp.dot(h_hi, rwl_ref[...], preferred_element_type=F32))
    cls = _route_class(logits.T, rb_ref)
    tm = h2.shape[0]
    crow = lax.broadcasted_iota(jnp.int32, (32, tm), 0).astype(F32)
    onehot = jnp.where(crow == cls, 1.0, 0.0)
    prefix = jnp.dot(onehot.astype(BF16), tri_ref[...], preferred_element_type=F32)
    base = cnt_ref[:, 0:1]
    rank = jnp.sum(onehot * (prefix - 1.0 + base), axis=0, keepdims=True)
    cnt_ref[...] = cnt_ref[...] + jnp.sum(onehot, axis=1, keepdims=True)
    packed = cls.astype(jnp.int32) * 65536 + rank.astype(jnp.int32)
    row = lax.broadcasted_iota(jnp.int32, (8, tm), 0)
    info_ref[...] = jnp.where(row == 0, packed, 0)

    @pl.when(i == pl.num_programs(0) - 1)
    def _():
        cout_ref[...] = cnt_ref[...]


def _tri(tm):
    return jnp.asarray(np.triu(np.ones((tm, tm), np.float32))).astype(BF16)


def _group_spec(width, tm, first_tile, n_tiles):
    return pl.BlockSpec((tm, width), lambda i, *_: (jnp.clip(i - first_tile, 0, n_tiles - 1), 0))


def _post_mix(group_mixes, group_x, w_out_bf16, mod, layer, ln_w, ln_b, rw_hi, rw_lo, router_bias, cond_row, tm):
    group_tiles = tuple(x.shape[0] // tm for x in group_x)
    n_tiles = sum(group_tiles)
    total_rows = n_tiles * tm
    row = pl.BlockSpec((tm, D_MODEL), lambda i: (i, 0))
    vec = pl.BlockSpec((1, D_MODEL), lambda i: (0, 0))
    cnt = pl.BlockSpec((32, LANES), lambda i: (0, 0))
    in_specs, args = [], []
    first = 0
    for mixes, x, nt in zip(group_mixes, group_x, group_tiles):
        for m in mixes:
            in_specs.append(_group_spec(m.shape[1], tm, first, nt))
            args.append(m)
        in_specs.append(_group_spec(D_MODEL, tm, first, nt))
        args.append(x)
        first += nt
    in_specs += [
        pl.BlockSpec((D_MODEL, D_MODEL), lambda i: (0, 0)),
        _mod_spec(layer, 2, lambda i: cond_row(i, tm)), _mod_spec(layer, 3, lambda i: cond_row(i, tm)),
        _mod_spec(layer, 4, lambda i: cond_row(i, tm)),
        vec, vec,
        pl.BlockSpec((D_MODEL, LANES), lambda i: (0, 0)),
        pl.BlockSpec((D_MODEL, LANES), lambda i: (0, 0)),
        pl.BlockSpec(memory_space=pltpu.SMEM),
        pl.BlockSpec((tm, tm), lambda i: (0, 0)),
    ]
    args += [w_out_bf16, mod, mod, mod, ln_w.reshape(1, -1), ln_b.reshape(1, -1), rw_hi, rw_lo, router_bias,
             _tri(tm)]
    return pl.pallas_call(
        functools.partial(_post_kernel, n_mix=len(group_mixes[0]), group_tiles=group_tiles),
        grid=(n_tiles,),
        in_specs=in_specs,
        out_specs=[row, pl.BlockSpec((tm * TOK_ROWS, LANES), lambda i: (i, 0)),
                   pl.BlockSpec((8, tm), lambda i: (0, i)), cnt],
        out_shape=[jax.ShapeDtypeStruct((total_rows, D_MODEL), F32),
                   jax.ShapeDtypeStruct((total_rows * TOK_ROWS, LANES), F32),
                   jax.ShapeDtypeStruct((8, total_rows), jnp.int32), jax.ShapeDtypeStruct((32, LANES), F32)],
        scratch_shapes=[pltpu.VMEM((32, LANES), F32), pltpu.VMEM((tm, D_MODEL), F32)],
        compiler_params=_cparams(("arbitrary",)),
        name="post_mix",
    )(*args)


def _plan_kernel(cnt_ref, ea_ref, eb_ref, fl_ref, rs_ref, *, n_tiles):
    start = jnp.int32(0)
    prev_a = jnp.int32(-1)
    prev_b = jnp.int32(-1)
    for c in range(N_CLASSES):
        n = cnt_ref[c]
        tiles = lax.shift_right_logical(n + (MOE_TM - 1), MOE_SHIFT)
        row0 = start * MOE_TM
        rs_ref[c] = row0
        g, pr = divmod(c, N_PAIRS)
        a = g * GROUP_SIZE + PAIR_SLOTS[pr][0]
        b = g * GROUP_SIZE + PAIR_SLOTS[pr][1]
        first = 1 + 4 * (prev_a != a).astype(jnp.int32) + 8 * (prev_b != b).astype(jnp.int32)

        def tile_body(k, _, start=start, a=a, b=b, first=first):
            t = start + k
            ea_ref[t] = a
            eb_ref[t] = b
            fl_ref[t] = jnp.where(k == 0, first, 1)
            return 0

        lax.fori_loop(0, tiles, tile_body, 0)
        has = tiles > 0
        prev_a = jnp.where(has, a, prev_a)
        prev_b = jnp.where(has, b, prev_b)
        start = start + tiles
    for c in range(N_CLASSES, 31):
        rs_ref[c] = 0
    rs_ref[31] = start

    def idle_body(t, _):
        ea_ref[t] = prev_a
        eb_ref[t] = prev_b
        fl_ref[t] = 0
        return 0

    lax.fori_loop(start, n_tiles, idle_body, 0)


def _moe_plan(counts, n_tiles):
    smem = pl.BlockSpec(memory_space=pltpu.SMEM)
    i32 = lambda n: jax.ShapeDtypeStruct((n,), jnp.int32)
    return pl.pallas_call(
        functools.partial(_plan_kernel, n_tiles=n_tiles),
        in_specs=[smem],
        out_specs=[smem] * 4,
        out_shape=[i32(n_tiles), i32(n_tiles), i32(n_tiles), i32(32)],
        name="moe_plan",
    )(counts)


def _class_row(packed, rs_ref):
    return rs_ref[lax.shift_right_logical(packed, 16)] + (packed & 0xFFFF)


def _dispatch_kernel(packed_ref, rs_ref, cnt_ref, h_ref, hs_ref, sem, fill_sem, *, tm, n_tiles):
    i = pl.program_id(0)
    used = rs_ref[31]

    def tile_rows(row, n=1):
        return pl.ds(pl.multiple_of(row * TOK_ROWS, TOK_ROWS), n * TOK_ROWS)

    def fill_copy(row):
        return pltpu.make_async_copy(h_ref.at[tile_rows(0)], hs_ref.at[tile_rows(row)], fill_sem)

    def idle_copy(t):
        return pltpu.make_async_copy(h_ref.at[tile_rows(0, MOE_TM)], hs_ref.at[tile_rows(t * MOE_TM, MOE_TM)],
                                     fill_sem)

    def class_pad(c):
        n = cnt_ref[c]
        first = rs_ref[c] + n
        last = rs_ref[c] + lax.shift_left(lax.shift_right_logical(n + (MOE_TM - 1), MOE_SHIFT), MOE_SHIFT)
        return first, last

    @pl.when(i == 0)
    def _():
        for c in range(N_CLASSES):
            first, last = class_pad(c)
            lax.fori_loop(first, last, lambda r, _: (fill_copy(r).start(), 0)[1], 0)
        lax.fori_loop(used, n_tiles, lambda t, _: (idle_copy(t).start(), 0)[1], 0)

    def body(r8, _):
        for j in range(8):
            r = r8 * 8 + j
            row = _class_row(packed_ref[i * tm + r], rs_ref)
            pltpu.make_async_copy(h_ref.at[tile_rows(r)], hs_ref.at[tile_rows(row)], sem).start(priority=j % 2)
        return 0

    lax.fori_loop(0, tm // 8, body, 0)
    pltpu.make_async_copy(h_ref, hs_ref.at[tile_rows(0, tm)], sem).wait()

    @pl.when(i == 0)
    def _():
        for c in range(N_CLASSES):
            first, last = class_pad(c)
            lax.fori_loop(first, last, lambda r, _: (fill_copy(r).wait(), 0)[1], 0)
        lax.fori_loop(used, n_tiles, lambda t, _: (idle_copy(t).wait(), 0)[1], 0)


def _dispatch(h2_tiles, packed, plan, counts, n_tiles, tm=1024):
    T = packed.shape[0]
    assert tm >= MOE_TM and T % tm == 0
    grid_spec = pltpu.PrefetchScalarGridSpec(
        num_scalar_prefetch=3,
        grid=(T // tm,),
        in_specs=[pl.BlockSpec((tm * TOK_ROWS, LANES), lambda i, *_: (i, 0))],
        out_specs=pl.BlockSpec(memory_space=pl.ANY),
        scratch_shapes=[pltpu.SemaphoreType.DMA(()), pltpu.SemaphoreType.DMA(())],
    )
    return pl.pallas_call(
        functools.partial(_dispatch_kernel, tm=tm, n_tiles=n_tiles),
        grid_spec=grid_spec,
        out_shape=jax.ShapeDtypeStruct((n_tiles * MOE_TM * TOK_ROWS, LANES), F32),
        compiler_params=_cparams(("arbitrary",)),
        name="moe_dispatch",
    )(packed, plan[3], counts, h2_tiles)


def _moe_kernel(ea_ref, eb_ref, fl_ref, h_ref, rwt_ref, wga, wua, wda, wgb, wub, wdb, o_ref,
                sga, sua, sda, sgb, sub, sdb):
    t = pl.program_id(0)
    flags = fl_ref[t]
    valid = (flags & 1) != 0

    @pl.when((flags & 4) != 0)
    def _():
        sga[...] = wga[...].astype(BF16)
        sua[...] = wua[...].astype(BF16)
        sda[...] = wda[...].astype(BF16)

    @pl.when((flags & 8) != 0)
    def _():
        sgb[...] = wgb[...].astype(BF16)
        sub[...] = wub[...].astype(BF16)
        sdb[...] = wdb[...].astype(BF16)

    @pl.when(valid)
    def _():
        h = _load_token_tiles(h_ref, MOE_TM).astype(BF16)
        nt = (((1,), (1,)), ((), ()))
        score = jax.nn.sigmoid(lax.dot_general(h, rwt_ref[...], nt, preferred_element_type=F32))
        lane = lax.broadcasted_iota(jnp.int32, score.shape, 1)
        s_a = jnp.sum(jnp.where(lane == ea_ref[t], score, 0.0), axis=1, keepdims=True)
        s_b = jnp.sum(jnp.where(lane == eb_ref[t], score, 0.0), axis=1, keepdims=True)
        tot = s_a + s_b
        act_a = _silu(jnp.dot(h, sga[...], preferred_element_type=F32)) \
            * jnp.dot(h, sua[...], preferred_element_type=F32) * (s_a / tot)
        act_b = _silu(jnp.dot(h, sgb[...], preferred_element_type=F32)) \
            * jnp.dot(h, sub[...], preferred_element_type=F32) * (s_b / tot)
        _store_token_tiles(o_ref, jnp.dot(act_a.astype(BF16), sda[...], preferred_element_type=F32)
                           + jnp.dot(act_b.astype(BF16), sdb[...], preferred_element_type=F32))

    @pl.when(jnp.logical_not(valid))
    def _():
        o_ref[...] = jnp.zeros_like(o_ref)


def _moe(hs, plan, layer, rwt_bf16, w_gate, w_up, w_down):
    ea, eb, flags, _ = plan
    n_tiles = ea.shape[0]
    wspec = lambda shape, which: pl.BlockSpec(
        (None, None) + shape, (lambda t, ea, eb, fl: (layer, ea[t], 0, 0)) if which == 0
        else (lambda t, ea, eb, fl: (layer, eb[t], 0, 0)))
    up = (D_MODEL, D_FF_EXPERT)
    dn = (D_FF_EXPERT, D_MODEL)
    tile = pl.BlockSpec((MOE_TM * TOK_ROWS, LANES), lambda t, ea, eb, fl: (t, 0))
    grid_spec = pltpu.PrefetchScalarGridSpec(
        num_scalar_prefetch=3,
        grid=(n_tiles,),
        in_specs=[
            tile,
            pl.BlockSpec((N_EXPERTS, D_MODEL), lambda t, ea, eb, fl: (0, 0)),
            wspec(up, 0), wspec(up, 0), wspec(dn, 0), wspec(up, 1), wspec(up, 1), wspec(dn, 1),
        ],
        out_specs=tile,
        scratch_shapes=[pltpu.VMEM(up, BF16), pltpu.VMEM(up, BF16), pltpu.VMEM(dn, BF16),
                        pltpu.VMEM(up, BF16), pltpu.VMEM(up, BF16), pltpu.VMEM(dn, BF16)],
    )
    return pl.pallas_call(
        _moe_kernel,
        grid_spec=grid_spec,
        out_shape=jax.ShapeDtypeStruct((n_tiles * MOE_TM * TOK_ROWS, LANES), F32),
        compiler_params=_cparams(("arbitrary",)),
        name="moe",
    )(ea, eb, flags, hs, rwt_bf16, w_gate, w_up, w_down, w_gate, w_up, w_down)


def _fin_kernel(packed_ref, rs_ref, x_ref, ys_ref, g_ref, lnw_ref, lnb_ref, *rest, tm, group_tiles):
    o_refs = rest[:len(group_tiles)]
    ybuf, sem = rest[len(group_tiles):]
    i = pl.program_id(0)
    n_i = pl.num_programs(0)
    slot = i % 2

    def gather(tile, buf_slot):
        def body(r, _):
            d = _class_row(packed_ref[tile * tm + r], rs_ref)
            pltpu.make_async_copy(ys_ref.at[pl.ds(pl.multiple_of(d * TOK_ROWS, TOK_ROWS), TOK_ROWS)],
                                  ybuf.at[buf_slot, pl.ds(pl.multiple_of(r * TOK_ROWS, TOK_ROWS), TOK_ROWS)],
                                  sem.at[buf_slot]).start()
            return 0
        lax.fori_loop(0, tm, body, 0, unroll=8)

    @pl.when(i == 0)
    def _():
        gather(0, 0)

    @pl.when(i + 1 < n_i)
    def _():
        gather(jnp.minimum(i + 1, n_i - 1), 1 - slot)

    pltpu.make_async_copy(ys_ref.at[pl.ds(0, tm * TOK_ROWS)], ybuf.at[slot], sem.at[slot]).wait()
    y = _layer_norm(ALPHA * x_ref[...] + g_ref[...] * _load_token_tiles(ybuf, tm, slot), lnw_ref[...], lnb_ref[...])
    first = 0
    for o_ref, nt in zip(o_refs, group_tiles):
        @pl.when(jnp.logical_and(i >= first, i < first + nt))
        def _(o_ref=o_ref):
            o_ref[...] = y
        first += nt


def _fin(x1, ys, packed, row_start, mod, layer, ln_w, ln_b, cond_row, group_rows, tm=256):
    group_tiles = tuple(n // tm for n in group_rows)
    row = pl.BlockSpec((tm, D_MODEL), lambda i, *_: (i, 0))
    vec = pl.BlockSpec((1, D_MODEL), lambda i, *_: (0, 0))
    out_specs, first = [], 0
    for nt in group_tiles:
        out_specs.append(_group_spec(D_MODEL, tm, first, nt))
        first += nt
    grid_spec = pltpu.PrefetchScalarGridSpec(
        num_scalar_prefetch=2,
        grid=(sum(group_tiles),),
        in_specs=[row, pl.BlockSpec(memory_space=pl.ANY), _mod_spec(layer, 5, lambda i: cond_row(i, tm)), vec, vec],
        out_specs=out_specs,
        scratch_shapes=[pltpu.VMEM((2, tm * TOK_ROWS, LANES), F32), pltpu.SemaphoreType.DMA((2,))],
    )
    return pl.pallas_call(
        functools.partial(_fin_kernel, tm=tm, group_tiles=group_tiles),
        grid_spec=grid_spec,
        out_shape=[jax.ShapeDtypeStruct((n, D_MODEL), F32) for n in group_rows],
        compiler_params=_cparams(("arbitrary",)),
        name="post_moe",
    )(packed, row_start, x1, ys, mod, ln_w.reshape(1, -1), ln_b.reshape(1, -1))


def _filt_kernel(z_ref, w1_ref, b1_ref, w2_ref, b2_ref, fr_ref, w3_ref, dl_ref, o_ref):
    i = pl.program_id(0)
    z = z_ref[...]
    fr = fr_ref[...]
    a = jnp.sin(fr * (jnp.dot(z, w1_ref[...], precision=HIGHEST, preferred_element_type=F32) + b1_ref[...]))
    a = jnp.sin(fr * (jnp.dot(a, w2_ref[...], precision=HIGHEST, preferred_element_type=F32) + b2_ref[...]))
    filt = jnp.dot(a, w3_ref[...], precision=HIGHEST, preferred_element_type=F32)
    window = jnp.exp(-z[:, 0:1] * dl_ref[...]) + HY_SHIFT
    rows = i * z.shape[0] + lax.broadcasted_iota(jnp.int32, filt.shape, 0)
    o_ref[...] = jnp.where(rows == 0, 0.0, filt * window)


def _hyena_taps(seq_len, w1, b1, w2, b2, w3, freq):
    L = seq_len
    t = jnp.linspace(0.0, 1.0, L, dtype=F32)[:, None]
    bands = jnp.linspace(1e-4, HY_BANDS - 1, HY_BANDS, dtype=F32)
    ang = 2.0 * math.pi * bands[None, :] * jnp.arange(L, dtype=F32)[:, None] / L
    z = jnp.concatenate([t, jnp.cos(ang), -jnp.sin(ang)], axis=-1)
    offs = np.minimum(np.abs(np.arange(2 * L) - L), L - 1)
    emb = z.shape[1]
    z2 = jnp.zeros((2 * L, LANES), F32).at[:, :emb].set(z[offs])
    pad_c = lambda a: jnp.zeros((a.shape[0], LANES), F32).at[:, :a.shape[1]].set(a)
    pad_r = lambda a: jnp.zeros((LANES, a.shape[1]), F32).at[:a.shape[0], :].set(a)
    w1p = pad_r(pad_c(w1))
    w2p = pad_r(pad_c(w2))
    w3p = pad_r(w3)
    b1p, b2p, frp = pad_c(b1[None, :]), pad_c(b2[None, :]), pad_c(freq[None, :])
    deltas = jnp.abs(jnp.linspace(math.log(HY_DECAY_TARGET) / HY_SLOW, math.log(HY_DECAY_TARGET) / HY_FAST,
                                  D_MODEL, dtype=F32))[None, :]
    rb = min(512, L)
    cbf = D_MODEL
    ncb = D_MODEL // cbf
    nrb_back = L // rb
    sq = pl.BlockSpec((LANES, LANES), lambda i, j: (0, 0))
    vec = pl.BlockSpec((1, LANES), lambda i, j: (0, 0))
    return pl.pallas_call(
        _filt_kernel,
        grid=(2 * L // rb, ncb),
        in_specs=[
            pl.BlockSpec((rb, LANES), lambda i, j: (i, 0)),
            sq, vec, sq, vec, vec,
            pl.BlockSpec((LANES, cbf), lambda i, j: (0, jnp.where(i < nrb_back, ncb + j, j))),
            pl.BlockSpec((1, cbf), lambda i, j: (0, j)),
        ],
        out_specs=pl.BlockSpec((rb, cbf), lambda i, j: (i, j)),
        out_shape=jax.ShapeDtypeStruct((2 * L, D_MODEL), F32),
        compiler_params=_cparams(("arbitrary", "arbitrary")),
        name="hyena_taps",
    )(z2, w1p, b1p, w2p, b2p, frp, w3p, deltas)


def _dft_mats(cb):
    n = 2 * cb
    m = np.arange(cb)
    f = np.arange(cb)
    ang = 2.0 * np.pi * ((f[:, None] * m[None, :]) % n) / n
    fwd = np.concatenate([np.cos(ang), -np.sin(ang)], axis=0)
    fwd[cb, :] = np.where(m % 2 == 0, 1.0, -1.0)
    coef = np.where(f == 0, 1.0, 2.0)[None, :] / n
    inv = np.concatenate([coef * np.cos(ang.T), -coef * np.sin(ang.T)], axis=1)
    inv[:, cb] = np.where(m % 2 == 0, 1.0, -1.0) / n
    return fwd.astype(np.float32), inv.astype(np.float32)


def _hconv_kernel(x0_ref, x1_ref, v_ref, cw0_ref, cw1_ref, cw2_ref, cb0_ref, cb1_ref, cb2_ref,
                  taps_ref, fb_ref, fwd_ref, inv_ref, o_ref,
                  hs_ref, stage_ref, w32_ref, w_ref, x0c_ref, u_ref, y_ref, *, seq_len, blk, n_seq):
    L = seq_len
    nb = L // blk
    cw = x0_ref.shape[1]
    RC = 256
    bi = pl.program_id(1)
    row0 = lax.broadcasted_iota(jnp.int32, (RC, cw), 0) == 0

    @pl.when(bi == 0)
    def _():
        sign = jnp.where(lax.broadcasted_iota(jnp.int32, (2 * blk, 1), 0) % 2 == 0, 1.0, -1.0)
        for k in range(2 * nb - 1):
            base = L + (k - nb + 1) * blk
            head = taps_ref[base:base + blk, :].astype(BF16)
            tail = taps_ref[base - blk:base, :]
            tail = jnp.where(lax.broadcasted_iota(jnp.int32, tail.shape, 0) == 0, 0.0, tail).astype(BF16)
            hs_ref[k] = (jnp.dot(fwd_ref[...], head, preferred_element_type=F32)
                         + sign * jnp.dot(fwd_ref[...], tail, preferred_element_type=F32))

    zeros8 = jnp.zeros((8, cw), F32)
    for s in range(n_seq):
        r0 = s * L

        def short_conv(src_ref, cw_ref, cb_ref, store):
            stage_ref[0:8, :] = zeros8
            stage_ref[8 + L:16 + L, :] = zeros8
            for c in range(L // RC):
                stage_ref[8 + c * RC:8 + (c + 1) * RC, :] = src_ref[r0 + c * RC:r0 + (c + 1) * RC, :].astype(F32)
            w = cw_ref[...]
            for c in range(L // RC):
                lo = stage_ref[7 + c * RC:7 + (c + 1) * RC, :]
                mid = stage_ref[8 + c * RC:8 + (c + 1) * RC, :]
                hi = stage_ref[9 + c * RC:9 + (c + 1) * RC, :]
                store(c, lo * w[0:1, :] + mid * w[1:2, :] + hi * w[2:3, :] + cb_ref[...])

        def st_x1(c, val):
            w32_ref[c * RC:(c + 1) * RC, :] = val

        def st_v(c, val):
            w_ref[c * RC:(c + 1) * RC, :] = (w32_ref[c * RC:(c + 1) * RC, :] * val).astype(BF16)

        def st_x0(c, val):
            x0c_ref[c * RC:(c + 1) * RC, :] = val.astype(BF16)

        short_conv(x1_ref, cw1_ref, cb1_ref, st_x1)
        short_conv(v_ref, cw2_ref, cb2_ref, st_v)
        short_conv(x0_ref, cw0_ref, cb0_ref, st_x0)

        for j in range(nb):
            u_ref[j] = jnp.dot(fwd_ref[...], w_ref[j * blk:(j + 1) * blk, :], preferred_element_type=F32)

        for i in range(nb):
            for c in range(blk // RC):
                re = None
                im = None
                for j in range(nb):
                    k = i - j + nb - 1
                    a = u_ref[j, c * RC:(c + 1) * RC, :]
                    b = u_ref[j, blk + c * RC:blk + (c + 1) * RC, :]
                    hr = hs_ref[k, c * RC:(c + 1) * RC, :]
                    hi = hs_ref[k, blk + c * RC:blk + (c + 1) * RC, :]
                    bb = b * hi
                    if c == 0:
                        t_re = a * hr - jnp.where(row0, 0.0, bb)
                        t_im = jnp.where(row0, bb, a * hi + b * hr)
                    else:
                        t_re = a * hr - bb
                        t_im = a * hi + b * hr
                    re = t_re if re is None else re + t_re
                    im = t_im if im is None else im + t_im
                y_ref[c * RC:(c + 1) * RC, :] = re.astype(BF16)
                y_ref[blk + c * RC:blk + (c + 1) * RC, :] = im.astype(BF16)
            conv = jnp.dot(inv_ref[...], y_ref[...], preferred_element_type=F32)
            sl = slice(i * blk, (i + 1) * blk)
            z = x0c_ref[sl, :].astype(F32) * (conv + w_ref[sl, :].astype(F32) * fb_ref[...])
            o_ref[r0 + i * blk:r0 + (i + 1) * blk, :] = z.astype(BF16)


def _hyena_conv(u, conv_w, conv_b, taps, filt_bias, batch, seq_len, blk, n_seq, cw=256):
    T = u.shape[0]
    L = seq_len
    nb = L // blk
    ncw = D_MODEL // cw
    fwd_np, inv_np = _dft_mats(blk)
    fwd = jnp.asarray(fwd_np).astype(BF16)
    inv = jnp.asarray(inv_np).astype(BF16)
    rows = n_seq * L
    sec = lambda s: pl.BlockSpec((rows, cw), lambda c, b: (b, s * ncw + c))
    cws = lambda s: pl.BlockSpec((3, cw), lambda c, b: (0, s * ncw + c))
    cbs = lambda s: pl.BlockSpec((1, cw), lambda c, b: (0, s * ncw + c))
    return pl.pallas_call(
        functools.partial(_hconv_kernel, seq_len=L, blk=blk, n_seq=n_seq),
        grid=(ncw, batch // n_seq),
        in_specs=[sec(0), sec(1), sec(2), cws(0), cws(1), cws(2), cbs(0), cbs(1), cbs(2),
                  pl.BlockSpec((2 * L, cw), lambda c, b: (0, c)),
                  pl.BlockSpec((1, cw), lambda c, b: (0, c)),
                  pl.BlockSpec((2 * blk, blk), lambda c, b: (0, 0)),
                  pl.BlockSpec((blk, 2 * blk), lambda c, b: (0, 0))],
        out_specs=pl.BlockSpec((rows, cw), lambda c, b: (b, c)),
        out_shape=jax.ShapeDtypeStruct((T, D_MODEL), BF16),
        scratch_shapes=[
            pltpu.VMEM((2 * nb - 1, 2 * blk, cw), F32),
            pltpu.VMEM((L + 16, cw), F32),
            pltpu.VMEM((L, cw), F32),
            pltpu.VMEM((L, cw), BF16),
            pltpu.VMEM((L, cw), BF16),
            pltpu.VMEM((nb, 2 * blk, cw), F32),
            pltpu.VMEM((2 * blk, cw), BF16),
        ],
        compiler_params=_cparams(("arbitrary", "arbitrary")),
        name="hyena_conv",
    )(u, u, u, conv_w, conv_w, conv_w, conv_b.reshape(1, -1), conv_b.reshape(1, -1), conv_b.reshape(1, -1),
      taps, filt_bias.reshape(1, -1), fwd, inv)


def kernel(x_prompt, x_sample, cache_diff_k, cache_diff_v, state_ret_fwd, state_ret_bwd, c, c_ctx, ada_w, ada_b, ln_w, ln_b, ev_w_in, ev_w_out, ret_decay_fwd, ret_decay_bwd, ret_gn_w, diff_lambda, diff_subln_w, hy_w_in, hy_conv_w, hy_conv_b, hy_ffn_w1, hy_ffn_b1, hy_ffn_w2, hy_ffn_b2, hy_ffn_w3, hy_freq, hy_filter_bias, hy_w_out, router_w, router_bias, moe_w_gate, moe_w_up, moe_w_down):
    B, S, D = x_prompt.shape
    DB, DS, _ = x_sample.shape
    PAST = cache_diff_k.shape[2]
    TP, TS = B * S, DB * DS
    T_ALL = TP + TS
    assert D == D_MODEL and 1 + DB <= COND_ROWS and T_ALL < 65536

    cond8 = jnp.zeros((COND_ROWS, D), F32).at[0].set(c_ctx).at[1:1 + DB].set(c)
    mod = _ada_mod(cond8, ada_w, ada_b).reshape(DEPTH * COND_ROWS * N_MOD, 1, D)

    tm = 512
    groups = [
        dict(x=x_prompt.reshape(TP, D), batch=B, seq=S, off=0, cond_row=lambda i, tm: 0),
        dict(x=x_sample.reshape(TS, D), batch=DB, seq=DS, off=TP, cond_row=lambda i, tm: 1 + (i * tm) // DS),
    ]

    def cond_row_all(i, tm):
        return jnp.where(i * tm < TP, 0, 1 + (i * tm - TP) // DS)

    rw_pad = jnp.zeros((D, LANES), F32).at[:, :N_EXPERTS].set(router_w.astype(F32))
    rw_hi = rw_pad.astype(BF16)
    rw_lo = (rw_pad - rw_hi.astype(F32)).astype(BF16)
    rwt = router_w.T.astype(BF16)
    rbias = router_bias.astype(F32)
    n_tiles = T_ALL // MOE_TM + N_CLASSES
    outs = {}

    for l in range(DEPTH):
        mixes = []
        if l % 2 == 0:
            e = l // 2
            w_in = ev_w_in[e].astype(BF16)
            w_out = ev_w_out[e].astype(BF16)
            lg = jnp.stack([jnp.log1p(-jnp.exp2(ret_decay_fwd[e].astype(F32))),
                            jnp.log1p(-jnp.exp2(ret_decay_bwd[e].astype(F32)))])
            lam_init = 0.8 - 0.6 * math.exp(-0.3 * l)
            lq1, lk1, lq2, lk2 = diff_lambda[e].astype(F32)
            lam = (jnp.exp(jnp.sum(lq1 * lk1)) - jnp.exp(jnp.sum(lq2 * lk2)) + lam_init).reshape(1)
            kscale = RET_DK ** -0.5
            for gi, g in enumerate(groups):
                crow = functools.partial(g["cond_row"], tm=tm)
                if gi == 0:
                    secs = (("none", 1.0, False), ("none", kscale, False), ("none", 1.0, False),
                            ("none", 1.0, False), ("none", 1.0, False), ("none", 1.0, True),
                            ("none", 1.0, True))
                    proj, kd, vd = _in_proj(g["x"], mod, l, w_in, secs, crow, tm, tm=tm)
                    outs.setdefault("kd", []).append(kd.reshape(B, 1, S, H_DIFF, 2 * DIFF_QK))
                    outs.setdefault("vd", []).append(vd.reshape(B, 1, S, H_DIFF, DIFF_V))
                    ret, sf, sb = _retention(proj, lg, ret_gn_w[e], g["batch"], g["seq"], write_state=True)
                    outs.setdefault("sf", []).append(sf.reshape(B, 1, H_RET, RET_DK, RET_DV))
                    outs.setdefault("sb", []).append(sb.reshape(B, 1, H_RET, RET_DK, RET_DV))
                    att = _diff_attention(proj, lam, diff_subln_w[e], lam_init, g["batch"], g["seq"],
                                          tq=min(256, g["seq"]))
                else:
                    secs = (("ret", 1.0, False), ("ret", kscale, False), ("none", 1.0, False),
                            ("none", 1.0, False), ("diff", 1.0, False), ("diff", 1.0, False),
                            ("none", 1.0, False))
                    tabs = {"ret": _rope_tables(g["seq"], RET_DK), "diff": _rope_tables(g["seq"], DIFF_QK)}
                    (proj,) = _in_proj(g["x"], mod, l, w_in, secs, crow, g["seq"], rope_tabs=tabs, tm=tm)
                    ret, = _retention(proj, lg, ret_gn_w[e], g["batch"], g["seq"],
                                      states=(state_ret_fwd[:, e], state_ret_bwd[:, e]))
                    ctx = (cache_diff_k[:, e].reshape(DB, PAST, H_DIFF * 2 * DIFF_QK),
                           cache_diff_v[:, e].reshape(DB, PAST, H_DIFF * DIFF_V))
                    att = _diff_attention(proj, lam, diff_subln_w[e], lam_init, g["batch"], g["seq"], ctx=ctx,
                                          tq=512)
                mixes.append((ret, att))
        else:
            o = l // 2
            w_in = hy_w_in[o].astype(BF16)
            w_out = hy_w_out[o].astype(BF16)
            secs = (("none", 1.0, False),) * (3 * D // SEC)
            for gi, g in enumerate(groups):
                crow = functools.partial(g["cond_row"], tm=tm)
                (u,) = _in_proj(g["x"], mod, l, w_in, secs, crow, tm, tm=tm)
                taps = _hyena_taps(g["seq"], hy_ffn_w1[o], hy_ffn_b1[o], hy_ffn_w2[o], hy_ffn_b2[o],
                                   hy_ffn_w3[o], hy_freq[o])
                blk = min(g["seq"], 1024)
                n_seq = max(1, 1024 // g["seq"])
                z = _hyena_conv(u, hy_conv_w[o], hy_conv_b[o], taps, hy_filter_bias[o], g["batch"], g["seq"],
                                blk, n_seq)
                mixes.append((z,))

        x1_all, h2_all, info_all, counts = _post_mix(mixes, [g["x"] for g in groups], w_out, mod, l, ln_w[l, 0],
                                                     ln_b[l, 0], rw_hi, rw_lo, rbias, cond_row_all, tm)
        packed = info_all[0]
        counts_i = counts[:, 0].astype(jnp.int32)
        plan = _moe_plan(counts_i, n_tiles)
        hs = _dispatch(h2_all, packed, plan, counts_i, n_tiles)
        ys = _moe(hs, plan, l, rwt, moe_w_gate, moe_w_up, moe_w_down)
        xs = _fin(x1_all, ys, packed, plan[3], mod, l, ln_w[l, 1], ln_b[l, 1], cond_row_all, (TP, TS))
        for g, x in zip(groups, xs):
            g["x"] = x

    y_prompt = groups[0]["x"].reshape(B, S, D)
    y_sample = groups[1]["x"].reshape(DB, DS, D)
    cat = lambda xs: xs[0] if len(xs) == 1 else jnp.concatenate(xs, axis=1)
    return (y_prompt, y_sample, cat(outs["kd"]), cat(outs["vd"]), cat(outs["sf"]), cat(outs["sb"]))
```

```python
import functools
import math

import numpy as np
import jax
import jax.numpy as jnp
from jax import lax
from jax.experimental import pallas as pl
from jax.experimental.pallas import tpu as pltpu

F32 = jnp.float32
BF16 = jnp.bfloat16
HIGHEST = lax.Precision.HIGHEST

D_MODEL = 1024
DEPTH = 2
GRID_W = 64
H_RET = 4
RET_DK = 128
RET_DV = 128
RET_CHUNK = 128
H_DIFF = 4
DIFF_QK = 64
DIFF_V = 128
ROPE_BASE = 10000.0
HY_BANDS = 16
HY_FH = 64
HY_DECAY_TARGET = 1e-2
HY_FAST = 0.3
HY_SLOW = 1.5
HY_SHIFT = 0.05
N_EXPERTS = 16
N_GROUPS = 4
GROUP_SIZE = N_EXPERTS // N_GROUPS
D_FF_EXPERT = 512
ALPHA = (2 * DEPTH) ** 0.25
LN_EPS = 1e-5

LANES = 128
SEC = 512
COND_ROWS = 8
N_MOD = 6
VMEM_LIMIT = 50 * 1024 * 1024

PAIR_SLOTS = ((0, 1), (2, 1), (2, 3), (0, 3), (0, 2), (1, 3))
N_PAIRS = len(PAIR_SLOTS)
N_CLASSES = N_GROUPS * N_PAIRS
MOE_TM = 256
MOE_SHIFT = MOE_TM.bit_length() - 1
RET_HEADS_PER_STEP = 2
POST_SUB = 512
TOK_ROWS = D_MODEL // LANES


def _cparams(sem):
    return pltpu.CompilerParams(dimension_semantics=sem, vmem_limit_bytes=VMEM_LIMIT)


def _silu(x):
    return x * jax.nn.sigmoid(x)


def _store_token_tiles(ref, x):
    n = x.shape[0]
    for j in range(TOK_ROWS):
        ref[pl.ds(j, n, stride=TOK_ROWS), :] = x[:, j * LANES:(j + 1) * LANES]


def _load_token_tiles(ref, n, slot=None):
    idx = () if slot is None else (slot,)
    return jnp.concatenate([ref[idx + (pl.ds(j, n, stride=TOK_ROWS), slice(None))] for j in range(TOK_ROWS)],
                           axis=1)


def _layer_norm(x, w, b):
    mu = jnp.mean(x, axis=-1, keepdims=True)
    xc = x - mu
    var = jnp.mean(xc * xc, axis=-1, keepdims=True)
    return xc * lax.rsqrt(var + LN_EPS) * w + b


def _dot_3pass(a, w):
    a_hi = a.astype(BF16)
    a_lo = (a - a_hi.astype(F32)).astype(BF16)
    w_hi = w.astype(BF16)
    w_lo = (w - w_hi.astype(F32)).astype(BF16)
    return (jnp.dot(a_hi, w_hi, preferred_element_type=F32) + jnp.dot(a_lo, w_hi, preferred_element_type=F32)
            + jnp.dot(a_hi, w_lo, preferred_element_type=F32))


def _ada_kernel(c_ref, w_ref, b_ref, o_ref):
    o_ref[...] = _dot_3pass(_silu(c_ref[...]), w_ref[...]) + b_ref[...]


def _ada_mod(cond8, ada_w, ada_b):
    tn = 1024
    nj = ada_w.shape[2] // tn
    return pl.pallas_call(
        _ada_kernel,
        grid=(DEPTH, nj),
        in_specs=[
            pl.BlockSpec((COND_ROWS, D_MODEL), lambda l, j: (0, 0)),
            pl.BlockSpec((None, D_MODEL, tn), lambda l, j: (l, 0, j)),
            pl.BlockSpec((None, 1, tn), lambda l, j: (l, 0, j)),
        ],
        out_specs=pl.BlockSpec((None, COND_ROWS, tn), lambda l, j: (l, 0, j)),
        out_shape=jax.ShapeDtypeStruct((DEPTH, COND_ROWS, ada_w.shape[2]), F32),
        compiler_params=_cparams(("arbitrary", "arbitrary")),
        name="ada_mod",
    )(cond8, ada_w, ada_b.reshape(DEPTH, 1, -1))


def _mod_spec(layer, chunk, row_of_tile):
    def imap(i, *_):
        return ((layer * COND_ROWS + row_of_tile(i)) * N_MOD + chunk, 0, 0)
    return pl.BlockSpec((None, 1, D_MODEL), imap)


def _rope(a, tabs, quarter):
    c, sa, sb = tabs
    out = []
    for hb in range(a.shape[1] // LANES):
        blk = a[:, hb * LANES:(hb + 1) * LANES]
        up = pltpu.roll(blk, LANES - quarter, axis=1)
        dn = pltpu.roll(blk, quarter, axis=1)
        out.append(blk * c + up * sa + dn * sb)
    return jnp.concatenate(out, axis=1)


def _in_kernel(*refs, secs, n_f32_out):
    x_ref, sh_ref, sc_ref, w_ref = refs[:4]
    pos = 4
    tabs = {}
    for kind in ("ret", "diff"):
        if any(s[0] == kind for s in secs):
            tabs[kind] = tuple(r[...] for r in refs[pos:pos + 3])
            pos += 3
    o_ref = refs[pos]
    f32_refs = refs[pos + 1:]
    h = (x_ref[...] * (1.0 + sc_ref[...]) + sh_ref[...]).astype(BF16)
    k32 = 0
    for s, (kind, scale, want_f32) in enumerate(secs):
        acc = jnp.dot(h, w_ref[:, s * SEC:(s + 1) * SEC], preferred_element_type=F32)
        if scale != 1.0:
            acc = acc * scale
        if kind == "ret":
            acc = _rope(acc, tabs["ret"], RET_DK // 4)
        elif kind == "diff":
            acc = _rope(acc, tabs["diff"], DIFF_QK // 4)
        o_ref[:, s * SEC:(s + 1) * SEC] = acc.astype(BF16)
        if want_f32:
            f32_refs[k32][...] = acc
            k32 += 1
    assert k32 == n_f32_out


def _in_proj(x2d, mod, layer, w_bf16, secs, row_of_tile, seq_len, rope_tabs=None, tm=512):
    T = x2d.shape[0]
    N = w_bf16.shape[1]
    assert N == SEC * len(secs) and T % tm == 0 and seq_len % tm == 0
    tiles_per_seq = seq_len // tm
    in_specs = [
        pl.BlockSpec((tm, D_MODEL), lambda i: (i, 0)),
        _mod_spec(layer, 0, row_of_tile),
        _mod_spec(layer, 1, row_of_tile),
        pl.BlockSpec((D_MODEL, N), lambda i: (0, 0)),
    ]
    args = [x2d, mod, mod, w_bf16]
    for kind in ("ret", "diff"):
        if any(s[0] == kind for s in secs):
            for t in rope_tabs[kind]:
                in_specs.append(pl.BlockSpec((tm, LANES), lambda i: (i % tiles_per_seq, 0)))
                args.append(t)
    n_f32 = sum(1 for s in secs if s[2])
    out_shape = [jax.ShapeDtypeStruct((T, N), BF16)] + [jax.ShapeDtypeStruct((T, SEC), F32)] * n_f32
    out_specs = [pl.BlockSpec((tm, N), lambda i: (i, 0))] + [pl.BlockSpec((tm, SEC), lambda i: (i, 0))] * n_f32
    return pl.pallas_call(
        functools.partial(_in_kernel, secs=secs, n_f32_out=n_f32),
        grid=(T // tm,),
        in_specs=in_specs,
        out_specs=out_specs,
        out_shape=out_shape,
        compiler_params=_cparams(("arbitrary",)),
        name="in_proj",
    )(*args)


def _rope_tables(seq_len, d):
    half = d // 2
    quarter = half // 2
    t = jnp.arange(seq_len)
    row = (t // GRID_W).astype(F32)
    col = (t % GRID_W).astype(F32)
    inv = ROPE_BASE ** (-jnp.arange(quarter, dtype=F32) / quarter)
    ang_r = row[:, None] * inv[None, :]
    ang_c = col[:, None] * inv[None, :]
    zero = jnp.zeros_like(ang_r)
    cos = jnp.concatenate([jnp.cos(ang_r)] * 2 + [jnp.cos(ang_c)] * 2, axis=1)
    sa = jnp.concatenate([-jnp.sin(ang_r), zero, -jnp.sin(ang_c), zero], axis=1)
    sb = jnp.concatenate([zero, jnp.sin(ang_r), zero, jnp.sin(ang_c)], axis=1)
    reps = LANES // d
    return tuple(jnp.tile(a, (1, reps)) for a in (cos, sa, sb))


def _ret_kernel(*refs, n_chunks, has_state, write_state):
    lg_ref, q_ref, k_ref, v_ref, g_ref, gnw_ref = refs[:6]
    pos = 6
    if has_state:
        s0f_ref, s0b_ref = refs[pos:pos + 2]
        pos += 2
    o_ref = refs[pos]
    pos += 1
    if write_state:
        sf_ref, sb_ref = refs[pos:pos + 2]
        pos += 2
    acc_ref, kv_ref = refs[pos:pos + 2]
    for hh in range(RET_HEADS_PER_STEP):
        _ret_head(hh, lg_ref, q_ref, k_ref, v_ref, g_ref, gnw_ref,
                  (s0f_ref, s0b_ref) if has_state else None, o_ref, (sf_ref, sb_ref) if write_state else None,
                  acc_ref, kv_ref, n_chunks)


def _ret_head(hh, lg_ref, q_ref, k_ref, v_ref, g_ref, gnw_ref, s0_refs, o_ref, s_out_refs, acc_ref, kv_ref,
              n_chunks):
    C = RET_CHUNK
    cols = slice(hh * LANES, (hh + 1) * LANES)
    hd = pl.program_id(1) * RET_HEADS_PER_STEP + hh
    lgf = lg_ref[0, hd]
    lgb = lg_ref[1, hd]
    ii = lax.broadcasted_iota(jnp.int32, (C, C), 0).astype(F32)
    jj = lax.broadcasted_iota(jnp.int32, (C, C), 1).astype(F32)
    rel = ii - jj
    d_f = jnp.where(rel >= 0, jnp.exp(jnp.maximum(rel, 0.0) * lgf), 0.0)
    d_b = jnp.where(rel <= 0, jnp.exp(jnp.maximum(-rel, 0.0) * lgb), 0.0)
    d_sum = d_f + d_b
    idx = lax.broadcasted_iota(jnp.int32, (C, 1), 0).astype(F32)
    xi_f = jnp.exp((idx + 1.0) * lgf)
    zeta_f = jnp.exp((C - 1.0 - idx) * lgf)
    xi_b = jnp.exp((C - idx) * lgb)
    zeta_b = jnp.exp(idx * lgb)
    one = jnp.ones((1, 1), F32)
    gc_f = jnp.exp(one * (C * lgf))
    gc_b = jnp.exp(one * (C * lgb))

    nt = (((1,), (1,)), ((), ()))
    tn = (((0,), (0,)), ((), ()))

    if s0_refs is not None:
        s_f = s0_refs[0][hh]
        s_b = s0_refs[1][hh]
    else:
        s_f = jnp.zeros((RET_DK, RET_DV), F32)
        s_b = jnp.zeros((RET_DK, RET_DV), F32)

    for n in range(n_chunks):
        sl = slice(n * C, (n + 1) * C)
        qc, kc, vc = q_ref[sl, cols], k_ref[sl, cols], v_ref[sl, cols]
        scores = lax.dot_general(qc, kc, nt, preferred_element_type=F32) * d_sum
        acc_ref[sl, cols] = jnp.dot(scores.astype(BF16), vc, preferred_element_type=F32)
        kf = kc.astype(F32)
        kv_ref[0, hh, n] = lax.dot_general((kf * zeta_f).astype(BF16), vc, tn, preferred_element_type=F32)
        kv_ref[1, hh, n] = lax.dot_general((kf * zeta_b).astype(BF16), vc, tn, preferred_element_type=F32)

    for n in range(n_chunks):
        sl = slice(n * C, (n + 1) * C)
        qf = q_ref[sl, cols].astype(F32)
        acc_ref[sl, cols] += jnp.dot((qf * xi_f).astype(BF16), s_f.astype(BF16), preferred_element_type=F32)
        s_f = gc_f * s_f + kv_ref[0, hh, n]

    gnw = gnw_ref[:, cols]
    for n in reversed(range(n_chunks)):
        sl = slice(n * C, (n + 1) * C)
        qf = q_ref[sl, cols].astype(F32)
        cross = jnp.dot((qf * xi_b).astype(BF16), s_b.astype(BF16), preferred_element_type=F32)
        r = acc_ref[sl, cols] + cross
        mu = jnp.mean(r, axis=-1, keepdims=True)
        rc = r - mu
        var = jnp.mean(rc * rc, axis=-1, keepdims=True)
        rn = rc * lax.rsqrt(var + LN_EPS) * gnw
        o_ref[sl, cols] = (_silu(g_ref[sl, cols].astype(F32)) * rn).astype(BF16)
        s_b = gc_b * s_b + kv_ref[1, hh, n]

    if s_out_refs is not None:
        s_out_refs[0][hh] = s_f
        s_out_refs[1][hh] = s_b


def _retention(proj, lg, gn_w, batch, seq_len, states=None, write_state=False):
    T = proj.shape[0]
    hp = RET_HEADS_PER_STEP
    width = hp * LANES
    per_sec = SEC // width
    blk = lambda sec: pl.BlockSpec((seq_len, width), lambda b, h: (b, sec * per_sec + h))
    in_specs = [pl.BlockSpec(memory_space=pltpu.SMEM), blk(0), blk(1), blk(2), blk(3),
                pl.BlockSpec((1, width), lambda b, h: (0, h))]
    args = [lg, proj, proj, proj, proj, gn_w.reshape(1, -1)]
    if states is not None:
        st = pl.BlockSpec((None, hp, RET_DK, RET_DV), lambda b, h: (b, h, 0, 0))
        in_specs += [st, st]
        args += list(states)
    out_shape = [jax.ShapeDtypeStruct((T, SEC), BF16)]
    out_specs = [pl.BlockSpec((seq_len, width), lambda b, h: (b, h))]
    if write_state:
        st_o = pl.BlockSpec((None, hp, RET_DK, RET_DV), lambda b, h: (b, h, 0, 0))
        out_shape += [jax.ShapeDtypeStruct((batch, H_RET, RET_DK, RET_DV), F32)] * 2
        out_specs += [st_o, st_o]
    return pl.pallas_call(
        functools.partial(_ret_kernel, n_chunks=seq_len // RET_CHUNK, has_state=states is not None,
                          write_state=write_state),
        grid=(batch, H_RET // hp),
        in_specs=in_specs,
        out_specs=out_specs,
        out_shape=out_shape,
        scratch_shapes=[pltpu.VMEM((seq_len, width), F32),
                        pltpu.VMEM((2, hp, seq_len // RET_CHUNK, RET_DK, RET_DV), F32)],
        compiler_params=_cparams(("arbitrary", "arbitrary")),
        name="retention",
    )(*args)


def _att_kernel(*refs, has_ctx, out_scale, key_chunk):
    lam_ref, q_ref, k_ref, v_ref = refs[:4]
    pos = 4
    if has_ctx:
        ck_ref, cv_ref = refs[pos:pos + 2]
        pos += 2
    w_ref, o_ref = refs[pos:pos + 2]
    lam = lam_ref[0]
    q = q_ref[...]
    tq = q.shape[0]
    lane = lax.broadcasted_iota(jnp.int32, q.shape, 1)
    zero = jnp.zeros_like(q)
    qq = jnp.concatenate([jnp.where(lane < DIFF_QK, q, zero), jnp.where(lane >= DIFF_QK, q, zero)], axis=0)
    qq = qq * jnp.asarray(DIFF_QK ** -0.5, BF16)
    nt = (((1,), (1,)), ((), ()))
    chunks = [(k_ref, v_ref, c * key_chunk, key_chunk) for c in range(k_ref.shape[0] // key_chunk)]
    if has_ctx:
        chunks.append((ck_ref, cv_ref, 0, ck_ref.shape[0]))
    m = l = acc = None
    for kr, vr, off, n in chunks:
        kch = kr[off:off + n, :].astype(BF16)
        vch = vr[off:off + n, :].astype(BF16)
        s = lax.dot_general(qq, kch, nt, preferred_element_type=F32)
        cm = jnp.max(s, axis=-1, keepdims=True)
        m_new = cm if m is None else jnp.maximum(m, cm)
        p = jnp.exp(s - m_new)
        ps = jnp.sum(p, axis=-1, keepdims=True)
        pv = jnp.dot(p.astype(BF16), vch, preferred_element_type=F32)
        if m is None:
            l, acc = ps, pv
        else:
            alpha = jnp.exp(m - m_new)
            l = alpha * l + ps
            acc = alpha * acc + pv
        m = m_new
    o = acc / l
    att = o[:tq] - lam * o[tq:]
    att = att * lax.rsqrt(jnp.mean(att * att, axis=-1, keepdims=True) + LN_EPS)
    o_ref[...] = (att * w_ref[...] * out_scale).astype(BF16)


def _diff_attention(proj, lam, subln_w, lam_init, batch, seq_len, ctx=None, tq=256):
    T = proj.shape[0]
    hsec = SEC // LANES
    nq = seq_len // tq
    in_specs = [
        pl.BlockSpec(memory_space=pltpu.SMEM),
        pl.BlockSpec((tq, LANES), lambda b, h, i: (b * nq + i, 4 * hsec + h)),
        pl.BlockSpec((seq_len, LANES), lambda b, h, i: (b, 5 * hsec + h)),
        pl.BlockSpec((seq_len, LANES), lambda b, h, i: (b, 6 * hsec + h)),
    ]
    args = [lam, proj, proj, proj]
    if ctx is not None:
        ck, cv = ctx
        past = ck.shape[1]
        cspec = pl.BlockSpec((None, past, LANES), lambda b, h, i: (b, 0, h))
        in_specs += [cspec, cspec]
        args += [ck, cv]
    in_specs.append(pl.BlockSpec((1, LANES), lambda b, h, i: (0, 0)))
    args.append(subln_w.reshape(1, -1))
    return pl.pallas_call(
        functools.partial(_att_kernel, has_ctx=ctx is not None, out_scale=1.0 - lam_init,
                          key_chunk=min(512, seq_len)),
        grid=(batch, H_DIFF, nq),
        in_specs=in_specs,
        out_specs=pl.BlockSpec((tq, LANES), lambda b, h, i: (b * nq + i, h)),
        out_shape=jax.ShapeDtypeStruct((T, SEC), BF16),
        compiler_params=_cparams(("arbitrary", "arbitrary", "arbitrary")),
        name="diff_attention",
    )(*args)


def _route_class(lt, rb_ref):
    sel = [jax.nn.sigmoid(lt[e:e + 1, :]) + rb_ref[e] for e in range(N_EXPERTS)]
    gscore = []
    for g in range(N_GROUPS):
        mem = sel[g * GROUP_SIZE:(g + 1) * GROUP_SIZE]
        best = None
        for a in range(GROUP_SIZE):
            for b in range(a + 1, GROUP_SIZE):
                pair = mem[a] + mem[b]
                best = pair if best is None else jnp.maximum(best, pair)
        gscore.append(best)
    gbest = gscore[0]
    gidx = jnp.zeros_like(gbest)
    for g in range(1, N_GROUPS):
        upd = gscore[g] > gbest
        gidx = jnp.where(upd, float(g), gidx)
        gbest = jnp.where(upd, gscore[g], gbest)
    msel = []
    for j in range(GROUP_SIZE):
        out = sel[j]
        for g in range(1, N_GROUPS):
            out = jnp.where(gidx == float(g), sel[g * GROUP_SIZE + j], out)
        msel.append(out)
    one = jnp.ones_like(gbest)
    zero = jnp.zeros_like(gbest)
    chosen = []
    for j in range(GROUP_SIZE):
        rank = zero
        for k in range(GROUP_SIZE):
            if k < j:
                rank = rank + jnp.where(msel[k] >= msel[j], one, zero)
            elif k > j:
                rank = rank + jnp.where(msel[k] > msel[j], one, zero)
        chosen.append(jnp.where(rank < 2.0, one, zero))
    c0, c1, c2, c3 = chosen
    order = jnp.where(c0 * c1 > 0, 0.0, jnp.where(c1 * c2 > 0, 1.0, jnp.where(c2 * c3 > 0, 2.0,
            jnp.where(c0 * c3 > 0, 3.0, jnp.where(c0 * c2 > 0, 4.0, 5.0)))))
    return gidx * float(N_PAIRS) + order


def _post_kernel(*refs, n_mix, group_tiles):
    n_groups = len(group_tiles)
    per_group = n_mix + 1
    group_refs = [refs[g * per_group:(g + 1) * per_group] for g in range(n_groups)]
    refs = refs[n_groups * per_group:]
    (w_ref, g_ref, sh_ref, sc_ref, lnw_ref, lnb_ref, rwh_ref, rwl_ref, rb_ref, tri_ref,
     x1_ref, h2_ref, info_ref, cout_ref, cnt_ref, pre_ref) = refs
    i = pl.program_id(0)

    @pl.when(i == 0)
    def _():
        cnt_ref[...] = jnp.zeros_like(cnt_ref)

    first = 0
    for g in range(n_groups):
        @pl.when(jnp.logical_and(i >= first, i < first + group_tiles[g]))
        def _(g=g):
            out = None
            off = 0
            for m_ref in group_refs[g][:n_mix]:
                width = m_ref.shape[1]
                part = jnp.dot(m_ref[...], w_ref[off:off + width, :], preferred_element_type=F32)
                out = part if out is None else out + part
                off += width
            pre_ref[...] = ALPHA * group_refs[g][n_mix][...] + g_ref[...] * out
        first += group_tiles[g]

    tm = pre_ref.shape[0]
    ts = tri_ref.shape[0]
    base = cnt_ref[:, 0:1]
    for part in range(tm // ts):
        rows = slice(part * ts, (part + 1) * ts)
        x1 = _layer_norm(pre_ref[rows, :], lnw_ref[...], lnb_ref[...])
        x1_ref[rows, :] = x1
        h2 = x1 * (1.0 + sc_ref[...]) + sh_ref[...]
        for j in range(TOK_ROWS):
            h2_ref[pl.ds(part * ts * TOK_ROWS + j, ts, stride=TOK_ROWS), :] = h2[:, j * LANES:(j + 1) * LANES]
        h_hi = h2.astype(BF16)
        h_lo = (h2 - h_hi.astype(F32)).astype(BF16)
        logits = (jnp.dot(h_hi, rwh_ref[...], preferred_element_type=F32)
                  + jnp.dot(h_lo, rwh_ref[...], preferred_element_type=F32)
                  + jnp.dot(h_hi, rwl_ref[...], preferred_element_type=F32))
        cls = _route_class(logits.T, rb_ref)
        crow = lax.broadcasted_iota(jnp.int32, (32, ts), 0).astype(F32)
        onehot = jnp.where(crow == cls, 1.0, 0.0)
        prefix = jnp.dot(onehot.astype(BF16), tri_ref[...], preferred_element_type=F32)
        rank = jnp.sum(onehot * (prefix - 1.0 + base), axis=0, keepdims=True)
        base = base + jnp.sum(onehot, axis=1, keepdims=True)
        packed = cls.astype(jnp.int32) * 65536 + rank.astype(jnp.int32)
        row = lax.broadcasted_iota(jnp.int32, (8, ts), 0)
        info_ref[:, rows] = jnp.where(row == 0, packed, 0)
    cnt_ref[...] = jnp.broadcast_to(base, cnt_ref.shape)

    @pl.when(i == pl.num_programs(0) - 1)
    def _():
        cout_ref[...] = cnt_ref[...]


def _tri(tm):
    return jnp.asarray(np.triu(np.ones((tm, tm), np.float32))).astype(BF16)


def _group_spec(width, tm, first_tile, n_tiles):
    return pl.BlockSpec((tm, width), lambda i, *_: (jnp.clip(i - first_tile, 0, n_tiles - 1), 0))


def _post_mix(group_mixes, group_x, w_out_bf16, mod, layer, ln_w, ln_b, rw_hi, rw_lo, router_bias, cond_row, tm):
    group_tiles = tuple(x.shape[0] // tm for x in group_x)
    n_tiles = sum(group_tiles)
    total_rows = n_tiles * tm
    row = pl.BlockSpec((tm, D_MODEL), lambda i: (i, 0))
    vec = pl.BlockSpec((1, D_MODEL), lambda i: (0, 0))
    cnt = pl.BlockSpec((32, LANES), lambda i: (0, 0))
    in_specs, args = [], []
    first = 0
    for mixes, x, nt in zip(group_mixes, group_x, group_tiles):
        for m in mixes:
            in_specs.append(_group_spec(m.shape[1], tm, first, nt))
            args.append(m)
        in_specs.append(_group_spec(D_MODEL, tm, first, nt))
        args.append(x)
        first += nt
    in_specs += [
        pl.BlockSpec((D_MODEL, D_MODEL), lambda i: (0, 0)),
        _mod_spec(layer, 2, lambda i: cond_row(i, tm)), _mod_spec(layer, 3, lambda i: cond_row(i, tm)),
        _mod_spec(layer, 4, lambda i: cond_row(i, tm)),
        vec, vec,
        pl.BlockSpec((D_MODEL, LANES), lambda i: (0, 0)),
        pl.BlockSpec((D_MODEL, LANES), lambda i: (0, 0)),
        pl.BlockSpec(memory_space=pltpu.SMEM),
        pl.BlockSpec((POST_SUB, POST_SUB), lambda i: (0, 0)),
    ]
    args += [w_out_bf16, mod, mod, mod, ln_w.reshape(1, -1), ln_b.reshape(1, -1), rw_hi, rw_lo, router_bias,
             _tri(POST_SUB)]
    return pl.pallas_call(
        functools.partial(_post_kernel, n_mix=len(group_mixes[0]), group_tiles=group_tiles),
        grid=(n_tiles,),
        in_specs=in_specs,
        out_specs=[row, pl.BlockSpec((tm * TOK_ROWS, LANES), lambda i: (i, 0)),
                   pl.BlockSpec((8, tm), lambda i: (0, i)), cnt],
        out_shape=[jax.ShapeDtypeStruct((total_rows, D_MODEL), F32),
                   jax.ShapeDtypeStruct((total_rows * TOK_ROWS, LANES), F32),
                   jax.ShapeDtypeStruct((8, total_rows), jnp.int32), jax.ShapeDtypeStruct((32, LANES), F32)],
        scratch_shapes=[pltpu.VMEM((32, LANES), F32), pltpu.VMEM((tm, D_MODEL), F32)],
        compiler_params=_cparams(("arbitrary",)),
        name="post_mix",
    )(*args)


def _plan_kernel(cnt_ref, ea_ref, eb_ref, fl_ref, rs_ref, *, n_tiles):
    start = jnp.int32(0)
    prev_a = jnp.int32(-1)
    prev_b = jnp.int32(-1)
    for c in range(N_CLASSES):
        n = cnt_ref[c]
        tiles = lax.shift_right_logical(n + (MOE_TM - 1), MOE_SHIFT)
        row0 = start * MOE_TM
        rs_ref[c] = row0
        g, pr = divmod(c, N_PAIRS)
        a = g * GROUP_SIZE + PAIR_SLOTS[pr][0]
        b = g * GROUP_SIZE + PAIR_SLOTS[pr][1]
        first = 1 + 4 * (prev_a != a).astype(jnp.int32) + 8 * (prev_b != b).astype(jnp.int32)

        def tile_body(k, _, start=start, a=a, b=b, first=first):
            t = start + k
            ea_ref[t] = a
            eb_ref[t] = b
            fl_ref[t] = jnp.where(k == 0, first, 1)
            return 0

        lax.fori_loop(0, tiles, tile_body, 0)
        has = tiles > 0
        prev_a = jnp.where(has, a, prev_a)
        prev_b = jnp.where(has, b, prev_b)
        start = start + tiles
    for c in range(N_CLASSES, 31):
        rs_ref[c] = 0
    rs_ref[31] = start

    def idle_body(t, _):
        ea_ref[t] = prev_a
        eb_ref[t] = prev_b
        fl_ref[t] = 0
        return 0

    lax.fori_loop(start, n_tiles, idle_body, 0)


def _moe_plan(counts, n_tiles):
    smem = pl.BlockSpec(memory_space=pltpu.SMEM)
    i32 = lambda n: jax.ShapeDtypeStruct((n,), jnp.int32)
    return pl.pallas_call(
        functools.partial(_plan_kernel, n_tiles=n_tiles),
        in_specs=[smem],
        out_specs=[smem] * 4,
        out_shape=[i32(n_tiles), i32(n_tiles), i32(n_tiles), i32(32)],
        name="moe_plan",
    )(counts)


def _class_row(packed, rs_ref):
    return rs_ref[lax.shift_right_logical(packed, 16)] + (packed & 0xFFFF)


def _dispatch_kernel(packed_ref, rs_ref, cnt_ref, h_ref, hs_ref, sem, fill_sem, *, tm, n_tiles):
    i = pl.program_id(0)
    used = rs_ref[31]

    def tile_rows(row, n=1):
        return pl.ds(pl.multiple_of(row * TOK_ROWS, TOK_ROWS), n * TOK_ROWS)

    def fill_copy(row):
        return pltpu.make_async_copy(h_ref.at[tile_rows(0)], hs_ref.at[tile_rows(row)], fill_sem)

    def idle_copy(t):
        return pltpu.make_async_copy(h_ref.at[tile_rows(0, MOE_TM)], hs_ref.at[tile_rows(t * MOE_TM, MOE_TM)],
                                     fill_sem)

    def class_pad(c):
        n = cnt_ref[c]
        first = rs_ref[c] + n
        last = rs_ref[c] + lax.shift_left(lax.shift_right_logical(n + (MOE_TM - 1), MOE_SHIFT), MOE_SHIFT)
        return first, last

    @pl.when(i == 0)
    def _():
        for c in range(N_CLASSES):
            first, last = class_pad(c)
            lax.fori_loop(first, last, lambda r, _: (fill_copy(r).start(), 0)[1], 0)
        lax.fori_loop(used, n_tiles, lambda t, _: (idle_copy(t).start(), 0)[1], 0)

    def body(r8, _):
        for j in range(8):
            r = r8 * 8 + j
            row = _class_row(packed_ref[i * tm + r], rs_ref)
            pltpu.make_async_copy(h_ref.at[tile_rows(r)], hs_ref.at[tile_rows(row)], sem).start(priority=j % 2)
        return 0

    lax.fori_loop(0, tm // 8, body, 0)
    pltpu.make_async_copy(h_ref, hs_ref.at[tile_rows(0, tm)], sem).wait()

    @pl.when(i == 0)
    def _():
        for c in range(N_CLASSES):
            first, last = class_pad(c)
            lax.fori_loop(first, last, lambda r, _: (fill_copy(r).wait(), 0)[1], 0)
        lax.fori_loop(used, n_tiles, lambda t, _: (idle_copy(t).wait(), 0)[1], 0)


def _dispatch(h2_tiles, packed, plan, counts, n_tiles, tm=1024):
    T = packed.shape[0]
    assert tm >= MOE_TM and T % tm == 0
    grid_spec = pltpu.PrefetchScalarGridSpec(
        num_scalar_prefetch=3,
        grid=(T // tm,),
        in_specs=[pl.BlockSpec((tm * TOK_ROWS, LANES), lambda i, *_: (i, 0))],
        out_specs=pl.BlockSpec(memory_space=pl.ANY),
        scratch_shapes=[pltpu.SemaphoreType.DMA(()), pltpu.SemaphoreType.DMA(())],
    )
    return pl.pallas_call(
        functools.partial(_dispatch_kernel, tm=tm, n_tiles=n_tiles),
        grid_spec=grid_spec,
        out_shape=jax.ShapeDtypeStruct((n_tiles * MOE_TM * TOK_ROWS, LANES), F32),
        compiler_params=_cparams(("arbitrary",)),
        name="moe_dispatch",
    )(packed, plan[3], counts, h2_tiles)


def _moe_kernel(ea_ref, eb_ref, fl_ref, h_ref, rwt_ref, wga, wua, wda, wgb, wub, wdb, o_ref,
                sga, sua, sda, sgb, sub, sdb):
    t = pl.program_id(0)
    flags = fl_ref[t]
    valid = (flags & 1) != 0

    @pl.when((flags & 4) != 0)
    def _():
        sga[...] = wga[...].astype(BF16)
        sua[...] = wua[...].astype(BF16)
        sda[...] = wda[...].astype(BF16)

    @pl.when((flags & 8) != 0)
    def _():
        sgb[...] = wgb[...].astype(BF16)
        sub[...] = wub[...].astype(BF16)
        sdb[...] = wdb[...].astype(BF16)

    @pl.when(valid)
    def _():
        h = _load_token_tiles(h_ref, MOE_TM).astype(BF16)
        nt = (((1,), (1,)), ((), ()))
        score = jax.nn.sigmoid(lax.dot_general(h, rwt_ref[...], nt, preferred_element_type=F32))
        lane = lax.broadcasted_iota(jnp.int32, score.shape, 1)
        s_a = jnp.sum(jnp.where(lane == ea_ref[t], score, 0.0), axis=1, keepdims=True)
        s_b = jnp.sum(jnp.where(lane == eb_ref[t], score, 0.0), axis=1, keepdims=True)
        tot = s_a + s_b
        act_a = _silu(jnp.dot(h, sga[...], preferred_element_type=F32)) \
            * jnp.dot(h, sua[...], preferred_element_type=F32) * (s_a / tot)
        act_b = _silu(jnp.dot(h, sgb[...], preferred_element_type=F32)) \
            * jnp.dot(h, sub[...], preferred_element_type=F32) * (s_b / tot)
        _store_token_tiles(o_ref, jnp.dot(act_a.astype(BF16), sda[...], preferred_element_type=F32)
                           + jnp.dot(act_b.astype(BF16), sdb[...], preferred_element_type=F32))

    @pl.when(jnp.logical_not(valid))
    def _():
        o_ref[...] = jnp.zeros_like(o_ref)


def _moe(hs, plan, layer, rwt_bf16, w_gate, w_up, w_down):
    ea, eb, flags, _ = plan
    n_tiles = ea.shape[0]
    wspec = lambda shape, which: pl.BlockSpec(
        (None, None) + shape, (lambda t, ea, eb, fl: (layer, ea[t], 0, 0)) if which == 0
        else (lambda t, ea, eb, fl: (layer, eb[t], 0, 0)))
    up = (D_MODEL, D_FF_EXPERT)
    dn = (D_FF_EXPERT, D_MODEL)
    tile = pl.BlockSpec((MOE_TM * TOK_ROWS, LANES), lambda t, ea, eb, fl: (t, 0))
    grid_spec = pltpu.PrefetchScalarGridSpec(
        num_scalar_prefetch=3,
        grid=(n_tiles,),
        in_specs=[
            tile,
            pl.BlockSpec((N_EXPERTS, D_MODEL), lambda t, ea, eb, fl: (0, 0)),
            wspec(up, 0), wspec(up, 0), wspec(dn, 0), wspec(up, 1), wspec(up, 1), wspec(dn, 1),
        ],
        out_specs=tile,
        scratch_shapes=[pltpu.VMEM(up, BF16), pltpu.VMEM(up, BF16), pltpu.VMEM(dn, BF16),
                        pltpu.VMEM(up, BF16), pltpu.VMEM(up, BF16), pltpu.VMEM(dn, BF16)],
    )
    return pl.pallas_call(
        _moe_kernel,
        grid_spec=grid_spec,
        out_shape=jax.ShapeDtypeStruct((n_tiles * MOE_TM * TOK_ROWS, LANES), F32),
        compiler_params=_cparams(("arbitrary",)),
        name="moe",
    )(ea, eb, flags, hs, rwt_bf16, w_gate, w_up, w_down, w_gate, w_up, w_down)


def _fin_kernel(packed_ref, rs_ref, x_ref, ys_ref, g_ref, lnw_ref, lnb_ref, *rest, tm, group_tiles):
    o_refs = rest[:len(group_tiles)]
    ybuf, sem = rest[len(group_tiles):]
    i = pl.program_id(0)
    n_i = pl.num_programs(0)
    slot = i % 2

    def gather(tile, buf_slot):
        def body(r, _):
            d = _class_row(packed_ref[tile * tm + r], rs_ref)
            pltpu.make_async_copy(ys_ref.at[pl.ds(pl.multiple_of(d * TOK_ROWS, TOK_ROWS), TOK_ROWS)],
                                  ybuf.at[buf_slot, pl.ds(pl.multiple_of(r * TOK_ROWS, TOK_ROWS), TOK_ROWS)],
                                  sem.at[buf_slot]).start()
            return 0
        lax.fori_loop(0, tm, body, 0, unroll=8)

    @pl.when(i == 0)
    def _():
        gather(0, 0)

    @pl.when(i + 1 < n_i)
    def _():
        gather(jnp.minimum(i + 1, n_i - 1), 1 - slot)

    pltpu.make_async_copy(ys_ref.at[pl.ds(0, tm * TOK_ROWS)], ybuf.at[slot], sem.at[slot]).wait()
    y = _layer_norm(ALPHA * x_ref[...] + g_ref[...] * _load_token_tiles(ybuf, tm, slot), lnw_ref[...], lnb_ref[...])
    first = 0
    for o_ref, nt in zip(o_refs, group_tiles):
        @pl.when(jnp.logical_and(i >= first, i < first + nt))
        def _(o_ref=o_ref):
            o_ref[...] = y
        first += nt


def _fin(x1, ys, packed, row_start, mod, layer, ln_w, ln_b, cond_row, group_rows, tm=256):
    group_tiles = tuple(n // tm for n in group_rows)
    row = pl.BlockSpec((tm, D_MODEL), lambda i, *_: (i, 0))
    vec = pl.BlockSpec((1, D_MODEL), lambda i, *_: (0, 0))
    out_specs, first = [], 0
    for nt in group_tiles:
        out_specs.append(_group_spec(D_MODEL, tm, first, nt))
        first += nt
    grid_spec = pltpu.PrefetchScalarGridSpec(
        num_scalar_prefetch=2,
        grid=(sum(group_tiles),),
        in_specs=[row, pl.BlockSpec(memory_space=pl.ANY), _mod_spec(layer, 5, lambda i: cond_row(i, tm)), vec, vec],
        out_specs=out_specs,
        scratch_shapes=[pltpu.VMEM((2, tm * TOK_ROWS, LANES), F32), pltpu.SemaphoreType.DMA((2,))],
    )
    return pl.pallas_call(
        functools.partial(_fin_kernel, tm=tm, group_tiles=group_tiles),
        grid_spec=grid_spec,
        out_shape=[jax.ShapeDtypeStruct((n, D_MODEL), F32) for n in group_rows],
        compiler_params=_cparams(("arbitrary",)),
        name="post_moe",
    )(packed, row_start, x1, ys, mod, ln_w.reshape(1, -1), ln_b.reshape(1, -1))


def _filt_kernel(z_ref, w1_ref, b1_ref, w2_ref, b2_ref, fr_ref, w3_ref, dl_ref, o_ref):
    i = pl.program_id(0)
    z = z_ref[...]
    fr = fr_ref[...]
    a = jnp.sin(fr * (jnp.dot(z, w1_ref[...], precision=HIGHEST, preferred_element_type=F32) + b1_ref[...]))
    a = jnp.sin(fr * (jnp.dot(a, w2_ref[...], precision=HIGHEST, preferred_element_type=F32) + b2_ref[...]))
    filt = _dot_3pass(a, w3_ref[...])
    window = jnp.exp(-z[:, 0:1] * dl_ref[...]) + HY_SHIFT
    rows = i * z.shape[0] + lax.broadcasted_iota(jnp.int32, filt.shape, 0)
    o_ref[...] = jnp.where(rows == 0, 0.0, filt * window)


def _hyena_taps(seq_len, w1, b1, w2, b2, w3, freq):
    L = seq_len
    t = jnp.linspace(0.0, 1.0, L, dtype=F32)[:, None]
    bands = jnp.linspace(1e-4, HY_BANDS - 1, HY_BANDS, dtype=F32)
    ang = 2.0 * math.pi * bands[None, :] * jnp.arange(L, dtype=F32)[:, None] / L
    z = jnp.concatenate([t, jnp.cos(ang), -jnp.sin(ang)], axis=-1)
    offs = np.minimum(np.abs(np.arange(2 * L) - L), L - 1)
    emb = z.shape[1]
    z2 = jnp.zeros((2 * L, LANES), F32).at[:, :emb].set(z[offs])
    pad_c = lambda a: jnp.zeros((a.shape[0], LANES), F32).at[:, :a.shape[1]].set(a)
    pad_r = lambda a: jnp.zeros((LANES, a.shape[1]), F32).at[:a.shape[0], :].set(a)
    w1p = pad_r(pad_c(w1))
    w2p = pad_r(pad_c(w2))
    w3p = pad_r(w3)
    b1p, b2p, frp = pad_c(b1[None, :]), pad_c(b2[None, :]), pad_c(freq[None, :])
    deltas = jnp.abs(jnp.linspace(math.log(HY_DECAY_TARGET) / HY_SLOW, math.log(HY_DECAY_TARGET) / HY_FAST,
                                  D_MODEL, dtype=F32))[None, :]
    rb = min(512, L)
    cbf = D_MODEL
    ncb = D_MODEL // cbf
    nrb_back = L // rb
    sq = pl.BlockSpec((LANES, LANES), lambda i, j: (0, 0))
    vec = pl.BlockSpec((1, LANES), lambda i, j: (0, 0))
    return pl.pallas_call(
        _filt_kernel,
        grid=(2 * L // rb, ncb),
        in_specs=[
            pl.BlockSpec((rb, LANES), lambda i, j: (i, 0)),
            sq, vec, sq, vec, vec,
            pl.BlockSpec((LANES, cbf), lambda i, j: (0, jnp.where(i < nrb_back, ncb + j, j))),
            pl.BlockSpec((1, cbf), lambda i, j: (0, j)),
        ],
        out_specs=pl.BlockSpec((rb, cbf), lambda i, j: (i, j)),
        out_shape=jax.ShapeDtypeStruct((2 * L, D_MODEL), F32),
        compiler_params=_cparams(("arbitrary", "arbitrary")),
        name="hyena_taps",
    )(z2, w1p, b1p, w2p, b2p, frp, w3p, deltas)


def _dft_mats(cb):
    n = 2 * cb
    m = np.arange(cb)
    f = np.arange(cb)
    ang = 2.0 * np.pi * ((f[:, None] * m[None, :]) % n) / n
    fwd = np.concatenate([np.cos(ang), -np.sin(ang)], axis=0)
    fwd[cb, :] = np.where(m % 2 == 0, 1.0, -1.0)
    coef = np.where(f == 0, 1.0, 2.0)[None, :] / n
    inv = np.concatenate([coef * np.cos(ang.T), -coef * np.sin(ang.T)], axis=1)
    inv[:, cb] = np.where(m % 2 == 0, 1.0, -1.0) / n
    return fwd.astype(np.float32), inv.astype(np.float32)


def _hconv_kernel(x0_ref, x1_ref, v_ref, cw0_ref, cw1_ref, cw2_ref, cb0_ref, cb1_ref, cb2_ref,
                  taps_ref, fb_ref, fwd_ref, inv_ref, o_ref,
                  hs_ref, stage_ref, w32_ref, w_ref, x0c_ref, u_ref, y_ref, *, seq_len, blk, n_seq):
    L = seq_len
    nb = L // blk
    cw = x0_ref.shape[1]
    RC = 256
    bi = pl.program_id(1)
    row0 = lax.broadcasted_iota(jnp.int32, (RC, cw), 0) == 0

    @pl.when(bi == 0)
    def _():
        rows = lax.broadcasted_iota(jnp.int32, (2 * blk, 1), 0)
        sign = jnp.where(rows % 2 == 0, 1.0, -1.0)
        real_row = rows <= blk
        prev = None
        for jb in range(2 * nb):
            cur = jnp.dot(fwd_ref[...], taps_ref[jb * blk:(jb + 1) * blk, :].astype(BF16),
                          preferred_element_type=F32)
            if prev is not None:
                first_tap = taps_ref[(jb - 1) * blk:(jb - 1) * blk + 1, :].astype(BF16).astype(F32)
                hs_ref[jb - 1] = cur + sign * (prev - jnp.where(real_row, first_tap, 0.0))
            prev = cur

    zeros8 = jnp.zeros((8, cw), F32)
    for s in range(n_seq):
        r0 = s * L

        def short_conv(k, src_ref, cw_ref, cb_ref, store):
            stage_ref[k, 0:8, :] = zeros8
            stage_ref[k, 8 + L:16 + L, :] = zeros8
            for c in range(L // RC):
                stage_ref[k, 8 + c * RC:8 + (c + 1) * RC, :] = \
                    src_ref[r0 + c * RC:r0 + (c + 1) * RC, :].astype(F32)
            w = cw_ref[...]
            for c in range(L // RC):
                lo = stage_ref[k, 7 + c * RC:7 + (c + 1) * RC, :]
                mid = stage_ref[k, 8 + c * RC:8 + (c + 1) * RC, :]
                hi = stage_ref[k, 9 + c * RC:9 + (c + 1) * RC, :]
                store(c, lo * w[0:1, :] + mid * w[1:2, :] + hi * w[2:3, :] + cb_ref[...])

        def st_x1(c, val):
            w32_ref[c * RC:(c + 1) * RC, :] = val

        def st_v(c, val):
            w_ref[c * RC:(c + 1) * RC, :] = (w32_ref[c * RC:(c + 1) * RC, :] * val).astype(BF16)

        def st_x0(c, val):
            x0c_ref[c * RC:(c + 1) * RC, :] = val.astype(BF16)

        short_conv(0, x1_ref, cw1_ref, cb1_ref, st_x1)
        short_conv(1, v_ref, cw2_ref, cb2_ref, st_v)
        short_conv(2, x0_ref, cw0_ref, cb0_ref, st_x0)

        for j in range(nb):
            u_ref[j] = jnp.dot(fwd_ref[...], w_ref[j * blk:(j + 1) * blk, :], preferred_element_type=F32)

        for i in range(nb):
            for c in range(blk // RC):
                re = None
                im = None
                for j in range(nb):
                    k = i - j + nb - 1
                    a = u_ref[j, c * RC:(c + 1) * RC, :]
                    b = u_ref[j, blk + c * RC:blk + (c + 1) * RC, :]
                    hr = hs_ref[k, c * RC:(c + 1) * RC, :]
                    hi = hs_ref[k, blk + c * RC:blk + (c + 1) * RC, :]
                    bb = b * hi
                    if c == 0:
                        t_re = a * hr - jnp.where(row0, 0.0, bb)
                        t_im = jnp.where(row0, bb, a * hi + b * hr)
                    else:
                        t_re = a * hr - bb
                        t_im = a * hi + b * hr
                    re = t_re if re is None else re + t_re
                    im = t_im if im is None else im + t_im
                y_ref[i, c * RC:(c + 1) * RC, :] = re.astype(BF16)
                y_ref[i, blk + c * RC:blk + (c + 1) * RC, :] = im.astype(BF16)
            conv = jnp.dot(inv_ref[...], y_ref[i], preferred_element_type=F32)
            sl = slice(i * blk, (i + 1) * blk)
            z = x0c_ref[sl, :].astype(F32) * (conv + w_ref[sl, :].astype(F32) * fb_ref[...])
            o_ref[r0 + i * blk:r0 + (i + 1) * blk, :] = z.astype(BF16)


def _hyena_conv(u, conv_w, conv_b, taps, filt_bias, batch, seq_len, blk, n_seq, cw=256):
    T = u.shape[0]
    L = seq_len
    nb = L // blk
    ncw = D_MODEL // cw
    fwd_np, inv_np = _dft_mats(blk)
    fwd = jnp.asarray(fwd_np).astype(BF16)
    inv = jnp.asarray(inv_np).astype(BF16)
    rows = n_seq * L
    sec = lambda s: pl.BlockSpec((rows, cw), lambda c, b: (b, s * ncw + c))
    cws = lambda s: pl.BlockSpec((3, cw), lambda c, b: (0, s * ncw + c))
    cbs = lambda s: pl.BlockSpec((1, cw), lambda c, b: (0, s * ncw + c))
    return pl.pallas_call(
        functools.partial(_hconv_kernel, seq_len=L, blk=blk, n_seq=n_seq),
        grid=(ncw, batch // n_seq),
        in_specs=[sec(0), sec(1), sec(2), cws(0), cws(1), cws(2), cbs(0), cbs(1), cbs(2),
                  pl.BlockSpec((2 * L, cw), lambda c, b: (0, c), pipeline_mode=pl.Buffered(1)),
                  pl.BlockSpec((1, cw), lambda c, b: (0, c)),
                  pl.BlockSpec((2 * blk, blk), lambda c, b: (0, 0), pipeline_mode=pl.Buffered(1)),
                  pl.BlockSpec((blk, 2 * blk), lambda c, b: (0, 0), pipeline_mode=pl.Buffered(1))],
        out_specs=pl.BlockSpec((rows, cw), lambda c, b: (b, c)),
        out_shape=jax.ShapeDtypeStruct((T, D_MODEL), BF16),
        scratch_shapes=[
            pltpu.VMEM((2 * nb - 1, 2 * blk, cw), F32),
            pltpu.VMEM((3, L + 16, cw), F32),
            pltpu.VMEM((L, cw), F32),
            pltpu.VMEM((L, cw), BF16),
            pltpu.VMEM((L, cw), BF16),
            pltpu.VMEM((nb, 2 * blk, cw), F32),
            pltpu.VMEM((nb, 2 * blk, cw), BF16),
        ],
        compiler_params=_cparams(("arbitrary", "arbitrary")),
        name="hyena_conv",
    )(u, u, u, conv_w, conv_w, conv_w, conv_b.reshape(1, -1), conv_b.reshape(1, -1), conv_b.reshape(1, -1),
      taps, filt_bias.reshape(1, -1), fwd, inv)


def kernel(x_prompt, x_sample, cache_diff_k, cache_diff_v, state_ret_fwd, state_ret_bwd, c, c_ctx, ada_w, ada_b, ln_w, ln_b, ev_w_in, ev_w_out, ret_decay_fwd, ret_decay_bwd, ret_gn_w, diff_lambda, diff_subln_w, hy_w_in, hy_conv_w, hy_conv_b, hy_ffn_w1, hy_ffn_b1, hy_ffn_w2, hy_ffn_b2, hy_ffn_w3, hy_freq, hy_filter_bias, hy_w_out, router_w, router_bias, moe_w_gate, moe_w_up, moe_w_down):
    B, S, D = x_prompt.shape
    DB, DS, _ = x_sample.shape
    PAST = cache_diff_k.shape[2]
    TP, TS = B * S, DB * DS
    T_ALL = TP + TS
    assert D == D_MODEL and 1 + DB <= COND_ROWS and T_ALL < 65536

    cond8 = jnp.zeros((COND_ROWS, D), F32).at[0].set(c_ctx).at[1:1 + DB].set(c)
    mod = _ada_mod(cond8, ada_w, ada_b).reshape(DEPTH * COND_ROWS * N_MOD, 1, D)

    tm = 512
    groups = [
        dict(x=x_prompt.reshape(TP, D), batch=B, seq=S, off=0, cond_row=lambda i, tm: 0),
        dict(x=x_sample.reshape(TS, D), batch=DB, seq=DS, off=TP, cond_row=lambda i, tm: 1 + (i * tm) // DS),
    ]

    def cond_row_all(i, tm):
        return jnp.where(i * tm < TP, 0, 1 + (i * tm - TP) // DS)

    rw_pad = jnp.zeros((D, LANES), F32).at[:, :N_EXPERTS].set(router_w.astype(F32))
    rw_hi = rw_pad.astype(BF16)
    rw_lo = (rw_pad - rw_hi.astype(F32)).astype(BF16)
    rwt = router_w.T.astype(BF16)
    rbias = router_bias.astype(F32)
    n_tiles = T_ALL // MOE_TM + N_CLASSES
    outs = {}

    for l in range(DEPTH):
        mixes = []
        if l % 2 == 0:
            e = l // 2
            w_in = ev_w_in[e].astype(BF16)
            w_out = ev_w_out[e].astype(BF16)
            lg = jnp.stack([jnp.log1p(-jnp.exp2(ret_decay_fwd[e].astype(F32))),
                            jnp.log1p(-jnp.exp2(ret_decay_bwd[e].astype(F32)))])
            lam_init = 0.8 - 0.6 * math.exp(-0.3 * l)
            lq1, lk1, lq2, lk2 = diff_lambda[e].astype(F32)
            lam = (jnp.exp(jnp.sum(lq1 * lk1)) - jnp.exp(jnp.sum(lq2 * lk2)) + lam_init).reshape(1)
            kscale = RET_DK ** -0.5
            for gi, g in enumerate(groups):
                crow = functools.partial(g["cond_row"], tm=tm)
                if gi == 0:
                    secs = (("none", 1.0, False), ("none", kscale, False), ("none", 1.0, False),
                            ("none", 1.0, False), ("none", 1.0, False), ("none", 1.0, True),
                            ("none", 1.0, True))
                    proj, kd, vd = _in_proj(g["x"], mod, l, w_in, secs, crow, tm, tm=tm)
                    outs.setdefault("kd", []).append(kd.reshape(B, 1, S, H_DIFF, 2 * DIFF_QK))
                    outs.setdefault("vd", []).append(vd.reshape(B, 1, S, H_DIFF, DIFF_V))
                    ret, sf, sb = _retention(proj, lg, ret_gn_w[e], g["batch"], g["seq"], write_state=True)
                    outs.setdefault("sf", []).append(sf.reshape(B, 1, H_RET, RET_DK, RET_DV))
                    outs.setdefault("sb", []).append(sb.reshape(B, 1, H_RET, RET_DK, RET_DV))
                    att = _diff_attention(proj, lam, diff_subln_w[e], lam_init, g["batch"], g["seq"],
                                          tq=min(256, g["seq"]))
                else:
                    secs = (("ret", 1.0, False), ("ret", kscale, False), ("none", 1.0, False),
                            ("none", 1.0, False), ("diff", 1.0, False), ("diff", 1.0, False),
                            ("none", 1.0, False))
                    tabs = {"ret": _rope_tables(g["seq"], RET_DK), "diff": _rope_tables(g["seq"], DIFF_QK)}
                    (proj,) = _in_proj(g["x"], mod, l, w_in, secs, crow, g["seq"], rope_tabs=tabs, tm=tm)
                    ret, = _retention(proj, lg, ret_gn_w[e], g["batch"], g["seq"],
                                      states=(state_ret_fwd[:, e], state_ret_bwd[:, e]))
                    ctx = (cache_diff_k[:, e].reshape(DB, PAST, H_DIFF * 2 * DIFF_QK),
                           cache_diff_v[:, e].reshape(DB, PAST, H_DIFF * DIFF_V))
                    att = _diff_attention(proj, lam, diff_subln_w[e], lam_init, g["batch"], g["seq"], ctx=ctx,
                                          tq=512)
                mixes.append((ret, att))
        else:
            o = l // 2
            w_in = hy_w_in[o].astype(BF16)
            w_out = hy_w_out[o].astype(BF16)
            secs = (("none", 1.0, False),) * (3 * D // SEC)
            for gi, g in enumerate(groups):
                crow = functools.partial(g["cond_row"], tm=tm)
                (u,) = _in_proj(g["x"], mod, l, w_in, secs, crow, tm, tm=tm)
                taps = _hyena_taps(g["seq"], hy_ffn_w1[o], hy_ffn_b1[o], hy_ffn_w2[o], hy_ffn_b2[o],
                                   hy_ffn_w3[o], hy_freq[o])
                blk = min(g["seq"], 1024)
                n_seq = max(1, 1024 // g["seq"])
                z = _hyena_conv(u, hy_conv_w[o], hy_conv_b[o], taps, hy_filter_bias[o], g["batch"], g["seq"],
                                blk, n_seq)
                mixes.append((z,))

        x1_all, h2_all, info_all, counts = _post_mix(mixes, [g["x"] for g in groups], w_out, mod, l, ln_w[l, 0],
                                                     ln_b[l, 0], rw_hi, rw_lo, rbias, cond_row_all, tm)
        packed = info_all[0]
        counts_i = counts[:, 0].astype(jnp.int32)
        plan = _moe_plan(counts_i, n_tiles)
        hs = _dispatch(h2_all, packed, plan, counts_i, n_tiles)
        ys = _moe(hs, plan, l, rwt, moe_w_gate, moe_w_up, moe_w_down)
        xs = _fin(x1_all, ys, packed, plan[3], mod, l, ln_w[l, 1], ln_b[l, 1], cond_row_all, (TP, TS))
        for g, x in zip(groups, xs):
            g["x"] = x

    y_prompt = groups[0]["x"].reshape(B, S, D)
    y_sample = groups[1]["x"].reshape(DB, DS, D)
    cat = lambda xs: xs[0] if len(xs) == 1 else jnp.concatenate(xs, axis=1)
    return (y_prompt, y_sample, cat(outs["kd"]), cat(outs["vd"]), cat(outs["sf"]), cat(outs["sb"]))
```

```python
import functools
import math

import numpy as np
import jax
import jax.numpy as jnp
from jax import lax
from jax.experimental import pallas as pl
from jax.experimental.pallas import tpu as pltpu

F32 = jnp.float32
BF16 = jnp.bfloat16
HIGHEST = lax.Precision.HIGHEST

D_MODEL = 1024
DEPTH = 2
GRID_W = 64
H_RET = 4
RET_DK = 128
RET_DV = 128
RET_CHUNK = 128
H_DIFF = 4
DIFF_QK = 64
DIFF_V = 128
ROPE_BASE = 10000.0
HY_BANDS = 16
HY_FH = 64
HY_DECAY_TARGET = 1e-2
HY_FAST = 0.3
HY_SLOW = 1.5
HY_SHIFT = 0.05
N_EXPERTS = 16
N_GROUPS = 4
GROUP_SIZE = N_EXPERTS // N_GROUPS
D_FF_EXPERT = 512
ALPHA = (2 * DEPTH) ** 0.25
LN_EPS = 1e-5

LANES = 128
SEC = 512
COND_ROWS = 8
N_MOD = 6
VMEM_LIMIT = 50 * 1024 * 1024

PAIR_SLOTS = ((0, 1), (2, 1), (2, 3), (0, 3), (0, 2), (1, 3))
N_PAIRS = len(PAIR_SLOTS)
N_CLASSES = N_GROUPS * N_PAIRS
MOE_TM = 256
MOE_SHIFT = MOE_TM.bit_length() - 1
RET_HEADS_PER_STEP = 2
POST_SUB = 512
TOK_ROWS = D_MODEL // LANES


def _cparams(sem):
    return pltpu.CompilerParams(dimension_semantics=sem, vmem_limit_bytes=VMEM_LIMIT)


def _silu(x):
    return x * jax.nn.sigmoid(x)


def _store_token_tiles(ref, x):
    n = x.shape[0]
    for j in range(TOK_ROWS):
        ref[pl.ds(j, n, stride=TOK_ROWS), :] = x[:, j * LANES:(j + 1) * LANES]


def _load_token_tiles(ref, n, slot=None):
    idx = () if slot is None else (slot,)
    return jnp.concatenate([ref[idx + (pl.ds(j, n, stride=TOK_ROWS), slice(None))] for j in range(TOK_ROWS)],
                           axis=1)


def _layer_norm(x, w, b):
    mu = jnp.mean(x, axis=-1, keepdims=True)
    xc = x - mu
    var = jnp.mean(xc * xc, axis=-1, keepdims=True)
    return xc * lax.rsqrt(var + LN_EPS) * w + b


def _dot_3pass(a, w):
    a_hi = a.astype(BF16)
    a_lo = (a - a_hi.astype(F32)).astype(BF16)
    w_hi = w.astype(BF16)
    w_lo = (w - w_hi.astype(F32)).astype(BF16)
    return (jnp.dot(a_hi, w_hi, preferred_element_type=F32) + jnp.dot(a_lo, w_hi, preferred_element_type=F32)
            + jnp.dot(a_hi, w_lo, preferred_element_type=F32))


def _ada_kernel(c_ref, w_ref, b_ref, o_ref):
    o_ref[...] = _dot_3pass(_silu(c_ref[...]), w_ref[...]) + b_ref[...]


def _ada_mod(cond8, ada_w, ada_b):
    tn = 1024
    nj = ada_w.shape[2] // tn
    return pl.pallas_call(
        _ada_kernel,
        grid=(DEPTH, nj),
        in_specs=[
            pl.BlockSpec((COND_ROWS, D_MODEL), lambda l, j: (0, 0)),
            pl.BlockSpec((None, D_MODEL, tn), lambda l, j: (l, 0, j)),
            pl.BlockSpec((None, 1, tn), lambda l, j: (l, 0, j)),
        ],
        out_specs=pl.BlockSpec((None, COND_ROWS, tn), lambda l, j: (l, 0, j)),
        out_shape=jax.ShapeDtypeStruct((DEPTH, COND_ROWS, ada_w.shape[2]), F32),
        compiler_params=_cparams(("arbitrary", "arbitrary")),
        name="ada_mod",
    )(cond8, ada_w, ada_b.reshape(DEPTH, 1, -1))


def _mod_spec(layer, chunk, row_of_tile):
    def imap(i, *_):
        return ((layer * COND_ROWS + row_of_tile(i)) * N_MOD + chunk, 0, 0)
    return pl.BlockSpec((None, 1, D_MODEL), imap)


def _rope(a, tabs, quarter):
    c, sa, sb = tabs
    out = []
    for hb in range(a.shape[1] // LANES):
        blk = a[:, hb * LANES:(hb + 1) * LANES]
        up = pltpu.roll(blk, LANES - quarter, axis=1)
        dn = pltpu.roll(blk, quarter, axis=1)
        out.append(blk * c + up * sa + dn * sb)
    return jnp.concatenate(out, axis=1)


def _in_kernel(*refs, secs, n_f32_out):
    x_ref, sh_ref, sc_ref, w_ref = refs[:4]
    pos = 4
    tabs = {}
    for kind in ("ret", "diff"):
        if any(s[0] == kind for s in secs):
            tabs[kind] = tuple(r[...] for r in refs[pos:pos + 3])
            pos += 3
    o_ref = refs[pos]
    f32_refs = refs[pos + 1:]
    h = (x_ref[...] * (1.0 + sc_ref[...]) + sh_ref[...]).astype(BF16)
    k32 = 0
    for s, (kind, scale, want_f32) in enumerate(secs):
        acc = jnp.dot(h, w_ref[:, s * SEC:(s + 1) * SEC], preferred_element_type=F32)
        if scale != 1.0:
            acc = acc * scale
        if kind == "ret":
            acc = _rope(acc, tabs["ret"], RET_DK // 4)
        elif kind == "diff":
            acc = _rope(acc, tabs["diff"], DIFF_QK // 4)
        o_ref[:, s * SEC:(s + 1) * SEC] = acc.astype(BF16)
        if want_f32:
            for hb in range(SEC // LANES):
                f32_refs[k32][:, hb, :] = acc[:, hb * LANES:(hb + 1) * LANES]
            k32 += 1
    assert k32 == n_f32_out


def _in_proj(x2d, mod, layer, w_bf16, secs, row_of_tile, seq_len, rope_tabs=None, tm=512):
    T = x2d.shape[0]
    N = w_bf16.shape[1]
    assert N == SEC * len(secs) and T % tm == 0 and seq_len % tm == 0
    tiles_per_seq = seq_len // tm
    in_specs = [
        pl.BlockSpec((tm, D_MODEL), lambda i: (i, 0)),
        _mod_spec(layer, 0, row_of_tile),
        _mod_spec(layer, 1, row_of_tile),
        pl.BlockSpec((D_MODEL, N), lambda i: (0, 0)),
    ]
    args = [x2d, mod, mod, w_bf16]
    for kind in ("ret", "diff"):
        if any(s[0] == kind for s in secs):
            for t in rope_tabs[kind]:
                in_specs.append(pl.BlockSpec((tm, LANES), lambda i: (i % tiles_per_seq, 0)))
                args.append(t)
    n_f32 = sum(1 for s in secs if s[2])
    heads = SEC // LANES
    out_shape = [jax.ShapeDtypeStruct((T, N), BF16)] + [jax.ShapeDtypeStruct((T, heads, LANES), F32)] * n_f32
    out_specs = ([pl.BlockSpec((tm, N), lambda i: (i, 0))]
                 + [pl.BlockSpec((tm, heads, LANES), lambda i: (i, 0, 0))] * n_f32)
    return pl.pallas_call(
        functools.partial(_in_kernel, secs=secs, n_f32_out=n_f32),
        grid=(T // tm,),
        in_specs=in_specs,
        out_specs=out_specs,
        out_shape=out_shape,
        compiler_params=_cparams(("arbitrary",)),
        name="in_proj",
    )(*args)


def _rope_tables(seq_len, d):
    half = d // 2
    quarter = half // 2
    t = np.arange(seq_len)
    inv = ROPE_BASE ** (-np.arange(quarter, dtype=np.float64) / quarter)
    ang_r = (t // GRID_W)[:, None] * inv[None, :]
    ang_c = (t % GRID_W)[:, None] * inv[None, :]
    zero = np.zeros_like(ang_r)
    cos = np.concatenate([np.cos(ang_r)] * 2 + [np.cos(ang_c)] * 2, axis=1)
    sa = np.concatenate([-np.sin(ang_r), zero, -np.sin(ang_c), zero], axis=1)
    sb = np.concatenate([zero, np.sin(ang_r), zero, np.sin(ang_c)], axis=1)
    reps = LANES // d
    return tuple(jnp.asarray(np.tile(a, (1, reps)).astype(np.float32)) for a in (cos, sa, sb))


def _ret_kernel(*refs, n_chunks, has_state, write_state):
    lg_ref, q_ref, k_ref, v_ref, g_ref, gnw_ref = refs[:6]
    pos = 6
    if has_state:
        s0f_ref, s0b_ref = refs[pos:pos + 2]
        pos += 2
    o_ref = refs[pos]
    pos += 1
    if write_state:
        sf_ref, sb_ref = refs[pos:pos + 2]
        pos += 2
    acc_ref, kv_ref = refs[pos:pos + 2]
    for hh in range(RET_HEADS_PER_STEP):
        _ret_head(hh, lg_ref, q_ref, k_ref, v_ref, g_ref, gnw_ref,
                  (s0f_ref, s0b_ref) if has_state else None, o_ref, (sf_ref, sb_ref) if write_state else None,
                  acc_ref, kv_ref, n_chunks)


def _ret_head(hh, lg_ref, q_ref, k_ref, v_ref, g_ref, gnw_ref, s0_refs, o_ref, s_out_refs, acc_ref, kv_ref,
              n_chunks):
    C = RET_CHUNK
    cols = slice(hh * LANES, (hh + 1) * LANES)
    hd = pl.program_id(1) * RET_HEADS_PER_STEP + hh
    lgf = lg_ref[0, hd]
    lgb = lg_ref[1, hd]
    ii = lax.broadcasted_iota(jnp.int32, (C, C), 0).astype(F32)
    jj = lax.broadcasted_iota(jnp.int32, (C, C), 1).astype(F32)
    rel = ii - jj
    d_f = jnp.where(rel >= 0, jnp.exp(jnp.maximum(rel, 0.0) * lgf), 0.0)
    d_b = jnp.where(rel <= 0, jnp.exp(jnp.maximum(-rel, 0.0) * lgb), 0.0)
    d_sum = d_f + d_b
    idx = lax.broadcasted_iota(jnp.int32, (C, 1), 0).astype(F32)
    xi_f = jnp.exp((idx + 1.0) * lgf)
    zeta_f = jnp.exp((C - 1.0 - idx) * lgf)
    xi_b = jnp.exp((C - idx) * lgb)
    zeta_b = jnp.exp(idx * lgb)
    one = jnp.ones((1, 1), F32)
    gc_f = jnp.exp(one * (C * lgf))
    gc_b = jnp.exp(one * (C * lgb))

    nt = (((1,), (1,)), ((), ()))
    tn = (((0,), (0,)), ((), ()))

    if s0_refs is not None:
        s_f = s0_refs[0][hh]
        s_b = s0_refs[1][hh]
    else:
        s_f = jnp.zeros((RET_DK, RET_DV), F32)
        s_b = jnp.zeros((RET_DK, RET_DV), F32)

    for n in range(n_chunks):
        sl = slice(n * C, (n + 1) * C)
        qc, kc, vc = q_ref[sl, cols], k_ref[sl, cols], v_ref[sl, cols]
        scores = lax.dot_general(qc, kc, nt, preferred_element_type=F32) * d_sum
        acc_ref[sl, cols] = jnp.dot(scores.astype(BF16), vc, preferred_element_type=F32)
        kf = kc.astype(F32)
        kv_ref[0, hh, n] = lax.dot_general((kf * zeta_f).astype(BF16), vc, tn, preferred_element_type=F32)
        kv_ref[1, hh, n] = lax.dot_general((kf * zeta_b).astype(BF16), vc, tn, preferred_element_type=F32)

    for n in range(n_chunks):
        sl = slice(n * C, (n + 1) * C)
        qf = q_ref[sl, cols].astype(F32)
        acc_ref[sl, cols] += jnp.dot((qf * xi_f).astype(BF16), s_f.astype(BF16), preferred_element_type=F32)
        s_f = gc_f * s_f + kv_ref[0, hh, n]

    gnw = gnw_ref[:, cols]
    for n in reversed(range(n_chunks)):
        sl = slice(n * C, (n + 1) * C)
        qf = q_ref[sl, cols].astype(F32)
        cross = jnp.dot((qf * xi_b).astype(BF16), s_b.astype(BF16), preferred_element_type=F32)
        r = acc_ref[sl, cols] + cross
        mu = jnp.mean(r, axis=-1, keepdims=True)
        rc = r - mu
        var = jnp.mean(rc * rc, axis=-1, keepdims=True)
        rn = rc * lax.rsqrt(var + LN_EPS) * gnw
        o_ref[sl, cols] = (_silu(g_ref[sl, cols].astype(F32)) * rn).astype(BF16)
        s_b = gc_b * s_b + kv_ref[1, hh, n]

    if s_out_refs is not None:
        s_out_refs[0][hh] = s_f
        s_out_refs[1][hh] = s_b


def _retention(proj, lg, gn_w, batch, seq_len, states=None, write_state=False):
    T = proj.shape[0]
    hp = RET_HEADS_PER_STEP
    width = hp * LANES
    per_sec = SEC // width
    blk = lambda sec: pl.BlockSpec((seq_len, width), lambda b, h: (b, sec * per_sec + h))
    in_specs = [pl.BlockSpec(memory_space=pltpu.SMEM), blk(0), blk(1), blk(2), blk(3),
                pl.BlockSpec((1, width), lambda b, h: (0, h))]
    args = [lg, proj, proj, proj, proj, gn_w.reshape(1, -1)]
    if states is not None:
        st = pl.BlockSpec((None, hp, RET_DK, RET_DV), lambda b, h: (b, h, 0, 0))
        in_specs += [st, st]
        args += list(states)
    out_shape = [jax.ShapeDtypeStruct((T, SEC), BF16)]
    out_specs = [pl.BlockSpec((seq_len, width), lambda b, h: (b, h))]
    if write_state:
        st_o = pl.BlockSpec((None, hp, RET_DK, RET_DV), lambda b, h: (b, h, 0, 0))
        out_shape += [jax.ShapeDtypeStruct((batch, H_RET, RET_DK, RET_DV), F32)] * 2
        out_specs += [st_o, st_o]
    return pl.pallas_call(
        functools.partial(_ret_kernel, n_chunks=seq_len // RET_CHUNK, has_state=states is not None,
                          write_state=write_state),
        grid=(batch, H_RET // hp),
        in_specs=in_specs,
        out_specs=out_specs,
        out_shape=out_shape,
        scratch_shapes=[pltpu.VMEM((seq_len, width), F32),
                        pltpu.VMEM((2, hp, seq_len // RET_CHUNK, RET_DK, RET_DV), F32)],
        compiler_params=_cparams(("arbitrary", "arbitrary")),
        name="retention",
    )(*args)


def _att_kernel(*refs, has_ctx, out_scale, key_chunk):
    lam_ref, q_ref, k_ref, v_ref = refs[:4]
    pos = 4
    if has_ctx:
        ck_ref, cv_ref = refs[pos:pos + 2]
        pos += 2
    w_ref, o_ref = refs[pos:pos + 2]
    lam = lam_ref[0]
    q = q_ref[...]
    tq = q.shape[0]
    lane = lax.broadcasted_iota(jnp.int32, q.shape, 1)
    zero = jnp.zeros_like(q)
    qq = jnp.concatenate([jnp.where(lane < DIFF_QK, q, zero), jnp.where(lane >= DIFF_QK, q, zero)], axis=0)
    qq = qq * jnp.asarray(DIFF_QK ** -0.5, BF16)
    nt = (((1,), (1,)), ((), ()))
    chunks = [(k_ref, v_ref, c * key_chunk, key_chunk) for c in range(k_ref.shape[0] // key_chunk)]
    if has_ctx:
        chunks.append((ck_ref, cv_ref, 0, ck_ref.shape[0]))
    m = l = acc = None
    for kr, vr, off, n in chunks:
        kch = kr[off:off + n, :].astype(BF16)
        vch = vr[off:off + n, :].astype(BF16)
        s = lax.dot_general(qq, kch, nt, preferred_element_type=F32)
        cm = jnp.max(s, axis=-1, keepdims=True)
        m_new = cm if m is None else jnp.maximum(m, cm)
        p = jnp.exp(s - m_new)
        ps = jnp.sum(p, axis=-1, keepdims=True)
        pv = jnp.dot(p.astype(BF16), vch, preferred_element_type=F32)
        if m is None:
            l, acc = ps, pv
        else:
            alpha = jnp.exp(m - m_new)
            l = alpha * l + ps
            acc = alpha * acc + pv
        m = m_new
    o = acc / l
    att = o[:tq] - lam * o[tq:]
    att = att * lax.rsqrt(jnp.mean(att * att, axis=-1, keepdims=True) + LN_EPS)
    o_ref[...] = (att * w_ref[...] * out_scale).astype(BF16)


def _diff_attention(proj, lam, subln_w, lam_init, batch, seq_len, ctx=None, tq=256):
    T = proj.shape[0]
    hsec = SEC // LANES
    nq = seq_len // tq
    in_specs = [
        pl.BlockSpec(memory_space=pltpu.SMEM),
        pl.BlockSpec((tq, LANES), lambda b, h, i: (b * nq + i, 4 * hsec + h)),
        pl.BlockSpec((seq_len, LANES), lambda b, h, i: (b, 5 * hsec + h)),
        pl.BlockSpec((seq_len, LANES), lambda b, h, i: (b, 6 * hsec + h)),
    ]
    args = [lam, proj, proj, proj]
    if ctx is not None:
        ck, cv = ctx
        past = ck.shape[1]
        cspec = pl.BlockSpec((None, past, LANES), lambda b, h, i: (b, 0, h))
        in_specs += [cspec, cspec]
        args += [ck, cv]
    in_specs.append(pl.BlockSpec((1, LANES), lambda b, h, i: (0, 0)))
    args.append(subln_w.reshape(1, -1))
    return pl.pallas_call(
        functools.partial(_att_kernel, has_ctx=ctx is not None, out_scale=1.0 - lam_init,
                          key_chunk=min(512, seq_len)),
        grid=(batch, H_DIFF, nq),
        in_specs=in_specs,
        out_specs=pl.BlockSpec((tq, LANES), lambda b, h, i: (b * nq + i, h)),
        out_shape=jax.ShapeDtypeStruct((T, SEC), BF16),
        compiler_params=_cparams(("arbitrary", "arbitrary", "arbitrary")),
        name="diff_attention",
    )(*args)


def _route_class(lt, rb_ref):
    sel = [jax.nn.sigmoid(lt[e:e + 1, :]) + rb_ref[e] for e in range(N_EXPERTS)]
    gscore = []
    for g in range(N_GROUPS):
        mem = sel[g * GROUP_SIZE:(g + 1) * GROUP_SIZE]
        best = None
        for a in range(GROUP_SIZE):
            for b in range(a + 1, GROUP_SIZE):
                pair = mem[a] + mem[b]
                best = pair if best is None else jnp.maximum(best, pair)
        gscore.append(best)
    gbest = gscore[0]
    gidx = jnp.zeros_like(gbest)
    for g in range(1, N_GROUPS):
        upd = gscore[g] > gbest
        gidx = jnp.where(upd, float(g), gidx)
        gbest = jnp.where(upd, gscore[g], gbest)
    msel = []
    for j in range(GROUP_SIZE):
        out = sel[j]
        for g in range(1, N_GROUPS):
            out = jnp.where(gidx == float(g), sel[g * GROUP_SIZE + j], out)
        msel.append(out)
    one = jnp.ones_like(gbest)
    zero = jnp.zeros_like(gbest)
    chosen = []
    for j in range(GROUP_SIZE):
        rank = zero
        for k in range(GROUP_SIZE):
            if k < j:
                rank = rank + jnp.where(msel[k] >= msel[j], one, zero)
            elif k > j:
                rank = rank + jnp.where(msel[k] > msel[j], one, zero)
        chosen.append(jnp.where(rank < 2.0, one, zero))
    c0, c1, c2, c3 = chosen
    order = jnp.where(c0 * c1 > 0, 0.0, jnp.where(c1 * c2 > 0, 1.0, jnp.where(c2 * c3 > 0, 2.0,
            jnp.where(c0 * c3 > 0, 3.0, jnp.where(c0 * c2 > 0, 4.0, 5.0)))))
    return gidx * float(N_PAIRS) + order


def _post_kernel(*refs, n_mix, group_tiles):
    n_groups = len(group_tiles)
    per_group = n_mix + 1
    group_refs = [refs[g * per_group:(g + 1) * per_group] for g in range(n_groups)]
    refs = refs[n_groups * per_group:]
    (w_ref, g_ref, sh_ref, sc_ref, lnw_ref, lnb_ref, rwh_ref, rwl_ref, rb_ref, tri_ref,
     x1_ref, h2_ref, info_ref, cout_ref, cnt_ref, pre_ref) = refs
    i = pl.program_id(0)

    @pl.when(i == 0)
    def _():
        cnt_ref[...] = jnp.zeros_like(cnt_ref)

    first = 0
    for g in range(n_groups):
        @pl.when(jnp.logical_and(i >= first, i < first + group_tiles[g]))
        def _(g=g):
            out = None
            off = 0
            for m_ref in group_refs[g][:n_mix]:
                width = m_ref.shape[1]
                part = jnp.dot(m_ref[...], w_ref[off:off + width, :], preferred_element_type=F32)
                out = part if out is None else out + part
                off += width
            pre_ref[...] = ALPHA * group_refs[g][n_mix][...] + g_ref[...] * out
        first += group_tiles[g]

    tm = pre_ref.shape[0]
    ts = tri_ref.shape[0]
    base = cnt_ref[:, 0:1]
    for part in range(tm // ts):
        rows = slice(part * ts, (part + 1) * ts)
        x1 = _layer_norm(pre_ref[rows, :], lnw_ref[...], lnb_ref[...])
        x1_ref[rows, :] = x1
        h2 = x1 * (1.0 + sc_ref[...]) + sh_ref[...]
        for j in range(TOK_ROWS):
            h2_ref[pl.ds(part * ts * TOK_ROWS + j, ts, stride=TOK_ROWS), :] = h2[:, j * LANES:(j + 1) * LANES]
        h_hi = h2.astype(BF16)
        h_lo = (h2 - h_hi.astype(F32)).astype(BF16)
        logits = (jnp.dot(h_hi, rwh_ref[...], preferred_element_type=F32)
                  + jnp.dot(h_lo, rwh_ref[...], preferred_element_type=F32)
                  + jnp.dot(h_hi, rwl_ref[...], preferred_element_type=F32))
        cls = _route_class(logits.T, rb_ref)
        crow = lax.broadcasted_iota(jnp.int32, (32, ts), 0).astype(F32)
        onehot = jnp.where(crow == cls, 1.0, 0.0)
        prefix = jnp.dot(onehot.astype(BF16), tri_ref[...], preferred_element_type=F32)
        rank = jnp.sum(onehot * (prefix - 1.0 + base), axis=0, keepdims=True)
        base = base + jnp.sum(onehot, axis=1, keepdims=True)
        packed = cls.astype(jnp.int32) * 65536 + rank.astype(jnp.int32)
        info_ref[:, rows] = packed
    cnt_ref[...] = jnp.broadcast_to(base, cnt_ref.shape)

    @pl.when(i == pl.num_programs(0) - 1)
    def _():
        cout_ref[...] = cnt_ref[...]


def _tri(tm):
    return jnp.asarray(np.triu(np.ones((tm, tm), np.float32))).astype(BF16)


def _group_spec(width, tm, first_tile, n_tiles, array_tile_off=0):
    return pl.BlockSpec((tm, width),
                        lambda i, *_: (jnp.clip(i - first_tile, 0, n_tiles - 1) + array_tile_off, 0))


def _post_mix(group_mixes, group_x, w_out_bf16, mod, layer, ln_w, ln_b, rw_hi, rw_lo, router_bias, cond_row, tm):
    group_tiles = tuple(m[0].shape[0] // tm for m in group_mixes)
    n_tiles = sum(group_tiles)
    total_rows = n_tiles * tm
    row = pl.BlockSpec((tm, D_MODEL), lambda i: (i, 0))
    vec = pl.BlockSpec((1, D_MODEL), lambda i: (0, 0))
    cnt = pl.BlockSpec((32, LANES), lambda i: (0, 0))
    in_specs, args = [], []
    first = 0
    for mixes, (x, x_row_off), nt in zip(group_mixes, group_x, group_tiles):
        for m in mixes:
            in_specs.append(_group_spec(m.shape[1], tm, first, nt))
            args.append(m)
        in_specs.append(_group_spec(D_MODEL, tm, first, nt, x_row_off // tm))
        args.append(x)
        first += nt
    in_specs += [
        pl.BlockSpec((D_MODEL, D_MODEL), lambda i: (0, 0)),
        _mod_spec(layer, 2, lambda i: cond_row(i, tm)), _mod_spec(layer, 3, lambda i: cond_row(i, tm)),
        _mod_spec(layer, 4, lambda i: cond_row(i, tm)),
        vec, vec,
        pl.BlockSpec((D_MODEL, LANES), lambda i: (0, 0)),
        pl.BlockSpec((D_MODEL, LANES), lambda i: (0, 0)),
        pl.BlockSpec(memory_space=pltpu.SMEM),
        pl.BlockSpec((POST_SUB, POST_SUB), lambda i: (0, 0)),
    ]
    args += [w_out_bf16, mod, mod, mod, ln_w.reshape(1, -1), ln_b.reshape(1, -1), rw_hi, rw_lo, router_bias,
             _tri(POST_SUB)]
    return pl.pallas_call(
        functools.partial(_post_kernel, n_mix=len(group_mixes[0]), group_tiles=group_tiles),
        grid=(n_tiles,),
        in_specs=in_specs,
        out_specs=[row, pl.BlockSpec((tm * TOK_ROWS, LANES), lambda i: (i, 0)),
                   pl.BlockSpec((1, tm), lambda i: (0, i)), cnt],
        out_shape=[jax.ShapeDtypeStruct((total_rows, D_MODEL), F32),
                   jax.ShapeDtypeStruct((total_rows * TOK_ROWS, LANES), F32),
                   jax.ShapeDtypeStruct((1, total_rows), jnp.int32), jax.ShapeDtypeStruct((32, LANES), F32)],
        scratch_shapes=[pltpu.VMEM((32, LANES), F32), pltpu.VMEM((tm, D_MODEL), F32)],
        compiler_params=_cparams(("arbitrary",)),
        name="post_mix",
    )(*args)


def _plan_kernel(cnt_ref, ea_ref, eb_ref, fl_ref, rs_ref, *, n_tiles):
    start = jnp.int32(0)
    prev_a = jnp.int32(-1)
    prev_b = jnp.int32(-1)
    for c in range(N_CLASSES):
        n = cnt_ref[c]
        tiles = lax.shift_right_logical(n + (MOE_TM - 1), MOE_SHIFT)
        row0 = start * MOE_TM
        rs_ref[c] = row0
        g, pr = divmod(c, N_PAIRS)
        a = g * GROUP_SIZE + PAIR_SLOTS[pr][0]
        b = g * GROUP_SIZE + PAIR_SLOTS[pr][1]
        first = 1 + 4 * (prev_a != a).astype(jnp.int32) + 8 * (prev_b != b).astype(jnp.int32)

        def tile_body(k, _, start=start, a=a, b=b, first=first):
            t = start + k
            ea_ref[t] = a
            eb_ref[t] = b
            fl_ref[t] = jnp.where(k == 0, first, 1)
            return 0

        lax.fori_loop(0, tiles, tile_body, 0)
        has = tiles > 0
        prev_a = jnp.where(has, a, prev_a)
        prev_b = jnp.where(has, b, prev_b)
        start = start + tiles
    for c in range(N_CLASSES, 31):
        rs_ref[c] = 0
    rs_ref[31] = start

    def idle_body(t, _):
        ea_ref[t] = prev_a
        eb_ref[t] = prev_b
        fl_ref[t] = 0
        return 0

    lax.fori_loop(start, n_tiles, idle_body, 0)


def _moe_plan(counts, n_tiles):
    smem = pl.BlockSpec(memory_space=pltpu.SMEM)
    i32 = lambda n: jax.ShapeDtypeStruct((n,), jnp.int32)
    return pl.pallas_call(
        functools.partial(_plan_kernel, n_tiles=n_tiles),
        in_specs=[smem],
        out_specs=[smem] * 4,
        out_shape=[i32(n_tiles), i32(n_tiles), i32(n_tiles), i32(32)],
        name="moe_plan",
    )(counts)


def _class_row(packed, rs_ref):
    return rs_ref[lax.shift_right_logical(packed, 16)] + (packed & 0xFFFF)


def _dispatch_kernel(packed_ref, rs_ref, cnt_ref, h_ref, hs_ref, sem, fill_sem, *, tm, n_tiles):
    i = pl.program_id(0)
    used = rs_ref[31]

    def tile_rows(row, n=1):
        return pl.ds(pl.multiple_of(row * TOK_ROWS, TOK_ROWS), n * TOK_ROWS)

    def fill_copy(row):
        return pltpu.make_async_copy(h_ref.at[tile_rows(0)], hs_ref.at[tile_rows(row)], fill_sem)

    def idle_copy(t):
        return pltpu.make_async_copy(h_ref.at[tile_rows(0, MOE_TM)], hs_ref.at[tile_rows(t * MOE_TM, MOE_TM)],
                                     fill_sem)

    def class_pad(c):
        n = cnt_ref[c]
        first = rs_ref[c] + n
        last = rs_ref[c] + lax.shift_left(lax.shift_right_logical(n + (MOE_TM - 1), MOE_SHIFT), MOE_SHIFT)
        return first, last

    @pl.when(i == 0)
    def _():
        for c in range(N_CLASSES):
            first, last = class_pad(c)
            lax.fori_loop(first, last, lambda r, _: (fill_copy(r).start(), 0)[1], 0)
        lax.fori_loop(used, n_tiles, lambda t, _: (idle_copy(t).start(), 0)[1], 0)

    def body(r8, _):
        for j in range(8):
            r = r8 * 8 + j
            row = _class_row(packed_ref[i * tm + r], rs_ref)
            pltpu.make_async_copy(h_ref.at[tile_rows(r)], hs_ref.at[tile_rows(row)], sem).start(priority=j % 2)
        return 0

    lax.fori_loop(0, tm // 8, body, 0)
    pltpu.make_async_copy(h_ref, hs_ref.at[tile_rows(0, tm)], sem).wait()

    @pl.when(i == 0)
    def _():
        for c in range(N_CLASSES):
            first, last = class_pad(c)
            lax.fori_loop(first, last, lambda r, _: (fill_copy(r).wait(), 0)[1], 0)
        lax.fori_loop(used, n_tiles, lambda t, _: (idle_copy(t).wait(), 0)[1], 0)


def _dispatch(h2_tiles, packed, plan, counts, n_tiles, tm=1024):
    T = packed.shape[0]
    assert tm >= MOE_TM and T % tm == 0
    grid_spec = pltpu.PrefetchScalarGridSpec(
        num_scalar_prefetch=3,
        grid=(T // tm,),
        in_specs=[pl.BlockSpec((tm * TOK_ROWS, LANES), lambda i, *_: (i, 0))],
        out_specs=pl.BlockSpec(memory_space=pl.ANY),
        scratch_shapes=[pltpu.SemaphoreType.DMA(()), pltpu.SemaphoreType.DMA(())],
    )
    return pl.pallas_call(
        functools.partial(_dispatch_kernel, tm=tm, n_tiles=n_tiles),
        grid_spec=grid_spec,
        out_shape=jax.ShapeDtypeStruct((n_tiles * MOE_TM * TOK_ROWS, LANES), F32),
        compiler_params=_cparams(("arbitrary",)),
        name="moe_dispatch",
    )(packed, plan[3], counts, h2_tiles)


def _moe_kernel(ea_ref, eb_ref, fl_ref, h_ref, rwt_ref, wga, wua, wda, wgb, wub, wdb, o_ref,
                sga, sua, sda, sgb, sub, sdb):
    t = pl.program_id(0)
    flags = fl_ref[t]
    valid = (flags & 1) != 0

    @pl.when((flags & 4) != 0)
    def _():
        sga[...] = wga[...].astype(BF16)
        sua[...] = wua[...].astype(BF16)
        sda[...] = wda[...].astype(BF16)

    @pl.when((flags & 8) != 0)
    def _():
        sgb[...] = wgb[...].astype(BF16)
        sub[...] = wub[...].astype(BF16)
        sdb[...] = wdb[...].astype(BF16)

    @pl.when(valid)
    def _():
        h = _load_token_tiles(h_ref, MOE_TM).astype(BF16)
        nt = (((1,), (1,)), ((), ()))
        score = jax.nn.sigmoid(lax.dot_general(h, rwt_ref[...], nt, preferred_element_type=F32))
        lane = lax.broadcasted_iota(jnp.int32, score.shape, 1)
        s_a = jnp.sum(jnp.where(lane == ea_ref[t], score, 0.0), axis=1, keepdims=True)
        s_b = jnp.sum(jnp.where(lane == eb_ref[t], score, 0.0), axis=1, keepdims=True)
        tot = s_a + s_b
        act_a = _silu(jnp.dot(h, sga[...], preferred_element_type=F32)) \
            * jnp.dot(h, sua[...], preferred_element_type=F32) * (s_a / tot)
        act_b = _silu(jnp.dot(h, sgb[...], preferred_element_type=F32)) \
            * jnp.dot(h, sub[...], preferred_element_type=F32) * (s_b / tot)
        _store_token_tiles(o_ref, jnp.dot(act_a.astype(BF16), sda[...], preferred_element_type=F32)
                           + jnp.dot(act_b.astype(BF16), sdb[...], preferred_element_type=F32))

    @pl.when(jnp.logical_not(valid))
    def _():
        o_ref[...] = jnp.zeros_like(o_ref)


def _moe(hs, plan, layer, rwt_bf16, w_gate, w_up, w_down):
    ea, eb, flags, _ = plan
    n_tiles = ea.shape[0]
    wspec = lambda shape, which: pl.BlockSpec(
        (None, None) + shape, (lambda t, ea, eb, fl: (layer, ea[t], 0, 0)) if which == 0
        else (lambda t, ea, eb, fl: (layer, eb[t], 0, 0)))
    up = (D_MODEL, D_FF_EXPERT)
    dn = (D_FF_EXPERT, D_MODEL)
    tile = pl.BlockSpec((MOE_TM * TOK_ROWS, LANES), lambda t, ea, eb, fl: (t, 0))
    grid_spec = pltpu.PrefetchScalarGridSpec(
        num_scalar_prefetch=3,
        grid=(n_tiles,),
        in_specs=[
            tile,
            pl.BlockSpec((N_EXPERTS, D_MODEL), lambda t, ea, eb, fl: (0, 0)),
            wspec(up, 0), wspec(up, 0), wspec(dn, 0), wspec(up, 1), wspec(up, 1), wspec(dn, 1),
        ],
        out_specs=tile,
        scratch_shapes=[pltpu.VMEM(up, BF16), pltpu.VMEM(up, BF16), pltpu.VMEM(dn, BF16),
                        pltpu.VMEM(up, BF16), pltpu.VMEM(up, BF16), pltpu.VMEM(dn, BF16)],
    )
    return pl.pallas_call(
        _moe_kernel,
        grid_spec=grid_spec,
        out_shape=jax.ShapeDtypeStruct((n_tiles * MOE_TM * TOK_ROWS, LANES), F32),
        compiler_params=_cparams(("arbitrary",)),
        name="moe",
    )(ea, eb, flags, hs, rwt_bf16, w_gate, w_up, w_down, w_gate, w_up, w_down)


def _gather_expert_rows(packed_ref, rs_ref, ys_ref, ybuf, sem, tm):
    i = pl.program_id(0)
    n_i = pl.num_programs(0)
    slot = i % 2

    def gather(tile, buf_slot):
        def body(r, _):
            d = _class_row(packed_ref[tile * tm + r], rs_ref)
            pltpu.make_async_copy(ys_ref.at[pl.ds(pl.multiple_of(d * TOK_ROWS, TOK_ROWS), TOK_ROWS)],
                                  ybuf.at[buf_slot, pl.ds(pl.multiple_of(r * TOK_ROWS, TOK_ROWS), TOK_ROWS)],
                                  sem.at[buf_slot]).start()
            return 0
        lax.fori_loop(0, tm, body, 0, unroll=8)

    @pl.when(i == 0)
    def _():
        gather(0, 0)

    @pl.when(i + 1 < n_i)
    def _():
        gather(jnp.minimum(i + 1, n_i - 1), 1 - slot)

    pltpu.make_async_copy(ys_ref.at[pl.ds(0, tm * TOK_ROWS)], ybuf.at[slot], sem.at[slot]).wait()
    return _load_token_tiles(ybuf, tm, slot)


def _fin_proj_kernel(packed_ref, rs_ref, x_ref, ys_ref, g_ref, lnw_ref, lnb_ref, sh_ref, sc_ref, w_ref,
                     x2_ref, u_ref, ybuf, sem, *, tm):
    y = _gather_expert_rows(packed_ref, rs_ref, ys_ref, ybuf, sem, tm)
    x2 = _layer_norm(ALPHA * x_ref[...] + g_ref[...] * y, lnw_ref[...], lnb_ref[...])
    x2_ref[...] = x2
    h = (x2 * (1.0 + sc_ref[...]) + sh_ref[...]).astype(BF16)
    for s in range(u_ref.shape[1] // SEC):
        u_ref[:, s * SEC:(s + 1) * SEC] = jnp.dot(h, w_ref[:, s * SEC:(s + 1) * SEC],
                                                  preferred_element_type=F32).astype(BF16)


def _fin_proj(x1, ys, packed, row_start, mod, layer, ln_w, ln_b, w_next_bf16, cond_row, tm=512):
    T = x1.shape[0]
    N = w_next_bf16.shape[1]
    row = pl.BlockSpec((tm, D_MODEL), lambda i, *_: (i, 0))
    vec = pl.BlockSpec((1, D_MODEL), lambda i, *_: (0, 0))
    crow = lambda i: cond_row(i, tm)
    grid_spec = pltpu.PrefetchScalarGridSpec(
        num_scalar_prefetch=2,
        grid=(T // tm,),
        in_specs=[row, pl.BlockSpec(memory_space=pl.ANY), _mod_spec(layer, 5, crow), vec, vec,
                  _mod_spec(layer + 1, 0, crow), _mod_spec(layer + 1, 1, crow),
                  pl.BlockSpec((D_MODEL, N), lambda i, *_: (0, 0))],
        out_specs=[row, pl.BlockSpec((tm, N), lambda i, *_: (i, 0))],
        scratch_shapes=[pltpu.VMEM((2, tm * TOK_ROWS, LANES), F32), pltpu.SemaphoreType.DMA((2,))],
    )
    return pl.pallas_call(
        functools.partial(_fin_proj_kernel, tm=tm),
        grid_spec=grid_spec,
        out_shape=[jax.ShapeDtypeStruct((T, D_MODEL), F32), jax.ShapeDtypeStruct((T, N), BF16)],
        compiler_params=_cparams(("arbitrary",)),
        name="post_moe_in_proj",
    )(packed, row_start, x1, ys, mod, ln_w.reshape(1, -1), ln_b.reshape(1, -1), mod, mod, w_next_bf16)


def _fin_kernel(packed_ref, rs_ref, x_ref, ys_ref, g_ref, lnw_ref, lnb_ref, *rest, tm, group_tiles):
    o_refs = rest[:len(group_tiles)]
    ybuf, sem = rest[len(group_tiles):]
    i = pl.program_id(0)
    y = _layer_norm(ALPHA * x_ref[...] + g_ref[...] * _gather_expert_rows(packed_ref, rs_ref, ys_ref, ybuf, sem, tm),
                    lnw_ref[...], lnb_ref[...])
    first = 0
    for o_ref, nt in zip(o_refs, group_tiles):
        @pl.when(jnp.logical_and(i >= first, i < first + nt))
        def _(o_ref=o_ref):
            o_ref[...] = y
        first += nt


def _fin(x1, ys, packed, row_start, mod, layer, ln_w, ln_b, cond_row, group_rows, tm=256):
    group_tiles = tuple(n // tm for n in group_rows)
    row = pl.BlockSpec((tm, D_MODEL), lambda i, *_: (i, 0))
    vec = pl.BlockSpec((1, D_MODEL), lambda i, *_: (0, 0))
    out_specs, first = [], 0
    for nt in group_tiles:
        out_specs.append(_group_spec(D_MODEL, tm, first, nt))
        first += nt
    grid_spec = pltpu.PrefetchScalarGridSpec(
        num_scalar_prefetch=2,
        grid=(sum(group_tiles),),
        in_specs=[row, pl.BlockSpec(memory_space=pl.ANY), _mod_spec(layer, 5, lambda i: cond_row(i, tm)), vec, vec],
        out_specs=out_specs,
        scratch_shapes=[pltpu.VMEM((2, tm * TOK_ROWS, LANES), F32), pltpu.SemaphoreType.DMA((2,))],
    )
    return pl.pallas_call(
        functools.partial(_fin_kernel, tm=tm, group_tiles=group_tiles),
        grid_spec=grid_spec,
        out_shape=[jax.ShapeDtypeStruct((n, D_MODEL), F32) for n in group_rows],
        compiler_params=_cparams(("arbitrary",)),
        name="post_moe",
    )(packed, row_start, x1, ys, mod, ln_w.reshape(1, -1), ln_b.reshape(1, -1))


def _filt_kernel(z_ref, w1_ref, b1_ref, w2_ref, b2_ref, fr_ref, w3_ref, dl_ref, o_ref):
    i = pl.program_id(0)
    z = z_ref[...]
    fr = fr_ref[...]
    a = jnp.sin(fr * (jnp.dot(z, w1_ref[...], precision=HIGHEST, preferred_element_type=F32) + b1_ref[...]))
    a = jnp.sin(fr * (jnp.dot(a, w2_ref[...], precision=HIGHEST, preferred_element_type=F32) + b2_ref[...]))
    filt = _dot_3pass(a, w3_ref[...])
    window = jnp.exp(-z[:, 0:1] * dl_ref[...]) + HY_SHIFT
    rows = i * z.shape[0] + lax.broadcasted_iota(jnp.int32, filt.shape, 0)
    o_ref[...] = jnp.where(rows == 0, 0.0, filt * window)


def _hyena_taps(seq_len, w1, b1, w2, b2, w3, freq):
    L = seq_len
    t = np.linspace(0.0, 1.0, L)[:, None]
    bands = np.linspace(1e-4, HY_BANDS - 1, HY_BANDS)
    ang = 2.0 * math.pi * bands[None, :] * np.arange(L)[:, None] / L
    z = np.concatenate([t, np.cos(ang), -np.sin(ang)], axis=-1)
    offs = np.minimum(np.abs(np.arange(2 * L) - L), L - 1)
    z2_np = np.zeros((2 * L, LANES), np.float32)
    z2_np[:, :z.shape[1]] = z[offs]
    z2 = jnp.asarray(z2_np)
    pad_c = lambda a: jnp.zeros((a.shape[0], LANES), F32).at[:, :a.shape[1]].set(a)
    pad_r = lambda a: jnp.zeros((LANES, a.shape[1]), F32).at[:a.shape[0], :].set(a)
    w1p = pad_r(pad_c(w1))
    w2p = pad_r(pad_c(w2))
    w3p = pad_r(w3)
    b1p, b2p, frp = pad_c(b1[None, :]), pad_c(b2[None, :]), pad_c(freq[None, :])
    deltas = jnp.asarray(np.abs(np.linspace(math.log(HY_DECAY_TARGET) / HY_SLOW, math.log(HY_DECAY_TARGET) / HY_FAST,
                                            D_MODEL))[None, :].astype(np.float32))
    rb = min(512, L)
    cbf = D_MODEL
    ncb = D_MODEL // cbf
    nrb_back = L // rb
    sq = pl.BlockSpec((LANES, LANES), lambda i, j: (0, 0))
    vec = pl.BlockSpec((1, LANES), lambda i, j: (0, 0))
    return pl.pallas_call(
        _filt_kernel,
        grid=(2 * L // rb, ncb),
        in_specs=[
            pl.BlockSpec((rb, LANES), lambda i, j: (i, 0)),
            sq, vec, sq, vec, vec,
            pl.BlockSpec((LANES, cbf), lambda i, j: (0, jnp.where(i < nrb_back, ncb + j, j))),
            pl.BlockSpec((1, cbf), lambda i, j: (0, j)),
        ],
        out_specs=pl.BlockSpec((rb, cbf), lambda i, j: (i, j)),
        out_shape=jax.ShapeDtypeStruct((2 * L, D_MODEL), F32),
        compiler_params=_cparams(("arbitrary", "arbitrary")),
        name="hyena_taps",
    )(z2, w1p, b1p, w2p, b2p, frp, w3p, deltas)


def _dft_mats(cb):
    n = 2 * cb
    m = np.arange(cb)
    f = np.arange(cb)
    ang = 2.0 * np.pi * ((f[:, None] * m[None, :]) % n) / n
    fwd = np.concatenate([np.cos(ang), -np.sin(ang)], axis=0)
    fwd[cb, :] = np.where(m % 2 == 0, 1.0, -1.0)
    coef = np.where(f == 0, 1.0, 2.0)[None, :] / n
    inv = np.concatenate([coef * np.cos(ang.T), -coef * np.sin(ang.T)], axis=1)
    inv[:, cb] = np.where(m % 2 == 0, 1.0, -1.0) / n
    return fwd.astype(np.float32), inv.astype(np.float32)


def _hconv_kernel(x0_ref, x1_ref, v_ref, cw0_ref, cw1_ref, cw2_ref, cb0_ref, cb1_ref, cb2_ref,
                  taps_ref, fb_ref, fwd_ref, inv_ref, o_ref,
                  hs_ref, stage_ref, w32_ref, w_ref, x0c_ref, u_ref, y_ref, *, seq_len, blk, n_seq):
    L = seq_len
    nb = L // blk
    cw = x0_ref.shape[1]
    RC = 256
    bi = pl.program_id(1)
    row0 = lax.broadcasted_iota(jnp.int32, (RC, cw), 0) == 0

    @pl.when(bi == 0)
    def _():
        rows = lax.broadcasted_iota(jnp.int32, (2 * blk, 1), 0)
        sign = jnp.where(rows % 2 == 0, 1.0, -1.0)
        real_row = rows <= blk
        prev = None
        for jb in range(2 * nb):
            cur = jnp.dot(fwd_ref[...], taps_ref[jb * blk:(jb + 1) * blk, :].astype(BF16),
                          preferred_element_type=F32)
            if prev is not None:
                first_tap = taps_ref[(jb - 1) * blk:(jb - 1) * blk + 1, :].astype(BF16).astype(F32)
                hs_ref[jb - 1] = cur + sign * (prev - jnp.where(real_row, first_tap, 0.0))
            prev = cur

    zeros8 = jnp.zeros((8, cw), F32)
    for s in range(n_seq):
        r0 = s * L

        def short_conv(k, src_ref, cw_ref, cb_ref, store):
            stage_ref[k, 0:8, :] = zeros8
            stage_ref[k, 8 + L:16 + L, :] = zeros8
            for c in range(L // RC):
                stage_ref[k, 8 + c * RC:8 + (c + 1) * RC, :] = \
                    src_ref[r0 + c * RC:r0 + (c + 1) * RC, :].astype(F32)
            w = cw_ref[...]
            for c in range(L // RC):
                lo = stage_ref[k, 7 + c * RC:7 + (c + 1) * RC, :]
                mid = stage_ref[k, 8 + c * RC:8 + (c + 1) * RC, :]
                hi = stage_ref[k, 9 + c * RC:9 + (c + 1) * RC, :]
                store(c, lo * w[0:1, :] + mid * w[1:2, :] + hi * w[2:3, :] + cb_ref[...])

        def st_x1(c, val):
            w32_ref[c * RC:(c + 1) * RC, :] = val

        def st_v(c, val):
            w_ref[c * RC:(c + 1) * RC, :] = (w32_ref[c * RC:(c + 1) * RC, :] * val).astype(BF16)

        def st_x0(c, val):
            x0c_ref[c * RC:(c + 1) * RC, :] = val.astype(BF16)

        short_conv(0, x1_ref, cw1_ref, cb1_ref, st_x1)
        short_conv(1, v_ref, cw2_ref, cb2_ref, st_v)
        short_conv(2, x0_ref, cw0_ref, cb0_ref, st_x0)

        for j in range(nb):
            u_ref[j] = jnp.dot(fwd_ref[...], w_ref[j * blk:(j + 1) * blk, :], preferred_element_type=F32)

        for i in range(nb):
            for c in range(blk // RC):
                re = None
                im = None
                for j in range(nb):
                    k = i - j + nb - 1
                    a = u_ref[j, c * RC:(c + 1) * RC, :]
                    b = u_ref[j, blk + c * RC:blk + (c + 1) * RC, :]
                    hr = hs_ref[k, c * RC:(c + 1) * RC, :]
                    hi = hs_ref[k, blk + c * RC:blk + (c + 1) * RC, :]
                    bb = b * hi
                    if c == 0:
                        t_re = a * hr - jnp.where(row0, 0.0, bb)
                        t_im = jnp.where(row0, bb, a * hi + b * hr)
                    else:
                        t_re = a * hr - bb
                        t_im = a * hi + b * hr
                    re = t_re if re is None else re + t_re
                    im = t_im if im is None else im + t_im
                y_ref[i, c * RC:(c + 1) * RC, :] = re.astype(BF16)
                y_ref[i, blk + c * RC:blk + (c + 1) * RC, :] = im.astype(BF16)
            conv = jnp.dot(inv_ref[...], y_ref[i], preferred_element_type=F32)
            sl = slice(i * blk, (i + 1) * blk)
            z = x0c_ref[sl, :].astype(F32) * (conv + w_ref[sl, :].astype(F32) * fb_ref[...])
            o_ref[r0 + i * blk:r0 + (i + 1) * blk, :] = z.astype(BF16)


def _hyena_conv(u, conv_w, conv_b, taps, filt_bias, batch, seq_len, blk, n_seq, row_off=0, cw=256):
    T = batch * seq_len
    L = seq_len
    nb = L // blk
    ncw = D_MODEL // cw
    fwd_np, inv_np = _dft_mats(blk)
    fwd = jnp.asarray(fwd_np).astype(BF16)
    inv = jnp.asarray(inv_np).astype(BF16)
    rows = n_seq * L
    assert row_off % rows == 0
    boff = row_off // rows
    sec = lambda s: pl.BlockSpec((rows, cw), lambda c, b: (b + boff, s * ncw + c))
    cws = lambda s: pl.BlockSpec((3, cw), lambda c, b: (0, s * ncw + c))
    cbs = lambda s: pl.BlockSpec((1, cw), lambda c, b: (0, s * ncw + c))
    return pl.pallas_call(
        functools.partial(_hconv_kernel, seq_len=L, blk=blk, n_seq=n_seq),
        grid=(ncw, batch // n_seq),
        in_specs=[sec(0), sec(1), sec(2), cws(0), cws(1), cws(2), cbs(0), cbs(1), cbs(2),
                  pl.BlockSpec((2 * L, cw), lambda c, b: (0, c), pipeline_mode=pl.Buffered(1)),
                  pl.BlockSpec((1, cw), lambda c, b: (0, c)),
                  pl.BlockSpec((2 * blk, blk), lambda c, b: (0, 0), pipeline_mode=pl.Buffered(1)),
                  pl.BlockSpec((blk, 2 * blk), lambda c, b: (0, 0), pipeline_mode=pl.Buffered(1))],
        out_specs=pl.BlockSpec((rows, cw), lambda c, b: (b, c)),
        out_shape=jax.ShapeDtypeStruct((T, D_MODEL), BF16),
        scratch_shapes=[
            pltpu.VMEM((2 * nb - 1, 2 * blk, cw), F32),
            pltpu.VMEM((3, L + 16, cw), F32),
            pltpu.VMEM((L, cw), F32),
            pltpu.VMEM((L, cw), BF16),
            pltpu.VMEM((L, cw), BF16),
            pltpu.VMEM((nb, 2 * blk, cw), F32),
            pltpu.VMEM((nb, 2 * blk, cw), BF16),
        ],
        compiler_params=_cparams(("arbitrary", "arbitrary")),
        name="hyena_conv",
    )(u, u, u, conv_w, conv_w, conv_w, conv_b.reshape(1, -1), conv_b.reshape(1, -1), conv_b.reshape(1, -1),
      taps, filt_bias.reshape(1, -1), fwd, inv)


def kernel(x_prompt, x_sample, cache_diff_k, cache_diff_v, state_ret_fwd, state_ret_bwd, c, c_ctx, ada_w, ada_b, ln_w, ln_b, ev_w_in, ev_w_out, ret_decay_fwd, ret_decay_bwd, ret_gn_w, diff_lambda, diff_subln_w, hy_w_in, hy_conv_w, hy_conv_b, hy_ffn_w1, hy_ffn_b1, hy_ffn_w2, hy_ffn_b2, hy_ffn_w3, hy_freq, hy_filter_bias, hy_w_out, router_w, router_bias, moe_w_gate, moe_w_up, moe_w_down):
    B, S, D = x_prompt.shape
    DB, DS, _ = x_sample.shape
    PAST = cache_diff_k.shape[2]
    TP, TS = B * S, DB * DS
    T_ALL = TP + TS
    assert D == D_MODEL and 1 + DB <= COND_ROWS and T_ALL < 65536

    cond8 = jnp.zeros((COND_ROWS, D), F32).at[0].set(c_ctx).at[1:1 + DB].set(c)
    mod = _ada_mod(cond8, ada_w, ada_b).reshape(DEPTH * COND_ROWS * N_MOD, 1, D)

    tm = 512
    groups = [
        dict(x=x_prompt.reshape(TP, D), x_off=0, batch=B, seq=S, off=0, cond_row=lambda i, tm: 0),
        dict(x=x_sample.reshape(TS, D), x_off=0, batch=DB, seq=DS, off=TP,
             cond_row=lambda i, tm: 1 + (i * tm) // DS),
    ]
    u_next = None

    def cond_row_all(i, tm):
        return jnp.where(i * tm < TP, 0, 1 + (i * tm - TP) // DS)

    rw_pad = jnp.zeros((D, LANES), F32).at[:, :N_EXPERTS].set(router_w.astype(F32))
    rw_hi = rw_pad.astype(BF16)
    rw_lo = (rw_pad - rw_hi.astype(F32)).astype(BF16)
    rwt = router_w.T.astype(BF16)
    rbias = router_bias.astype(F32)
    n_tiles = T_ALL // MOE_TM + N_CLASSES
    outs = {}

    for l in range(DEPTH):
        mixes = []
        if l % 2 == 0:
            e = l // 2
            w_in = ev_w_in[e].astype(BF16)
            w_out = ev_w_out[e].astype(BF16)
            lg = jnp.stack([jnp.log1p(-jnp.exp2(ret_decay_fwd[e].astype(F32))),
                            jnp.log1p(-jnp.exp2(ret_decay_bwd[e].astype(F32)))])
            lam_init = 0.8 - 0.6 * math.exp(-0.3 * l)
            lq1, lk1, lq2, lk2 = diff_lambda[e].astype(F32)
            lam = (jnp.exp(jnp.sum(lq1 * lk1)) - jnp.exp(jnp.sum(lq2 * lk2)) + lam_init).reshape(1)
            kscale = RET_DK ** -0.5
            for gi, g in enumerate(groups):
                crow = functools.partial(g["cond_row"], tm=tm)
                if gi == 0:
                    secs = (("none", 1.0, False), ("none", kscale, False), ("none", 1.0, False),
                            ("none", 1.0, False), ("none", 1.0, False), ("none", 1.0, True),
                            ("none", 1.0, True))
                    proj, kd, vd = _in_proj(g["x"], mod, l, w_in, secs, crow, tm, tm=tm)
                    outs.setdefault("kd", []).append(kd.reshape(B, 1, S, H_DIFF, 2 * DIFF_QK))
                    outs.setdefault("vd", []).append(vd.reshape(B, 1, S, H_DIFF, DIFF_V))
                    ret, sf, sb = _retention(proj, lg, ret_gn_w[e], g["batch"], g["seq"], write_state=True)
                    outs.setdefault("sf", []).append(sf.reshape(B, 1, H_RET, RET_DK, RET_DV))
                    outs.setdefault("sb", []).append(sb.reshape(B, 1, H_RET, RET_DK, RET_DV))
                    att = _diff_attention(proj, lam, diff_subln_w[e], lam_init, g["batch"], g["seq"],
                                          tq=min(256, g["seq"]))
                else:
                    secs = (("ret", 1.0, False), ("ret", kscale, False), ("none", 1.0, False),
                            ("none", 1.0, False), ("diff", 1.0, False), ("diff", 1.0, False),
                            ("none", 1.0, False))
                    tabs = {"ret": _rope_tables(g["seq"], RET_DK), "diff": _rope_tables(g["seq"], DIFF_QK)}
                    (proj,) = _in_proj(g["x"], mod, l, w_in, secs, crow, g["seq"], rope_tabs=tabs, tm=tm)
                    ret, = _retention(proj, lg, ret_gn_w[e], g["batch"], g["seq"],
                                      states=(state_ret_fwd[:, e], state_ret_bwd[:, e]))
                    ctx = (cache_diff_k[:, e].reshape(DB, PAST, H_DIFF * 2 * DIFF_QK),
                           cache_diff_v[:, e].reshape(DB, PAST, H_DIFF * DIFF_V))
                    att = _diff_attention(proj, lam, diff_subln_w[e], lam_init, g["batch"], g["seq"], ctx=ctx,
                                          tq=512)
                mixes.append((ret, att))
        else:
            o = l // 2
            w_in = hy_w_in[o].astype(BF16)
            w_out = hy_w_out[o].astype(BF16)
            secs = (("none", 1.0, False),) * (3 * D // SEC)
            for gi, g in enumerate(groups):
                if u_next is None:
                    crow = functools.partial(g["cond_row"], tm=tm)
                    (u,) = _in_proj(g["x"], mod, l, w_in, secs, crow, tm, tm=tm)
                    u_off = 0
                else:
                    u, u_off = u_next, g["off"]
                taps = _hyena_taps(g["seq"], hy_ffn_w1[o], hy_ffn_b1[o], hy_ffn_w2[o], hy_ffn_b2[o],
                                   hy_ffn_w3[o], hy_freq[o])
                blk = min(g["seq"], 1024)
                n_seq = max(1, 1024 // g["seq"])
                z = _hyena_conv(u, hy_conv_w[o], hy_conv_b[o], taps, hy_filter_bias[o], g["batch"], g["seq"],
                                blk, n_seq, row_off=u_off)
                mixes.append((z,))

        x1_all, h2_all, info_all, counts = _post_mix(mixes, [(g["x"], g["x_off"]) for g in groups], w_out, mod, l, ln_w[l, 0],
                                                     ln_b[l, 0], rw_hi, rw_lo, rbias, cond_row_all, tm)
        packed = info_all.reshape(T_ALL)
        counts_i = counts[:, 0].astype(jnp.int32)
        plan = _moe_plan(counts_i, n_tiles)
        hs = _dispatch(h2_all, packed, plan, counts_i, n_tiles)
        ys = _moe(hs, plan, l, rwt, moe_w_gate, moe_w_up, moe_w_down)
        if l + 1 < DEPTH and (l + 1) % 2 == 1:
            x_all, u_next = _fin_proj(x1_all, ys, packed, plan[3], mod, l, ln_w[l, 1], ln_b[l, 1],
                                      hy_w_in[(l + 1) // 2].astype(BF16), cond_row_all, tm=tm)
            for g in groups:
                g["x"], g["x_off"] = x_all, g["off"]
        else:
            u_next = None
            xs = _fin(x1_all, ys, packed, plan[3], mod, l, ln_w[l, 1], ln_b[l, 1], cond_row_all, (TP, TS))
            for g, x in zip(groups, xs):
                g["x"], g["x_off"] = x, 0

    y_prompt = groups[0]["x"].reshape(B, S, D)
    y_sample = groups[1]["x"].reshape(DB, DS, D)
    cat = lambda xs: xs[0] if len(xs) == 1 else jnp.concatenate(xs, axis=1)
    return (y_prompt, y_sample, cat(outs["kd"]), cat(outs["vd"]), cat(outs["sf"]), cat(outs["sb"]))
```

```python
import functools
import math

import numpy as np
import jax
import jax.numpy as jnp
from jax import lax
from jax.experimental import pallas as pl
from jax.experimental.pallas import tpu as pltpu

F32 = jnp.float32
BF16 = jnp.bfloat16
HIGHEST = lax.Precision.HIGHEST

D_MODEL = 1024
DEPTH = 2
GRID_W = 64
H_RET = 4
RET_DK = 128
RET_DV = 128
RET_CHUNK = 128
H_DIFF = 4
DIFF_QK = 64
DIFF_V = 128
ROPE_BASE = 10000.0
HY_BANDS = 16
HY_FH = 64
HY_DECAY_TARGET = 1e-2
HY_FAST = 0.3
HY_SLOW = 1.5
HY_SHIFT = 0.05
N_EXPERTS = 16
N_GROUPS = 4
GROUP_SIZE = N_EXPERTS // N_GROUPS
D_FF_EXPERT = 512
ALPHA = (2 * DEPTH) ** 0.25
LN_EPS = 1e-5

LANES = 128
SEC = 512
COND_ROWS = 8
N_MOD = 6
VMEM_LIMIT = 50 * 1024 * 1024

PAIR_SLOTS = ((0, 1), (2, 1), (2, 3), (0, 3), (0, 2), (1, 3))
N_PAIRS = len(PAIR_SLOTS)
N_CLASSES = N_GROUPS * N_PAIRS
MOE_TM = 256
MOE_SHIFT = MOE_TM.bit_length() - 1
RET_HEADS_PER_STEP = 2
POST_SUB = 512
TOK_ROWS = D_MODEL // LANES


def _cparams(sem):
    return pltpu.CompilerParams(dimension_semantics=sem, vmem_limit_bytes=VMEM_LIMIT)


def _silu(x):
    return x * jax.nn.sigmoid(x)


def _store_token_tiles(ref, x):
    n = x.shape[0]
    for j in range(TOK_ROWS):
        ref[pl.ds(j, n, stride=TOK_ROWS), :] = x[:, j * LANES:(j + 1) * LANES]


def _load_token_tiles(ref, n, slot=None):
    idx = () if slot is None else (slot,)
    return jnp.concatenate([ref[idx + (pl.ds(j, n, stride=TOK_ROWS), slice(None))] for j in range(TOK_ROWS)],
                           axis=1)


def _layer_norm(x, w, b):
    mu = jnp.mean(x, axis=-1, keepdims=True)
    xc = x - mu
    var = jnp.mean(xc * xc, axis=-1, keepdims=True)
    return xc * lax.rsqrt(var + LN_EPS) * w + b


def _dot_3pass(a, w):
    a_hi = a.astype(BF16)
    a_lo = (a - a_hi.astype(F32)).astype(BF16)
    w_hi = w.astype(BF16)
    w_lo = (w - w_hi.astype(F32)).astype(BF16)
    return (jnp.dot(a_hi, w_hi, preferred_element_type=F32) + jnp.dot(a_lo, w_hi, preferred_element_type=F32)
            + jnp.dot(a_hi, w_lo, preferred_element_type=F32))


def _ada_kernel(c_ref, w_ref, b_ref, o_ref):
    o_ref[...] = _dot_3pass(_silu(c_ref[...]), w_ref[...]) + b_ref[...]


def _ada_mod(cond8, ada_w, ada_b):
    tn = 1024
    nj = ada_w.shape[2] // tn
    return pl.pallas_call(
        _ada_kernel,
        grid=(DEPTH, nj),
        in_specs=[
            pl.BlockSpec((COND_ROWS, D_MODEL), lambda l, j: (0, 0)),
            pl.BlockSpec((None, D_MODEL, tn), lambda l, j: (l, 0, j)),
            pl.BlockSpec((None, 1, tn), lambda l, j: (l, 0, j)),
        ],
        out_specs=pl.BlockSpec((None, COND_ROWS, tn), lambda l, j: (l, 0, j)),
        out_shape=jax.ShapeDtypeStruct((DEPTH, COND_ROWS, ada_w.shape[2]), F32),
        compiler_params=_cparams(("arbitrary", "arbitrary")),
        name="ada_mod",
    )(cond8, ada_w, ada_b.reshape(DEPTH, 1, -1))


def _mod_spec(layer, chunk, row_of_tile):
    def imap(i, *_):
        return ((layer * COND_ROWS + row_of_tile(i)) * N_MOD + chunk, 0, 0)
    return pl.BlockSpec((None, 1, D_MODEL), imap)


def _rope(a, tabs, quarter):
    c, sa, sb = tabs
    out = []
    for hb in range(a.shape[1] // LANES):
        blk = a[:, hb * LANES:(hb + 1) * LANES]
        up = pltpu.roll(blk, LANES - quarter, axis=1)
        dn = pltpu.roll(blk, quarter, axis=1)
        out.append(blk * c + up * sa + dn * sb)
    return jnp.concatenate(out, axis=1)


def _in_kernel(*refs, secs, n_f32_out):
    x_ref, sh_ref, sc_ref, w_ref = refs[:4]
    pos = 4
    tabs = {}
    for kind in ("ret", "diff"):
        if any(s[0] == kind for s in secs):
            tabs[kind] = tuple(r[...] for r in refs[pos:pos + 3])
            pos += 3
    o_ref = refs[pos]
    f32_refs = refs[pos + 1:]
    h = (x_ref[...] * (1.0 + sc_ref[...]) + sh_ref[...]).astype(BF16)
    k32 = 0
    for s, (kind, scale, want_f32) in enumerate(secs):
        acc = jnp.dot(h, w_ref[:, s * SEC:(s + 1) * SEC], preferred_element_type=F32)
        if scale != 1.0:
            acc = acc * scale
        if kind == "ret":
            acc = _rope(acc, tabs["ret"], RET_DK // 4)
        elif kind == "diff":
            acc = _rope(acc, tabs["diff"], DIFF_QK // 4)
        o_ref[:, s * SEC:(s + 1) * SEC] = acc.astype(BF16)
        if want_f32:
            for hb in range(SEC // LANES):
                f32_refs[k32][:, hb, :] = acc[:, hb * LANES:(hb + 1) * LANES]
            k32 += 1
    assert k32 == n_f32_out


def _in_proj(x2d, mod, layer, w_bf16, secs, row_of_tile, seq_len, rope_tabs=None, tm=512):
    T = x2d.shape[0]
    N = w_bf16.shape[1]
    assert N == SEC * len(secs) and T % tm == 0 and seq_len % tm == 0
    tiles_per_seq = seq_len // tm
    in_specs = [
        pl.BlockSpec((tm, D_MODEL), lambda i: (i, 0)),
        _mod_spec(layer, 0, row_of_tile),
        _mod_spec(layer, 1, row_of_tile),
        pl.BlockSpec((D_MODEL, N), lambda i: (0, 0), pipeline_mode=pl.Buffered(1)),
    ]
    args = [x2d, mod, mod, w_bf16]
    for kind in ("ret", "diff"):
        if any(s[0] == kind for s in secs):
            for t in rope_tabs[kind]:
                in_specs.append(pl.BlockSpec((tm, LANES), lambda i: (i % tiles_per_seq, 0)))
                args.append(t)
    n_f32 = sum(1 for s in secs if s[2])
    heads = SEC // LANES
    out_shape = [jax.ShapeDtypeStruct((T, N), BF16)] + [jax.ShapeDtypeStruct((T, heads, LANES), F32)] * n_f32
    out_specs = ([pl.BlockSpec((tm, N), lambda i: (i, 0))]
                 + [pl.BlockSpec((tm, heads, LANES), lambda i: (i, 0, 0))] * n_f32)
    return pl.pallas_call(
        functools.partial(_in_kernel, secs=secs, n_f32_out=n_f32),
        grid=(T // tm,),
        in_specs=in_specs,
        out_specs=out_specs,
        out_shape=out_shape,
        compiler_params=_cparams(("arbitrary",)),
        name="in_proj",
    )(*args)


def _rope_tables(seq_len, d):
    half = d // 2
    quarter = half // 2
    t = np.arange(seq_len)
    inv = ROPE_BASE ** (-np.arange(quarter, dtype=np.float64) / quarter)
    ang_r = (t // GRID_W)[:, None] * inv[None, :]
    ang_c = (t % GRID_W)[:, None] * inv[None, :]
    zero = np.zeros_like(ang_r)
    cos = np.concatenate([np.cos(ang_r)] * 2 + [np.cos(ang_c)] * 2, axis=1)
    sa = np.concatenate([-np.sin(ang_r), zero, -np.sin(ang_c), zero], axis=1)
    sb = np.concatenate([zero, np.sin(ang_r), zero, np.sin(ang_c)], axis=1)
    reps = LANES // d
    return tuple(jnp.asarray(np.tile(a, (1, reps)).astype(np.float32)) for a in (cos, sa, sb))


def _ret_kernel(*refs, n_chunks, has_state, write_state):
    lg_ref, q_ref, k_ref, v_ref, g_ref, gnw_ref = refs[:6]
    pos = 6
    if has_state:
        s0f_ref, s0b_ref = refs[pos:pos + 2]
        pos += 2
    o_ref = refs[pos]
    pos += 1
    if write_state:
        sf_ref, sb_ref = refs[pos:pos + 2]
        pos += 2
    acc_ref, kv_ref = refs[pos:pos + 2]
    for hh in range(RET_HEADS_PER_STEP):
        _ret_head(hh, lg_ref, q_ref, k_ref, v_ref, g_ref, gnw_ref,
                  (s0f_ref, s0b_ref) if has_state else None, o_ref, (sf_ref, sb_ref) if write_state else None,
                  acc_ref, kv_ref, n_chunks)


def _ret_head(hh, lg_ref, q_ref, k_ref, v_ref, g_ref, gnw_ref, s0_refs, o_ref, s_out_refs, acc_ref, kv_ref,
              n_chunks):
    C = RET_CHUNK
    cols = slice(hh * LANES, (hh + 1) * LANES)
    hd = pl.program_id(1) * RET_HEADS_PER_STEP + hh
    lgf = lg_ref[0, hd]
    lgb = lg_ref[1, hd]
    ii = lax.broadcasted_iota(jnp.int32, (C, C), 0).astype(F32)
    jj = lax.broadcasted_iota(jnp.int32, (C, C), 1).astype(F32)
    rel = ii - jj
    d_f = jnp.where(rel >= 0, jnp.exp(jnp.maximum(rel, 0.0) * lgf), 0.0)
    d_b = jnp.where(rel <= 0, jnp.exp(jnp.maximum(-rel, 0.0) * lgb), 0.0)
    d_sum = d_f + d_b
    idx = lax.broadcasted_iota(jnp.int32, (C, 1), 0).astype(F32)
    xi_f = jnp.exp((idx + 1.0) * lgf)
    zeta_f = jnp.exp((C - 1.0 - idx) * lgf)
    xi_b = jnp.exp((C - idx) * lgb)
    zeta_b = jnp.exp(idx * lgb)
    one = jnp.ones((1, 1), F32)
    gc_f = jnp.exp(one * (C * lgf))
    gc_b = jnp.exp(one * (C * lgb))

    nt = (((1,), (1,)), ((), ()))
    tn = (((0,), (0,)), ((), ()))

    if s0_refs is not None:
        s_f = s0_refs[0][hh]
        s_b = s0_refs[1][hh]
    else:
        s_f = jnp.zeros((RET_DK, RET_DV), F32)
        s_b = jnp.zeros((RET_DK, RET_DV), F32)

    for n in range(n_chunks):
        sl = slice(n * C, (n + 1) * C)
        qc, kc, vc = q_ref[sl, cols], k_ref[sl, cols], v_ref[sl, cols]
        scores = lax.dot_general(qc, kc, nt, preferred_element_type=F32) * d_sum
        acc_ref[sl, cols] = jnp.dot(scores.astype(BF16), vc, preferred_element_type=F32)
        kf = kc.astype(F32)
        kv_ref[0, hh, n] = lax.dot_general((kf * zeta_f).astype(BF16), vc, tn, preferred_element_type=F32)
        kv_ref[1, hh, n] = lax.dot_general((kf * zeta_b).astype(BF16), vc, tn, preferred_element_type=F32)

    for n in range(n_chunks):
        sl = slice(n * C, (n + 1) * C)
        qf = q_ref[sl, cols].astype(F32)
        acc_ref[sl, cols] += jnp.dot((qf * xi_f).astype(BF16), s_f.astype(BF16), preferred_element_type=F32)
        s_f = gc_f * s_f + kv_ref[0, hh, n]

    gnw = gnw_ref[:, cols]
    for n in reversed(range(n_chunks)):
        sl = slice(n * C, (n + 1) * C)
        qf = q_ref[sl, cols].astype(F32)
        cross = jnp.dot((qf * xi_b).astype(BF16), s_b.astype(BF16), preferred_element_type=F32)
        r = acc_ref[sl, cols] + cross
        mu = jnp.mean(r, axis=-1, keepdims=True)
        rc = r - mu
        var = jnp.mean(rc * rc, axis=-1, keepdims=True)
        rn = rc * lax.rsqrt(var + LN_EPS) * gnw
        o_ref[sl, cols] = (_silu(g_ref[sl, cols].astype(F32)) * rn).astype(BF16)
        s_b = gc_b * s_b + kv_ref[1, hh, n]

    if s_out_refs is not None:
        s_out_refs[0][hh] = s_f
        s_out_refs[1][hh] = s_b


def _retention(proj, lg, gn_w, batch, seq_len, states=None, write_state=False):
    T = proj.shape[0]
    hp = RET_HEADS_PER_STEP
    width = hp * LANES
    per_sec = SEC // width
    blk = lambda sec: pl.BlockSpec((seq_len, width), lambda b, h: (b, sec * per_sec + h))
    in_specs = [pl.BlockSpec(memory_space=pltpu.SMEM), blk(0), blk(1), blk(2), blk(3),
                pl.BlockSpec((1, width), lambda b, h: (0, h))]
    args = [lg, proj, proj, proj, proj, gn_w.reshape(1, -1)]
    if states is not None:
        st = pl.BlockSpec((None, hp, RET_DK, RET_DV), lambda b, h: (b, h, 0, 0))
        in_specs += [st, st]
        args += list(states)
    out_shape = [jax.ShapeDtypeStruct((T, SEC), BF16)]
    out_specs = [pl.BlockSpec((seq_len, width), lambda b, h: (b, h))]
    if write_state:
        st_o = pl.BlockSpec((None, hp, RET_DK, RET_DV), lambda b, h: (b, h, 0, 0))
        out_shape += [jax.ShapeDtypeStruct((batch, H_RET, RET_DK, RET_DV), F32)] * 2
        out_specs += [st_o, st_o]
    return pl.pallas_call(
        functools.partial(_ret_kernel, n_chunks=seq_len // RET_CHUNK, has_state=states is not None,
                          write_state=write_state),
        grid=(batch, H_RET // hp),
        in_specs=in_specs,
        out_specs=out_specs,
        out_shape=out_shape,
        scratch_shapes=[pltpu.VMEM((seq_len, width), F32),
                        pltpu.VMEM((2, hp, seq_len // RET_CHUNK, RET_DK, RET_DV), F32)],
        compiler_params=_cparams(("arbitrary", "arbitrary")),
        name="retention",
    )(*args)


def _att_kernel(*refs, has_ctx, out_scale, key_chunk):
    lam_ref, q_ref, k_ref, v_ref = refs[:4]
    pos = 4
    if has_ctx:
        ck_ref, cv_ref = refs[pos:pos + 2]
        pos += 2
    w_ref, o_ref = refs[pos:pos + 2]
    lam = lam_ref[0]
    q = q_ref[...]
    tq = q.shape[0]
    lane = lax.broadcasted_iota(jnp.int32, q.shape, 1)
    zero = jnp.zeros_like(q)
    qq = jnp.concatenate([jnp.where(lane < DIFF_QK, q, zero), jnp.where(lane >= DIFF_QK, q, zero)], axis=0)
    qq = qq * jnp.asarray(DIFF_QK ** -0.5, BF16)
    nt = (((1,), (1,)), ((), ()))
    chunks = [(k_ref, v_ref, c * key_chunk, key_chunk) for c in range(k_ref.shape[0] // key_chunk)]
    if has_ctx:
        chunks.append((ck_ref, cv_ref, 0, ck_ref.shape[0]))
    m = l = acc = None
    for kr, vr, off, n in chunks:
        kch = kr[off:off + n, :].astype(BF16)
        vch = vr[off:off + n, :].astype(BF16)
        s = lax.dot_general(qq, kch, nt, preferred_element_type=F32)
        cm = jnp.max(s, axis=-1, keepdims=True)
        m_new = cm if m is None else jnp.maximum(m, cm)
        p = jnp.exp(s - m_new)
        ps = jnp.sum(p, axis=-1, keepdims=True)
        pv = jnp.dot(p.astype(BF16), vch, preferred_element_type=F32)
        if m is None:
            l, acc = ps, pv
        else:
            alpha = jnp.exp(m - m_new)
            l = alpha * l + ps
            acc = alpha * acc + pv
        m = m_new
    o = acc / l
    att = o[:tq] - lam * o[tq:]
    att = att * lax.rsqrt(jnp.mean(att * att, axis=-1, keepdims=True) + LN_EPS)
    o_ref[...] = (att * w_ref[...] * out_scale).astype(BF16)


def _diff_attention(proj, lam, subln_w, lam_init, batch, seq_len, ctx=None, tq=256):
    T = proj.shape[0]
    hsec = SEC // LANES
    nq = seq_len // tq
    in_specs = [
        pl.BlockSpec(memory_space=pltpu.SMEM),
        pl.BlockSpec((tq, LANES), lambda b, h, i: (b * nq + i, 4 * hsec + h)),
        pl.BlockSpec((seq_len, LANES), lambda b, h, i: (b, 5 * hsec + h)),
        pl.BlockSpec((seq_len, LANES), lambda b, h, i: (b, 6 * hsec + h)),
    ]
    args = [lam, proj, proj, proj]
    if ctx is not None:
        ck, cv = ctx
        past = ck.shape[1]
        cspec = pl.BlockSpec((None, past, LANES), lambda b, h, i: (b, 0, h))
        in_specs += [cspec, cspec]
        args += [ck, cv]
    in_specs.append(pl.BlockSpec((1, LANES), lambda b, h, i: (0, 0)))
    args.append(subln_w.reshape(1, -1))
    return pl.pallas_call(
        functools.partial(_att_kernel, has_ctx=ctx is not None, out_scale=1.0 - lam_init,
                          key_chunk=min(512, seq_len)),
        grid=(batch, H_DIFF, nq),
        in_specs=in_specs,
        out_specs=pl.BlockSpec((tq, LANES), lambda b, h, i: (b * nq + i, h)),
        out_shape=jax.ShapeDtypeStruct((T, SEC), BF16),
        compiler_params=_cparams(("arbitrary", "arbitrary", "arbitrary")),
        name="diff_attention",
    )(*args)


def _route_class(lt, rb_ref):
    sel = [jax.nn.sigmoid(lt[e:e + 1, :]) + rb_ref[e] for e in range(N_EXPERTS)]
    gscore = []
    for g in range(N_GROUPS):
        mem = sel[g * GROUP_SIZE:(g + 1) * GROUP_SIZE]
        best = None
        for a in range(GROUP_SIZE):
            for b in range(a + 1, GROUP_SIZE):
                pair = mem[a] + mem[b]
                best = pair if best is None else jnp.maximum(best, pair)
        gscore.append(best)
    gbest = gscore[0]
    gidx = jnp.zeros_like(gbest)
    for g in range(1, N_GROUPS):
        upd = gscore[g] > gbest
        gidx = jnp.where(upd, float(g), gidx)
        gbest = jnp.where(upd, gscore[g], gbest)
    msel = []
    for j in range(GROUP_SIZE):
        out = sel[j]
        for g in range(1, N_GROUPS):
            out = jnp.where(gidx == float(g), sel[g * GROUP_SIZE + j], out)
        msel.append(out)
    one = jnp.ones_like(gbest)
    zero = jnp.zeros_like(gbest)
    chosen = []
    for j in range(GROUP_SIZE):
        rank = zero
        for k in range(GROUP_SIZE):
            if k < j:
                rank = rank + jnp.where(msel[k] >= msel[j], one, zero)
            elif k > j:
                rank = rank + jnp.where(msel[k] > msel[j], one, zero)
        chosen.append(jnp.where(rank < 2.0, one, zero))
    c0, c1, c2, c3 = chosen
    order = jnp.where(c0 * c1 > 0, 0.0, jnp.where(c1 * c2 > 0, 1.0, jnp.where(c2 * c3 > 0, 2.0,
            jnp.where(c0 * c3 > 0, 3.0, jnp.where(c0 * c2 > 0, 4.0, 5.0)))))
    return gidx * float(N_PAIRS) + order


def _post_kernel(*refs, n_mix, group_tiles):
    n_groups = len(group_tiles)
    per_group = n_mix + 1
    group_refs = [refs[g * per_group:(g + 1) * per_group] for g in range(n_groups)]
    refs = refs[n_groups * per_group:]
    (w_ref, g_ref, sh_ref, sc_ref, lnw_ref, lnb_ref, rwh_ref, rwl_ref, rb_ref, tri_ref,
     x1_ref, h2_ref, info_ref, cout_ref, cnt_ref, pre_ref) = refs
    i = pl.program_id(0)

    @pl.when(i == 0)
    def _():
        cnt_ref[...] = jnp.zeros_like(cnt_ref)

    first = 0
    for g in range(n_groups):
        @pl.when(jnp.logical_and(i >= first, i < first + group_tiles[g]))
        def _(g=g):
            out = None
            off = 0
            for m_ref in group_refs[g][:n_mix]:
                width = m_ref.shape[1]
                part = jnp.dot(m_ref[...], w_ref[off:off + width, :], preferred_element_type=F32)
                out = part if out is None else out + part
                off += width
            pre_ref[...] = ALPHA * group_refs[g][n_mix][...] + g_ref[...] * out
        first += group_tiles[g]

    tm = pre_ref.shape[0]
    ts = tri_ref.shape[0]
    base = cnt_ref[:, 0:1]
    for part in range(tm // ts):
        rows = slice(part * ts, (part + 1) * ts)
        x1 = _layer_norm(pre_ref[rows, :], lnw_ref[...], lnb_ref[...])
        x1_ref[rows, :] = x1
        h2 = x1 * (1.0 + sc_ref[...]) + sh_ref[...]
        for j in range(TOK_ROWS):
            h2_ref[pl.ds(part * ts * TOK_ROWS + j, ts, stride=TOK_ROWS), :] = h2[:, j * LANES:(j + 1) * LANES]
        h_hi = h2.astype(BF16)
        h_lo = (h2 - h_hi.astype(F32)).astype(BF16)
        logits = (jnp.dot(h_hi, rwh_ref[...], preferred_element_type=F32)
                  + jnp.dot(h_lo, rwh_ref[...], preferred_element_type=F32)
                  + jnp.dot(h_hi, rwl_ref[...], preferred_element_type=F32))
        cls = _route_class(logits.T, rb_ref)
        crow = lax.broadcasted_iota(jnp.int32, (32, ts), 0).astype(F32)
        onehot = jnp.where(crow == cls, 1.0, 0.0)
        prefix = jnp.dot(onehot.astype(BF16), tri_ref[...], preferred_element_type=F32)
        rank = jnp.sum(onehot * (prefix - 1.0 + base), axis=0, keepdims=True)
        base = base + jnp.sum(onehot, axis=1, keepdims=True)
        packed = cls.astype(jnp.int32) * 65536 + rank.astype(jnp.int32)
        info_ref[:, rows] = packed
    cnt_ref[...] = jnp.broadcast_to(base, cnt_ref.shape)

    @pl.when(i == pl.num_programs(0) - 1)
    def _():
        cout_ref[...] = cnt_ref[...]


def _tri(tm):
    return jnp.asarray(np.triu(np.ones((tm, tm), np.float32))).astype(BF16)


def _group_spec(width, tm, first_tile, n_tiles, array_tile_off=0):
    return pl.BlockSpec((tm, width),
                        lambda i, *_: (jnp.clip(i - first_tile, 0, n_tiles - 1) + array_tile_off, 0))


def _post_mix(group_mixes, group_x, w_out_bf16, mod, layer, ln_w, ln_b, rw_hi, rw_lo, router_bias, cond_row, tm):
    group_tiles = tuple(m[0].shape[0] // tm for m in group_mixes)
    n_tiles = sum(group_tiles)
    total_rows = n_tiles * tm
    row = pl.BlockSpec((tm, D_MODEL), lambda i: (i, 0))
    vec = pl.BlockSpec((1, D_MODEL), lambda i: (0, 0))
    cnt = pl.BlockSpec((32, LANES), lambda i: (0, 0))
    in_specs, args = [], []
    first = 0
    for mixes, (x, x_row_off), nt in zip(group_mixes, group_x, group_tiles):
        for m in mixes:
            in_specs.append(_group_spec(m.shape[1], tm, first, nt))
            args.append(m)
        in_specs.append(_group_spec(D_MODEL, tm, first, nt, x_row_off // tm))
        args.append(x)
        first += nt
    in_specs += [
        pl.BlockSpec((D_MODEL, D_MODEL), lambda i: (0, 0)),
        _mod_spec(layer, 2, lambda i: cond_row(i, tm)), _mod_spec(layer, 3, lambda i: cond_row(i, tm)),
        _mod_spec(layer, 4, lambda i: cond_row(i, tm)),
        vec, vec,
        pl.BlockSpec((D_MODEL, LANES), lambda i: (0, 0)),
        pl.BlockSpec((D_MODEL, LANES), lambda i: (0, 0)),
        pl.BlockSpec(memory_space=pltpu.SMEM),
        pl.BlockSpec((POST_SUB, POST_SUB), lambda i: (0, 0)),
    ]
    args += [w_out_bf16, mod, mod, mod, ln_w.reshape(1, -1), ln_b.reshape(1, -1), rw_hi, rw_lo, router_bias,
             _tri(POST_SUB)]
    return pl.pallas_call(
        functools.partial(_post_kernel, n_mix=len(group_mixes[0]), group_tiles=group_tiles),
        grid=(n_tiles,),
        in_specs=in_specs,
        out_specs=[row, pl.BlockSpec((tm * TOK_ROWS, LANES), lambda i: (i, 0)),
                   pl.BlockSpec((1, tm), lambda i: (0, i)), cnt],
        out_shape=[jax.ShapeDtypeStruct((total_rows, D_MODEL), F32),
                   jax.ShapeDtypeStruct((total_rows * TOK_ROWS, LANES), F32),
                   jax.ShapeDtypeStruct((1, total_rows), jnp.int32), jax.ShapeDtypeStruct((32, LANES), F32)],
        scratch_shapes=[pltpu.VMEM((32, LANES), F32), pltpu.VMEM((tm, D_MODEL), F32)],
        compiler_params=_cparams(("arbitrary",)),
        name="post_mix",
    )(*args)


def _plan_kernel(cnt_ref, ea_ref, eb_ref, fl_ref, rs_ref, *, n_tiles):
    start = jnp.int32(0)
    prev_a = jnp.int32(-1)
    prev_b = jnp.int32(-1)
    for c in range(N_CLASSES):
        n = cnt_ref[c]
        tiles = lax.shift_right_logical(n + (MOE_TM - 1), MOE_SHIFT)
        row0 = start * MOE_TM
        rs_ref[c] = row0
        g, pr = divmod(c, N_PAIRS)
        a = g * GROUP_SIZE + PAIR_SLOTS[pr][0]
        b = g * GROUP_SIZE + PAIR_SLOTS[pr][1]
        first = 1 + 4 * (prev_a != a).astype(jnp.int32) + 8 * (prev_b != b).astype(jnp.int32)

        def tile_body(k, _, start=start, a=a, b=b, first=first):
            t = start + k
            ea_ref[t] = a
            eb_ref[t] = b
            fl_ref[t] = jnp.where(k == 0, first, 1)
            return 0

        lax.fori_loop(0, tiles, tile_body, 0)
        has = tiles > 0
        prev_a = jnp.where(has, a, prev_a)
        prev_b = jnp.where(has, b, prev_b)
        start = start + tiles
    for c in range(N_CLASSES, 31):
        rs_ref[c] = 0
    rs_ref[31] = start

    def idle_body(t, _):
        ea_ref[t] = prev_a
        eb_ref[t] = prev_b
        fl_ref[t] = 0
        return 0

    lax.fori_loop(start, n_tiles, idle_body, 0)


def _moe_plan(counts, n_tiles):
    smem = pl.BlockSpec(memory_space=pltpu.SMEM)
    i32 = lambda n: jax.ShapeDtypeStruct((n,), jnp.int32)
    return pl.pallas_call(
        functools.partial(_plan_kernel, n_tiles=n_tiles),
        in_specs=[smem],
        out_specs=[smem] * 4,
        out_shape=[i32(n_tiles), i32(n_tiles), i32(n_tiles), i32(32)],
        name="moe_plan",
    )(counts)


def _class_row(packed, rs_ref):
    return rs_ref[lax.shift_right_logical(packed, 16)] + (packed & 0xFFFF)


def _dispatch_kernel(packed_ref, rs_ref, cnt_ref, h_ref, hs_ref, sem, fill_sem, *, tm, n_tiles):
    i = pl.program_id(0)
    used = rs_ref[31]

    def tile_rows(row, n=1):
        return pl.ds(pl.multiple_of(row * TOK_ROWS, TOK_ROWS), n * TOK_ROWS)

    def fill_copy(row, n):
        return pltpu.make_async_copy(h_ref.at[tile_rows(0, n)], hs_ref.at[tile_rows(row, n)], fill_sem)

    def fill(act):
        for c in range(N_CLASSES):
            n = cnt_ref[c]
            row = rs_ref[c] + n
            pad = (-n) & (MOE_TM - 1)
            for bit in reversed(range(MOE_SHIFT)):
                size = 1 << bit

                @pl.when((pad & size) != 0)
                def _(row=row, size=size):
                    act(fill_copy(row, size))
                row = row + (pad & size)
        lax.fori_loop(used, n_tiles, lambda t, _: (act(fill_copy(t * MOE_TM, MOE_TM)), 0)[1], 0)

    @pl.when(i == 0)
    def _():
        fill(lambda cp: cp.start())

    def body(r8, _):
        for j in range(8):
            r = r8 * 8 + j
            row = _class_row(packed_ref[i * tm + r], rs_ref)
            pltpu.make_async_copy(h_ref.at[tile_rows(r)], hs_ref.at[tile_rows(row)], sem).start(priority=j % 2)
        return 0

    lax.fori_loop(0, tm // 8, body, 0)
    pltpu.make_async_copy(h_ref, hs_ref.at[tile_rows(0, tm)], sem).wait()

    @pl.when(i == 0)
    def _():
        fill(lambda cp: cp.wait())


def _dispatch(h2_tiles, packed, plan, counts, n_tiles, tm=1024):
    T = packed.shape[0]
    assert tm >= MOE_TM and T % tm == 0
    grid_spec = pltpu.PrefetchScalarGridSpec(
        num_scalar_prefetch=3,
        grid=(T // tm,),
        in_specs=[pl.BlockSpec((tm * TOK_ROWS, LANES), lambda i, *_: (i, 0))],
        out_specs=pl.BlockSpec(memory_space=pl.ANY),
        scratch_shapes=[pltpu.SemaphoreType.DMA(()), pltpu.SemaphoreType.DMA(())],
    )
    return pl.pallas_call(
        functools.partial(_dispatch_kernel, tm=tm, n_tiles=n_tiles),
        grid_spec=grid_spec,
        out_shape=jax.ShapeDtypeStruct((n_tiles * MOE_TM * TOK_ROWS, LANES), F32),
        compiler_params=_cparams(("arbitrary",)),
        name="moe_dispatch",
    )(packed, plan[3], counts, h2_tiles)


def _moe_kernel(ea_ref, eb_ref, fl_ref, h_ref, rwt_ref, wga, wua, wda, wgb, wub, wdb, o_ref,
                sga, sua, sda, sgb, sub, sdb):
    t = pl.program_id(0)
    flags = fl_ref[t]
    valid = (flags & 1) != 0

    @pl.when((flags & 4) != 0)
    def _():
        sga[...] = wga[...].astype(BF16)
        sua[...] = wua[...].astype(BF16)
        sda[...] = wda[...].astype(BF16)

    @pl.when((flags & 8) != 0)
    def _():
        sgb[...] = wgb[...].astype(BF16)
        sub[...] = wub[...].astype(BF16)
        sdb[...] = wdb[...].astype(BF16)

    @pl.when(valid)
    def _():
        h = _load_token_tiles(h_ref, MOE_TM).astype(BF16)
        nt = (((1,), (1,)), ((), ()))
        score = jax.nn.sigmoid(lax.dot_general(h, rwt_ref[...], nt, preferred_element_type=F32))
        lane = lax.broadcasted_iota(jnp.int32, score.shape, 1)
        s_a = jnp.sum(jnp.where(lane == ea_ref[t], score, 0.0), axis=1, keepdims=True)
        s_b = jnp.sum(jnp.where(lane == eb_ref[t], score, 0.0), axis=1, keepdims=True)
        tot = s_a + s_b
        act_a = _silu(jnp.dot(h, sga[...], preferred_element_type=F32)) \
            * jnp.dot(h, sua[...], preferred_element_type=F32) * (s_a / tot)
        act_b = _silu(jnp.dot(h, sgb[...], preferred_element_type=F32)) \
            * jnp.dot(h, sub[...], preferred_element_type=F32) * (s_b / tot)
        _store_token_tiles(o_ref, jnp.dot(act_a.astype(BF16), sda[...], preferred_element_type=F32)
                           + jnp.dot(act_b.astype(BF16), sdb[...], preferred_element_type=F32))

    @pl.when(jnp.logical_not(valid))
    def _():
        o_ref[...] = jnp.zeros_like(o_ref)


def _moe(hs, plan, layer, rwt_bf16, w_gate, w_up, w_down):
    ea, eb, flags, _ = plan
    n_tiles = ea.shape[0]
    wspec = lambda shape, which: pl.BlockSpec(
        (None, None) + shape, (lambda t, ea, eb, fl: (layer, ea[t], 0, 0)) if which == 0
        else (lambda t, ea, eb, fl: (layer, eb[t], 0, 0)))
    up = (D_MODEL, D_FF_EXPERT)
    dn = (D_FF_EXPERT, D_MODEL)
    tile = pl.BlockSpec((MOE_TM * TOK_ROWS, LANES), lambda t, ea, eb, fl: (t, 0))
    grid_spec = pltpu.PrefetchScalarGridSpec(
        num_scalar_prefetch=3,
        grid=(n_tiles,),
        in_specs=[
            tile,
            pl.BlockSpec((N_EXPERTS, D_MODEL), lambda t, ea, eb, fl: (0, 0)),
            wspec(up, 0), wspec(up, 0), wspec(dn, 0), wspec(up, 1), wspec(up, 1), wspec(dn, 1),
        ],
        out_specs=tile,
        scratch_shapes=[pltpu.VMEM(up, BF16), pltpu.VMEM(up, BF16), pltpu.VMEM(dn, BF16),
                        pltpu.VMEM(up, BF16), pltpu.VMEM(up, BF16), pltpu.VMEM(dn, BF16)],
    )
    return pl.pallas_call(
        _moe_kernel,
        grid_spec=grid_spec,
        out_shape=jax.ShapeDtypeStruct((n_tiles * MOE_TM * TOK_ROWS, LANES), F32),
        compiler_params=_cparams(("arbitrary",)),
        name="moe",
    )(ea, eb, flags, hs, rwt_bf16, w_gate, w_up, w_down, w_gate, w_up, w_down)


def _gather_expert_rows(packed_ref, rs_ref, ys_ref, ybuf, sem, tm):
    i = pl.program_id(0)
    n_i = pl.num_programs(0)
    slot = i % 2

    def gather(tile, buf_slot):
        def body(r8, _):
            for j in range(8):
                r = r8 * 8 + j
                d = _class_row(packed_ref[tile * tm + r], rs_ref)
                pltpu.make_async_copy(ys_ref.at[pl.ds(pl.multiple_of(d * TOK_ROWS, TOK_ROWS), TOK_ROWS)],
                                      ybuf.at[buf_slot, pl.ds(pl.multiple_of(r * TOK_ROWS, TOK_ROWS), TOK_ROWS)],
                                      sem.at[buf_slot]).start(priority=j % 2)
            return 0
        lax.fori_loop(0, tm // 8, body, 0)

    @pl.when(i == 0)
    def _():
        gather(0, 0)

    @pl.when(i + 1 < n_i)
    def _():
        gather(jnp.minimum(i + 1, n_i - 1), 1 - slot)

    pltpu.make_async_copy(ys_ref.at[pl.ds(0, tm * TOK_ROWS)], ybuf.at[slot], sem.at[slot]).wait()
    return _load_token_tiles(ybuf, tm, slot)


def _fin_proj_kernel(packed_ref, rs_ref, x_ref, ys_ref, g_ref, lnw_ref, lnb_ref, sh_ref, sc_ref, w_ref,
                     x2_ref, u_ref, ybuf, sem, *, tm):
    y = _gather_expert_rows(packed_ref, rs_ref, ys_ref, ybuf, sem, tm)
    x2 = _layer_norm(ALPHA * x_ref[...] + g_ref[...] * y, lnw_ref[...], lnb_ref[...])
    x2_ref[...] = x2
    h = (x2 * (1.0 + sc_ref[...]) + sh_ref[...]).astype(BF16)
    for s in range(u_ref.shape[1] // SEC):
        u_ref[:, s * SEC:(s + 1) * SEC] = jnp.dot(h, w_ref[:, s * SEC:(s + 1) * SEC],
                                                  preferred_element_type=F32).astype(BF16)


def _fin_proj(x1, ys, packed, row_start, mod, layer, ln_w, ln_b, w_next_bf16, cond_row, tm=512):
    T = x1.shape[0]
    N = w_next_bf16.shape[1]
    row = pl.BlockSpec((tm, D_MODEL), lambda i, *_: (i, 0))
    vec = pl.BlockSpec((1, D_MODEL), lambda i, *_: (0, 0))
    crow = lambda i: cond_row(i, tm)
    grid_spec = pltpu.PrefetchScalarGridSpec(
        num_scalar_prefetch=2,
        grid=(T // tm,),
        in_specs=[row, pl.BlockSpec(memory_space=pl.ANY), _mod_spec(layer, 5, crow), vec, vec,
                  _mod_spec(layer + 1, 0, crow), _mod_spec(layer + 1, 1, crow),
                  pl.BlockSpec((D_MODEL, N), lambda i, *_: (0, 0))],
        out_specs=[row, pl.BlockSpec((tm, N), lambda i, *_: (i, 0))],
        scratch_shapes=[pltpu.VMEM((2, tm * TOK_ROWS, LANES), F32), pltpu.SemaphoreType.DMA((2,))],
    )
    return pl.pallas_call(
        functools.partial(_fin_proj_kernel, tm=tm),
        grid_spec=grid_spec,
        out_shape=[jax.ShapeDtypeStruct((T, D_MODEL), F32), jax.ShapeDtypeStruct((T, N), BF16)],
        compiler_params=_cparams(("arbitrary",)),
        name="post_moe_in_proj",
    )(packed, row_start, x1, ys, mod, ln_w.reshape(1, -1), ln_b.reshape(1, -1), mod, mod, w_next_bf16)


def _fin_kernel(packed_ref, rs_ref, x_ref, ys_ref, g_ref, lnw_ref, lnb_ref, *rest, tm, group_tiles):
    o_refs = rest[:len(group_tiles)]
    ybuf, sem = rest[len(group_tiles):]
    i = pl.program_id(0)
    y = _layer_norm(ALPHA * x_ref[...] + g_ref[...] * _gather_expert_rows(packed_ref, rs_ref, ys_ref, ybuf, sem, tm),
                    lnw_ref[...], lnb_ref[...])
    first = 0
    for o_ref, nt in zip(o_refs, group_tiles):
        @pl.when(jnp.logical_and(i >= first, i < first + nt))
        def _(o_ref=o_ref):
            o_ref[...] = y
        first += nt


def _fin(x1, ys, packed, row_start, mod, layer, ln_w, ln_b, cond_row, group_rows, tm=256):
    group_tiles = tuple(n // tm for n in group_rows)
    row = pl.BlockSpec((tm, D_MODEL), lambda i, *_: (i, 0))
    vec = pl.BlockSpec((1, D_MODEL), lambda i, *_: (0, 0))
    out_specs, first = [], 0
    for nt in group_tiles:
        out_specs.append(_group_spec(D_MODEL, tm, first, nt))
        first += nt
    grid_spec = pltpu.PrefetchScalarGridSpec(
        num_scalar_prefetch=2,
        grid=(sum(group_tiles),),
        in_specs=[row, pl.BlockSpec(memory_space=pl.ANY), _mod_spec(layer, 5, lambda i: cond_row(i, tm)), vec, vec],
        out_specs=out_specs,
        scratch_shapes=[pltpu.VMEM((2, tm * TOK_ROWS, LANES), F32), pltpu.SemaphoreType.DMA((2,))],
    )
    return pl.pallas_call(
        functools.partial(_fin_kernel, tm=tm, group_tiles=group_tiles),
        grid_spec=grid_spec,
        out_shape=[jax.ShapeDtypeStruct((n, D_MODEL), F32) for n in group_rows],
        compiler_params=_cparams(("arbitrary",)),
        name="post_moe",
    )(packed, row_start, x1, ys, mod, ln_w.reshape(1, -1), ln_b.reshape(1, -1))


def _filt_kernel(z_ref, w1_ref, b1_ref, w2_ref, b2_ref, fr_ref, w3_ref, dl_ref, o_ref):
    i = pl.program_id(0)
    z = z_ref[...]
    fr = fr_ref[...]
    a = jnp.sin(fr * (jnp.dot(z, w1_ref[...], precision=HIGHEST, preferred_element_type=F32) + b1_ref[...]))
    a = jnp.sin(fr * (jnp.dot(a, w2_ref[...], precision=HIGHEST, preferred_element_type=F32) + b2_ref[...]))
    filt = _dot_3pass(a, w3_ref[...])
    window = jnp.exp(-z[:, 0:1] * dl_ref[...]) + HY_SHIFT
    rows = i * z.shape[0] + lax.broadcasted_iota(jnp.int32, filt.shape, 0)
    o_ref[...] = jnp.where(rows == 0, 0.0, filt * window)


def _hyena_taps(seq_len, w1, b1, w2, b2, w3, freq):
    L = seq_len
    t = np.linspace(0.0, 1.0, L)[:, None]
    bands = np.linspace(1e-4, HY_BANDS - 1, HY_BANDS)
    ang = 2.0 * math.pi * bands[None, :] * np.arange(L)[:, None] / L
    z = np.concatenate([t, np.cos(ang), -np.sin(ang)], axis=-1)
    offs = np.minimum(np.abs(np.arange(2 * L) - L), L - 1)
    z2_np = np.zeros((2 * L, LANES), np.float32)
    z2_np[:, :z.shape[1]] = z[offs]
    z2 = jnp.asarray(z2_np)
    pad_c = lambda a: jnp.zeros((a.shape[0], LANES), F32).at[:, :a.shape[1]].set(a)
    pad_r = lambda a: jnp.zeros((LANES, a.shape[1]), F32).at[:a.shape[0], :].set(a)
    w1p = pad_r(pad_c(w1))
    w2p = pad_r(pad_c(w2))
    w3p = pad_r(w3)
    b1p, b2p, frp = pad_c(b1[None, :]), pad_c(b2[None, :]), pad_c(freq[None, :])
    deltas = jnp.asarray(np.abs(np.linspace(math.log(HY_DECAY_TARGET) / HY_SLOW, math.log(HY_DECAY_TARGET) / HY_FAST,
                                            D_MODEL))[None, :].astype(np.float32))
    rb = min(512, L)
    cbf = D_MODEL
    ncb = D_MODEL // cbf
    nrb_back = L // rb
    sq = pl.BlockSpec((LANES, LANES), lambda i, j: (0, 0))
    vec = pl.BlockSpec((1, LANES), lambda i, j: (0, 0))
    return pl.pallas_call(
        _filt_kernel,
        grid=(2 * L // rb, ncb),
        in_specs=[
            pl.BlockSpec((rb, LANES), lambda i, j: (i, 0)),
            sq, vec, sq, vec, vec,
            pl.BlockSpec((LANES, cbf), lambda i, j: (0, jnp.where(i < nrb_back, ncb + j, j))),
            pl.BlockSpec((1, cbf), lambda i, j: (0, j)),
        ],
        out_specs=pl.BlockSpec((rb, cbf), lambda i, j: (i, j)),
        out_shape=jax.ShapeDtypeStruct((2 * L, D_MODEL), F32),
        compiler_params=_cparams(("arbitrary", "arbitrary")),
        name="hyena_taps",
    )(z2, w1p, b1p, w2p, b2p, frp, w3p, deltas)


def _dft_mats(cb):
    n = 2 * cb
    m = np.arange(cb)
    f = np.arange(cb)
    ang = 2.0 * np.pi * ((f[:, None] * m[None, :]) % n) / n
    fwd = np.concatenate([np.cos(ang), -np.sin(ang)], axis=0)
    fwd[cb, :] = np.where(m % 2 == 0, 1.0, -1.0)
    coef = np.where(f == 0, 1.0, 2.0)[None, :] / n
    inv = np.concatenate([coef * np.cos(ang.T), -coef * np.sin(ang.T)], axis=1)
    inv[:, cb] = np.where(m % 2 == 0, 1.0, -1.0) / n
    return fwd.astype(np.float32), inv.astype(np.float32)


def _hconv_kernel(x0_ref, x1_ref, v_ref, cw0_ref, cw1_ref, cw2_ref, cb0_ref, cb1_ref, cb2_ref,
                  taps_ref, fb_ref, fwd_ref, inv_ref, o_ref,
                  hs_ref, stage_ref, w32_ref, w_ref, x0c_ref, u_ref, y_ref, *, seq_len, blk, n_seq):
    L = seq_len
    nb = L // blk
    cw = x0_ref.shape[1]
    RC = 256
    bi = pl.program_id(1)
    row0 = lax.broadcasted_iota(jnp.int32, (RC, cw), 0) == 0

    @pl.when(bi == 0)
    def _():
        rows = lax.broadcasted_iota(jnp.int32, (2 * blk, 1), 0)
        sign = jnp.where(rows % 2 == 0, 1.0, -1.0)
        real_row = rows <= blk
        prev = None
        for jb in range(2 * nb):
            cur = jnp.dot(fwd_ref[...], taps_ref[jb * blk:(jb + 1) * blk, :].astype(BF16),
                          preferred_element_type=F32)
            if prev is not None:
                first_tap = taps_ref[(jb - 1) * blk:(jb - 1) * blk + 1, :].astype(BF16).astype(F32)
                hs_ref[jb - 1] = cur + sign * (prev - jnp.where(real_row, first_tap, 0.0))
            prev = cur

    zeros8 = jnp.zeros((8, cw), F32)
    for s in range(n_seq):
        r0 = s * L

        def short_conv(k, src_ref, cw_ref, cb_ref, store):
            stage_ref[k, 0:8, :] = zeros8
            stage_ref[k, 8 + L:16 + L, :] = zeros8
            for c in range(L // RC):
                stage_ref[k, 8 + c * RC:8 + (c + 1) * RC, :] = \
                    src_ref[r0 + c * RC:r0 + (c + 1) * RC, :].astype(F32)
            w = cw_ref[...]
            for c in range(L // RC):
                lo = stage_ref[k, 7 + c * RC:7 + (c + 1) * RC, :]
                mid = stage_ref[k, 8 + c * RC:8 + (c + 1) * RC, :]
                hi = stage_ref[k, 9 + c * RC:9 + (c + 1) * RC, :]
                store(c, lo * w[0:1, :] + mid * w[1:2, :] + hi * w[2:3, :] + cb_ref[...])

        def st_x1(c, val):
            w32_ref[c * RC:(c + 1) * RC, :] = val

        def st_v(c, val):
            w_ref[c * RC:(c + 1) * RC, :] = (w32_ref[c * RC:(c + 1) * RC, :] * val).astype(BF16)

        def st_x0(c, val):
            x0c_ref[c * RC:(c + 1) * RC, :] = val.astype(BF16)

        short_conv(0, x1_ref, cw1_ref, cb1_ref, st_x1)
        short_conv(1, v_ref, cw2_ref, cb2_ref, st_v)
        short_conv(2, x0_ref, cw0_ref, cb0_ref, st_x0)

        for j in range(nb):
            u_ref[j] = jnp.dot(fwd_ref[...], w_ref[j * blk:(j + 1) * blk, :], preferred_element_type=F32)

        for i in range(nb):
            for c in range(blk // RC):
                re = None
                im = None
                for j in range(nb):
                    k = i - j + nb - 1
                    a = u_ref[j, c * RC:(c + 1) * RC, :]
                    b = u_ref[j, blk + c * RC:blk + (c + 1) * RC, :]
                    hr = hs_ref[k, c * RC:(c + 1) * RC, :]
                    hi = hs_ref[k, blk + c * RC:blk + (c + 1) * RC, :]
                    bb = b * hi
                    if c == 0:
                        t_re = a * hr - jnp.where(row0, 0.0, bb)
                        t_im = jnp.where(row0, bb, a * hi + b * hr)
                    else:
                        t_re = a * hr - bb
                        t_im = a * hi + b * hr
                    re = t_re if re is None else re + t_re
                    im = t_im if im is None else im + t_im
                y_ref[i, c * RC:(c + 1) * RC, :] = re.astype(BF16)
                y_ref[i, blk + c * RC:blk + (c + 1) * RC, :] = im.astype(BF16)
            conv = jnp.dot(inv_ref[...], y_ref[i], preferred_element_type=F32)
            sl = slice(i * blk, (i + 1) * blk)
            z = x0c_ref[sl, :].astype(F32) * (conv + w_ref[sl, :].astype(F32) * fb_ref[...])
            o_ref[r0 + i * blk:r0 + (i + 1) * blk, :] = z.astype(BF16)


def _hyena_conv(u, conv_w, conv_b, taps, filt_bias, batch, seq_len, blk, n_seq, row_off=0, cw=256):
    T = batch * seq_len
    L = seq_len
    nb = L // blk
    ncw = D_MODEL // cw
    fwd_np, inv_np = _dft_mats(blk)
    fwd = jnp.asarray(fwd_np).astype(BF16)
    inv = jnp.asarray(inv_np).astype(BF16)
    rows = n_seq * L
    assert row_off % rows == 0
    boff = row_off // rows
    sec = lambda s: pl.BlockSpec((rows, cw), lambda c, b: (b + boff, s * ncw + c))
    cws = lambda s: pl.BlockSpec((3, cw), lambda c, b: (0, s * ncw + c))
    cbs = lambda s: pl.BlockSpec((1, cw), lambda c, b: (0, s * ncw + c))
    return pl.pallas_call(
        functools.partial(_hconv_kernel, seq_len=L, blk=blk, n_seq=n_seq),
        grid=(ncw, batch // n_seq),
        in_specs=[sec(0), sec(1), sec(2), cws(0), cws(1), cws(2), cbs(0), cbs(1), cbs(2),
                  pl.BlockSpec((2 * L, cw), lambda c, b: (0, c), pipeline_mode=pl.Buffered(1)),
                  pl.BlockSpec((1, cw), lambda c, b: (0, c)),
                  pl.BlockSpec((2 * blk, blk), lambda c, b: (0, 0), pipeline_mode=pl.Buffered(1)),
                  pl.BlockSpec((blk, 2 * blk), lambda c, b: (0, 0), pipeline_mode=pl.Buffered(1))],
        out_specs=pl.BlockSpec((rows, cw), lambda c, b: (b, c)),
        out_shape=jax.ShapeDtypeStruct((T, D_MODEL), BF16),
        scratch_shapes=[
            pltpu.VMEM((2 * nb - 1, 2 * blk, cw), F32),
            pltpu.VMEM((3, L + 16, cw), F32),
            pltpu.VMEM((L, cw), F32),
            pltpu.VMEM((L, cw), BF16),
            pltpu.VMEM((L, cw), BF16),
            pltpu.VMEM((nb, 2 * blk, cw), F32),
            pltpu.VMEM((nb, 2 * blk, cw), BF16),
        ],
        compiler_params=_cparams(("arbitrary", "arbitrary")),
        name="hyena_conv",
    )(u, u, u, conv_w, conv_w, conv_w, conv_b.reshape(1, -1), conv_b.reshape(1, -1), conv_b.reshape(1, -1),
      taps, filt_bias.reshape(1, -1), fwd, inv)


def kernel(x_prompt, x_sample, cache_diff_k, cache_diff_v, state_ret_fwd, state_ret_bwd, c, c_ctx, ada_w, ada_b, ln_w, ln_b, ev_w_in, ev_w_out, ret_decay_fwd, ret_decay_bwd, ret_gn_w, diff_lambda, diff_subln_w, hy_w_in, hy_conv_w, hy_conv_b, hy_ffn_w1, hy_ffn_b1, hy_ffn_w2, hy_ffn_b2, hy_ffn_w3, hy_freq, hy_filter_bias, hy_w_out, router_w, router_bias, moe_w_gate, moe_w_up, moe_w_down):
    B, S, D = x_prompt.shape
    DB, DS, _ = x_sample.shape
    PAST = cache_diff_k.shape[2]
    TP, TS = B * S, DB * DS
    T_ALL = TP + TS
    assert D == D_MODEL and 1 + DB <= COND_ROWS and T_ALL < 65536

    cond8 = jnp.zeros((COND_ROWS, D), F32).at[0].set(c_ctx).at[1:1 + DB].set(c)
    mod = _ada_mod(cond8, ada_w, ada_b).reshape(DEPTH * COND_ROWS * N_MOD, 1, D)

    tm = 512
    groups = [
        dict(x=x_prompt.reshape(TP, D), x_off=0, batch=B, seq=S, off=0, cond_row=lambda i, tm: 0),
        dict(x=x_sample.reshape(TS, D), x_off=0, batch=DB, seq=DS, off=TP,
             cond_row=lambda i, tm: 1 + (i * tm) // DS),
    ]
    u_next = None

    def cond_row_all(i, tm):
        return jnp.where(i * tm < TP, 0, 1 + (i * tm - TP) // DS)

    rw_pad = jnp.zeros((D, LANES), F32).at[:, :N_EXPERTS].set(router_w.astype(F32))
    rw_hi = rw_pad.astype(BF16)
    rw_lo = (rw_pad - rw_hi.astype(F32)).astype(BF16)
    rwt = router_w.T.astype(BF16)
    rbias = router_bias.astype(F32)
    n_tiles = T_ALL // MOE_TM + N_CLASSES
    outs = {}

    for l in range(DEPTH):
        mixes = []
        if l % 2 == 0:
            e = l // 2
            w_in = ev_w_in[e].astype(BF16)
            w_out = ev_w_out[e].astype(BF16)
            lg = jnp.stack([jnp.log1p(-jnp.exp2(ret_decay_fwd[e].astype(F32))),
                            jnp.log1p(-jnp.exp2(ret_decay_bwd[e].astype(F32)))])
            lam_init = 0.8 - 0.6 * math.exp(-0.3 * l)
            lq1, lk1, lq2, lk2 = diff_lambda[e].astype(F32)
            lam = (jnp.exp(jnp.sum(lq1 * lk1)) - jnp.exp(jnp.sum(lq2 * lk2)) + lam_init).reshape(1)
            kscale = RET_DK ** -0.5
            for gi, g in enumerate(groups):
                tm_in = tm if gi == 0 else 2 * tm
                crow = functools.partial(g["cond_row"], tm=tm_in)
                if gi == 0:
                    secs = (("none", 1.0, False), ("none", kscale, False), ("none", 1.0, False),
                            ("none", 1.0, False), ("none", 1.0, False), ("none", 1.0, True),
                            ("none", 1.0, True))
                    proj, kd, vd = _in_proj(g["x"], mod, l, w_in, secs, crow, tm_in, tm=tm_in)
                    outs.setdefault("kd", []).append(kd.reshape(B, 1, S, H_DIFF, 2 * DIFF_QK))
                    outs.setdefault("vd", []).append(vd.reshape(B, 1, S, H_DIFF, DIFF_V))
                    ret, sf, sb = _retention(proj, lg, ret_gn_w[e], g["batch"], g["seq"], write_state=True)
                    outs.setdefault("sf", []).append(sf.reshape(B, 1, H_RET, RET_DK, RET_DV))
                    outs.setdefault("sb", []).append(sb.reshape(B, 1, H_RET, RET_DK, RET_DV))
                    att = _diff_attention(proj, lam, diff_subln_w[e], lam_init, g["batch"], g["seq"],
                                          tq=min(256, g["seq"]))
                else:
                    secs = (("ret", 1.0, False), ("ret", kscale, False), ("none", 1.0, False),
                            ("none", 1.0, False), ("diff", 1.0, False), ("diff", 1.0, False),
                            ("none", 1.0, False))
                    tabs = {"ret": _rope_tables(g["seq"], RET_DK), "diff": _rope_tables(g["seq"], DIFF_QK)}
                    (proj,) = _in_proj(g["x"], mod, l, w_in, secs, crow, g["seq"], rope_tabs=tabs, tm=tm_in)
                    ret, = _retention(proj, lg, ret_gn_w[e], g["batch"], g["seq"],
                                      states=(state_ret_fwd[:, e], state_ret_bwd[:, e]))
                    ctx = (cache_diff_k[:, e].reshape(DB, PAST, H_DIFF * 2 * DIFF_QK),
                           cache_diff_v[:, e].reshape(DB, PAST, H_DIFF * DIFF_V))
                    att = _diff_attention(proj, lam, diff_subln_w[e], lam_init, g["batch"], g["seq"], ctx=ctx,
                                          tq=512)
                mixes.append((ret, att))
        else:
            o = l // 2
            w_in = hy_w_in[o].astype(BF16)
            w_out = hy_w_out[o].astype(BF16)
            secs = (("none", 1.0, False),) * (3 * D // SEC)
            for gi, g in enumerate(groups):
                if u_next is None:
                    crow = functools.partial(g["cond_row"], tm=tm)
                    (u,) = _in_proj(g["x"], mod, l, w_in, secs, crow, tm, tm=tm)
                    u_off = 0
                else:
                    u, u_off = u_next, g["off"]
                taps = _hyena_taps(g["seq"], hy_ffn_w1[o], hy_ffn_b1[o], hy_ffn_w2[o], hy_ffn_b2[o],
                                   hy_ffn_w3[o], hy_freq[o])
                blk = min(g["seq"], 1024)
                n_seq = max(1, 1024 // g["seq"])
                z = _hyena_conv(u, hy_conv_w[o], hy_conv_b[o], taps, hy_filter_bias[o], g["batch"], g["seq"],
                                blk, n_seq, row_off=u_off)
                mixes.append((z,))

        x1_all, h2_all, info_all, counts = _post_mix(mixes, [(g["x"], g["x_off"]) for g in groups], w_out, mod, l, ln_w[l, 0],
                                                     ln_b[l, 0], rw_hi, rw_lo, rbias, cond_row_all, tm)
        packed = info_all.reshape(T_ALL)
        counts_i = counts[:, 0].astype(jnp.int32)
        plan = _moe_plan(counts_i, n_tiles)
        hs = _dispatch(h2_all, packed, plan, counts_i, n_tiles)
        ys = _moe(hs, plan, l, rwt, moe_w_gate, moe_w_up, moe_w_down)
        if l + 1 < DEPTH and (l + 1) % 2 == 1:
            x_all, u_next = _fin_proj(x1_all, ys, packed, plan[3], mod, l, ln_w[l, 1], ln_b[l, 1],
                                      hy_w_in[(l + 1) // 2].astype(BF16), cond_row_all, tm=tm)
            for g in groups:
                g["x"], g["x_off"] = x_all, g["off"]
        else:
            u_next = None
            xs = _fin(x1_all, ys, packed, plan[3], mod, l, ln_w[l, 1], ln_b[l, 1], cond_row_all, (TP, TS))
            for g, x in zip(groups, xs):
                g["x"], g["x_off"] = x, 0

    y_prompt = groups[0]["x"].reshape(B, S, D)
    y_sample = groups[1]["x"].reshape(DB, DS, D)
    cat = lambda xs: xs[0] if len(xs) == 1 else jnp.concatenate(xs, axis=1)
    return (y_prompt, y_sample, cat(outs["kd"]), cat(outs["vd"]), cat(outs["sf"]), cat(outs["sb"]))
```

```python
import functools
import math

import numpy as np
import jax
import jax.numpy as jnp
from jax import lax
from jax.experimental import pallas as pl
from jax.experimental.pallas import tpu as pltpu

F32 = jnp.float32
BF16 = jnp.bfloat16
HIGHEST = lax.Precision.HIGHEST

D_MODEL = 1024
DEPTH = 2
GRID_W = 64
H_RET = 4
RET_DK = 128
RET_DV = 128
RET_CHUNK = 128
H_DIFF = 4
DIFF_QK = 64
DIFF_V = 128
ROPE_BASE = 10000.0
HY_BANDS = 16
HY_FH = 64
HY_DECAY_TARGET = 1e-2
HY_FAST = 0.3
HY_SLOW = 1.5
HY_SHIFT = 0.05
N_EXPERTS = 16
N_GROUPS = 4
GROUP_SIZE = N_EXPERTS // N_GROUPS
D_FF_EXPERT = 512
ALPHA = (2 * DEPTH) ** 0.25
LN_EPS = 1e-5

LANES = 128
SEC = 512
COND_ROWS = 8
N_MOD = 6
VMEM_LIMIT = 50 * 1024 * 1024

PAIR_SLOTS = ((0, 1), (2, 1), (2, 3), (0, 3), (0, 2), (1, 3))
N_PAIRS = len(PAIR_SLOTS)
N_CLASSES = N_GROUPS * N_PAIRS
MOE_TM = 256
MOE_SHIFT = MOE_TM.bit_length() - 1
RET_HEADS_PER_STEP = 2
POST_SUB = 512
TOK_ROWS = D_MODEL // LANES


def _cparams(sem):
    return pltpu.CompilerParams(dimension_semantics=sem, vmem_limit_bytes=VMEM_LIMIT)


def _silu(x):
    return x * jax.nn.sigmoid(x)


def _store_token_tiles(ref, x):
    n = x.shape[0]
    for j in range(TOK_ROWS):
        ref[pl.ds(j, n, stride=TOK_ROWS), :] = x[:, j * LANES:(j + 1) * LANES]


def _load_token_tiles(ref, n, slot=None):
    idx = () if slot is None else (slot,)
    return jnp.concatenate([ref[idx + (pl.ds(j, n, stride=TOK_ROWS), slice(None))] for j in range(TOK_ROWS)],
                           axis=1)


def _layer_norm(x, w, b):
    mu = jnp.mean(x, axis=-1, keepdims=True)
    xc = x - mu
    var = jnp.mean(xc * xc, axis=-1, keepdims=True)
    return xc * lax.rsqrt(var + LN_EPS) * w + b


def _dot_3pass(a, w):
    a_hi = a.astype(BF16)
    a_lo = (a - a_hi.astype(F32)).astype(BF16)
    w_hi = w.astype(BF16)
    w_lo = (w - w_hi.astype(F32)).astype(BF16)
    return (jnp.dot(a_hi, w_hi, preferred_element_type=F32) + jnp.dot(a_lo, w_hi, preferred_element_type=F32)
            + jnp.dot(a_hi, w_lo, preferred_element_type=F32))


def _ada_kernel(c_ref, w_ref, b_ref, o_ref):
    o_ref[...] = _dot_3pass(_silu(c_ref[...]), w_ref[...]) + b_ref[...]


def _ada_mod(cond8, ada_w, ada_b):
    tn = 1024
    nj = ada_w.shape[2] // tn
    return pl.pallas_call(
        _ada_kernel,
        grid=(DEPTH, nj),
        in_specs=[
            pl.BlockSpec((COND_ROWS, D_MODEL), lambda l, j: (0, 0)),
            pl.BlockSpec((None, D_MODEL, tn), lambda l, j: (l, 0, j)),
            pl.BlockSpec((None, 1, tn), lambda l, j: (l, 0, j)),
        ],
        out_specs=pl.BlockSpec((None, COND_ROWS, tn), lambda l, j: (l, 0, j)),
        out_shape=jax.ShapeDtypeStruct((DEPTH, COND_ROWS, ada_w.shape[2]), F32),
        compiler_params=_cparams(("arbitrary", "arbitrary")),
        name="ada_mod",
    )(cond8, ada_w, ada_b.reshape(DEPTH, 1, -1))


def _mod_spec(layer, chunk, row_of_tile):
    def imap(i, *_):
        return ((layer * COND_ROWS + row_of_tile(i)) * N_MOD + chunk, 0, 0)
    return pl.BlockSpec((None, 1, D_MODEL), imap)


def _rope(a, tabs, quarter):
    c, sa, sb = tabs
    out = []
    for hb in range(a.shape[1] // LANES):
        blk = a[:, hb * LANES:(hb + 1) * LANES]
        up = pltpu.roll(blk, LANES - quarter, axis=1)
        dn = pltpu.roll(blk, quarter, axis=1)
        out.append(blk * c + up * sa + dn * sb)
    return jnp.concatenate(out, axis=1)


def _cast_once(w_ref, w_bf16_ref):
    @pl.when(pl.program_id(0) == 0)
    def _():
        for c in range(0, w_ref.shape[1], SEC):
            w_bf16_ref[:, c:c + SEC] = w_ref[:, c:c + SEC].astype(BF16)


def _in_kernel(*refs, secs, n_f32_out):
    x_ref, sh_ref, sc_ref, w_ref = refs[:4]
    pos = 4
    tabs = {}
    for kind in ("ret", "diff"):
        if any(s[0] == kind for s in secs):
            tabs[kind] = tuple(r[...] for r in refs[pos:pos + 3])
            pos += 3
    o_ref = refs[pos]
    f32_refs = refs[pos + 1:pos + 1 + n_f32_out]
    w_bf16_ref = refs[-1]
    _cast_once(w_ref, w_bf16_ref)
    h = (x_ref[...] * (1.0 + sc_ref[...]) + sh_ref[...]).astype(BF16)
    k32 = 0
    for s, (kind, scale, want_f32) in enumerate(secs):
        acc = jnp.dot(h, w_bf16_ref[:, s * SEC:(s + 1) * SEC], preferred_element_type=F32)
        if scale != 1.0:
            acc = acc * scale
        if kind == "ret":
            acc = _rope(acc, tabs["ret"], RET_DK // 4)
        elif kind == "diff":
            acc = _rope(acc, tabs["diff"], DIFF_QK // 4)
        o_ref[:, s * SEC:(s + 1) * SEC] = acc.astype(BF16)
        if want_f32:
            for hb in range(SEC // LANES):
                f32_refs[k32][:, hb, :] = acc[:, hb * LANES:(hb + 1) * LANES]
            k32 += 1
    assert k32 == n_f32_out


def _in_proj(x2d, mod, layer, w_f32, secs, row_of_tile, seq_len, rope_tabs=None, tm=512):
    T = x2d.shape[0]
    N = w_f32.shape[1]
    assert N == SEC * len(secs) and T % tm == 0 and seq_len % tm == 0
    tiles_per_seq = seq_len // tm
    in_specs = [
        pl.BlockSpec((tm, D_MODEL), lambda i: (i, 0)),
        _mod_spec(layer, 0, row_of_tile),
        _mod_spec(layer, 1, row_of_tile),
        pl.BlockSpec((D_MODEL, N), lambda i: (0, 0), pipeline_mode=pl.Buffered(1)),
    ]
    args = [x2d, mod, mod, w_f32]
    for kind in ("ret", "diff"):
        if any(s[0] == kind for s in secs):
            for t in rope_tabs[kind]:
                in_specs.append(pl.BlockSpec((tm, LANES), lambda i: (i % tiles_per_seq, 0)))
                args.append(t)
    n_f32 = sum(1 for s in secs if s[2])
    heads = SEC // LANES
    out_shape = [jax.ShapeDtypeStruct((T, N), BF16)] + [jax.ShapeDtypeStruct((T, heads, LANES), F32)] * n_f32
    out_specs = ([pl.BlockSpec((tm, N), lambda i: (i, 0))]
                 + [pl.BlockSpec((tm, heads, LANES), lambda i: (i, 0, 0))] * n_f32)
    return pl.pallas_call(
        functools.partial(_in_kernel, secs=secs, n_f32_out=n_f32),
        grid=(T // tm,),
        in_specs=in_specs,
        out_specs=out_specs,
        out_shape=out_shape,
        scratch_shapes=[pltpu.VMEM((D_MODEL, N), BF16)],
        compiler_params=_cparams(("arbitrary",)),
        name="in_proj",
    )(*args)


def _rope_tables(seq_len, d):
    half = d // 2
    quarter = half // 2
    t = np.arange(seq_len)
    inv = ROPE_BASE ** (-np.arange(quarter, dtype=np.float64) / quarter)
    ang_r = (t // GRID_W)[:, None] * inv[None, :]
    ang_c = (t % GRID_W)[:, None] * inv[None, :]
    zero = np.zeros_like(ang_r)
    cos = np.concatenate([np.cos(ang_r)] * 2 + [np.cos(ang_c)] * 2, axis=1)
    sa = np.concatenate([-np.sin(ang_r), zero, -np.sin(ang_c), zero], axis=1)
    sb = np.concatenate([zero, np.sin(ang_r), zero, np.sin(ang_c)], axis=1)
    reps = LANES // d
    return tuple(jnp.asarray(np.tile(a, (1, reps)).astype(np.float32)) for a in (cos, sa, sb))


def _ret_kernel(*refs, n_chunks, has_state, write_state):
    lg_ref, q_ref, k_ref, v_ref, g_ref, gnw_ref = refs[:6]
    pos = 6
    if has_state:
        s0f_ref, s0b_ref = refs[pos:pos + 2]
        pos += 2
    o_ref = refs[pos]
    pos += 1
    if write_state:
        sf_ref, sb_ref = refs[pos:pos + 2]
        pos += 2
    acc_ref, kv_ref = refs[pos:pos + 2]
    for hh in range(RET_HEADS_PER_STEP):
        _ret_head(hh, lg_ref, q_ref, k_ref, v_ref, g_ref, gnw_ref,
                  (s0f_ref, s0b_ref) if has_state else None, o_ref, (sf_ref, sb_ref) if write_state else None,
                  acc_ref, kv_ref, n_chunks)


def _ret_head(hh, lg_ref, q_ref, k_ref, v_ref, g_ref, gnw_ref, s0_refs, o_ref, s_out_refs, acc_ref, kv_ref,
              n_chunks):
    C = RET_CHUNK
    cols = slice(hh * LANES, (hh + 1) * LANES)
    hd = pl.program_id(1) * RET_HEADS_PER_STEP + hh
    lgf = lg_ref[0, hd]
    lgb = lg_ref[1, hd]
    ii = lax.broadcasted_iota(jnp.int32, (C, C), 0).astype(F32)
    jj = lax.broadcasted_iota(jnp.int32, (C, C), 1).astype(F32)
    rel = ii - jj
    d_f = jnp.where(rel >= 0, jnp.exp(jnp.maximum(rel, 0.0) * lgf), 0.0)
    d_b = jnp.where(rel <= 0, jnp.exp(jnp.maximum(-rel, 0.0) * lgb), 0.0)
    d_sum = d_f + d_b
    idx = lax.broadcasted_iota(jnp.int32, (C, 1), 0).astype(F32)
    xi_f = jnp.exp((idx + 1.0) * lgf)
    zeta_f = jnp.exp((C - 1.0 - idx) * lgf)
    xi_b = jnp.exp((C - idx) * lgb)
    zeta_b = jnp.exp(idx * lgb)
    one = jnp.ones((1, 1), F32)
    gc_f = jnp.exp(one * (C * lgf))
    gc_b = jnp.exp(one * (C * lgb))

    nt = (((1,), (1,)), ((), ()))
    tn = (((0,), (0,)), ((), ()))

    if s0_refs is not None:
        s_f = s0_refs[0][hh]
        s_b = s0_refs[1][hh]
    else:
        s_f = jnp.zeros((RET_DK, RET_DV), F32)
        s_b = jnp.zeros((RET_DK, RET_DV), F32)

    for n in range(n_chunks):
        sl = slice(n * C, (n + 1) * C)
        qc, kc, vc = q_ref[sl, cols], k_ref[sl, cols], v_ref[sl, cols]
        scores = lax.dot_general(qc, kc, nt, preferred_element_type=F32) * d_sum
        acc_ref[sl, cols] = jnp.dot(scores.astype(BF16), vc, preferred_element_type=F32)
        kf = kc.astype(F32)
        kv_ref[0, hh, n] = lax.dot_general((kf * zeta_f).astype(BF16), vc, tn, preferred_element_type=F32)
        kv_ref[1, hh, n] = lax.dot_general((kf * zeta_b).astype(BF16), vc, tn, preferred_element_type=F32)

    for n in range(n_chunks):
        sl = slice(n * C, (n + 1) * C)
        qf = q_ref[sl, cols].astype(F32)
        acc_ref[sl, cols] += jnp.dot((qf * xi_f).astype(BF16), s_f.astype(BF16), preferred_element_type=F32)
        s_f = gc_f * s_f + kv_ref[0, hh, n]

    gnw = gnw_ref[:, cols]
    for n in reversed(range(n_chunks)):
        sl = slice(n * C, (n + 1) * C)
        qf = q_ref[sl, cols].astype(F32)
        cross = jnp.dot((qf * xi_b).astype(BF16), s_b.astype(BF16), preferred_element_type=F32)
        r = acc_ref[sl, cols] + cross
        mu = jnp.mean(r, axis=-1, keepdims=True)
        rc = r - mu
        var = jnp.mean(rc * rc, axis=-1, keepdims=True)
        rn = rc * lax.rsqrt(var + LN_EPS) * gnw
        o_ref[sl, cols] = (_silu(g_ref[sl, cols].astype(F32)) * rn).astype(BF16)
        s_b = gc_b * s_b + kv_ref[1, hh, n]

    if s_out_refs is not None:
        s_out_refs[0][hh] = s_f
        s_out_refs[1][hh] = s_b


def _retention(proj, lg, gn_w, batch, seq_len, states=None, write_state=False):
    T = proj.shape[0]
    hp = RET_HEADS_PER_STEP
    width = hp * LANES
    per_sec = SEC // width
    blk = lambda sec: pl.BlockSpec((seq_len, width), lambda b, h: (b, sec * per_sec + h))
    in_specs = [pl.BlockSpec(memory_space=pltpu.SMEM), blk(0), blk(1), blk(2), blk(3),
                pl.BlockSpec((1, width), lambda b, h: (0, h))]
    args = [lg, proj, proj, proj, proj, gn_w.reshape(1, -1)]
    if states is not None:
        st = pl.BlockSpec((None, hp, RET_DK, RET_DV), lambda b, h: (b, h, 0, 0))
        in_specs += [st, st]
        args += list(states)
    out_shape = [jax.ShapeDtypeStruct((T, SEC), BF16)]
    out_specs = [pl.BlockSpec((seq_len, width), lambda b, h: (b, h))]
    if write_state:
        st_o = pl.BlockSpec((None, hp, RET_DK, RET_DV), lambda b, h: (b, h, 0, 0))
        out_shape += [jax.ShapeDtypeStruct((batch, H_RET, RET_DK, RET_DV), F32)] * 2
        out_specs += [st_o, st_o]
    return pl.pallas_call(
        functools.partial(_ret_kernel, n_chunks=seq_len // RET_CHUNK, has_state=states is not None,
                          write_state=write_state),
        grid=(batch, H_RET // hp),
        in_specs=in_specs,
        out_specs=out_specs,
        out_shape=out_shape,
        scratch_shapes=[pltpu.VMEM((seq_len, width), F32),
                        pltpu.VMEM((2, hp, seq_len // RET_CHUNK, RET_DK, RET_DV), F32)],
        compiler_params=_cparams(("arbitrary", "arbitrary")),
        name="retention",
    )(*args)


def _att_kernel(*refs, has_ctx, out_scale, key_chunk):
    lam_ref, q_ref, k_ref, v_ref = refs[:4]
    pos = 4
    if has_ctx:
        ck_ref, cv_ref = refs[pos:pos + 2]
        pos += 2
    w_ref, o_ref = refs[pos:pos + 2]
    lam = lam_ref[0]
    q = q_ref[...]
    tq = q.shape[0]
    lane = lax.broadcasted_iota(jnp.int32, q.shape, 1)
    zero = jnp.zeros_like(q)
    qq = jnp.concatenate([jnp.where(lane < DIFF_QK, q, zero), jnp.where(lane >= DIFF_QK, q, zero)], axis=0)
    qq = qq * jnp.asarray(DIFF_QK ** -0.5, BF16)
    nt = (((1,), (1,)), ((), ()))
    chunks = [(k_ref, v_ref, c * key_chunk, key_chunk) for c in range(k_ref.shape[0] // key_chunk)]
    if has_ctx:
        chunks.append((ck_ref, cv_ref, 0, ck_ref.shape[0]))
    m = l = acc = None
    for kr, vr, off, n in chunks:
        kch = kr[off:off + n, :].astype(BF16)
        vch = vr[off:off + n, :].astype(BF16)
        s = lax.dot_general(qq, kch, nt, preferred_element_type=F32)
        cm = jnp.max(s, axis=-1, keepdims=True)
        m_new = cm if m is None else jnp.maximum(m, cm)
        p = jnp.exp(s - m_new)
        ps = jnp.sum(p, axis=-1, keepdims=True)
        pv = jnp.dot(p.astype(BF16), vch, preferred_element_type=F32)
        if m is None:
            l, acc = ps, pv
        else:
            alpha = jnp.exp(m - m_new)
            l = alpha * l + ps
            acc = alpha * acc + pv
        m = m_new
    o = acc / l
    att = o[:tq] - lam * o[tq:]
    att = att * lax.rsqrt(jnp.mean(att * att, axis=-1, keepdims=True) + LN_EPS)
    o_ref[...] = (att * w_ref[...] * out_scale).astype(BF16)


def _diff_attention(proj, lam, subln_w, lam_init, batch, seq_len, ctx=None, tq=256):
    T = proj.shape[0]
    hsec = SEC // LANES
    nq = seq_len // tq
    in_specs = [
        pl.BlockSpec(memory_space=pltpu.SMEM),
        pl.BlockSpec((tq, LANES), lambda b, h, i: (b * nq + i, 4 * hsec + h)),
        pl.BlockSpec((seq_len, LANES), lambda b, h, i: (b, 5 * hsec + h)),
        pl.BlockSpec((seq_len, LANES), lambda b, h, i: (b, 6 * hsec + h)),
    ]
    args = [lam, proj, proj, proj]
    if ctx is not None:
        ck, cv = ctx
        past = ck.shape[1]
        cspec = pl.BlockSpec((None, past, LANES), lambda b, h, i: (b, 0, h))
        in_specs += [cspec, cspec]
        args += [ck, cv]
    in_specs.append(pl.BlockSpec((1, LANES), lambda b, h, i: (0, 0)))
    args.append(subln_w.reshape(1, -1))
    return pl.pallas_call(
        functools.partial(_att_kernel, has_ctx=ctx is not None, out_scale=1.0 - lam_init,
                          key_chunk=min(512, seq_len)),
        grid=(batch, H_DIFF, nq),
        in_specs=in_specs,
        out_specs=pl.BlockSpec((tq, LANES), lambda b, h, i: (b * nq + i, h)),
        out_shape=jax.ShapeDtypeStruct((T, SEC), BF16),
        compiler_params=_cparams(("arbitrary", "arbitrary", "arbitrary")),
        name="diff_attention",
    )(*args)


def _route_class(lt, rb_ref):
    sel = [jax.nn.sigmoid(lt[e:e + 1, :]) + rb_ref[e] for e in range(N_EXPERTS)]
    gscore = []
    for g in range(N_GROUPS):
        mem = sel[g * GROUP_SIZE:(g + 1) * GROUP_SIZE]
        best = None
        for a in range(GROUP_SIZE):
            for b in range(a + 1, GROUP_SIZE):
                pair = mem[a] + mem[b]
                best = pair if best is None else jnp.maximum(best, pair)
        gscore.append(best)
    gbest = gscore[0]
    gidx = jnp.zeros_like(gbest)
    for g in range(1, N_GROUPS):
        upd = gscore[g] > gbest
        gidx = jnp.where(upd, float(g), gidx)
        gbest = jnp.where(upd, gscore[g], gbest)
    msel = []
    for j in range(GROUP_SIZE):
        out = sel[j]
        for g in range(1, N_GROUPS):
            out = jnp.where(gidx == float(g), sel[g * GROUP_SIZE + j], out)
        msel.append(out)
    one = jnp.ones_like(gbest)
    zero = jnp.zeros_like(gbest)
    chosen = []
    for j in range(GROUP_SIZE):
        rank = zero
        for k in range(GROUP_SIZE):
            if k < j:
                rank = rank + jnp.where(msel[k] >= msel[j], one, zero)
            elif k > j:
                rank = rank + jnp.where(msel[k] > msel[j], one, zero)
        chosen.append(jnp.where(rank < 2.0, one, zero))
    c0, c1, c2, c3 = chosen
    order = jnp.where(c0 * c1 > 0, 0.0, jnp.where(c1 * c2 > 0, 1.0, jnp.where(c2 * c3 > 0, 2.0,
            jnp.where(c0 * c3 > 0, 3.0, jnp.where(c0 * c2 > 0, 4.0, 5.0)))))
    return gidx * float(N_PAIRS) + order


def _post_kernel(*refs, n_mix, group_tiles):
    n_groups = len(group_tiles)
    per_group = n_mix + 1
    group_refs = [refs[g * per_group:(g + 1) * per_group] for g in range(n_groups)]
    refs = refs[n_groups * per_group:]
    (w_f32_ref, g_ref, sh_ref, sc_ref, lnw_ref, lnb_ref, rwh_ref, rwl_ref, rb_ref, tri_ref,
     x1_ref, h2_ref, info_ref, cout_ref, cnt_ref, pre_ref, w_ref) = refs
    _cast_once(w_f32_ref, w_ref)
    i = pl.program_id(0)

    @pl.when(i == 0)
    def _():
        cnt_ref[...] = jnp.zeros_like(cnt_ref)

    first = 0
    for g in range(n_groups):
        @pl.when(jnp.logical_and(i >= first, i < first + group_tiles[g]))
        def _(g=g):
            out = None
            off = 0
            for m_ref in group_refs[g][:n_mix]:
                width = m_ref.shape[1]
                part = jnp.dot(m_ref[...], w_ref[off:off + width, :], preferred_element_type=F32)
                out = part if out is None else out + part
                off += width
            pre_ref[...] = ALPHA * group_refs[g][n_mix][...] + g_ref[...] * out
        first += group_tiles[g]

    tm = pre_ref.shape[0]
    ts = tri_ref.shape[0]
    base = cnt_ref[:, 0:1]
    for part in range(tm // ts):
        rows = slice(part * ts, (part + 1) * ts)
        x1 = _layer_norm(pre_ref[rows, :], lnw_ref[...], lnb_ref[...])
        x1_ref[rows, :] = x1
        h2 = x1 * (1.0 + sc_ref[...]) + sh_ref[...]
        for j in range(TOK_ROWS):
            h2_ref[pl.ds(part * ts * TOK_ROWS + j, ts, stride=TOK_ROWS), :] = h2[:, j * LANES:(j + 1) * LANES]
        h_hi = h2.astype(BF16)
        h_lo = (h2 - h_hi.astype(F32)).astype(BF16)
        logits = (jnp.dot(h_hi, rwh_ref[...], preferred_element_type=F32)
                  + jnp.dot(h_lo, rwh_ref[...], preferred_element_type=F32)
                  + jnp.dot(h_hi, rwl_ref[...], preferred_element_type=F32))
        cls = _route_class(logits.T, rb_ref)
        crow = lax.broadcasted_iota(jnp.int32, (32, ts), 0).astype(F32)
        onehot = jnp.where(crow == cls, 1.0, 0.0)
        prefix = jnp.dot(onehot.astype(BF16), tri_ref[...], preferred_element_type=F32)
        rank = jnp.sum(onehot * (prefix - 1.0 + base), axis=0, keepdims=True)
        base = base + jnp.sum(onehot, axis=1, keepdims=True)
        packed = cls.astype(jnp.int32) * 65536 + rank.astype(jnp.int32)
        info_ref[:, rows] = packed
    cnt_ref[...] = jnp.broadcast_to(base, cnt_ref.shape)

    @pl.when(i == pl.num_programs(0) - 1)
    def _():
        cout_ref[...] = cnt_ref[...]


def _tri(tm):
    return jnp.asarray(np.triu(np.ones((tm, tm), np.float32))).astype(BF16)


def _group_spec(width, tm, first_tile, n_tiles, array_tile_off=0):
    return pl.BlockSpec((tm, width),
                        lambda i, *_: (jnp.clip(i - first_tile, 0, n_tiles - 1) + array_tile_off, 0))


def _post_mix(group_mixes, group_x, w_out_bf16, mod, layer, ln_w, ln_b, rw_hi, rw_lo, router_bias, cond_row, tm):
    group_tiles = tuple(m[0].shape[0] // tm for m in group_mixes)
    n_tiles = sum(group_tiles)
    total_rows = n_tiles * tm
    row = pl.BlockSpec((tm, D_MODEL), lambda i: (i, 0))
    vec = pl.BlockSpec((1, D_MODEL), lambda i: (0, 0))
    cnt = pl.BlockSpec((32, LANES), lambda i: (0, 0))
    in_specs, args = [], []
    first = 0
    for mixes, (x, x_row_off), nt in zip(group_mixes, group_x, group_tiles):
        for m in mixes:
            in_specs.append(_group_spec(m.shape[1], tm, first, nt))
            args.append(m)
        in_specs.append(_group_spec(D_MODEL, tm, first, nt, x_row_off // tm))
        args.append(x)
        first += nt
    in_specs += [
        pl.BlockSpec((D_MODEL, D_MODEL), lambda i: (0, 0), pipeline_mode=pl.Buffered(1)),
        _mod_spec(layer, 2, lambda i: cond_row(i, tm)), _mod_spec(layer, 3, lambda i: cond_row(i, tm)),
        _mod_spec(layer, 4, lambda i: cond_row(i, tm)),
        vec, vec,
        pl.BlockSpec((D_MODEL, LANES), lambda i: (0, 0)),
        pl.BlockSpec((D_MODEL, LANES), lambda i: (0, 0)),
        pl.BlockSpec(memory_space=pltpu.SMEM),
        pl.BlockSpec((POST_SUB, POST_SUB), lambda i: (0, 0)),
    ]
    args += [w_out_bf16, mod, mod, mod, ln_w.reshape(1, -1), ln_b.reshape(1, -1), rw_hi, rw_lo, router_bias,
             _tri(POST_SUB)]
    return pl.pallas_call(
        functools.partial(_post_kernel, n_mix=len(group_mixes[0]), group_tiles=group_tiles),
        grid=(n_tiles,),
        in_specs=in_specs,
        out_specs=[row, pl.BlockSpec((tm * TOK_ROWS, LANES), lambda i: (i, 0)),
                   pl.BlockSpec((1, tm), lambda i: (0, i)), cnt],
        out_shape=[jax.ShapeDtypeStruct((total_rows, D_MODEL), F32),
                   jax.ShapeDtypeStruct((total_rows * TOK_ROWS, LANES), F32),
                   jax.ShapeDtypeStruct((1, total_rows), jnp.int32), jax.ShapeDtypeStruct((32, LANES), F32)],
        scratch_shapes=[pltpu.VMEM((32, LANES), F32), pltpu.VMEM((tm, D_MODEL), F32),
                        pltpu.VMEM((D_MODEL, D_MODEL), BF16)],
        compiler_params=_cparams(("arbitrary",)),
        name="post_mix",
    )(*args)


def _plan_kernel(cnt_ref, ea_ref, eb_ref, fl_ref, rs_ref, *, n_tiles):
    start = jnp.int32(0)
    prev_a = jnp.int32(-1)
    prev_b = jnp.int32(-1)
    for c in range(N_CLASSES):
        n = cnt_ref[c]
        tiles = lax.shift_right_logical(n + (MOE_TM - 1), MOE_SHIFT)
        row0 = start * MOE_TM
        rs_ref[c] = row0
        g, pr = divmod(c, N_PAIRS)
        a = g * GROUP_SIZE + PAIR_SLOTS[pr][0]
        b = g * GROUP_SIZE + PAIR_SLOTS[pr][1]
        first = 1 + 4 * (prev_a != a).astype(jnp.int32) + 8 * (prev_b != b).astype(jnp.int32)

        def tile_body(k, _, start=start, a=a, b=b, first=first):
            t = start + k
            ea_ref[t] = a
            eb_ref[t] = b
            fl_ref[t] = jnp.where(k == 0, first, 1)
            return 0

        lax.fori_loop(0, tiles, tile_body, 0)
        has = tiles > 0
        prev_a = jnp.where(has, a, prev_a)
        prev_b = jnp.where(has, b, prev_b)
        start = start + tiles
    for c in range(N_CLASSES, 31):
        rs_ref[c] = 0
    rs_ref[31] = start

    def idle_body(t, _):
        ea_ref[t] = prev_a
        eb_ref[t] = prev_b
        fl_ref[t] = 0
        return 0

    lax.fori_loop(start, n_tiles, idle_body, 0)


def _moe_plan(counts, n_tiles):
    smem = pl.BlockSpec(memory_space=pltpu.SMEM)
    i32 = lambda n: jax.ShapeDtypeStruct((n,), jnp.int32)
    return pl.pallas_call(
        functools.partial(_plan_kernel, n_tiles=n_tiles),
        in_specs=[smem],
        out_specs=[smem] * 4,
        out_shape=[i32(n_tiles), i32(n_tiles), i32(n_tiles), i32(32)],
        name="moe_plan",
    )(counts)


def _class_row(packed, rs_ref):
    return rs_ref[lax.shift_right_logical(packed, 16)] + (packed & 0xFFFF)


def _dispatch_kernel(packed_ref, rs_ref, cnt_ref, h_ref, hs_ref, sem, fill_sem, *, tm, n_tiles):
    i = pl.program_id(0)
    used = rs_ref[31]

    def tile_rows(row, n=1):
        return pl.ds(pl.multiple_of(row * TOK_ROWS, TOK_ROWS), n * TOK_ROWS)

    def fill_copy(row, n):
        return pltpu.make_async_copy(h_ref.at[tile_rows(0, n)], hs_ref.at[tile_rows(row, n)], fill_sem)

    def fill(act):
        for c in range(N_CLASSES):
            n = cnt_ref[c]
            row = rs_ref[c] + n
            pad = (-n) & (MOE_TM - 1)
            for bit in reversed(range(MOE_SHIFT)):
                size = 1 << bit

                @pl.when((pad & size) != 0)
                def _(row=row, size=size):
                    act(fill_copy(row, size))
                row = row + (pad & size)
        lax.fori_loop(used, n_tiles, lambda t, _: (act(fill_copy(t * MOE_TM, MOE_TM)), 0)[1], 0)

    @pl.when(i == 0)
    def _():
        fill(lambda cp: cp.start())

    def body(r8, _):
        for j in range(8):
            r = r8 * 8 + j
            row = _class_row(packed_ref[0, i * tm + r], rs_ref)
            pltpu.make_async_copy(h_ref.at[tile_rows(r)], hs_ref.at[tile_rows(row)], sem).start(priority=j % 2)
        return 0

    lax.fori_loop(0, tm // 8, body, 0)
    pltpu.make_async_copy(h_ref, hs_ref.at[tile_rows(0, tm)], sem).wait()

    @pl.when(i == 0)
    def _():
        fill(lambda cp: cp.wait())


def _dispatch(h2_tiles, packed, plan, counts, n_tiles, tm=1024):
    T = packed.shape[1]
    assert tm >= MOE_TM and T % tm == 0
    grid_spec = pltpu.PrefetchScalarGridSpec(
        num_scalar_prefetch=3,
        grid=(T // tm,),
        in_specs=[pl.BlockSpec((tm * TOK_ROWS, LANES), lambda i, *_: (i, 0))],
        out_specs=pl.BlockSpec(memory_space=pl.ANY),
        scratch_shapes=[pltpu.SemaphoreType.DMA(()), pltpu.SemaphoreType.DMA(())],
    )
    return pl.pallas_call(
        functools.partial(_dispatch_kernel, tm=tm, n_tiles=n_tiles),
        grid_spec=grid_spec,
        out_shape=jax.ShapeDtypeStruct((n_tiles * MOE_TM * TOK_ROWS, LANES), F32),
        compiler_params=_cparams(("arbitrary",)),
        name="moe_dispatch",
    )(packed, plan[3], counts, h2_tiles)


def _moe_kernel(ea_ref, eb_ref, fl_ref, h_ref, rwt_ref, wga, wua, wda, wgb, wub, wdb, o_ref,
                sga, sua, sda, sgb, sub, sdb):
    t = pl.program_id(0)
    flags = fl_ref[t]
    valid = (flags & 1) != 0

    @pl.when((flags & 4) != 0)
    def _():
        sga[...] = wga[...].astype(BF16)
        sua[...] = wua[...].astype(BF16)
        sda[...] = wda[...].astype(BF16)

    @pl.when((flags & 8) != 0)
    def _():
        sgb[...] = wgb[...].astype(BF16)
        sub[...] = wub[...].astype(BF16)
        sdb[...] = wdb[...].astype(BF16)

    @pl.when(valid)
    def _():
        h = _load_token_tiles(h_ref, MOE_TM).astype(BF16)
        nt = (((1,), (1,)), ((), ()))
        score = jax.nn.sigmoid(lax.dot_general(h, rwt_ref[...], nt, preferred_element_type=F32))
        lane = lax.broadcasted_iota(jnp.int32, score.shape, 1)
        s_a = jnp.sum(jnp.where(lane == ea_ref[t], score, 0.0), axis=1, keepdims=True)
        s_b = jnp.sum(jnp.where(lane == eb_ref[t], score, 0.0), axis=1, keepdims=True)
        tot = s_a + s_b
        act_a = _silu(jnp.dot(h, sga[...], preferred_element_type=F32)) \
            * jnp.dot(h, sua[...], preferred_element_type=F32) * (s_a / tot)
        act_b = _silu(jnp.dot(h, sgb[...], preferred_element_type=F32)) \
            * jnp.dot(h, sub[...], preferred_element_type=F32) * (s_b / tot)
        _store_token_tiles(o_ref, jnp.dot(act_a.astype(BF16), sda[...], preferred_element_type=F32)
                           + jnp.dot(act_b.astype(BF16), sdb[...], preferred_element_type=F32))

    @pl.when(jnp.logical_not(valid))
    def _():
        o_ref[...] = jnp.zeros_like(o_ref)


def _moe(hs, plan, layer, rwt_bf16, w_gate, w_up, w_down):
    ea, eb, flags, _ = plan
    n_tiles = ea.shape[0]
    wspec = lambda shape, which: pl.BlockSpec(
        (None, None) + shape, (lambda t, ea, eb, fl: (layer, ea[t], 0, 0)) if which == 0
        else (lambda t, ea, eb, fl: (layer, eb[t], 0, 0)))
    up = (D_MODEL, D_FF_EXPERT)
    dn = (D_FF_EXPERT, D_MODEL)
    tile = pl.BlockSpec((MOE_TM * TOK_ROWS, LANES), lambda t, ea, eb, fl: (t, 0))
    grid_spec = pltpu.PrefetchScalarGridSpec(
        num_scalar_prefetch=3,
        grid=(n_tiles,),
        in_specs=[
            tile,
            pl.BlockSpec((N_EXPERTS, D_MODEL), lambda t, ea, eb, fl: (0, 0)),
            wspec(up, 0), wspec(up, 0), wspec(dn, 0), wspec(up, 1), wspec(up, 1), wspec(dn, 1),
        ],
        out_specs=tile,
        scratch_shapes=[pltpu.VMEM(up, BF16), pltpu.VMEM(up, BF16), pltpu.VMEM(dn, BF16),
                        pltpu.VMEM(up, BF16), pltpu.VMEM(up, BF16), pltpu.VMEM(dn, BF16)],
    )
    return pl.pallas_call(
        _moe_kernel,
        grid_spec=grid_spec,
        out_shape=jax.ShapeDtypeStruct((n_tiles * MOE_TM * TOK_ROWS, LANES), F32),
        compiler_params=_cparams(("arbitrary",)),
        name="moe",
    )(ea, eb, flags, hs, rwt_bf16, w_gate, w_up, w_down, w_gate, w_up, w_down)


def _gather_expert_rows(packed_ref, rs_ref, ys_ref, ybuf, sem, tm):
    i = pl.program_id(0)
    n_i = pl.num_programs(0)
    slot = i % 2

    def gather(tile, buf_slot):
        def body(r8, _):
            for j in range(8):
                r = r8 * 8 + j
                d = _class_row(packed_ref[0, tile * tm + r], rs_ref)
                pltpu.make_async_copy(ys_ref.at[pl.ds(pl.multiple_of(d * TOK_ROWS, TOK_ROWS), TOK_ROWS)],
                                      ybuf.at[buf_slot, pl.ds(pl.multiple_of(r * TOK_ROWS, TOK_ROWS), TOK_ROWS)],
                                      sem.at[buf_slot]).start(priority=j % 2)
            return 0
        lax.fori_loop(0, tm // 8, body, 0)

    @pl.when(i == 0)
    def _():
        gather(0, 0)

    @pl.when(i + 1 < n_i)
    def _():
        gather(jnp.minimum(i + 1, n_i - 1), 1 - slot)

    pltpu.make_async_copy(ys_ref.at[pl.ds(0, tm * TOK_ROWS)], ybuf.at[slot], sem.at[slot]).wait()
    return _load_token_tiles(ybuf, tm, slot)


def _fin_proj_kernel(packed_ref, rs_ref, x_ref, ys_ref, g_ref, lnw_ref, lnb_ref, sh_ref, sc_ref, w_f32_ref,
                     x2_ref, u_ref, ybuf, sem, w_ref, *, tm):
    _cast_once(w_f32_ref, w_ref)
    y = _gather_expert_rows(packed_ref, rs_ref, ys_ref, ybuf, sem, tm)
    x2 = _layer_norm(ALPHA * x_ref[...] + g_ref[...] * y, lnw_ref[...], lnb_ref[...])
    x2_ref[...] = x2
    h = (x2 * (1.0 + sc_ref[...]) + sh_ref[...]).astype(BF16)
    for s in range(u_ref.shape[1] // SEC):
        u_ref[:, s * SEC:(s + 1) * SEC] = jnp.dot(h, w_ref[:, s * SEC:(s + 1) * SEC],
                                                  preferred_element_type=F32).astype(BF16)


def _fin_proj(x1, ys, packed, row_start, mod, layer, ln_w, ln_b, w_next, cond_row, tm=512):
    T = x1.shape[0]
    N = w_next.shape[1]
    row = pl.BlockSpec((tm, D_MODEL), lambda i, *_: (i, 0))
    vec = pl.BlockSpec((1, D_MODEL), lambda i, *_: (0, 0))
    crow = lambda i: cond_row(i, tm)
    grid_spec = pltpu.PrefetchScalarGridSpec(
        num_scalar_prefetch=2,
        grid=(T // tm,),
        in_specs=[row, pl.BlockSpec(memory_space=pl.ANY), _mod_spec(layer, 5, crow), vec, vec,
                  _mod_spec(layer + 1, 0, crow), _mod_spec(layer + 1, 1, crow),
                  pl.BlockSpec((D_MODEL, N), lambda i, *_: (0, 0), pipeline_mode=pl.Buffered(1))],
        out_specs=[row, pl.BlockSpec((tm, N), lambda i, *_: (i, 0))],
        scratch_shapes=[pltpu.VMEM((2, tm * TOK_ROWS, LANES), F32), pltpu.SemaphoreType.DMA((2,)),
                        pltpu.VMEM((D_MODEL, N), BF16)],
    )
    return pl.pallas_call(
        functools.partial(_fin_proj_kernel, tm=tm),
        grid_spec=grid_spec,
        out_shape=[jax.ShapeDtypeStruct((T, D_MODEL), F32), jax.ShapeDtypeStruct((T, N), BF16)],
        compiler_params=_cparams(("arbitrary",)),
        name="post_moe_in_proj",
    )(packed, row_start, x1, ys, mod, ln_w.reshape(1, -1), ln_b.reshape(1, -1), mod, mod, w_next)


def _fin_kernel(packed_ref, rs_ref, x_ref, ys_ref, g_ref, lnw_ref, lnb_ref, *rest, tm, group_tiles):
    o_refs = rest[:len(group_tiles)]
    ybuf, sem = rest[len(group_tiles):]
    i = pl.program_id(0)
    y = _layer_norm(ALPHA * x_ref[...] + g_ref[...] * _gather_expert_rows(packed_ref, rs_ref, ys_ref, ybuf, sem, tm),
                    lnw_ref[...], lnb_ref[...])
    first = 0
    for o_ref, nt in zip(o_refs, group_tiles):
        @pl.when(jnp.logical_and(i >= first, i < first + nt))
        def _(o_ref=o_ref):
            o_ref[...] = y
        first += nt


def _fin(x1, ys, packed, row_start, mod, layer, ln_w, ln_b, cond_row, group_rows, tm=256):
    group_tiles = tuple(n // tm for n in group_rows)
    row = pl.BlockSpec((tm, D_MODEL), lambda i, *_: (i, 0))
    vec = pl.BlockSpec((1, D_MODEL), lambda i, *_: (0, 0))
    out_specs, first = [], 0
    for nt in group_tiles:
        out_specs.append(_group_spec(D_MODEL, tm, first, nt))
        first += nt
    grid_spec = pltpu.PrefetchScalarGridSpec(
        num_scalar_prefetch=2,
        grid=(sum(group_tiles),),
        in_specs=[row, pl.BlockSpec(memory_space=pl.ANY), _mod_spec(layer, 5, lambda i: cond_row(i, tm)), vec, vec],
        out_specs=out_specs,
        scratch_shapes=[pltpu.VMEM((2, tm * TOK_ROWS, LANES), F32), pltpu.SemaphoreType.DMA((2,))],
    )
    return pl.pallas_call(
        functools.partial(_fin_kernel, tm=tm, group_tiles=group_tiles),
        grid_spec=grid_spec,
        out_shape=[jax.ShapeDtypeStruct((n, D_MODEL), F32) for n in group_rows],
        compiler_params=_cparams(("arbitrary",)),
        name="post_moe",
    )(packed, row_start, x1, ys, mod, ln_w.reshape(1, -1), ln_b.reshape(1, -1))


def _filt_kernel(z_ref, w1_ref, b1_ref, w2_ref, b2_ref, fr_ref, w3_ref, dl_ref, o_ref):
    i = pl.program_id(0)
    z = z_ref[...]
    fr = fr_ref[...]
    a = jnp.sin(fr * (jnp.dot(z, w1_ref[...], precision=HIGHEST, preferred_element_type=F32) + b1_ref[...]))
    a = jnp.sin(fr * (jnp.dot(a, w2_ref[...], precision=HIGHEST, preferred_element_type=F32) + b2_ref[...]))
    filt = _dot_3pass(a, w3_ref[...])
    window = jnp.exp(-z[:, 0:1] * dl_ref[...]) + HY_SHIFT
    rows = i * z.shape[0] + lax.broadcasted_iota(jnp.int32, filt.shape, 0)
    o_ref[...] = jnp.where(rows == 0, 0.0, filt * window)


def _hyena_taps(seq_len, w1, b1, w2, b2, w3, freq):
    L = seq_len
    t = np.linspace(0.0, 1.0, L)[:, None]
    bands = np.linspace(1e-4, HY_BANDS - 1, HY_BANDS)
    ang = 2.0 * math.pi * bands[None, :] * np.arange(L)[:, None] / L
    z = np.concatenate([t, np.cos(ang), -np.sin(ang)], axis=-1)
    offs = np.minimum(np.abs(np.arange(2 * L) - L), L - 1)
    z2_np = np.zeros((2 * L, LANES), np.float32)
    z2_np[:, :z.shape[1]] = z[offs]
    z2 = jnp.asarray(z2_np)
    pad_c = lambda a: jnp.zeros((a.shape[0], LANES), F32).at[:, :a.shape[1]].set(a)
    pad_r = lambda a: jnp.zeros((LANES, a.shape[1]), F32).at[:a.shape[0], :].set(a)
    w1p = pad_r(pad_c(w1))
    w2p = pad_r(pad_c(w2))
    w3p = pad_r(w3)
    b1p, b2p, frp = pad_c(b1[None, :]), pad_c(b2[None, :]), pad_c(freq[None, :])
    deltas = jnp.asarray(np.abs(np.linspace(math.log(HY_DECAY_TARGET) / HY_SLOW, math.log(HY_DECAY_TARGET) / HY_FAST,
                                            D_MODEL))[None, :].astype(np.float32))
    rb = min(1024, L)
    cbf = D_MODEL
    ncb = D_MODEL // cbf
    nrb_back = L // rb
    sq = pl.BlockSpec((LANES, LANES), lambda i, j: (0, 0))
    vec = pl.BlockSpec((1, LANES), lambda i, j: (0, 0))
    return pl.pallas_call(
        _filt_kernel,
        grid=(2 * L // rb, ncb),
        in_specs=[
            pl.BlockSpec((rb, LANES), lambda i, j: (i, 0)),
            sq, vec, sq, vec, vec,
            pl.BlockSpec((LANES, cbf), lambda i, j: (0, jnp.where(i < nrb_back, ncb + j, j))),
            pl.BlockSpec((1, cbf), lambda i, j: (0, j)),
        ],
        out_specs=pl.BlockSpec((rb, cbf), lambda i, j: (i, j)),
        out_shape=jax.ShapeDtypeStruct((2 * L, D_MODEL), F32),
        compiler_params=_cparams(("arbitrary", "arbitrary")),
        name="hyena_taps",
    )(z2, w1p, b1p, w2p, b2p, frp, w3p, deltas)


def _dft_mats(cb):
    n = 2 * cb
    m = np.arange(cb)
    f = np.arange(cb)
    ang = 2.0 * np.pi * ((f[:, None] * m[None, :]) % n) / n
    fwd = np.concatenate([np.cos(ang), -np.sin(ang)], axis=0)
    fwd[cb, :] = np.where(m % 2 == 0, 1.0, -1.0)
    coef = np.where(f == 0, 1.0, 2.0)[None, :] / n
    inv = np.concatenate([coef * np.cos(ang.T), -coef * np.sin(ang.T)], axis=1)
    inv[:, cb] = np.where(m % 2 == 0, 1.0, -1.0) / n
    return fwd.astype(np.float32), inv.astype(np.float32)


def _hconv_kernel(x0_ref, x1_ref, v_ref, cw0_ref, cw1_ref, cw2_ref, cb0_ref, cb1_ref, cb2_ref,
                  taps_ref, fb_ref, fwd_ref, inv_ref, o_ref,
                  hs_ref, stage_ref, w32_ref, w_ref, x0c_ref, u_ref, y_ref, *, seq_len, blk, n_seq):
    L = seq_len
    nb = L // blk
    cw = x0_ref.shape[1]
    RC = 256
    bi = pl.program_id(1)
    row0 = lax.broadcasted_iota(jnp.int32, (RC, cw), 0) == 0

    @pl.when(bi == 0)
    def _():
        rows = lax.broadcasted_iota(jnp.int32, (2 * blk, 1), 0)
        sign = jnp.where(rows % 2 == 0, 1.0, -1.0)
        real_row = rows <= blk
        prev = None
        for jb in range(2 * nb):
            cur = jnp.dot(fwd_ref[...], taps_ref[jb * blk:(jb + 1) * blk, :].astype(BF16),
                          preferred_element_type=F32)
            if prev is not None:
                first_tap = taps_ref[(jb - 1) * blk:(jb - 1) * blk + 1, :].astype(BF16).astype(F32)
                hs_ref[jb - 1] = cur + sign * (prev - jnp.where(real_row, first_tap, 0.0))
            prev = cur

    zeros8 = jnp.zeros((8, cw), F32)
    for s in range(n_seq):
        r0 = s * L

        def short_conv(k, src_ref, cw_ref, cb_ref, store):
            stage_ref[k, 0:8, :] = zeros8
            stage_ref[k, 8 + L:16 + L, :] = zeros8
            for c in range(L // RC):
                stage_ref[k, 8 + c * RC:8 + (c + 1) * RC, :] = \
                    src_ref[r0 + c * RC:r0 + (c + 1) * RC, :].astype(F32)
            w = cw_ref[...]
            for c in range(L // RC):
                lo = stage_ref[k, 7 + c * RC:7 + (c + 1) * RC, :]
                mid = stage_ref[k, 8 + c * RC:8 + (c + 1) * RC, :]
                hi = stage_ref[k, 9 + c * RC:9 + (c + 1) * RC, :]
                store(c, lo * w[0:1, :] + mid * w[1:2, :] + hi * w[2:3, :] + cb_ref[...])

        def st_x1(c, val):
            w32_ref[c * RC:(c + 1) * RC, :] = val

        def st_v(c, val):
            w_ref[c * RC:(c + 1) * RC, :] = (w32_ref[c * RC:(c + 1) * RC, :] * val).astype(BF16)

        def st_x0(c, val):
            x0c_ref[c * RC:(c + 1) * RC, :] = val.astype(BF16)

        short_conv(0, x1_ref, cw1_ref, cb1_ref, st_x1)
        short_conv(1, v_ref, cw2_ref, cb2_ref, st_v)
        short_conv(2, x0_ref, cw0_ref, cb0_ref, st_x0)

        for j in range(nb):
            u_ref[j] = jnp.dot(fwd_ref[...], w_ref[j * blk:(j + 1) * blk, :], preferred_element_type=F32)

        for i in range(nb):
            for c in range(blk // RC):
                re = None
                im = None
                for j in range(nb):
                    k = i - j + nb - 1
                    a = u_ref[j, c * RC:(c + 1) * RC, :]
                    b = u_ref[j, blk + c * RC:blk + (c + 1) * RC, :]
                    hr = hs_ref[k, c * RC:(c + 1) * RC, :]
                    hi = hs_ref[k, blk + c * RC:blk + (c + 1) * RC, :]
                    bb = b * hi
                    if c == 0:
                        t_re = a * hr - jnp.where(row0, 0.0, bb)
                        t_im = jnp.where(row0, bb, a * hi + b * hr)
                    else:
                        t_re = a * hr - bb
                        t_im = a * hi + b * hr
                    re = t_re if re is None else re + t_re
                    im = t_im if im is None else im + t_im
                y_ref[i, c * RC:(c + 1) * RC, :] = re.astype(BF16)
                y_ref[i, blk + c * RC:blk + (c + 1) * RC, :] = im.astype(BF16)
            conv = jnp.dot(inv_ref[...], y_ref[i], preferred_element_type=F32)
            sl = slice(i * blk, (i + 1) * blk)
            z = x0c_ref[sl, :].astype(F32) * (conv + w_ref[sl, :].astype(F32) * fb_ref[...])
            o_ref[r0 + i * blk:r0 + (i + 1) * blk, :] = z.astype(BF16)


def _hyena_conv(u, conv_w, conv_b, taps, filt_bias, batch, seq_len, blk, n_seq, row_off=0, cw=256):
    T = batch * seq_len
    L = seq_len
    nb = L // blk
    ncw = D_MODEL // cw
    fwd_np, inv_np = _dft_mats(blk)
    fwd = jnp.asarray(fwd_np).astype(BF16)
    inv = jnp.asarray(inv_np).astype(BF16)
    rows = n_seq * L
    assert row_off % rows == 0
    boff = row_off // rows
    sec = lambda s: pl.BlockSpec((rows, cw), lambda c, b: (b + boff, s * ncw + c))
    cws = lambda s: pl.BlockSpec((3, cw), lambda c, b: (0, s * ncw + c))
    cbs = lambda s: pl.BlockSpec((1, cw), lambda c, b: (0, s * ncw + c))
    return pl.pallas_call(
        functools.partial(_hconv_kernel, seq_len=L, blk=blk, n_seq=n_seq),
        grid=(ncw, batch // n_seq),
        in_specs=[sec(0), sec(1), sec(2), cws(0), cws(1), cws(2), cbs(0), cbs(1), cbs(2),
                  pl.BlockSpec((2 * L, cw), lambda c, b: (0, c), pipeline_mode=pl.Buffered(1)),
                  pl.BlockSpec((1, cw), lambda c, b: (0, c)),
                  pl.BlockSpec((2 * blk, blk), lambda c, b: (0, 0), pipeline_mode=pl.Buffered(1)),
                  pl.BlockSpec((blk, 2 * blk), lambda c, b: (0, 0), pipeline_mode=pl.Buffered(1))],
        out_specs=pl.BlockSpec((rows, cw), lambda c, b: (b, c)),
        out_shape=jax.ShapeDtypeStruct((T, D_MODEL), BF16),
        scratch_shapes=[
            pltpu.VMEM((2 * nb - 1, 2 * blk, cw), F32),
            pltpu.VMEM((3, L + 16, cw), F32),
            pltpu.VMEM((L, cw), F32),
            pltpu.VMEM((L, cw), BF16),
            pltpu.VMEM((L, cw), BF16),
            pltpu.VMEM((nb, 2 * blk, cw), F32),
            pltpu.VMEM((nb, 2 * blk, cw), BF16),
        ],
        compiler_params=_cparams(("arbitrary", "arbitrary")),
        name="hyena_conv",
    )(u, u, u, conv_w, conv_w, conv_w, conv_b.reshape(1, -1), conv_b.reshape(1, -1), conv_b.reshape(1, -1),
      taps, filt_bias.reshape(1, -1), fwd, inv)


def kernel(x_prompt, x_sample, cache_diff_k, cache_diff_v, state_ret_fwd, state_ret_bwd, c, c_ctx, ada_w, ada_b, ln_w, ln_b, ev_w_in, ev_w_out, ret_decay_fwd, ret_decay_bwd, ret_gn_w, diff_lambda, diff_subln_w, hy_w_in, hy_conv_w, hy_conv_b, hy_ffn_w1, hy_ffn_b1, hy_ffn_w2, hy_ffn_b2, hy_ffn_w3, hy_freq, hy_filter_bias, hy_w_out, router_w, router_bias, moe_w_gate, moe_w_up, moe_w_down):
    B, S, D = x_prompt.shape
    DB, DS, _ = x_sample.shape
    PAST = cache_diff_k.shape[2]
    TP, TS = B * S, DB * DS
    T_ALL = TP + TS
    assert D == D_MODEL and 1 + DB <= COND_ROWS and T_ALL < 65536

    cond8 = jnp.zeros((COND_ROWS, D), F32).at[0].set(c_ctx).at[1:1 + DB].set(c)
    mod = _ada_mod(cond8, ada_w, ada_b).reshape(DEPTH * COND_ROWS * N_MOD, 1, D)

    tm = 512
    groups = [
        dict(x=x_prompt.reshape(TP, D), x_off=0, batch=B, seq=S, off=0, cond_row=lambda i, tm: 0),
        dict(x=x_sample.reshape(TS, D), x_off=0, batch=DB, seq=DS, off=TP,
             cond_row=lambda i, tm: 1 + (i * tm) // DS),
    ]
    u_next = None

    def cond_row_all(i, tm):
        return jnp.where(i * tm < TP, 0, 1 + (i * tm - TP) // DS)

    rw_pad = jnp.zeros((D, LANES), F32).at[:, :N_EXPERTS].set(router_w.astype(F32))
    rw_hi = rw_pad.astype(BF16)
    rw_lo = (rw_pad - rw_hi.astype(F32)).astype(BF16)
    rwt = router_w.T.astype(BF16)
    rbias = router_bias.astype(F32)
    n_tiles = T_ALL // MOE_TM + N_CLASSES
    outs = {}

    for l in range(DEPTH):
        mixes = []
        if l % 2 == 0:
            e = l // 2
            w_in = ev_w_in[e]
            w_out = ev_w_out[e]
            lg = jnp.stack([jnp.log1p(-jnp.exp2(ret_decay_fwd[e].astype(F32))),
                            jnp.log1p(-jnp.exp2(ret_decay_bwd[e].astype(F32)))])
            lam_init = 0.8 - 0.6 * math.exp(-0.3 * l)
            lq1, lk1, lq2, lk2 = diff_lambda[e].astype(F32)
            lam = (jnp.exp(jnp.sum(lq1 * lk1)) - jnp.exp(jnp.sum(lq2 * lk2)) + lam_init).reshape(1)
            kscale = RET_DK ** -0.5
            for gi, g in enumerate(groups):
                tm_in = tm
                crow = functools.partial(g["cond_row"], tm=tm_in)
                if gi == 0:
                    secs = (("none", 1.0, False), ("none", kscale, False), ("none", 1.0, False),
                            ("none", 1.0, False), ("none", 1.0, False), ("none", 1.0, True),
                            ("none", 1.0, True))
                    proj, kd, vd = _in_proj(g["x"], mod, l, w_in, secs, crow, tm_in, tm=tm_in)
                    outs.setdefault("kd", []).append(kd.reshape(B, 1, S, H_DIFF, 2 * DIFF_QK))
                    outs.setdefault("vd", []).append(vd.reshape(B, 1, S, H_DIFF, DIFF_V))
                    ret, sf, sb = _retention(proj, lg, ret_gn_w[e], g["batch"], g["seq"], write_state=True)
                    outs.setdefault("sf", []).append(sf.reshape(B, 1, H_RET, RET_DK, RET_DV))
                    outs.setdefault("sb", []).append(sb.reshape(B, 1, H_RET, RET_DK, RET_DV))
                    att = _diff_attention(proj, lam, diff_subln_w[e], lam_init, g["batch"], g["seq"],
                                          tq=min(256, g["seq"]))
                else:
                    secs = (("ret", 1.0, False), ("ret", kscale, False), ("none", 1.0, False),
                            ("none", 1.0, False), ("diff", 1.0, False), ("diff", 1.0, False),
                            ("none", 1.0, False))
                    tabs = {"ret": _rope_tables(g["seq"], RET_DK), "diff": _rope_tables(g["seq"], DIFF_QK)}
                    (proj,) = _in_proj(g["x"], mod, l, w_in, secs, crow, g["seq"], rope_tabs=tabs, tm=tm_in)
                    ret, = _retention(proj, lg, ret_gn_w[e], g["batch"], g["seq"],
                                      states=(state_ret_fwd[:, e], state_ret_bwd[:, e]))
                    ctx = (cache_diff_k[:, e].reshape(DB, PAST, H_DIFF * 2 * DIFF_QK),
                           cache_diff_v[:, e].reshape(DB, PAST, H_DIFF * DIFF_V))
                    att = _diff_attention(proj, lam, diff_subln_w[e], lam_init, g["batch"], g["seq"], ctx=ctx,
                                          tq=512)
                mixes.append((ret, att))
        else:
            o = l // 2
            w_in = hy_w_in[o]
            w_out = hy_w_out[o]
            secs = (("none", 1.0, False),) * (3 * D // SEC)
            for gi, g in enumerate(groups):
                if u_next is None:
                    crow = functools.partial(g["cond_row"], tm=tm)
                    (u,) = _in_proj(g["x"], mod, l, w_in, secs, crow, tm, tm=tm)
                    u_off = 0
                else:
                    u, u_off = u_next, g["off"]
                taps = _hyena_taps(g["seq"], hy_ffn_w1[o], hy_ffn_b1[o], hy_ffn_w2[o], hy_ffn_b2[o],
                                   hy_ffn_w3[o], hy_freq[o])
                blk = min(g["seq"], 1024)
                n_seq = max(1, 1024 // g["seq"])
                z = _hyena_conv(u, hy_conv_w[o], hy_conv_b[o], taps, hy_filter_bias[o], g["batch"], g["seq"],
                                blk, n_seq, row_off=u_off)
                mixes.append((z,))

        x1_all, h2_all, info_all, counts = _post_mix(mixes, [(g["x"], g["x_off"]) for g in groups], w_out, mod, l, ln_w[l, 0],
                                                     ln_b[l, 0], rw_hi, rw_lo, rbias, cond_row_all, tm)
        packed = info_all
        counts_i = counts[:, 0].astype(jnp.int32)
        plan = _moe_plan(counts_i, n_tiles)
        hs = _dispatch(h2_all, packed, plan, counts_i, n_tiles)
        ys = _moe(hs, plan, l, rwt, moe_w_gate, moe_w_up, moe_w_down)
        if l + 1 < DEPTH and (l + 1) % 2 == 1:
            x_all, u_next = _fin_proj(x1_all, ys, packed, plan[3], mod, l, ln_w[l, 1], ln_b[l, 1],
                                      hy_w_in[(l + 1) // 2], cond_row_all, tm=tm)
            for g in groups:
                g["x"], g["x_off"] = x_all, g["off"]
        else:
            u_next = None
            xs = _fin(x1_all, ys, packed, plan[3], mod, l, ln_w[l, 1], ln_b[l, 1], cond_row_all, (TP, TS))
            for g, x in zip(groups, xs):
                g["x"], g["x_off"] = x, 0

    y_prompt = groups[0]["x"].reshape(B, S, D)
    y_sample = groups[1]["x"].reshape(DB, DS, D)
    cat = lambda xs: xs[0] if len(xs) == 1 else jnp.concatenate(xs, axis=1)
    return (y_prompt, y_sample, cat(outs["kd"]), cat(outs["vd"]), cat(outs["sf"]), cat(outs["sb"]))
```

```python
import functools
import math

import numpy as np
import jax
import jax.numpy as jnp
from jax import lax
from jax.experimental import pallas as pl
from jax.experimental.pallas import tpu as pltpu

F32 = jnp.float32
BF16 = jnp.bfloat16
HIGHEST = lax.Precision.HIGHEST

D_MODEL = 1024
DEPTH = 2
GRID_W = 64
H_RET = 4
RET_DK = 128
RET_DV = 128
RET_CHUNK = 128
H_DIFF = 4
DIFF_QK = 64
DIFF_V = 128
ROPE_BASE = 10000.0
HY_BANDS = 16
HY_FH = 64
HY_DECAY_TARGET = 1e-2
HY_FAST = 0.3
HY_SLOW = 1.5
HY_SHIFT = 0.05
N_EXPERTS = 16
N_GROUPS = 4
GROUP_SIZE = N_EXPERTS // N_GROUPS
D_FF_EXPERT = 512
ALPHA = (2 * DEPTH) ** 0.25
LN_EPS = 1e-5

LANES = 128
SEC = 512
COND_ROWS = 8
N_MOD = 6
VMEM_LIMIT = 50 * 1024 * 1024

PAIR_SLOTS = ((0, 1), (2, 1), (2, 3), (0, 3), (0, 2), (1, 3))
N_PAIRS = len(PAIR_SLOTS)
N_CLASSES = N_GROUPS * N_PAIRS
MOE_TM = 256
MOE_SHIFT = MOE_TM.bit_length() - 1
RET_HEADS_PER_STEP = 2
POST_SUB = 512
TOK_ROWS = D_MODEL // LANES


def _cparams(sem):
    return pltpu.CompilerParams(dimension_semantics=sem, vmem_limit_bytes=VMEM_LIMIT)


def _silu(x):
    return x * jax.nn.sigmoid(x)


def _store_token_tiles(ref, x):
    n = x.shape[0]
    for j in range(TOK_ROWS):
        ref[pl.ds(j, n, stride=TOK_ROWS), :] = x[:, j * LANES:(j + 1) * LANES]


def _load_token_tiles(ref, n, slot=None):
    idx = () if slot is None else (slot,)
    return jnp.concatenate([ref[idx + (pl.ds(j, n, stride=TOK_ROWS), slice(None))] for j in range(TOK_ROWS)],
                           axis=1)


def _layer_norm(x, w, b):
    mu = jnp.mean(x, axis=-1, keepdims=True)
    xc = x - mu
    var = jnp.mean(xc * xc, axis=-1, keepdims=True)
    return xc * lax.rsqrt(var + LN_EPS) * w + b


def _dot_3pass(a, w):
    a_hi = a.astype(BF16)
    a_lo = (a - a_hi.astype(F32)).astype(BF16)
    w_hi = w.astype(BF16)
    w_lo = (w - w_hi.astype(F32)).astype(BF16)
    return (jnp.dot(a_hi, w_hi, preferred_element_type=F32) + jnp.dot(a_lo, w_hi, preferred_element_type=F32)
            + jnp.dot(a_hi, w_lo, preferred_element_type=F32))


def _ada_kernel(c_ref, w_ref, b_ref, o_ref):
    o_ref[...] = _dot_3pass(_silu(c_ref[...]), w_ref[...]) + b_ref[...]


def _ada_mod(cond8, ada_w, ada_b):
    tn = 1024
    nj = ada_w.shape[2] // tn
    return pl.pallas_call(
        _ada_kernel,
        grid=(DEPTH, nj),
        in_specs=[
            pl.BlockSpec((COND_ROWS, D_MODEL), lambda l, j: (0, 0)),
            pl.BlockSpec((None, D_MODEL, tn), lambda l, j: (l, 0, j)),
            pl.BlockSpec((None, 1, tn), lambda l, j: (l, 0, j)),
        ],
        out_specs=pl.BlockSpec((None, COND_ROWS, tn), lambda l, j: (l, 0, j)),
        out_shape=jax.ShapeDtypeStruct((DEPTH, COND_ROWS, ada_w.shape[2]), F32),
        compiler_params=_cparams(("arbitrary", "arbitrary")),
        name="ada_mod",
    )(cond8, ada_w, ada_b.reshape(DEPTH, 1, -1))


def _mod_spec(layer, chunk, row_of_tile):
    def imap(i, *_):
        return ((layer * COND_ROWS + row_of_tile(i)) * N_MOD + chunk, 0, 0)
    return pl.BlockSpec((None, 1, D_MODEL), imap)


def _rope(a, tabs, quarter):
    c, sa, sb = tabs
    out = []
    for hb in range(a.shape[1] // LANES):
        blk = a[:, hb * LANES:(hb + 1) * LANES]
        up = pltpu.roll(blk, LANES - quarter, axis=1)
        dn = pltpu.roll(blk, quarter, axis=1)
        out.append(blk * c + up * sa + dn * sb)
    return jnp.concatenate(out, axis=1)


def _cast_once(w_ref, w_bf16_ref):
    @pl.when(pl.program_id(0) == 0)
    def _():
        for c in range(0, w_ref.shape[1], SEC):
            w_bf16_ref[:, c:c + SEC] = w_ref[:, c:c + SEC].astype(BF16)


def _in_kernel(*refs, secs, n_f32_out):
    x_ref, sh_ref, sc_ref, w_ref = refs[:4]
    pos = 4
    tabs = {}
    for kind in ("ret", "diff"):
        if any(s[0] == kind for s in secs):
            tabs[kind] = tuple(r[...] for r in refs[pos:pos + 3])
            pos += 3
    o_ref = refs[pos]
    f32_refs = refs[pos + 1:pos + 1 + n_f32_out]
    w_bf16_ref = refs[-1]
    _cast_once(w_ref, w_bf16_ref)
    h = (x_ref[...] * (1.0 + sc_ref[...]) + sh_ref[...]).astype(BF16)
    k32 = 0
    for s, (kind, scale, want_f32) in enumerate(secs):
        acc = jnp.dot(h, w_bf16_ref[:, s * SEC:(s + 1) * SEC], preferred_element_type=F32)
        if scale != 1.0:
            acc = acc * scale
        if kind == "ret":
            acc = _rope(acc, tabs["ret"], RET_DK // 4)
        elif kind == "diff":
            acc = _rope(acc, tabs["diff"], DIFF_QK // 4)
        o_ref[:, s * SEC:(s + 1) * SEC] = acc.astype(BF16)
        if want_f32:
            for hb in range(SEC // LANES):
                f32_refs[k32][:, hb, :] = acc[:, hb * LANES:(hb + 1) * LANES]
            k32 += 1
    assert k32 == n_f32_out


def _in_proj(x2d, mod, layer, w_f32, secs, row_of_tile, seq_len, rope_tabs=None, tm=512):
    T = x2d.shape[0]
    N = w_f32.shape[1]
    assert N == SEC * len(secs) and T % tm == 0 and seq_len % tm == 0
    tiles_per_seq = seq_len // tm
    in_specs = [
        pl.BlockSpec((tm, D_MODEL), lambda i: (i, 0)),
        _mod_spec(layer, 0, row_of_tile),
        _mod_spec(layer, 1, row_of_tile),
        pl.BlockSpec((D_MODEL, N), lambda i: (0, 0), pipeline_mode=pl.Buffered(1)),
    ]
    args = [x2d, mod, mod, w_f32]
    for kind in ("ret", "diff"):
        if any(s[0] == kind for s in secs):
            for t in rope_tabs[kind]:
                in_specs.append(pl.BlockSpec((tm, LANES), lambda i: (i % tiles_per_seq, 0)))
                args.append(t)
    n_f32 = sum(1 for s in secs if s[2])
    heads = SEC // LANES
    out_shape = [jax.ShapeDtypeStruct((T, N), BF16)] + [jax.ShapeDtypeStruct((T, heads, LANES), F32)] * n_f32
    out_specs = ([pl.BlockSpec((tm, N), lambda i: (i, 0))]
                 + [pl.BlockSpec((tm, heads, LANES), lambda i: (i, 0, 0))] * n_f32)
    return pl.pallas_call(
        functools.partial(_in_kernel, secs=secs, n_f32_out=n_f32),
        grid=(T // tm,),
        in_specs=in_specs,
        out_specs=out_specs,
        out_shape=out_shape,
        scratch_shapes=[pltpu.VMEM((D_MODEL, N), BF16)],
        compiler_params=_cparams(("arbitrary",)),
        name="in_proj",
    )(*args)


def _rope_tables(seq_len, d):
    half = d // 2
    quarter = half // 2
    t = np.arange(seq_len)
    inv = ROPE_BASE ** (-np.arange(quarter, dtype=np.float64) / quarter)
    ang_r = (t // GRID_W)[:, None] * inv[None, :]
    ang_c = (t % GRID_W)[:, None] * inv[None, :]
    zero = np.zeros_like(ang_r)
    cos = np.concatenate([np.cos(ang_r)] * 2 + [np.cos(ang_c)] * 2, axis=1)
    sa = np.concatenate([-np.sin(ang_r), zero, -np.sin(ang_c), zero], axis=1)
    sb = np.concatenate([zero, np.sin(ang_r), zero, np.sin(ang_c)], axis=1)
    reps = LANES // d
    return tuple(jnp.asarray(np.tile(a, (1, reps)).astype(np.float32)) for a in (cos, sa, sb))


def _ret_kernel(*refs, n_chunks, has_state, write_state):
    lg_ref, q_ref, k_ref, v_ref, g_ref, gnw_ref = refs[:6]
    pos = 6
    if has_state:
        s0f_ref, s0b_ref = refs[pos:pos + 2]
        pos += 2
    o_ref = refs[pos]
    pos += 1
    if write_state:
        sf_ref, sb_ref = refs[pos:pos + 2]
        pos += 2
    acc_ref, kv_ref = refs[pos:pos + 2]
    for hh in range(RET_HEADS_PER_STEP):
        _ret_head(hh, lg_ref, q_ref, k_ref, v_ref, g_ref, gnw_ref,
                  (s0f_ref, s0b_ref) if has_state else None, o_ref, (sf_ref, sb_ref) if write_state else None,
                  acc_ref, kv_ref, n_chunks)


def _ret_head(hh, lg_ref, q_ref, k_ref, v_ref, g_ref, gnw_ref, s0_refs, o_ref, s_out_refs, acc_ref, kv_ref,
              n_chunks):
    C = RET_CHUNK
    cols = slice(hh * LANES, (hh + 1) * LANES)
    hd = pl.program_id(1) * RET_HEADS_PER_STEP + hh
    lgf = lg_ref[0, hd]
    lgb = lg_ref[1, hd]
    ii = lax.broadcasted_iota(jnp.int32, (C, C), 0).astype(F32)
    jj = lax.broadcasted_iota(jnp.int32, (C, C), 1).astype(F32)
    rel = ii - jj
    d_f = jnp.where(rel >= 0, jnp.exp(jnp.maximum(rel, 0.0) * lgf), 0.0)
    d_b = jnp.where(rel <= 0, jnp.exp(jnp.maximum(-rel, 0.0) * lgb), 0.0)
    d_sum = d_f + d_b
    idx = lax.broadcasted_iota(jnp.int32, (C, 1), 0).astype(F32)
    xi_f = jnp.exp((idx + 1.0) * lgf)
    zeta_f = jnp.exp((C - 1.0 - idx) * lgf)
    xi_b = jnp.exp((C - idx) * lgb)
    zeta_b = jnp.exp(idx * lgb)
    one = jnp.ones((1, 1), F32)
    gc_f = jnp.exp(one * (C * lgf))
    gc_b = jnp.exp(one * (C * lgb))

    nt = (((1,), (1,)), ((), ()))
    tn = (((0,), (0,)), ((), ()))

    if s0_refs is not None:
        s_f = s0_refs[0][hh]
        s_b = s0_refs[1][hh]
    else:
        s_f = jnp.zeros((RET_DK, RET_DV), F32)
        s_b = jnp.zeros((RET_DK, RET_DV), F32)

    for n in range(n_chunks):
        sl = slice(n * C, (n + 1) * C)
        qc, kc, vc = q_ref[sl, cols], k_ref[sl, cols], v_ref[sl, cols]
        scores = lax.dot_general(qc, kc, nt, preferred_element_type=F32) * d_sum
        acc_ref[sl, cols] = jnp.dot(scores.astype(BF16), vc, preferred_element_type=F32)
        kf = kc.astype(F32)
        kv_ref[0, hh, n] = lax.dot_general((kf * zeta_f).astype(BF16), vc, tn, preferred_element_type=F32)
        kv_ref[1, hh, n] = lax.dot_general((kf * zeta_b).astype(BF16), vc, tn, preferred_element_type=F32)

    for n in range(n_chunks):
        sl = slice(n * C, (n + 1) * C)
        qf = q_ref[sl, cols].astype(F32)
        acc_ref[sl, cols] += jnp.dot((qf * xi_f).astype(BF16), s_f.astype(BF16), preferred_element_type=F32)
        s_f = gc_f * s_f + kv_ref[0, hh, n]

    gnw = gnw_ref[:, cols]
    for n in reversed(range(n_chunks)):
        sl = slice(n * C, (n + 1) * C)
        qf = q_ref[sl, cols].astype(F32)
        cross = jnp.dot((qf * xi_b).astype(BF16), s_b.astype(BF16), preferred_element_type=F32)
        r = acc_ref[sl, cols] + cross
        mu = jnp.mean(r, axis=-1, keepdims=True)
        rc = r - mu
        var = jnp.mean(rc * rc, axis=-1, keepdims=True)
        rn = rc * lax.rsqrt(var + LN_EPS) * gnw
        o_ref[sl, cols] = (_silu(g_ref[sl, cols].astype(F32)) * rn).astype(BF16)
        s_b = gc_b * s_b + kv_ref[1, hh, n]

    if s_out_refs is not None:
        s_out_refs[0][hh] = s_f
        s_out_refs[1][hh] = s_b


def _retention(proj, lg, gn_w, batch, seq_len, states=None, write_state=False):
    T = proj.shape[0]
    hp = RET_HEADS_PER_STEP
    width = hp * LANES
    per_sec = SEC // width
    blk = lambda sec: pl.BlockSpec((seq_len, width), lambda b, h: (b, sec * per_sec + h))
    in_specs = [pl.BlockSpec(memory_space=pltpu.SMEM), blk(0), blk(1), blk(2), blk(3),
                pl.BlockSpec((1, width), lambda b, h: (0, h))]
    args = [lg, proj, proj, proj, proj, gn_w.reshape(1, -1)]
    if states is not None:
        st = pl.BlockSpec((None, hp, RET_DK, RET_DV), lambda b, h: (b, h, 0, 0))
        in_specs += [st, st]
        args += list(states)
    out_shape = [jax.ShapeDtypeStruct((T, SEC), BF16)]
    out_specs = [pl.BlockSpec((seq_len, width), lambda b, h: (b, h))]
    if write_state:
        st_o = pl.BlockSpec((None, hp, RET_DK, RET_DV), lambda b, h: (b, h, 0, 0))
        out_shape += [jax.ShapeDtypeStruct((batch, H_RET, RET_DK, RET_DV), F32)] * 2
        out_specs += [st_o, st_o]
    return pl.pallas_call(
        functools.partial(_ret_kernel, n_chunks=seq_len // RET_CHUNK, has_state=states is not None,
                          write_state=write_state),
        grid=(batch, H_RET // hp),
        in_specs=in_specs,
        out_specs=out_specs,
        out_shape=out_shape,
        scratch_shapes=[pltpu.VMEM((seq_len, width), F32),
                        pltpu.VMEM((2, hp, seq_len // RET_CHUNK, RET_DK, RET_DV), F32)],
        compiler_params=_cparams(("arbitrary", "arbitrary")),
        name="retention",
    )(*args)


def _att_kernel(*refs, has_ctx, out_scale, key_chunk):
    lam_ref, q_ref, k_ref, v_ref = refs[:4]
    pos = 4
    if has_ctx:
        ck_ref, cv_ref = refs[pos:pos + 2]
        pos += 2
    w_ref, o_ref = refs[pos:pos + 2]
    lam = lam_ref[0]
    q = q_ref[...]
    tq = q.shape[0]
    lane = lax.broadcasted_iota(jnp.int32, q.shape, 1)
    zero = jnp.zeros_like(q)
    qq = jnp.concatenate([jnp.where(lane < DIFF_QK, q, zero), jnp.where(lane >= DIFF_QK, q, zero)], axis=0)
    qq = qq * jnp.asarray(DIFF_QK ** -0.5, BF16)
    nt = (((1,), (1,)), ((), ()))
    chunks = [(k_ref, v_ref, c * key_chunk, key_chunk) for c in range(k_ref.shape[0] // key_chunk)]
    if has_ctx:
        chunks.append((ck_ref, cv_ref, 0, ck_ref.shape[0]))
    m = l = acc = None
    for kr, vr, off, n in chunks:
        kch = kr[off:off + n, :].astype(BF16)
        vch = vr[off:off + n, :].astype(BF16)
        s = lax.dot_general(qq, kch, nt, preferred_element_type=F32)
        cm = jnp.max(s, axis=-1, keepdims=True)
        m_new = cm if m is None else jnp.maximum(m, cm)
        p = jnp.exp(s - m_new)
        ps = jnp.sum(p, axis=-1, keepdims=True)
        pv = jnp.dot(p.astype(BF16), vch, preferred_element_type=F32)
        if m is None:
            l, acc = ps, pv
        else:
            alpha = jnp.exp(m - m_new)
            l = alpha * l + ps
            acc = alpha * acc + pv
        m = m_new
    o = acc / l
    att = o[:tq] - lam * o[tq:]
    att = att * lax.rsqrt(jnp.mean(att * att, axis=-1, keepdims=True) + LN_EPS)
    o_ref[...] = (att * w_ref[...] * out_scale).astype(BF16)


def _diff_attention(proj, lam, subln_w, lam_init, batch, seq_len, ctx=None, tq=256):
    T = proj.shape[0]
    hsec = SEC // LANES
    nq = seq_len // tq
    in_specs = [
        pl.BlockSpec(memory_space=pltpu.SMEM),
        pl.BlockSpec((tq, LANES), lambda b, h, i: (b * nq + i, 4 * hsec + h)),
        pl.BlockSpec((seq_len, LANES), lambda b, h, i: (b, 5 * hsec + h)),
        pl.BlockSpec((seq_len, LANES), lambda b, h, i: (b, 6 * hsec + h)),
    ]
    args = [lam, proj, proj, proj]
    if ctx is not None:
        ck, cv = ctx
        past = ck.shape[1]
        cspec = pl.BlockSpec((None, past, LANES), lambda b, h, i: (b, 0, h))
        in_specs += [cspec, cspec]
        args += [ck, cv]
    in_specs.append(pl.BlockSpec((1, LANES), lambda b, h, i: (0, 0)))
    args.append(subln_w.reshape(1, -1))
    return pl.pallas_call(
        functools.partial(_att_kernel, has_ctx=ctx is not None, out_scale=1.0 - lam_init,
                          key_chunk=min(512, seq_len)),
        grid=(batch, H_DIFF, nq),
        in_specs=in_specs,
        out_specs=pl.BlockSpec((tq, LANES), lambda b, h, i: (b * nq + i, h)),
        out_shape=jax.ShapeDtypeStruct((T, SEC), BF16),
        compiler_params=_cparams(("arbitrary", "arbitrary", "arbitrary")),
        name="diff_attention",
    )(*args)


def _route_class(lt, rb_ref):
    sel = [jax.nn.sigmoid(lt[e:e + 1, :]) + rb_ref[e] for e in range(N_EXPERTS)]
    gscore = []
    for g in range(N_GROUPS):
        mem = sel[g * GROUP_SIZE:(g + 1) * GROUP_SIZE]
        best = None
        for a in range(GROUP_SIZE):
            for b in range(a + 1, GROUP_SIZE):
                pair = mem[a] + mem[b]
                best = pair if best is None else jnp.maximum(best, pair)
        gscore.append(best)
    gbest = gscore[0]
    gidx = jnp.zeros_like(gbest)
    for g in range(1, N_GROUPS):
        upd = gscore[g] > gbest
        gidx = jnp.where(upd, float(g), gidx)
        gbest = jnp.where(upd, gscore[g], gbest)
    msel = []
    for j in range(GROUP_SIZE):
        out = sel[j]
        for g in range(1, N_GROUPS):
            out = jnp.where(gidx == float(g), sel[g * GROUP_SIZE + j], out)
        msel.append(out)
    one = jnp.ones_like(gbest)
    zero = jnp.zeros_like(gbest)
    chosen = []
    for j in range(GROUP_SIZE):
        rank = zero
        for k in range(GROUP_SIZE):
            if k < j:
                rank = rank + jnp.where(msel[k] >= msel[j], one, zero)
            elif k > j:
                rank = rank + jnp.where(msel[k] > msel[j], one, zero)
        chosen.append(jnp.where(rank < 2.0, one, zero))
    c0, c1, c2, c3 = chosen
    order = jnp.where(c0 * c1 > 0, 0.0, jnp.where(c1 * c2 > 0, 1.0, jnp.where(c2 * c3 > 0, 2.0,
            jnp.where(c0 * c3 > 0, 3.0, jnp.where(c0 * c2 > 0, 4.0, 5.0)))))
    return gidx * float(N_PAIRS) + order


def _post_kernel(*refs, n_mix, group_tiles):
    n_groups = len(group_tiles)
    per_group = n_mix + 1
    group_refs = [refs[g * per_group:(g + 1) * per_group] for g in range(n_groups)]
    refs = refs[n_groups * per_group:]
    (w_f32_ref, g_ref, sh_ref, sc_ref, lnw_ref, lnb_ref, rwh_ref, rwl_ref, rb_ref, tri_ref,
     x1_ref, h2_ref, info_ref, cout_ref, cnt_ref, pre_ref, w_ref) = refs
    _cast_once(w_f32_ref, w_ref)
    i = pl.program_id(0)

    @pl.when(i == 0)
    def _():
        cnt_ref[...] = jnp.zeros_like(cnt_ref)

    first = 0
    for g in range(n_groups):
        @pl.when(jnp.logical_and(i >= first, i < first + group_tiles[g]))
        def _(g=g):
            out = None
            off = 0
            for m_ref in group_refs[g][:n_mix]:
                width = m_ref.shape[1]
                part = jnp.dot(m_ref[...], w_ref[off:off + width, :], preferred_element_type=F32)
                out = part if out is None else out + part
                off += width
            pre_ref[...] = ALPHA * group_refs[g][n_mix][...] + g_ref[...] * out
        first += group_tiles[g]

    tm = pre_ref.shape[0]
    ts = tri_ref.shape[0]
    base = cnt_ref[:, 0:1]
    for part in range(tm // ts):
        rows = slice(part * ts, (part + 1) * ts)
        x1 = _layer_norm(pre_ref[rows, :], lnw_ref[...], lnb_ref[...])
        x1_ref[rows, :] = x1
        h2 = x1 * (1.0 + sc_ref[...]) + sh_ref[...]
        for j in range(TOK_ROWS):
            h2_ref[pl.ds(part * ts * TOK_ROWS + j, ts, stride=TOK_ROWS), :] = h2[:, j * LANES:(j + 1) * LANES]
        h_hi = h2.astype(BF16)
        h_lo = (h2 - h_hi.astype(F32)).astype(BF16)
        logits = (jnp.dot(h_hi, rwh_ref[...], preferred_element_type=F32)
                  + jnp.dot(h_lo, rwh_ref[...], preferred_element_type=F32)
                  + jnp.dot(h_hi, rwl_ref[...], preferred_element_type=F32))
        cls = _route_class(logits.T, rb_ref)
        crow = lax.broadcasted_iota(jnp.int32, (32, ts), 0).astype(F32)
        onehot = jnp.where(crow == cls, 1.0, 0.0)
        prefix = jnp.dot(onehot.astype(BF16), tri_ref[...], preferred_element_type=F32)
        rank = jnp.sum(onehot * (prefix - 1.0 + base), axis=0, keepdims=True)
        base = base + jnp.sum(onehot, axis=1, keepdims=True)
        packed = cls.astype(jnp.int32) * 65536 + rank.astype(jnp.int32)
        info_ref[:, rows] = packed
    cnt_ref[...] = jnp.broadcast_to(base, cnt_ref.shape)

    @pl.when(i == pl.num_programs(0) - 1)
    def _():
        cout_ref[...] = cnt_ref[...]


def _tri(tm):
    return jnp.asarray(np.triu(np.ones((tm, tm), np.float32))).astype(BF16)


def _group_spec(width, tm, first_tile, n_tiles, array_tile_off=0):
    return pl.BlockSpec((tm, width),
                        lambda i, *_: (jnp.clip(i - first_tile, 0, n_tiles - 1) + array_tile_off, 0))


def _post_mix(group_mixes, group_x, w_out_bf16, mod, layer, ln_w, ln_b, rw_hi, rw_lo, router_bias, cond_row, tm):
    group_tiles = tuple(m[0].shape[0] // tm for m in group_mixes)
    n_tiles = sum(group_tiles)
    total_rows = n_tiles * tm
    row = pl.BlockSpec((tm, D_MODEL), lambda i: (i, 0))
    vec = pl.BlockSpec((1, D_MODEL), lambda i: (0, 0))
    cnt = pl.BlockSpec((32, LANES), lambda i: (0, 0))
    in_specs, args = [], []
    first = 0
    for mixes, (x, x_row_off), nt in zip(group_mixes, group_x, group_tiles):
        for m in mixes:
            in_specs.append(_group_spec(m.shape[1], tm, first, nt))
            args.append(m)
        in_specs.append(_group_spec(D_MODEL, tm, first, nt, x_row_off // tm))
        args.append(x)
        first += nt
    in_specs += [
        pl.BlockSpec((D_MODEL, D_MODEL), lambda i: (0, 0), pipeline_mode=pl.Buffered(1)),
        _mod_spec(layer, 2, lambda i: cond_row(i, tm)), _mod_spec(layer, 3, lambda i: cond_row(i, tm)),
        _mod_spec(layer, 4, lambda i: cond_row(i, tm)),
        vec, vec,
        pl.BlockSpec((D_MODEL, LANES), lambda i: (0, 0)),
        pl.BlockSpec((D_MODEL, LANES), lambda i: (0, 0)),
        pl.BlockSpec(memory_space=pltpu.SMEM),
        pl.BlockSpec((POST_SUB, POST_SUB), lambda i: (0, 0)),
    ]
    args += [w_out_bf16, mod, mod, mod, ln_w.reshape(1, -1), ln_b.reshape(1, -1), rw_hi, rw_lo, router_bias,
             _tri(POST_SUB)]
    return pl.pallas_call(
        functools.partial(_post_kernel, n_mix=len(group_mixes[0]), group_tiles=group_tiles),
        grid=(n_tiles,),
        in_specs=in_specs,
        out_specs=[row, pl.BlockSpec((tm * TOK_ROWS, LANES), lambda i: (i, 0)),
                   pl.BlockSpec((1, tm), lambda i: (0, i)), cnt],
        out_shape=[jax.ShapeDtypeStruct((total_rows, D_MODEL), F32),
                   jax.ShapeDtypeStruct((total_rows * TOK_ROWS, LANES), F32),
                   jax.ShapeDtypeStruct((1, total_rows), jnp.int32), jax.ShapeDtypeStruct((32, LANES), F32)],
        scratch_shapes=[pltpu.VMEM((32, LANES), F32), pltpu.VMEM((tm, D_MODEL), F32),
                        pltpu.VMEM((D_MODEL, D_MODEL), BF16)],
        compiler_params=_cparams(("arbitrary",)),
        name="post_mix",
    )(*args)


def _plan_kernel(cnt_ref, ea_ref, eb_ref, fl_ref, rs_ref, *, n_tiles):
    start = jnp.int32(0)
    prev_a = jnp.int32(-1)
    prev_b = jnp.int32(-1)
    for c in range(N_CLASSES):
        n = cnt_ref[c]
        tiles = lax.shift_right_logical(n + (MOE_TM - 1), MOE_SHIFT)
        row0 = start * MOE_TM
        rs_ref[c] = row0
        g, pr = divmod(c, N_PAIRS)
        a = g * GROUP_SIZE + PAIR_SLOTS[pr][0]
        b = g * GROUP_SIZE + PAIR_SLOTS[pr][1]
        first = 1 + 4 * (prev_a != a).astype(jnp.int32) + 8 * (prev_b != b).astype(jnp.int32)

        def tile_body(k, _, start=start, a=a, b=b, first=first):
            t = start + k
            ea_ref[t] = a
            eb_ref[t] = b
            fl_ref[t] = jnp.where(k == 0, first, 1)
            return 0

        lax.fori_loop(0, tiles, tile_body, 0)
        has = tiles > 0
        prev_a = jnp.where(has, a, prev_a)
        prev_b = jnp.where(has, b, prev_b)
        start = start + tiles
    for c in range(N_CLASSES, 31):
        rs_ref[c] = 0
    rs_ref[31] = start

    def idle_body(t, _):
        ea_ref[t] = prev_a
        eb_ref[t] = prev_b
        fl_ref[t] = 0
        return 0

    lax.fori_loop(start, n_tiles, idle_body, 0)


def _moe_plan(counts, n_tiles):
    smem = pl.BlockSpec(memory_space=pltpu.SMEM)
    i32 = lambda n: jax.ShapeDtypeStruct((n,), jnp.int32)
    return pl.pallas_call(
        functools.partial(_plan_kernel, n_tiles=n_tiles),
        in_specs=[smem],
        out_specs=[smem] * 4,
        out_shape=[i32(n_tiles), i32(n_tiles), i32(n_tiles), i32(32)],
        name="moe_plan",
    )(counts)


def _class_row(packed, rs_ref):
    return rs_ref[lax.shift_right_logical(packed, 16)] + (packed & 0xFFFF)


def _dispatch_kernel(packed_ref, rs_ref, cnt_ref, h_ref, hs_ref, sem, fill_sem, *, tm, n_tiles):
    i = pl.program_id(0)
    used = rs_ref[31]

    def tile_rows(row, n=1):
        return pl.ds(pl.multiple_of(row * TOK_ROWS, TOK_ROWS), n * TOK_ROWS)

    def fill_copy(row, n):
        return pltpu.make_async_copy(h_ref.at[tile_rows(0, n)], hs_ref.at[tile_rows(row, n)], fill_sem)

    def fill(act):
        for c in range(N_CLASSES):
            n = cnt_ref[c]
            row = rs_ref[c] + n
            pad = (-n) & (MOE_TM - 1)
            for bit in reversed(range(MOE_SHIFT)):
                size = 1 << bit

                @pl.when((pad & size) != 0)
                def _(row=row, size=size):
                    act(fill_copy(row, size))
                row = row + (pad & size)
        lax.fori_loop(used, n_tiles, lambda t, _: (act(fill_copy(t * MOE_TM, MOE_TM)), 0)[1], 0)

    @pl.when(i == 0)
    def _():
        fill(lambda cp: cp.start())

    def body(r8, _):
        for j in range(8):
            r = r8 * 8 + j
            row = _class_row(packed_ref[0, i * tm + r], rs_ref)
            pltpu.make_async_copy(h_ref.at[tile_rows(r)], hs_ref.at[tile_rows(row)], sem).start(priority=j % 2)
        return 0

    lax.fori_loop(0, tm // 8, body, 0)
    pltpu.make_async_copy(h_ref, hs_ref.at[tile_rows(0, tm)], sem).wait()

    @pl.when(i == 0)
    def _():
        fill(lambda cp: cp.wait())


def _dispatch(h2_tiles, packed, plan, counts, n_tiles, tm=1024):
    T = packed.shape[1]
    assert tm >= MOE_TM and T % tm == 0
    grid_spec = pltpu.PrefetchScalarGridSpec(
        num_scalar_prefetch=3,
        grid=(T // tm,),
        in_specs=[pl.BlockSpec((tm * TOK_ROWS, LANES), lambda i, *_: (i, 0))],
        out_specs=pl.BlockSpec(memory_space=pl.ANY),
        scratch_shapes=[pltpu.SemaphoreType.DMA(()), pltpu.SemaphoreType.DMA(())],
    )
    return pl.pallas_call(
        functools.partial(_dispatch_kernel, tm=tm, n_tiles=n_tiles),
        grid_spec=grid_spec,
        out_shape=jax.ShapeDtypeStruct((n_tiles * MOE_TM * TOK_ROWS, LANES), F32),
        compiler_params=_cparams(("arbitrary",)),
        name="moe_dispatch",
    )(packed, plan[3], counts, h2_tiles)


def _moe_kernel(ea_ref, eb_ref, fl_ref, h_ref, rwt_ref, wga, wua, wda, wgb, wub, wdb, o_ref,
                sga, sua, sda, sgb, sub, sdb):
    t = pl.program_id(0)
    flags = fl_ref[t]
    valid = (flags & 1) != 0

    @pl.when((flags & 4) != 0)
    def _():
        sga[...] = wga[...].astype(BF16)
        sua[...] = wua[...].astype(BF16)
        sda[...] = wda[...].astype(BF16)

    @pl.when((flags & 8) != 0)
    def _():
        sgb[...] = wgb[...].astype(BF16)
        sub[...] = wub[...].astype(BF16)
        sdb[...] = wdb[...].astype(BF16)

    @pl.when(valid)
    def _():
        h = _load_token_tiles(h_ref, MOE_TM).astype(BF16)
        nt = (((1,), (1,)), ((), ()))
        score = jax.nn.sigmoid(lax.dot_general(h, rwt_ref[...], nt, preferred_element_type=F32))
        lane = lax.broadcasted_iota(jnp.int32, score.shape, 1)
        s_a = jnp.sum(jnp.where(lane == ea_ref[t], score, 0.0), axis=1, keepdims=True)
        s_b = jnp.sum(jnp.where(lane == eb_ref[t], score, 0.0), axis=1, keepdims=True)
        tot = s_a + s_b
        act_a = _silu(jnp.dot(h, sga[...], preferred_element_type=F32)) \
            * jnp.dot(h, sua[...], preferred_element_type=F32) * (s_a / tot)
        act_b = _silu(jnp.dot(h, sgb[...], preferred_element_type=F32)) \
            * jnp.dot(h, sub[...], preferred_element_type=F32) * (s_b / tot)
        _store_token_tiles(o_ref, jnp.dot(act_a.astype(BF16), sda[...], preferred_element_type=F32)
                           + jnp.dot(act_b.astype(BF16), sdb[...], preferred_element_type=F32))

    @pl.when(jnp.logical_not(valid))
    def _():
        o_ref[...] = jnp.zeros_like(o_ref)


def _moe(hs, plan, layer, rwt_bf16, w_gate, w_up, w_down):
    ea, eb, flags, _ = plan
    n_tiles = ea.shape[0]
    wspec = lambda shape, which: pl.BlockSpec(
        (None, None) + shape, (lambda t, ea, eb, fl: (layer, ea[t], 0, 0)) if which == 0
        else (lambda t, ea, eb, fl: (layer, eb[t], 0, 0)))
    up = (D_MODEL, D_FF_EXPERT)
    dn = (D_FF_EXPERT, D_MODEL)
    tile = pl.BlockSpec((MOE_TM * TOK_ROWS, LANES), lambda t, ea, eb, fl: (t, 0))
    grid_spec = pltpu.PrefetchScalarGridSpec(
        num_scalar_prefetch=3,
        grid=(n_tiles,),
        in_specs=[
            tile,
            pl.BlockSpec((N_EXPERTS, D_MODEL), lambda t, ea, eb, fl: (0, 0)),
            wspec(up, 0), wspec(up, 0), wspec(dn, 0), wspec(up, 1), wspec(up, 1), wspec(dn, 1),
        ],
        out_specs=tile,
        scratch_shapes=[pltpu.VMEM(up, BF16), pltpu.VMEM(up, BF16), pltpu.VMEM(dn, BF16),
                        pltpu.VMEM(up, BF16), pltpu.VMEM(up, BF16), pltpu.VMEM(dn, BF16)],
    )
    return pl.pallas_call(
        _moe_kernel,
        grid_spec=grid_spec,
        out_shape=jax.ShapeDtypeStruct((n_tiles * MOE_TM * TOK_ROWS, LANES), F32),
        compiler_params=_cparams(("arbitrary",)),
        name="moe",
    )(ea, eb, flags, hs, rwt_bf16, w_gate, w_up, w_down, w_gate, w_up, w_down)


def _gather_expert_rows(packed_ref, rs_ref, ys_ref, ybuf, sem, tm, n_batches=1):
    i = pl.program_id(0)
    n_i = pl.num_programs(0)
    slot = i % 2
    groups = tm // 8
    assert groups % n_batches == 0
    per_batch = groups // n_batches

    def gather(tile, buf_slot, g0, g1):
        def body(r8, _):
            for j in range(8):
                r = r8 * 8 + j
                d = _class_row(packed_ref[0, tile * tm + r], rs_ref)
                pltpu.make_async_copy(ys_ref.at[pl.ds(pl.multiple_of(d * TOK_ROWS, TOK_ROWS), TOK_ROWS)],
                                      ybuf.at[buf_slot, pl.ds(pl.multiple_of(r * TOK_ROWS, TOK_ROWS), TOK_ROWS)],
                                      sem.at[buf_slot]).start(priority=j % 2)
            return 0
        lax.fori_loop(g0, g1, body, 0)

    def issue_next(k):
        @pl.when(i + 1 < n_i)
        def _():
            gather(jnp.minimum(i + 1, n_i - 1), 1 - slot, k * per_batch, (k + 1) * per_batch)

    @pl.when(i == 0)
    def _():
        gather(0, 0, 0, groups)

    issue_next(0)
    pltpu.make_async_copy(ys_ref.at[pl.ds(0, tm * TOK_ROWS)], ybuf.at[slot], sem.at[slot]).wait()
    return _load_token_tiles(ybuf, tm, slot), issue_next


def _fin_proj_kernel(packed_ref, rs_ref, x_ref, ys_ref, g_ref, lnw_ref, lnb_ref, sh_ref, sc_ref, w_f32_ref,
                     x2_ref, u_ref, ybuf, sem, w_ref, h_ref, *, tm):
    _cast_once(w_f32_ref, w_ref)
    n_sec = u_ref.shape[1] // SEC
    y, issue_next = _gather_expert_rows(packed_ref, rs_ref, ys_ref, ybuf, sem, tm, n_batches=n_sec + 2)
    x2 = _layer_norm(ALPHA * x_ref[...] + g_ref[...] * y, lnw_ref[...], lnb_ref[...])
    x2_ref[...] = x2
    h_ref[...] = (x2 * (1.0 + sc_ref[...]) + sh_ref[...]).astype(BF16)
    issue_next(1)
    for s in range(n_sec):
        u_ref[:, s * SEC:(s + 1) * SEC] = jnp.dot(h_ref[...], w_ref[:, s * SEC:(s + 1) * SEC],
                                                  preferred_element_type=F32).astype(BF16)
        issue_next(2 + s)


def _fin_proj(x1, ys, packed, row_start, mod, layer, ln_w, ln_b, w_next, cond_row, tm=512):
    T = x1.shape[0]
    N = w_next.shape[1]
    row = pl.BlockSpec((tm, D_MODEL), lambda i, *_: (i, 0))
    vec = pl.BlockSpec((1, D_MODEL), lambda i, *_: (0, 0))
    crow = lambda i: cond_row(i, tm)
    grid_spec = pltpu.PrefetchScalarGridSpec(
        num_scalar_prefetch=2,
        grid=(T // tm,),
        in_specs=[row, pl.BlockSpec(memory_space=pl.ANY), _mod_spec(layer, 5, crow), vec, vec,
                  _mod_spec(layer + 1, 0, crow), _mod_spec(layer + 1, 1, crow),
                  pl.BlockSpec((D_MODEL, N), lambda i, *_: (0, 0), pipeline_mode=pl.Buffered(1))],
        out_specs=[row, pl.BlockSpec((tm, N), lambda i, *_: (i, 0))],
        scratch_shapes=[pltpu.VMEM((2, tm * TOK_ROWS, LANES), F32), pltpu.SemaphoreType.DMA((2,)),
                        pltpu.VMEM((D_MODEL, N), BF16), pltpu.VMEM((tm, D_MODEL), BF16)],
    )
    return pl.pallas_call(
        functools.partial(_fin_proj_kernel, tm=tm),
        grid_spec=grid_spec,
        out_shape=[jax.ShapeDtypeStruct((T, D_MODEL), F32), jax.ShapeDtypeStruct((T, N), BF16)],
        compiler_params=_cparams(("arbitrary",)),
        name="post_moe_in_proj",
    )(packed, row_start, x1, ys, mod, ln_w.reshape(1, -1), ln_b.reshape(1, -1), mod, mod, w_next)


def _fin_kernel(packed_ref, rs_ref, x_ref, ys_ref, g_ref, lnw_ref, lnb_ref, *rest, tm, group_tiles):
    o_refs = rest[:len(group_tiles)]
    ybuf, sem = rest[len(group_tiles):]
    i = pl.program_id(0)
    rows, _ = _gather_expert_rows(packed_ref, rs_ref, ys_ref, ybuf, sem, tm)
    y = _layer_norm(ALPHA * x_ref[...] + g_ref[...] * rows, lnw_ref[...], lnb_ref[...])
    first = 0
    for o_ref, nt in zip(o_refs, group_tiles):
        @pl.when(jnp.logical_and(i >= first, i < first + nt))
        def _(o_ref=o_ref):
            o_ref[...] = y
        first += nt


def _fin(x1, ys, packed, row_start, mod, layer, ln_w, ln_b, cond_row, group_rows, tm=256):
    group_tiles = tuple(n // tm for n in group_rows)
    row = pl.BlockSpec((tm, D_MODEL), lambda i, *_: (i, 0))
    vec = pl.BlockSpec((1, D_MODEL), lambda i, *_: (0, 0))
    out_specs, first = [], 0
    for nt in group_tiles:
        out_specs.append(_group_spec(D_MODEL, tm, first, nt))
        first += nt
    grid_spec = pltpu.PrefetchScalarGridSpec(
        num_scalar_prefetch=2,
        grid=(sum(group_tiles),),
        in_specs=[row, pl.BlockSpec(memory_space=pl.ANY), _mod_spec(layer, 5, lambda i: cond_row(i, tm)), vec, vec],
        out_specs=out_specs,
        scratch_shapes=[pltpu.VMEM((2, tm * TOK_ROWS, LANES), F32), pltpu.SemaphoreType.DMA((2,))],
    )
    return pl.pallas_call(
        functools.partial(_fin_kernel, tm=tm, group_tiles=group_tiles),
        grid_spec=grid_spec,
        out_shape=[jax.ShapeDtypeStruct((n, D_MODEL), F32) for n in group_rows],
        compiler_params=_cparams(("arbitrary",)),
        name="post_moe",
    )(packed, row_start, x1, ys, mod, ln_w.reshape(1, -1), ln_b.reshape(1, -1))


def _filt_kernel(z_ref, w1_ref, b1_ref, w2_ref, b2_ref, fr_ref, w3_ref, dl_ref, o_ref):
    i = pl.program_id(0)
    z = z_ref[...]
    fr = fr_ref[...]
    a = jnp.sin(fr * (jnp.dot(z, w1_ref[...], precision=HIGHEST, preferred_element_type=F32) + b1_ref[...]))
    a = jnp.sin(fr * (jnp.dot(a, w2_ref[...], precision=HIGHEST, preferred_element_type=F32) + b2_ref[...]))
    filt = _dot_3pass(a, w3_ref[...])
    window = jnp.exp(-z[:, 0:1] * dl_ref[...]) + HY_SHIFT
    rows = i * z.shape[0] + lax.broadcasted_iota(jnp.int32, filt.shape, 0)
    o_ref[...] = jnp.where(rows == 0, 0.0, filt * window)


def _hyena_taps(seq_len, w1, b1, w2, b2, w3, freq):
    L = seq_len
    t = np.linspace(0.0, 1.0, L)[:, None]
    bands = np.linspace(1e-4, HY_BANDS - 1, HY_BANDS)
    ang = 2.0 * math.pi * bands[None, :] * np.arange(L)[:, None] / L
    z = np.concatenate([t, np.cos(ang), -np.sin(ang)], axis=-1)
    offs = np.minimum(np.abs(np.arange(2 * L) - L), L - 1)
    z2_np = np.zeros((2 * L, LANES), np.float32)
    z2_np[:, :z.shape[1]] = z[offs]
    z2 = jnp.asarray(z2_np)
    pad_c = lambda a: jnp.zeros((a.shape[0], LANES), F32).at[:, :a.shape[1]].set(a)
    pad_r = lambda a: jnp.zeros((LANES, a.shape[1]), F32).at[:a.shape[0], :].set(a)
    w1p = pad_r(pad_c(w1))
    w2p = pad_r(pad_c(w2))
    w3p = pad_r(w3)
    b1p, b2p, frp = pad_c(b1[None, :]), pad_c(b2[None, :]), pad_c(freq[None, :])
    deltas = jnp.asarray(np.abs(np.linspace(math.log(HY_DECAY_TARGET) / HY_SLOW, math.log(HY_DECAY_TARGET) / HY_FAST,
                                            D_MODEL))[None, :].astype(np.float32))
    rb = min(1024, L)
    cbf = D_MODEL
    ncb = D_MODEL // cbf
    nrb_back = L // rb
    sq = pl.BlockSpec((LANES, LANES), lambda i, j: (0, 0))
    vec = pl.BlockSpec((1, LANES), lambda i, j: (0, 0))
    return pl.pallas_call(
        _filt_kernel,
        grid=(2 * L // rb, ncb),
        in_specs=[
            pl.BlockSpec((rb, LANES), lambda i, j: (i, 0)),
            sq, vec, sq, vec, vec,
            pl.BlockSpec((LANES, cbf), lambda i, j: (0, jnp.where(i < nrb_back, ncb + j, j))),
            pl.BlockSpec((1, cbf), lambda i, j: (0, j)),
        ],
        out_specs=pl.BlockSpec((rb, cbf), lambda i, j: (i, j)),
        out_shape=jax.ShapeDtypeStruct((2 * L, D_MODEL), F32),
        compiler_params=_cparams(("arbitrary", "arbitrary")),
        name="hyena_taps",
    )(z2, w1p, b1p, w2p, b2p, frp, w3p, deltas)


def _dft_mats(cb):
    n = 2 * cb
    m = np.arange(cb)
    f = np.arange(cb)
    ang = 2.0 * np.pi * ((f[:, None] * m[None, :]) % n) / n
    fwd = np.concatenate([np.cos(ang), -np.sin(ang)], axis=0)
    fwd[cb, :] = np.where(m % 2 == 0, 1.0, -1.0)
    coef = np.where(f == 0, 1.0, 2.0)[None, :] / n
    inv = np.concatenate([coef * np.cos(ang.T), -coef * np.sin(ang.T)], axis=1)
    inv[:, cb] = np.where(m % 2 == 0, 1.0, -1.0) / n
    return fwd.astype(np.float32), inv.astype(np.float32)


def _hconv_kernel(x0_ref, x1_ref, v_ref, cw0_ref, cw1_ref, cw2_ref, cb0_ref, cb1_ref, cb2_ref,
                  taps_ref, fb_ref, fwd_ref, inv_ref, o_ref,
                  hs_ref, stage_ref, w32_ref, w_ref, x0c_ref, u_ref, y_ref, *, seq_len, blk, n_seq):
    L = seq_len
    nb = L // blk
    cw = x0_ref.shape[1]
    RC = 256
    bi = pl.program_id(1)
    row0 = lax.broadcasted_iota(jnp.int32, (RC, cw), 0) == 0

    @pl.when(bi == 0)
    def _():
        rows = lax.broadcasted_iota(jnp.int32, (2 * blk, 1), 0)
        sign = jnp.where(rows % 2 == 0, 1.0, -1.0)
        real_row = rows <= blk
        prev = None
        for jb in range(2 * nb):
            cur = jnp.dot(fwd_ref[...], taps_ref[jb * blk:(jb + 1) * blk, :].astype(BF16),
                          preferred_element_type=F32)
            if prev is not None:
                first_tap = taps_ref[(jb - 1) * blk:(jb - 1) * blk + 1, :].astype(BF16).astype(F32)
                hs_ref[jb - 1] = cur + sign * (prev - jnp.where(real_row, first_tap, 0.0))
            prev = cur

    zeros8 = jnp.zeros((8, cw), F32)
    for s in range(n_seq):
        r0 = s * L

        def short_conv(k, src_ref, cw_ref, cb_ref, store):
            stage_ref[k, 0:8, :] = zeros8
            stage_ref[k, 8 + L:16 + L, :] = zeros8
            for c in range(L // RC):
                stage_ref[k, 8 + c * RC:8 + (c + 1) * RC, :] = \
                    src_ref[r0 + c * RC:r0 + (c + 1) * RC, :].astype(F32)
            w = cw_ref[...]
            for c in range(L // RC):
                lo = stage_ref[k, 7 + c * RC:7 + (c + 1) * RC, :]
                mid = stage_ref[k, 8 + c * RC:8 + (c + 1) * RC, :]
                hi = stage_ref[k, 9 + c * RC:9 + (c + 1) * RC, :]
                store(c, lo * w[0:1, :] + mid * w[1:2, :] + hi * w[2:3, :] + cb_ref[...])

        def st_x1(c, val):
            w32_ref[c * RC:(c + 1) * RC, :] = val

        def st_v(c, val):
            w_ref[c * RC:(c + 1) * RC, :] = (w32_ref[c * RC:(c + 1) * RC, :] * val).astype(BF16)

        def st_x0(c, val):
            x0c_ref[c * RC:(c + 1) * RC, :] = val.astype(BF16)

        short_conv(0, x1_ref, cw1_ref, cb1_ref, st_x1)
        short_conv(1, v_ref, cw2_ref, cb2_ref, st_v)
        short_conv(2, x0_ref, cw0_ref, cb0_ref, st_x0)

        for j in range(nb):
            u_ref[j] = jnp.dot(fwd_ref[...], w_ref[j * blk:(j + 1) * blk, :], preferred_element_type=F32)

        for i in range(nb):
            for c in range(blk // RC):
                re = None
                im = None
                for j in range(nb):
                    k = i - j + nb - 1
                    a = u_ref[j, c * RC:(c + 1) * RC, :]
                    b = u_ref[j, blk + c * RC:blk + (c + 1) * RC, :]
                    hr = hs_ref[k, c * RC:(c + 1) * RC, :]
                    hi = hs_ref[k, blk + c * RC:blk + (c + 1) * RC, :]
                    bb = b * hi
                    if c == 0:
                        t_re = a * hr - jnp.where(row0, 0.0, bb)
                        t_im = jnp.where(row0, bb, a * hi + b * hr)
                    else:
                        t_re = a * hr - bb
                        t_im = a * hi + b * hr
                    re = t_re if re is None else re + t_re
                    im = t_im if im is None else im + t_im
                y_ref[i, c * RC:(c + 1) * RC, :] = re.astype(BF16)
                y_ref[i, blk + c * RC:blk + (c + 1) * RC, :] = im.astype(BF16)
            conv = jnp.dot(inv_ref[...], y_ref[i], preferred_element_type=F32)
            sl = slice(i * blk, (i + 1) * blk)
            z = x0c_ref[sl, :].astype(F32) * (conv + w_ref[sl, :].astype(F32) * fb_ref[...])
            o_ref[r0 + i * blk:r0 + (i + 1) * blk, :] = z.astype(BF16)


def _hyena_conv(u, conv_w, conv_b, taps, filt_bias, batch, seq_len, blk, n_seq, row_off=0, cw=256):
    T = batch * seq_len
    L = seq_len
    nb = L // blk
    ncw = D_MODEL // cw
    fwd_np, inv_np = _dft_mats(blk)
    fwd = jnp.asarray(fwd_np).astype(BF16)
    inv = jnp.asarray(inv_np).astype(BF16)
    rows = n_seq * L
    assert row_off % rows == 0
    boff = row_off // rows
    sec = lambda s: pl.BlockSpec((rows, cw), lambda c, b: (b + boff, s * ncw + c))
    cws = lambda s: pl.BlockSpec((3, cw), lambda c, b: (0, s * ncw + c))
    cbs = lambda s: pl.BlockSpec((1, cw), lambda c, b: (0, s * ncw + c))
    return pl.pallas_call(
        functools.partial(_hconv_kernel, seq_len=L, blk=blk, n_seq=n_seq),
        grid=(ncw, batch // n_seq),
        in_specs=[sec(0), sec(1), sec(2), cws(0), cws(1), cws(2), cbs(0), cbs(1), cbs(2),
                  pl.BlockSpec((2 * L, cw), lambda c, b: (0, c), pipeline_mode=pl.Buffered(1)),
                  pl.BlockSpec((1, cw), lambda c, b: (0, c)),
                  pl.BlockSpec((2 * blk, blk), lambda c, b: (0, 0), pipeline_mode=pl.Buffered(1)),
                  pl.BlockSpec((blk, 2 * blk), lambda c, b: (0, 0), pipeline_mode=pl.Buffered(1))],
        out_specs=pl.BlockSpec((rows, cw), lambda c, b: (b, c)),
        out_shape=jax.ShapeDtypeStruct((T, D_MODEL), BF16),
        scratch_shapes=[
            pltpu.VMEM((2 * nb - 1, 2 * blk, cw), F32),
            pltpu.VMEM((3, L + 16, cw), F32),
            pltpu.VMEM((L, cw), F32),
            pltpu.VMEM((L, cw), BF16),
            pltpu.VMEM((L, cw), BF16),
            pltpu.VMEM((nb, 2 * blk, cw), F32),
            pltpu.VMEM((nb, 2 * blk, cw), BF16),
        ],
        compiler_params=_cparams(("arbitrary", "arbitrary")),
        name="hyena_conv",
    )(u, u, u, conv_w, conv_w, conv_w, conv_b.reshape(1, -1), conv_b.reshape(1, -1), conv_b.reshape(1, -1),
      taps, filt_bias.reshape(1, -1), fwd, inv)


def kernel(x_prompt, x_sample, cache_diff_k, cache_diff_v, state_ret_fwd, state_ret_bwd, c, c_ctx, ada_w, ada_b, ln_w, ln_b, ev_w_in, ev_w_out, ret_decay_fwd, ret_decay_bwd, ret_gn_w, diff_lambda, diff_subln_w, hy_w_in, hy_conv_w, hy_conv_b, hy_ffn_w1, hy_ffn_b1, hy_ffn_w2, hy_ffn_b2, hy_ffn_w3, hy_freq, hy_filter_bias, hy_w_out, router_w, router_bias, moe_w_gate, moe_w_up, moe_w_down):
    B, S, D = x_prompt.shape
    DB, DS, _ = x_sample.shape
    PAST = cache_diff_k.shape[2]
    TP, TS = B * S, DB * DS
    T_ALL = TP + TS
    assert D == D_MODEL and 1 + DB <= COND_ROWS and T_ALL < 65536

    cond8 = jnp.zeros((COND_ROWS, D), F32).at[0].set(c_ctx).at[1:1 + DB].set(c)
    mod = _ada_mod(cond8, ada_w, ada_b).reshape(DEPTH * COND_ROWS * N_MOD, 1, D)

    tm = 512
    groups = [
        dict(x=x_prompt.reshape(TP, D), x_off=0, batch=B, seq=S, off=0, cond_row=lambda i, tm: 0),
        dict(x=x_sample.reshape(TS, D), x_off=0, batch=DB, seq=DS, off=TP,
             cond_row=lambda i, tm: 1 + (i * tm) // DS),
    ]
    u_next = None

    def cond_row_all(i, tm):
        return jnp.where(i * tm < TP, 0, 1 + (i * tm - TP) // DS)

    rw_pad = jnp.zeros((D, LANES), F32).at[:, :N_EXPERTS].set(router_w.astype(F32))
    rw_hi = rw_pad.astype(BF16)
    rw_lo = (rw_pad - rw_hi.astype(F32)).astype(BF16)
    rwt = router_w.T.astype(BF16)
    rbias = router_bias.astype(F32)
    n_tiles = T_ALL // MOE_TM + N_CLASSES
    outs = {}

    for l in range(DEPTH):
        mixes = []
        if l % 2 == 0:
            e = l // 2
            w_in = ev_w_in[e]
            w_out = ev_w_out[e]
            lg = jnp.stack([jnp.log1p(-jnp.exp2(ret_decay_fwd[e].astype(F32))),
                            jnp.log1p(-jnp.exp2(ret_decay_bwd[e].astype(F32)))])
            lam_init = 0.8 - 0.6 * math.exp(-0.3 * l)
            lq1, lk1, lq2, lk2 = diff_lambda[e].astype(F32)
            lam = (jnp.exp(jnp.sum(lq1 * lk1)) - jnp.exp(jnp.sum(lq2 * lk2)) + lam_init).reshape(1)
            kscale = RET_DK ** -0.5
            for gi, g in enumerate(groups):
                tm_in = tm
                crow = functools.partial(g["cond_row"], tm=tm_in)
                if gi == 0:
                    secs = (("none", 1.0, False), ("none", kscale, False), ("none", 1.0, False),
                            ("none", 1.0, False), ("none", 1.0, False), ("none", 1.0, True),
                            ("none", 1.0, True))
                    proj, kd, vd = _in_proj(g["x"], mod, l, w_in, secs, crow, tm_in, tm=tm_in)
                    outs.setdefault("kd", []).append(kd.reshape(B, 1, S, H_DIFF, 2 * DIFF_QK))
                    outs.setdefault("vd", []).append(vd.reshape(B, 1, S, H_DIFF, DIFF_V))
                    ret, sf, sb = _retention(proj, lg, ret_gn_w[e], g["batch"], g["seq"], write_state=True)
                    outs.setdefault("sf", []).append(sf.reshape(B, 1, H_RET, RET_DK, RET_DV))
                    outs.setdefault("sb", []).append(sb.reshape(B, 1, H_RET, RET_DK, RET_DV))
                    att = _diff_attention(proj, lam, diff_subln_w[e], lam_init, g["batch"], g["seq"],
                                          tq=min(256, g["seq"]))
                else:
                    secs = (("ret", 1.0, False), ("ret", kscale, False), ("none", 1.0, False),
                            ("none", 1.0, False), ("diff", 1.0, False), ("diff", 1.0, False),
                            ("none", 1.0, False))
                    tabs = {"ret": _rope_tables(g["seq"], RET_DK), "diff": _rope_tables(g["seq"], DIFF_QK)}
                    (proj,) = _in_proj(g["x"], mod, l, w_in, secs, crow, g["seq"], rope_tabs=tabs, tm=tm_in)
                    ret, = _retention(proj, lg, ret_gn_w[e], g["batch"], g["seq"],
                                      states=(state_ret_fwd[:, e], state_ret_bwd[:, e]))
                    ctx = (cache_diff_k[:, e].reshape(DB, PAST, H_DIFF * 2 * DIFF_QK),
                           cache_diff_v[:, e].reshape(DB, PAST, H_DIFF * DIFF_V))
                    att = _diff_attention(proj, lam, diff_subln_w[e], lam_init, g["batch"], g["seq"], ctx=ctx,
                                          tq=512)
                mixes.append((ret, att))
        else:
            o = l // 2
            w_in = hy_w_in[o]
            w_out = hy_w_out[o]
            secs = (("none", 1.0, False),) * (3 * D // SEC)
            for gi, g in enumerate(groups):
                if u_next is None:
                    crow = functools.partial(g["cond_row"], tm=tm)
                    (u,) = _in_proj(g["x"], mod, l, w_in, secs, crow, tm, tm=tm)
                    u_off = 0
                else:
                    u, u_off = u_next, g["off"]
                taps = _hyena_taps(g["seq"], hy_ffn_w1[o], hy_ffn_b1[o], hy_ffn_w2[o], hy_ffn_b2[o],
                                   hy_ffn_w3[o], hy_freq[o])
                blk = min(g["seq"], 1024)
                n_seq = max(1, 1024 // g["seq"])
                z = _hyena_conv(u, hy_conv_w[o], hy_conv_b[o], taps, hy_filter_bias[o], g["batch"], g["seq"],
                                blk, n_seq, row_off=u_off)
                mixes.append((z,))

        x1_all, h2_all, info_all, counts = _post_mix(mixes, [(g["x"], g["x_off"]) for g in groups], w_out, mod, l, ln_w[l, 0],
                                                     ln_b[l, 0], rw_hi, rw_lo, rbias, cond_row_all, tm)
        packed = info_all
        counts_i = counts[:, 0].astype(jnp.int32)
        plan = _moe_plan(counts_i, n_tiles)
        hs = _dispatch(h2_all, packed, plan, counts_i, n_tiles)
        ys = _moe(hs, plan, l, rwt, moe_w_gate, moe_w_up, moe_w_down)
        if l + 1 < DEPTH and (l + 1) % 2 == 1:
            x_all, u_next = _fin_proj(x1_all, ys, packed, plan[3], mod, l, ln_w[l, 1], ln_b[l, 1],
                                      hy_w_in[(l + 1) // 2], cond_row_all, tm=tm)
            for g in groups:
                g["x"], g["x_off"] = x_all, g["off"]
        else:
            u_next = None
            xs = _fin(x1_all, ys, packed, plan[3], mod, l, ln_w[l, 1], ln_b[l, 1], cond_row_all, (TP, TS))
            for g, x in zip(groups, xs):
                g["x"], g["x_off"] = x, 0

    y_prompt = groups[0]["x"].reshape(B, S, D)
    y_sample = groups[1]["x"].reshape(DB, DS, D)
    cat = lambda xs: xs[0] if len(xs) == 1 else jnp.concatenate(xs, axis=1)
    return (y_prompt, y_sample, cat(outs["kd"]), cat(outs["vd"]), cat(outs["sf"]), cat(outs["sb"]))
```

```python
import functools
import math

import numpy as np
import jax
import jax.numpy as jnp
from jax import lax
from jax.experimental import pallas as pl
from jax.experimental.pallas import tpu as pltpu

F32 = jnp.float32
BF16 = jnp.bfloat16
HIGHEST = lax.Precision.HIGHEST

D_MODEL = 1024
DEPTH = 2
GRID_W = 64
H_RET = 4
RET_DK = 128
RET_DV = 128
RET_CHUNK = 128
H_DIFF = 4
DIFF_QK = 64
DIFF_V = 128
ROPE_BASE = 10000.0
HY_BANDS = 16
HY_FH = 64
HY_DECAY_TARGET = 1e-2
HY_FAST = 0.3
HY_SLOW = 1.5
HY_SHIFT = 0.05
N_EXPERTS = 16
N_GROUPS = 4
GROUP_SIZE = N_EXPERTS // N_GROUPS
D_FF_EXPERT = 512
ALPHA = (2 * DEPTH) ** 0.25
LN_EPS = 1e-5

LANES = 128
SEC = 512
COND_ROWS = 8
N_MOD = 6
VMEM_LIMIT = 50 * 1024 * 1024

PAIR_SLOTS = ((0, 1), (2, 1), (2, 3), (0, 3), (0, 2), (1, 3))
N_PAIRS = len(PAIR_SLOTS)
N_CLASSES = N_GROUPS * N_PAIRS
MOE_TM = 256
MOE_SHIFT = MOE_TM.bit_length() - 1
POST_SUB = 512
TOK_ROWS = D_MODEL // LANES


def _cparams(sem):
    return pltpu.CompilerParams(dimension_semantics=sem, vmem_limit_bytes=VMEM_LIMIT)


def _silu(x):
    return x * jax.nn.sigmoid(x)


def _store_token_tiles(ref, x):
    n = x.shape[0]
    for j in range(TOK_ROWS):
        ref[pl.ds(j, n, stride=TOK_ROWS), :] = x[:, j * LANES:(j + 1) * LANES]


def _load_token_tiles(ref, n, slot=None):
    idx = () if slot is None else (slot,)
    return jnp.concatenate([ref[idx + (pl.ds(j, n, stride=TOK_ROWS), slice(None))] for j in range(TOK_ROWS)],
                           axis=1)


def _layer_norm(x, w, b):
    mu = jnp.mean(x, axis=-1, keepdims=True)
    xc = x - mu
    var = jnp.mean(xc * xc, axis=-1, keepdims=True)
    return xc * lax.rsqrt(var + LN_EPS) * w + b


def _dot_3pass(a, w):
    a_hi = a.astype(BF16)
    a_lo = (a - a_hi.astype(F32)).astype(BF16)
    w_hi = w.astype(BF16)
    w_lo = (w - w_hi.astype(F32)).astype(BF16)
    return (jnp.dot(a_hi, w_hi, preferred_element_type=F32) + jnp.dot(a_lo, w_hi, preferred_element_type=F32)
            + jnp.dot(a_hi, w_lo, preferred_element_type=F32))


def _ada_kernel(c_ref, w_ref, b_ref, o_ref):
    o_ref[...] = _dot_3pass(_silu(c_ref[...]), w_ref[...]) + b_ref[...]


def _ada_mod(cond8, ada_w, ada_b):
    tn = 1024
    nj = ada_w.shape[2] // tn
    return pl.pallas_call(
        _ada_kernel,
        grid=(DEPTH, nj),
        in_specs=[
            pl.BlockSpec((COND_ROWS, D_MODEL), lambda l, j: (0, 0)),
            pl.BlockSpec((None, D_MODEL, tn), lambda l, j: (l, 0, j)),
            pl.BlockSpec((None, 1, tn), lambda l, j: (l, 0, j)),
        ],
        out_specs=pl.BlockSpec((None, COND_ROWS, tn), lambda l, j: (l, 0, j)),
        out_shape=jax.ShapeDtypeStruct((DEPTH, COND_ROWS, ada_w.shape[2]), F32),
        compiler_params=_cparams(("arbitrary", "arbitrary")),
        name="ada_mod",
    )(cond8, ada_w, ada_b.reshape(DEPTH, 1, -1))


def _mod_spec(layer, chunk, row_of_tile):
    def imap(i, *_):
        return ((layer * COND_ROWS + row_of_tile(i)) * N_MOD + chunk, 0, 0)
    return pl.BlockSpec((None, 1, D_MODEL), imap)


def _rope(a, tabs, quarter):
    c, sa, sb = tabs
    out = []
    for hb in range(a.shape[1] // LANES):
        blk = a[:, hb * LANES:(hb + 1) * LANES]
        up = pltpu.roll(blk, LANES - quarter, axis=1)
        dn = pltpu.roll(blk, quarter, axis=1)
        out.append(blk * c + up * sa + dn * sb)
    return jnp.concatenate(out, axis=1)


def _cast_once(w_ref, w_bf16_ref):
    @pl.when(pl.program_id(0) == 0)
    def _():
        for c in range(0, w_ref.shape[1], SEC):
            w_bf16_ref[:, c:c + SEC] = w_ref[:, c:c + SEC].astype(BF16)


def _in_kernel(*refs, secs, n_f32_out):
    x_ref, sh_ref, sc_ref, w_ref = refs[:4]
    pos = 4
    tabs = {}
    for kind in ("ret", "diff"):
        if any(s[0] == kind for s in secs):
            tabs[kind] = tuple(r[...] for r in refs[pos:pos + 3])
            pos += 3
    o_ref = refs[pos]
    f32_refs = refs[pos + 1:pos + 1 + n_f32_out]
    w_bf16_ref = refs[-1]
    _cast_once(w_ref, w_bf16_ref)
    h = (x_ref[...] * (1.0 + sc_ref[...]) + sh_ref[...]).astype(BF16)
    k32 = 0
    for s, (kind, scale, want_f32) in enumerate(secs):
        acc = jnp.dot(h, w_bf16_ref[:, s * SEC:(s + 1) * SEC], preferred_element_type=F32)
        if scale != 1.0:
            acc = acc * scale
        if kind == "ret":
            acc = _rope(acc, tabs["ret"], RET_DK // 4)
        elif kind == "diff":
            acc = _rope(acc, tabs["diff"], DIFF_QK // 4)
        o_ref[:, s * SEC:(s + 1) * SEC] = acc.astype(BF16)
        if want_f32:
            for hb in range(SEC // LANES):
                f32_refs[k32][:, hb, :] = acc[:, hb * LANES:(hb + 1) * LANES]
            k32 += 1
    assert k32 == n_f32_out


def _in_proj(x2d, mod, layer, w_f32, secs, row_of_tile, seq_len, rope_tabs=None, tm=512):
    T = x2d.shape[0]
    N = w_f32.shape[1]
    assert N == SEC * len(secs) and T % tm == 0 and seq_len % tm == 0
    tiles_per_seq = seq_len // tm
    in_specs = [
        pl.BlockSpec((tm, D_MODEL), lambda i: (i, 0)),
        _mod_spec(layer, 0, row_of_tile),
        _mod_spec(layer, 1, row_of_tile),
        pl.BlockSpec((D_MODEL, N), lambda i: (0, 0), pipeline_mode=pl.Buffered(1)),
    ]
    args = [x2d, mod, mod, w_f32]
    for kind in ("ret", "diff"):
        if any(s[0] == kind for s in secs):
            for t in rope_tabs[kind]:
                in_specs.append(pl.BlockSpec((tm, LANES), lambda i: (i % tiles_per_seq, 0)))
                args.append(t)
    n_f32 = sum(1 for s in secs if s[2])
    heads = SEC // LANES
    out_shape = [jax.ShapeDtypeStruct((T, N), BF16)] + [jax.ShapeDtypeStruct((T, heads, LANES), F32)] * n_f32
    out_specs = ([pl.BlockSpec((tm, N), lambda i: (i, 0))]
                 + [pl.BlockSpec((tm, heads, LANES), lambda i: (i, 0, 0))] * n_f32)
    return pl.pallas_call(
        functools.partial(_in_kernel, secs=secs, n_f32_out=n_f32),
        grid=(T // tm,),
        in_specs=in_specs,
        out_specs=out_specs,
        out_shape=out_shape,
        scratch_shapes=[pltpu.VMEM((D_MODEL, N), BF16)],
        compiler_params=_cparams(("arbitrary",)),
        name="in_proj",
    )(*args)


def _rope_tables(seq_len, d):
    half = d // 2
    quarter = half // 2
    t = np.arange(seq_len)
    inv = ROPE_BASE ** (-np.arange(quarter, dtype=np.float64) / quarter)
    ang_r = (t // GRID_W)[:, None] * inv[None, :]
    ang_c = (t % GRID_W)[:, None] * inv[None, :]
    zero = np.zeros_like(ang_r)
    cos = np.concatenate([np.cos(ang_r)] * 2 + [np.cos(ang_c)] * 2, axis=1)
    sa = np.concatenate([-np.sin(ang_r), zero, -np.sin(ang_c), zero], axis=1)
    sb = np.concatenate([zero, np.sin(ang_r), zero, np.sin(ang_c)], axis=1)
    reps = LANES // d
    return tuple(jnp.asarray(np.tile(a, (1, reps)).astype(np.float32)) for a in (cos, sa, sb))


def _ret_kernel(*refs, n_chunks, has_state, write_state, heads):
    lg_ref, q_ref, k_ref, v_ref, g_ref, gnw_ref = refs[:6]
    pos = 6
    if has_state:
        s0f_ref, s0b_ref = refs[pos:pos + 2]
        pos += 2
    o_ref = refs[pos]
    pos += 1
    if write_state:
        sf_ref, sb_ref = refs[pos:pos + 2]
        pos += 2
    acc_ref, kv_ref = refs[pos:pos + 2]
    for hh in range(heads):
        _ret_head(hh, heads, lg_ref, q_ref, k_ref, v_ref, g_ref, gnw_ref,
                  (s0f_ref, s0b_ref) if has_state else None, o_ref, (sf_ref, sb_ref) if write_state else None,
                  acc_ref, kv_ref, n_chunks)


def _ret_head(hh, heads, lg_ref, q_ref, k_ref, v_ref, g_ref, gnw_ref, s0_refs, o_ref, s_out_refs, acc_ref, kv_ref,
              n_chunks):
    C = RET_CHUNK
    cols = slice(hh * LANES, (hh + 1) * LANES)
    hd = pl.program_id(1) * heads + hh
    lgf = lg_ref[0, hd]
    lgb = lg_ref[1, hd]
    ii = lax.broadcasted_iota(jnp.int32, (C, C), 0).astype(F32)
    jj = lax.broadcasted_iota(jnp.int32, (C, C), 1).astype(F32)
    rel = ii - jj
    d_f = jnp.where(rel >= 0, jnp.exp(jnp.maximum(rel, 0.0) * lgf), 0.0)
    d_b = jnp.where(rel <= 0, jnp.exp(jnp.maximum(-rel, 0.0) * lgb), 0.0)
    d_sum = d_f + d_b
    idx = lax.broadcasted_iota(jnp.int32, (C, 1), 0).astype(F32)
    xi_f = jnp.exp((idx + 1.0) * lgf)
    zeta_f = jnp.exp((C - 1.0 - idx) * lgf)
    xi_b = jnp.exp((C - idx) * lgb)
    zeta_b = jnp.exp(idx * lgb)
    one = jnp.ones((1, 1), F32)
    gc_f = jnp.exp(one * (C * lgf))
    gc_b = jnp.exp(one * (C * lgb))

    nt = (((1,), (1,)), ((), ()))
    tn = (((0,), (0,)), ((), ()))

    if s0_refs is not None:
        s_f = s0_refs[0][hh]
        s_b = s0_refs[1][hh]
    else:
        s_f = jnp.zeros((RET_DK, RET_DV), F32)
        s_b = jnp.zeros((RET_DK, RET_DV), F32)

    for n in range(n_chunks):
        sl = slice(n * C, (n + 1) * C)
        qc, kc, vc = q_ref[sl, cols], k_ref[sl, cols], v_ref[sl, cols]
        scores = lax.dot_general(qc, kc, nt, preferred_element_type=F32) * d_sum
        acc_ref[sl, cols] = jnp.dot(scores.astype(BF16), vc, preferred_element_type=F32)
        kf = kc.astype(F32)
        kv_ref[0, hh, n] = lax.dot_general((kf * zeta_f).astype(BF16), vc, tn, preferred_element_type=F32)
        kv_ref[1, hh, n] = lax.dot_general((kf * zeta_b).astype(BF16), vc, tn, preferred_element_type=F32)

    for n in range(n_chunks):
        sl = slice(n * C, (n + 1) * C)
        qf = q_ref[sl, cols].astype(F32)
        acc_ref[sl, cols] += jnp.dot((qf * xi_f).astype(BF16), s_f.astype(BF16), preferred_element_type=F32)
        s_f = gc_f * s_f + kv_ref[0, hh, n]

    gnw = gnw_ref[:, cols]
    for n in reversed(range(n_chunks)):
        sl = slice(n * C, (n + 1) * C)
        qf = q_ref[sl, cols].astype(F32)
        cross = jnp.dot((qf * xi_b).astype(BF16), s_b.astype(BF16), preferred_element_type=F32)
        r = acc_ref[sl, cols] + cross
        mu = jnp.mean(r, axis=-1, keepdims=True)
        rc = r - mu
        var = jnp.mean(rc * rc, axis=-1, keepdims=True)
        rn = rc * lax.rsqrt(var + LN_EPS) * gnw
        o_ref[sl, cols] = (_silu(g_ref[sl, cols].astype(F32)) * rn).astype(BF16)
        s_b = gc_b * s_b + kv_ref[1, hh, n]

    if s_out_refs is not None:
        s_out_refs[0][hh] = s_f
        s_out_refs[1][hh] = s_b


def _retention(proj, lg, gn_w, batch, seq_len, states=None, write_state=False, heads=2):
    T = proj.shape[0]
    hp = heads
    width = hp * LANES
    per_sec = SEC // width
    blk = lambda sec: pl.BlockSpec((seq_len, width), lambda b, h: (b, sec * per_sec + h))
    in_specs = [pl.BlockSpec(memory_space=pltpu.SMEM), blk(0), blk(1), blk(2), blk(3),
                pl.BlockSpec((1, width), lambda b, h: (0, h))]
    args = [lg, proj, proj, proj, proj, gn_w.reshape(1, -1)]
    if states is not None:
        st = pl.BlockSpec((None, hp, RET_DK, RET_DV), lambda b, h: (b, h, 0, 0))
        in_specs += [st, st]
        args += list(states)
    out_shape = [jax.ShapeDtypeStruct((T, SEC), BF16)]
    out_specs = [pl.BlockSpec((seq_len, width), lambda b, h: (b, h))]
    if write_state:
        st_o = pl.BlockSpec((None, hp, RET_DK, RET_DV), lambda b, h: (b, h, 0, 0))
        out_shape += [jax.ShapeDtypeStruct((batch, H_RET, RET_DK, RET_DV), F32)] * 2
        out_specs += [st_o, st_o]
    return pl.pallas_call(
        functools.partial(_ret_kernel, n_chunks=seq_len // RET_CHUNK, has_state=states is not None,
                          write_state=write_state, heads=hp),
        grid=(batch, H_RET // hp),
        in_specs=in_specs,
        out_specs=out_specs,
        out_shape=out_shape,
        scratch_shapes=[pltpu.VMEM((seq_len, width), F32),
                        pltpu.VMEM((2, hp, seq_len // RET_CHUNK, RET_DK, RET_DV), F32)],
        compiler_params=_cparams(("arbitrary", "arbitrary")),
        name="retention",
    )(*args)


def _att_kernel(*refs, has_ctx, out_scale, key_chunk, heads):
    lam_ref, q_ref, k_ref, v_ref = refs[:4]
    pos = 4
    if has_ctx:
        ck_ref, cv_ref = refs[pos:pos + 2]
        pos += 2
    w_ref, o_ref = refs[pos:pos + 2]
    lam = lam_ref[0]
    tq = q_ref.shape[0]
    nt = (((1,), (1,)), ((), ()))
    for hh in range(heads):
        cols = slice(hh * LANES, (hh + 1) * LANES)
        q = q_ref[:, cols]
        lane = lax.broadcasted_iota(jnp.int32, q.shape, 1)
        zero = jnp.zeros_like(q)
        qq = jnp.concatenate([jnp.where(lane < DIFF_QK, q, zero), jnp.where(lane >= DIFF_QK, q, zero)], axis=0)
        qq = qq * jnp.asarray(DIFF_QK ** -0.5, BF16)
        chunks = [(k_ref, v_ref, c * key_chunk, key_chunk) for c in range(k_ref.shape[0] // key_chunk)]
        if has_ctx:
            chunks.append((ck_ref, cv_ref, 0, ck_ref.shape[0]))
        m = l = acc = None
        for kr, vr, off, n in chunks:
            kch = kr[off:off + n, cols].astype(BF16)
            vch = vr[off:off + n, cols].astype(BF16)
            s = lax.dot_general(qq, kch, nt, preferred_element_type=F32)
            cm = jnp.max(s, axis=-1, keepdims=True)
            m_new = cm if m is None else jnp.maximum(m, cm)
            p = jnp.exp(s - m_new)
            ps = jnp.sum(p, axis=-1, keepdims=True)
            pv = jnp.dot(p.astype(BF16), vch, preferred_element_type=F32)
            if m is None:
                l, acc = ps, pv
            else:
                alpha = jnp.exp(m - m_new)
                l = alpha * l + ps
                acc = alpha * acc + pv
            m = m_new
        o = acc / l
        att = o[:tq] - lam * o[tq:]
        att = att * lax.rsqrt(jnp.mean(att * att, axis=-1, keepdims=True) + LN_EPS)
        o_ref[:, cols] = (att * w_ref[...] * out_scale).astype(BF16)


def _diff_attention(proj, lam, subln_w, lam_init, batch, seq_len, ctx=None, tq=256, heads=1):
    T = proj.shape[0]
    width = heads * LANES
    per_sec = SEC // width
    nq = seq_len // tq
    in_specs = [
        pl.BlockSpec(memory_space=pltpu.SMEM),
        pl.BlockSpec((tq, width), lambda b, h, i: (b * nq + i, 4 * per_sec + h)),
        pl.BlockSpec((seq_len, width), lambda b, h, i: (b, 5 * per_sec + h)),
        pl.BlockSpec((seq_len, width), lambda b, h, i: (b, 6 * per_sec + h)),
    ]
    args = [lam, proj, proj, proj]
    if ctx is not None:
        ck, cv = ctx
        past = ck.shape[1]
        cspec = pl.BlockSpec((None, past, width), lambda b, h, i: (b, 0, h))
        in_specs += [cspec, cspec]
        args += [ck, cv]
    in_specs.append(pl.BlockSpec((1, LANES), lambda b, h, i: (0, 0)))
    args.append(subln_w.reshape(1, -1))
    return pl.pallas_call(
        functools.partial(_att_kernel, has_ctx=ctx is not None, out_scale=1.0 - lam_init,
                          key_chunk=min(512, seq_len), heads=heads),
        grid=(batch, H_DIFF // heads, nq),
        in_specs=in_specs,
        out_specs=pl.BlockSpec((tq, width), lambda b, h, i: (b * nq + i, h)),
        out_shape=jax.ShapeDtypeStruct((T, SEC), BF16),
        compiler_params=_cparams(("arbitrary", "arbitrary", "arbitrary")),
        name="diff_attention",
    )(*args)


def _route_class(lt, rb_ref):
    sel = [jax.nn.sigmoid(lt[e:e + 1, :]) + rb_ref[e] for e in range(N_EXPERTS)]
    gscore = []
    for g in range(N_GROUPS):
        mem = sel[g * GROUP_SIZE:(g + 1) * GROUP_SIZE]
        best = None
        for a in range(GROUP_SIZE):
            for b in range(a + 1, GROUP_SIZE):
                pair = mem[a] + mem[b]
                best = pair if best is None else jnp.maximum(best, pair)
        gscore.append(best)
    gbest = gscore[0]
    gidx = jnp.zeros_like(gbest)
    for g in range(1, N_GROUPS):
        upd = gscore[g] > gbest
        gidx = jnp.where(upd, float(g), gidx)
        gbest = jnp.where(upd, gscore[g], gbest)
    msel = []
    for j in range(GROUP_SIZE):
        out = sel[j]
        for g in range(1, N_GROUPS):
            out = jnp.where(gidx == float(g), sel[g * GROUP_SIZE + j], out)
        msel.append(out)
    one = jnp.ones_like(gbest)
    zero = jnp.zeros_like(gbest)
    chosen = []
    for j in range(GROUP_SIZE):
        rank = zero
        for k in range(GROUP_SIZE):
            if k < j:
                rank = rank + jnp.where(msel[k] >= msel[j], one, zero)
            elif k > j:
                rank = rank + jnp.where(msel[k] > msel[j], one, zero)
        chosen.append(jnp.where(rank < 2.0, one, zero))
    c0, c1, c2, c3 = chosen
    order = jnp.where(c0 * c1 > 0, 0.0, jnp.where(c1 * c2 > 0, 1.0, jnp.where(c2 * c3 > 0, 2.0,
            jnp.where(c0 * c3 > 0, 3.0, jnp.where(c0 * c2 > 0, 4.0, 5.0)))))
    return gidx * float(N_PAIRS) + order


def _post_kernel(*refs, n_mix, group_tiles):
    n_groups = len(group_tiles)
    per_group = n_mix + 1
    group_refs = [refs[g * per_group:(g + 1) * per_group] for g in range(n_groups)]
    refs = refs[n_groups * per_group:]
    (w_f32_ref, g_ref, sh_ref, sc_ref, lnw_ref, lnb_ref, rwh_ref, rwl_ref, rb_ref, tri_ref,
     x1_ref, h2_ref, info_ref, cout_ref, cnt_ref, pre_ref, w_ref) = refs
    _cast_once(w_f32_ref, w_ref)
    i = pl.program_id(0)

    @pl.when(i == 0)
    def _():
        cnt_ref[...] = jnp.zeros_like(cnt_ref)

    first = 0
    for g in range(n_groups):
        @pl.when(jnp.logical_and(i >= first, i < first + group_tiles[g]))
        def _(g=g):
            out = None
            off = 0
            for m_ref in group_refs[g][:n_mix]:
                width = m_ref.shape[1]
                part = jnp.dot(m_ref[...], w_ref[off:off + width, :], preferred_element_type=F32)
                out = part if out is None else out + part
                off += width
            pre_ref[...] = ALPHA * group_refs[g][n_mix][...] + g_ref[...] * out
        first += group_tiles[g]

    tm = pre_ref.shape[0]
    ts = tri_ref.shape[0]
    base = cnt_ref[:, 0:1]
    for part in range(tm // ts):
        rows = slice(part * ts, (part + 1) * ts)
        x1 = _layer_norm(pre_ref[rows, :], lnw_ref[...], lnb_ref[...])
        x1_ref[rows, :] = x1
        h2 = x1 * (1.0 + sc_ref[...]) + sh_ref[...]
        for j in range(TOK_ROWS):
            h2_ref[pl.ds(part * ts * TOK_ROWS + j, ts, stride=TOK_ROWS), :] = h2[:, j * LANES:(j + 1) * LANES]
        h_hi = h2.astype(BF16)
        h_lo = (h2 - h_hi.astype(F32)).astype(BF16)
        both = jnp.dot(h_hi, rwl_ref[...], preferred_element_type=F32)
        logits = (both[:, :LANES] + both[:, LANES:]
                  + jnp.dot(h_lo, rwh_ref[...], preferred_element_type=F32))
        cls = _route_class(logits.T, rb_ref)
        crow = lax.broadcasted_iota(jnp.int32, (32, ts), 0).astype(F32)
        onehot = jnp.where(crow == cls, 1.0, 0.0)
        prefix = jnp.dot(onehot.astype(BF16), tri_ref[...], preferred_element_type=F32)
        rank = jnp.sum(onehot * (prefix - 1.0 + base), axis=0, keepdims=True)
        base = base + jnp.sum(onehot, axis=1, keepdims=True)
        packed = cls.astype(jnp.int32) * 65536 + rank.astype(jnp.int32)
        info_ref[:, rows] = packed
    cnt_ref[...] = jnp.broadcast_to(base, cnt_ref.shape)

    @pl.when(i == pl.num_programs(0) - 1)
    def _():
        cout_ref[...] = cnt_ref[...]


def _tri(tm):
    return jnp.asarray(np.triu(np.ones((tm, tm), np.float32))).astype(BF16)


def _group_spec(width, tm, first_tile, n_tiles, array_tile_off=0):
    return pl.BlockSpec((tm, width),
                        lambda i, *_: (jnp.clip(i - first_tile, 0, n_tiles - 1) + array_tile_off, 0))


def _post_mix(group_mixes, group_x, w_out_bf16, mod, layer, ln_w, ln_b, rw_hi, rw_lo, router_bias, cond_row, tm):
    group_tiles = tuple(m[0].shape[0] // tm for m in group_mixes)
    n_tiles = sum(group_tiles)
    total_rows = n_tiles * tm
    row = pl.BlockSpec((tm, D_MODEL), lambda i: (i, 0))
    vec = pl.BlockSpec((1, D_MODEL), lambda i: (0, 0))
    cnt = pl.BlockSpec((32, LANES), lambda i: (0, 0))
    in_specs, args = [], []
    first = 0
    for mixes, (x, x_row_off), nt in zip(group_mixes, group_x, group_tiles):
        for m in mixes:
            in_specs.append(_group_spec(m.shape[1], tm, first, nt))
            args.append(m)
        in_specs.append(_group_spec(D_MODEL, tm, first, nt, x_row_off // tm))
        args.append(x)
        first += nt
    in_specs += [
        pl.BlockSpec((D_MODEL, D_MODEL), lambda i: (0, 0), pipeline_mode=pl.Buffered(1)),
        _mod_spec(layer, 2, lambda i: cond_row(i, tm)), _mod_spec(layer, 3, lambda i: cond_row(i, tm)),
        _mod_spec(layer, 4, lambda i: cond_row(i, tm)),
        vec, vec,
        pl.BlockSpec((D_MODEL, LANES), lambda i: (0, 0)),
        pl.BlockSpec((D_MODEL, 2 * LANES), lambda i: (0, 0)),
        pl.BlockSpec(memory_space=pltpu.SMEM),
        pl.BlockSpec((POST_SUB, POST_SUB), lambda i: (0, 0)),
    ]
    args += [w_out_bf16, mod, mod, mod, ln_w.reshape(1, -1), ln_b.reshape(1, -1), rw_hi, rw_lo, router_bias,
             _tri(POST_SUB)]
    return pl.pallas_call(
        functools.partial(_post_kernel, n_mix=len(group_mixes[0]), group_tiles=group_tiles),
        grid=(n_tiles,),
        in_specs=in_specs,
        out_specs=[row, pl.BlockSpec((tm * TOK_ROWS, LANES), lambda i: (i, 0)),
                   pl.BlockSpec((1, tm), lambda i: (0, i)), cnt],
        out_shape=[jax.ShapeDtypeStruct((total_rows, D_MODEL), F32),
                   jax.ShapeDtypeStruct((total_rows * TOK_ROWS, LANES), F32),
                   jax.ShapeDtypeStruct((1, total_rows), jnp.int32), jax.ShapeDtypeStruct((32, LANES), F32)],
        scratch_shapes=[pltpu.VMEM((32, LANES), F32), pltpu.VMEM((tm, D_MODEL), F32),
                        pltpu.VMEM((D_MODEL, D_MODEL), BF16)],
        compiler_params=_cparams(("arbitrary",)),
        name="post_mix",
    )(*args)


def _plan_kernel(cnt_ref, ea_ref, eb_ref, fl_ref, rs_ref, *, n_tiles):
    start = jnp.int32(0)
    prev_a = jnp.int32(-1)
    prev_b = jnp.int32(-1)
    for c in range(N_CLASSES):
        n = cnt_ref[c]
        tiles = lax.shift_right_logical(n + (MOE_TM - 1), MOE_SHIFT)
        row0 = start * MOE_TM
        rs_ref[c] = row0
        g, pr = divmod(c, N_PAIRS)
        a = g * GROUP_SIZE + PAIR_SLOTS[pr][0]
        b = g * GROUP_SIZE + PAIR_SLOTS[pr][1]
        first = 1 + 4 * (prev_a != a).astype(jnp.int32) + 8 * (prev_b != b).astype(jnp.int32)

        def tile_body(k, _, start=start, a=a, b=b, first=first):
            t = start + k
            ea_ref[t] = a
            eb_ref[t] = b
            fl_ref[t] = jnp.where(k == 0, first, 1)
            return 0

        lax.fori_loop(0, tiles, tile_body, 0)
        has = tiles > 0
        prev_a = jnp.where(has, a, prev_a)
        prev_b = jnp.where(has, b, prev_b)
        start = start + tiles
    for c in range(N_CLASSES, 31):
        rs_ref[c] = 0
    rs_ref[31] = start

    def idle_body(t, _):
        ea_ref[t] = prev_a
        eb_ref[t] = prev_b
        fl_ref[t] = 0
        return 0

    lax.fori_loop(start, n_tiles, idle_body, 0)


def _moe_plan(counts, n_tiles):
    smem = pl.BlockSpec(memory_space=pltpu.SMEM)
    i32 = lambda n: jax.ShapeDtypeStruct((n,), jnp.int32)
    return pl.pallas_call(
        functools.partial(_plan_kernel, n_tiles=n_tiles),
        in_specs=[smem],
        out_specs=[smem] * 4,
        out_shape=[i32(n_tiles), i32(n_tiles), i32(n_tiles), i32(32)],
        name="moe_plan",
    )(counts)


def _class_row(packed, rs_ref):
    return rs_ref[lax.shift_right_logical(packed, 16)] + (packed & 0xFFFF)


def _dispatch_kernel(packed_ref, rs_ref, cnt_ref, h_ref, hs_ref, sem, fill_sem, *, tm, n_tiles):
    i = pl.program_id(0)
    used = rs_ref[31]

    def tile_rows(row, n=1):
        return pl.ds(pl.multiple_of(row * TOK_ROWS, TOK_ROWS), n * TOK_ROWS)

    def fill_copy(row, n):
        return pltpu.make_async_copy(h_ref.at[tile_rows(0, n)], hs_ref.at[tile_rows(row, n)], fill_sem)

    def fill(act):
        for c in range(N_CLASSES):
            n = cnt_ref[c]
            row = rs_ref[c] + n
            pad = (-n) & (MOE_TM - 1)
            for bit in reversed(range(MOE_SHIFT)):
                size = 1 << bit

                @pl.when((pad & size) != 0)
                def _(row=row, size=size):
                    act(fill_copy(row, size))
                row = row + (pad & size)
        lax.fori_loop(used, n_tiles, lambda t, _: (act(fill_copy(t * MOE_TM, MOE_TM)), 0)[1], 0)

    @pl.when(i == 0)
    def _():
        fill(lambda cp: cp.start())

    def body(r8, _):
        for j in range(8):
            r = r8 * 8 + j
            row = _class_row(packed_ref[0, i * tm + r], rs_ref)
            pltpu.make_async_copy(h_ref.at[tile_rows(r)], hs_ref.at[tile_rows(row)], sem).start(priority=j % 2)
        return 0

    lax.fori_loop(0, tm // 8, body, 0)
    pltpu.make_async_copy(h_ref, hs_ref.at[tile_rows(0, tm)], sem).wait()

    @pl.when(i == 0)
    def _():
        fill(lambda cp: cp.wait())


def _dispatch(h2_tiles, packed, plan, counts, n_tiles, tm=1024):
    T = packed.shape[1]
    assert tm >= MOE_TM and T % tm == 0
    grid_spec = pltpu.PrefetchScalarGridSpec(
        num_scalar_prefetch=3,
        grid=(T // tm,),
        in_specs=[pl.BlockSpec((tm * TOK_ROWS, LANES), lambda i, *_: (i, 0))],
        out_specs=pl.BlockSpec(memory_space=pl.ANY),
        scratch_shapes=[pltpu.SemaphoreType.DMA(()), pltpu.SemaphoreType.DMA(())],
    )
    return pl.pallas_call(
        functools.partial(_dispatch_kernel, tm=tm, n_tiles=n_tiles),
        grid_spec=grid_spec,
        out_shape=jax.ShapeDtypeStruct((n_tiles * MOE_TM * TOK_ROWS, LANES), F32),
        compiler_params=_cparams(("arbitrary",)),
        name="moe_dispatch",
    )(packed, plan[3], counts, h2_tiles)


def _moe_kernel(ea_ref, eb_ref, fl_ref, h_ref, rwt_ref, wga, wua, wda, wgb, wub, wdb, o_ref,
                sga, sua, sda, sgb, sub, sdb):
    t = pl.program_id(0)
    flags = fl_ref[t]
    valid = (flags & 1) != 0

    @pl.when((flags & 4) != 0)
    def _():
        sga[...] = wga[...].astype(BF16)
        sua[...] = wua[...].astype(BF16)
        sda[...] = wda[...].astype(BF16)

    @pl.when((flags & 8) != 0)
    def _():
        sgb[...] = wgb[...].astype(BF16)
        sub[...] = wub[...].astype(BF16)
        sdb[...] = wdb[...].astype(BF16)

    @pl.when(valid)
    def _():
        h = _load_token_tiles(h_ref, MOE_TM).astype(BF16)
        nt = (((1,), (1,)), ((), ()))
        score = jax.nn.sigmoid(lax.dot_general(h, rwt_ref[...], nt, preferred_element_type=F32))
        lane = lax.broadcasted_iota(jnp.int32, score.shape, 1)
        s_a = jnp.sum(jnp.where(lane == ea_ref[t], score, 0.0), axis=1, keepdims=True)
        s_b = jnp.sum(jnp.where(lane == eb_ref[t], score, 0.0), axis=1, keepdims=True)
        tot = s_a + s_b
        act_a = _silu(jnp.dot(h, sga[...], preferred_element_type=F32)) \
            * jnp.dot(h, sua[...], preferred_element_type=F32) * (s_a / tot)
        act_b = _silu(jnp.dot(h, sgb[...], preferred_element_type=F32)) \
            * jnp.dot(h, sub[...], preferred_element_type=F32) * (s_b / tot)
        _store_token_tiles(o_ref, jnp.dot(act_a.astype(BF16), sda[...], preferred_element_type=F32)
                           + jnp.dot(act_b.astype(BF16), sdb[...], preferred_element_type=F32))

    @pl.when(jnp.logical_not(valid))
    def _():
        o_ref[...] = jnp.zeros_like(o_ref)


def _moe(hs, plan, layer, rwt_bf16, w_gate, w_up, w_down):
    ea, eb, flags, _ = plan
    n_tiles = ea.shape[0]
    wspec = lambda shape, which: pl.BlockSpec(
        (None, None) + shape, (lambda t, ea, eb, fl: (layer, ea[t], 0, 0)) if which == 0
        else (lambda t, ea, eb, fl: (layer, eb[t], 0, 0)))
    up = (D_MODEL, D_FF_EXPERT)
    dn = (D_FF_EXPERT, D_MODEL)
    tile = pl.BlockSpec((MOE_TM * TOK_ROWS, LANES), lambda t, ea, eb, fl: (t, 0))
    grid_spec = pltpu.PrefetchScalarGridSpec(
        num_scalar_prefetch=3,
        grid=(n_tiles,),
        in_specs=[
            tile,
            pl.BlockSpec((N_EXPERTS, D_MODEL), lambda t, ea, eb, fl: (0, 0)),
            wspec(up, 0), wspec(up, 0), wspec(dn, 0), wspec(up, 1), wspec(up, 1), wspec(dn, 1),
        ],
        out_specs=tile,
        scratch_shapes=[pltpu.VMEM(up, BF16), pltpu.VMEM(up, BF16), pltpu.VMEM(dn, BF16),
                        pltpu.VMEM(up, BF16), pltpu.VMEM(up, BF16), pltpu.VMEM(dn, BF16)],
    )
    return pl.pallas_call(
        _moe_kernel,
        grid_spec=grid_spec,
        out_shape=jax.ShapeDtypeStruct((n_tiles * MOE_TM * TOK_ROWS, LANES), F32),
        compiler_params=_cparams(("arbitrary",)),
        name="moe",
    )(ea, eb, flags, hs, rwt_bf16, w_gate, w_up, w_down, w_gate, w_up, w_down)


def _gather_expert_rows(packed_ref, rs_ref, ys_ref, ybuf, sem, tm):
    i = pl.program_id(0)
    n_i = pl.num_programs(0)
    slot = i % 2

    def gather(tile, buf_slot):
        def body(r8, _):
            for j in range(8):
                r = r8 * 8 + j
                d = _class_row(packed_ref[0, tile * tm + r], rs_ref)
                pltpu.make_async_copy(ys_ref.at[pl.ds(pl.multiple_of(d * TOK_ROWS, TOK_ROWS), TOK_ROWS)],
                                      ybuf.at[buf_slot, pl.ds(pl.multiple_of(r * TOK_ROWS, TOK_ROWS), TOK_ROWS)],
                                      sem.at[buf_slot]).start(priority=j % 2)
            return 0
        lax.fori_loop(0, tm // 8, body, 0)

    @pl.when(i == 0)
    def _():
        gather(0, 0)

    @pl.when(i + 1 < n_i)
    def _():
        gather(jnp.minimum(i + 1, n_i - 1), 1 - slot)

    pltpu.make_async_copy(ys_ref.at[pl.ds(0, tm * TOK_ROWS)], ybuf.at[slot], sem.at[slot]).wait()
    return _load_token_tiles(ybuf, tm, slot)


def _fin_proj_kernel(packed_ref, rs_ref, x_ref, ys_ref, g_ref, lnw_ref, lnb_ref, sh_ref, sc_ref, w_f32_ref,
                     x2_ref, u_ref, ybuf, sem, w_ref, *, tm):
    _cast_once(w_f32_ref, w_ref)
    y = _gather_expert_rows(packed_ref, rs_ref, ys_ref, ybuf, sem, tm)
    x2 = _layer_norm(ALPHA * x_ref[...] + g_ref[...] * y, lnw_ref[...], lnb_ref[...])
    x2_ref[...] = x2
    h = (x2 * (1.0 + sc_ref[...]) + sh_ref[...]).astype(BF16)
    for s in range(u_ref.shape[1] // SEC):
        u_ref[:, s * SEC:(s + 1) * SEC] = jnp.dot(h, w_ref[:, s * SEC:(s + 1) * SEC],
                                                  preferred_element_type=F32).astype(BF16)


def _fin_proj(x1, ys, packed, row_start, mod, layer, ln_w, ln_b, w_next, cond_row, tm=512):
    T = x1.shape[0]
    N = w_next.shape[1]
    row = pl.BlockSpec((tm, D_MODEL), lambda i, *_: (i, 0))
    vec = pl.BlockSpec((1, D_MODEL), lambda i, *_: (0, 0))
    crow = lambda i: cond_row(i, tm)
    grid_spec = pltpu.PrefetchScalarGridSpec(
        num_scalar_prefetch=2,
        grid=(T // tm,),
        in_specs=[row, pl.BlockSpec(memory_space=pl.ANY), _mod_spec(layer, 5, crow), vec, vec,
                  _mod_spec(layer + 1, 0, crow), _mod_spec(layer + 1, 1, crow),
                  pl.BlockSpec((D_MODEL, N), lambda i, *_: (0, 0), pipeline_mode=pl.Buffered(1))],
        out_specs=[row, pl.BlockSpec((tm, N), lambda i, *_: (i, 0))],
        scratch_shapes=[pltpu.VMEM((2, tm * TOK_ROWS, LANES), F32), pltpu.SemaphoreType.DMA((2,)),
                        pltpu.VMEM((D_MODEL, N), BF16)],
    )
    return pl.pallas_call(
        functools.partial(_fin_proj_kernel, tm=tm),
        grid_spec=grid_spec,
        out_shape=[jax.ShapeDtypeStruct((T, D_MODEL), F32), jax.ShapeDtypeStruct((T, N), BF16)],
        compiler_params=_cparams(("arbitrary",)),
        name="post_moe_in_proj",
    )(packed, row_start, x1, ys, mod, ln_w.reshape(1, -1), ln_b.reshape(1, -1), mod, mod, w_next)


def _fin_kernel(packed_ref, rs_ref, x_ref, ys_ref, g_ref, lnw_ref, lnb_ref, *rest, tm, group_tiles):
    o_refs = rest[:len(group_tiles)]
    ybuf, sem = rest[len(group_tiles):]
    i = pl.program_id(0)
    y = _layer_norm(ALPHA * x_ref[...] + g_ref[...] * _gather_expert_rows(packed_ref, rs_ref, ys_ref, ybuf, sem, tm),
                    lnw_ref[...], lnb_ref[...])
    first = 0
    for o_ref, nt in zip(o_refs, group_tiles):
        @pl.when(jnp.logical_and(i >= first, i < first + nt))
        def _(o_ref=o_ref):
            o_ref[...] = y
        first += nt


def _fin(x1, ys, packed, row_start, mod, layer, ln_w, ln_b, cond_row, group_rows, tm=256):
    group_tiles = tuple(n // tm for n in group_rows)
    row = pl.BlockSpec((tm, D_MODEL), lambda i, *_: (i, 0))
    vec = pl.BlockSpec((1, D_MODEL), lambda i, *_: (0, 0))
    out_specs, first = [], 0
    for nt in group_tiles:
        out_specs.append(_group_spec(D_MODEL, tm, first, nt))
        first += nt
    grid_spec = pltpu.PrefetchScalarGridSpec(
        num_scalar_prefetch=2,
        grid=(sum(group_tiles),),
        in_specs=[row, pl.BlockSpec(memory_space=pl.ANY), _mod_spec(layer, 5, lambda i: cond_row(i, tm)), vec, vec],
        out_specs=out_specs,
        scratch_shapes=[pltpu.VMEM((2, tm * TOK_ROWS, LANES), F32), pltpu.SemaphoreType.DMA((2,))],
    )
    return pl.pallas_call(
        functools.partial(_fin_kernel, tm=tm, group_tiles=group_tiles),
        grid_spec=grid_spec,
        out_shape=[jax.ShapeDtypeStruct((n, D_MODEL), F32) for n in group_rows],
        compiler_params=_cparams(("arbitrary",)),
        name="post_moe",
    )(packed, row_start, x1, ys, mod, ln_w.reshape(1, -1), ln_b.reshape(1, -1))


def _filt_kernel(z_ref, w1_ref, b1_ref, w2_ref, b2_ref, fr_ref, w3_ref, dl_ref, o_ref):
    i = pl.program_id(0)
    z = z_ref[...]
    fr = fr_ref[...]
    a = jnp.sin(fr * (jnp.dot(z, w1_ref[...], precision=HIGHEST, preferred_element_type=F32) + b1_ref[...]))
    a = jnp.sin(fr * (jnp.dot(a, w2_ref[...], precision=HIGHEST, preferred_element_type=F32) + b2_ref[...]))
    filt = _dot_3pass(a, w3_ref[...])
    window = jnp.exp(-z[:, 0:1] * dl_ref[...]) + HY_SHIFT
    rows = i * z.shape[0] + lax.broadcasted_iota(jnp.int32, filt.shape, 0)
    o_ref[...] = jnp.where(rows == 0, 0.0, filt * window)


def _hyena_taps(seq_len, w1, b1, w2, b2, w3, freq):
    L = seq_len
    t = np.linspace(0.0, 1.0, L)[:, None]
    bands = np.linspace(1e-4, HY_BANDS - 1, HY_BANDS)
    ang = 2.0 * math.pi * bands[None, :] * np.arange(L)[:, None] / L
    z = np.concatenate([t, np.cos(ang), -np.sin(ang)], axis=-1)
    offs = np.minimum(np.abs(np.arange(2 * L) - L), L - 1)
    z2_np = np.zeros((2 * L, LANES), np.float32)
    z2_np[:, :z.shape[1]] = z[offs]
    z2 = jnp.asarray(z2_np)
    pad_c = lambda a: jnp.zeros((a.shape[0], LANES), F32).at[:, :a.shape[1]].set(a)
    pad_r = lambda a: jnp.zeros((LANES, a.shape[1]), F32).at[:a.shape[0], :].set(a)
    w1p = pad_r(pad_c(w1))
    w2p = pad_r(pad_c(w2))
    w3p = pad_r(w3)
    b1p, b2p, frp = pad_c(b1[None, :]), pad_c(b2[None, :]), pad_c(freq[None, :])
    deltas = jnp.asarray(np.abs(np.linspace(math.log(HY_DECAY_TARGET) / HY_SLOW, math.log(HY_DECAY_TARGET) / HY_FAST,
                                            D_MODEL))[None, :].astype(np.float32))
    rb = min(1024, L)
    cbf = D_MODEL
    ncb = D_MODEL // cbf
    nrb_back = L // rb
    sq = pl.BlockSpec((LANES, LANES), lambda i, j: (0, 0))
    vec = pl.BlockSpec((1, LANES), lambda i, j: (0, 0))
    return pl.pallas_call(
        _filt_kernel,
        grid=(2 * L // rb, ncb),
        in_specs=[
            pl.BlockSpec((rb, LANES), lambda i, j: (i, 0)),
            sq, vec, sq, vec, vec,
            pl.BlockSpec((LANES, cbf), lambda i, j: (0, jnp.where(i < nrb_back, ncb + j, j))),
            pl.BlockSpec((1, cbf), lambda i, j: (0, j)),
        ],
        out_specs=pl.BlockSpec((rb, cbf), lambda i, j: (i, j)),
        out_shape=jax.ShapeDtypeStruct((2 * L, D_MODEL), F32),
        compiler_params=_cparams(("arbitrary", "arbitrary")),
        name="hyena_taps",
    )(z2, w1p, b1p, w2p, b2p, frp, w3p, deltas)


def _dft_mats(cb):
    n = 2 * cb
    m = np.arange(cb)
    f = np.arange(cb)
    ang = 2.0 * np.pi * ((f[:, None] * m[None, :]) % n) / n
    fwd = np.concatenate([np.cos(ang), -np.sin(ang)], axis=0)
    fwd[cb, :] = np.where(m % 2 == 0, 1.0, -1.0)
    coef = np.where(f == 0, 1.0, 2.0)[None, :] / n
    inv = np.concatenate([coef * np.cos(ang.T), -coef * np.sin(ang.T)], axis=1)
    inv[:, cb] = np.where(m % 2 == 0, 1.0, -1.0) / n
    return fwd.astype(np.float32), inv.astype(np.float32)


def _hconv_kernel(x0_ref, x1_ref, v_ref, cw0_ref, cw1_ref, cw2_ref, cb0_ref, cb1_ref, cb2_ref,
                  taps_ref, fb_ref, fwd_ref, inv_ref, o_ref,
                  hs_ref, stage_ref, w32_ref, w_ref, x0c_ref, u_ref, y_ref, *, seq_len, blk, n_seq):
    L = seq_len
    nb = L // blk
    cw = x0_ref.shape[1]
    RC = 256
    bi = pl.program_id(1)
    row0 = lax.broadcasted_iota(jnp.int32, (RC, cw), 0) == 0

    @pl.when(bi == 0)
    def _():
        rows = lax.broadcasted_iota(jnp.int32, (2 * blk, 1), 0)
        sign = jnp.where(rows % 2 == 0, 1.0, -1.0)
        real_row = rows <= blk
        prev = None
        for jb in range(2 * nb):
            cur = jnp.dot(fwd_ref[...], taps_ref[jb * blk:(jb + 1) * blk, :].astype(BF16),
                          preferred_element_type=F32)
            if prev is not None:
                first_tap = taps_ref[(jb - 1) * blk:(jb - 1) * blk + 1, :].astype(BF16).astype(F32)
                hs_ref[jb - 1] = cur + sign * (prev - jnp.where(real_row, first_tap, 0.0))
            prev = cur

    zeros8 = jnp.zeros((8, cw), F32)
    for s in range(n_seq):
        r0 = s * L

        def short_conv(k, src_ref, cw_ref, cb_ref, store):
            stage_ref[k, 0:8, :] = zeros8
            stage_ref[k, 8 + L:16 + L, :] = zeros8
            for c in range(L // RC):
                stage_ref[k, 8 + c * RC:8 + (c + 1) * RC, :] = \
                    src_ref[r0 + c * RC:r0 + (c + 1) * RC, :].astype(F32)
            w = cw_ref[...]
            for c in range(L // RC):
                lo = stage_ref[k, 7 + c * RC:7 + (c + 1) * RC, :]
                mid = stage_ref[k, 8 + c * RC:8 + (c + 1) * RC, :]
                hi = stage_ref[k, 9 + c * RC:9 + (c + 1) * RC, :]
                store(c, lo * w[0:1, :] + mid * w[1:2, :] + hi * w[2:3, :] + cb_ref[...])

        def st_x1(c, val):
            w32_ref[c * RC:(c + 1) * RC, :] = val

        def st_v(c, val):
            w_ref[c * RC:(c + 1) * RC, :] = (w32_ref[c * RC:(c + 1) * RC, :] * val).astype(BF16)

        def st_x0(c, val):
            x0c_ref[c * RC:(c + 1) * RC, :] = val.astype(BF16)

        short_conv(0, x1_ref, cw1_ref, cb1_ref, st_x1)
        short_conv(1, v_ref, cw2_ref, cb2_ref, st_v)
        short_conv(2, x0_ref, cw0_ref, cb0_ref, st_x0)

        for j in range(nb):
            u_ref[j] = jnp.dot(fwd_ref[...], w_ref[j * blk:(j + 1) * blk, :], preferred_element_type=F32)

        for i in range(nb):
            for c in range(blk // RC):
                re = None
                im = None
                for j in range(nb):
                    k = i - j + nb - 1
                    a = u_ref[j, c * RC:(c + 1) * RC, :]
                    b = u_ref[j, blk + c * RC:blk + (c + 1) * RC, :]
                    hr = hs_ref[k, c * RC:(c + 1) * RC, :]
                    hi = hs_ref[k, blk + c * RC:blk + (c + 1) * RC, :]
                    bb = b * hi
                    if c == 0:
                        t_re = a * hr - jnp.where(row0, 0.0, bb)
                        t_im = jnp.where(row0, bb, a * hi + b * hr)
                    else:
                        t_re = a * hr - bb
                        t_im = a * hi + b * hr
                    re = t_re if re is None else re + t_re
                    im = t_im if im is None else im + t_im
                y_ref[i, c * RC:(c + 1) * RC, :] = re.astype(BF16)
                y_ref[i, blk + c * RC:blk + (c + 1) * RC, :] = im.astype(BF16)
            conv = jnp.dot(inv_ref[...], y_ref[i], preferred_element_type=F32)
            sl = slice(i * blk, (i + 1) * blk)
            z = x0c_ref[sl, :].astype(F32) * (conv + w_ref[sl, :].astype(F32) * fb_ref[...])
            o_ref[r0 + i * blk:r0 + (i + 1) * blk, :] = z.astype(BF16)


def _hyena_conv(u, conv_w, conv_b, taps, filt_bias, batch, seq_len, blk, n_seq, row_off=0, cw=256):
    T = batch * seq_len
    L = seq_len
    nb = L // blk
    ncw = D_MODEL // cw
    fwd_np, inv_np = _dft_mats(blk)
    fwd = jnp.asarray(fwd_np).astype(BF16)
    inv = jnp.asarray(inv_np).astype(BF16)
    rows = n_seq * L
    assert row_off % rows == 0
    boff = row_off // rows
    sec = lambda s: pl.BlockSpec((rows, cw), lambda c, b: (b + boff, s * ncw + c))
    cws = lambda s: pl.BlockSpec((3, cw), lambda c, b: (0, s * ncw + c))
    cbs = lambda s: pl.BlockSpec((1, cw), lambda c, b: (0, s * ncw + c))
    return pl.pallas_call(
        functools.partial(_hconv_kernel, seq_len=L, blk=blk, n_seq=n_seq),
        grid=(ncw, batch // n_seq),
        in_specs=[sec(0), sec(1), sec(2), cws(0), cws(1), cws(2), cbs(0), cbs(1), cbs(2),
                  pl.BlockSpec((2 * L, cw), lambda c, b: (0, c), pipeline_mode=pl.Buffered(1)),
                  pl.BlockSpec((1, cw), lambda c, b: (0, c)),
                  pl.BlockSpec((2 * blk, blk), lambda c, b: (0, 0), pipeline_mode=pl.Buffered(1)),
                  pl.BlockSpec((blk, 2 * blk), lambda c, b: (0, 0), pipeline_mode=pl.Buffered(1))],
        out_specs=pl.BlockSpec((rows, cw), lambda c, b: (b, c)),
        out_shape=jax.ShapeDtypeStruct((T, D_MODEL), BF16),
        scratch_shapes=[
            pltpu.VMEM((2 * nb - 1, 2 * blk, cw), F32),
            pltpu.VMEM((3, L + 16, cw), F32),
            pltpu.VMEM((L, cw), F32),
            pltpu.VMEM((L, cw), BF16),
            pltpu.VMEM((L, cw), BF16),
            pltpu.VMEM((nb, 2 * blk, cw), F32),
            pltpu.VMEM((nb, 2 * blk, cw), BF16),
        ],
        compiler_params=_cparams(("arbitrary", "arbitrary")),
        name="hyena_conv",
    )(u, u, u, conv_w, conv_w, conv_w, conv_b.reshape(1, -1), conv_b.reshape(1, -1), conv_b.reshape(1, -1),
      taps, filt_bias.reshape(1, -1), fwd, inv)


def kernel(x_prompt, x_sample, cache_diff_k, cache_diff_v, state_ret_fwd, state_ret_bwd, c, c_ctx, ada_w, ada_b, ln_w, ln_b, ev_w_in, ev_w_out, ret_decay_fwd, ret_decay_bwd, ret_gn_w, diff_lambda, diff_subln_w, hy_w_in, hy_conv_w, hy_conv_b, hy_ffn_w1, hy_ffn_b1, hy_ffn_w2, hy_ffn_b2, hy_ffn_w3, hy_freq, hy_filter_bias, hy_w_out, router_w, router_bias, moe_w_gate, moe_w_up, moe_w_down):
    B, S, D = x_prompt.shape
    DB, DS, _ = x_sample.shape
    PAST = cache_diff_k.shape[2]
    TP, TS = B * S, DB * DS
    T_ALL = TP + TS
    assert D == D_MODEL and 1 + DB <= COND_ROWS and T_ALL < 65536

    cond8 = jnp.zeros((COND_ROWS, D), F32).at[0].set(c_ctx).at[1:1 + DB].set(c)
    mod = _ada_mod(cond8, ada_w, ada_b).reshape(DEPTH * COND_ROWS * N_MOD, 1, D)

    tm = 512
    groups = [
        dict(x=x_prompt.reshape(TP, D), x_off=0, batch=B, seq=S, off=0, cond_row=lambda i, tm: 0),
        dict(x=x_sample.reshape(TS, D), x_off=0, batch=DB, seq=DS, off=TP,
             cond_row=lambda i, tm: 1 + (i * tm) // DS),
    ]
    u_next = None

    def cond_row_all(i, tm):
        return jnp.where(i * tm < TP, 0, 1 + (i * tm - TP) // DS)

    rw_pad = jnp.zeros((D, LANES), F32).at[:, :N_EXPERTS].set(router_w.astype(F32))
    rw_hi = rw_pad.astype(BF16)
    rw_lo = jnp.concatenate([rw_hi, (rw_pad - rw_hi.astype(F32)).astype(BF16)], axis=1)
    rwt = router_w.T.astype(BF16)
    rbias = router_bias.astype(F32)
    n_tiles = T_ALL // MOE_TM + N_CLASSES
    outs = {}

    for l in range(DEPTH):
        mixes = []
        if l % 2 == 0:
            e = l // 2
            w_in = ev_w_in[e]
            w_out = ev_w_out[e]
            lg = jnp.stack([jnp.log1p(-jnp.exp2(ret_decay_fwd[e].astype(F32))),
                            jnp.log1p(-jnp.exp2(ret_decay_bwd[e].astype(F32)))])
            lam_init = 0.8 - 0.6 * math.exp(-0.3 * l)
            lq1, lk1, lq2, lk2 = diff_lambda[e].astype(F32)
            lam = (jnp.exp(jnp.sum(lq1 * lk1)) - jnp.exp(jnp.sum(lq2 * lk2)) + lam_init).reshape(1)
            kscale = RET_DK ** -0.5
            for gi, g in enumerate(groups):
                tm_in = tm
                crow = functools.partial(g["cond_row"], tm=tm_in)
                if gi == 0:
                    secs = (("none", 1.0, False), ("none", kscale, False), ("none", 1.0, False),
                            ("none", 1.0, False), ("none", 1.0, False), ("none", 1.0, True),
                            ("none", 1.0, True))
                    proj, kd, vd = _in_proj(g["x"], mod, l, w_in, secs, crow, tm_in, tm=tm_in)
                    outs.setdefault("kd", []).append(kd.reshape(B, 1, S, H_DIFF, 2 * DIFF_QK))
                    outs.setdefault("vd", []).append(vd.reshape(B, 1, S, H_DIFF, DIFF_V))
                    ret, sf, sb = _retention(proj, lg, ret_gn_w[e], g["batch"], g["seq"], write_state=True,
                                             heads=H_RET)
                    outs.setdefault("sf", []).append(sf.reshape(B, 1, H_RET, RET_DK, RET_DV))
                    outs.setdefault("sb", []).append(sb.reshape(B, 1, H_RET, RET_DK, RET_DV))
                    att = _diff_attention(proj, lam, diff_subln_w[e], lam_init, g["batch"], g["seq"],
                                          tq=min(256, g["seq"]), heads=H_DIFF)
                else:
                    secs = (("ret", 1.0, False), ("ret", kscale, False), ("none", 1.0, False),
                            ("none", 1.0, False), ("diff", 1.0, False), ("diff", 1.0, False),
                            ("none", 1.0, False))
                    tabs = {"ret": _rope_tables(g["seq"], RET_DK), "diff": _rope_tables(g["seq"], DIFF_QK)}
                    (proj,) = _in_proj(g["x"], mod, l, w_in, secs, crow, g["seq"], rope_tabs=tabs, tm=tm_in)
                    ret, = _retention(proj, lg, ret_gn_w[e], g["batch"], g["seq"],
                                      states=(state_ret_fwd[:, e], state_ret_bwd[:, e]))
                    ctx = (cache_diff_k[:, e].reshape(DB, PAST, H_DIFF * 2 * DIFF_QK),
                           cache_diff_v[:, e].reshape(DB, PAST, H_DIFF * DIFF_V))
                    att = _diff_attention(proj, lam, diff_subln_w[e], lam_init, g["batch"], g["seq"], ctx=ctx,
                                          tq=512, heads=2)
                mixes.append((ret, att))
        else:
            o = l // 2
            w_in = hy_w_in[o]
            w_out = hy_w_out[o]
            secs = (("none", 1.0, False),) * (3 * D // SEC)
            for gi, g in enumerate(groups):
                if u_next is None:
                    crow = functools.partial(g["cond_row"], tm=tm)
                    (u,) = _in_proj(g["x"], mod, l, w_in, secs, crow, tm, tm=tm)
                    u_off = 0
                else:
                    u, u_off = u_next, g["off"]
                taps = _hyena_taps(g["seq"], hy_ffn_w1[o], hy_ffn_b1[o], hy_ffn_w2[o], hy_ffn_b2[o],
                                   hy_ffn_w3[o], hy_freq[o])
                blk = min(g["seq"], 1024)
                n_seq = max(1, 1024 // g["seq"])
                z = _hyena_conv(u, hy_conv_w[o], hy_conv_b[o], taps, hy_filter_bias[o], g["batch"], g["seq"],
                                blk, n_seq, row_off=u_off)
                mixes.append((z,))

        x1_all, h2_all, info_all, counts = _post_mix(mixes, [(g["x"], g["x_off"]) for g in groups], w_out, mod, l, ln_w[l, 0],
                                                     ln_b[l, 0], rw_hi, rw_lo, rbias, cond_row_all, tm)
        packed = info_all
        counts_i = counts[:, 0].astype(jnp.int32)
        plan = _moe_plan(counts_i, n_tiles)
        hs = _dispatch(h2_all, packed, plan, counts_i, n_tiles)
        ys = _moe(hs, plan, l, rwt, moe_w_gate, moe_w_up, moe_w_down)
        if l + 1 < DEPTH and (l + 1) % 2 == 1:
            x_all, u_next = _fin_proj(x1_all, ys, packed, plan[3], mod, l, ln_w[l, 1], ln_b[l, 1],
                                      hy_w_in[(l + 1) // 2], cond_row_all, tm=tm)
            for g in groups:
                g["x"], g["x_off"] = x_all, g["off"]
        else:
            u_next = None
            xs = _fin(x1_all, ys, packed, plan[3], mod, l, ln_w[l, 1], ln_b[l, 1], cond_row_all, (TP, TS))
            for g, x in zip(groups, xs):
                g["x"], g["x_off"] = x, 0

    y_prompt = groups[0]["x"].reshape(B, S, D)
    y_sample = groups[1]["x"].reshape(DB, DS, D)
    cat = lambda xs: xs[0] if len(xs) == 1 else jnp.concatenate(xs, axis=1)
    return (y_prompt, y_sample, cat(outs["kd"]), cat(outs["vd"]), cat(outs["sf"]), cat(outs["sb"]))
```

```python
import functools
import math

import numpy as np
import jax
import jax.numpy as jnp
from jax import lax
from jax.experimental import pallas as pl
from jax.experimental.pallas import tpu as pltpu

F32 = jnp.float32
BF16 = jnp.bfloat16
HIGHEST = lax.Precision.HIGHEST

D_MODEL = 1024
DEPTH = 2
GRID_W = 64
H_RET = 4
RET_DK = 128
RET_DV = 128
RET_CHUNK = 128
H_DIFF = 4
DIFF_QK = 64
DIFF_V = 128
ROPE_BASE = 10000.0
HY_BANDS = 16
HY_FH = 64
HY_DECAY_TARGET = 1e-2
HY_FAST = 0.3
HY_SLOW = 1.5
HY_SHIFT = 0.05
N_EXPERTS = 16
N_GROUPS = 4
GROUP_SIZE = N_EXPERTS // N_GROUPS
D_FF_EXPERT = 512
ALPHA = (2 * DEPTH) ** 0.25
LN_EPS = 1e-5

LANES = 128
SEC = 512
COND_ROWS = 8
N_MOD = 6
VMEM_LIMIT = 50 * 1024 * 1024

PAIR_SLOTS = ((0, 1), (2, 1), (2, 3), (0, 3), (0, 2), (1, 3))
N_PAIRS = len(PAIR_SLOTS)
N_CLASSES = N_GROUPS * N_PAIRS
MOE_TM = 256
MOE_SHIFT = MOE_TM.bit_length() - 1
POST_SUB = 512
TOK_ROWS = D_MODEL // LANES


def _cparams(sem):
    return pltpu.CompilerParams(dimension_semantics=sem, vmem_limit_bytes=VMEM_LIMIT)


def _silu(x):
    return x * jax.nn.sigmoid(x)


def _store_token_tiles(ref, x):
    n = x.shape[0]
    for j in range(TOK_ROWS):
        ref[pl.ds(j, n, stride=TOK_ROWS), :] = x[:, j * LANES:(j + 1) * LANES]


def _load_token_tiles(ref, n, slot=None):
    idx = () if slot is None else (slot,)
    return jnp.concatenate([ref[idx + (pl.ds(j, n, stride=TOK_ROWS), slice(None))] for j in range(TOK_ROWS)],
                           axis=1)


def _layer_norm(x, w, b):
    mu = jnp.mean(x, axis=-1, keepdims=True)
    xc = x - mu
    var = jnp.mean(xc * xc, axis=-1, keepdims=True)
    return xc * lax.rsqrt(var + LN_EPS) * w + b


def _dot_3pass(a, w):
    a_hi = a.astype(BF16)
    a_lo = (a - a_hi.astype(F32)).astype(BF16)
    w_hi = w.astype(BF16)
    w_lo = (w - w_hi.astype(F32)).astype(BF16)
    return (jnp.dot(a_hi, w_hi, preferred_element_type=F32) + jnp.dot(a_lo, w_hi, preferred_element_type=F32)
            + jnp.dot(a_hi, w_lo, preferred_element_type=F32))


def _ada_kernel(c_ref, w_ref, b_ref, o_ref):
    o_ref[...] = _dot_3pass(_silu(c_ref[...]), w_ref[...]) + b_ref[...]


def _ada_mod(cond8, ada_w, ada_b):
    tn = 1024
    nj = ada_w.shape[2] // tn
    return pl.pallas_call(
        _ada_kernel,
        grid=(DEPTH, nj),
        in_specs=[
            pl.BlockSpec((COND_ROWS, D_MODEL), lambda l, j: (0, 0)),
            pl.BlockSpec((None, D_MODEL, tn), lambda l, j: (l, 0, j)),
            pl.BlockSpec((None, 1, tn), lambda l, j: (l, 0, j)),
        ],
        out_specs=pl.BlockSpec((None, COND_ROWS, tn), lambda l, j: (l, 0, j)),
        out_shape=jax.ShapeDtypeStruct((DEPTH, COND_ROWS, ada_w.shape[2]), F32),
        compiler_params=_cparams(("arbitrary", "arbitrary")),
        name="ada_mod",
    )(cond8, ada_w, ada_b.reshape(DEPTH, 1, -1))


def _mod_spec(layer, chunk, row_of_tile):
    def imap(i, *_):
        return ((layer * COND_ROWS + row_of_tile(i)) * N_MOD + chunk, 0, 0)
    return pl.BlockSpec((None, 1, D_MODEL), imap)


def _rope(a, tabs, quarter):
    c, sa, sb = tabs
    out = []
    for hb in range(a.shape[1] // LANES):
        blk = a[:, hb * LANES:(hb + 1) * LANES]
        up = pltpu.roll(blk, LANES - quarter, axis=1)
        dn = pltpu.roll(blk, quarter, axis=1)
        out.append(blk * c + up * sa + dn * sb)
    return jnp.concatenate(out, axis=1)


def _cast_once(w_ref, w_bf16_ref):
    @pl.when(pl.program_id(0) == 0)
    def _():
        for c in range(0, w_ref.shape[1], SEC):
            w_bf16_ref[:, c:c + SEC] = w_ref[:, c:c + SEC].astype(BF16)


def _in_kernel(*refs, secs, n_f32_out):
    x_ref, sh_ref, sc_ref, w_ref = refs[:4]
    pos = 4
    tabs = {}
    for kind in ("ret", "diff"):
        if any(s[0] == kind for s in secs):
            tabs[kind] = tuple(r[...] for r in refs[pos:pos + 3])
            pos += 3
    o_ref = refs[pos]
    f32_refs = refs[pos + 1:pos + 1 + n_f32_out]
    w_bf16_ref = refs[-1]
    _cast_once(w_ref, w_bf16_ref)
    h = (x_ref[...] * (1.0 + sc_ref[...]) + sh_ref[...]).astype(BF16)
    k32 = 0
    for s, (kind, scale, want_f32) in enumerate(secs):
        acc = jnp.dot(h, w_bf16_ref[:, s * SEC:(s + 1) * SEC], preferred_element_type=F32)
        if scale != 1.0:
            acc = acc * scale
        if kind == "ret":
            acc = _rope(acc, tabs["ret"], RET_DK // 4)
        elif kind == "diff":
            acc = _rope(acc, tabs["diff"], DIFF_QK // 4)
        o_ref[:, s * SEC:(s + 1) * SEC] = acc.astype(BF16)
        if want_f32:
            for hb in range(SEC // LANES):
                f32_refs[k32][:, hb, :] = acc[:, hb * LANES:(hb + 1) * LANES]
            k32 += 1
    assert k32 == n_f32_out


def _in_proj(x2d, mod, layer, w_f32, secs, row_of_tile, seq_len, rope_tabs=None, tm=512):
    T = x2d.shape[0]
    N = w_f32.shape[1]
    assert N == SEC * len(secs) and T % tm == 0 and seq_len % tm == 0
    tiles_per_seq = seq_len // tm
    in_specs = [
        pl.BlockSpec((tm, D_MODEL), lambda i: (i, 0)),
        _mod_spec(layer, 0, row_of_tile),
        _mod_spec(layer, 1, row_of_tile),
        pl.BlockSpec((D_MODEL, N), lambda i: (0, 0), pipeline_mode=pl.Buffered(1)),
    ]
    args = [x2d, mod, mod, w_f32]
    for kind in ("ret", "diff"):
        if any(s[0] == kind for s in secs):
            for t in rope_tabs[kind]:
                in_specs.append(pl.BlockSpec((tm, LANES), lambda i: (i % tiles_per_seq, 0)))
                args.append(t)
    n_f32 = sum(1 for s in secs if s[2])
    heads = SEC // LANES
    out_shape = [jax.ShapeDtypeStruct((T, N), BF16)] + [jax.ShapeDtypeStruct((T, heads, LANES), F32)] * n_f32
    out_specs = ([pl.BlockSpec((tm, N), lambda i: (i, 0))]
                 + [pl.BlockSpec((tm, heads, LANES), lambda i: (i, 0, 0))] * n_f32)
    return pl.pallas_call(
        functools.partial(_in_kernel, secs=secs, n_f32_out=n_f32),
        grid=(T // tm,),
        in_specs=in_specs,
        out_specs=out_specs,
        out_shape=out_shape,
        scratch_shapes=[pltpu.VMEM((D_MODEL, N), BF16)],
        compiler_params=_cparams(("arbitrary",)),
        name="in_proj",
    )(*args)


def _rope_tables(seq_len, d):
    half = d // 2
    quarter = half // 2
    t = np.arange(seq_len)
    inv = ROPE_BASE ** (-np.arange(quarter, dtype=np.float64) / quarter)
    ang_r = (t // GRID_W)[:, None] * inv[None, :]
    ang_c = (t % GRID_W)[:, None] * inv[None, :]
    zero = np.zeros_like(ang_r)
    cos = np.concatenate([np.cos(ang_r)] * 2 + [np.cos(ang_c)] * 2, axis=1)
    sa = np.concatenate([-np.sin(ang_r), zero, -np.sin(ang_c), zero], axis=1)
    sb = np.concatenate([zero, np.sin(ang_r), zero, np.sin(ang_c)], axis=1)
    reps = LANES // d
    return tuple(jnp.asarray(np.tile(a, (1, reps)).astype(np.float32)) for a in (cos, sa, sb))


def _ret_kernel(*refs, n_chunks, has_state, write_state, heads):
    lg_ref, q_ref, k_ref, v_ref, g_ref, gnw_ref = refs[:6]
    pos = 6
    if has_state:
        s0f_ref, s0b_ref = refs[pos:pos + 2]
        pos += 2
    o_ref = refs[pos]
    pos += 1
    if write_state:
        sf_ref, sb_ref = refs[pos:pos + 2]
        pos += 2
    acc_ref, kv_ref = refs[pos:pos + 2]
    for hh in range(heads):
        _ret_head(hh, heads, lg_ref, q_ref, k_ref, v_ref, g_ref, gnw_ref,
                  (s0f_ref, s0b_ref) if has_state else None, o_ref, (sf_ref, sb_ref) if write_state else None,
                  acc_ref, kv_ref, n_chunks)


def _ret_head(hh, heads, lg_ref, q_ref, k_ref, v_ref, g_ref, gnw_ref, s0_refs, o_ref, s_out_refs, acc_ref, kv_ref,
              n_chunks):
    C = RET_CHUNK
    cols = slice(hh * LANES, (hh + 1) * LANES)
    hd = pl.program_id(1) * heads + hh
    lgf = lg_ref[0, hd]
    lgb = lg_ref[1, hd]
    ii = lax.broadcasted_iota(jnp.int32, (C, C), 0).astype(F32)
    jj = lax.broadcasted_iota(jnp.int32, (C, C), 1).astype(F32)
    rel = ii - jj
    d_f = jnp.where(rel >= 0, jnp.exp(jnp.maximum(rel, 0.0) * lgf), 0.0)
    d_b = jnp.where(rel <= 0, jnp.exp(jnp.maximum(-rel, 0.0) * lgb), 0.0)
    d_sum = d_f + d_b
    idx = lax.broadcasted_iota(jnp.int32, (C, 1), 0).astype(F32)
    xi_f = jnp.exp((idx + 1.0) * lgf)
    zeta_f = jnp.exp((C - 1.0 - idx) * lgf)
    xi_b = jnp.exp((C - idx) * lgb)
    zeta_b = jnp.exp(idx * lgb)
    one = jnp.ones((1, 1), F32)
    gc_f = jnp.exp(one * (C * lgf))
    gc_b = jnp.exp(one * (C * lgb))

    nt = (((1,), (1,)), ((), ()))
    tn = (((0,), (0,)), ((), ()))

    if s0_refs is not None:
        s_f = s0_refs[0][hh]
        s_b = s0_refs[1][hh]
    else:
        s_f = jnp.zeros((RET_DK, RET_DV), F32)
        s_b = jnp.zeros((RET_DK, RET_DV), F32)

    for n in range(n_chunks):
        sl = slice(n * C, (n + 1) * C)
        qc, kc, vc = q_ref[sl, cols], k_ref[sl, cols], v_ref[sl, cols]
        scores = lax.dot_general(qc, kc, nt, preferred_element_type=F32) * d_sum
        acc_ref[sl, cols] = jnp.dot(scores.astype(BF16), vc, preferred_element_type=F32)
        kf = kc.astype(F32)
        kv_ref[0, hh, n] = lax.dot_general((kf * zeta_f).astype(BF16), vc, tn, preferred_element_type=F32)
        kv_ref[1, hh, n] = lax.dot_general((kf * zeta_b).astype(BF16), vc, tn, preferred_element_type=F32)

    for n in range(n_chunks):
        sl = slice(n * C, (n + 1) * C)
        qf = q_ref[sl, cols].astype(F32)
        acc_ref[sl, cols] += jnp.dot((qf * xi_f).astype(BF16), s_f.astype(BF16), preferred_element_type=F32)
        s_f = gc_f * s_f + kv_ref[0, hh, n]

    gnw = gnw_ref[:, cols]
    for n in reversed(range(n_chunks)):
        sl = slice(n * C, (n + 1) * C)
        qf = q_ref[sl, cols].astype(F32)
        cross = jnp.dot((qf * xi_b).astype(BF16), s_b.astype(BF16), preferred_element_type=F32)
        r = acc_ref[sl, cols] + cross
        mu = jnp.mean(r, axis=-1, keepdims=True)
        rc = r - mu
        var = jnp.mean(rc * rc, axis=-1, keepdims=True)
        rn = rc * lax.rsqrt(var + LN_EPS) * gnw
        o_ref[sl, cols] = (_silu(g_ref[sl, cols].astype(F32)) * rn).astype(BF16)
        s_b = gc_b * s_b + kv_ref[1, hh, n]

    if s_out_refs is not None:
        s_out_refs[0][hh] = s_f
        s_out_refs[1][hh] = s_b


def _retention(proj, lg, gn_w, batch, seq_len, states=None, write_state=False, heads=2):
    T = proj.shape[0]
    hp = heads
    width = hp * LANES
    per_sec = SEC // width
    blk = lambda sec: pl.BlockSpec((seq_len, width), lambda b, h: (b, sec * per_sec + h))
    in_specs = [pl.BlockSpec(memory_space=pltpu.SMEM), blk(0), blk(1), blk(2), blk(3),
                pl.BlockSpec((1, width), lambda b, h: (0, h))]
    args = [lg, proj, proj, proj, proj, gn_w.reshape(1, -1)]
    if states is not None:
        st = pl.BlockSpec((None, hp, RET_DK, RET_DV), lambda b, h: (b, h, 0, 0))
        in_specs += [st, st]
        args += list(states)
    out_shape = [jax.ShapeDtypeStruct((T, SEC), BF16)]
    out_specs = [pl.BlockSpec((seq_len, width), lambda b, h: (b, h))]
    if write_state:
        st_o = pl.BlockSpec((None, hp, RET_DK, RET_DV), lambda b, h: (b, h, 0, 0))
        out_shape += [jax.ShapeDtypeStruct((batch, H_RET, RET_DK, RET_DV), F32)] * 2
        out_specs += [st_o, st_o]
    return pl.pallas_call(
        functools.partial(_ret_kernel, n_chunks=seq_len // RET_CHUNK, has_state=states is not None,
                          write_state=write_state, heads=hp),
        grid=(batch, H_RET // hp),
        in_specs=in_specs,
        out_specs=out_specs,
        out_shape=out_shape,
        scratch_shapes=[pltpu.VMEM((seq_len, width), F32),
                        pltpu.VMEM((2, hp, seq_len // RET_CHUNK, RET_DK, RET_DV), F32)],
        compiler_params=_cparams(("arbitrary", "arbitrary")),
        name="retention",
    )(*args)


def _att_kernel(*refs, has_ctx, out_scale, key_chunk, heads):
    lam_ref, q_ref, k_ref, v_ref = refs[:4]
    pos = 4
    if has_ctx:
        ck_ref, cv_ref = refs[pos:pos + 2]
        pos += 2
    w_ref, o_ref = refs[pos:pos + 2]
    lam = lam_ref[0]
    tq = q_ref.shape[0]
    nt = (((1,), (1,)), ((), ()))
    for hh in range(heads):
        cols = slice(hh * LANES, (hh + 1) * LANES)
        q = q_ref[:, cols]
        lane = lax.broadcasted_iota(jnp.int32, q.shape, 1)
        zero = jnp.zeros_like(q)
        qq = jnp.concatenate([jnp.where(lane < DIFF_QK, q, zero), jnp.where(lane >= DIFF_QK, q, zero)], axis=0)
        qq = qq * jnp.asarray(DIFF_QK ** -0.5, BF16)
        chunks = [(k_ref, v_ref, c * key_chunk, key_chunk) for c in range(k_ref.shape[0] // key_chunk)]
        if has_ctx:
            chunks.append((ck_ref, cv_ref, 0, ck_ref.shape[0]))
        m = l = acc = None
        for kr, vr, off, n in chunks:
            kch = kr[off:off + n, cols].astype(BF16)
            vch = vr[off:off + n, cols].astype(BF16)
            s = lax.dot_general(qq, kch, nt, preferred_element_type=F32)
            cm = jnp.max(s, axis=-1, keepdims=True)
            m_new = cm if m is None else jnp.maximum(m, cm)
            p = jnp.exp(s - m_new)
            ps = jnp.sum(p, axis=-1, keepdims=True)
            pv = jnp.dot(p.astype(BF16), vch, preferred_element_type=F32)
            if m is None:
                l, acc = ps, pv
            else:
                alpha = jnp.exp(m - m_new)
                l = alpha * l + ps
                acc = alpha * acc + pv
            m = m_new
        o = acc / l
        att = o[:tq] - lam * o[tq:]
        att = att * lax.rsqrt(jnp.mean(att * att, axis=-1, keepdims=True) + LN_EPS)
        o_ref[:, cols] = (att * w_ref[...] * out_scale).astype(BF16)


def _diff_attention(proj, lam, subln_w, lam_init, batch, seq_len, ctx=None, tq=256, heads=1):
    T = proj.shape[0]
    width = heads * LANES
    per_sec = SEC // width
    nq = seq_len // tq
    in_specs = [
        pl.BlockSpec(memory_space=pltpu.SMEM),
        pl.BlockSpec((tq, width), lambda b, h, i: (b * nq + i, 4 * per_sec + h)),
        pl.BlockSpec((seq_len, width), lambda b, h, i: (b, 5 * per_sec + h)),
        pl.BlockSpec((seq_len, width), lambda b, h, i: (b, 6 * per_sec + h)),
    ]
    args = [lam, proj, proj, proj]
    if ctx is not None:
        ck, cv = ctx
        past = ck.shape[1]
        cspec = pl.BlockSpec((None, past, width), lambda b, h, i: (b, 0, h))
        in_specs += [cspec, cspec]
        args += [ck, cv]
    in_specs.append(pl.BlockSpec((1, LANES), lambda b, h, i: (0, 0)))
    args.append(subln_w.reshape(1, -1))
    return pl.pallas_call(
        functools.partial(_att_kernel, has_ctx=ctx is not None, out_scale=1.0 - lam_init,
                          key_chunk=min(512, seq_len), heads=heads),
        grid=(batch, H_DIFF // heads, nq),
        in_specs=in_specs,
        out_specs=pl.BlockSpec((tq, width), lambda b, h, i: (b * nq + i, h)),
        out_shape=jax.ShapeDtypeStruct((T, SEC), BF16),
        compiler_params=_cparams(("arbitrary", "arbitrary", "arbitrary")),
        name="diff_attention",
    )(*args)


def _route_class(lt, rb_ref):
    sel = [jax.nn.sigmoid(lt[e:e + 1, :]) + rb_ref[e] for e in range(N_EXPERTS)]
    gscore = []
    for g in range(N_GROUPS):
        mem = sel[g * GROUP_SIZE:(g + 1) * GROUP_SIZE]
        best = None
        for a in range(GROUP_SIZE):
            for b in range(a + 1, GROUP_SIZE):
                pair = mem[a] + mem[b]
                best = pair if best is None else jnp.maximum(best, pair)
        gscore.append(best)
    gbest = gscore[0]
    gidx = jnp.zeros_like(gbest)
    for g in range(1, N_GROUPS):
        upd = gscore[g] > gbest
        gidx = jnp.where(upd, float(g), gidx)
        gbest = jnp.where(upd, gscore[g], gbest)
    msel = []
    for j in range(GROUP_SIZE):
        out = sel[j]
        for g in range(1, N_GROUPS):
            out = jnp.where(gidx == float(g), sel[g * GROUP_SIZE + j], out)
        msel.append(out)
    one = jnp.ones_like(gbest)
    zero = jnp.zeros_like(gbest)
    chosen = []
    for j in range(GROUP_SIZE):
        rank = zero
        for k in range(GROUP_SIZE):
            if k < j:
                rank = rank + jnp.where(msel[k] >= msel[j], one, zero)
            elif k > j:
                rank = rank + jnp.where(msel[k] > msel[j], one, zero)
        chosen.append(jnp.where(rank < 2.0, one, zero))
    c0, c1, c2, c3 = chosen
    order = jnp.where(c0 * c1 > 0, 0.0, jnp.where(c1 * c2 > 0, 1.0, jnp.where(c2 * c3 > 0, 2.0,
            jnp.where(c0 * c3 > 0, 3.0, jnp.where(c0 * c2 > 0, 4.0, 5.0)))))
    return gidx * float(N_PAIRS) + order


def _post_kernel(*refs, n_mix, group_tiles):
    n_groups = len(group_tiles)
    per_group = n_mix + 1
    group_refs = [refs[g * per_group:(g + 1) * per_group] for g in range(n_groups)]
    refs = refs[n_groups * per_group:]
    (w_f32_ref, g_ref, sh_ref, sc_ref, lnw_ref, lnb_ref, rwh_ref, rwl_ref, rb_ref, tri_ref,
     x1_ref, h2_ref, info_ref, cout_ref, cnt_ref, pre_ref, w_ref) = refs
    _cast_once(w_f32_ref, w_ref)
    i = pl.program_id(0)

    @pl.when(i == 0)
    def _():
        cnt_ref[...] = jnp.zeros_like(cnt_ref)

    first = 0
    for g in range(n_groups):
        @pl.when(jnp.logical_and(i >= first, i < first + group_tiles[g]))
        def _(g=g):
            out = None
            off = 0
            for m_ref in group_refs[g][:n_mix]:
                width = m_ref.shape[1]
                part = jnp.dot(m_ref[...], w_ref[off:off + width, :], preferred_element_type=F32)
                out = part if out is None else out + part
                off += width
            pre_ref[...] = ALPHA * group_refs[g][n_mix][...] + g_ref[...] * out
        first += group_tiles[g]

    tm = pre_ref.shape[0]
    ts = tri_ref.shape[0]
    base = cnt_ref[:, 0:1]
    for part in range(tm // ts):
        rows = slice(part * ts, (part + 1) * ts)
        x1 = _layer_norm(pre_ref[rows, :], lnw_ref[...], lnb_ref[...])
        x1_ref[rows, :] = x1
        h2 = x1 * (1.0 + sc_ref[...]) + sh_ref[...]
        for j in range(TOK_ROWS):
            h2_ref[pl.ds(part * ts * TOK_ROWS + j, ts, stride=TOK_ROWS), :] = h2[:, j * LANES:(j + 1) * LANES]
        h_hi = h2.astype(BF16)
        h_lo = (h2 - h_hi.astype(F32)).astype(BF16)
        both = jnp.dot(h_hi, rwl_ref[...], preferred_element_type=F32)
        logits = (both[:, :LANES] + both[:, LANES:]
                  + jnp.dot(h_lo, rwh_ref[...], preferred_element_type=F32))
        cls = _route_class(logits.T, rb_ref)
        crow = lax.broadcasted_iota(jnp.int32, (32, ts), 0).astype(F32)
        onehot = jnp.where(crow == cls, 1.0, 0.0)
        prefix = jnp.dot(onehot.astype(BF16), tri_ref[...], preferred_element_type=F32)
        rank = jnp.sum(onehot * (prefix - 1.0 + base), axis=0, keepdims=True)
        base = base + jnp.sum(onehot, axis=1, keepdims=True)
        packed = cls.astype(jnp.int32) * 65536 + rank.astype(jnp.int32)
        info_ref[:, rows] = packed
    cnt_ref[...] = jnp.broadcast_to(base, cnt_ref.shape)

    @pl.when(i == pl.num_programs(0) - 1)
    def _():
        cout_ref[...] = cnt_ref[...]


def _tri(tm):
    return jnp.asarray(np.triu(np.ones((tm, tm), np.float32))).astype(BF16)


def _group_spec(width, tm, first_tile, n_tiles, array_tile_off=0):
    return pl.BlockSpec((tm, width),
                        lambda i, *_: (jnp.clip(i - first_tile, 0, n_tiles - 1) + array_tile_off, 0))


def _post_mix(group_mixes, group_x, w_out_bf16, mod, layer, ln_w, ln_b, rw_hi, rw_lo, router_bias, cond_row, tm):
    group_tiles = tuple(m[0].shape[0] // tm for m in group_mixes)
    n_tiles = sum(group_tiles)
    total_rows = n_tiles * tm
    row = pl.BlockSpec((tm, D_MODEL), lambda i: (i, 0))
    vec = pl.BlockSpec((1, D_MODEL), lambda i: (0, 0))
    cnt = pl.BlockSpec((32, LANES), lambda i: (0, 0))
    in_specs, args = [], []
    first = 0
    for mixes, (x, x_row_off), nt in zip(group_mixes, group_x, group_tiles):
        for m in mixes:
            in_specs.append(_group_spec(m.shape[1], tm, first, nt))
            args.append(m)
        in_specs.append(_group_spec(D_MODEL, tm, first, nt, x_row_off // tm))
        args.append(x)
        first += nt
    in_specs += [
        pl.BlockSpec((D_MODEL, D_MODEL), lambda i: (0, 0), pipeline_mode=pl.Buffered(1)),
        _mod_spec(layer, 2, lambda i: cond_row(i, tm)), _mod_spec(layer, 3, lambda i: cond_row(i, tm)),
        _mod_spec(layer, 4, lambda i: cond_row(i, tm)),
        vec, vec,
        pl.BlockSpec((D_MODEL, LANES), lambda i: (0, 0)),
        pl.BlockSpec((D_MODEL, 2 * LANES), lambda i: (0, 0)),
        pl.BlockSpec(memory_space=pltpu.SMEM),
        pl.BlockSpec((POST_SUB, POST_SUB), lambda i: (0, 0)),
    ]
    args += [w_out_bf16, mod, mod, mod, ln_w.reshape(1, -1), ln_b.reshape(1, -1), rw_hi, rw_lo, router_bias,
             _tri(POST_SUB)]
    return pl.pallas_call(
        functools.partial(_post_kernel, n_mix=len(group_mixes[0]), group_tiles=group_tiles),
        grid=(n_tiles,),
        in_specs=in_specs,
        out_specs=[row, pl.BlockSpec((tm * TOK_ROWS, LANES), lambda i: (i, 0)),
                   pl.BlockSpec((1, tm), lambda i: (0, i)), cnt],
        out_shape=[jax.ShapeDtypeStruct((total_rows, D_MODEL), F32),
                   jax.ShapeDtypeStruct((total_rows * TOK_ROWS, LANES), F32),
                   jax.ShapeDtypeStruct((1, total_rows), jnp.int32), jax.ShapeDtypeStruct((32, LANES), F32)],
        scratch_shapes=[pltpu.VMEM((32, LANES), F32), pltpu.VMEM((tm, D_MODEL), F32),
                        pltpu.VMEM((D_MODEL, D_MODEL), BF16)],
        compiler_params=_cparams(("arbitrary",)),
        name="post_mix",
    )(*args)


def _plan_kernel(cnt_ref, ea_ref, eb_ref, fl_ref, rs_ref, *, n_tiles):
    start = jnp.int32(0)
    prev_a = jnp.int32(-1)
    prev_b = jnp.int32(-1)
    for c in range(N_CLASSES):
        n = cnt_ref[c]
        tiles = lax.shift_right_logical(n + (MOE_TM - 1), MOE_SHIFT)
        row0 = start * MOE_TM
        rs_ref[c] = row0
        g, pr = divmod(c, N_PAIRS)
        a = g * GROUP_SIZE + PAIR_SLOTS[pr][0]
        b = g * GROUP_SIZE + PAIR_SLOTS[pr][1]
        first = 1 + 4 * (prev_a != a).astype(jnp.int32) + 8 * (prev_b != b).astype(jnp.int32)

        def tile_body(k, _, start=start, a=a, b=b, first=first):
            t = start + k
            ea_ref[t] = a
            eb_ref[t] = b
            fl_ref[t] = jnp.where(k == 0, first, 1)
            return 0

        lax.fori_loop(0, tiles, tile_body, 0)
        has = tiles > 0
        prev_a = jnp.where(has, a, prev_a)
        prev_b = jnp.where(has, b, prev_b)
        start = start + tiles
    for c in range(N_CLASSES, 31):
        rs_ref[c] = 0
    rs_ref[31] = start

    def idle_body(t, _):
        ea_ref[t] = prev_a
        eb_ref[t] = prev_b
        fl_ref[t] = 0
        return 0

    lax.fori_loop(start, n_tiles, idle_body, 0)


def _moe_plan(counts, n_tiles):
    smem = pl.BlockSpec(memory_space=pltpu.SMEM)
    i32 = lambda n: jax.ShapeDtypeStruct((n,), jnp.int32)
    return pl.pallas_call(
        functools.partial(_plan_kernel, n_tiles=n_tiles),
        in_specs=[smem],
        out_specs=[smem] * 4,
        out_shape=[i32(n_tiles), i32(n_tiles), i32(n_tiles), i32(32)],
        name="moe_plan",
    )(counts)


def _class_row(packed, rs_ref):
    return rs_ref[lax.shift_right_logical(packed, 16)] + (packed & 0xFFFF)


def _dispatch_kernel(packed_ref, rs_ref, cnt_ref, h_ref, hs_ref, sem, fill_sem, *, tm, n_tiles):
    i = pl.program_id(0)
    used = rs_ref[31]

    def tile_rows(row, n=1):
        return pl.ds(pl.multiple_of(row * TOK_ROWS, TOK_ROWS), n * TOK_ROWS)

    def fill_copy(row, n):
        return pltpu.make_async_copy(h_ref.at[tile_rows(0, n)], hs_ref.at[tile_rows(row, n)], fill_sem)

    def fill(act):
        for c in range(N_CLASSES):
            n = cnt_ref[c]
            row = rs_ref[c] + n
            pad = (-n) & (MOE_TM - 1)
            for bit in reversed(range(MOE_SHIFT)):
                size = 1 << bit

                @pl.when((pad & size) != 0)
                def _(row=row, size=size):
                    act(fill_copy(row, size))
                row = row + (pad & size)
        lax.fori_loop(used, n_tiles, lambda t, _: (act(fill_copy(t * MOE_TM, MOE_TM)), 0)[1], 0)

    @pl.when(i == 0)
    def _():
        fill(lambda cp: cp.start())

    def body(r8, _):
        for j in range(8):
            r = r8 * 8 + j
            row = _class_row(packed_ref[0, i * tm + r], rs_ref)
            pltpu.make_async_copy(h_ref.at[tile_rows(r)], hs_ref.at[tile_rows(row)], sem).start(priority=j % 2)
        return 0

    lax.fori_loop(0, tm // 8, body, 0)
    pltpu.make_async_copy(h_ref, hs_ref.at[tile_rows(0, tm)], sem).wait()

    @pl.when(i == 0)
    def _():
        fill(lambda cp: cp.wait())


def _dispatch(h2_tiles, packed, plan, counts, n_tiles, tm=1024):
    T = packed.shape[1]
    assert tm >= MOE_TM and T % tm == 0
    grid_spec = pltpu.PrefetchScalarGridSpec(
        num_scalar_prefetch=3,
        grid=(T // tm,),
        in_specs=[pl.BlockSpec((tm * TOK_ROWS, LANES), lambda i, *_: (i, 0))],
        out_specs=pl.BlockSpec(memory_space=pl.ANY),
        scratch_shapes=[pltpu.SemaphoreType.DMA(()), pltpu.SemaphoreType.DMA(())],
    )
    return pl.pallas_call(
        functools.partial(_dispatch_kernel, tm=tm, n_tiles=n_tiles),
        grid_spec=grid_spec,
        out_shape=jax.ShapeDtypeStruct((n_tiles * MOE_TM * TOK_ROWS, LANES), F32),
        compiler_params=_cparams(("arbitrary",)),
        name="moe_dispatch",
    )(packed, plan[3], counts, h2_tiles)


def _moe_kernel(ea_ref, eb_ref, fl_ref, h_ref, rwt_ref, wga, wua, wda, wgb, wub, wdb, o_ref,
                sga, sua, sda, sgb, sub, sdb):
    t = pl.program_id(0)
    flags = fl_ref[t]
    valid = (flags & 1) != 0

    @pl.when((flags & 4) != 0)
    def _():
        sga[...] = wga[...].astype(BF16)
        sua[...] = wua[...].astype(BF16)
        sda[...] = wda[...].astype(BF16)

    @pl.when((flags & 8) != 0)
    def _():
        sgb[...] = wgb[...].astype(BF16)
        sub[...] = wub[...].astype(BF16)
        sdb[...] = wdb[...].astype(BF16)

    @pl.when(valid)
    def _():
        h = _load_token_tiles(h_ref, MOE_TM).astype(BF16)
        nt = (((1,), (1,)), ((), ()))
        score = jax.nn.sigmoid(lax.dot_general(h, rwt_ref[...], nt, preferred_element_type=F32))
        lane = lax.broadcasted_iota(jnp.int32, score.shape, 1)
        s_a = jnp.sum(jnp.where(lane == ea_ref[t], score, 0.0), axis=1, keepdims=True)
        s_b = jnp.sum(jnp.where(lane == eb_ref[t], score, 0.0), axis=1, keepdims=True)
        tot = s_a + s_b
        act_a = _silu(jnp.dot(h, sga[...], preferred_element_type=F32)) \
            * jnp.dot(h, sua[...], preferred_element_type=F32) * (s_a / tot)
        act_b = _silu(jnp.dot(h, sgb[...], preferred_element_type=F32)) \
            * jnp.dot(h, sub[...], preferred_element_type=F32) * (s_b / tot)
        _store_token_tiles(o_ref, jnp.dot(act_a.astype(BF16), sda[...], preferred_element_type=F32)
                           + jnp.dot(act_b.astype(BF16), sdb[...], preferred_element_type=F32))

    @pl.when(jnp.logical_not(valid))
    def _():
        o_ref[...] = jnp.zeros_like(o_ref)


def _moe(hs, plan, layer, rwt_bf16, w_gate, w_up, w_down):
    ea, eb, flags, _ = plan
    n_tiles = ea.shape[0]
    wspec = lambda shape, which: pl.BlockSpec(
        (None, None) + shape, (lambda t, ea, eb, fl: (layer, ea[t], 0, 0)) if which == 0
        else (lambda t, ea, eb, fl: (layer, eb[t], 0, 0)))
    up = (D_MODEL, D_FF_EXPERT)
    dn = (D_FF_EXPERT, D_MODEL)
    tile = pl.BlockSpec((MOE_TM * TOK_ROWS, LANES), lambda t, ea, eb, fl: (t, 0))
    grid_spec = pltpu.PrefetchScalarGridSpec(
        num_scalar_prefetch=3,
        grid=(n_tiles,),
        in_specs=[
            tile,
            pl.BlockSpec((N_EXPERTS, D_MODEL), lambda t, ea, eb, fl: (0, 0)),
            wspec(up, 0), wspec(up, 0), wspec(dn, 0), wspec(up, 1), wspec(up, 1), wspec(dn, 1),
        ],
        out_specs=tile,
        scratch_shapes=[pltpu.VMEM(up, BF16), pltpu.VMEM(up, BF16), pltpu.VMEM(dn, BF16),
                        pltpu.VMEM(up, BF16), pltpu.VMEM(up, BF16), pltpu.VMEM(dn, BF16)],
    )
    return pl.pallas_call(
        _moe_kernel,
        grid_spec=grid_spec,
        out_shape=jax.ShapeDtypeStruct((n_tiles * MOE_TM * TOK_ROWS, LANES), F32),
        compiler_params=_cparams(("arbitrary",)),
        name="moe",
    )(ea, eb, flags, hs, rwt_bf16, w_gate, w_up, w_down, w_gate, w_up, w_down)


def _gather_expert_rows(packed_ref, rs_ref, ys_ref, ybuf, sem, tm):
    i = pl.program_id(0)
    n_i = pl.num_programs(0)
    slot = i % 2

    def gather(tile, buf_slot):
        def body(r8, _):
            for j in range(8):
                r = r8 * 8 + j
                d = _class_row(packed_ref[0, tile * tm + r], rs_ref)
                pltpu.make_async_copy(ys_ref.at[pl.ds(pl.multiple_of(d * TOK_ROWS, TOK_ROWS), TOK_ROWS)],
                                      ybuf.at[buf_slot, pl.ds(pl.multiple_of(r * TOK_ROWS, TOK_ROWS), TOK_ROWS)],
                                      sem.at[buf_slot]).start(priority=j % 2)
            return 0
        lax.fori_loop(0, tm // 8, body, 0)

    @pl.when(i == 0)
    def _():
        gather(0, 0)

    @pl.when(i + 1 < n_i)
    def _():
        gather(jnp.minimum(i + 1, n_i - 1), 1 - slot)

    pltpu.make_async_copy(ys_ref.at[pl.ds(0, tm * TOK_ROWS)], ybuf.at[slot], sem.at[slot]).wait()
    return _load_token_tiles(ybuf, tm, slot)


def _fin_proj_kernel(packed_ref, rs_ref, x_ref, ys_ref, g_ref, lnw_ref, lnb_ref, sh_ref, sc_ref, w_f32_ref,
                     x2_ref, u_ref, ybuf, sem, w_ref, *, tm):
    _cast_once(w_f32_ref, w_ref)
    y = _gather_expert_rows(packed_ref, rs_ref, ys_ref, ybuf, sem, tm)
    x2 = _layer_norm(ALPHA * x_ref[...] + g_ref[...] * y, lnw_ref[...], lnb_ref[...])
    x2_ref[...] = x2
    h = (x2 * (1.0 + sc_ref[...]) + sh_ref[...]).astype(BF16)
    for s in range(u_ref.shape[1] // SEC):
        u_ref[:, s * SEC:(s + 1) * SEC] = jnp.dot(h, w_ref[:, s * SEC:(s + 1) * SEC],
                                                  preferred_element_type=F32).astype(BF16)


def _fin_proj(x1, ys, packed, row_start, mod, layer, ln_w, ln_b, w_next, cond_row, tm=512):
    T = x1.shape[0]
    N = w_next.shape[1]
    row = pl.BlockSpec((tm, D_MODEL), lambda i, *_: (i, 0))
    vec = pl.BlockSpec((1, D_MODEL), lambda i, *_: (0, 0))
    crow = lambda i: cond_row(i, tm)
    grid_spec = pltpu.PrefetchScalarGridSpec(
        num_scalar_prefetch=2,
        grid=(T // tm,),
        in_specs=[row, pl.BlockSpec(memory_space=pl.ANY), _mod_spec(layer, 5, crow), vec, vec,
                  _mod_spec(layer + 1, 0, crow), _mod_spec(layer + 1, 1, crow),
                  pl.BlockSpec((D_MODEL, N), lambda i, *_: (0, 0), pipeline_mode=pl.Buffered(1))],
        out_specs=[row, pl.BlockSpec((tm, N), lambda i, *_: (i, 0))],
        scratch_shapes=[pltpu.VMEM((2, tm * TOK_ROWS, LANES), F32), pltpu.SemaphoreType.DMA((2,)),
                        pltpu.VMEM((D_MODEL, N), BF16)],
    )
    return pl.pallas_call(
        functools.partial(_fin_proj_kernel, tm=tm),
        grid_spec=grid_spec,
        out_shape=[jax.ShapeDtypeStruct((T, D_MODEL), F32), jax.ShapeDtypeStruct((T, N), BF16)],
        compiler_params=_cparams(("arbitrary",)),
        name="post_moe_in_proj",
    )(packed, row_start, x1, ys, mod, ln_w.reshape(1, -1), ln_b.reshape(1, -1), mod, mod, w_next)


def _fin_kernel(packed_ref, rs_ref, x_ref, ys_ref, g_ref, lnw_ref, lnb_ref, *rest, tm, group_tiles):
    o_refs = rest[:len(group_tiles)]
    ybuf, sem = rest[len(group_tiles):]
    i = pl.program_id(0)
    y = _layer_norm(ALPHA * x_ref[...] + g_ref[...] * _gather_expert_rows(packed_ref, rs_ref, ys_ref, ybuf, sem, tm),
                    lnw_ref[...], lnb_ref[...])
    first = 0
    for o_ref, nt in zip(o_refs, group_tiles):
        @pl.when(jnp.logical_and(i >= first, i < first + nt))
        def _(o_ref=o_ref):
            o_ref[...] = y
        first += nt


def _fin(x1, ys, packed, row_start, mod, layer, ln_w, ln_b, cond_row, group_rows, tm=256):
    group_tiles = tuple(n // tm for n in group_rows)
    row = pl.BlockSpec((tm, D_MODEL), lambda i, *_: (i, 0))
    vec = pl.BlockSpec((1, D_MODEL), lambda i, *_: (0, 0))
    out_specs, first = [], 0
    for nt in group_tiles:
        out_specs.append(_group_spec(D_MODEL, tm, first, nt))
        first += nt
    grid_spec = pltpu.PrefetchScalarGridSpec(
        num_scalar_prefetch=2,
        grid=(sum(group_tiles),),
        in_specs=[row, pl.BlockSpec(memory_space=pl.ANY), _mod_spec(layer, 5, lambda i: cond_row(i, tm)), vec, vec],
        out_specs=out_specs,
        scratch_shapes=[pltpu.VMEM((2, tm * TOK_ROWS, LANES), F32), pltpu.SemaphoreType.DMA((2,))],
    )
    return pl.pallas_call(
        functools.partial(_fin_kernel, tm=tm, group_tiles=group_tiles),
        grid_spec=grid_spec,
        out_shape=[jax.ShapeDtypeStruct((n, D_MODEL), F32) for n in group_rows],
        compiler_params=_cparams(("arbitrary",)),
        name="post_moe",
    )(packed, row_start, x1, ys, mod, ln_w.reshape(1, -1), ln_b.reshape(1, -1))


def _filt_kernel(z_ref, w1_ref, b1_ref, w2_ref, b2_ref, fr_ref, w3_ref, dl_ref, o_ref):
    i = pl.program_id(0)
    z = z_ref[...]
    fr = fr_ref[...]
    a = jnp.sin(fr * (jnp.dot(z, w1_ref[...], precision=HIGHEST, preferred_element_type=F32) + b1_ref[...]))
    a = jnp.sin(fr * (jnp.dot(a, w2_ref[...], precision=HIGHEST, preferred_element_type=F32) + b2_ref[...]))
    filt = _dot_3pass(a, w3_ref[...])
    window = jnp.exp(-z[:, 0:1] * dl_ref[...]) + HY_SHIFT
    rows = i * z.shape[0] + lax.broadcasted_iota(jnp.int32, filt.shape, 0)
    o_ref[...] = jnp.where(rows == 0, 0.0, filt * window)


def _hyena_taps(seq_len, w1, b1, w2, b2, w3, freq):
    L = seq_len
    t = np.linspace(0.0, 1.0, L)[:, None]
    bands = np.linspace(1e-4, HY_BANDS - 1, HY_BANDS)
    ang = 2.0 * math.pi * bands[None, :] * np.arange(L)[:, None] / L
    z = np.concatenate([t, np.cos(ang), -np.sin(ang)], axis=-1)
    offs = np.minimum(np.abs(np.arange(2 * L) - L), L - 1)
    z2_np = np.zeros((2 * L, LANES), np.float32)
    z2_np[:, :z.shape[1]] = z[offs]
    z2 = jnp.asarray(z2_np)
    pad_c = lambda a: jnp.zeros((a.shape[0], LANES), F32).at[:, :a.shape[1]].set(a)
    pad_r = lambda a: jnp.zeros((LANES, a.shape[1]), F32).at[:a.shape[0], :].set(a)
    w1p = pad_r(pad_c(w1))
    w2p = pad_r(pad_c(w2))
    w3p = pad_r(w3)
    b1p, b2p, frp = pad_c(b1[None, :]), pad_c(b2[None, :]), pad_c(freq[None, :])
    deltas = jnp.asarray(np.abs(np.linspace(math.log(HY_DECAY_TARGET) / HY_SLOW, math.log(HY_DECAY_TARGET) / HY_FAST,
                                            D_MODEL))[None, :].astype(np.float32))
    rb = min(1024, L)
    cbf = D_MODEL
    ncb = D_MODEL // cbf
    nrb_back = L // rb
    sq = pl.BlockSpec((LANES, LANES), lambda i, j: (0, 0))
    vec = pl.BlockSpec((1, LANES), lambda i, j: (0, 0))
    return pl.pallas_call(
        _filt_kernel,
        grid=(2 * L // rb, ncb),
        in_specs=[
            pl.BlockSpec((rb, LANES), lambda i, j: (i, 0)),
            sq, vec, sq, vec, vec,
            pl.BlockSpec((LANES, cbf), lambda i, j: (0, jnp.where(i < nrb_back, ncb + j, j))),
            pl.BlockSpec((1, cbf), lambda i, j: (0, j)),
        ],
        out_specs=pl.BlockSpec((rb, cbf), lambda i, j: (i, j)),
        out_shape=jax.ShapeDtypeStruct((2 * L, D_MODEL), F32),
        compiler_params=_cparams(("arbitrary", "arbitrary")),
        name="hyena_taps",
    )(z2, w1p, b1p, w2p, b2p, frp, w3p, deltas)


def _dft_mats(cb):
    n = 2 * cb
    m = np.arange(cb)
    f = np.arange(cb)
    ang = 2.0 * np.pi * ((f[:, None] * m[None, :]) % n) / n
    fwd = np.concatenate([np.cos(ang), -np.sin(ang)], axis=0)
    fwd[cb, :] = np.where(m % 2 == 0, 1.0, -1.0)
    coef = np.where(f == 0, 1.0, 2.0)[None, :] / n
    inv = np.concatenate([coef * np.cos(ang.T), -coef * np.sin(ang.T)], axis=1)
    inv[:, cb] = np.where(m % 2 == 0, 1.0, -1.0) / n
    return fwd.astype(np.float32), inv.astype(np.float32)


def _hconv_kernel(x0_ref, x1_ref, v_ref, cw0_ref, cw1_ref, cw2_ref, cb0_ref, cb1_ref, cb2_ref,
                  taps_ref, fb_ref, fwd_ref, inv_ref, o_ref,
                  hs_ref, stage_ref, w32_ref, w_ref, x0c_ref, u_ref, y_ref, *, seq_len, blk, n_seq):
    L = seq_len
    nb = L // blk
    cw = x0_ref.shape[1]
    RC = 256
    bi = pl.program_id(1)
    row0 = lax.broadcasted_iota(jnp.int32, (RC, cw), 0) == 0

    @pl.when(bi == 0)
    def _():
        rows = lax.broadcasted_iota(jnp.int32, (2 * blk, 1), 0)
        sign = jnp.where(rows % 2 == 0, 1.0, -1.0)
        real_row = rows <= blk
        prev = None
        for jb in range(2 * nb):
            cur = jnp.dot(fwd_ref[...], taps_ref[jb * blk:(jb + 1) * blk, :].astype(BF16),
                          preferred_element_type=F32)
            if prev is not None:
                first_tap = taps_ref[(jb - 1) * blk:(jb - 1) * blk + 1, :].astype(BF16).astype(F32)
                hs_ref[jb - 1] = cur + sign * (prev - jnp.where(real_row, first_tap, 0.0))
            prev = cur

    zeros8 = jnp.zeros((8, cw), F32)
    for s in range(n_seq):
        r0 = s * L

        def short_conv(k, src_ref, cw_ref, cb_ref, store):
            stage_ref[k, 0:8, :] = zeros8
            stage_ref[k, 8 + L:16 + L, :] = zeros8
            for c in range(L // RC):
                stage_ref[k, 8 + c * RC:8 + (c + 1) * RC, :] = \
                    src_ref[r0 + c * RC:r0 + (c + 1) * RC, :].astype(F32)
            w = cw_ref[...]
            for c in range(L // RC):
                lo = stage_ref[k, 7 + c * RC:7 + (c + 1) * RC, :]
                mid = stage_ref[k, 8 + c * RC:8 + (c + 1) * RC, :]
                hi = stage_ref[k, 9 + c * RC:9 + (c + 1) * RC, :]
                store(c, lo * w[0:1, :] + mid * w[1:2, :] + hi * w[2:3, :] + cb_ref[...])

        def st_x1(c, val):
            w32_ref[c * RC:(c + 1) * RC, :] = val

        def st_v(c, val):
            w_ref[c * RC:(c + 1) * RC, :] = (w32_ref[c * RC:(c + 1) * RC, :] * val).astype(BF16)

        def st_x0(c, val):
            x0c_ref[c * RC:(c + 1) * RC, :] = val.astype(BF16)

        short_conv(0, x1_ref, cw1_ref, cb1_ref, st_x1)
        short_conv(1, v_ref, cw2_ref, cb2_ref, st_v)
        short_conv(2, x0_ref, cw0_ref, cb0_ref, st_x0)

        for j in range(nb):
            u_ref[j] = jnp.dot(fwd_ref[...], w_ref[j * blk:(j + 1) * blk, :], preferred_element_type=F32)

        for i in range(nb):
            for c in range(blk // RC):
                re = None
                im = None
                for j in range(nb):
                    k = i - j + nb - 1
                    a = u_ref[j, c * RC:(c + 1) * RC, :]
                    b = u_ref[j, blk + c * RC:blk + (c + 1) * RC, :]
                    hr = hs_ref[k, c * RC:(c + 1) * RC, :]
                    hi = hs_ref[k, blk + c * RC:blk + (c + 1) * RC, :]
                    bb = b * hi
                    if c == 0:
                        t_re = a * hr - jnp.where(row0, 0.0, bb)
                        t_im = jnp.where(row0, bb, a * hi + b * hr)
                    else:
                        t_re = a * hr - bb
                        t_im = a * hi + b * hr
                    re = t_re if re is None else re + t_re
                    im = t_im if im is None else im + t_im
                y_ref[i, c * RC:(c + 1) * RC, :] = re.astype(BF16)
                y_ref[i, blk + c * RC:blk + (c + 1) * RC, :] = im.astype(BF16)
            conv = jnp.dot(inv_ref[...], y_ref[i], preferred_element_type=F32)
            sl = slice(i * blk, (i + 1) * blk)
            z = x0c_ref[sl, :].astype(F32) * (conv + w_ref[sl, :].astype(F32) * fb_ref[...])
            o_ref[r0 + i * blk:r0 + (i + 1) * blk, :] = z.astype(BF16)


def _hyena_conv(u, conv_w, conv_b, taps, filt_bias, batch, seq_len, blk, n_seq, row_off=0, cw=256):
    T = batch * seq_len
    L = seq_len
    nb = L // blk
    ncw = D_MODEL // cw
    fwd_np, inv_np = _dft_mats(blk)
    fwd = jnp.asarray(fwd_np).astype(BF16)
    inv = jnp.asarray(inv_np).astype(BF16)
    rows = n_seq * L
    assert row_off % rows == 0
    boff = row_off // rows
    sec = lambda s: pl.BlockSpec((rows, cw), lambda c, b: (b + boff, s * ncw + c))
    cws = lambda s: pl.BlockSpec((3, cw), lambda c, b: (0, s * ncw + c))
    cbs = lambda s: pl.BlockSpec((1, cw), lambda c, b: (0, s * ncw + c))
    return pl.pallas_call(
        functools.partial(_hconv_kernel, seq_len=L, blk=blk, n_seq=n_seq),
        grid=(ncw, batch // n_seq),
        in_specs=[sec(0), sec(1), sec(2), cws(0), cws(1), cws(2), cbs(0), cbs(1), cbs(2),
                  pl.BlockSpec((2 * L, cw), lambda c, b: (0, c), pipeline_mode=pl.Buffered(1)),
                  pl.BlockSpec((1, cw), lambda c, b: (0, c)),
                  pl.BlockSpec((2 * blk, blk), lambda c, b: (0, 0), pipeline_mode=pl.Buffered(1)),
                  pl.BlockSpec((blk, 2 * blk), lambda c, b: (0, 0), pipeline_mode=pl.Buffered(1))],
        out_specs=pl.BlockSpec((rows, cw), lambda c, b: (b, c)),
        out_shape=jax.ShapeDtypeStruct((T, D_MODEL), BF16),
        scratch_shapes=[
            pltpu.VMEM((2 * nb - 1, 2 * blk, cw), F32),
            pltpu.VMEM((3, L + 16, cw), F32),
            pltpu.VMEM((L, cw), F32),
            pltpu.VMEM((L, cw), BF16),
            pltpu.VMEM((L, cw), BF16),
            pltpu.VMEM((nb, 2 * blk, cw), F32),
            pltpu.VMEM((nb, 2 * blk, cw), BF16),
        ],
        compiler_params=_cparams(("arbitrary", "arbitrary")),
        name="hyena_conv",
    )(u, u, u, conv_w, conv_w, conv_w, conv_b.reshape(1, -1), conv_b.reshape(1, -1), conv_b.reshape(1, -1),
      taps, filt_bias.reshape(1, -1), fwd, inv)


def kernel(x_prompt, x_sample, cache_diff_k, cache_diff_v, state_ret_fwd, state_ret_bwd, c, c_ctx, ada_w, ada_b, ln_w, ln_b, ev_w_in, ev_w_out, ret_decay_fwd, ret_decay_bwd, ret_gn_w, diff_lambda, diff_subln_w, hy_w_in, hy_conv_w, hy_conv_b, hy_ffn_w1, hy_ffn_b1, hy_ffn_w2, hy_ffn_b2, hy_ffn_w3, hy_freq, hy_filter_bias, hy_w_out, router_w, router_bias, moe_w_gate, moe_w_up, moe_w_down):
    B, S, D = x_prompt.shape
    DB, DS, _ = x_sample.shape
    PAST = cache_diff_k.shape[2]
    TP, TS = B * S, DB * DS
    T_ALL = TP + TS
    assert D == D_MODEL and 1 + DB <= COND_ROWS and T_ALL < 65536

    cond8 = jnp.zeros((COND_ROWS, D), F32).at[0].set(c_ctx).at[1:1 + DB].set(c)
    mod = _ada_mod(cond8, ada_w, ada_b).reshape(DEPTH * COND_ROWS * N_MOD, 1, D)

    tm = 512
    groups = [
        dict(x=x_prompt.reshape(TP, D), x_off=0, batch=B, seq=S, off=0, cond_row=lambda i, tm: 0),
        dict(x=x_sample.reshape(TS, D), x_off=0, batch=DB, seq=DS, off=TP,
             cond_row=lambda i, tm: 1 + (i * tm) // DS),
    ]
    u_next = None

    def cond_row_all(i, tm):
        return jnp.where(i * tm < TP, 0, 1 + (i * tm - TP) // DS)

    rw_pad = jnp.zeros((D, LANES), F32).at[:, :N_EXPERTS].set(router_w.astype(F32))
    rw_hi = rw_pad.astype(BF16)
    rw_lo = jnp.concatenate([rw_hi, (rw_pad - rw_hi.astype(F32)).astype(BF16)], axis=1)
    rwt = router_w.T.astype(BF16)
    rbias = router_bias.astype(F32)
    n_tiles = T_ALL // MOE_TM + N_CLASSES
    outs = {}

    for l in range(DEPTH):
        mixes = []
        if l % 2 == 0:
            e = l // 2
            w_in = ev_w_in[e]
            w_out = ev_w_out[e]
            lg = jnp.stack([jnp.log1p(-jnp.exp2(ret_decay_fwd[e].astype(F32))),
                            jnp.log1p(-jnp.exp2(ret_decay_bwd[e].astype(F32)))])
            lam_init = 0.8 - 0.6 * math.exp(-0.3 * l)
            lq1, lk1, lq2, lk2 = diff_lambda[e].astype(F32)
            lam = (jnp.exp(jnp.sum(lq1 * lk1)) - jnp.exp(jnp.sum(lq2 * lk2)) + lam_init).reshape(1)
            kscale = RET_DK ** -0.5
            for gi, g in enumerate(groups):
                tm_in = tm
                crow = functools.partial(g["cond_row"], tm=tm_in)
                if gi == 0:
                    secs = (("none", 1.0, False), ("none", kscale, False), ("none", 1.0, False),
                            ("none", 1.0, False), ("none", 1.0, False), ("none", 1.0, True),
                            ("none", 1.0, True))
                    proj, kd, vd = _in_proj(g["x"], mod, l, w_in, secs, crow, tm_in, tm=tm_in)
                    outs.setdefault("kd", []).append(kd.reshape(B, 1, S, H_DIFF, 2 * DIFF_QK))
                    outs.setdefault("vd", []).append(vd.reshape(B, 1, S, H_DIFF, DIFF_V))
                    ret, sf, sb = _retention(proj, lg, ret_gn_w[e], g["batch"], g["seq"], write_state=True,
                                             heads=H_RET)
                    outs.setdefault("sf", []).append(sf.reshape(B, 1, H_RET, RET_DK, RET_DV))
                    outs.setdefault("sb", []).append(sb.reshape(B, 1, H_RET, RET_DK, RET_DV))
                    att = _diff_attention(proj, lam, diff_subln_w[e], lam_init, g["batch"], g["seq"],
                                          tq=min(256, g["seq"]), heads=H_DIFF)
                else:
                    secs = (("ret", 1.0, False), ("ret", kscale, False), ("none", 1.0, False),
                            ("none", 1.0, False), ("diff", 1.0, False), ("diff", 1.0, False),
                            ("none", 1.0, False))
                    tabs = {"ret": _rope_tables(g["seq"], RET_DK), "diff": _rope_tables(g["seq"], DIFF_QK)}
                    (proj,) = _in_proj(g["x"], mod, l, w_in, secs, crow, g["seq"], rope_tabs=tabs, tm=tm_in)
                    ret, = _retention(proj, lg, ret_gn_w[e], g["batch"], g["seq"],
                                      states=(state_ret_fwd[:, e], state_ret_bwd[:, e]))
                    ctx = (cache_diff_k[:, e].reshape(DB, PAST, H_DIFF * 2 * DIFF_QK),
                           cache_diff_v[:, e].reshape(DB, PAST, H_DIFF * DIFF_V))
                    att = _diff_attention(proj, lam, diff_subln_w[e], lam_init, g["batch"], g["seq"], ctx=ctx,
                                          tq=512, heads=2)
                mixes.append((ret, att))
        else:
            o = l // 2
            w_in = hy_w_in[o]
            w_out = hy_w_out[o]
            secs = (("none", 1.0, False),) * (3 * D // SEC)
            for gi, g in enumerate(groups):
                if u_next is None:
                    crow = functools.partial(g["cond_row"], tm=tm)
                    (u,) = _in_proj(g["x"], mod, l, w_in, secs, crow, tm, tm=tm)
                    u_off = 0
                else:
                    u, u_off = u_next, g["off"]
                taps = _hyena_taps(g["seq"], hy_ffn_w1[o], hy_ffn_b1[o], hy_ffn_w2[o], hy_ffn_b2[o],
                                   hy_ffn_w3[o], hy_freq[o])
                blk = min(g["seq"], 512)
                n_seq = max(1, 1024 // g["seq"])
                z = _hyena_conv(u, hy_conv_w[o], hy_conv_b[o], taps, hy_filter_bias[o], g["batch"], g["seq"],
                                blk, n_seq, row_off=u_off)
                mixes.append((z,))

        x1_all, h2_all, info_all, counts = _post_mix(mixes, [(g["x"], g["x_off"]) for g in groups], w_out, mod, l, ln_w[l, 0],
                                                     ln_b[l, 0], rw_hi, rw_lo, rbias, cond_row_all, tm)
        packed = info_all
        counts_i = counts[:, 0].astype(jnp.int32)
        plan = _moe_plan(counts_i, n_tiles)
        hs = _dispatch(h2_all, packed, plan, counts_i, n_tiles)
        ys = _moe(hs, plan, l, rwt, moe_w_gate, moe_w_up, moe_w_down)
        if l + 1 < DEPTH and (l + 1) % 2 == 1:
            x_all, u_next = _fin_proj(x1_all, ys, packed, plan[3], mod, l, ln_w[l, 1], ln_b[l, 1],
                                      hy_w_in[(l + 1) // 2], cond_row_all, tm=tm)
            for g in groups:
                g["x"], g["x_off"] = x_all, g["off"]
        else:
            u_next = None
            xs = _fin(x1_all, ys, packed, plan[3], mod, l, ln_w[l, 1], ln_b[l, 1], cond_row_all, (TP, TS))
            for g, x in zip(groups, xs):
                g["x"], g["x_off"] = x, 0

    y_prompt = groups[0]["x"].reshape(B, S, D)
    y_sample = groups[1]["x"].reshape(DB, DS, D)
    cat = lambda xs: xs[0] if len(xs) == 1 else jnp.concatenate(xs, axis=1)
    return (y_prompt, y_sample, cat(outs["kd"]), cat(outs["vd"]), cat(outs["sf"]), cat(outs["sb"]))
```

```python
import functools
import math

import numpy as np
import jax
import jax.numpy as jnp
from jax import lax
from jax.experimental import pallas as pl
from jax.experimental.pallas import tpu as pltpu

F32 = jnp.float32
BF16 = jnp.bfloat16
HIGHEST = lax.Precision.HIGHEST

D_MODEL = 1024
DEPTH = 2
GRID_W = 64
H_RET = 4
RET_DK = 128
RET_DV = 128
RET_CHUNK = 128
H_DIFF = 4
DIFF_QK = 64
DIFF_V = 128
ROPE_BASE = 10000.0
HY_BANDS = 16
HY_FH = 64
HY_DECAY_TARGET = 1e-2
HY_FAST = 0.3
HY_SLOW = 1.5
HY_SHIFT = 0.05
N_EXPERTS = 16
N_GROUPS = 4
GROUP_SIZE = N_EXPERTS // N_GROUPS
D_FF_EXPERT = 512
ALPHA = (2 * DEPTH) ** 0.25
LN_EPS = 1e-5

LANES = 128
SEC = 512
COND_ROWS = 8
N_MOD = 6
VMEM_LIMIT = 50 * 1024 * 1024

PAIR_SLOTS = ((0, 1), (2, 1), (2, 3), (0, 3), (0, 2), (1, 3))
N_PAIRS = len(PAIR_SLOTS)
N_CLASSES = N_GROUPS * N_PAIRS
MOE_TM = 256
MOE_SHIFT = MOE_TM.bit_length() - 1
POST_SUB = 512
TOK_ROWS = D_MODEL // LANES


def _cparams(sem):
    return pltpu.CompilerParams(dimension_semantics=sem, vmem_limit_bytes=VMEM_LIMIT)


def _silu(x):
    return x * jax.nn.sigmoid(x)


def _store_token_tiles(ref, x):
    n = x.shape[0]
    for j in range(TOK_ROWS):
        ref[pl.ds(j, n, stride=TOK_ROWS), :] = x[:, j * LANES:(j + 1) * LANES]


def _load_token_tiles(ref, n, slot=None):
    idx = () if slot is None else (slot,)
    return jnp.concatenate([ref[idx + (pl.ds(j, n, stride=TOK_ROWS), slice(None))] for j in range(TOK_ROWS)],
                           axis=1)


def _layer_norm(x, w, b):
    mu = jnp.mean(x, axis=-1, keepdims=True)
    xc = x - mu
    var = jnp.mean(xc * xc, axis=-1, keepdims=True)
    return xc * lax.rsqrt(var + LN_EPS) * w + b


def _dot_3pass(a, w):
    a_hi = a.astype(BF16)
    a_lo = (a - a_hi.astype(F32)).astype(BF16)
    w_hi = w.astype(BF16)
    w_lo = (w - w_hi.astype(F32)).astype(BF16)
    return (jnp.dot(a_hi, w_hi, preferred_element_type=F32) + jnp.dot(a_lo, w_hi, preferred_element_type=F32)
            + jnp.dot(a_hi, w_lo, preferred_element_type=F32))


def _ada_kernel(c_ref, w_ref, b_ref, o_ref):
    o_ref[...] = _dot_3pass(_silu(c_ref[...]), w_ref[...]) + b_ref[...]


def _ada_mod(cond8, ada_w, ada_b):
    tn = 1024
    nj = ada_w.shape[2] // tn
    return pl.pallas_call(
        _ada_kernel,
        grid=(DEPTH, nj),
        in_specs=[
            pl.BlockSpec((COND_ROWS, D_MODEL), lambda l, j: (0, 0)),
            pl.BlockSpec((None, D_MODEL, tn), lambda l, j: (l, 0, j)),
            pl.BlockSpec((None, 1, tn), lambda l, j: (l, 0, j)),
        ],
        out_specs=pl.BlockSpec((None, COND_ROWS, tn), lambda l, j: (l, 0, j)),
        out_shape=jax.ShapeDtypeStruct((DEPTH, COND_ROWS, ada_w.shape[2]), F32),
        compiler_params=_cparams(("arbitrary", "arbitrary")),
        name="ada_mod",
    )(cond8, ada_w, ada_b.reshape(DEPTH, 1, -1))


def _mod_spec(layer, chunk, row_of_tile):
    def imap(i, *_):
        return ((layer * COND_ROWS + row_of_tile(i)) * N_MOD + chunk, 0, 0)
    return pl.BlockSpec((None, 1, D_MODEL), imap)


def _rope(a, tabs, quarter):
    c, sa, sb = tabs
    out = []
    for hb in range(a.shape[1] // LANES):
        blk = a[:, hb * LANES:(hb + 1) * LANES]
        up = pltpu.roll(blk, LANES - quarter, axis=1)
        dn = pltpu.roll(blk, quarter, axis=1)
        out.append(blk * c + up * sa + dn * sb)
    return jnp.concatenate(out, axis=1)


def _cast_once(w_ref, w_bf16_ref):
    @pl.when(pl.program_id(0) == 0)
    def _():
        for c in range(0, w_ref.shape[1], SEC):
            w_bf16_ref[:, c:c + SEC] = w_ref[:, c:c + SEC].astype(BF16)


def _in_kernel(*refs, secs, n_f32_out):
    x_ref, sh_ref, sc_ref, w_ref = refs[:4]
    pos = 4
    tabs = {}
    for kind in ("ret", "diff"):
        if any(s[0] == kind for s in secs):
            tabs[kind] = tuple(r[...] for r in refs[pos:pos + 3])
            pos += 3
    o_ref = refs[pos]
    f32_refs = refs[pos + 1:pos + 1 + n_f32_out]
    w_bf16_ref = refs[-1]
    _cast_once(w_ref, w_bf16_ref)
    h = (x_ref[...] * (1.0 + sc_ref[...]) + sh_ref[...]).astype(BF16)
    k32 = 0
    for s, (kind, scale, want_f32) in enumerate(secs):
        acc = jnp.dot(h, w_bf16_ref[:, s * SEC:(s + 1) * SEC], preferred_element_type=F32)
        if scale != 1.0:
            acc = acc * scale
        if kind == "ret":
            acc = _rope(acc, tabs["ret"], RET_DK // 4)
        elif kind == "diff":
            acc = _rope(acc, tabs["diff"], DIFF_QK // 4)
        o_ref[:, s * SEC:(s + 1) * SEC] = acc.astype(BF16)
        if want_f32:
            for hb in range(SEC // LANES):
                f32_refs[k32][:, hb, :] = acc[:, hb * LANES:(hb + 1) * LANES]
            k32 += 1
    assert k32 == n_f32_out


def _in_proj(x2d, mod, layer, w_f32, secs, row_of_tile, seq_len, rope_tabs=None, tm=512):
    T = x2d.shape[0]
    N = w_f32.shape[1]
    assert N == SEC * len(secs) and T % tm == 0 and seq_len % tm == 0
    tiles_per_seq = seq_len // tm
    in_specs = [
        pl.BlockSpec((tm, D_MODEL), lambda i: (i, 0)),
        _mod_spec(layer, 0, row_of_tile),
        _mod_spec(layer, 1, row_of_tile),
        pl.BlockSpec((D_MODEL, N), lambda i: (0, 0), pipeline_mode=pl.Buffered(1)),
    ]
    args = [x2d, mod, mod, w_f32]
    for kind in ("ret", "diff"):
        if any(s[0] == kind for s in secs):
            for t in rope_tabs[kind]:
                in_specs.append(pl.BlockSpec((tm, LANES), lambda i: (i % tiles_per_seq, 0)))
                args.append(t)
    n_f32 = sum(1 for s in secs if s[2])
    heads = SEC // LANES
    out_shape = [jax.ShapeDtypeStruct((T, N), BF16)] + [jax.ShapeDtypeStruct((T, heads, LANES), F32)] * n_f32
    out_specs = ([pl.BlockSpec((tm, N), lambda i: (i, 0))]
                 + [pl.BlockSpec((tm, heads, LANES), lambda i: (i, 0, 0))] * n_f32)
    return pl.pallas_call(
        functools.partial(_in_kernel, secs=secs, n_f32_out=n_f32),
        grid=(T // tm,),
        in_specs=in_specs,
        out_specs=out_specs,
        out_shape=out_shape,
        scratch_shapes=[pltpu.VMEM((D_MODEL, N), BF16)],
        compiler_params=_cparams(("arbitrary",)),
        name="in_proj",
    )(*args)


def _rope_tables(seq_len, d):
    half = d // 2
    quarter = half // 2
    t = np.arange(seq_len)
    inv = ROPE_BASE ** (-np.arange(quarter, dtype=np.float64) / quarter)
    ang_r = (t // GRID_W)[:, None] * inv[None, :]
    ang_c = (t % GRID_W)[:, None] * inv[None, :]
    zero = np.zeros_like(ang_r)
    cos = np.concatenate([np.cos(ang_r)] * 2 + [np.cos(ang_c)] * 2, axis=1)
    sa = np.concatenate([-np.sin(ang_r), zero, -np.sin(ang_c), zero], axis=1)
    sb = np.concatenate([zero, np.sin(ang_r), zero, np.sin(ang_c)], axis=1)
    reps = LANES // d
    return tuple(jnp.asarray(np.tile(a, (1, reps)).astype(np.float32)) for a in (cos, sa, sb))


def _ret_kernel(*refs, n_chunks, has_state, write_state, heads):
    lg_ref, q_ref, k_ref, v_ref, g_ref, gnw_ref = refs[:6]
    pos = 6
    if has_state:
        s0f_ref, s0b_ref = refs[pos:pos + 2]
        pos += 2
    o_ref = refs[pos]
    pos += 1
    if write_state:
        sf_ref, sb_ref = refs[pos:pos + 2]
        pos += 2
    acc_ref, kv_ref = refs[pos:pos + 2]
    for hh in range(heads):
        _ret_head(hh, heads, lg_ref, q_ref, k_ref, v_ref, g_ref, gnw_ref,
                  (s0f_ref, s0b_ref) if has_state else None, o_ref, (sf_ref, sb_ref) if write_state else None,
                  acc_ref, kv_ref, n_chunks)


def _ret_head(hh, heads, lg_ref, q_ref, k_ref, v_ref, g_ref, gnw_ref, s0_refs, o_ref, s_out_refs, acc_ref, kv_ref,
              n_chunks):
    C = RET_CHUNK
    cols = slice(hh * LANES, (hh + 1) * LANES)
    hd = pl.program_id(1) * heads + hh
    lgf = lg_ref[0, hd]
    lgb = lg_ref[1, hd]
    ii = lax.broadcasted_iota(jnp.int32, (C, C), 0).astype(F32)
    jj = lax.broadcasted_iota(jnp.int32, (C, C), 1).astype(F32)
    rel = ii - jj
    d_f = jnp.where(rel >= 0, jnp.exp(jnp.maximum(rel, 0.0) * lgf), 0.0)
    d_b = jnp.where(rel <= 0, jnp.exp(jnp.maximum(-rel, 0.0) * lgb), 0.0)
    d_sum = d_f + d_b
    idx = lax.broadcasted_iota(jnp.int32, (C, 1), 0).astype(F32)
    xi_f = jnp.exp((idx + 1.0) * lgf)
    zeta_f = jnp.exp((C - 1.0 - idx) * lgf)
    xi_b = jnp.exp((C - idx) * lgb)
    zeta_b = jnp.exp(idx * lgb)
    one = jnp.ones((1, 1), F32)
    gc_f = jnp.exp(one * (C * lgf))
    gc_b = jnp.exp(one * (C * lgb))

    nt = (((1,), (1,)), ((), ()))
    tn = (((0,), (0,)), ((), ()))

    if s0_refs is not None:
        s_f = s0_refs[0][hh]
        s_b = s0_refs[1][hh]
    else:
        s_f = jnp.zeros((RET_DK, RET_DV), F32)
        s_b = jnp.zeros((RET_DK, RET_DV), F32)

    for n in range(n_chunks):
        sl = slice(n * C, (n + 1) * C)
        qc, kc, vc = q_ref[sl, cols], k_ref[sl, cols], v_ref[sl, cols]
        scores = lax.dot_general(qc, kc, nt, preferred_element_type=F32) * d_sum
        acc_ref[sl, cols] = jnp.dot(scores.astype(BF16), vc, preferred_element_type=F32)
        kf = kc.astype(F32)
        kv_ref[0, hh, n] = lax.dot_general((kf * zeta_f).astype(BF16), vc, tn, preferred_element_type=F32)
        kv_ref[1, hh, n] = lax.dot_general((kf * zeta_b).astype(BF16), vc, tn, preferred_element_type=F32)

    for n in range(n_chunks):
        sl = slice(n * C, (n + 1) * C)
        qf = q_ref[sl, cols].astype(F32)
        acc_ref[sl, cols] += jnp.dot((qf * xi_f).astype(BF16), s_f.astype(BF16), preferred_element_type=F32)
        s_f = gc_f * s_f + kv_ref[0, hh, n]

    gnw = gnw_ref[:, cols]
    for n in reversed(range(n_chunks)):
        sl = slice(n * C, (n + 1) * C)
        qf = q_ref[sl, cols].astype(F32)
        cross = jnp.dot((qf * xi_b).astype(BF16), s_b.astype(BF16), preferred_element_type=F32)
        r = acc_ref[sl, cols] + cross
        mu = jnp.mean(r, axis=-1, keepdims=True)
        rc = r - mu
        var = jnp.mean(rc * rc, axis=-1, keepdims=True)
        rn = rc * lax.rsqrt(var + LN_EPS) * gnw
        o_ref[sl, cols] = (_silu(g_ref[sl, cols].astype(F32)) * rn).astype(BF16)
        s_b = gc_b * s_b + kv_ref[1, hh, n]

    if s_out_refs is not None:
        s_out_refs[0][hh] = s_f
        s_out_refs[1][hh] = s_b


def _retention(proj, lg, gn_w, batch, seq_len, states=None, write_state=False, heads=2):
    T = proj.shape[0]
    hp = heads
    width = hp * LANES
    per_sec = SEC // width
    blk = lambda sec: pl.BlockSpec((seq_len, width), lambda b, h: (b, sec * per_sec + h))
    in_specs = [pl.BlockSpec(memory_space=pltpu.SMEM), blk(0), blk(1), blk(2), blk(3),
                pl.BlockSpec((1, width), lambda b, h: (0, h))]
    args = [lg, proj, proj, proj, proj, gn_w.reshape(1, -1)]
    if states is not None:
        st = pl.BlockSpec((None, hp, RET_DK, RET_DV), lambda b, h: (b, h, 0, 0))
        in_specs += [st, st]
        args += list(states)
    out_shape = [jax.ShapeDtypeStruct((T, SEC), BF16)]
    out_specs = [pl.BlockSpec((seq_len, width), lambda b, h: (b, h))]
    if write_state:
        st_o = pl.BlockSpec((None, hp, RET_DK, RET_DV), lambda b, h: (b, h, 0, 0))
        out_shape += [jax.ShapeDtypeStruct((batch, H_RET, RET_DK, RET_DV), F32)] * 2
        out_specs += [st_o, st_o]
    return pl.pallas_call(
        functools.partial(_ret_kernel, n_chunks=seq_len // RET_CHUNK, has_state=states is not None,
                          write_state=write_state, heads=hp),
        grid=(batch, H_RET // hp),
        in_specs=in_specs,
        out_specs=out_specs,
        out_shape=out_shape,
        scratch_shapes=[pltpu.VMEM((seq_len, width), F32),
                        pltpu.VMEM((2, hp, seq_len // RET_CHUNK, RET_DK, RET_DV), F32)],
        compiler_params=_cparams(("arbitrary", "arbitrary")),
        name="retention",
    )(*args)


def _att_kernel(*refs, has_ctx, out_scale, key_chunk, heads):
    lam_ref, q_ref, k_ref, v_ref = refs[:4]
    pos = 4
    if has_ctx:
        ck_ref, cv_ref = refs[pos:pos + 2]
        pos += 2
    w_ref, o_ref = refs[pos:pos + 2]
    lam = lam_ref[0]
    tq = q_ref.shape[0]
    nt = (((1,), (1,)), ((), ()))
    for hh in range(heads):
        cols = slice(hh * LANES, (hh + 1) * LANES)
        q = q_ref[:, cols]
        lane = lax.broadcasted_iota(jnp.int32, q.shape, 1)
        zero = jnp.zeros_like(q)
        qq = jnp.concatenate([jnp.where(lane < DIFF_QK, q, zero), jnp.where(lane >= DIFF_QK, q, zero)], axis=0)
        qq = qq * jnp.asarray(DIFF_QK ** -0.5, BF16)
        chunks = [(k_ref, v_ref, c * key_chunk, key_chunk) for c in range(k_ref.shape[0] // key_chunk)]
        if has_ctx:
            chunks.append((ck_ref, cv_ref, 0, ck_ref.shape[0]))
        m = l = acc = None
        for kr, vr, off, n in chunks:
            kch = kr[off:off + n, cols].astype(BF16)
            vch = vr[off:off + n, cols].astype(BF16)
            s = lax.dot_general(qq, kch, nt, preferred_element_type=F32)
            cm = jnp.max(s, axis=-1, keepdims=True)
            m_new = cm if m is None else jnp.maximum(m, cm)
            p = jnp.exp(s - m_new)
            ps = jnp.sum(p, axis=-1, keepdims=True)
            pv = jnp.dot(p.astype(BF16), vch, preferred_element_type=F32)
            if m is None:
                l, acc = ps, pv
            else:
                alpha = jnp.exp(m - m_new)
                l = alpha * l + ps
                acc = alpha * acc + pv
            m = m_new
        o = acc / l
        att = o[:tq] - lam * o[tq:]
        att = att * lax.rsqrt(jnp.mean(att * att, axis=-1, keepdims=True) + LN_EPS)
        o_ref[:, cols] = (att * w_ref[...] * out_scale).astype(BF16)


def _diff_attention(proj, lam, subln_w, lam_init, batch, seq_len, ctx=None, tq=256, heads=1):
    T = proj.shape[0]
    width = heads * LANES
    per_sec = SEC // width
    nq = seq_len // tq
    in_specs = [
        pl.BlockSpec(memory_space=pltpu.SMEM),
        pl.BlockSpec((tq, width), lambda b, h, i: (b * nq + i, 4 * per_sec + h)),
        pl.BlockSpec((seq_len, width), lambda b, h, i: (b, 5 * per_sec + h)),
        pl.BlockSpec((seq_len, width), lambda b, h, i: (b, 6 * per_sec + h)),
    ]
    args = [lam, proj, proj, proj]
    if ctx is not None:
        ck, cv = ctx
        past = ck.shape[1]
        cspec = pl.BlockSpec((None, past, width), lambda b, h, i: (b, 0, h))
        in_specs += [cspec, cspec]
        args += [ck, cv]
    in_specs.append(pl.BlockSpec((1, LANES), lambda b, h, i: (0, 0)))
    args.append(subln_w.reshape(1, -1))
    return pl.pallas_call(
        functools.partial(_att_kernel, has_ctx=ctx is not None, out_scale=1.0 - lam_init,
                          key_chunk=min(512, seq_len), heads=heads),
        grid=(batch, H_DIFF // heads, nq),
        in_specs=in_specs,
        out_specs=pl.BlockSpec((tq, width), lambda b, h, i: (b * nq + i, h)),
        out_shape=jax.ShapeDtypeStruct((T, SEC), BF16),
        compiler_params=_cparams(("arbitrary", "arbitrary", "arbitrary")),
        name="diff_attention",
    )(*args)


def _route_class(lt, rb_ref):
    sel = [jax.nn.sigmoid(lt[e:e + 1, :]) + rb_ref[e] for e in range(N_EXPERTS)]
    gscore = []
    for g in range(N_GROUPS):
        mem = sel[g * GROUP_SIZE:(g + 1) * GROUP_SIZE]
        best = None
        for a in range(GROUP_SIZE):
            for b in range(a + 1, GROUP_SIZE):
                pair = mem[a] + mem[b]
                best = pair if best is None else jnp.maximum(best, pair)
        gscore.append(best)
    gbest = gscore[0]
    gidx = jnp.zeros_like(gbest)
    for g in range(1, N_GROUPS):
        upd = gscore[g] > gbest
        gidx = jnp.where(upd, float(g), gidx)
        gbest = jnp.where(upd, gscore[g], gbest)
    msel = []
    for j in range(GROUP_SIZE):
        out = sel[j]
        for g in range(1, N_GROUPS):
            out = jnp.where(gidx == float(g), sel[g * GROUP_SIZE + j], out)
        msel.append(out)
    one = jnp.ones_like(gbest)
    zero = jnp.zeros_like(gbest)
    chosen = []
    for j in range(GROUP_SIZE):
        rank = zero
        for k in range(GROUP_SIZE):
            if k < j:
                rank = rank + jnp.where(msel[k] >= msel[j], one, zero)
            elif k > j:
                rank = rank + jnp.where(msel[k] > msel[j], one, zero)
        chosen.append(jnp.where(rank < 2.0, one, zero))
    c0, c1, c2, c3 = chosen
    order = jnp.where(c0 * c1 > 0, 0.0, jnp.where(c1 * c2 > 0, 1.0, jnp.where(c2 * c3 > 0, 2.0,
            jnp.where(c0 * c3 > 0, 3.0, jnp.where(c0 * c2 > 0, 4.0, 5.0)))))
    return gidx * float(N_PAIRS) + order


def _post_kernel(*refs, n_mix, group_tiles):
    n_groups = len(group_tiles)
    per_group = n_mix + 1
    group_refs = [refs[g * per_group:(g + 1) * per_group] for g in range(n_groups)]
    refs = refs[n_groups * per_group:]
    (w_f32_ref, g_ref, sh_ref, sc_ref, lnw_ref, lnb_ref, rwh_ref, rwl_ref, rb_ref, tri_ref,
     x1_ref, h2_ref, info_ref, cout_ref, cnt_ref, pre_ref, w_ref) = refs
    _cast_once(w_f32_ref, w_ref)
    i = pl.program_id(0)

    @pl.when(i == 0)
    def _():
        cnt_ref[...] = jnp.zeros_like(cnt_ref)

    first = 0
    for g in range(n_groups):
        @pl.when(jnp.logical_and(i >= first, i < first + group_tiles[g]))
        def _(g=g):
            out = None
            off = 0
            for m_ref in group_refs[g][:n_mix]:
                width = m_ref.shape[1]
                part = jnp.dot(m_ref[...], w_ref[off:off + width, :], preferred_element_type=F32)
                out = part if out is None else out + part
                off += width
            pre_ref[...] = ALPHA * group_refs[g][n_mix][...] + g_ref[...] * out
        first += group_tiles[g]

    tm = pre_ref.shape[0]
    ts = tri_ref.shape[0]
    base = cnt_ref[:, 0:1]
    for part in range(tm // ts):
        rows = slice(part * ts, (part + 1) * ts)
        x1 = _layer_norm(pre_ref[rows, :], lnw_ref[...], lnb_ref[...])
        x1_ref[rows, :] = x1
        h2 = x1 * (1.0 + sc_ref[...]) + sh_ref[...]
        for j in range(TOK_ROWS):
            h2_ref[pl.ds(part * ts * TOK_ROWS + j, ts, stride=TOK_ROWS), :] = h2[:, j * LANES:(j + 1) * LANES]
        h_hi = h2.astype(BF16)
        h_lo = (h2 - h_hi.astype(F32)).astype(BF16)
        both = jnp.dot(h_hi, rwl_ref[...], preferred_element_type=F32)
        logits = (both[:, :LANES] + both[:, LANES:]
                  + jnp.dot(h_lo, rwh_ref[...], preferred_element_type=F32))
        cls = _route_class(logits.T, rb_ref)
        crow = lax.broadcasted_iota(jnp.int32, (32, ts), 0).astype(F32)
        onehot = jnp.where(crow == cls, 1.0, 0.0)
        prefix = jnp.dot(onehot.astype(BF16), tri_ref[...], preferred_element_type=F32)
        rank = jnp.sum(onehot * (prefix - 1.0 + base), axis=0, keepdims=True)
        base = base + jnp.sum(onehot, axis=1, keepdims=True)
        packed = cls.astype(jnp.int32) * 65536 + rank.astype(jnp.int32)
        info_ref[:, rows] = packed
    cnt_ref[...] = jnp.broadcast_to(base, cnt_ref.shape)

    @pl.when(i == pl.num_programs(0) - 1)
    def _():
        cout_ref[...] = cnt_ref[...]


def _tri(tm):
    return jnp.asarray(np.triu(np.ones((tm, tm), np.float32))).astype(BF16)


def _group_spec(width, tm, first_tile, n_tiles, array_tile_off=0):
    return pl.BlockSpec((tm, width),
                        lambda i, *_: (jnp.clip(i - first_tile, 0, n_tiles - 1) + array_tile_off, 0))


def _post_mix(group_mixes, group_x, w_out_bf16, mod, layer, ln_w, ln_b, rw_hi, rw_lo, router_bias, cond_row, tm):
    group_tiles = tuple(m[0].shape[0] // tm for m in group_mixes)
    n_tiles = sum(group_tiles)
    total_rows = n_tiles * tm
    row = pl.BlockSpec((tm, D_MODEL), lambda i: (i, 0))
    vec = pl.BlockSpec((1, D_MODEL), lambda i: (0, 0))
    cnt = pl.BlockSpec((32, LANES), lambda i: (0, 0))
    in_specs, args = [], []
    first = 0
    for mixes, (x, x_row_off), nt in zip(group_mixes, group_x, group_tiles):
        for m in mixes:
            in_specs.append(_group_spec(m.shape[1], tm, first, nt))
            args.append(m)
        in_specs.append(_group_spec(D_MODEL, tm, first, nt, x_row_off // tm))
        args.append(x)
        first += nt
    in_specs += [
        pl.BlockSpec((D_MODEL, D_MODEL), lambda i: (0, 0), pipeline_mode=pl.Buffered(1)),
        _mod_spec(layer, 2, lambda i: cond_row(i, tm)), _mod_spec(layer, 3, lambda i: cond_row(i, tm)),
        _mod_spec(layer, 4, lambda i: cond_row(i, tm)),
        vec, vec,
        pl.BlockSpec((D_MODEL, LANES), lambda i: (0, 0)),
        pl.BlockSpec((D_MODEL, 2 * LANES), lambda i: (0, 0)),
        pl.BlockSpec(memory_space=pltpu.SMEM),
        pl.BlockSpec((POST_SUB, POST_SUB), lambda i: (0, 0)),
    ]
    args += [w_out_bf16, mod, mod, mod, ln_w.reshape(1, -1), ln_b.reshape(1, -1), rw_hi, rw_lo, router_bias,
             _tri(POST_SUB)]
    return pl.pallas_call(
        functools.partial(_post_kernel, n_mix=len(group_mixes[0]), group_tiles=group_tiles),
        grid=(n_tiles,),
        in_specs=in_specs,
        out_specs=[row, pl.BlockSpec((tm * TOK_ROWS, LANES), lambda i: (i, 0)),
                   pl.BlockSpec((1, tm), lambda i: (0, i)), cnt],
        out_shape=[jax.ShapeDtypeStruct((total_rows, D_MODEL), F32),
                   jax.ShapeDtypeStruct((total_rows * TOK_ROWS, LANES), F32),
                   jax.ShapeDtypeStruct((1, total_rows), jnp.int32), jax.ShapeDtypeStruct((32, LANES), F32)],
        scratch_shapes=[pltpu.VMEM((32, LANES), F32), pltpu.VMEM((tm, D_MODEL), F32),
                        pltpu.VMEM((D_MODEL, D_MODEL), BF16)],
        compiler_params=_cparams(("arbitrary",)),
        name="post_mix",
    )(*args)


def _plan_kernel(cnt_ref, ea_ref, eb_ref, fl_ref, rs_ref, *, n_tiles):
    start = jnp.int32(0)
    prev_a = jnp.int32(-1)
    prev_b = jnp.int32(-1)
    for c in range(N_CLASSES):
        n = cnt_ref[c]
        tiles = lax.shift_right_logical(n + (MOE_TM - 1), MOE_SHIFT)
        row0 = start * MOE_TM
        rs_ref[c] = row0
        g, pr = divmod(c, N_PAIRS)
        a = g * GROUP_SIZE + PAIR_SLOTS[pr][0]
        b = g * GROUP_SIZE + PAIR_SLOTS[pr][1]
        first = 1 + 4 * (prev_a != a).astype(jnp.int32) + 8 * (prev_b != b).astype(jnp.int32)

        def tile_body(k, _, start=start, a=a, b=b, first=first, n=n):
            t = start + k
            ea_ref[t] = a
            eb_ref[t] = b
            half = 2 * (n - k * MOE_TM <= MOE_TM // 2).astype(jnp.int32)
            fl_ref[t] = jnp.where(k == 0, first, 1) + half
            return 0

        lax.fori_loop(0, tiles, tile_body, 0)
        has = tiles > 0
        prev_a = jnp.where(has, a, prev_a)
        prev_b = jnp.where(has, b, prev_b)
        start = start + tiles
    for c in range(N_CLASSES, 31):
        rs_ref[c] = 0
    rs_ref[31] = start

    def idle_body(t, _):
        ea_ref[t] = prev_a
        eb_ref[t] = prev_b
        fl_ref[t] = 0
        return 0

    lax.fori_loop(start, n_tiles, idle_body, 0)


def _moe_plan(counts, n_tiles):
    smem = pl.BlockSpec(memory_space=pltpu.SMEM)
    i32 = lambda n: jax.ShapeDtypeStruct((n,), jnp.int32)
    return pl.pallas_call(
        functools.partial(_plan_kernel, n_tiles=n_tiles),
        in_specs=[smem],
        out_specs=[smem] * 4,
        out_shape=[i32(n_tiles), i32(n_tiles), i32(n_tiles), i32(32)],
        name="moe_plan",
    )(counts)


def _class_row(packed, rs_ref):
    return rs_ref[lax.shift_right_logical(packed, 16)] + (packed & 0xFFFF)


def _dispatch_kernel(packed_ref, rs_ref, cnt_ref, h_ref, hs_ref, sem, fill_sem, *, tm, n_tiles):
    i = pl.program_id(0)
    used = rs_ref[31]

    def tile_rows(row, n=1):
        return pl.ds(pl.multiple_of(row * TOK_ROWS, TOK_ROWS), n * TOK_ROWS)

    def fill_copy(row, n):
        return pltpu.make_async_copy(h_ref.at[tile_rows(0, n)], hs_ref.at[tile_rows(row, n)], fill_sem)

    def fill(act):
        for c in range(N_CLASSES):
            n = cnt_ref[c]
            row = rs_ref[c] + n
            pad = (-n) & (MOE_TM - 1)
            for bit in reversed(range(MOE_SHIFT)):
                size = 1 << bit

                @pl.when((pad & size) != 0)
                def _(row=row, size=size):
                    act(fill_copy(row, size))
                row = row + (pad & size)
        lax.fori_loop(used, n_tiles, lambda t, _: (act(fill_copy(t * MOE_TM, MOE_TM)), 0)[1], 0)

    @pl.when(i == 0)
    def _():
        fill(lambda cp: cp.start())

    def body(r8, _):
        for j in range(8):
            r = r8 * 8 + j
            row = _class_row(packed_ref[0, i * tm + r], rs_ref)
            pltpu.make_async_copy(h_ref.at[tile_rows(r)], hs_ref.at[tile_rows(row)], sem).start(priority=j % 2)
        return 0

    lax.fori_loop(0, tm // 8, body, 0)
    pltpu.make_async_copy(h_ref, hs_ref.at[tile_rows(0, tm)], sem).wait()

    @pl.when(i == 0)
    def _():
        fill(lambda cp: cp.wait())


def _dispatch(h2_tiles, packed, plan, counts, n_tiles, tm=1024):
    T = packed.shape[1]
    assert tm >= MOE_TM and T % tm == 0
    grid_spec = pltpu.PrefetchScalarGridSpec(
        num_scalar_prefetch=3,
        grid=(T // tm,),
        in_specs=[pl.BlockSpec((tm * TOK_ROWS, LANES), lambda i, *_: (i, 0))],
        out_specs=pl.BlockSpec(memory_space=pl.ANY),
        scratch_shapes=[pltpu.SemaphoreType.DMA(()), pltpu.SemaphoreType.DMA(())],
    )
    return pl.pallas_call(
        functools.partial(_dispatch_kernel, tm=tm, n_tiles=n_tiles),
        grid_spec=grid_spec,
        out_shape=jax.ShapeDtypeStruct((n_tiles * MOE_TM * TOK_ROWS, LANES), F32),
        compiler_params=_cparams(("arbitrary",)),
        name="moe_dispatch",
    )(packed, plan[3], counts, h2_tiles)


def _moe_kernel(ea_ref, eb_ref, fl_ref, h_ref, rwt_ref, wga, wua, wda, wgb, wub, wdb, o_ref,
                sga, sua, sda, sgb, sub, sdb):
    t = pl.program_id(0)
    flags = fl_ref[t]
    valid = (flags & 1) != 0

    @pl.when((flags & 4) != 0)
    def _():
        sga[...] = wga[...].astype(BF16)
        sua[...] = wua[...].astype(BF16)
        sda[...] = wda[...].astype(BF16)

    @pl.when((flags & 8) != 0)
    def _():
        sgb[...] = wgb[...].astype(BF16)
        sub[...] = wub[...].astype(BF16)
        sdb[...] = wdb[...].astype(BF16)

    def experts(rows):
        h = _load_token_tiles(h_ref, rows).astype(BF16)
        nt = (((1,), (1,)), ((), ()))
        score = jax.nn.sigmoid(lax.dot_general(h, rwt_ref[...], nt, preferred_element_type=F32))
        lane = lax.broadcasted_iota(jnp.int32, score.shape, 1)
        s_a = jnp.sum(jnp.where(lane == ea_ref[t], score, 0.0), axis=1, keepdims=True)
        s_b = jnp.sum(jnp.where(lane == eb_ref[t], score, 0.0), axis=1, keepdims=True)
        tot = s_a + s_b
        act_a = _silu(jnp.dot(h, sga[...], preferred_element_type=F32)) \
            * jnp.dot(h, sua[...], preferred_element_type=F32) * (s_a / tot)
        act_b = _silu(jnp.dot(h, sgb[...], preferred_element_type=F32)) \
            * jnp.dot(h, sub[...], preferred_element_type=F32) * (s_b / tot)
        _store_token_tiles(o_ref, jnp.dot(act_a.astype(BF16), sda[...], preferred_element_type=F32)
                           + jnp.dot(act_b.astype(BF16), sdb[...], preferred_element_type=F32))
        if rows < MOE_TM:
            o_ref[rows * TOK_ROWS:, :] = jnp.zeros(((MOE_TM - rows) * TOK_ROWS, LANES), F32)

    half = (flags & 2) != 0

    @pl.when(jnp.logical_and(valid, jnp.logical_not(half)))
    def _():
        experts(MOE_TM)

    @pl.when(jnp.logical_and(valid, half))
    def _():
        experts(MOE_TM // 2)

    @pl.when(jnp.logical_not(valid))
    def _():
        o_ref[...] = jnp.zeros_like(o_ref)


def _moe(hs, plan, layer, rwt_bf16, w_gate, w_up, w_down):
    ea, eb, flags, _ = plan
    n_tiles = ea.shape[0]
    wspec = lambda shape, which: pl.BlockSpec(
        (None, None) + shape, (lambda t, ea, eb, fl: (layer, ea[t], 0, 0)) if which == 0
        else (lambda t, ea, eb, fl: (layer, eb[t], 0, 0)))
    up = (D_MODEL, D_FF_EXPERT)
    dn = (D_FF_EXPERT, D_MODEL)
    tile = pl.BlockSpec((MOE_TM * TOK_ROWS, LANES), lambda t, ea, eb, fl: (t, 0))
    grid_spec = pltpu.PrefetchScalarGridSpec(
        num_scalar_prefetch=3,
        grid=(n_tiles,),
        in_specs=[
            tile,
            pl.BlockSpec((N_EXPERTS, D_MODEL), lambda t, ea, eb, fl: (0, 0)),
            wspec(up, 0), wspec(up, 0), wspec(dn, 0), wspec(up, 1), wspec(up, 1), wspec(dn, 1),
        ],
        out_specs=tile,
        scratch_shapes=[pltpu.VMEM(up, BF16), pltpu.VMEM(up, BF16), pltpu.VMEM(dn, BF16),
                        pltpu.VMEM(up, BF16), pltpu.VMEM(up, BF16), pltpu.VMEM(dn, BF16)],
    )
    return pl.pallas_call(
        _moe_kernel,
        grid_spec=grid_spec,
        out_shape=jax.ShapeDtypeStruct((n_tiles * MOE_TM * TOK_ROWS, LANES), F32),
        compiler_params=_cparams(("arbitrary",)),
        name="moe",
    )(ea, eb, flags, hs, rwt_bf16, w_gate, w_up, w_down, w_gate, w_up, w_down)


def _gather_expert_rows(packed_ref, rs_ref, ys_ref, ybuf, sem, tm):
    i = pl.program_id(0)
    n_i = pl.num_programs(0)
    slot = i % 2

    def gather(tile, buf_slot):
        def body(r8, _):
            for j in range(8):
                r = r8 * 8 + j
                d = _class_row(packed_ref[0, tile * tm + r], rs_ref)
                pltpu.make_async_copy(ys_ref.at[pl.ds(pl.multiple_of(d * TOK_ROWS, TOK_ROWS), TOK_ROWS)],
                                      ybuf.at[buf_slot, pl.ds(pl.multiple_of(r * TOK_ROWS, TOK_ROWS), TOK_ROWS)],
                                      sem.at[buf_slot]).start(priority=j % 2)
            return 0
        lax.fori_loop(0, tm // 8, body, 0)

    @pl.when(i == 0)
    def _():
        gather(0, 0)

    @pl.when(i + 1 < n_i)
    def _():
        gather(jnp.minimum(i + 1, n_i - 1), 1 - slot)

    pltpu.make_async_copy(ys_ref.at[pl.ds(0, tm * TOK_ROWS)], ybuf.at[slot], sem.at[slot]).wait()
    return _load_token_tiles(ybuf, tm, slot)


def _fin_proj_kernel(packed_ref, rs_ref, x_ref, ys_ref, g_ref, lnw_ref, lnb_ref, sh_ref, sc_ref, w_f32_ref,
                     x2_ref, u_ref, ybuf, sem, w_ref, *, tm):
    _cast_once(w_f32_ref, w_ref)
    y = _gather_expert_rows(packed_ref, rs_ref, ys_ref, ybuf, sem, tm)
    x2 = _layer_norm(ALPHA * x_ref[...] + g_ref[...] * y, lnw_ref[...], lnb_ref[...])
    x2_ref[...] = x2
    h = (x2 * (1.0 + sc_ref[...]) + sh_ref[...]).astype(BF16)
    for s in range(u_ref.shape[1] // SEC):
        u_ref[:, s * SEC:(s + 1) * SEC] = jnp.dot(h, w_ref[:, s * SEC:(s + 1) * SEC],
                                                  preferred_element_type=F32).astype(BF16)


def _fin_proj(x1, ys, packed, row_start, mod, layer, ln_w, ln_b, w_next, cond_row, tm=512):
    T = x1.shape[0]
    N = w_next.shape[1]
    row = pl.BlockSpec((tm, D_MODEL), lambda i, *_: (i, 0))
    vec = pl.BlockSpec((1, D_MODEL), lambda i, *_: (0, 0))
    crow = lambda i: cond_row(i, tm)
    grid_spec = pltpu.PrefetchScalarGridSpec(
        num_scalar_prefetch=2,
        grid=(T // tm,),
        in_specs=[row, pl.BlockSpec(memory_space=pl.ANY), _mod_spec(layer, 5, crow), vec, vec,
                  _mod_spec(layer + 1, 0, crow), _mod_spec(layer + 1, 1, crow),
                  pl.BlockSpec((D_MODEL, N), lambda i, *_: (0, 0), pipeline_mode=pl.Buffered(1))],
        out_specs=[row, pl.BlockSpec((tm, N), lambda i, *_: (i, 0))],
        scratch_shapes=[pltpu.VMEM((2, tm * TOK_ROWS, LANES), F32), pltpu.SemaphoreType.DMA((2,)),
                        pltpu.VMEM((D_MODEL, N), BF16)],
    )
    return pl.pallas_call(
        functools.partial(_fin_proj_kernel, tm=tm),
        grid_spec=grid_spec,
        out_shape=[jax.ShapeDtypeStruct((T, D_MODEL), F32), jax.ShapeDtypeStruct((T, N), BF16)],
        compiler_params=_cparams(("arbitrary",)),
        name="post_moe_in_proj",
    )(packed, row_start, x1, ys, mod, ln_w.reshape(1, -1), ln_b.reshape(1, -1), mod, mod, w_next)


def _fin_kernel(packed_ref, rs_ref, x_ref, ys_ref, g_ref, lnw_ref, lnb_ref, *rest, tm, group_tiles):
    o_refs = rest[:len(group_tiles)]
    ybuf, sem = rest[len(group_tiles):]
    i = pl.program_id(0)
    y = _layer_norm(ALPHA * x_ref[...] + g_ref[...] * _gather_expert_rows(packed_ref, rs_ref, ys_ref, ybuf, sem, tm),
                    lnw_ref[...], lnb_ref[...])
    first = 0
    for o_ref, nt in zip(o_refs, group_tiles):
        @pl.when(jnp.logical_and(i >= first, i < first + nt))
        def _(o_ref=o_ref):
            o_ref[...] = y
        first += nt


def _fin(x1, ys, packed, row_start, mod, layer, ln_w, ln_b, cond_row, group_rows, tm=256):
    group_tiles = tuple(n // tm for n in group_rows)
    row = pl.BlockSpec((tm, D_MODEL), lambda i, *_: (i, 0))
    vec = pl.BlockSpec((1, D_MODEL), lambda i, *_: (0, 0))
    out_specs, first = [], 0
    for nt in group_tiles:
        out_specs.append(_group_spec(D_MODEL, tm, first, nt))
        first += nt
    grid_spec = pltpu.PrefetchScalarGridSpec(
        num_scalar_prefetch=2,
        grid=(sum(group_tiles),),
        in_specs=[row, pl.BlockSpec(memory_space=pl.ANY), _mod_spec(layer, 5, lambda i: cond_row(i, tm)), vec, vec],
        out_specs=out_specs,
        scratch_shapes=[pltpu.VMEM((2, tm * TOK_ROWS, LANES), F32), pltpu.SemaphoreType.DMA((2,))],
    )
    return pl.pallas_call(
        functools.partial(_fin_kernel, tm=tm, group_tiles=group_tiles),
        grid_spec=grid_spec,
        out_shape=[jax.ShapeDtypeStruct((n, D_MODEL), F32) for n in group_rows],
        compiler_params=_cparams(("arbitrary",)),
        name="post_moe",
    )(packed, row_start, x1, ys, mod, ln_w.reshape(1, -1), ln_b.reshape(1, -1))


def _filt_kernel(z_ref, w1_ref, b1_ref, w2_ref, b2_ref, fr_ref, w3_ref, dl_ref, o_ref):
    i = pl.program_id(0)
    z = z_ref[...]
    fr = fr_ref[...]
    a = jnp.sin(fr * (jnp.dot(z, w1_ref[...], precision=HIGHEST, preferred_element_type=F32) + b1_ref[...]))
    a = jnp.sin(fr * (jnp.dot(a, w2_ref[...], precision=HIGHEST, preferred_element_type=F32) + b2_ref[...]))
    filt = _dot_3pass(a, w3_ref[...])
    window = jnp.exp(-z[:, 0:1] * dl_ref[...]) + HY_SHIFT
    rows = i * z.shape[0] + lax.broadcasted_iota(jnp.int32, filt.shape, 0)
    o_ref[...] = jnp.where(rows == 0, 0.0, filt * window)


def _hyena_taps(seq_len, w1, b1, w2, b2, w3, freq):
    L = seq_len
    t = np.linspace(0.0, 1.0, L)[:, None]
    bands = np.linspace(1e-4, HY_BANDS - 1, HY_BANDS)
    ang = 2.0 * math.pi * bands[None, :] * np.arange(L)[:, None] / L
    z = np.concatenate([t, np.cos(ang), -np.sin(ang)], axis=-1)
    offs = np.minimum(np.abs(np.arange(2 * L) - L), L - 1)
    z2_np = np.zeros((2 * L, LANES), np.float32)
    z2_np[:, :z.shape[1]] = z[offs]
    z2 = jnp.asarray(z2_np)
    pad_c = lambda a: jnp.zeros((a.shape[0], LANES), F32).at[:, :a.shape[1]].set(a)
    pad_r = lambda a: jnp.zeros((LANES, a.shape[1]), F32).at[:a.shape[0], :].set(a)
    w1p = pad_r(pad_c(w1))
    w2p = pad_r(pad_c(w2))
    w3p = pad_r(w3)
    b1p, b2p, frp = pad_c(b1[None, :]), pad_c(b2[None, :]), pad_c(freq[None, :])
    deltas = jnp.asarray(np.abs(np.linspace(math.log(HY_DECAY_TARGET) / HY_SLOW, math.log(HY_DECAY_TARGET) / HY_FAST,
                                            D_MODEL))[None, :].astype(np.float32))
    rb = min(1024, L)
    cbf = D_MODEL
    ncb = D_MODEL // cbf
    nrb_back = L // rb
    sq = pl.BlockSpec((LANES, LANES), lambda i, j: (0, 0))
    vec = pl.BlockSpec((1, LANES), lambda i, j: (0, 0))
    return pl.pallas_call(
        _filt_kernel,
        grid=(2 * L // rb, ncb),
        in_specs=[
            pl.BlockSpec((rb, LANES), lambda i, j: (i, 0)),
            sq, vec, sq, vec, vec,
            pl.BlockSpec((LANES, cbf), lambda i, j: (0, jnp.where(i < nrb_back, ncb + j, j))),
            pl.BlockSpec((1, cbf), lambda i, j: (0, j)),
        ],
        out_specs=pl.BlockSpec((rb, cbf), lambda i, j: (i, j)),
        out_shape=jax.ShapeDtypeStruct((2 * L, D_MODEL), F32),
        compiler_params=_cparams(("arbitrary", "arbitrary")),
        name="hyena_taps",
    )(z2, w1p, b1p, w2p, b2p, frp, w3p, deltas)


def _dft_mats(cb):
    n = 2 * cb
    m = np.arange(cb)
    f = np.arange(cb)
    ang = 2.0 * np.pi * ((f[:, None] * m[None, :]) % n) / n
    fwd = np.concatenate([np.cos(ang), -np.sin(ang)], axis=0)
    fwd[cb, :] = np.where(m % 2 == 0, 1.0, -1.0)
    coef = np.where(f == 0, 1.0, 2.0)[None, :] / n
    inv = np.concatenate([coef * np.cos(ang.T), -coef * np.sin(ang.T)], axis=1)
    inv[:, cb] = np.where(m % 2 == 0, 1.0, -1.0) / n
    return fwd.astype(np.float32), inv.astype(np.float32)


def _hconv_kernel(x0_ref, x1_ref, v_ref, cw0_ref, cw1_ref, cw2_ref, cb0_ref, cb1_ref, cb2_ref,
                  taps_ref, fb_ref, fwd_ref, inv_ref, o_ref,
                  hs_ref, stage_ref, w32_ref, w_ref, x0c_ref, u_ref, y_ref, *, seq_len, blk, n_seq):
    L = seq_len
    nb = L // blk
    cw = x0_ref.shape[1]
    RC = 256
    bi = pl.program_id(1)
    row0 = lax.broadcasted_iota(jnp.int32, (RC, cw), 0) == 0

    @pl.when(bi == 0)
    def _():
        rows = lax.broadcasted_iota(jnp.int32, (2 * blk, 1), 0)
        sign = jnp.where(rows % 2 == 0, 1.0, -1.0)
        real_row = rows <= blk
        prev = None
        for jb in range(2 * nb):
            cur = jnp.dot(fwd_ref[...], taps_ref[jb * blk:(jb + 1) * blk, :].astype(BF16),
                          preferred_element_type=F32)
            if prev is not None:
                first_tap = taps_ref[(jb - 1) * blk:(jb - 1) * blk + 1, :].astype(BF16).astype(F32)
                hs_ref[jb - 1] = cur + sign * (prev - jnp.where(real_row, first_tap, 0.0))
            prev = cur

    zeros8 = jnp.zeros((8, cw), F32)
    for s in range(n_seq):
        r0 = s * L

        def short_conv(k, src_ref, cw_ref, cb_ref, store):
            stage_ref[k, 0:8, :] = zeros8
            stage_ref[k, 8 + L:16 + L, :] = zeros8
            for c in range(L // RC):
                stage_ref[k, 8 + c * RC:8 + (c + 1) * RC, :] = \
                    src_ref[r0 + c * RC:r0 + (c + 1) * RC, :].astype(F32)
            w = cw_ref[...]
            for c in range(L // RC):
                lo = stage_ref[k, 7 + c * RC:7 + (c + 1) * RC, :]
                mid = stage_ref[k, 8 + c * RC:8 + (c + 1) * RC, :]
                hi = stage_ref[k, 9 + c * RC:9 + (c + 1) * RC, :]
                store(c, lo * w[0:1, :] + mid * w[1:2, :] + hi * w[2:3, :] + cb_ref[...])

        def st_x1(c, val):
            w32_ref[c * RC:(c + 1) * RC, :] = val

        def st_v(c, val):
            w_ref[c * RC:(c + 1) * RC, :] = (w32_ref[c * RC:(c + 1) * RC, :] * val).astype(BF16)

        def st_x0(c, val):
            x0c_ref[c * RC:(c + 1) * RC, :] = val.astype(BF16)

        short_conv(0, x1_ref, cw1_ref, cb1_ref, st_x1)
        short_conv(1, v_ref, cw2_ref, cb2_ref, st_v)
        short_conv(2, x0_ref, cw0_ref, cb0_ref, st_x0)

        for j in range(nb):
            u_ref[j] = jnp.dot(fwd_ref[...], w_ref[j * blk:(j + 1) * blk, :], preferred_element_type=F32)

        for i in range(nb):
            for c in range(blk // RC):
                re = None
                im = None
                for j in range(nb):
                    k = i - j + nb - 1
                    a = u_ref[j, c * RC:(c + 1) * RC, :]
                    b = u_ref[j, blk + c * RC:blk + (c + 1) * RC, :]
                    hr = hs_ref[k, c * RC:(c + 1) * RC, :]
                    hi = hs_ref[k, blk + c * RC:blk + (c + 1) * RC, :]
                    bb = b * hi
                    if c == 0:
                        t_re = a * hr - jnp.where(row0, 0.0, bb)
                        t_im = jnp.where(row0, bb, a * hi + b * hr)
                    else:
                        t_re = a * hr - bb
                        t_im = a * hi + b * hr
                    re = t_re if re is None else re + t_re
                    im = t_im if im is None else im + t_im
                y_ref[i, c * RC:(c + 1) * RC, :] = re.astype(BF16)
                y_ref[i, blk + c * RC:blk + (c + 1) * RC, :] = im.astype(BF16)
            conv = jnp.dot(inv_ref[...], y_ref[i], preferred_element_type=F32)
            sl = slice(i * blk, (i + 1) * blk)
            z = x0c_ref[sl, :].astype(F32) * (conv + w_ref[sl, :].astype(F32) * fb_ref[...])
            o_ref[r0 + i * blk:r0 + (i + 1) * blk, :] = z.astype(BF16)


def _hyena_conv(u, conv_w, conv_b, taps, filt_bias, batch, seq_len, blk, n_seq, row_off=0, cw=256):
    T = batch * seq_len
    L = seq_len
    nb = L // blk
    ncw = D_MODEL // cw
    fwd_np, inv_np = _dft_mats(blk)
    fwd = jnp.asarray(fwd_np).astype(BF16)
    inv = jnp.asarray(inv_np).astype(BF16)
    rows = n_seq * L
    assert row_off % rows == 0
    boff = row_off // rows
    sec = lambda s: pl.BlockSpec((rows, cw), lambda c, b: (b + boff, s * ncw + c))
    cws = lambda s: pl.BlockSpec((3, cw), lambda c, b: (0, s * ncw + c))
    cbs = lambda s: pl.BlockSpec((1, cw), lambda c, b: (0, s * ncw + c))
    return pl.pallas_call(
        functools.partial(_hconv_kernel, seq_len=L, blk=blk, n_seq=n_seq),
        grid=(ncw, batch // n_seq),
        in_specs=[sec(0), sec(1), sec(2), cws(0), cws(1), cws(2), cbs(0), cbs(1), cbs(2),
                  pl.BlockSpec((2 * L, cw), lambda c, b: (0, c), pipeline_mode=pl.Buffered(1)),
                  pl.BlockSpec((1, cw), lambda c, b: (0, c)),
                  pl.BlockSpec((2 * blk, blk), lambda c, b: (0, 0), pipeline_mode=pl.Buffered(1)),
                  pl.BlockSpec((blk, 2 * blk), lambda c, b: (0, 0), pipeline_mode=pl.Buffered(1))],
        out_specs=pl.BlockSpec((rows, cw), lambda c, b: (b, c)),
        out_shape=jax.ShapeDtypeStruct((T, D_MODEL), BF16),
        scratch_shapes=[
            pltpu.VMEM((2 * nb - 1, 2 * blk, cw), F32),
            pltpu.VMEM((3, L + 16, cw), F32),
            pltpu.VMEM((L, cw), F32),
            pltpu.VMEM((L, cw), BF16),
            pltpu.VMEM((L, cw), BF16),
            pltpu.VMEM((nb, 2 * blk, cw), F32),
            pltpu.VMEM((nb, 2 * blk, cw), BF16),
        ],
        compiler_params=_cparams(("arbitrary", "arbitrary")),
        name="hyena_conv",
    )(u, u, u, conv_w, conv_w, conv_w, conv_b.reshape(1, -1), conv_b.reshape(1, -1), conv_b.reshape(1, -1),
      taps, filt_bias.reshape(1, -1), fwd, inv)


def kernel(x_prompt, x_sample, cache_diff_k, cache_diff_v, state_ret_fwd, state_ret_bwd, c, c_ctx, ada_w, ada_b, ln_w, ln_b, ev_w_in, ev_w_out, ret_decay_fwd, ret_decay_bwd, ret_gn_w, diff_lambda, diff_subln_w, hy_w_in, hy_conv_w, hy_conv_b, hy_ffn_w1, hy_ffn_b1, hy_ffn_w2, hy_ffn_b2, hy_ffn_w3, hy_freq, hy_filter_bias, hy_w_out, router_w, router_bias, moe_w_gate, moe_w_up, moe_w_down):
    B, S, D = x_prompt.shape
    DB, DS, _ = x_sample.shape
    PAST = cache_diff_k.shape[2]
    TP, TS = B * S, DB * DS
    T_ALL = TP + TS
    assert D == D_MODEL and 1 + DB <= COND_ROWS and T_ALL < 65536

    cond8 = jnp.zeros((COND_ROWS, D), F32).at[0].set(c_ctx).at[1:1 + DB].set(c)
    mod = _ada_mod(cond8, ada_w, ada_b).reshape(DEPTH * COND_ROWS * N_MOD, 1, D)

    tm = 512
    groups = [
        dict(x=x_prompt.reshape(TP, D), x_off=0, batch=B, seq=S, off=0, cond_row=lambda i, tm: 0),
        dict(x=x_sample.reshape(TS, D), x_off=0, batch=DB, seq=DS, off=TP,
             cond_row=lambda i, tm: 1 + (i * tm) // DS),
    ]
    u_next = None

    def cond_row_all(i, tm):
        return jnp.where(i * tm < TP, 0, 1 + (i * tm - TP) // DS)

    rw_pad = jnp.zeros((D, LANES), F32).at[:, :N_EXPERTS].set(router_w.astype(F32))
    rw_hi = rw_pad.astype(BF16)
    rw_lo = jnp.concatenate([rw_hi, (rw_pad - rw_hi.astype(F32)).astype(BF16)], axis=1)
    rwt = router_w.T.astype(BF16)
    rbias = router_bias.astype(F32)
    n_tiles = T_ALL // MOE_TM + N_CLASSES
    outs = {}

    for l in range(DEPTH):
        mixes = []
        if l % 2 == 0:
            e = l // 2
            w_in = ev_w_in[e]
            w_out = ev_w_out[e]
            lg = jnp.stack([jnp.log1p(-jnp.exp2(ret_decay_fwd[e].astype(F32))),
                            jnp.log1p(-jnp.exp2(ret_decay_bwd[e].astype(F32)))])
            lam_init = 0.8 - 0.6 * math.exp(-0.3 * l)
            lq1, lk1, lq2, lk2 = diff_lambda[e].astype(F32)
            lam = (jnp.exp(jnp.sum(lq1 * lk1)) - jnp.exp(jnp.sum(lq2 * lk2)) + lam_init).reshape(1)
            kscale = RET_DK ** -0.5
            for gi, g in enumerate(groups):
                tm_in = tm
                crow = functools.partial(g["cond_row"], tm=tm_in)
                if gi == 0:
                    secs = (("none", 1.0, False), ("none", kscale, False), ("none", 1.0, False),
                            ("none", 1.0, False), ("none", 1.0, False), ("none", 1.0, True),
                            ("none", 1.0, True))
                    proj, kd, vd = _in_proj(g["x"], mod, l, w_in, secs, crow, tm_in, tm=tm_in)
                    outs.setdefault("kd", []).append(kd.reshape(B, 1, S, H_DIFF, 2 * DIFF_QK))
                    outs.setdefault("vd", []).append(vd.reshape(B, 1, S, H_DIFF, DIFF_V))
                    ret, sf, sb = _retention(proj, lg, ret_gn_w[e], g["batch"], g["seq"], write_state=True,
                                             heads=H_RET)
                    outs.setdefault("sf", []).append(sf.reshape(B, 1, H_RET, RET_DK, RET_DV))
                    outs.setdefault("sb", []).append(sb.reshape(B, 1, H_RET, RET_DK, RET_DV))
                    att = _diff_attention(proj, lam, diff_subln_w[e], lam_init, g["batch"], g["seq"],
                                          tq=min(256, g["seq"]), heads=H_DIFF)
                else:
                    secs = (("ret", 1.0, False), ("ret", kscale, False), ("none", 1.0, False),
                            ("none", 1.0, False), ("diff", 1.0, False), ("diff", 1.0, False),
                            ("none", 1.0, False))
                    tabs = {"ret": _rope_tables(g["seq"], RET_DK), "diff": _rope_tables(g["seq"], DIFF_QK)}
                    (proj,) = _in_proj(g["x"], mod, l, w_in, secs, crow, g["seq"], rope_tabs=tabs, tm=tm_in)
                    ret, = _retention(proj, lg, ret_gn_w[e], g["batch"], g["seq"],
                                      states=(state_ret_fwd[:, e], state_ret_bwd[:, e]))
                    ctx = (cache_diff_k[:, e].reshape(DB, PAST, H_DIFF * 2 * DIFF_QK),
                           cache_diff_v[:, e].reshape(DB, PAST, H_DIFF * DIFF_V))
                    att = _diff_attention(proj, lam, diff_subln_w[e], lam_init, g["batch"], g["seq"], ctx=ctx,
                                          tq=1024, heads=1)
                mixes.append((ret, att))
        else:
            o = l // 2
            w_in = hy_w_in[o]
            w_out = hy_w_out[o]
            secs = (("none", 1.0, False),) * (3 * D // SEC)
            for gi, g in enumerate(groups):
                if u_next is None:
                    crow = functools.partial(g["cond_row"], tm=tm)
                    (u,) = _in_proj(g["x"], mod, l, w_in, secs, crow, tm, tm=tm)
                    u_off = 0
                else:
                    u, u_off = u_next, g["off"]
                taps = _hyena_taps(g["seq"], hy_ffn_w1[o], hy_ffn_b1[o], hy_ffn_w2[o], hy_ffn_b2[o],
                                   hy_ffn_w3[o], hy_freq[o])
                blk = min(g["seq"], 512)
                n_seq = max(1, 1024 // g["seq"])
                z = _hyena_conv(u, hy_conv_w[o], hy_conv_b[o], taps, hy_filter_bias[o], g["batch"], g["seq"],
                                blk, n_seq, row_off=u_off)
                mixes.append((z,))

        x1_all, h2_all, info_all, counts = _post_mix(mixes, [(g["x"], g["x_off"]) for g in groups], w_out, mod, l, ln_w[l, 0],
                                                     ln_b[l, 0], rw_hi, rw_lo, rbias, cond_row_all, tm)
        packed = info_all
        counts_i = counts[:, 0].astype(jnp.int32)
        plan = _moe_plan(counts_i, n_tiles)
        hs = _dispatch(h2_all, packed, plan, counts_i, n_tiles)
        ys = _moe(hs, plan, l, rwt, moe_w_gate, moe_w_up, moe_w_down)
        if l + 1 < DEPTH and (l + 1) % 2 == 1:
            x_all, u_next = _fin_proj(x1_all, ys, packed, plan[3], mod, l, ln_w[l, 1], ln_b[l, 1],
                                      hy_w_in[(l + 1) // 2], cond_row_all, tm=tm)
            for g in groups:
                g["x"], g["x_off"] = x_all, g["off"]
        else:
            u_next = None
            xs = _fin(x1_all, ys, packed, plan[3], mod, l, ln_w[l, 1], ln_b[l, 1], cond_row_all, (TP, TS))
            for g, x in zip(groups, xs):
                g["x"], g["x_off"] = x, 0

    y_prompt = groups[0]["x"].reshape(B, S, D)
    y_sample = groups[1]["x"].reshape(DB, DS, D)
    cat = lambda xs: xs[0] if len(xs) == 1 else jnp.concatenate(xs, axis=1)
    return (y_prompt, y_sample, cat(outs["kd"]), cat(outs["vd"]), cat(outs["sf"]), cat(outs["sb"]))
```

```python
import functools
import math

import numpy as np
import jax
import jax.numpy as jnp
from jax import lax
from jax.experimental import pallas as pl
from jax.experimental.pallas import tpu as pltpu

F32 = jnp.float32
BF16 = jnp.bfloat16
HIGHEST = lax.Precision.HIGHEST

D_MODEL = 1024
DEPTH = 2
GRID_W = 64
H_RET = 4
RET_DK = 128
RET_DV = 128
RET_CHUNK = 128
H_DIFF = 4
DIFF_QK = 64
DIFF_V = 128
ROPE_BASE = 10000.0
HY_BANDS = 16
HY_FH = 64
HY_DECAY_TARGET = 1e-2
HY_FAST = 0.3
HY_SLOW = 1.5
HY_SHIFT = 0.05
N_EXPERTS = 16
N_GROUPS = 4
GROUP_SIZE = N_EXPERTS // N_GROUPS
D_FF_EXPERT = 512
ALPHA = (2 * DEPTH) ** 0.25
LN_EPS = 1e-5

LANES = 128
SEC = 512
COND_ROWS = 8
N_MOD = 6
VMEM_LIMIT = 50 * 1024 * 1024

PAIR_SLOTS = ((0, 1), (2, 1), (2, 3), (0, 3), (0, 2), (1, 3))
N_PAIRS = len(PAIR_SLOTS)
N_CLASSES = N_GROUPS * N_PAIRS
MOE_TM = 256
MOE_SHIFT = MOE_TM.bit_length() - 1
POST_SUB = 512
TOK_ROWS = D_MODEL // LANES


def _cparams(sem):
    return pltpu.CompilerParams(dimension_semantics=sem, vmem_limit_bytes=VMEM_LIMIT)


def _silu(x):
    return x * jax.nn.sigmoid(x)


def _store_token_tiles(ref, x):
    n = x.shape[0]
    for j in range(TOK_ROWS):
        ref[pl.ds(j, n, stride=TOK_ROWS), :] = x[:, j * LANES:(j + 1) * LANES]


def _load_token_tiles(ref, n, slot=None):
    idx = () if slot is None else (slot,)
    return jnp.concatenate([ref[idx + (pl.ds(j, n, stride=TOK_ROWS), slice(None))] for j in range(TOK_ROWS)],
                           axis=1)


def _layer_norm(x, w, b):
    mu = jnp.mean(x, axis=-1, keepdims=True)
    xc = x - mu
    var = jnp.mean(xc * xc, axis=-1, keepdims=True)
    return xc * lax.rsqrt(var + LN_EPS) * w + b


def _dot_3pass(a, w):
    a_hi = a.astype(BF16)
    a_lo = (a - a_hi.astype(F32)).astype(BF16)
    w_hi = w.astype(BF16)
    w_lo = (w - w_hi.astype(F32)).astype(BF16)
    return (jnp.dot(a_hi, w_hi, preferred_element_type=F32) + jnp.dot(a_lo, w_hi, preferred_element_type=F32)
            + jnp.dot(a_hi, w_lo, preferred_element_type=F32))


def _ada_kernel(c_ref, w_ref, b_ref, o_ref):
    o_ref[...] = _dot_3pass(_silu(c_ref[...]), w_ref[...]) + b_ref[...]


def _ada_mod(cond8, ada_w, ada_b):
    tn = 1024
    nj = ada_w.shape[2] // tn
    return pl.pallas_call(
        _ada_kernel,
        grid=(DEPTH, nj),
        in_specs=[
            pl.BlockSpec((COND_ROWS, D_MODEL), lambda l, j: (0, 0)),
            pl.BlockSpec((None, D_MODEL, tn), lambda l, j: (l, 0, j)),
            pl.BlockSpec((None, 1, tn), lambda l, j: (l, 0, j)),
        ],
        out_specs=pl.BlockSpec((None, COND_ROWS, tn), lambda l, j: (l, 0, j)),
        out_shape=jax.ShapeDtypeStruct((DEPTH, COND_ROWS, ada_w.shape[2]), F32),
        compiler_params=_cparams(("arbitrary", "arbitrary")),
        name="ada_mod",
    )(cond8, ada_w, ada_b.reshape(DEPTH, 1, -1))


def _mod_spec(layer, chunk, row_of_tile):
    def imap(i, *_):
        return ((layer * COND_ROWS + row_of_tile(i)) * N_MOD + chunk, 0, 0)
    return pl.BlockSpec((None, 1, D_MODEL), imap)


def _rope(a, tabs, quarter):
    c, sa, sb = tabs
    out = []
    for hb in range(a.shape[1] // LANES):
        blk = a[:, hb * LANES:(hb + 1) * LANES]
        up = pltpu.roll(blk, LANES - quarter, axis=1)
        dn = pltpu.roll(blk, quarter, axis=1)
        out.append(blk * c + up * sa + dn * sb)
    return jnp.concatenate(out, axis=1)


def _cast_once(w_ref, w_bf16_ref):
    @pl.when(pl.program_id(0) == 0)
    def _():
        for c in range(0, w_ref.shape[1], SEC):
            w_bf16_ref[:, c:c + SEC] = w_ref[:, c:c + SEC].astype(BF16)


def _in_kernel(*refs, secs, n_f32_out):
    x_ref, sh_ref, sc_ref, w_ref = refs[:4]
    pos = 4
    tabs = {}
    for kind in ("ret", "diff"):
        if any(s[0] == kind for s in secs):
            tabs[kind] = tuple(r[...] for r in refs[pos:pos + 3])
            pos += 3
    o_ref = refs[pos]
    f32_refs = refs[pos + 1:pos + 1 + n_f32_out]
    w_bf16_ref = refs[-1]
    _cast_once(w_ref, w_bf16_ref)
    h = (x_ref[...] * (1.0 + sc_ref[...]) + sh_ref[...]).astype(BF16)
    k32 = 0
    for s, (kind, scale, want_f32) in enumerate(secs):
        acc = jnp.dot(h, w_bf16_ref[:, s * SEC:(s + 1) * SEC], preferred_element_type=F32)
        if scale != 1.0:
            acc = acc * scale
        if kind == "ret":
            acc = _rope(acc, tabs["ret"], RET_DK // 4)
        elif kind == "diff":
            acc = _rope(acc, tabs["diff"], DIFF_QK // 4)
        o_ref[:, s * SEC:(s + 1) * SEC] = acc.astype(BF16)
        if want_f32:
            for hb in range(SEC // LANES):
                f32_refs[k32][:, hb, :] = acc[:, hb * LANES:(hb + 1) * LANES]
            k32 += 1
    assert k32 == n_f32_out


def _in_proj(x2d, mod, layer, w_f32, secs, row_of_tile, seq_len, rope_tabs=None, tm=512):
    T = x2d.shape[0]
    N = w_f32.shape[1]
    assert N == SEC * len(secs) and T % tm == 0 and seq_len % tm == 0
    tiles_per_seq = seq_len // tm
    in_specs = [
        pl.BlockSpec((tm, D_MODEL), lambda i: (i, 0)),
        _mod_spec(layer, 0, row_of_tile),
        _mod_spec(layer, 1, row_of_tile),
        pl.BlockSpec((D_MODEL, N), lambda i: (0, 0), pipeline_mode=pl.Buffered(1)),
    ]
    args = [x2d, mod, mod, w_f32]
    for kind in ("ret", "diff"):
        if any(s[0] == kind for s in secs):
            for t in rope_tabs[kind]:
                in_specs.append(pl.BlockSpec((tm, LANES), lambda i: (i % tiles_per_seq, 0)))
                args.append(t)
    n_f32 = sum(1 for s in secs if s[2])
    heads = SEC // LANES
    out_shape = [jax.ShapeDtypeStruct((T, N), BF16)] + [jax.ShapeDtypeStruct((T, heads, LANES), F32)] * n_f32
    out_specs = ([pl.BlockSpec((tm, N), lambda i: (i, 0))]
                 + [pl.BlockSpec((tm, heads, LANES), lambda i: (i, 0, 0))] * n_f32)
    return pl.pallas_call(
        functools.partial(_in_kernel, secs=secs, n_f32_out=n_f32),
        grid=(T // tm,),
        in_specs=in_specs,
        out_specs=out_specs,
        out_shape=out_shape,
        scratch_shapes=[pltpu.VMEM((D_MODEL, N), BF16)],
        compiler_params=_cparams(("arbitrary",)),
        name="in_proj",
    )(*args)


def _rope_tables(seq_len, d):
    half = d // 2
    quarter = half // 2
    t = np.arange(seq_len)
    inv = ROPE_BASE ** (-np.arange(quarter, dtype=np.float64) / quarter)
    ang_r = (t // GRID_W)[:, None] * inv[None, :]
    ang_c = (t % GRID_W)[:, None] * inv[None, :]
    zero = np.zeros_like(ang_r)
    cos = np.concatenate([np.cos(ang_r)] * 2 + [np.cos(ang_c)] * 2, axis=1)
    sa = np.concatenate([-np.sin(ang_r), zero, -np.sin(ang_c), zero], axis=1)
    sb = np.concatenate([zero, np.sin(ang_r), zero, np.sin(ang_c)], axis=1)
    reps = LANES // d
    return tuple(jnp.asarray(np.tile(a, (1, reps)).astype(np.float32)) for a in (cos, sa, sb))


def _ret_kernel(*refs, n_chunks, has_state, write_state, heads):
    lg_ref, q_ref, k_ref, v_ref, g_ref, gnw_ref = refs[:6]
    pos = 6
    if has_state:
        s0f_ref, s0b_ref = refs[pos:pos + 2]
        pos += 2
    o_ref = refs[pos]
    pos += 1
    if write_state:
        sf_ref, sb_ref = refs[pos:pos + 2]
        pos += 2
    acc_ref, kv_ref = refs[pos:pos + 2]
    for hh in range(heads):
        _ret_head(hh, heads, lg_ref, q_ref, k_ref, v_ref, g_ref, gnw_ref,
                  (s0f_ref, s0b_ref) if has_state else None, o_ref, (sf_ref, sb_ref) if write_state else None,
                  acc_ref, kv_ref, n_chunks)


def _ret_head(hh, heads, lg_ref, q_ref, k_ref, v_ref, g_ref, gnw_ref, s0_refs, o_ref, s_out_refs, acc_ref, kv_ref,
              n_chunks):
    C = RET_CHUNK
    cols = slice(hh * LANES, (hh + 1) * LANES)
    hd = pl.program_id(1) * heads + hh
    lgf = lg_ref[0, hd]
    lgb = lg_ref[1, hd]
    ii = lax.broadcasted_iota(jnp.int32, (C, C), 0).astype(F32)
    jj = lax.broadcasted_iota(jnp.int32, (C, C), 1).astype(F32)
    rel = ii - jj
    d_f = jnp.where(rel >= 0, jnp.exp(jnp.maximum(rel, 0.0) * lgf), 0.0)
    d_b = jnp.where(rel <= 0, jnp.exp(jnp.maximum(-rel, 0.0) * lgb), 0.0)
    d_sum = d_f + d_b
    idx = lax.broadcasted_iota(jnp.int32, (C, 1), 0).astype(F32)
    xi_f = jnp.exp((idx + 1.0) * lgf)
    zeta_f = jnp.exp((C - 1.0 - idx) * lgf)
    xi_b = jnp.exp((C - idx) * lgb)
    zeta_b = jnp.exp(idx * lgb)
    one = jnp.ones((1, 1), F32)
    gc_f = jnp.exp(one * (C * lgf))
    gc_b = jnp.exp(one * (C * lgb))

    nt = (((1,), (1,)), ((), ()))
    tn = (((0,), (0,)), ((), ()))

    if s0_refs is not None:
        s_f = s0_refs[0][hh]
        s_b = s0_refs[1][hh]
    else:
        s_f = jnp.zeros((RET_DK, RET_DV), F32)
        s_b = jnp.zeros((RET_DK, RET_DV), F32)

    for n in range(n_chunks):
        sl = slice(n * C, (n + 1) * C)
        qc, kc, vc = q_ref[sl, cols], k_ref[sl, cols], v_ref[sl, cols]
        scores = lax.dot_general(qc, kc, nt, preferred_element_type=F32) * d_sum
        acc_ref[sl, cols] = jnp.dot(scores.astype(BF16), vc, preferred_element_type=F32)
        kf = kc.astype(F32)
        kv_ref[0, hh, n] = lax.dot_general((kf * zeta_f).astype(BF16), vc, tn, preferred_element_type=F32)
        kv_ref[1, hh, n] = lax.dot_general((kf * zeta_b).astype(BF16), vc, tn, preferred_element_type=F32)

    for n in range(n_chunks):
        sl = slice(n * C, (n + 1) * C)
        qf = q_ref[sl, cols].astype(F32)
        acc_ref[sl, cols] += jnp.dot((qf * xi_f).astype(BF16), s_f.astype(BF16), preferred_element_type=F32)
        s_f = gc_f * s_f + kv_ref[0, hh, n]

    gnw = gnw_ref[:, cols]
    for n in reversed(range(n_chunks)):
        sl = slice(n * C, (n + 1) * C)
        qf = q_ref[sl, cols].astype(F32)
        cross = jnp.dot((qf * xi_b).astype(BF16), s_b.astype(BF16), preferred_element_type=F32)
        r = acc_ref[sl, cols] + cross
        mu = jnp.mean(r, axis=-1, keepdims=True)
        rc = r - mu
        var = jnp.mean(rc * rc, axis=-1, keepdims=True)
        rn = rc * lax.rsqrt(var + LN_EPS) * gnw
        o_ref[sl, cols] = (_silu(g_ref[sl, cols].astype(F32)) * rn).astype(BF16)
        s_b = gc_b * s_b + kv_ref[1, hh, n]

    if s_out_refs is not None:
        s_out_refs[0][hh] = s_f
        s_out_refs[1][hh] = s_b


def _retention(proj, lg, gn_w, batch, seq_len, states=None, write_state=False, heads=2):
    T = proj.shape[0]
    hp = heads
    width = hp * LANES
    per_sec = SEC // width
    blk = lambda sec: pl.BlockSpec((seq_len, width), lambda b, h: (b, sec * per_sec + h))
    in_specs = [pl.BlockSpec(memory_space=pltpu.SMEM), blk(0), blk(1), blk(2), blk(3),
                pl.BlockSpec((1, width), lambda b, h: (0, h))]
    args = [lg, proj, proj, proj, proj, gn_w.reshape(1, -1)]
    if states is not None:
        st = pl.BlockSpec((None, hp, RET_DK, RET_DV), lambda b, h: (b, h, 0, 0))
        in_specs += [st, st]
        args += list(states)
    out_shape = [jax.ShapeDtypeStruct((T, SEC), BF16)]
    out_specs = [pl.BlockSpec((seq_len, width), lambda b, h: (b, h))]
    if write_state:
        st_o = pl.BlockSpec((None, hp, RET_DK, RET_DV), lambda b, h: (b, h, 0, 0))
        out_shape += [jax.ShapeDtypeStruct((batch, H_RET, RET_DK, RET_DV), F32)] * 2
        out_specs += [st_o, st_o]
    return pl.pallas_call(
        functools.partial(_ret_kernel, n_chunks=seq_len // RET_CHUNK, has_state=states is not None,
                          write_state=write_state, heads=hp),
        grid=(batch, H_RET // hp),
        in_specs=in_specs,
        out_specs=out_specs,
        out_shape=out_shape,
        scratch_shapes=[pltpu.VMEM((seq_len, width), F32),
                        pltpu.VMEM((2, hp, seq_len // RET_CHUNK, RET_DK, RET_DV), F32)],
        compiler_params=_cparams(("arbitrary", "arbitrary")),
        name="retention",
    )(*args)


def _att_kernel(*refs, has_ctx, out_scale, key_chunk, heads):
    lam_ref, q_ref, k_ref, v_ref = refs[:4]
    pos = 4
    if has_ctx:
        ck_ref, cv_ref = refs[pos:pos + 2]
        pos += 2
    w_ref, o_ref = refs[pos:pos + 2]
    lam = lam_ref[0]
    tq = q_ref.shape[0]
    nt = (((1,), (1,)), ((), ()))
    for hh in range(heads):
        cols = slice(hh * LANES, (hh + 1) * LANES)
        q = q_ref[:, cols]
        lane = lax.broadcasted_iota(jnp.int32, q.shape, 1)
        zero = jnp.zeros_like(q)
        qq = jnp.concatenate([jnp.where(lane < DIFF_QK, q, zero), jnp.where(lane >= DIFF_QK, q, zero)], axis=0)
        qq = qq * jnp.asarray(DIFF_QK ** -0.5, BF16)
        chunks = [(k_ref, v_ref, c * key_chunk, key_chunk) for c in range(k_ref.shape[0] // key_chunk)]
        if has_ctx:
            chunks.append((ck_ref, cv_ref, 0, ck_ref.shape[0]))
        m = l = acc = None
        for kr, vr, off, n in chunks:
            kch = kr[off:off + n, cols].astype(BF16)
            vch = vr[off:off + n, cols].astype(BF16)
            s = lax.dot_general(qq, kch, nt, preferred_element_type=F32)
            cm = jnp.max(s, axis=-1, keepdims=True)
            m_new = cm if m is None else jnp.maximum(m, cm)
            p = jnp.exp(s - m_new)
            ps = jnp.sum(p, axis=-1, keepdims=True)
            pv = jnp.dot(p.astype(BF16), vch, preferred_element_type=F32)
            if m is None:
                l, acc = ps, pv
            else:
                alpha = jnp.exp(m - m_new)
                l = alpha * l + ps
                acc = alpha * acc + pv
            m = m_new
        o = acc / l
        att = o[:tq] - lam * o[tq:]
        att = att * lax.rsqrt(jnp.mean(att * att, axis=-1, keepdims=True) + LN_EPS)
        o_ref[:, cols] = (att * w_ref[...] * out_scale).astype(BF16)


def _diff_attention(proj, lam, subln_w, lam_init, batch, seq_len, ctx=None, tq=256, heads=1):
    T = proj.shape[0]
    width = heads * LANES
    per_sec = SEC // width
    nq = seq_len // tq
    in_specs = [
        pl.BlockSpec(memory_space=pltpu.SMEM),
        pl.BlockSpec((tq, width), lambda b, h, i: (b * nq + i, 4 * per_sec + h)),
        pl.BlockSpec((seq_len, width), lambda b, h, i: (b, 5 * per_sec + h)),
        pl.BlockSpec((seq_len, width), lambda b, h, i: (b, 6 * per_sec + h)),
    ]
    args = [lam, proj, proj, proj]
    if ctx is not None:
        ck, cv = ctx
        past = ck.shape[1]
        cspec = pl.BlockSpec((None, past, width), lambda b, h, i: (b, 0, h))
        in_specs += [cspec, cspec]
        args += [ck, cv]
    in_specs.append(pl.BlockSpec((1, LANES), lambda b, h, i: (0, 0)))
    args.append(subln_w.reshape(1, -1))
    return pl.pallas_call(
        functools.partial(_att_kernel, has_ctx=ctx is not None, out_scale=1.0 - lam_init,
                          key_chunk=min(512, seq_len), heads=heads),
        grid=(batch, H_DIFF // heads, nq),
        in_specs=in_specs,
        out_specs=pl.BlockSpec((tq, width), lambda b, h, i: (b * nq + i, h)),
        out_shape=jax.ShapeDtypeStruct((T, SEC), BF16),
        compiler_params=_cparams(("arbitrary", "arbitrary", "arbitrary")),
        name="diff_attention",
    )(*args)


def _route_class(lt, rb_ref):
    sel = [jax.nn.sigmoid(lt[e:e + 1, :]) + rb_ref[e] for e in range(N_EXPERTS)]
    gscore = []
    for g in range(N_GROUPS):
        mem = sel[g * GROUP_SIZE:(g + 1) * GROUP_SIZE]
        best = None
        for a in range(GROUP_SIZE):
            for b in range(a + 1, GROUP_SIZE):
                pair = mem[a] + mem[b]
                best = pair if best is None else jnp.maximum(best, pair)
        gscore.append(best)
    gbest = gscore[0]
    gidx = jnp.zeros_like(gbest)
    for g in range(1, N_GROUPS):
        upd = gscore[g] > gbest
        gidx = jnp.where(upd, float(g), gidx)
        gbest = jnp.where(upd, gscore[g], gbest)
    msel = []
    for j in range(GROUP_SIZE):
        out = sel[j]
        for g in range(1, N_GROUPS):
            out = jnp.where(gidx == float(g), sel[g * GROUP_SIZE + j], out)
        msel.append(out)
    one = jnp.ones_like(gbest)
    zero = jnp.zeros_like(gbest)
    chosen = []
    for j in range(GROUP_SIZE):
        rank = zero
        for k in range(GROUP_SIZE):
            if k < j:
                rank = rank + jnp.where(msel[k] >= msel[j], one, zero)
            elif k > j:
                rank = rank + jnp.where(msel[k] > msel[j], one, zero)
        chosen.append(jnp.where(rank < 2.0, one, zero))
    c0, c1, c2, c3 = chosen
    order = jnp.where(c0 * c1 > 0, 0.0, jnp.where(c1 * c2 > 0, 1.0, jnp.where(c2 * c3 > 0, 2.0,
            jnp.where(c0 * c3 > 0, 3.0, jnp.where(c0 * c2 > 0, 4.0, 5.0)))))
    return gidx * float(N_PAIRS) + order


def _post_kernel(*refs, n_mix, group_tiles):
    n_groups = len(group_tiles)
    per_group = n_mix + 1
    group_refs = [refs[g * per_group:(g + 1) * per_group] for g in range(n_groups)]
    refs = refs[n_groups * per_group:]
    (w_f32_ref, g_ref, sh_ref, sc_ref, lnw_ref, lnb_ref, rwh_ref, rwl_ref, rb_ref, tri_ref,
     x1_ref, h2_ref, info_ref, cout_ref, cnt_ref, pre_ref, w_ref) = refs
    _cast_once(w_f32_ref, w_ref)
    i = pl.program_id(0)

    @pl.when(i == 0)
    def _():
        cnt_ref[...] = jnp.zeros_like(cnt_ref)

    first = 0
    for g in range(n_groups):
        @pl.when(jnp.logical_and(i >= first, i < first + group_tiles[g]))
        def _(g=g):
            out = None
            off = 0
            for m_ref in group_refs[g][:n_mix]:
                width = m_ref.shape[1]
                part = jnp.dot(m_ref[...], w_ref[off:off + width, :], preferred_element_type=F32)
                out = part if out is None else out + part
                off += width
            pre_ref[...] = ALPHA * group_refs[g][n_mix][...] + g_ref[...] * out
        first += group_tiles[g]

    tm = pre_ref.shape[0]
    ts = tri_ref.shape[0]
    base = cnt_ref[:, 0:1]
    for part in range(tm // ts):
        rows = slice(part * ts, (part + 1) * ts)
        x1 = _layer_norm(pre_ref[rows, :], lnw_ref[...], lnb_ref[...])
        x1_ref[rows, :] = x1
        h2 = x1 * (1.0 + sc_ref[...]) + sh_ref[...]
        for j in range(TOK_ROWS):
            h2_ref[pl.ds(part * ts * TOK_ROWS + j, ts, stride=TOK_ROWS), :] = h2[:, j * LANES:(j + 1) * LANES]
        h_hi = h2.astype(BF16)
        h_lo = (h2 - h_hi.astype(F32)).astype(BF16)
        both = jnp.dot(h_hi, rwl_ref[...], preferred_element_type=F32)
        logits = (both[:, :LANES] + both[:, LANES:]
                  + jnp.dot(h_lo, rwh_ref[...], preferred_element_type=F32))
        cls = _route_class(logits.T, rb_ref)
        crow = lax.broadcasted_iota(jnp.int32, (32, ts), 0).astype(F32)
        onehot = jnp.where(crow == cls, 1.0, 0.0)
        prefix = jnp.dot(onehot.astype(BF16), tri_ref[...], preferred_element_type=F32)
        rank = jnp.sum(onehot * (prefix - 1.0 + base), axis=0, keepdims=True)
        base = base + jnp.sum(onehot, axis=1, keepdims=True)
        packed = cls.astype(jnp.int32) * 65536 + rank.astype(jnp.int32)
        info_ref[:, rows] = packed
    cnt_ref[...] = jnp.broadcast_to(base, cnt_ref.shape)

    @pl.when(i == pl.num_programs(0) - 1)
    def _():
        cout_ref[...] = cnt_ref[...]


def _tri(tm):
    return jnp.asarray(np.triu(np.ones((tm, tm), np.float32))).astype(BF16)


def _group_spec(width, tm, first_tile, n_tiles, array_tile_off=0):
    return pl.BlockSpec((tm, width),
                        lambda i, *_: (jnp.clip(i - first_tile, 0, n_tiles - 1) + array_tile_off, 0))


def _post_mix(group_mixes, group_x, w_out_bf16, mod, layer, ln_w, ln_b, rw_hi, rw_lo, router_bias, cond_row, tm):
    group_tiles = tuple(m[0].shape[0] // tm for m in group_mixes)
    n_tiles = sum(group_tiles)
    total_rows = n_tiles * tm
    row = pl.BlockSpec((tm, D_MODEL), lambda i: (i, 0))
    vec = pl.BlockSpec((1, D_MODEL), lambda i: (0, 0))
    cnt = pl.BlockSpec((32, LANES), lambda i: (0, 0))
    in_specs, args = [], []
    first = 0
    for mixes, (x, x_row_off), nt in zip(group_mixes, group_x, group_tiles):
        for m in mixes:
            in_specs.append(_group_spec(m.shape[1], tm, first, nt))
            args.append(m)
        in_specs.append(_group_spec(D_MODEL, tm, first, nt, x_row_off // tm))
        args.append(x)
        first += nt
    in_specs += [
        pl.BlockSpec((D_MODEL, D_MODEL), lambda i: (0, 0), pipeline_mode=pl.Buffered(1)),
        _mod_spec(layer, 2, lambda i: cond_row(i, tm)), _mod_spec(layer, 3, lambda i: cond_row(i, tm)),
        _mod_spec(layer, 4, lambda i: cond_row(i, tm)),
        vec, vec,
        pl.BlockSpec((D_MODEL, LANES), lambda i: (0, 0)),
        pl.BlockSpec((D_MODEL, 2 * LANES), lambda i: (0, 0)),
        pl.BlockSpec(memory_space=pltpu.SMEM),
        pl.BlockSpec((POST_SUB, POST_SUB), lambda i: (0, 0)),
    ]
    args += [w_out_bf16, mod, mod, mod, ln_w.reshape(1, -1), ln_b.reshape(1, -1), rw_hi, rw_lo, router_bias,
             _tri(POST_SUB)]
    return pl.pallas_call(
        functools.partial(_post_kernel, n_mix=len(group_mixes[0]), group_tiles=group_tiles),
        grid=(n_tiles,),
        in_specs=in_specs,
        out_specs=[row, pl.BlockSpec((tm * TOK_ROWS, LANES), lambda i: (i, 0)),
                   pl.BlockSpec((1, tm), lambda i: (0, i)), cnt],
        out_shape=[jax.ShapeDtypeStruct((total_rows, D_MODEL), F32),
                   jax.ShapeDtypeStruct((total_rows * TOK_ROWS, LANES), F32),
                   jax.ShapeDtypeStruct((1, total_rows), jnp.int32), jax.ShapeDtypeStruct((32, LANES), F32)],
        scratch_shapes=[pltpu.VMEM((32, LANES), F32), pltpu.VMEM((tm, D_MODEL), F32),
                        pltpu.VMEM((D_MODEL, D_MODEL), BF16)],
        compiler_params=_cparams(("arbitrary",)),
        name="post_mix",
    )(*args)


def _plan_kernel(cnt_ref, ea_ref, eb_ref, fl_ref, rs_ref, *, n_tiles):
    start = jnp.int32(0)
    prev_a = jnp.int32(-1)
    prev_b = jnp.int32(-1)
    par_a = jnp.int32(0)
    par_b = jnp.int32(0)
    for c in range(N_CLASSES):
        n = cnt_ref[c]
        tiles = lax.shift_right_logical(n + (MOE_TM - 1), MOE_SHIFT)
        row0 = start * MOE_TM
        rs_ref[c] = row0
        g, pr = divmod(c, N_PAIRS)
        a = g * GROUP_SIZE + PAIR_SLOTS[pr][0]
        b = g * GROUP_SIZE + PAIR_SLOTS[pr][1]
        new_a = (prev_a != a).astype(jnp.int32)
        new_b = (prev_b != b).astype(jnp.int32)
        has = tiles > 0
        par_a = jnp.where(has, par_a ^ new_a, par_a)
        par_b = jnp.where(has, par_b ^ new_b, par_b)
        first = 1 + 4 * new_a + 8 * new_b
        common = 16 * par_a + 32 * par_b

        def tile_body(k, _, start=start, a=a, b=b, first=first, n=n, common=common):
            t = start + k
            ea_ref[t] = a
            eb_ref[t] = b
            half = 2 * (n - k * MOE_TM <= MOE_TM // 2).astype(jnp.int32)
            fl_ref[t] = jnp.where(k == 0, first, 1) + half + common
            return 0

        lax.fori_loop(0, tiles, tile_body, 0)
        prev_a = jnp.where(has, a, prev_a)
        prev_b = jnp.where(has, b, prev_b)
        start = start + tiles
    for c in range(N_CLASSES, 31):
        rs_ref[c] = 0
    rs_ref[31] = start

    def idle_body(t, _):
        ea_ref[t] = prev_a
        eb_ref[t] = prev_b
        fl_ref[t] = 0
        return 0

    lax.fori_loop(start, n_tiles, idle_body, 0)


def _moe_plan(counts, n_tiles):
    smem = pl.BlockSpec(memory_space=pltpu.SMEM)
    i32 = lambda n: jax.ShapeDtypeStruct((n,), jnp.int32)
    return pl.pallas_call(
        functools.partial(_plan_kernel, n_tiles=n_tiles),
        in_specs=[smem],
        out_specs=[smem] * 4,
        out_shape=[i32(n_tiles), i32(n_tiles), i32(n_tiles), i32(32)],
        name="moe_plan",
    )(counts)


def _class_row(packed, rs_ref):
    return rs_ref[lax.shift_right_logical(packed, 16)] + (packed & 0xFFFF)


def _dispatch_kernel(packed_ref, rs_ref, cnt_ref, h_ref, hs_ref, sem, fill_sem, *, tm, n_tiles):
    i = pl.program_id(0)
    used = rs_ref[31]

    def tile_rows(row, n=1):
        return pl.ds(pl.multiple_of(row * TOK_ROWS, TOK_ROWS), n * TOK_ROWS)

    def fill_copy(row, n):
        return pltpu.make_async_copy(h_ref.at[tile_rows(0, n)], hs_ref.at[tile_rows(row, n)], fill_sem)

    def fill(act):
        for c in range(N_CLASSES):
            n = cnt_ref[c]
            row = rs_ref[c] + n
            pad = (-n) & (MOE_TM - 1)
            for bit in reversed(range(MOE_SHIFT)):
                size = 1 << bit

                @pl.when((pad & size) != 0)
                def _(row=row, size=size):
                    act(fill_copy(row, size))
                row = row + (pad & size)
        lax.fori_loop(used, n_tiles, lambda t, _: (act(fill_copy(t * MOE_TM, MOE_TM)), 0)[1], 0)

    @pl.when(i == 0)
    def _():
        fill(lambda cp: cp.start())

    def body(r8, _):
        for j in range(8):
            r = r8 * 8 + j
            row = _class_row(packed_ref[0, i * tm + r], rs_ref)
            pltpu.make_async_copy(h_ref.at[tile_rows(r)], hs_ref.at[tile_rows(row)], sem).start(priority=j % 2)
        return 0

    lax.fori_loop(0, tm // 8, body, 0)
    pltpu.make_async_copy(h_ref, hs_ref.at[tile_rows(0, tm)], sem).wait()

    @pl.when(i == 0)
    def _():
        fill(lambda cp: cp.wait())


def _dispatch(h2_tiles, packed, plan, counts, n_tiles, tm=1024):
    T = packed.shape[1]
    assert tm >= MOE_TM and T % tm == 0
    grid_spec = pltpu.PrefetchScalarGridSpec(
        num_scalar_prefetch=3,
        grid=(T // tm,),
        in_specs=[pl.BlockSpec((tm * TOK_ROWS, LANES), lambda i, *_: (i, 0))],
        out_specs=pl.BlockSpec(memory_space=pl.ANY),
        scratch_shapes=[pltpu.SemaphoreType.DMA(()), pltpu.SemaphoreType.DMA(())],
    )
    return pl.pallas_call(
        functools.partial(_dispatch_kernel, tm=tm, n_tiles=n_tiles),
        grid_spec=grid_spec,
        out_shape=jax.ShapeDtypeStruct((n_tiles * MOE_TM * TOK_ROWS, LANES), F32),
        compiler_params=_cparams(("arbitrary",)),
        name="moe_dispatch",
    )(packed, plan[3], counts, h2_tiles)


def _moe_kernel(ea_ref, eb_ref, fl_ref, h_ref, rwt_ref, wga, wua, wda, wgb, wub, wdb, o_ref,
                sga, sua, sda, sgb, sub, sdb):
    s = pl.program_id(0)
    n_tiles = pl.num_programs(0) - 1
    nxt = fl_ref[jnp.minimum(s, n_tiles - 1)]
    stage = s < n_tiles

    @pl.when(jnp.logical_and(stage, (nxt & 4) != 0))
    def _():
        p = (nxt >> 4) & 1
        sga[p] = wga[...].astype(BF16)
        sua[p] = wua[...].astype(BF16)
        sda[p] = wda[...].astype(BF16)

    @pl.when(jnp.logical_and(stage, (nxt & 8) != 0))
    def _():
        p = (nxt >> 5) & 1
        sgb[p] = wgb[...].astype(BF16)
        sub[p] = wub[...].astype(BF16)
        sdb[p] = wdb[...].astype(BF16)

    t = jnp.maximum(s - 1, 0)
    flags = fl_ref[t]
    run = s > 0
    valid = jnp.logical_and(run, (flags & 1) != 0)
    pa = (flags >> 4) & 1
    pb = (flags >> 5) & 1

    def experts(rows):
        h = _load_token_tiles(h_ref, rows).astype(BF16)
        nt = (((1,), (1,)), ((), ()))
        score = jax.nn.sigmoid(lax.dot_general(h, rwt_ref[...], nt, preferred_element_type=F32))
        lane = lax.broadcasted_iota(jnp.int32, score.shape, 1)
        s_a = jnp.sum(jnp.where(lane == ea_ref[t], score, 0.0), axis=1, keepdims=True)
        s_b = jnp.sum(jnp.where(lane == eb_ref[t], score, 0.0), axis=1, keepdims=True)
        tot = s_a + s_b
        act_a = _silu(jnp.dot(h, sga[pa], preferred_element_type=F32)) \
            * jnp.dot(h, sua[pa], preferred_element_type=F32) * (s_a / tot)
        act_b = _silu(jnp.dot(h, sgb[pb], preferred_element_type=F32)) \
            * jnp.dot(h, sub[pb], preferred_element_type=F32) * (s_b / tot)
        _store_token_tiles(o_ref, jnp.dot(act_a.astype(BF16), sda[pa], preferred_element_type=F32)
                           + jnp.dot(act_b.astype(BF16), sdb[pb], preferred_element_type=F32))
        if rows < MOE_TM:
            o_ref[rows * TOK_ROWS:, :] = jnp.zeros(((MOE_TM - rows) * TOK_ROWS, LANES), F32)

    half = (flags & 2) != 0

    @pl.when(jnp.logical_and(valid, jnp.logical_not(half)))
    def _():
        experts(MOE_TM)

    @pl.when(jnp.logical_and(valid, half))
    def _():
        experts(MOE_TM // 2)

    @pl.when(jnp.logical_and(run, jnp.logical_not(valid)))
    def _():
        o_ref[...] = jnp.zeros_like(o_ref)


def _moe(hs, plan, layer, rwt_bf16, w_gate, w_up, w_down):
    ea, eb, flags, _ = plan
    n_tiles = ea.shape[0]
    ahead = lambda s: jnp.minimum(s, n_tiles - 1)
    wspec = lambda shape, which: pl.BlockSpec(
        (None, None) + shape, (lambda s, ea, eb, fl: (layer, ea[ahead(s)], 0, 0)) if which == 0
        else (lambda s, ea, eb, fl: (layer, eb[ahead(s)], 0, 0)))
    up = (D_MODEL, D_FF_EXPERT)
    dn = (D_FF_EXPERT, D_MODEL)
    tile = pl.BlockSpec((MOE_TM * TOK_ROWS, LANES), lambda s, ea, eb, fl: (jnp.maximum(s - 1, 0), 0))
    grid_spec = pltpu.PrefetchScalarGridSpec(
        num_scalar_prefetch=3,
        grid=(n_tiles + 1,),
        in_specs=[
            tile,
            pl.BlockSpec((N_EXPERTS, D_MODEL), lambda s, ea, eb, fl: (0, 0)),
            wspec(up, 0), wspec(up, 0), wspec(dn, 0), wspec(up, 1), wspec(up, 1), wspec(dn, 1),
        ],
        out_specs=tile,
        scratch_shapes=[pltpu.VMEM((2,) + up, BF16), pltpu.VMEM((2,) + up, BF16), pltpu.VMEM((2,) + dn, BF16),
                        pltpu.VMEM((2,) + up, BF16), pltpu.VMEM((2,) + up, BF16), pltpu.VMEM((2,) + dn, BF16)],
    )
    return pl.pallas_call(
        _moe_kernel,
        grid_spec=grid_spec,
        out_shape=jax.ShapeDtypeStruct((n_tiles * MOE_TM * TOK_ROWS, LANES), F32),
        compiler_params=_cparams(("arbitrary",)),
        name="moe",
    )(ea, eb, flags, hs, rwt_bf16, w_gate, w_up, w_down, w_gate, w_up, w_down)


def _gather_expert_rows(packed_ref, rs_ref, ys_ref, ybuf, sem, tm):
    i = pl.program_id(0)
    n_i = pl.num_programs(0)
    slot = i % 2

    def gather(tile, buf_slot):
        def body(r8, _):
            for j in range(8):
                r = r8 * 8 + j
                d = _class_row(packed_ref[0, tile * tm + r], rs_ref)
                pltpu.make_async_copy(ys_ref.at[pl.ds(pl.multiple_of(d * TOK_ROWS, TOK_ROWS), TOK_ROWS)],
                                      ybuf.at[buf_slot, pl.ds(pl.multiple_of(r * TOK_ROWS, TOK_ROWS), TOK_ROWS)],
                                      sem.at[buf_slot]).start(priority=j % 2)
            return 0
        lax.fori_loop(0, tm // 8, body, 0)

    @pl.when(i == 0)
    def _():
        gather(0, 0)

    @pl.when(i + 1 < n_i)
    def _():
        gather(jnp.minimum(i + 1, n_i - 1), 1 - slot)

    pltpu.make_async_copy(ys_ref.at[pl.ds(0, tm * TOK_ROWS)], ybuf.at[slot], sem.at[slot]).wait()
    return _load_token_tiles(ybuf, tm, slot)


def _fin_proj_kernel(packed_ref, rs_ref, x_ref, ys_ref, g_ref, lnw_ref, lnb_ref, sh_ref, sc_ref, w_f32_ref,
                     x2_ref, u_ref, ybuf, sem, w_ref, *, tm):
    _cast_once(w_f32_ref, w_ref)
    y = _gather_expert_rows(packed_ref, rs_ref, ys_ref, ybuf, sem, tm)
    x2 = _layer_norm(ALPHA * x_ref[...] + g_ref[...] * y, lnw_ref[...], lnb_ref[...])
    x2_ref[...] = x2
    h = (x2 * (1.0 + sc_ref[...]) + sh_ref[...]).astype(BF16)
    for s in range(u_ref.shape[1] // SEC):
        u_ref[:, s * SEC:(s + 1) * SEC] = jnp.dot(h, w_ref[:, s * SEC:(s + 1) * SEC],
                                                  preferred_element_type=F32).astype(BF16)


def _fin_proj(x1, ys, packed, row_start, mod, layer, ln_w, ln_b, w_next, cond_row, tm=512):
    T = x1.shape[0]
    N = w_next.shape[1]
    row = pl.BlockSpec((tm, D_MODEL), lambda i, *_: (i, 0))
    vec = pl.BlockSpec((1, D_MODEL), lambda i, *_: (0, 0))
    crow = lambda i: cond_row(i, tm)
    grid_spec = pltpu.PrefetchScalarGridSpec(
        num_scalar_prefetch=2,
        grid=(T // tm,),
        in_specs=[row, pl.BlockSpec(memory_space=pl.ANY), _mod_spec(layer, 5, crow), vec, vec,
                  _mod_spec(layer + 1, 0, crow), _mod_spec(layer + 1, 1, crow),
                  pl.BlockSpec((D_MODEL, N), lambda i, *_: (0, 0), pipeline_mode=pl.Buffered(1))],
        out_specs=[row, pl.BlockSpec((tm, N), lambda i, *_: (i, 0))],
        scratch_shapes=[pltpu.VMEM((2, tm * TOK_ROWS, LANES), F32), pltpu.SemaphoreType.DMA((2,)),
                        pltpu.VMEM((D_MODEL, N), BF16)],
    )
    return pl.pallas_call(
        functools.partial(_fin_proj_kernel, tm=tm),
        grid_spec=grid_spec,
        out_shape=[jax.ShapeDtypeStruct((T, D_MODEL), F32), jax.ShapeDtypeStruct((T, N), BF16)],
        compiler_params=_cparams(("arbitrary",)),
        name="post_moe_in_proj",
    )(packed, row_start, x1, ys, mod, ln_w.reshape(1, -1), ln_b.reshape(1, -1), mod, mod, w_next)


def _fin_kernel(packed_ref, rs_ref, x_ref, ys_ref, g_ref, lnw_ref, lnb_ref, *rest, tm, group_tiles):
    o_refs = rest[:len(group_tiles)]
    ybuf, sem = rest[len(group_tiles):]
    i = pl.program_id(0)
    y = _layer_norm(ALPHA * x_ref[...] + g_ref[...] * _gather_expert_rows(packed_ref, rs_ref, ys_ref, ybuf, sem, tm),
                    lnw_ref[...], lnb_ref[...])
    first = 0
    for o_ref, nt in zip(o_refs, group_tiles):
        @pl.when(jnp.logical_and(i >= first, i < first + nt))
        def _(o_ref=o_ref):
            o_ref[...] = y
        first += nt


def _fin(x1, ys, packed, row_start, mod, layer, ln_w, ln_b, cond_row, group_rows, tm=256):
    group_tiles = tuple(n // tm for n in group_rows)
    row = pl.BlockSpec((tm, D_MODEL), lambda i, *_: (i, 0))
    vec = pl.BlockSpec((1, D_MODEL), lambda i, *_: (0, 0))
    out_specs, first = [], 0
    for nt in group_tiles:
        out_specs.append(_group_spec(D_MODEL, tm, first, nt))
        first += nt
    grid_spec = pltpu.PrefetchScalarGridSpec(
        num_scalar_prefetch=2,
        grid=(sum(group_tiles),),
        in_specs=[row, pl.BlockSpec(memory_space=pl.ANY), _mod_spec(layer, 5, lambda i: cond_row(i, tm)), vec, vec],
        out_specs=out_specs,
        scratch_shapes=[pltpu.VMEM((2, tm * TOK_ROWS, LANES), F32), pltpu.SemaphoreType.DMA((2,))],
    )
    return pl.pallas_call(
        functools.partial(_fin_kernel, tm=tm, group_tiles=group_tiles),
        grid_spec=grid_spec,
        out_shape=[jax.ShapeDtypeStruct((n, D_MODEL), F32) for n in group_rows],
        compiler_params=_cparams(("arbitrary",)),
        name="post_moe",
    )(packed, row_start, x1, ys, mod, ln_w.reshape(1, -1), ln_b.reshape(1, -1))


def _filt_kernel(z_ref, w1_ref, b1_ref, w2_ref, b2_ref, fr_ref, w3_ref, dl_ref, o_ref):
    i = pl.program_id(0)
    z = z_ref[...]
    fr = fr_ref[...]
    a = jnp.sin(fr * (jnp.dot(z, w1_ref[...], precision=HIGHEST, preferred_element_type=F32) + b1_ref[...]))
    a = jnp.sin(fr * (jnp.dot(a, w2_ref[...], precision=HIGHEST, preferred_element_type=F32) + b2_ref[...]))
    filt = _dot_3pass(a, w3_ref[...])
    window = jnp.exp(-z[:, 0:1] * dl_ref[...]) + HY_SHIFT
    rows = i * z.shape[0] + lax.broadcasted_iota(jnp.int32, filt.shape, 0)
    o_ref[...] = jnp.where(rows == 0, 0.0, filt * window)


def _hyena_taps(seq_len, w1, b1, w2, b2, w3, freq):
    L = seq_len
    t = np.linspace(0.0, 1.0, L)[:, None]
    bands = np.linspace(1e-4, HY_BANDS - 1, HY_BANDS)
    ang = 2.0 * math.pi * bands[None, :] * np.arange(L)[:, None] / L
    z = np.concatenate([t, np.cos(ang), -np.sin(ang)], axis=-1)
    offs = np.minimum(np.abs(np.arange(2 * L) - L), L - 1)
    z2_np = np.zeros((2 * L, LANES), np.float32)
    z2_np[:, :z.shape[1]] = z[offs]
    z2 = jnp.asarray(z2_np)
    pad_c = lambda a: jnp.zeros((a.shape[0], LANES), F32).at[:, :a.shape[1]].set(a)
    pad_r = lambda a: jnp.zeros((LANES, a.shape[1]), F32).at[:a.shape[0], :].set(a)
    w1p = pad_r(pad_c(w1))
    w2p = pad_r(pad_c(w2))
    w3p = pad_r(w3)
    b1p, b2p, frp = pad_c(b1[None, :]), pad_c(b2[None, :]), pad_c(freq[None, :])
    deltas = jnp.asarray(np.abs(np.linspace(math.log(HY_DECAY_TARGET) / HY_SLOW, math.log(HY_DECAY_TARGET) / HY_FAST,
                                            D_MODEL))[None, :].astype(np.float32))
    rb = min(1024, L)
    cbf = D_MODEL
    ncb = D_MODEL // cbf
    nrb_back = L // rb
    sq = pl.BlockSpec((LANES, LANES), lambda i, j: (0, 0))
    vec = pl.BlockSpec((1, LANES), lambda i, j: (0, 0))
    return pl.pallas_call(
        _filt_kernel,
        grid=(2 * L // rb, ncb),
        in_specs=[
            pl.BlockSpec((rb, LANES), lambda i, j: (i, 0)),
            sq, vec, sq, vec, vec,
            pl.BlockSpec((LANES, cbf), lambda i, j: (0, jnp.where(i < nrb_back, ncb + j, j))),
            pl.BlockSpec((1, cbf), lambda i, j: (0, j)),
        ],
        out_specs=pl.BlockSpec((rb, cbf), lambda i, j: (i, j)),
        out_shape=jax.ShapeDtypeStruct((2 * L, D_MODEL), F32),
        compiler_params=_cparams(("arbitrary", "arbitrary")),
        name="hyena_taps",
    )(z2, w1p, b1p, w2p, b2p, frp, w3p, deltas)


def _dft_mats(cb):
    n = 2 * cb
    m = np.arange(cb)
    f = np.arange(cb)
    ang = 2.0 * np.pi * ((f[:, None] * m[None, :]) % n) / n
    fwd = np.concatenate([np.cos(ang), -np.sin(ang)], axis=0)
    fwd[cb, :] = np.where(m % 2 == 0, 1.0, -1.0)
    coef = np.where(f == 0, 1.0, 2.0)[None, :] / n
    inv = np.concatenate([coef * np.cos(ang.T), -coef * np.sin(ang.T)], axis=1)
    inv[:, cb] = np.where(m % 2 == 0, 1.0, -1.0) / n
    return fwd.astype(np.float32), inv.astype(np.float32)


def _hconv_kernel(x0_ref, x1_ref, v_ref, cw0_ref, cw1_ref, cw2_ref, cb0_ref, cb1_ref, cb2_ref,
                  taps_ref, fb_ref, fwd_ref, inv_ref, o_ref,
                  hs_ref, stage_ref, w32_ref, w_ref, x0c_ref, u_ref, y_ref, *, seq_len, blk, n_seq):
    L = seq_len
    nb = L // blk
    cw = x0_ref.shape[1]
    RC = 256
    bi = pl.program_id(1)
    row0 = lax.broadcasted_iota(jnp.int32, (RC, cw), 0) == 0

    @pl.when(bi == 0)
    def _():
        rows = lax.broadcasted_iota(jnp.int32, (2 * blk, 1), 0)
        sign = jnp.where(rows % 2 == 0, 1.0, -1.0)
        real_row = rows <= blk
        prev = None
        for jb in range(2 * nb):
            cur = jnp.dot(fwd_ref[...], taps_ref[jb * blk:(jb + 1) * blk, :].astype(BF16),
                          preferred_element_type=F32)
            if prev is not None:
                first_tap = taps_ref[(jb - 1) * blk:(jb - 1) * blk + 1, :].astype(BF16).astype(F32)
                hs_ref[jb - 1] = cur + sign * (prev - jnp.where(real_row, first_tap, 0.0))
            prev = cur

    zeros8 = jnp.zeros((8, cw), F32)
    for s in range(n_seq):
        r0 = s * L

        def short_conv(k, src_ref, cw_ref, cb_ref, store):
            stage_ref[k, 0:8, :] = zeros8
            stage_ref[k, 8 + L:16 + L, :] = zeros8
            for c in range(L // RC):
                stage_ref[k, 8 + c * RC:8 + (c + 1) * RC, :] = \
                    src_ref[r0 + c * RC:r0 + (c + 1) * RC, :].astype(F32)
            w = cw_ref[...]
            for c in range(L // RC):
                lo = stage_ref[k, 7 + c * RC:7 + (c + 1) * RC, :]
                mid = stage_ref[k, 8 + c * RC:8 + (c + 1) * RC, :]
                hi = stage_ref[k, 9 + c * RC:9 + (c + 1) * RC, :]
                store(c, lo * w[0:1, :] + mid * w[1:2, :] + hi * w[2:3, :] + cb_ref[...])

        def st_x1(c, val):
            w32_ref[c * RC:(c + 1) * RC, :] = val

        def st_v(c, val):
            w_ref[c * RC:(c + 1) * RC, :] = (w32_ref[c * RC:(c + 1) * RC, :] * val).astype(BF16)

        def st_x0(c, val):
            x0c_ref[c * RC:(c + 1) * RC, :] = val.astype(BF16)

        short_conv(0, x1_ref, cw1_ref, cb1_ref, st_x1)
        short_conv(1, v_ref, cw2_ref, cb2_ref, st_v)
        short_conv(2, x0_ref, cw0_ref, cb0_ref, st_x0)

        for j in range(nb):
            u_ref[j] = jnp.dot(fwd_ref[...], w_ref[j * blk:(j + 1) * blk, :], preferred_element_type=F32)

        for i in range(nb):
            for c in range(blk // RC):
                re = None
                im = None
                for j in range(nb):
                    k = i - j + nb - 1
                    a = u_ref[j, c * RC:(c + 1) * RC, :]
                    b = u_ref[j, blk + c * RC:blk + (c + 1) * RC, :]
                    hr = hs_ref[k, c * RC:(c + 1) * RC, :]
                    hi = hs_ref[k, blk + c * RC:blk + (c + 1) * RC, :]
                    bb = b * hi
                    if c == 0:
                        t_re = a * hr - jnp.where(row0, 0.0, bb)
                        t_im = jnp.where(row0, bb, a * hi + b * hr)
                    else:
                        t_re = a * hr - bb
                        t_im = a * hi + b * hr
                    re = t_re if re is None else re + t_re
                    im = t_im if im is None else im + t_im
                y_ref[i, c * RC:(c + 1) * RC, :] = re.astype(BF16)
                y_ref[i, blk + c * RC:blk + (c + 1) * RC, :] = im.astype(BF16)
            conv = jnp.dot(inv_ref[...], y_ref[i], preferred_element_type=F32)
            sl = slice(i * blk, (i + 1) * blk)
            z = x0c_ref[sl, :].astype(F32) * (conv + w_ref[sl, :].astype(F32) * fb_ref[...])
            o_ref[r0 + i * blk:r0 + (i + 1) * blk, :] = z.astype(BF16)


def _hyena_conv(u, conv_w, conv_b, taps, filt_bias, batch, seq_len, blk, n_seq, row_off=0, cw=256):
    T = batch * seq_len
    L = seq_len
    nb = L // blk
    ncw = D_MODEL // cw
    fwd_np, inv_np = _dft_mats(blk)
    fwd = jnp.asarray(fwd_np).astype(BF16)
    inv = jnp.asarray(inv_np).astype(BF16)
    rows = n_seq * L
    assert row_off % rows == 0
    boff = row_off // rows
    sec = lambda s: pl.BlockSpec((rows, cw), lambda c, b: (b + boff, s * ncw + c))
    cws = lambda s: pl.BlockSpec((3, cw), lambda c, b: (0, s * ncw + c))
    cbs = lambda s: pl.BlockSpec((1, cw), lambda c, b: (0, s * ncw + c))
    return pl.pallas_call(
        functools.partial(_hconv_kernel, seq_len=L, blk=blk, n_seq=n_seq),
        grid=(ncw, batch // n_seq),
        in_specs=[sec(0), sec(1), sec(2), cws(0), cws(1), cws(2), cbs(0), cbs(1), cbs(2),
                  pl.BlockSpec((2 * L, cw), lambda c, b: (0, c), pipeline_mode=pl.Buffered(1)),
                  pl.BlockSpec((1, cw), lambda c, b: (0, c)),
                  pl.BlockSpec((2 * blk, blk), lambda c, b: (0, 0), pipeline_mode=pl.Buffered(1)),
                  pl.BlockSpec((blk, 2 * blk), lambda c, b: (0, 0), pipeline_mode=pl.Buffered(1))],
        out_specs=pl.BlockSpec((rows, cw), lambda c, b: (b, c)),
        out_shape=jax.ShapeDtypeStruct((T, D_MODEL), BF16),
        scratch_shapes=[
            pltpu.VMEM((2 * nb - 1, 2 * blk, cw), F32),
            pltpu.VMEM((3, L + 16, cw), F32),
            pltpu.VMEM((L, cw), F32),
            pltpu.VMEM((L, cw), BF16),
            pltpu.VMEM((L, cw), BF16),
            pltpu.VMEM((nb, 2 * blk, cw), F32),
            pltpu.VMEM((nb, 2 * blk, cw), BF16),
        ],
        compiler_params=_cparams(("arbitrary", "arbitrary")),
        name="hyena_conv",
    )(u, u, u, conv_w, conv_w, conv_w, conv_b.reshape(1, -1), conv_b.reshape(1, -1), conv_b.reshape(1, -1),
      taps, filt_bias.reshape(1, -1), fwd, inv)


def kernel(x_prompt, x_sample, cache_diff_k, cache_diff_v, state_ret_fwd, state_ret_bwd, c, c_ctx, ada_w, ada_b, ln_w, ln_b, ev_w_in, ev_w_out, ret_decay_fwd, ret_decay_bwd, ret_gn_w, diff_lambda, diff_subln_w, hy_w_in, hy_conv_w, hy_conv_b, hy_ffn_w1, hy_ffn_b1, hy_ffn_w2, hy_ffn_b2, hy_ffn_w3, hy_freq, hy_filter_bias, hy_w_out, router_w, router_bias, moe_w_gate, moe_w_up, moe_w_down):
    B, S, D = x_prompt.shape
    DB, DS, _ = x_sample.shape
    PAST = cache_diff_k.shape[2]
    TP, TS = B * S, DB * DS
    T_ALL = TP + TS
    assert D == D_MODEL and 1 + DB <= COND_ROWS and T_ALL < 65536

    cond8 = jnp.zeros((COND_ROWS, D), F32).at[0].set(c_ctx).at[1:1 + DB].set(c)
    mod = _ada_mod(cond8, ada_w, ada_b).reshape(DEPTH * COND_ROWS * N_MOD, 1, D)

    tm = 512
    groups = [
        dict(x=x_prompt.reshape(TP, D), x_off=0, batch=B, seq=S, off=0, cond_row=lambda i, tm: 0),
        dict(x=x_sample.reshape(TS, D), x_off=0, batch=DB, seq=DS, off=TP,
             cond_row=lambda i, tm: 1 + (i * tm) // DS),
    ]
    u_next = None

    def cond_row_all(i, tm):
        return jnp.where(i * tm < TP, 0, 1 + (i * tm - TP) // DS)

    rw_pad = jnp.zeros((D, LANES), F32).at[:, :N_EXPERTS].set(router_w.astype(F32))
    rw_hi = rw_pad.astype(BF16)
    rw_lo = jnp.concatenate([rw_hi, (rw_pad - rw_hi.astype(F32)).astype(BF16)], axis=1)
    rwt = router_w.T.astype(BF16)
    rbias = router_bias.astype(F32)
    n_tiles = T_ALL // MOE_TM + N_CLASSES
    outs = {}

    for l in range(DEPTH):
        mixes = []
        if l % 2 == 0:
            e = l // 2
            w_in = ev_w_in[e]
            w_out = ev_w_out[e]
            lg = jnp.stack([jnp.log1p(-jnp.exp2(ret_decay_fwd[e].astype(F32))),
                            jnp.log1p(-jnp.exp2(ret_decay_bwd[e].astype(F32)))])
            lam_init = 0.8 - 0.6 * math.exp(-0.3 * l)
            lq1, lk1, lq2, lk2 = diff_lambda[e].astype(F32)
            lam = (jnp.exp(jnp.sum(lq1 * lk1)) - jnp.exp(jnp.sum(lq2 * lk2)) + lam_init).reshape(1)
            kscale = RET_DK ** -0.5
            for gi, g in enumerate(groups):
                tm_in = tm
                crow = functools.partial(g["cond_row"], tm=tm_in)
                if gi == 0:
                    secs = (("none", 1.0, False), ("none", kscale, False), ("none", 1.0, False),
                            ("none", 1.0, False), ("none", 1.0, False), ("none", 1.0, True),
                            ("none", 1.0, True))
                    proj, kd, vd = _in_proj(g["x"], mod, l, w_in, secs, crow, tm_in, tm=tm_in)
                    outs.setdefault("kd", []).append(kd.reshape(B, 1, S, H_DIFF, 2 * DIFF_QK))
                    outs.setdefault("vd", []).append(vd.reshape(B, 1, S, H_DIFF, DIFF_V))
                    ret, sf, sb = _retention(proj, lg, ret_gn_w[e], g["batch"], g["seq"], write_state=True,
                                             heads=H_RET)
                    outs.setdefault("sf", []).append(sf.reshape(B, 1, H_RET, RET_DK, RET_DV))
                    outs.setdefault("sb", []).append(sb.reshape(B, 1, H_RET, RET_DK, RET_DV))
                    att = _diff_attention(proj, lam, diff_subln_w[e], lam_init, g["batch"], g["seq"],
                                          tq=min(256, g["seq"]), heads=H_DIFF)
                else:
                    secs = (("ret", 1.0, False), ("ret", kscale, False), ("none", 1.0, False),
                            ("none", 1.0, False), ("diff", 1.0, False), ("diff", 1.0, False),
                            ("none", 1.0, False))
                    tabs = {"ret": _rope_tables(g["seq"], RET_DK), "diff": _rope_tables(g["seq"], DIFF_QK)}
                    (proj,) = _in_proj(g["x"], mod, l, w_in, secs, crow, g["seq"], rope_tabs=tabs, tm=tm_in)
                    ret, = _retention(proj, lg, ret_gn_w[e], g["batch"], g["seq"],
                                      states=(state_ret_fwd[:, e], state_ret_bwd[:, e]))
                    ctx = (cache_diff_k[:, e].reshape(DB, PAST, H_DIFF * 2 * DIFF_QK),
                           cache_diff_v[:, e].reshape(DB, PAST, H_DIFF * DIFF_V))
                    att = _diff_attention(proj, lam, diff_subln_w[e], lam_init, g["batch"], g["seq"], ctx=ctx,
                                          tq=1024, heads=1)
                mixes.append((ret, att))
        else:
            o = l // 2
            w_in = hy_w_in[o]
            w_out = hy_w_out[o]
            secs = (("none", 1.0, False),) * (3 * D // SEC)
            for gi, g in enumerate(groups):
                if u_next is None:
                    crow = functools.partial(g["cond_row"], tm=tm)
                    (u,) = _in_proj(g["x"], mod, l, w_in, secs, crow, tm, tm=tm)
                    u_off = 0
                else:
                    u, u_off = u_next, g["off"]
                taps = _hyena_taps(g["seq"], hy_ffn_w1[o], hy_ffn_b1[o], hy_ffn_w2[o], hy_ffn_b2[o],
                                   hy_ffn_w3[o], hy_freq[o])
                blk = min(g["seq"], 512)
                n_seq = max(1, 1024 // g["seq"])
                z = _hyena_conv(u, hy_conv_w[o], hy_conv_b[o], taps, hy_filter_bias[o], g["batch"], g["seq"],
                                blk, n_seq, row_off=u_off)
                mixes.append((z,))

        x1_all, h2_all, info_all, counts = _post_mix(mixes, [(g["x"], g["x_off"]) for g in groups], w_out, mod, l, ln_w[l, 0],
                                                     ln_b[l, 0], rw_hi, rw_lo, rbias, cond_row_all, tm)
        packed = info_all
        counts_i = counts[:, 0].astype(jnp.int32)
        plan = _moe_plan(counts_i, n_tiles)
        hs = _dispatch(h2_all, packed, plan, counts_i, n_tiles)
        ys = _moe(hs, plan, l, rwt, moe_w_gate, moe_w_up, moe_w_down)
        if l + 1 < DEPTH and (l + 1) % 2 == 1:
            x_all, u_next = _fin_proj(x1_all, ys, packed, plan[3], mod, l, ln_w[l, 1], ln_b[l, 1],
                                      hy_w_in[(l + 1) // 2], cond_row_all, tm=tm)
            for g in groups:
                g["x"], g["x_off"] = x_all, g["off"]
        else:
            u_next = None
            xs = _fin(x1_all, ys, packed, plan[3], mod, l, ln_w[l, 1], ln_b[l, 1], cond_row_all, (TP, TS))
            for g, x in zip(groups, xs):
                g["x"], g["x_off"] = x, 0

    y_prompt = groups[0]["x"].reshape(B, S, D)
    y_sample = groups[1]["x"].reshape(DB, DS, D)
    cat = lambda xs: xs[0] if len(xs) == 1 else jnp.concatenate(xs, axis=1)
    return (y_prompt, y_sample, cat(outs["kd"]), cat(outs["vd"]), cat(outs["sf"]), cat(outs["sb"]))
```

```python
import functools
import math

import numpy as np
import jax
import jax.numpy as jnp
from jax import lax
from jax.experimental import pallas as pl
from jax.experimental.pallas import tpu as pltpu

F32 = jnp.float32
BF16 = jnp.bfloat16
HIGHEST = lax.Precision.HIGHEST

D_MODEL = 1024
DEPTH = 2
GRID_W = 64
H_RET = 4
RET_DK = 128
RET_DV = 128
RET_CHUNK = 128
H_DIFF = 4
DIFF_QK = 64
DIFF_V = 128
ROPE_BASE = 10000.0
HY_BANDS = 16
HY_FH = 64
HY_DECAY_TARGET = 1e-2
HY_FAST = 0.3
HY_SLOW = 1.5
HY_SHIFT = 0.05
N_EXPERTS = 16
N_GROUPS = 4
GROUP_SIZE = N_EXPERTS // N_GROUPS
D_FF_EXPERT = 512
ALPHA = (2 * DEPTH) ** 0.25
LN_EPS = 1e-5

LANES = 128
SEC = 512
COND_ROWS = 8
N_MOD = 6
VMEM_LIMIT = 50 * 1024 * 1024

PAIR_SLOTS = ((0, 1), (2, 1), (2, 3), (0, 3), (0, 2), (1, 3))
N_PAIRS = len(PAIR_SLOTS)
N_CLASSES = N_GROUPS * N_PAIRS
MOE_TM = 256
MOE_SHIFT = MOE_TM.bit_length() - 1
POST_SUB = 512
TOK_ROWS = D_MODEL // LANES


def _cparams(sem):
    return pltpu.CompilerParams(dimension_semantics=sem, vmem_limit_bytes=VMEM_LIMIT)


def _silu(x):
    return x * jax.nn.sigmoid(x)


def _store_token_tiles(ref, x):
    n = x.shape[0]
    for j in range(TOK_ROWS):
        ref[pl.ds(j, n, stride=TOK_ROWS), :] = x[:, j * LANES:(j + 1) * LANES]


def _load_token_tiles(ref, n, slot=None):
    idx = () if slot is None else (slot,)
    return jnp.concatenate([ref[idx + (pl.ds(j, n, stride=TOK_ROWS), slice(None))] for j in range(TOK_ROWS)],
                           axis=1)


def _layer_norm(x, w, b):
    mu = jnp.mean(x, axis=-1, keepdims=True)
    xc = x - mu
    var = jnp.mean(xc * xc, axis=-1, keepdims=True)
    return xc * lax.rsqrt(var + LN_EPS) * w + b


def _dot_3pass(a, w):
    a_hi = a.astype(BF16)
    a_lo = (a - a_hi.astype(F32)).astype(BF16)
    w_hi = w.astype(BF16)
    w_lo = (w - w_hi.astype(F32)).astype(BF16)
    return (jnp.dot(a_hi, w_hi, preferred_element_type=F32) + jnp.dot(a_lo, w_hi, preferred_element_type=F32)
            + jnp.dot(a_hi, w_lo, preferred_element_type=F32))


def _ada_kernel(c_ref, w_ref, b_ref, o_ref):
    o_ref[...] = _dot_3pass(_silu(c_ref[...]), w_ref[...]) + b_ref[...]


def _ada_mod(cond8, ada_w, ada_b):
    tn = 1024
    nj = ada_w.shape[2] // tn
    return pl.pallas_call(
        _ada_kernel,
        grid=(DEPTH, nj),
        in_specs=[
            pl.BlockSpec((COND_ROWS, D_MODEL), lambda l, j: (0, 0)),
            pl.BlockSpec((None, D_MODEL, tn), lambda l, j: (l, 0, j)),
            pl.BlockSpec((None, 1, tn), lambda l, j: (l, 0, j)),
        ],
        out_specs=pl.BlockSpec((None, COND_ROWS, tn), lambda l, j: (l, 0, j)),
        out_shape=jax.ShapeDtypeStruct((DEPTH, COND_ROWS, ada_w.shape[2]), F32),
        compiler_params=_cparams(("arbitrary", "arbitrary")),
        name="ada_mod",
    )(cond8, ada_w, ada_b.reshape(DEPTH, 1, -1))


def _mod_spec(layer, chunk, row_of_tile):
    def imap(i, *_):
        return ((layer * COND_ROWS + row_of_tile(i)) * N_MOD + chunk, 0, 0)
    return pl.BlockSpec((None, 1, D_MODEL), imap)


def _rope(a, tabs, quarter):
    c, sa, sb = tabs
    out = []
    for hb in range(a.shape[1] // LANES):
        blk = a[:, hb * LANES:(hb + 1) * LANES]
        up = pltpu.roll(blk, LANES - quarter, axis=1)
        dn = pltpu.roll(blk, quarter, axis=1)
        out.append(blk * c + up * sa + dn * sb)
    return jnp.concatenate(out, axis=1)


def _cast_once(w_ref, w_bf16_ref):
    @pl.when(pl.program_id(0) == 0)
    def _():
        for c in range(0, w_ref.shape[1], SEC):
            w_bf16_ref[:, c:c + SEC] = w_ref[:, c:c + SEC].astype(BF16)


def _in_kernel(*refs, secs, n_f32_out):
    x_ref, sh_ref, sc_ref, w_ref = refs[:4]
    pos = 4
    tabs = {}
    for kind in ("ret", "diff"):
        if any(s[0] == kind for s in secs):
            tabs[kind] = tuple(r[...] for r in refs[pos:pos + 3])
            pos += 3
    o_ref = refs[pos]
    f32_refs = refs[pos + 1:pos + 1 + n_f32_out]
    w_bf16_ref = refs[-1]
    _cast_once(w_ref, w_bf16_ref)
    h = (x_ref[...] * (1.0 + sc_ref[...]) + sh_ref[...]).astype(BF16)
    k32 = 0
    for s, (kind, scale, want_f32) in enumerate(secs):
        acc = jnp.dot(h, w_bf16_ref[:, s * SEC:(s + 1) * SEC], preferred_element_type=F32)
        if scale != 1.0:
            acc = acc * scale
        if kind == "ret":
            acc = _rope(acc, tabs["ret"], RET_DK // 4)
        elif kind == "diff":
            acc = _rope(acc, tabs["diff"], DIFF_QK // 4)
        o_ref[:, s * SEC:(s + 1) * SEC] = acc.astype(BF16)
        if want_f32:
            for hb in range(SEC // LANES):
                f32_refs[k32][:, hb, :] = acc[:, hb * LANES:(hb + 1) * LANES]
            k32 += 1
    assert k32 == n_f32_out


def _in_proj(x2d, mod, layer, w_f32, secs, row_of_tile, seq_len, rope_tabs=None, tm=512):
    T = x2d.shape[0]
    N = w_f32.shape[1]
    assert N == SEC * len(secs) and T % tm == 0 and seq_len % tm == 0
    tiles_per_seq = seq_len // tm
    in_specs = [
        pl.BlockSpec((tm, D_MODEL), lambda i: (i, 0)),
        _mod_spec(layer, 0, row_of_tile),
        _mod_spec(layer, 1, row_of_tile),
        pl.BlockSpec((D_MODEL, N), lambda i: (0, 0), pipeline_mode=pl.Buffered(1)),
    ]
    args = [x2d, mod, mod, w_f32]
    for kind in ("ret", "diff"):
        if any(s[0] == kind for s in secs):
            for t in rope_tabs[kind]:
                in_specs.append(pl.BlockSpec((tm, LANES), lambda i: (i % tiles_per_seq, 0)))
                args.append(t)
    n_f32 = sum(1 for s in secs if s[2])
    heads = SEC // LANES
    out_shape = [jax.ShapeDtypeStruct((T, N), BF16)] + [jax.ShapeDtypeStruct((T, heads, LANES), F32)] * n_f32
    out_specs = ([pl.BlockSpec((tm, N), lambda i: (i, 0))]
                 + [pl.BlockSpec((tm, heads, LANES), lambda i: (i, 0, 0))] * n_f32)
    return pl.pallas_call(
        functools.partial(_in_kernel, secs=secs, n_f32_out=n_f32),
        grid=(T // tm,),
        in_specs=in_specs,
        out_specs=out_specs,
        out_shape=out_shape,
        scratch_shapes=[pltpu.VMEM((D_MODEL, N), BF16)],
        compiler_params=_cparams(("arbitrary",)),
        name="in_proj",
    )(*args)


def _rope_tables(seq_len, d):
    half = d // 2
    quarter = half // 2
    t = np.arange(seq_len)
    inv = ROPE_BASE ** (-np.arange(quarter, dtype=np.float64) / quarter)
    ang_r = (t // GRID_W)[:, None] * inv[None, :]
    ang_c = (t % GRID_W)[:, None] * inv[None, :]
    zero = np.zeros_like(ang_r)
    cos = np.concatenate([np.cos(ang_r)] * 2 + [np.cos(ang_c)] * 2, axis=1)
    sa = np.concatenate([-np.sin(ang_r), zero, -np.sin(ang_c), zero], axis=1)
    sb = np.concatenate([zero, np.sin(ang_r), zero, np.sin(ang_c)], axis=1)
    reps = LANES // d
    return tuple(jnp.asarray(np.tile(a, (1, reps)).astype(np.float32)) for a in (cos, sa, sb))


def _ret_kernel(*refs, n_chunks, has_state, write_state, heads):
    lg_ref, q_ref, k_ref, v_ref, g_ref, gnw_ref = refs[:6]
    pos = 6
    if has_state:
        s0f_ref, s0b_ref = refs[pos:pos + 2]
        pos += 2
    o_ref = refs[pos]
    pos += 1
    if write_state:
        sf_ref, sb_ref = refs[pos:pos + 2]
        pos += 2
    acc_ref, kv_ref = refs[pos:pos + 2]
    for hh in range(heads):
        _ret_head(hh, heads, lg_ref, q_ref, k_ref, v_ref, g_ref, gnw_ref,
                  (s0f_ref, s0b_ref) if has_state else None, o_ref, (sf_ref, sb_ref) if write_state else None,
                  acc_ref, kv_ref, n_chunks)


def _ret_head(hh, heads, lg_ref, q_ref, k_ref, v_ref, g_ref, gnw_ref, s0_refs, o_ref, s_out_refs, acc_ref, kv_ref,
              n_chunks):
    C = RET_CHUNK
    cols = slice(hh * LANES, (hh + 1) * LANES)
    hd = pl.program_id(1) * heads + hh
    lgf = lg_ref[0, hd]
    lgb = lg_ref[1, hd]
    ii = lax.broadcasted_iota(jnp.int32, (C, C), 0).astype(F32)
    jj = lax.broadcasted_iota(jnp.int32, (C, C), 1).astype(F32)
    rel = ii - jj
    d_f = jnp.where(rel >= 0, jnp.exp(jnp.maximum(rel, 0.0) * lgf), 0.0)
    d_b = jnp.where(rel <= 0, jnp.exp(jnp.maximum(-rel, 0.0) * lgb), 0.0)
    d_sum = d_f + d_b
    idx = lax.broadcasted_iota(jnp.int32, (C, 1), 0).astype(F32)
    xi_f = jnp.exp((idx + 1.0) * lgf)
    zeta_f = jnp.exp((C - 1.0 - idx) * lgf)
    xi_b = jnp.exp((C - idx) * lgb)
    zeta_b = jnp.exp(idx * lgb)
    one = jnp.ones((1, 1), F32)
    gc_f = jnp.exp(one * (C * lgf))
    gc_b = jnp.exp(one * (C * lgb))

    nt = (((1,), (1,)), ((), ()))
    tn = (((0,), (0,)), ((), ()))

    if s0_refs is not None:
        s_f = s0_refs[0][hh]
        s_b = s0_refs[1][hh]
    else:
        s_f = jnp.zeros((RET_DK, RET_DV), F32)
        s_b = jnp.zeros((RET_DK, RET_DV), F32)

    for n in range(n_chunks):
        sl = slice(n * C, (n + 1) * C)
        qc, kc, vc = q_ref[sl, cols], k_ref[sl, cols], v_ref[sl, cols]
        scores = lax.dot_general(qc, kc, nt, preferred_element_type=F32) * d_sum
        acc_ref[sl, cols] = jnp.dot(scores.astype(BF16), vc, preferred_element_type=F32)
        kf = kc.astype(F32)
        kv_ref[0, hh, n] = lax.dot_general((kf * zeta_f).astype(BF16), vc, tn, preferred_element_type=F32)
        kv_ref[1, hh, n] = lax.dot_general((kf * zeta_b).astype(BF16), vc, tn, preferred_element_type=F32)

    for n in range(n_chunks):
        sl = slice(n * C, (n + 1) * C)
        qf = q_ref[sl, cols].astype(F32)
        acc_ref[sl, cols] += jnp.dot((qf * xi_f).astype(BF16), s_f.astype(BF16), preferred_element_type=F32)
        s_f = gc_f * s_f + kv_ref[0, hh, n]

    gnw = gnw_ref[:, cols]
    for n in reversed(range(n_chunks)):
        sl = slice(n * C, (n + 1) * C)
        qf = q_ref[sl, cols].astype(F32)
        cross = jnp.dot((qf * xi_b).astype(BF16), s_b.astype(BF16), preferred_element_type=F32)
        r = acc_ref[sl, cols] + cross
        mu = jnp.mean(r, axis=-1, keepdims=True)
        rc = r - mu
        var = jnp.mean(rc * rc, axis=-1, keepdims=True)
        rn = rc * lax.rsqrt(var + LN_EPS) * gnw
        o_ref[sl, cols] = (_silu(g_ref[sl, cols].astype(F32)) * rn).astype(BF16)
        s_b = gc_b * s_b + kv_ref[1, hh, n]

    if s_out_refs is not None:
        s_out_refs[0][hh] = s_f
        s_out_refs[1][hh] = s_b


def _retention(proj, lg, gn_w, batch, seq_len, states=None, write_state=False, heads=2):
    T = proj.shape[0]
    hp = heads
    width = hp * LANES
    per_sec = SEC // width
    blk = lambda sec: pl.BlockSpec((seq_len, width), lambda b, h: (b, sec * per_sec + h))
    in_specs = [pl.BlockSpec(memory_space=pltpu.SMEM), blk(0), blk(1), blk(2), blk(3),
                pl.BlockSpec((1, width), lambda b, h: (0, h))]
    args = [lg, proj, proj, proj, proj, gn_w.reshape(1, -1)]
    if states is not None:
        st = pl.BlockSpec((None, hp, RET_DK, RET_DV), lambda b, h: (b, h, 0, 0))
        in_specs += [st, st]
        args += list(states)
    out_shape = [jax.ShapeDtypeStruct((T, SEC), BF16)]
    out_specs = [pl.BlockSpec((seq_len, width), lambda b, h: (b, h))]
    if write_state:
        st_o = pl.BlockSpec((None, hp, RET_DK, RET_DV), lambda b, h: (b, h, 0, 0))
        out_shape += [jax.ShapeDtypeStruct((batch, H_RET, RET_DK, RET_DV), F32)] * 2
        out_specs += [st_o, st_o]
    return pl.pallas_call(
        functools.partial(_ret_kernel, n_chunks=seq_len // RET_CHUNK, has_state=states is not None,
                          write_state=write_state, heads=hp),
        grid=(batch, H_RET // hp),
        in_specs=in_specs,
        out_specs=out_specs,
        out_shape=out_shape,
        scratch_shapes=[pltpu.VMEM((seq_len, width), F32),
                        pltpu.VMEM((2, hp, seq_len // RET_CHUNK, RET_DK, RET_DV), F32)],
        compiler_params=_cparams(("arbitrary", "arbitrary")),
        name="retention",
    )(*args)


def _att_kernel(*refs, has_ctx, out_scale, key_chunk, heads):
    lam_ref, q_ref, k_ref, v_ref = refs[:4]
    pos = 4
    if has_ctx:
        ck_ref, cv_ref = refs[pos:pos + 2]
        pos += 2
    w_ref, o_ref = refs[pos:pos + 2]
    lam = lam_ref[0]
    tq = q_ref.shape[0]
    nt = (((1,), (1,)), ((), ()))
    for hh in range(heads):
        cols = slice(hh * LANES, (hh + 1) * LANES)
        q = q_ref[:, cols]
        lane = lax.broadcasted_iota(jnp.int32, q.shape, 1)
        zero = jnp.zeros_like(q)
        qq = jnp.concatenate([jnp.where(lane < DIFF_QK, q, zero), jnp.where(lane >= DIFF_QK, q, zero)], axis=0)
        qq = qq * jnp.asarray(DIFF_QK ** -0.5, BF16)
        chunks = [(k_ref, v_ref, c * key_chunk, key_chunk) for c in range(k_ref.shape[0] // key_chunk)]
        if has_ctx:
            chunks.append((ck_ref, cv_ref, 0, ck_ref.shape[0]))
        m = l = acc = None
        for kr, vr, off, n in chunks:
            kch = kr[off:off + n, cols].astype(BF16)
            vch = vr[off:off + n, cols].astype(BF16)
            s = lax.dot_general(qq, kch, nt, preferred_element_type=F32)
            cm = jnp.max(s, axis=-1, keepdims=True)
            m_new = cm if m is None else jnp.maximum(m, cm)
            p = jnp.exp(s - m_new)
            ps = jnp.sum(p, axis=-1, keepdims=True)
            pv = jnp.dot(p.astype(BF16), vch, preferred_element_type=F32)
            if m is None:
                l, acc = ps, pv
            else:
                alpha = jnp.exp(m - m_new)
                l = alpha * l + ps
                acc = alpha * acc + pv
            m = m_new
        o = acc / l
        att = o[:tq] - lam * o[tq:]
        att = att * lax.rsqrt(jnp.mean(att * att, axis=-1, keepdims=True) + LN_EPS)
        o_ref[:, cols] = (att * w_ref[...] * out_scale).astype(BF16)


def _diff_attention(proj, lam, subln_w, lam_init, batch, seq_len, ctx=None, tq=256, heads=1):
    T = proj.shape[0]
    width = heads * LANES
    per_sec = SEC // width
    nq = seq_len // tq
    in_specs = [
        pl.BlockSpec(memory_space=pltpu.SMEM),
        pl.BlockSpec((tq, width), lambda b, h, i: (b * nq + i, 4 * per_sec + h)),
        pl.BlockSpec((seq_len, width), lambda b, h, i: (b, 5 * per_sec + h)),
        pl.BlockSpec((seq_len, width), lambda b, h, i: (b, 6 * per_sec + h)),
    ]
    args = [lam, proj, proj, proj]
    if ctx is not None:
        ck, cv = ctx
        past = ck.shape[1]
        cspec = pl.BlockSpec((None, past, width), lambda b, h, i: (b, 0, h))
        in_specs += [cspec, cspec]
        args += [ck, cv]
    in_specs.append(pl.BlockSpec((1, LANES), lambda b, h, i: (0, 0)))
    args.append(subln_w.reshape(1, -1))
    return pl.pallas_call(
        functools.partial(_att_kernel, has_ctx=ctx is not None, out_scale=1.0 - lam_init,
                          key_chunk=min(512, seq_len), heads=heads),
        grid=(batch, H_DIFF // heads, nq),
        in_specs=in_specs,
        out_specs=pl.BlockSpec((tq, width), lambda b, h, i: (b * nq + i, h)),
        out_shape=jax.ShapeDtypeStruct((T, SEC), BF16),
        compiler_params=_cparams(("arbitrary", "arbitrary", "arbitrary")),
        name="diff_attention",
    )(*args)


def _route_class(lt, rb_ref):
    sel = [jax.nn.sigmoid(lt[e:e + 1, :]) + rb_ref[e] for e in range(N_EXPERTS)]
    gscore = []
    for g in range(N_GROUPS):
        mem = sel[g * GROUP_SIZE:(g + 1) * GROUP_SIZE]
        best = None
        for a in range(GROUP_SIZE):
            for b in range(a + 1, GROUP_SIZE):
                pair = mem[a] + mem[b]
                best = pair if best is None else jnp.maximum(best, pair)
        gscore.append(best)
    gbest = gscore[0]
    gidx = jnp.zeros_like(gbest)
    for g in range(1, N_GROUPS):
        upd = gscore[g] > gbest
        gidx = jnp.where(upd, float(g), gidx)
        gbest = jnp.where(upd, gscore[g], gbest)
    msel = []
    for j in range(GROUP_SIZE):
        out = sel[j]
        for g in range(1, N_GROUPS):
            out = jnp.where(gidx == float(g), sel[g * GROUP_SIZE + j], out)
        msel.append(out)
    one = jnp.ones_like(gbest)
    zero = jnp.zeros_like(gbest)
    chosen = []
    for j in range(GROUP_SIZE):
        rank = zero
        for k in range(GROUP_SIZE):
            if k < j:
                rank = rank + jnp.where(msel[k] >= msel[j], one, zero)
            elif k > j:
                rank = rank + jnp.where(msel[k] > msel[j], one, zero)
        chosen.append(jnp.where(rank < 2.0, one, zero))
    c0, c1, c2, c3 = chosen
    order = jnp.where(c0 * c1 > 0, 0.0, jnp.where(c1 * c2 > 0, 1.0, jnp.where(c2 * c3 > 0, 2.0,
            jnp.where(c0 * c3 > 0, 3.0, jnp.where(c0 * c2 > 0, 4.0, 5.0)))))
    return gidx * float(N_PAIRS) + order


def _post_kernel(*refs, n_mix, group_tiles):
    n_groups = len(group_tiles)
    per_group = n_mix + 1
    group_refs = [refs[g * per_group:(g + 1) * per_group] for g in range(n_groups)]
    refs = refs[n_groups * per_group:]
    (w_f32_ref, g_ref, sh_ref, sc_ref, lnw_ref, lnb_ref, rwh_ref, rwl_ref, rb_ref, tri_ref,
     x1_ref, h2_ref, info_ref, cout_ref, cnt_ref, pre_ref, w_ref) = refs
    _cast_once(w_f32_ref, w_ref)
    i = pl.program_id(0)

    @pl.when(i == 0)
    def _():
        cnt_ref[...] = jnp.zeros_like(cnt_ref)

    first = 0
    for g in range(n_groups):
        @pl.when(jnp.logical_and(i >= first, i < first + group_tiles[g]))
        def _(g=g):
            out = None
            off = 0
            for m_ref in group_refs[g][:n_mix]:
                width = m_ref.shape[1]
                part = jnp.dot(m_ref[...], w_ref[off:off + width, :], preferred_element_type=F32)
                out = part if out is None else out + part
                off += width
            pre_ref[...] = ALPHA * group_refs[g][n_mix][...] + g_ref[...] * out
        first += group_tiles[g]

    tm = pre_ref.shape[0]
    ts = tri_ref.shape[0]
    base = cnt_ref[:, 0:1]
    for part in range(tm // ts):
        rows = slice(part * ts, (part + 1) * ts)
        x1 = _layer_norm(pre_ref[rows, :], lnw_ref[...], lnb_ref[...])
        x1_ref[rows, :] = x1
        h2 = x1 * (1.0 + sc_ref[...]) + sh_ref[...]
        for j in range(TOK_ROWS):
            h2_ref[pl.ds(part * ts * TOK_ROWS + j, ts, stride=TOK_ROWS), :] = h2[:, j * LANES:(j + 1) * LANES]
        h_hi = h2.astype(BF16)
        h_lo = (h2 - h_hi.astype(F32)).astype(BF16)
        both = jnp.dot(h_hi, rwl_ref[...], preferred_element_type=F32)
        logits = (both[:, :LANES] + both[:, LANES:]
                  + jnp.dot(h_lo, rwh_ref[...], preferred_element_type=F32))
        cls = _route_class(logits.T, rb_ref)
        crow = lax.broadcasted_iota(jnp.int32, (32, ts), 0).astype(F32)
        onehot = jnp.where(crow == cls, 1.0, 0.0)
        prefix = jnp.dot(onehot.astype(BF16), tri_ref[...], preferred_element_type=F32)
        rank = jnp.sum(onehot * (prefix - 1.0 + base), axis=0, keepdims=True)
        base = base + jnp.sum(onehot, axis=1, keepdims=True)
        packed = cls.astype(jnp.int32) * 65536 + rank.astype(jnp.int32)
        info_ref[:, rows] = packed
    cnt_ref[...] = jnp.broadcast_to(base, cnt_ref.shape)

    @pl.when(i == pl.num_programs(0) - 1)
    def _():
        cout_ref[...] = cnt_ref[...]


def _tri(tm):
    return jnp.asarray(np.triu(np.ones((tm, tm), np.float32))).astype(BF16)


def _group_spec(width, tm, first_tile, n_tiles, array_tile_off=0):
    return pl.BlockSpec((tm, width),
                        lambda i, *_: (jnp.clip(i - first_tile, 0, n_tiles - 1) + array_tile_off, 0))


def _post_mix(group_mixes, group_x, w_out_bf16, mod, layer, ln_w, ln_b, rw_hi, rw_lo, router_bias, cond_row, tm):
    group_tiles = tuple(m[0].shape[0] // tm for m in group_mixes)
    n_tiles = sum(group_tiles)
    total_rows = n_tiles * tm
    row = pl.BlockSpec((tm, D_MODEL), lambda i: (i, 0))
    vec = pl.BlockSpec((1, D_MODEL), lambda i: (0, 0))
    cnt = pl.BlockSpec((32, LANES), lambda i: (0, 0))
    in_specs, args = [], []
    first = 0
    for mixes, (x, x_row_off), nt in zip(group_mixes, group_x, group_tiles):
        for m in mixes:
            in_specs.append(_group_spec(m.shape[1], tm, first, nt))
            args.append(m)
        in_specs.append(_group_spec(D_MODEL, tm, first, nt, x_row_off // tm))
        args.append(x)
        first += nt
    in_specs += [
        pl.BlockSpec((D_MODEL, D_MODEL), lambda i: (0, 0), pipeline_mode=pl.Buffered(1)),
        _mod_spec(layer, 2, lambda i: cond_row(i, tm)), _mod_spec(layer, 3, lambda i: cond_row(i, tm)),
        _mod_spec(layer, 4, lambda i: cond_row(i, tm)),
        vec, vec,
        pl.BlockSpec((D_MODEL, LANES), lambda i: (0, 0)),
        pl.BlockSpec((D_MODEL, 2 * LANES), lambda i: (0, 0)),
        pl.BlockSpec(memory_space=pltpu.SMEM),
        pl.BlockSpec((POST_SUB, POST_SUB), lambda i: (0, 0)),
    ]
    args += [w_out_bf16, mod, mod, mod, ln_w.reshape(1, -1), ln_b.reshape(1, -1), rw_hi, rw_lo, router_bias,
             _tri(POST_SUB)]
    return pl.pallas_call(
        functools.partial(_post_kernel, n_mix=len(group_mixes[0]), group_tiles=group_tiles),
        grid=(n_tiles,),
        in_specs=in_specs,
        out_specs=[row, pl.BlockSpec((tm * TOK_ROWS, LANES), lambda i: (i, 0)),
                   pl.BlockSpec((1, tm), lambda i: (0, i)), cnt],
        out_shape=[jax.ShapeDtypeStruct((total_rows, D_MODEL), F32),
                   jax.ShapeDtypeStruct((total_rows * TOK_ROWS, LANES), F32),
                   jax.ShapeDtypeStruct((1, total_rows), jnp.int32), jax.ShapeDtypeStruct((32, LANES), F32)],
        scratch_shapes=[pltpu.VMEM((32, LANES), F32), pltpu.VMEM((tm, D_MODEL), F32),
                        pltpu.VMEM((D_MODEL, D_MODEL), BF16)],
        compiler_params=_cparams(("arbitrary",)),
        name="post_mix",
    )(*args)


def _plan_kernel(cnt_ref, ea_ref, eb_ref, fl_ref, rs_ref, *, n_tiles):
    start = jnp.int32(0)
    prev_a = jnp.int32(-1)
    prev_b = jnp.int32(-1)
    par_a = jnp.int32(0)
    par_b = jnp.int32(0)
    for c in range(N_CLASSES):
        n = cnt_ref[c]
        tiles = lax.shift_right_logical(n + (MOE_TM - 1), MOE_SHIFT)
        row0 = start * MOE_TM
        rs_ref[c] = row0
        g, pr = divmod(c, N_PAIRS)
        a = g * GROUP_SIZE + PAIR_SLOTS[pr][0]
        b = g * GROUP_SIZE + PAIR_SLOTS[pr][1]
        new_a = (prev_a != a).astype(jnp.int32)
        new_b = (prev_b != b).astype(jnp.int32)
        has = tiles > 0
        par_a = jnp.where(has, par_a ^ new_a, par_a)
        par_b = jnp.where(has, par_b ^ new_b, par_b)
        first = 1 + 4 * new_a + 8 * new_b
        common = 16 * par_a + 32 * par_b

        def tile_body(k, _, start=start, a=a, b=b, first=first, n=n, common=common):
            t = start + k
            ea_ref[t] = a
            eb_ref[t] = b
            half = 2 * (n - k * MOE_TM <= MOE_TM // 2).astype(jnp.int32)
            fl_ref[t] = jnp.where(k == 0, first, 1) + half + common
            return 0

        lax.fori_loop(0, tiles, tile_body, 0)
        prev_a = jnp.where(has, a, prev_a)
        prev_b = jnp.where(has, b, prev_b)
        start = start + tiles
    for c in range(N_CLASSES, 31):
        rs_ref[c] = 0
    rs_ref[31] = start

    def idle_body(t, _):
        ea_ref[t] = prev_a
        eb_ref[t] = prev_b
        fl_ref[t] = 0
        return 0

    lax.fori_loop(start, n_tiles, idle_body, 0)


def _moe_plan(counts, n_tiles):
    smem = pl.BlockSpec(memory_space=pltpu.SMEM)
    i32 = lambda n: jax.ShapeDtypeStruct((n,), jnp.int32)
    return pl.pallas_call(
        functools.partial(_plan_kernel, n_tiles=n_tiles),
        in_specs=[smem],
        out_specs=[smem] * 4,
        out_shape=[i32(n_tiles), i32(n_tiles), i32(n_tiles), i32(32)],
        name="moe_plan",
    )(counts)


def _class_row(packed, rs_ref):
    return rs_ref[lax.shift_right_logical(packed, 16)] + (packed & 0xFFFF)


def _dispatch_kernel(packed_ref, rs_ref, cnt_ref, h_ref, hs_ref, sem, fill_sem, *, tm, n_tiles):
    i = pl.program_id(0)
    used = rs_ref[31]

    def tile_rows(row, n=1):
        return pl.ds(pl.multiple_of(row * TOK_ROWS, TOK_ROWS), n * TOK_ROWS)

    def fill_copy(row, n):
        return pltpu.make_async_copy(h_ref.at[tile_rows(0, n)], hs_ref.at[tile_rows(row, n)], fill_sem)

    def fill(act):
        for c in range(N_CLASSES):
            n = cnt_ref[c]
            row = rs_ref[c] + n
            pad = (-n) & (MOE_TM - 1)
            for bit in reversed(range(MOE_SHIFT)):
                size = 1 << bit

                @pl.when((pad & size) != 0)
                def _(row=row, size=size):
                    act(fill_copy(row, size))
                row = row + (pad & size)
        lax.fori_loop(used, n_tiles, lambda t, _: (act(fill_copy(t * MOE_TM, MOE_TM)), 0)[1], 0)

    @pl.when(i == 0)
    def _():
        fill(lambda cp: cp.start())

    def body(r8, _):
        for j in range(8):
            r = r8 * 8 + j
            row = _class_row(packed_ref[0, i * tm + r], rs_ref)
            pltpu.make_async_copy(h_ref.at[tile_rows(r)], hs_ref.at[tile_rows(row)], sem).start(priority=j % 2)
        return 0

    lax.fori_loop(0, tm // 8, body, 0)
    pltpu.make_async_copy(h_ref, hs_ref.at[tile_rows(0, tm)], sem).wait()

    @pl.when(i == 0)
    def _():
        fill(lambda cp: cp.wait())


def _dispatch(h2_tiles, packed, plan, counts, n_tiles, tm=2048):
    T = packed.shape[1]
    assert tm >= MOE_TM and T % tm == 0
    grid_spec = pltpu.PrefetchScalarGridSpec(
        num_scalar_prefetch=3,
        grid=(T // tm,),
        in_specs=[pl.BlockSpec((tm * TOK_ROWS, LANES), lambda i, *_: (i, 0))],
        out_specs=pl.BlockSpec(memory_space=pl.ANY),
        scratch_shapes=[pltpu.SemaphoreType.DMA(()), pltpu.SemaphoreType.DMA(())],
    )
    return pl.pallas_call(
        functools.partial(_dispatch_kernel, tm=tm, n_tiles=n_tiles),
        grid_spec=grid_spec,
        out_shape=jax.ShapeDtypeStruct((n_tiles * MOE_TM * TOK_ROWS, LANES), F32),
        compiler_params=_cparams(("arbitrary",)),
        name="moe_dispatch",
    )(packed, plan[3], counts, h2_tiles)


def _moe_kernel(ea_ref, eb_ref, fl_ref, h_ref, rwt_ref, wga, wua, wda, wgb, wub, wdb, o_ref,
                sga, sua, sda, sgb, sub, sdb):
    s = pl.program_id(0)
    n_tiles = pl.num_programs(0) - 1
    nxt = fl_ref[jnp.minimum(s, n_tiles - 1)]
    stage = s < n_tiles

    @pl.when(jnp.logical_and(stage, (nxt & 4) != 0))
    def _():
        p = (nxt >> 4) & 1
        sga[p] = wga[...].astype(BF16)
        sua[p] = wua[...].astype(BF16)
        sda[p] = wda[...].astype(BF16)

    @pl.when(jnp.logical_and(stage, (nxt & 8) != 0))
    def _():
        p = (nxt >> 5) & 1
        sgb[p] = wgb[...].astype(BF16)
        sub[p] = wub[...].astype(BF16)
        sdb[p] = wdb[...].astype(BF16)

    t = jnp.maximum(s - 1, 0)
    flags = fl_ref[t]
    run = s > 0
    valid = jnp.logical_and(run, (flags & 1) != 0)
    pa = (flags >> 4) & 1
    pb = (flags >> 5) & 1

    def experts(rows):
        h = _load_token_tiles(h_ref, rows).astype(BF16)
        nt = (((1,), (1,)), ((), ()))
        score = jax.nn.sigmoid(lax.dot_general(h, rwt_ref[...], nt, preferred_element_type=F32))
        lane = lax.broadcasted_iota(jnp.int32, score.shape, 1)
        s_a = jnp.sum(jnp.where(lane == ea_ref[t], score, 0.0), axis=1, keepdims=True)
        s_b = jnp.sum(jnp.where(lane == eb_ref[t], score, 0.0), axis=1, keepdims=True)
        tot = s_a + s_b
        act_a = _silu(jnp.dot(h, sga[pa], preferred_element_type=F32)) \
            * jnp.dot(h, sua[pa], preferred_element_type=F32) * (s_a / tot)
        act_b = _silu(jnp.dot(h, sgb[pb], preferred_element_type=F32)) \
            * jnp.dot(h, sub[pb], preferred_element_type=F32) * (s_b / tot)
        _store_token_tiles(o_ref, jnp.dot(act_a.astype(BF16), sda[pa], preferred_element_type=F32)
                           + jnp.dot(act_b.astype(BF16), sdb[pb], preferred_element_type=F32))
        if rows < MOE_TM:
            o_ref[rows * TOK_ROWS:, :] = jnp.zeros(((MOE_TM - rows) * TOK_ROWS, LANES), F32)

    half = (flags & 2) != 0

    @pl.when(jnp.logical_and(valid, jnp.logical_not(half)))
    def _():
        experts(MOE_TM)

    @pl.when(jnp.logical_and(valid, half))
    def _():
        experts(MOE_TM // 2)

    @pl.when(jnp.logical_and(run, jnp.logical_not(valid)))
    def _():
        o_ref[...] = jnp.zeros_like(o_ref)


def _moe(hs, plan, layer, rwt_bf16, w_gate, w_up, w_down):
    ea, eb, flags, _ = plan
    n_tiles = ea.shape[0]
    ahead = lambda s: jnp.minimum(s, n_tiles - 1)
    wspec = lambda shape, which: pl.BlockSpec(
        (None, None) + shape, (lambda s, ea, eb, fl: (layer, ea[ahead(s)], 0, 0)) if which == 0
        else (lambda s, ea, eb, fl: (layer, eb[ahead(s)], 0, 0)))
    up = (D_MODEL, D_FF_EXPERT)
    dn = (D_FF_EXPERT, D_MODEL)
    tile = pl.BlockSpec((MOE_TM * TOK_ROWS, LANES), lambda s, ea, eb, fl: (jnp.maximum(s - 1, 0), 0))
    grid_spec = pltpu.PrefetchScalarGridSpec(
        num_scalar_prefetch=3,
        grid=(n_tiles + 1,),
        in_specs=[
            tile,
            pl.BlockSpec((N_EXPERTS, D_MODEL), lambda s, ea, eb, fl: (0, 0)),
            wspec(up, 0), wspec(up, 0), wspec(dn, 0), wspec(up, 1), wspec(up, 1), wspec(dn, 1),
        ],
        out_specs=tile,
        scratch_shapes=[pltpu.VMEM((2,) + up, BF16), pltpu.VMEM((2,) + up, BF16), pltpu.VMEM((2,) + dn, BF16),
                        pltpu.VMEM((2,) + up, BF16), pltpu.VMEM((2,) + up, BF16), pltpu.VMEM((2,) + dn, BF16)],
    )
    return pl.pallas_call(
        _moe_kernel,
        grid_spec=grid_spec,
        out_shape=jax.ShapeDtypeStruct((n_tiles * MOE_TM * TOK_ROWS, LANES), F32),
        compiler_params=_cparams(("arbitrary",)),
        name="moe",
    )(ea, eb, flags, hs, rwt_bf16, w_gate, w_up, w_down, w_gate, w_up, w_down)


def _gather_expert_rows(packed_ref, rs_ref, ys_ref, ybuf, sem, tm):
    i = pl.program_id(0)
    n_i = pl.num_programs(0)
    slot = i % 2

    def gather(tile, buf_slot):
        def body(r8, _):
            for j in range(8):
                r = r8 * 8 + j
                d = _class_row(packed_ref[0, tile * tm + r], rs_ref)
                pltpu.make_async_copy(ys_ref.at[pl.ds(pl.multiple_of(d * TOK_ROWS, TOK_ROWS), TOK_ROWS)],
                                      ybuf.at[buf_slot, pl.ds(pl.multiple_of(r * TOK_ROWS, TOK_ROWS), TOK_ROWS)],
                                      sem.at[buf_slot]).start(priority=j % 2)
            return 0
        lax.fori_loop(0, tm // 8, body, 0)

    @pl.when(i == 0)
    def _():
        gather(0, 0)

    @pl.when(i + 1 < n_i)
    def _():
        gather(jnp.minimum(i + 1, n_i - 1), 1 - slot)

    pltpu.make_async_copy(ys_ref.at[pl.ds(0, tm * TOK_ROWS)], ybuf.at[slot], sem.at[slot]).wait()
    return _load_token_tiles(ybuf, tm, slot)


def _fin_proj_kernel(packed_ref, rs_ref, x_ref, ys_ref, g_ref, lnw_ref, lnb_ref, sh_ref, sc_ref, w_f32_ref,
                     x2_ref, u_ref, ybuf, sem, w_ref, *, tm):
    _cast_once(w_f32_ref, w_ref)
    y = _gather_expert_rows(packed_ref, rs_ref, ys_ref, ybuf, sem, tm)
    x2 = _layer_norm(ALPHA * x_ref[...] + g_ref[...] * y, lnw_ref[...], lnb_ref[...])
    x2_ref[...] = x2
    h = (x2 * (1.0 + sc_ref[...]) + sh_ref[...]).astype(BF16)
    for s in range(u_ref.shape[1] // SEC):
        u_ref[:, s * SEC:(s + 1) * SEC] = jnp.dot(h, w_ref[:, s * SEC:(s + 1) * SEC],
                                                  preferred_element_type=F32).astype(BF16)


def _fin_proj(x1, ys, packed, row_start, mod, layer, ln_w, ln_b, w_next, cond_row, tm=512):
    T = x1.shape[0]
    N = w_next.shape[1]
    row = pl.BlockSpec((tm, D_MODEL), lambda i, *_: (i, 0))
    vec = pl.BlockSpec((1, D_MODEL), lambda i, *_: (0, 0))
    crow = lambda i: cond_row(i, tm)
    grid_spec = pltpu.PrefetchScalarGridSpec(
        num_scalar_prefetch=2,
        grid=(T // tm,),
        in_specs=[row, pl.BlockSpec(memory_space=pl.ANY), _mod_spec(layer, 5, crow), vec, vec,
                  _mod_spec(layer + 1, 0, crow), _mod_spec(layer + 1, 1, crow),
                  pl.BlockSpec((D_MODEL, N), lambda i, *_: (0, 0), pipeline_mode=pl.Buffered(1))],
        out_specs=[row, pl.BlockSpec((tm, N), lambda i, *_: (i, 0))],
        scratch_shapes=[pltpu.VMEM((2, tm * TOK_ROWS, LANES), F32), pltpu.SemaphoreType.DMA((2,)),
                        pltpu.VMEM((D_MODEL, N), BF16)],
    )
    return pl.pallas_call(
        functools.partial(_fin_proj_kernel, tm=tm),
        grid_spec=grid_spec,
        out_shape=[jax.ShapeDtypeStruct((T, D_MODEL), F32), jax.ShapeDtypeStruct((T, N), BF16)],
        compiler_params=_cparams(("arbitrary",)),
        name="post_moe_in_proj",
    )(packed, row_start, x1, ys, mod, ln_w.reshape(1, -1), ln_b.reshape(1, -1), mod, mod, w_next)


def _fin_kernel(packed_ref, rs_ref, x_ref, ys_ref, g_ref, lnw_ref, lnb_ref, *rest, tm, group_tiles):
    o_refs = rest[:len(group_tiles)]
    ybuf, sem = rest[len(group_tiles):]
    i = pl.program_id(0)
    y = _layer_norm(ALPHA * x_ref[...] + g_ref[...] * _gather_expert_rows(packed_ref, rs_ref, ys_ref, ybuf, sem, tm),
                    lnw_ref[...], lnb_ref[...])
    first = 0
    for o_ref, nt in zip(o_refs, group_tiles):
        @pl.when(jnp.logical_and(i >= first, i < first + nt))
        def _(o_ref=o_ref):
            o_ref[...] = y
        first += nt


def _fin(x1, ys, packed, row_start, mod, layer, ln_w, ln_b, cond_row, group_rows, tm=512):
    group_tiles = tuple(n // tm for n in group_rows)
    row = pl.BlockSpec((tm, D_MODEL), lambda i, *_: (i, 0))
    vec = pl.BlockSpec((1, D_MODEL), lambda i, *_: (0, 0))
    out_specs, first = [], 0
    for nt in group_tiles:
        out_specs.append(_group_spec(D_MODEL, tm, first, nt))
        first += nt
    grid_spec = pltpu.PrefetchScalarGridSpec(
        num_scalar_prefetch=2,
        grid=(sum(group_tiles),),
        in_specs=[row, pl.BlockSpec(memory_space=pl.ANY), _mod_spec(layer, 5, lambda i: cond_row(i, tm)), vec, vec],
        out_specs=out_specs,
        scratch_shapes=[pltpu.VMEM((2, tm * TOK_ROWS, LANES), F32), pltpu.SemaphoreType.DMA((2,))],
    )
    return pl.pallas_call(
        functools.partial(_fin_kernel, tm=tm, group_tiles=group_tiles),
        grid_spec=grid_spec,
        out_shape=[jax.ShapeDtypeStruct((n, D_MODEL), F32) for n in group_rows],
        compiler_params=_cparams(("arbitrary",)),
        name="post_moe",
    )(packed, row_start, x1, ys, mod, ln_w.reshape(1, -1), ln_b.reshape(1, -1))


def _filt_kernel(z_ref, w1_ref, b1_ref, w2_ref, b2_ref, fr_ref, w3_ref, dl_ref, o_ref):
    i = pl.program_id(0)
    z = z_ref[...]
    fr = fr_ref[...]
    a = jnp.sin(fr * (jnp.dot(z, w1_ref[...], precision=HIGHEST, preferred_element_type=F32) + b1_ref[...]))
    a = jnp.sin(fr * (jnp.dot(a, w2_ref[...], precision=HIGHEST, preferred_element_type=F32) + b2_ref[...]))
    filt = _dot_3pass(a, w3_ref[...])
    window = jnp.exp(-z[:, 0:1] * dl_ref[...]) + HY_SHIFT
    rows = i * z.shape[0] + lax.broadcasted_iota(jnp.int32, filt.shape, 0)
    o_ref[...] = jnp.where(rows == 0, 0.0, filt * window)


def _hyena_taps(seq_len, w1, b1, w2, b2, w3, freq):
    L = seq_len
    t = np.linspace(0.0, 1.0, L)[:, None]
    bands = np.linspace(1e-4, HY_BANDS - 1, HY_BANDS)
    ang = 2.0 * math.pi * bands[None, :] * np.arange(L)[:, None] / L
    z = np.concatenate([t, np.cos(ang), -np.sin(ang)], axis=-1)
    offs = np.minimum(np.abs(np.arange(2 * L) - L), L - 1)
    z2_np = np.zeros((2 * L, LANES), np.float32)
    z2_np[:, :z.shape[1]] = z[offs]
    z2 = jnp.asarray(z2_np)
    pad_c = lambda a: jnp.zeros((a.shape[0], LANES), F32).at[:, :a.shape[1]].set(a)
    pad_r = lambda a: jnp.zeros((LANES, a.shape[1]), F32).at[:a.shape[0], :].set(a)
    w1p = pad_r(pad_c(w1))
    w2p = pad_r(pad_c(w2))
    w3p = pad_r(w3)
    b1p, b2p, frp = pad_c(b1[None, :]), pad_c(b2[None, :]), pad_c(freq[None, :])
    deltas = jnp.asarray(np.abs(np.linspace(math.log(HY_DECAY_TARGET) / HY_SLOW, math.log(HY_DECAY_TARGET) / HY_FAST,
                                            D_MODEL))[None, :].astype(np.float32))
    rb = min(1024, L)
    cbf = D_MODEL
    ncb = D_MODEL // cbf
    nrb_back = L // rb
    sq = pl.BlockSpec((LANES, LANES), lambda i, j: (0, 0))
    vec = pl.BlockSpec((1, LANES), lambda i, j: (0, 0))
    return pl.pallas_call(
        _filt_kernel,
        grid=(2 * L // rb, ncb),
        in_specs=[
            pl.BlockSpec((rb, LANES), lambda i, j: (i, 0)),
            sq, vec, sq, vec, vec,
            pl.BlockSpec((LANES, cbf), lambda i, j: (0, jnp.where(i < nrb_back, ncb + j, j))),
            pl.BlockSpec((1, cbf), lambda i, j: (0, j)),
        ],
        out_specs=pl.BlockSpec((rb, cbf), lambda i, j: (i, j)),
        out_shape=jax.ShapeDtypeStruct((2 * L, D_MODEL), F32),
        compiler_params=_cparams(("arbitrary", "arbitrary")),
        name="hyena_taps",
    )(z2, w1p, b1p, w2p, b2p, frp, w3p, deltas)


def _dft_mats(cb):
    n = 2 * cb
    m = np.arange(cb)
    f = np.arange(cb)
    ang = 2.0 * np.pi * ((f[:, None] * m[None, :]) % n) / n
    fwd = np.concatenate([np.cos(ang), -np.sin(ang)], axis=0)
    fwd[cb, :] = np.where(m % 2 == 0, 1.0, -1.0)
    coef = np.where(f == 0, 1.0, 2.0)[None, :] / n
    inv = np.concatenate([coef * np.cos(ang.T), -coef * np.sin(ang.T)], axis=1)
    inv[:, cb] = np.where(m % 2 == 0, 1.0, -1.0) / n
    return fwd.astype(np.float32), inv.astype(np.float32)


def _hconv_kernel(x0_ref, x1_ref, v_ref, cw0_ref, cw1_ref, cw2_ref, cb0_ref, cb1_ref, cb2_ref,
                  taps_ref, fb_ref, fwd_ref, inv_ref, o_ref,
                  hs_ref, stage_ref, w32_ref, w_ref, x0c_ref, u_ref, y_ref, *, seq_len, blk, n_seq):
    L = seq_len
    nb = L // blk
    cw = x0_ref.shape[1]
    RC = 256
    bi = pl.program_id(1)
    row0 = lax.broadcasted_iota(jnp.int32, (RC, cw), 0) == 0

    @pl.when(bi == 0)
    def _():
        rows = lax.broadcasted_iota(jnp.int32, (2 * blk, 1), 0)
        sign = jnp.where(rows % 2 == 0, 1.0, -1.0)
        real_row = rows <= blk
        prev = None
        for jb in range(2 * nb):
            cur = jnp.dot(fwd_ref[...], taps_ref[jb * blk:(jb + 1) * blk, :].astype(BF16),
                          preferred_element_type=F32)
            if prev is not None:
                first_tap = taps_ref[(jb - 1) * blk:(jb - 1) * blk + 1, :].astype(BF16).astype(F32)
                hs_ref[jb - 1] = cur + sign * (prev - jnp.where(real_row, first_tap, 0.0))
            prev = cur

    zeros8 = jnp.zeros((8, cw), F32)
    for s in range(n_seq):
        r0 = s * L

        def short_conv(k, src_ref, cw_ref, cb_ref, store):
            stage_ref[k, 0:8, :] = zeros8
            stage_ref[k, 8 + L:16 + L, :] = zeros8
            for c in range(L // RC):
                stage_ref[k, 8 + c * RC:8 + (c + 1) * RC, :] = \
                    src_ref[r0 + c * RC:r0 + (c + 1) * RC, :].astype(F32)
            w = cw_ref[...]
            for c in range(L // RC):
                lo = stage_ref[k, 7 + c * RC:7 + (c + 1) * RC, :]
                mid = stage_ref[k, 8 + c * RC:8 + (c + 1) * RC, :]
                hi = stage_ref[k, 9 + c * RC:9 + (c + 1) * RC, :]
                store(c, lo * w[0:1, :] + mid * w[1:2, :] + hi * w[2:3, :] + cb_ref[...])

        def st_x1(c, val):
            w32_ref[c * RC:(c + 1) * RC, :] = val

        def st_v(c, val):
            w_ref[c * RC:(c + 1) * RC, :] = (w32_ref[c * RC:(c + 1) * RC, :] * val).astype(BF16)

        def st_x0(c, val):
            x0c_ref[c * RC:(c + 1) * RC, :] = val.astype(BF16)

        short_conv(0, x1_ref, cw1_ref, cb1_ref, st_x1)
        short_conv(1, v_ref, cw2_ref, cb2_ref, st_v)
        short_conv(2, x0_ref, cw0_ref, cb0_ref, st_x0)

        for j in range(nb):
            u_ref[j] = jnp.dot(fwd_ref[...], w_ref[j * blk:(j + 1) * blk, :], preferred_element_type=F32)

        for i in range(nb):
            for c in range(blk // RC):
                re = None
                im = None
                for j in range(nb):
                    k = i - j + nb - 1
                    a = u_ref[j, c * RC:(c + 1) * RC, :]
                    b = u_ref[j, blk + c * RC:blk + (c + 1) * RC, :]
                    hr = hs_ref[k, c * RC:(c + 1) * RC, :]
                    hi = hs_ref[k, blk + c * RC:blk + (c + 1) * RC, :]
                    bb = b * hi
                    if c == 0:
                        t_re = a * hr - jnp.where(row0, 0.0, bb)
                        t_im = jnp.where(row0, bb, a * hi + b * hr)
                    else:
                        t_re = a * hr - bb
                        t_im = a * hi + b * hr
                    re = t_re if re is None else re + t_re
                    im = t_im if im is None else im + t_im
                y_ref[i, c * RC:(c + 1) * RC, :] = re.astype(BF16)
                y_ref[i, blk + c * RC:blk + (c + 1) * RC, :] = im.astype(BF16)
            conv = jnp.dot(inv_ref[...], y_ref[i], preferred_element_type=F32)
            sl = slice(i * blk, (i + 1) * blk)
            z = x0c_ref[sl, :].astype(F32) * (conv + w_ref[sl, :].astype(F32) * fb_ref[...])
            o_ref[r0 + i * blk:r0 + (i + 1) * blk, :] = z.astype(BF16)


def _hyena_conv(u, conv_w, conv_b, taps, filt_bias, batch, seq_len, blk, n_seq, row_off=0, cw=256):
    T = batch * seq_len
    L = seq_len
    nb = L // blk
    ncw = D_MODEL // cw
    fwd_np, inv_np = _dft_mats(blk)
    fwd = jnp.asarray(fwd_np).astype(BF16)
    inv = jnp.asarray(inv_np).astype(BF16)
    rows = n_seq * L
    assert row_off % rows == 0
    boff = row_off // rows
    sec = lambda s: pl.BlockSpec((rows, cw), lambda c, b: (b + boff, s * ncw + c))
    cws = lambda s: pl.BlockSpec((3, cw), lambda c, b: (0, s * ncw + c))
    cbs = lambda s: pl.BlockSpec((1, cw), lambda c, b: (0, s * ncw + c))
    return pl.pallas_call(
        functools.partial(_hconv_kernel, seq_len=L, blk=blk, n_seq=n_seq),
        grid=(ncw, batch // n_seq),
        in_specs=[sec(0), sec(1), sec(2), cws(0), cws(1), cws(2), cbs(0), cbs(1), cbs(2),
                  pl.BlockSpec((2 * L, cw), lambda c, b: (0, c), pipeline_mode=pl.Buffered(1)),
                  pl.BlockSpec((1, cw), lambda c, b: (0, c)),
                  pl.BlockSpec((2 * blk, blk), lambda c, b: (0, 0), pipeline_mode=pl.Buffered(1)),
                  pl.BlockSpec((blk, 2 * blk), lambda c, b: (0, 0), pipeline_mode=pl.Buffered(1))],
        out_specs=pl.BlockSpec((rows, cw), lambda c, b: (b, c)),
        out_shape=jax.ShapeDtypeStruct((T, D_MODEL), BF16),
        scratch_shapes=[
            pltpu.VMEM((2 * nb - 1, 2 * blk, cw), F32),
            pltpu.VMEM((3, L + 16, cw), F32),
            pltpu.VMEM((L, cw), F32),
            pltpu.VMEM((L, cw), BF16),
            pltpu.VMEM((L, cw), BF16),
            pltpu.VMEM((nb, 2 * blk, cw), F32),
            pltpu.VMEM((nb, 2 * blk, cw), BF16),
        ],
        compiler_params=_cparams(("arbitrary", "arbitrary")),
        name="hyena_conv",
    )(u, u, u, conv_w, conv_w, conv_w, conv_b.reshape(1, -1), conv_b.reshape(1, -1), conv_b.reshape(1, -1),
      taps, filt_bias.reshape(1, -1), fwd, inv)


def kernel(x_prompt, x_sample, cache_diff_k, cache_diff_v, state_ret_fwd, state_ret_bwd, c, c_ctx, ada_w, ada_b, ln_w, ln_b, ev_w_in, ev_w_out, ret_decay_fwd, ret_decay_bwd, ret_gn_w, diff_lambda, diff_subln_w, hy_w_in, hy_conv_w, hy_conv_b, hy_ffn_w1, hy_ffn_b1, hy_ffn_w2, hy_ffn_b2, hy_ffn_w3, hy_freq, hy_filter_bias, hy_w_out, router_w, router_bias, moe_w_gate, moe_w_up, moe_w_down):
    B, S, D = x_prompt.shape
    DB, DS, _ = x_sample.shape
    PAST = cache_diff_k.shape[2]
    TP, TS = B * S, DB * DS
    T_ALL = TP + TS
    assert D == D_MODEL and 1 + DB <= COND_ROWS and T_ALL < 65536

    cond8 = jnp.zeros((COND_ROWS, D), F32).at[0].set(c_ctx).at[1:1 + DB].set(c)
    mod = _ada_mod(cond8, ada_w, ada_b).reshape(DEPTH * COND_ROWS * N_MOD, 1, D)

    tm = 512
    groups = [
        dict(x=x_prompt.reshape(TP, D), x_off=0, batch=B, seq=S, off=0, cond_row=lambda i, tm: 0),
        dict(x=x_sample.reshape(TS, D), x_off=0, batch=DB, seq=DS, off=TP,
             cond_row=lambda i, tm: 1 + (i * tm) // DS),
    ]
    u_next = None

    def cond_row_all(i, tm):
        return jnp.where(i * tm < TP, 0, 1 + (i * tm - TP) // DS)

    rw_pad = jnp.zeros((D, LANES), F32).at[:, :N_EXPERTS].set(router_w.astype(F32))
    rw_hi = rw_pad.astype(BF16)
    rw_lo = jnp.concatenate([rw_hi, (rw_pad - rw_hi.astype(F32)).astype(BF16)], axis=1)
    rwt = router_w.T.astype(BF16)
    rbias = router_bias.astype(F32)
    n_tiles = T_ALL // MOE_TM + N_CLASSES
    outs = {}

    for l in range(DEPTH):
        mixes = []
        if l % 2 == 0:
            e = l // 2
            w_in = ev_w_in[e]
            w_out = ev_w_out[e]
            lg = jnp.stack([jnp.log1p(-jnp.exp2(ret_decay_fwd[e].astype(F32))),
                            jnp.log1p(-jnp.exp2(ret_decay_bwd[e].astype(F32)))])
            lam_init = 0.8 - 0.6 * math.exp(-0.3 * l)
            lq1, lk1, lq2, lk2 = diff_lambda[e].astype(F32)
            lam = (jnp.exp(jnp.sum(lq1 * lk1)) - jnp.exp(jnp.sum(lq2 * lk2)) + lam_init).reshape(1)
            kscale = RET_DK ** -0.5
            for gi, g in enumerate(groups):
                tm_in = tm
                crow = functools.partial(g["cond_row"], tm=tm_in)
                if gi == 0:
                    secs = (("none", 1.0, False), ("none", kscale, False), ("none", 1.0, False),
                            ("none", 1.0, False), ("none", 1.0, False), ("none", 1.0, True),
                            ("none", 1.0, True))
                    proj, kd, vd = _in_proj(g["x"], mod, l, w_in, secs, crow, tm_in, tm=tm_in)
                    outs.setdefault("kd", []).append(kd.reshape(B, 1, S, H_DIFF, 2 * DIFF_QK))
                    outs.setdefault("vd", []).append(vd.reshape(B, 1, S, H_DIFF, DIFF_V))
                    ret, sf, sb = _retention(proj, lg, ret_gn_w[e], g["batch"], g["seq"], write_state=True,
                                             heads=H_RET)
                    outs.setdefault("sf", []).append(sf.reshape(B, 1, H_RET, RET_DK, RET_DV))
                    outs.setdefault("sb", []).append(sb.reshape(B, 1, H_RET, RET_DK, RET_DV))
                    att = _diff_attention(proj, lam, diff_subln_w[e], lam_init, g["batch"], g["seq"],
                                          tq=min(256, g["seq"]), heads=H_DIFF)
                else:
                    secs = (("ret", 1.0, False), ("ret", kscale, False), ("none", 1.0, False),
                            ("none", 1.0, False), ("diff", 1.0, False), ("diff", 1.0, False),
                            ("none", 1.0, False))
                    tabs = {"ret": _rope_tables(g["seq"], RET_DK), "diff": _rope_tables(g["seq"], DIFF_QK)}
                    (proj,) = _in_proj(g["x"], mod, l, w_in, secs, crow, g["seq"], rope_tabs=tabs, tm=tm_in)
                    ret, = _retention(proj, lg, ret_gn_w[e], g["batch"], g["seq"],
                                      states=(state_ret_fwd[:, e], state_ret_bwd[:, e]))
                    ctx = (cache_diff_k[:, e].reshape(DB, PAST, H_DIFF * 2 * DIFF_QK),
                           cache_diff_v[:, e].reshape(DB, PAST, H_DIFF * DIFF_V))
                    att = _diff_attention(proj, lam, diff_subln_w[e], lam_init, g["batch"], g["seq"], ctx=ctx,
                                          tq=1024, heads=1)
                mixes.append((ret, att))
        else:
            o = l // 2
            w_in = hy_w_in[o]
            w_out = hy_w_out[o]
            secs = (("none", 1.0, False),) * (3 * D // SEC)
            for gi, g in enumerate(groups):
                if u_next is None:
                    crow = functools.partial(g["cond_row"], tm=tm)
                    (u,) = _in_proj(g["x"], mod, l, w_in, secs, crow, tm, tm=tm)
                    u_off = 0
                else:
                    u, u_off = u_next, g["off"]
                taps = _hyena_taps(g["seq"], hy_ffn_w1[o], hy_ffn_b1[o], hy_ffn_w2[o], hy_ffn_b2[o],
                                   hy_ffn_w3[o], hy_freq[o])
                blk = min(g["seq"], 512)
                n_seq = max(1, 1024 // g["seq"])
                z = _hyena_conv(u, hy_conv_w[o], hy_conv_b[o], taps, hy_filter_bias[o], g["batch"], g["seq"],
                                blk, n_seq, row_off=u_off)
                mixes.append((z,))

        x1_all, h2_all, info_all, counts = _post_mix(mixes, [(g["x"], g["x_off"]) for g in groups], w_out, mod, l, ln_w[l, 0],
                                                     ln_b[l, 0], rw_hi, rw_lo, rbias, cond_row_all, tm)
        packed = info_all
        counts_i = counts[:, 0].astype(jnp.int32)
        plan = _moe_plan(counts_i, n_tiles)
        hs = _dispatch(h2_all, packed, plan, counts_i, n_tiles)
        ys = _moe(hs, plan, l, rwt, moe_w_gate, moe_w_up, moe_w_down)
        if l + 1 < DEPTH and (l + 1) % 2 == 1:
            x_all, u_next = _fin_proj(x1_all, ys, packed, plan[3], mod, l, ln_w[l, 1], ln_b[l, 1],
                                      hy_w_in[(l + 1) // 2], cond_row_all, tm=tm)
            for g in groups:
                g["x"], g["x_off"] = x_all, g["off"]
        else:
            u_next = None
            xs = _fin(x1_all, ys, packed, plan[3], mod, l, ln_w[l, 1], ln_b[l, 1], cond_row_all, (TP, TS))
            for g, x in zip(groups, xs):
                g["x"], g["x_off"] = x, 0

    y_prompt = groups[0]["x"].reshape(B, S, D)
    y_sample = groups[1]["x"].reshape(DB, DS, D)
    cat = lambda xs: xs[0] if len(xs) == 1 else jnp.concatenate(xs, axis=1)
    return (y_prompt, y_sample, cat(outs["kd"]), cat(outs["vd"]), cat(outs["sf"]), cat(outs["sb"]))
```

```python
import functools
import math

import numpy as np
import jax
import jax.numpy as jnp
from jax import lax
from jax.experimental import pallas as pl
from jax.experimental.pallas import tpu as pltpu

F32 = jnp.float32
BF16 = jnp.bfloat16
HIGHEST = lax.Precision.HIGHEST

D_MODEL = 1024
DEPTH = 2
GRID_W = 64
H_RET = 4
RET_DK = 128
RET_DV = 128
RET_CHUNK = 256
H_DIFF = 4
DIFF_QK = 64
DIFF_V = 128
ROPE_BASE = 10000.0
HY_BANDS = 16
HY_FH = 64
HY_DECAY_TARGET = 1e-2
HY_FAST = 0.3
HY_SLOW = 1.5
HY_SHIFT = 0.05
N_EXPERTS = 16
N_GROUPS = 4
GROUP_SIZE = N_EXPERTS // N_GROUPS
D_FF_EXPERT = 512
ALPHA = (2 * DEPTH) ** 0.25
LN_EPS = 1e-5

LANES = 128
SEC = 512
COND_ROWS = 8
N_MOD = 6
VMEM_LIMIT = 50 * 1024 * 1024

PAIR_SLOTS = ((0, 1), (2, 1), (2, 3), (0, 3), (0, 2), (1, 3))
N_PAIRS = len(PAIR_SLOTS)
N_CLASSES = N_GROUPS * N_PAIRS
MOE_TM = 256
MOE_SHIFT = MOE_TM.bit_length() - 1
POST_SUB = 512
TOK_ROWS = D_MODEL // LANES


def _cparams(sem):
    return pltpu.CompilerParams(dimension_semantics=sem, vmem_limit_bytes=VMEM_LIMIT)


def _silu(x):
    return x * jax.nn.sigmoid(x)


def _store_token_tiles(ref, x):
    n = x.shape[0]
    for j in range(TOK_ROWS):
        ref[pl.ds(j, n, stride=TOK_ROWS), :] = x[:, j * LANES:(j + 1) * LANES]


def _load_token_tiles(ref, n, slot=None):
    idx = () if slot is None else (slot,)
    return jnp.concatenate([ref[idx + (pl.ds(j, n, stride=TOK_ROWS), slice(None))] for j in range(TOK_ROWS)],
                           axis=1)


def _layer_norm(x, w, b):
    mu = jnp.mean(x, axis=-1, keepdims=True)
    xc = x - mu
    var = jnp.mean(xc * xc, axis=-1, keepdims=True)
    return xc * lax.rsqrt(var + LN_EPS) * w + b


def _dot_3pass(a, w):
    a_hi = a.astype(BF16)
    a_lo = (a - a_hi.astype(F32)).astype(BF16)
    w_hi = w.astype(BF16)
    w_lo = (w - w_hi.astype(F32)).astype(BF16)
    return (jnp.dot(a_hi, w_hi, preferred_element_type=F32) + jnp.dot(a_lo, w_hi, preferred_element_type=F32)
            + jnp.dot(a_hi, w_lo, preferred_element_type=F32))


def _ada_kernel(c_ref, w_ref, b_ref, o_ref):
    o_ref[...] = _dot_3pass(_silu(c_ref[...]), w_ref[...]) + b_ref[...]


def _ada_mod(cond8, ada_w, ada_b):
    tn = 1024
    nj = ada_w.shape[2] // tn
    return pl.pallas_call(
        _ada_kernel,
        grid=(DEPTH, nj),
        in_specs=[
            pl.BlockSpec((COND_ROWS, D_MODEL), lambda l, j: (0, 0)),
            pl.BlockSpec((None, D_MODEL, tn), lambda l, j: (l, 0, j)),
            pl.BlockSpec((None, 1, tn), lambda l, j: (l, 0, j)),
        ],
        out_specs=pl.BlockSpec((None, COND_ROWS, tn), lambda l, j: (l, 0, j)),
        out_shape=jax.ShapeDtypeStruct((DEPTH, COND_ROWS, ada_w.shape[2]), F32),
        compiler_params=_cparams(("arbitrary", "arbitrary")),
        name="ada_mod",
    )(cond8, ada_w, ada_b.reshape(DEPTH, 1, -1))


def _mod_spec(layer, chunk, row_of_tile):
    def imap(i, *_):
        return ((layer * COND_ROWS + row_of_tile(i)) * N_MOD + chunk, 0, 0)
    return pl.BlockSpec((None, 1, D_MODEL), imap)


def _rope(a, tabs, quarter):
    c, sa, sb = tabs
    out = []
    for hb in range(a.shape[1] // LANES):
        blk = a[:, hb * LANES:(hb + 1) * LANES]
        up = pltpu.roll(blk, LANES - quarter, axis=1)
        dn = pltpu.roll(blk, quarter, axis=1)
        out.append(blk * c + up * sa + dn * sb)
    return jnp.concatenate(out, axis=1)


def _cast_once(w_ref, w_bf16_ref):
    @pl.when(pl.program_id(0) == 0)
    def _():
        for c in range(0, w_ref.shape[1], SEC):
            w_bf16_ref[:, c:c + SEC] = w_ref[:, c:c + SEC].astype(BF16)


def _in_kernel(*refs, secs, n_f32_out):
    x_ref, sh_ref, sc_ref, w_ref = refs[:4]
    pos = 4
    tabs = {}
    for kind in ("ret", "diff"):
        if any(s[0] == kind for s in secs):
            tabs[kind] = tuple(r[...] for r in refs[pos:pos + 3])
            pos += 3
    o_ref = refs[pos]
    f32_refs = refs[pos + 1:pos + 1 + n_f32_out]
    w_bf16_ref = refs[-1]
    _cast_once(w_ref, w_bf16_ref)
    h = (x_ref[...] * (1.0 + sc_ref[...]) + sh_ref[...]).astype(BF16)
    k32 = 0
    for s, (kind, scale, want_f32) in enumerate(secs):
        acc = jnp.dot(h, w_bf16_ref[:, s * SEC:(s + 1) * SEC], preferred_element_type=F32)
        if scale != 1.0:
            acc = acc * scale
        if kind == "ret":
            acc = _rope(acc, tabs["ret"], RET_DK // 4)
        elif kind == "diff":
            acc = _rope(acc, tabs["diff"], DIFF_QK // 4)
        o_ref[:, s * SEC:(s + 1) * SEC] = acc.astype(BF16)
        if want_f32:
            for hb in range(SEC // LANES):
                f32_refs[k32][:, hb, :] = acc[:, hb * LANES:(hb + 1) * LANES]
            k32 += 1
    assert k32 == n_f32_out


def _in_proj(x2d, mod, layer, w_f32, secs, row_of_tile, seq_len, rope_tabs=None, tm=512):
    T = x2d.shape[0]
    N = w_f32.shape[1]
    assert N == SEC * len(secs) and T % tm == 0 and seq_len % tm == 0
    tiles_per_seq = seq_len // tm
    in_specs = [
        pl.BlockSpec((tm, D_MODEL), lambda i: (i, 0)),
        _mod_spec(layer, 0, row_of_tile),
        _mod_spec(layer, 1, row_of_tile),
        pl.BlockSpec((D_MODEL, N), lambda i: (0, 0), pipeline_mode=pl.Buffered(1)),
    ]
    args = [x2d, mod, mod, w_f32]
    for kind in ("ret", "diff"):
        if any(s[0] == kind for s in secs):
            for t in rope_tabs[kind]:
                in_specs.append(pl.BlockSpec((tm, LANES), lambda i: (i % tiles_per_seq, 0)))
                args.append(t)
    n_f32 = sum(1 for s in secs if s[2])
    heads = SEC // LANES
    out_shape = [jax.ShapeDtypeStruct((T, N), BF16)] + [jax.ShapeDtypeStruct((T, heads, LANES), F32)] * n_f32
    out_specs = ([pl.BlockSpec((tm, N), lambda i: (i, 0))]
                 + [pl.BlockSpec((tm, heads, LANES), lambda i: (i, 0, 0))] * n_f32)
    return pl.pallas_call(
        functools.partial(_in_kernel, secs=secs, n_f32_out=n_f32),
        grid=(T // tm,),
        in_specs=in_specs,
        out_specs=out_specs,
        out_shape=out_shape,
        scratch_shapes=[pltpu.VMEM((D_MODEL, N), BF16)],
        compiler_params=_cparams(("arbitrary",)),
        name="in_proj",
    )(*args)


def _rope_tables(seq_len, d):
    half = d // 2
    quarter = half // 2
    t = np.arange(seq_len)
    inv = ROPE_BASE ** (-np.arange(quarter, dtype=np.float64) / quarter)
    ang_r = (t // GRID_W)[:, None] * inv[None, :]
    ang_c = (t % GRID_W)[:, None] * inv[None, :]
    zero = np.zeros_like(ang_r)
    cos = np.concatenate([np.cos(ang_r)] * 2 + [np.cos(ang_c)] * 2, axis=1)
    sa = np.concatenate([-np.sin(ang_r), zero, -np.sin(ang_c), zero], axis=1)
    sb = np.concatenate([zero, np.sin(ang_r), zero, np.sin(ang_c)], axis=1)
    reps = LANES // d
    return tuple(jnp.asarray(np.tile(a, (1, reps)).astype(np.float32)) for a in (cos, sa, sb))


def _ret_kernel(*refs, n_chunks, has_state, write_state, heads):
    lg_ref, q_ref, k_ref, v_ref, g_ref, gnw_ref = refs[:6]
    pos = 6
    if has_state:
        s0f_ref, s0b_ref = refs[pos:pos + 2]
        pos += 2
    o_ref = refs[pos]
    pos += 1
    if write_state:
        sf_ref, sb_ref = refs[pos:pos + 2]
        pos += 2
    acc_ref, kv_ref = refs[pos:pos + 2]
    for hh in range(heads):
        _ret_head(hh, heads, lg_ref, q_ref, k_ref, v_ref, g_ref, gnw_ref,
                  (s0f_ref, s0b_ref) if has_state else None, o_ref, (sf_ref, sb_ref) if write_state else None,
                  acc_ref, kv_ref, n_chunks)


def _ret_head(hh, heads, lg_ref, q_ref, k_ref, v_ref, g_ref, gnw_ref, s0_refs, o_ref, s_out_refs, acc_ref, kv_ref,
              n_chunks):
    C = q_ref.shape[0] // n_chunks
    cols = slice(hh * LANES, (hh + 1) * LANES)
    hd = pl.program_id(1) * heads + hh
    lgf = lg_ref[0, hd]
    lgb = lg_ref[1, hd]
    ii = lax.broadcasted_iota(jnp.int32, (C, C), 0).astype(F32)
    jj = lax.broadcasted_iota(jnp.int32, (C, C), 1).astype(F32)
    rel = ii - jj
    d_f = jnp.where(rel >= 0, jnp.exp(jnp.maximum(rel, 0.0) * lgf), 0.0)
    d_b = jnp.where(rel <= 0, jnp.exp(jnp.maximum(-rel, 0.0) * lgb), 0.0)
    d_sum = d_f + d_b
    idx = lax.broadcasted_iota(jnp.int32, (C, 1), 0).astype(F32)
    xi_f = jnp.exp((idx + 1.0) * lgf)
    zeta_f = jnp.exp((C - 1.0 - idx) * lgf)
    xi_b = jnp.exp((C - idx) * lgb)
    zeta_b = jnp.exp(idx * lgb)
    one = jnp.ones((1, 1), F32)
    gc_f = jnp.exp(one * (C * lgf))
    gc_b = jnp.exp(one * (C * lgb))

    nt = (((1,), (1,)), ((), ()))
    tn = (((0,), (0,)), ((), ()))

    if s0_refs is not None:
        s_f = s0_refs[0][hh]
        s_b = s0_refs[1][hh]
    else:
        s_f = jnp.zeros((RET_DK, RET_DV), F32)
        s_b = jnp.zeros((RET_DK, RET_DV), F32)

    for n in range(n_chunks):
        sl = slice(n * C, (n + 1) * C)
        qc, kc, vc = q_ref[sl, cols], k_ref[sl, cols], v_ref[sl, cols]
        scores = lax.dot_general(qc, kc, nt, preferred_element_type=F32) * d_sum
        acc_ref[sl, cols] = jnp.dot(scores.astype(BF16), vc, preferred_element_type=F32)
        kf = kc.astype(F32)
        kv_ref[0, hh, n] = lax.dot_general((kf * zeta_f).astype(BF16), vc, tn, preferred_element_type=F32)
        kv_ref[1, hh, n] = lax.dot_general((kf * zeta_b).astype(BF16), vc, tn, preferred_element_type=F32)

    for n in range(n_chunks):
        sl = slice(n * C, (n + 1) * C)
        qf = q_ref[sl, cols].astype(F32)
        acc_ref[sl, cols] += jnp.dot((qf * xi_f).astype(BF16), s_f.astype(BF16), preferred_element_type=F32)
        s_f = gc_f * s_f + kv_ref[0, hh, n]

    gnw = gnw_ref[:, cols]
    for n in reversed(range(n_chunks)):
        sl = slice(n * C, (n + 1) * C)
        qf = q_ref[sl, cols].astype(F32)
        cross = jnp.dot((qf * xi_b).astype(BF16), s_b.astype(BF16), preferred_element_type=F32)
        r = acc_ref[sl, cols] + cross
        mu = jnp.mean(r, axis=-1, keepdims=True)
        rc = r - mu
        var = jnp.mean(rc * rc, axis=-1, keepdims=True)
        rn = rc * lax.rsqrt(var + LN_EPS) * gnw
        o_ref[sl, cols] = (_silu(g_ref[sl, cols].astype(F32)) * rn).astype(BF16)
        s_b = gc_b * s_b + kv_ref[1, hh, n]

    if s_out_refs is not None:
        s_out_refs[0][hh] = s_f
        s_out_refs[1][hh] = s_b


def _retention(proj, lg, gn_w, batch, seq_len, states=None, write_state=False, heads=2):
    T = proj.shape[0]
    hp = heads
    width = hp * LANES
    per_sec = SEC // width
    blk = lambda sec: pl.BlockSpec((seq_len, width), lambda b, h: (b, sec * per_sec + h))
    in_specs = [pl.BlockSpec(memory_space=pltpu.SMEM), blk(0), blk(1), blk(2), blk(3),
                pl.BlockSpec((1, width), lambda b, h: (0, h))]
    args = [lg, proj, proj, proj, proj, gn_w.reshape(1, -1)]
    if states is not None:
        st = pl.BlockSpec((None, hp, RET_DK, RET_DV), lambda b, h: (b, h, 0, 0))
        in_specs += [st, st]
        args += list(states)
    n_chunks = max(1, seq_len // RET_CHUNK)
    out_shape = [jax.ShapeDtypeStruct((T, SEC), BF16)]
    out_specs = [pl.BlockSpec((seq_len, width), lambda b, h: (b, h))]
    if write_state:
        st_o = pl.BlockSpec((None, hp, RET_DK, RET_DV), lambda b, h: (b, h, 0, 0))
        out_shape += [jax.ShapeDtypeStruct((batch, H_RET, RET_DK, RET_DV), F32)] * 2
        out_specs += [st_o, st_o]
    return pl.pallas_call(
        functools.partial(_ret_kernel, n_chunks=n_chunks, has_state=states is not None,
                          write_state=write_state, heads=hp),
        grid=(batch, H_RET // hp),
        in_specs=in_specs,
        out_specs=out_specs,
        out_shape=out_shape,
        scratch_shapes=[pltpu.VMEM((seq_len, width), F32),
                        pltpu.VMEM((2, hp, n_chunks, RET_DK, RET_DV), F32)],
        compiler_params=_cparams(("arbitrary", "arbitrary")),
        name="retention",
    )(*args)


def _att_kernel(*refs, has_ctx, out_scale, key_chunk, heads):
    lam_ref, q_ref, k_ref, v_ref = refs[:4]
    pos = 4
    if has_ctx:
        ck_ref, cv_ref = refs[pos:pos + 2]
        pos += 2
    w_ref, o_ref = refs[pos:pos + 2]
    lam = lam_ref[0]
    tq = q_ref.shape[0]
    nt = (((1,), (1,)), ((), ()))
    for hh in range(heads):
        cols = slice(hh * LANES, (hh + 1) * LANES)
        q = q_ref[:, cols]
        lane = lax.broadcasted_iota(jnp.int32, q.shape, 1)
        zero = jnp.zeros_like(q)
        qq = jnp.concatenate([jnp.where(lane < DIFF_QK, q, zero), jnp.where(lane >= DIFF_QK, q, zero)], axis=0)
        qq = qq * jnp.asarray(DIFF_QK ** -0.5, BF16)
        chunks = [(k_ref, v_ref, c * key_chunk, key_chunk) for c in range(k_ref.shape[0] // key_chunk)]
        if has_ctx:
            chunks.append((ck_ref, cv_ref, 0, ck_ref.shape[0]))
        m = l = acc = None
        for kr, vr, off, n in chunks:
            kch = kr[off:off + n, cols].astype(BF16)
            vch = vr[off:off + n, cols].astype(BF16)
            s = lax.dot_general(qq, kch, nt, preferred_element_type=F32)
            cm = jnp.max(s, axis=-1, keepdims=True)
            m_new = cm if m is None else jnp.maximum(m, cm)
            p = jnp.exp(s - m_new)
            ps = jnp.sum(p, axis=-1, keepdims=True)
            pv = jnp.dot(p.astype(BF16), vch, preferred_element_type=F32)
            if m is None:
                l, acc = ps, pv
            else:
                alpha = jnp.exp(m - m_new)
                l = alpha * l + ps
                acc = alpha * acc + pv
            m = m_new
        o = acc / l
        att = o[:tq] - lam * o[tq:]
        att = att * lax.rsqrt(jnp.mean(att * att, axis=-1, keepdims=True) + LN_EPS)
        o_ref[:, cols] = (att * w_ref[...] * out_scale).astype(BF16)


def _diff_attention(proj, lam, subln_w, lam_init, batch, seq_len, ctx=None, tq=256, heads=1):
    T = proj.shape[0]
    width = heads * LANES
    per_sec = SEC // width
    nq = seq_len // tq
    in_specs = [
        pl.BlockSpec(memory_space=pltpu.SMEM),
        pl.BlockSpec((tq, width), lambda b, h, i: (b * nq + i, 4 * per_sec + h)),
        pl.BlockSpec((seq_len, width), lambda b, h, i: (b, 5 * per_sec + h)),
        pl.BlockSpec((seq_len, width), lambda b, h, i: (b, 6 * per_sec + h)),
    ]
    args = [lam, proj, proj, proj]
    if ctx is not None:
        ck, cv = ctx
        past = ck.shape[1]
        cspec = pl.BlockSpec((None, past, width), lambda b, h, i: (b, 0, h))
        in_specs += [cspec, cspec]
        args += [ck, cv]
    in_specs.append(pl.BlockSpec((1, LANES), lambda b, h, i: (0, 0)))
    args.append(subln_w.reshape(1, -1))
    return pl.pallas_call(
        functools.partial(_att_kernel, has_ctx=ctx is not None, out_scale=1.0 - lam_init,
                          key_chunk=min(512, seq_len), heads=heads),
        grid=(batch, H_DIFF // heads, nq),
        in_specs=in_specs,
        out_specs=pl.BlockSpec((tq, width), lambda b, h, i: (b * nq + i, h)),
        out_shape=jax.ShapeDtypeStruct((T, SEC), BF16),
        compiler_params=_cparams(("arbitrary", "arbitrary", "arbitrary")),
        name="diff_attention",
    )(*args)


def _route_class(lt, rb_ref):
    sel = [jax.nn.sigmoid(lt[e:e + 1, :]) + rb_ref[e] for e in range(N_EXPERTS)]
    gscore = []
    for g in range(N_GROUPS):
        mem = sel[g * GROUP_SIZE:(g + 1) * GROUP_SIZE]
        best = None
        for a in range(GROUP_SIZE):
            for b in range(a + 1, GROUP_SIZE):
                pair = mem[a] + mem[b]
                best = pair if best is None else jnp.maximum(best, pair)
        gscore.append(best)
    gbest = gscore[0]
    gidx = jnp.zeros_like(gbest)
    for g in range(1, N_GROUPS):
        upd = gscore[g] > gbest
        gidx = jnp.where(upd, float(g), gidx)
        gbest = jnp.where(upd, gscore[g], gbest)
    msel = []
    for j in range(GROUP_SIZE):
        out = sel[j]
        for g in range(1, N_GROUPS):
            out = jnp.where(gidx == float(g), sel[g * GROUP_SIZE + j], out)
        msel.append(out)
    one = jnp.ones_like(gbest)
    zero = jnp.zeros_like(gbest)
    chosen = []
    for j in range(GROUP_SIZE):
        rank = zero
        for k in range(GROUP_SIZE):
            if k < j:
                rank = rank + jnp.where(msel[k] >= msel[j], one, zero)
            elif k > j:
                rank = rank + jnp.where(msel[k] > msel[j], one, zero)
        chosen.append(jnp.where(rank < 2.0, one, zero))
    c0, c1, c2, c3 = chosen
    order = jnp.where(c0 * c1 > 0, 0.0, jnp.where(c1 * c2 > 0, 1.0, jnp.where(c2 * c3 > 0, 2.0,
            jnp.where(c0 * c3 > 0, 3.0, jnp.where(c0 * c2 > 0, 4.0, 5.0)))))
    return gidx * float(N_PAIRS) + order


def _post_kernel(*refs, n_mix, group_tiles):
    n_groups = len(group_tiles)
    per_group = n_mix + 1
    group_refs = [refs[g * per_group:(g + 1) * per_group] for g in range(n_groups)]
    refs = refs[n_groups * per_group:]
    (w_f32_ref, g_ref, sh_ref, sc_ref, lnw_ref, lnb_ref, rwh_ref, rwl_ref, rb_ref, tri_ref,
     x1_ref, h2_ref, info_ref, cout_ref, cnt_ref, pre_ref, w_ref) = refs
    _cast_once(w_f32_ref, w_ref)
    i = pl.program_id(0)

    @pl.when(i == 0)
    def _():
        cnt_ref[...] = jnp.zeros_like(cnt_ref)

    first = 0
    for g in range(n_groups):
        @pl.when(jnp.logical_and(i >= first, i < first + group_tiles[g]))
        def _(g=g):
            out = None
            off = 0
            for m_ref in group_refs[g][:n_mix]:
                width = m_ref.shape[1]
                part = jnp.dot(m_ref[...], w_ref[off:off + width, :], preferred_element_type=F32)
                out = part if out is None else out + part
                off += width
            pre_ref[...] = ALPHA * group_refs[g][n_mix][...] + g_ref[...] * out
        first += group_tiles[g]

    tm = pre_ref.shape[0]
    ts = tri_ref.shape[0]
    base = cnt_ref[:, 0:1]
    for part in range(tm // ts):
        rows = slice(part * ts, (part + 1) * ts)
        x1 = _layer_norm(pre_ref[rows, :], lnw_ref[...], lnb_ref[...])
        x1_ref[rows, :] = x1
        h2 = x1 * (1.0 + sc_ref[...]) + sh_ref[...]
        for j in range(TOK_ROWS):
            h2_ref[pl.ds(part * ts * TOK_ROWS + j, ts, stride=TOK_ROWS), :] = h2[:, j * LANES:(j + 1) * LANES]
        h_hi = h2.astype(BF16)
        h_lo = (h2 - h_hi.astype(F32)).astype(BF16)
        both = jnp.dot(h_hi, rwl_ref[...], preferred_element_type=F32)
        logits = (both[:, :LANES] + both[:, LANES:]
                  + jnp.dot(h_lo, rwh_ref[...], preferred_element_type=F32))
        cls = _route_class(logits.T, rb_ref)
        crow = lax.broadcasted_iota(jnp.int32, (32, ts), 0).astype(F32)
        onehot = jnp.where(crow == cls, 1.0, 0.0)
        prefix = jnp.dot(onehot.astype(BF16), tri_ref[...], preferred_element_type=F32)
        rank = jnp.sum(onehot * (prefix - 1.0 + base), axis=0, keepdims=True)
        base = base + jnp.sum(onehot, axis=1, keepdims=True)
        packed = cls.astype(jnp.int32) * 65536 + rank.astype(jnp.int32)
        info_ref[:, rows] = packed
    cnt_ref[...] = jnp.broadcast_to(base, cnt_ref.shape)

    @pl.when(i == pl.num_programs(0) - 1)
    def _():
        cout_ref[...] = cnt_ref[...]


def _tri(tm):
    return jnp.asarray(np.triu(np.ones((tm, tm), np.float32))).astype(BF16)


def _group_spec(width, tm, first_tile, n_tiles, array_tile_off=0):
    return pl.BlockSpec((tm, width),
                        lambda i, *_: (jnp.clip(i - first_tile, 0, n_tiles - 1) + array_tile_off, 0))


def _post_mix(group_mixes, group_x, w_out_bf16, mod, layer, ln_w, ln_b, rw_hi, rw_lo, router_bias, cond_row, tm):
    group_tiles = tuple(m[0].shape[0] // tm for m in group_mixes)
    n_tiles = sum(group_tiles)
    total_rows = n_tiles * tm
    row = pl.BlockSpec((tm, D_MODEL), lambda i: (i, 0))
    vec = pl.BlockSpec((1, D_MODEL), lambda i: (0, 0))
    cnt = pl.BlockSpec((32, LANES), lambda i: (0, 0))
    in_specs, args = [], []
    first = 0
    for mixes, (x, x_row_off), nt in zip(group_mixes, group_x, group_tiles):
        for m in mixes:
            in_specs.append(_group_spec(m.shape[1], tm, first, nt))
            args.append(m)
        in_specs.append(_group_spec(D_MODEL, tm, first, nt, x_row_off // tm))
        args.append(x)
        first += nt
    in_specs += [
        pl.BlockSpec((D_MODEL, D_MODEL), lambda i: (0, 0), pipeline_mode=pl.Buffered(1)),
        _mod_spec(layer, 2, lambda i: cond_row(i, tm)), _mod_spec(layer, 3, lambda i: cond_row(i, tm)),
        _mod_spec(layer, 4, lambda i: cond_row(i, tm)),
        vec, vec,
        pl.BlockSpec((D_MODEL, LANES), lambda i: (0, 0)),
        pl.BlockSpec((D_MODEL, 2 * LANES), lambda i: (0, 0)),
        pl.BlockSpec(memory_space=pltpu.SMEM),
        pl.BlockSpec((POST_SUB, POST_SUB), lambda i: (0, 0)),
    ]
    args += [w_out_bf16, mod, mod, mod, ln_w.reshape(1, -1), ln_b.reshape(1, -1), rw_hi, rw_lo, router_bias,
             _tri(POST_SUB)]
    return pl.pallas_call(
        functools.partial(_post_kernel, n_mix=len(group_mixes[0]), group_tiles=group_tiles),
        grid=(n_tiles,),
        in_specs=in_specs,
        out_specs=[row, pl.BlockSpec((tm * TOK_ROWS, LANES), lambda i: (i, 0)),
                   pl.BlockSpec((1, tm), lambda i: (0, i)), cnt],
        out_shape=[jax.ShapeDtypeStruct((total_rows, D_MODEL), F32),
                   jax.ShapeDtypeStruct((total_rows * TOK_ROWS, LANES), F32),
                   jax.ShapeDtypeStruct((1, total_rows), jnp.int32), jax.ShapeDtypeStruct((32, LANES), F32)],
        scratch_shapes=[pltpu.VMEM((32, LANES), F32), pltpu.VMEM((tm, D_MODEL), F32),
                        pltpu.VMEM((D_MODEL, D_MODEL), BF16)],
        compiler_params=_cparams(("arbitrary",)),
        name="post_mix",
    )(*args)


def _plan_kernel(cnt_ref, ea_ref, eb_ref, fl_ref, rs_ref, *, n_tiles):
    start = jnp.int32(0)
    prev_a = jnp.int32(-1)
    prev_b = jnp.int32(-1)
    par_a = jnp.int32(0)
    par_b = jnp.int32(0)
    for c in range(N_CLASSES):
        n = cnt_ref[c]
        tiles = lax.shift_right_logical(n + (MOE_TM - 1), MOE_SHIFT)
        row0 = start * MOE_TM
        rs_ref[c] = row0
        g, pr = divmod(c, N_PAIRS)
        a = g * GROUP_SIZE + PAIR_SLOTS[pr][0]
        b = g * GROUP_SIZE + PAIR_SLOTS[pr][1]
        new_a = (prev_a != a).astype(jnp.int32)
        new_b = (prev_b != b).astype(jnp.int32)
        has = tiles > 0
        par_a = jnp.where(has, par_a ^ new_a, par_a)
        par_b = jnp.where(has, par_b ^ new_b, par_b)
        first = 1 + 4 * new_a + 8 * new_b
        common = 16 * par_a + 32 * par_b

        def tile_body(k, _, start=start, a=a, b=b, first=first, common=common):
            t = start + k
            ea_ref[t] = a
            eb_ref[t] = b
            fl_ref[t] = jnp.where(k == 0, first, 1) + common
            return 0

        lax.fori_loop(0, tiles, tile_body, 0)
        prev_a = jnp.where(has, a, prev_a)
        prev_b = jnp.where(has, b, prev_b)
        start = start + tiles
    for c in range(N_CLASSES, 31):
        rs_ref[c] = 0
    rs_ref[31] = start

    def idle_body(t, _):
        ea_ref[t] = prev_a
        eb_ref[t] = prev_b
        fl_ref[t] = 0
        return 0

    lax.fori_loop(start, n_tiles, idle_body, 0)


def _moe_plan(counts, n_tiles):
    smem = pl.BlockSpec(memory_space=pltpu.SMEM)
    i32 = lambda n: jax.ShapeDtypeStruct((n,), jnp.int32)
    return pl.pallas_call(
        functools.partial(_plan_kernel, n_tiles=n_tiles),
        in_specs=[smem],
        out_specs=[smem] * 4,
        out_shape=[i32(n_tiles), i32(n_tiles), i32(n_tiles), i32(32)],
        name="moe_plan",
    )(counts)


def _class_row(packed, rs_ref):
    return rs_ref[lax.shift_right_logical(packed, 16)] + (packed & 0xFFFF)


def _dispatch_kernel(packed_ref, rs_ref, cnt_ref, h_ref, hs_ref, sem, fill_sem, *, tm, n_tiles):
    i = pl.program_id(0)
    used = rs_ref[31]

    def tile_rows(row, n=1):
        return pl.ds(pl.multiple_of(row * TOK_ROWS, TOK_ROWS), n * TOK_ROWS)

    def fill_copy(row, n):
        return pltpu.make_async_copy(h_ref.at[tile_rows(0, n)], hs_ref.at[tile_rows(row, n)], fill_sem)

    def fill(act):
        for c in range(N_CLASSES):
            n = cnt_ref[c]
            row = rs_ref[c] + n
            pad = (-n) & (MOE_TM - 1)
            for bit in reversed(range(MOE_SHIFT)):
                size = 1 << bit

                @pl.when((pad & size) != 0)
                def _(row=row, size=size):
                    act(fill_copy(row, size))
                row = row + (pad & size)
        lax.fori_loop(used, n_tiles, lambda t, _: (act(fill_copy(t * MOE_TM, MOE_TM)), 0)[1], 0)

    @pl.when(i == 0)
    def _():
        fill(lambda cp: cp.start())

    def body(r8, _):
        for j in range(8):
            r = r8 * 8 + j
            row = _class_row(packed_ref[0, i * tm + r], rs_ref)
            pltpu.make_async_copy(h_ref.at[tile_rows(r)], hs_ref.at[tile_rows(row)], sem).start(priority=j % 2)
        return 0

    lax.fori_loop(0, tm // 8, body, 0)
    pltpu.make_async_copy(h_ref, hs_ref.at[tile_rows(0, tm)], sem).wait()

    @pl.when(i == 0)
    def _():
        fill(lambda cp: cp.wait())


def _dispatch(h2_tiles, packed, plan, counts, n_tiles, tm=2048):
    T = packed.shape[1]
    assert tm >= MOE_TM and T % tm == 0
    grid_spec = pltpu.PrefetchScalarGridSpec(
        num_scalar_prefetch=3,
        grid=(T // tm,),
        in_specs=[pl.BlockSpec((tm * TOK_ROWS, LANES), lambda i, *_: (i, 0))],
        out_specs=pl.BlockSpec(memory_space=pl.ANY),
        scratch_shapes=[pltpu.SemaphoreType.DMA(()), pltpu.SemaphoreType.DMA(())],
    )
    return pl.pallas_call(
        functools.partial(_dispatch_kernel, tm=tm, n_tiles=n_tiles),
        grid_spec=grid_spec,
        out_shape=jax.ShapeDtypeStruct((n_tiles * MOE_TM * TOK_ROWS, LANES), F32),
        compiler_params=_cparams(("arbitrary",)),
        name="moe_dispatch",
    )(packed, plan[3], counts, h2_tiles)


def _moe_kernel(ea_ref, eb_ref, fl_ref, h_ref, rwt_ref, wga, wua, wda, wgb, wub, wdb, o_ref,
                sga, sua, sda, sgb, sub, sdb):
    s = pl.program_id(0)
    n_tiles = pl.num_programs(0) - 1
    nxt = fl_ref[jnp.minimum(s, n_tiles - 1)]
    stage = s < n_tiles

    @pl.when(jnp.logical_and(stage, (nxt & 4) != 0))
    def _():
        p = (nxt >> 4) & 1
        sga[p] = wga[...].astype(BF16)
        sua[p] = wua[...].astype(BF16)
        sda[p] = wda[...].astype(BF16)

    @pl.when(jnp.logical_and(stage, (nxt & 8) != 0))
    def _():
        p = (nxt >> 5) & 1
        sgb[p] = wgb[...].astype(BF16)
        sub[p] = wub[...].astype(BF16)
        sdb[p] = wdb[...].astype(BF16)

    t = jnp.maximum(s - 1, 0)
    flags = fl_ref[t]
    run = s > 0
    valid = jnp.logical_and(run, (flags & 1) != 0)
    pa = (flags >> 4) & 1
    pb = (flags >> 5) & 1

    @pl.when(valid)
    def _():
        h = _load_token_tiles(h_ref, MOE_TM).astype(BF16)
        nt = (((1,), (1,)), ((), ()))
        score = jax.nn.sigmoid(lax.dot_general(h, rwt_ref[...], nt, preferred_element_type=F32))
        lane = lax.broadcasted_iota(jnp.int32, score.shape, 1)
        s_a = jnp.sum(jnp.where(lane == ea_ref[t], score, 0.0), axis=1, keepdims=True)
        s_b = jnp.sum(jnp.where(lane == eb_ref[t], score, 0.0), axis=1, keepdims=True)
        tot = s_a + s_b
        act_a = _silu(jnp.dot(h, sga[pa], preferred_element_type=F32)) \
            * jnp.dot(h, sua[pa], preferred_element_type=F32) * (s_a / tot)
        act_b = _silu(jnp.dot(h, sgb[pb], preferred_element_type=F32)) \
            * jnp.dot(h, sub[pb], preferred_element_type=F32) * (s_b / tot)
        _store_token_tiles(o_ref, jnp.dot(act_a.astype(BF16), sda[pa], preferred_element_type=F32)
                           + jnp.dot(act_b.astype(BF16), sdb[pb], preferred_element_type=F32))

    @pl.when(jnp.logical_and(run, jnp.logical_not(valid)))
    def _():
        o_ref[...] = jnp.zeros_like(o_ref)


def _moe(hs, plan, layer, rwt_bf16, w_gate, w_up, w_down):
    ea, eb, flags, _ = plan
    n_tiles = ea.shape[0]
    ahead = lambda s: jnp.minimum(s, n_tiles - 1)
    wspec = lambda shape, which: pl.BlockSpec(
        (None, None) + shape, (lambda s, ea, eb, fl: (layer, ea[ahead(s)], 0, 0)) if which == 0
        else (lambda s, ea, eb, fl: (layer, eb[ahead(s)], 0, 0)))
    up = (D_MODEL, D_FF_EXPERT)
    dn = (D_FF_EXPERT, D_MODEL)
    tile = pl.BlockSpec((MOE_TM * TOK_ROWS, LANES), lambda s, ea, eb, fl: (jnp.maximum(s - 1, 0), 0))
    grid_spec = pltpu.PrefetchScalarGridSpec(
        num_scalar_prefetch=3,
        grid=(n_tiles + 1,),
        in_specs=[
            tile,
            pl.BlockSpec((N_EXPERTS, D_MODEL), lambda s, ea, eb, fl: (0, 0)),
            wspec(up, 0), wspec(up, 0), wspec(dn, 0), wspec(up, 1), wspec(up, 1), wspec(dn, 1),
        ],
        out_specs=tile,
        scratch_shapes=[pltpu.VMEM((2,) + up, BF16), pltpu.VMEM((2,) + up, BF16), pltpu.VMEM((2,) + dn, BF16),
                        pltpu.VMEM((2,) + up, BF16), pltpu.VMEM((2,) + up, BF16), pltpu.VMEM((2,) + dn, BF16)],
    )
    return pl.pallas_call(
        _moe_kernel,
        grid_spec=grid_spec,
        out_shape=jax.ShapeDtypeStruct((n_tiles * MOE_TM * TOK_ROWS, LANES), F32),
        compiler_params=_cparams(("arbitrary",)),
        name="moe",
    )(ea, eb, flags, hs, rwt_bf16, w_gate, w_up, w_down, w_gate, w_up, w_down)


def _gather_expert_rows(packed_ref, rs_ref, ys_ref, ybuf, sem, tm):
    i = pl.program_id(0)
    n_i = pl.num_programs(0)
    slot = i % 2

    def gather(tile, buf_slot):
        def body(r8, _):
            for j in range(8):
                r = r8 * 8 + j
                d = _class_row(packed_ref[0, tile * tm + r], rs_ref)
                pltpu.make_async_copy(ys_ref.at[pl.ds(pl.multiple_of(d * TOK_ROWS, TOK_ROWS), TOK_ROWS)],
                                      ybuf.at[buf_slot, pl.ds(pl.multiple_of(r * TOK_ROWS, TOK_ROWS), TOK_ROWS)],
                                      sem.at[buf_slot]).start(priority=j % 2)
            return 0
        lax.fori_loop(0, tm // 8, body, 0)

    @pl.when(i == 0)
    def _():
        gather(0, 0)

    @pl.when(i + 1 < n_i)
    def _():
        gather(jnp.minimum(i + 1, n_i - 1), 1 - slot)

    pltpu.make_async_copy(ys_ref.at[pl.ds(0, tm * TOK_ROWS)], ybuf.at[slot], sem.at[slot]).wait()
    return _load_token_tiles(ybuf, tm, slot)


def _fin_proj_kernel(packed_ref, rs_ref, x_ref, ys_ref, g_ref, lnw_ref, lnb_ref, sh_ref, sc_ref, w_f32_ref,
                     x2_ref, u_ref, ybuf, sem, w_ref, *, tm):
    _cast_once(w_f32_ref, w_ref)
    y = _gather_expert_rows(packed_ref, rs_ref, ys_ref, ybuf, sem, tm)
    x2 = _layer_norm(ALPHA * x_ref[...] + g_ref[...] * y, lnw_ref[...], lnb_ref[...])
    x2_ref[...] = x2
    h = (x2 * (1.0 + sc_ref[...]) + sh_ref[...]).astype(BF16)
    for s in range(u_ref.shape[1] // SEC):
        u_ref[:, s * SEC:(s + 1) * SEC] = jnp.dot(h, w_ref[:, s * SEC:(s + 1) * SEC],
                                                  preferred_element_type=F32).astype(BF16)


def _fin_proj(x1, ys, packed, row_start, mod, layer, ln_w, ln_b, w_next, cond_row, tm=512):
    T = x1.shape[0]
    N = w_next.shape[1]
    row = pl.BlockSpec((tm, D_MODEL), lambda i, *_: (i, 0))
    vec = pl.BlockSpec((1, D_MODEL), lambda i, *_: (0, 0))
    crow = lambda i: cond_row(i, tm)
    grid_spec = pltpu.PrefetchScalarGridSpec(
        num_scalar_prefetch=2,
        grid=(T // tm,),
        in_specs=[row, pl.BlockSpec(memory_space=pl.ANY), _mod_spec(layer, 5, crow), vec, vec,
                  _mod_spec(layer + 1, 0, crow), _mod_spec(layer + 1, 1, crow),
                  pl.BlockSpec((D_MODEL, N), lambda i, *_: (0, 0), pipeline_mode=pl.Buffered(1))],
        out_specs=[row, pl.BlockSpec((tm, N), lambda i, *_: (i, 0))],
        scratch_shapes=[pltpu.VMEM((2, tm * TOK_ROWS, LANES), F32), pltpu.SemaphoreType.DMA((2,)),
                        pltpu.VMEM((D_MODEL, N), BF16)],
    )
    return pl.pallas_call(
        functools.partial(_fin_proj_kernel, tm=tm),
        grid_spec=grid_spec,
        out_shape=[jax.ShapeDtypeStruct((T, D_MODEL), F32), jax.ShapeDtypeStruct((T, N), BF16)],
        compiler_params=_cparams(("arbitrary",)),
        name="post_moe_in_proj",
    )(packed, row_start, x1, ys, mod, ln_w.reshape(1, -1), ln_b.reshape(1, -1), mod, mod, w_next)


def _fin_kernel(packed_ref, rs_ref, x_ref, ys_ref, g_ref, lnw_ref, lnb_ref, *rest, tm, group_tiles):
    o_refs = rest[:len(group_tiles)]
    ybuf, sem = rest[len(group_tiles):]
    i = pl.program_id(0)
    y = _layer_norm(ALPHA * x_ref[...] + g_ref[...] * _gather_expert_rows(packed_ref, rs_ref, ys_ref, ybuf, sem, tm),
                    lnw_ref[...], lnb_ref[...])
    first = 0
    for o_ref, nt in zip(o_refs, group_tiles):
        @pl.when(jnp.logical_and(i >= first, i < first + nt))
        def _(o_ref=o_ref):
            o_ref[...] = y
        first += nt


def _fin(x1, ys, packed, row_start, mod, layer, ln_w, ln_b, cond_row, group_rows, tm=512):
    group_tiles = tuple(n // tm for n in group_rows)
    row = pl.BlockSpec((tm, D_MODEL), lambda i, *_: (i, 0))
    vec = pl.BlockSpec((1, D_MODEL), lambda i, *_: (0, 0))
    out_specs, first = [], 0
    for nt in group_tiles:
        out_specs.append(_group_spec(D_MODEL, tm, first, nt))
        first += nt
    grid_spec = pltpu.PrefetchScalarGridSpec(
        num_scalar_prefetch=2,
        grid=(sum(group_tiles),),
        in_specs=[row, pl.BlockSpec(memory_space=pl.ANY), _mod_spec(layer, 5, lambda i: cond_row(i, tm)), vec, vec],
        out_specs=out_specs,
        scratch_shapes=[pltpu.VMEM((2, tm * TOK_ROWS, LANES), F32), pltpu.SemaphoreType.DMA((2,))],
    )
    return pl.pallas_call(
        functools.partial(_fin_kernel, tm=tm, group_tiles=group_tiles),
        grid_spec=grid_spec,
        out_shape=[jax.ShapeDtypeStruct((n, D_MODEL), F32) for n in group_rows],
        compiler_params=_cparams(("arbitrary",)),
        name="post_moe",
    )(packed, row_start, x1, ys, mod, ln_w.reshape(1, -1), ln_b.reshape(1, -1))


def _filt_kernel(z_ref, w1_ref, b1_ref, w2_ref, b2_ref, fr_ref, w3_ref, dl_ref, o_ref):
    i = pl.program_id(0)
    z = z_ref[...]
    fr = fr_ref[...]
    a = jnp.sin(fr * (jnp.dot(z, w1_ref[...], precision=HIGHEST, preferred_element_type=F32) + b1_ref[...]))
    a = jnp.sin(fr * (jnp.dot(a, w2_ref[...], precision=HIGHEST, preferred_element_type=F32) + b2_ref[...]))
    filt = _dot_3pass(a, w3_ref[...])
    window = jnp.exp(-z[:, 0:1] * dl_ref[...]) + HY_SHIFT
    rows = i * z.shape[0] + lax.broadcasted_iota(jnp.int32, filt.shape, 0)
    o_ref[...] = jnp.where(rows == 0, 0.0, filt * window)


def _hyena_taps(seq_len, w1, b1, w2, b2, w3, freq):
    L = seq_len
    t = np.linspace(0.0, 1.0, L)[:, None]
    bands = np.linspace(1e-4, HY_BANDS - 1, HY_BANDS)
    ang = 2.0 * math.pi * bands[None, :] * np.arange(L)[:, None] / L
    z = np.concatenate([t, np.cos(ang), -np.sin(ang)], axis=-1)
    offs = np.minimum(np.abs(np.arange(2 * L) - L), L - 1)
    z2_np = np.zeros((2 * L, LANES), np.float32)
    z2_np[:, :z.shape[1]] = z[offs]
    z2 = jnp.asarray(z2_np)
    pad_c = lambda a: jnp.zeros((a.shape[0], LANES), F32).at[:, :a.shape[1]].set(a)
    pad_r = lambda a: jnp.zeros((LANES, a.shape[1]), F32).at[:a.shape[0], :].set(a)
    w1p = pad_r(pad_c(w1))
    w2p = pad_r(pad_c(w2))
    w3p = pad_r(w3)
    b1p, b2p, frp = pad_c(b1[None, :]), pad_c(b2[None, :]), pad_c(freq[None, :])
    deltas = jnp.asarray(np.abs(np.linspace(math.log(HY_DECAY_TARGET) / HY_SLOW, math.log(HY_DECAY_TARGET) / HY_FAST,
                                            D_MODEL))[None, :].astype(np.float32))
    rb = min(1024, L)
    cbf = D_MODEL
    ncb = D_MODEL // cbf
    nrb_back = L // rb
    sq = pl.BlockSpec((LANES, LANES), lambda i, j: (0, 0))
    vec = pl.BlockSpec((1, LANES), lambda i, j: (0, 0))
    return pl.pallas_call(
        _filt_kernel,
        grid=(2 * L // rb, ncb),
        in_specs=[
            pl.BlockSpec((rb, LANES), lambda i, j: (i, 0)),
            sq, vec, sq, vec, vec,
            pl.BlockSpec((LANES, cbf), lambda i, j: (0, jnp.where(i < nrb_back, ncb + j, j))),
            pl.BlockSpec((1, cbf), lambda i, j: (0, j)),
        ],
        out_specs=pl.BlockSpec((rb, cbf), lambda i, j: (i, j)),
        out_shape=jax.ShapeDtypeStruct((2 * L, D_MODEL), F32),
        compiler_params=_cparams(("arbitrary", "arbitrary")),
        name="hyena_taps",
    )(z2, w1p, b1p, w2p, b2p, frp, w3p, deltas)


def _dft_mats(cb):
    n = 2 * cb
    m = np.arange(cb)
    f = np.arange(cb)
    ang = 2.0 * np.pi * ((f[:, None] * m[None, :]) % n) / n
    fwd = np.concatenate([np.cos(ang), -np.sin(ang)], axis=0)
    fwd[cb, :] = np.where(m % 2 == 0, 1.0, -1.0)
    coef = np.where(f == 0, 1.0, 2.0)[None, :] / n
    inv = np.concatenate([coef * np.cos(ang.T), -coef * np.sin(ang.T)], axis=1)
    inv[:, cb] = np.where(m % 2 == 0, 1.0, -1.0) / n
    return fwd.astype(np.float32), inv.astype(np.float32)


def _hconv_kernel(x0_ref, x1_ref, v_ref, cw0_ref, cw1_ref, cw2_ref, cb0_ref, cb1_ref, cb2_ref,
                  taps_ref, fb_ref, fwd_ref, inv_ref, o_ref,
                  hs_ref, stage_ref, w32_ref, w_ref, x0c_ref, u_ref, y_ref, *, seq_len, blk, n_seq):
    L = seq_len
    nb = L // blk
    cw = x0_ref.shape[1]
    RC = 256
    bi = pl.program_id(1)
    row0 = lax.broadcasted_iota(jnp.int32, (RC, cw), 0) == 0

    @pl.when(bi == 0)
    def _():
        rows = lax.broadcasted_iota(jnp.int32, (2 * blk, 1), 0)
        sign = jnp.where(rows % 2 == 0, 1.0, -1.0)
        real_row = rows <= blk
        prev = None
        for jb in range(2 * nb):
            cur = jnp.dot(fwd_ref[...], taps_ref[jb * blk:(jb + 1) * blk, :].astype(BF16),
                          preferred_element_type=F32)
            if prev is not None:
                first_tap = taps_ref[(jb - 1) * blk:(jb - 1) * blk + 1, :].astype(BF16).astype(F32)
                hs_ref[jb - 1] = cur + sign * (prev - jnp.where(real_row, first_tap, 0.0))
            prev = cur

    zeros8 = jnp.zeros((8, cw), F32)
    for s in range(n_seq):
        r0 = s * L

        def short_conv(k, src_ref, cw_ref, cb_ref, store):
            stage_ref[k, 0:8, :] = zeros8
            stage_ref[k, 8 + L:16 + L, :] = zeros8
            for c in range(L // RC):
                stage_ref[k, 8 + c * RC:8 + (c + 1) * RC, :] = \
                    src_ref[r0 + c * RC:r0 + (c + 1) * RC, :].astype(F32)
            w = cw_ref[...]
            for c in range(L // RC):
                lo = stage_ref[k, 7 + c * RC:7 + (c + 1) * RC, :]
                mid = stage_ref[k, 8 + c * RC:8 + (c + 1) * RC, :]
                hi = stage_ref[k, 9 + c * RC:9 + (c + 1) * RC, :]
                store(c, lo * w[0:1, :] + mid * w[1:2, :] + hi * w[2:3, :] + cb_ref[...])

        def st_x1(c, val):
            w32_ref[c * RC:(c + 1) * RC, :] = val

        def st_v(c, val):
            w_ref[c * RC:(c + 1) * RC, :] = (w32_ref[c * RC:(c + 1) * RC, :] * val).astype(BF16)

        def st_x0(c, val):
            x0c_ref[c * RC:(c + 1) * RC, :] = val.astype(BF16)

        short_conv(0, x1_ref, cw1_ref, cb1_ref, st_x1)
        short_conv(1, v_ref, cw2_ref, cb2_ref, st_v)
        short_conv(2, x0_ref, cw0_ref, cb0_ref, st_x0)

        for j in range(nb):
            u_ref[j] = jnp.dot(fwd_ref[...], w_ref[j * blk:(j + 1) * blk, :], preferred_element_type=F32)

        for i in range(nb):
            for c in range(blk // RC):
                re = None
                im = None
                for j in range(nb):
                    k = i - j + nb - 1
                    a = u_ref[j, c * RC:(c + 1) * RC, :]
                    b = u_ref[j, blk + c * RC:blk + (c + 1) * RC, :]
                    hr = hs_ref[k, c * RC:(c + 1) * RC, :]
                    hi = hs_ref[k, blk + c * RC:blk + (c + 1) * RC, :]
                    bb = b * hi
                    if c == 0:
                        t_re = a * hr - jnp.where(row0, 0.0, bb)
                        t_im = jnp.where(row0, bb, a * hi + b * hr)
                    else:
                        t_re = a * hr - bb
                        t_im = a * hi + b * hr
                    re = t_re if re is None else re + t_re
                    im = t_im if im is None else im + t_im
                y_ref[i, c * RC:(c + 1) * RC, :] = re.astype(BF16)
                y_ref[i, blk + c * RC:blk + (c + 1) * RC, :] = im.astype(BF16)
            conv = jnp.dot(inv_ref[...], y_ref[i], preferred_element_type=F32)
            sl = slice(i * blk, (i + 1) * blk)
            z = x0c_ref[sl, :].astype(F32) * (conv + w_ref[sl, :].astype(F32) * fb_ref[...])
            o_ref[r0 + i * blk:r0 + (i + 1) * blk, :] = z.astype(BF16)


def _hyena_conv(u, conv_w, conv_b, taps, filt_bias, batch, seq_len, blk, n_seq, row_off=0, cw=256):
    T = batch * seq_len
    L = seq_len
    nb = L // blk
    ncw = D_MODEL // cw
    fwd_np, inv_np = _dft_mats(blk)
    fwd = jnp.asarray(fwd_np).astype(BF16)
    inv = jnp.asarray(inv_np).astype(BF16)
    rows = n_seq * L
    assert row_off % rows == 0
    boff = row_off // rows
    sec = lambda s: pl.BlockSpec((rows, cw), lambda c, b: (b + boff, s * ncw + c))
    cws = lambda s: pl.BlockSpec((3, cw), lambda c, b: (0, s * ncw + c))
    cbs = lambda s: pl.BlockSpec((1, cw), lambda c, b: (0, s * ncw + c))
    return pl.pallas_call(
        functools.partial(_hconv_kernel, seq_len=L, blk=blk, n_seq=n_seq),
        grid=(ncw, batch // n_seq),
        in_specs=[sec(0), sec(1), sec(2), cws(0), cws(1), cws(2), cbs(0), cbs(1), cbs(2),
                  pl.BlockSpec((2 * L, cw), lambda c, b: (0, c), pipeline_mode=pl.Buffered(1)),
                  pl.BlockSpec((1, cw), lambda c, b: (0, c)),
                  pl.BlockSpec((2 * blk, blk), lambda c, b: (0, 0), pipeline_mode=pl.Buffered(1)),
                  pl.BlockSpec((blk, 2 * blk), lambda c, b: (0, 0), pipeline_mode=pl.Buffered(1))],
        out_specs=pl.BlockSpec((rows, cw), lambda c, b: (b, c)),
        out_shape=jax.ShapeDtypeStruct((T, D_MODEL), BF16),
        scratch_shapes=[
            pltpu.VMEM((2 * nb - 1, 2 * blk, cw), F32),
            pltpu.VMEM((3, L + 16, cw), F32),
            pltpu.VMEM((L, cw), F32),
            pltpu.VMEM((L, cw), BF16),
            pltpu.VMEM((L, cw), BF16),
            pltpu.VMEM((nb, 2 * blk, cw), F32),
            pltpu.VMEM((nb, 2 * blk, cw), BF16),
        ],
        compiler_params=_cparams(("arbitrary", "arbitrary")),
        name="hyena_conv",
    )(u, u, u, conv_w, conv_w, conv_w, conv_b.reshape(1, -1), conv_b.reshape(1, -1), conv_b.reshape(1, -1),
      taps, filt_bias.reshape(1, -1), fwd, inv)


def kernel(x_prompt, x_sample, cache_diff_k, cache_diff_v, state_ret_fwd, state_ret_bwd, c, c_ctx, ada_w, ada_b, ln_w, ln_b, ev_w_in, ev_w_out, ret_decay_fwd, ret_decay_bwd, ret_gn_w, diff_lambda, diff_subln_w, hy_w_in, hy_conv_w, hy_conv_b, hy_ffn_w1, hy_ffn_b1, hy_ffn_w2, hy_ffn_b2, hy_ffn_w3, hy_freq, hy_filter_bias, hy_w_out, router_w, router_bias, moe_w_gate, moe_w_up, moe_w_down):
    B, S, D = x_prompt.shape
    DB, DS, _ = x_sample.shape
    PAST = cache_diff_k.shape[2]
    TP, TS = B * S, DB * DS
    T_ALL = TP + TS
    assert D == D_MODEL and 1 + DB <= COND_ROWS and T_ALL < 65536

    cond8 = jnp.zeros((COND_ROWS, D), F32).at[0].set(c_ctx).at[1:1 + DB].set(c)
    mod = _ada_mod(cond8, ada_w, ada_b).reshape(DEPTH * COND_ROWS * N_MOD, 1, D)

    tm = 512
    groups = [
        dict(x=x_prompt.reshape(TP, D), x_off=0, batch=B, seq=S, off=0, cond_row=lambda i, tm: 0),
        dict(x=x_sample.reshape(TS, D), x_off=0, batch=DB, seq=DS, off=TP,
             cond_row=lambda i, tm: 1 + (i * tm) // DS),
    ]
    u_next = None

    def cond_row_all(i, tm):
        return jnp.where(i * tm < TP, 0, 1 + (i * tm - TP) // DS)

    rw_pad = jnp.zeros((D, LANES), F32).at[:, :N_EXPERTS].set(router_w.astype(F32))
    rw_hi = rw_pad.astype(BF16)
    rw_lo = jnp.concatenate([rw_hi, (rw_pad - rw_hi.astype(F32)).astype(BF16)], axis=1)
    rwt = router_w.T.astype(BF16)
    rbias = router_bias.astype(F32)
    n_tiles = T_ALL // MOE_TM + N_CLASSES
    outs = {}

    for l in range(DEPTH):
        mixes = []
        if l % 2 == 0:
            e = l // 2
            w_in = ev_w_in[e]
            w_out = ev_w_out[e]
            lg = jnp.stack([jnp.log1p(-jnp.exp2(ret_decay_fwd[e].astype(F32))),
                            jnp.log1p(-jnp.exp2(ret_decay_bwd[e].astype(F32)))])
            lam_init = 0.8 - 0.6 * math.exp(-0.3 * l)
            lq1, lk1, lq2, lk2 = diff_lambda[e].astype(F32)
            lam = (jnp.exp(jnp.sum(lq1 * lk1)) - jnp.exp(jnp.sum(lq2 * lk2)) + lam_init).reshape(1)
            kscale = RET_DK ** -0.5
            for gi, g in enumerate(groups):
                tm_in = tm
                crow = functools.partial(g["cond_row"], tm=tm_in)
                if gi == 0:
                    secs = (("none", 1.0, False), ("none", kscale, False), ("none", 1.0, False),
                            ("none", 1.0, False), ("none", 1.0, False), ("none", 1.0, True),
                            ("none", 1.0, True))
                    proj, kd, vd = _in_proj(g["x"], mod, l, w_in, secs, crow, tm_in, tm=tm_in)
                    outs.setdefault("kd", []).append(kd.reshape(B, 1, S, H_DIFF, 2 * DIFF_QK))
                    outs.setdefault("vd", []).append(vd.reshape(B, 1, S, H_DIFF, DIFF_V))
                    ret, sf, sb = _retention(proj, lg, ret_gn_w[e], g["batch"], g["seq"], write_state=True,
                                             heads=H_RET)
                    outs.setdefault("sf", []).append(sf.reshape(B, 1, H_RET, RET_DK, RET_DV))
                    outs.setdefault("sb", []).append(sb.reshape(B, 1, H_RET, RET_DK, RET_DV))
                    att = _diff_attention(proj, lam, diff_subln_w[e], lam_init, g["batch"], g["seq"],
                                          tq=min(256, g["seq"]), heads=H_DIFF)
                else:
                    secs = (("ret", 1.0, False), ("ret", kscale, False), ("none", 1.0, False),
                            ("none", 1.0, False), ("diff", 1.0, False), ("diff", 1.0, False),
                            ("none", 1.0, False))
                    tabs = {"ret": _rope_tables(g["seq"], RET_DK), "diff": _rope_tables(g["seq"], DIFF_QK)}
                    (proj,) = _in_proj(g["x"], mod, l, w_in, secs, crow, g["seq"], rope_tabs=tabs, tm=tm_in)
                    ret, = _retention(proj, lg, ret_gn_w[e], g["batch"], g["seq"],
                                      states=(state_ret_fwd[:, e], state_ret_bwd[:, e]))
                    ctx = (cache_diff_k[:, e].reshape(DB, PAST, H_DIFF * 2 * DIFF_QK),
                           cache_diff_v[:, e].reshape(DB, PAST, H_DIFF * DIFF_V))
                    att = _diff_attention(proj, lam, diff_subln_w[e], lam_init, g["batch"], g["seq"], ctx=ctx,
                                          tq=1024, heads=1)
                mixes.append((ret, att))
        else:
            o = l // 2
            w_in = hy_w_in[o]
            w_out = hy_w_out[o]
            secs = (("none", 1.0, False),) * (3 * D // SEC)
            for gi, g in enumerate(groups):
                if u_next is None:
                    crow = functools.partial(g["cond_row"], tm=tm)
                    (u,) = _in_proj(g["x"], mod, l, w_in, secs, crow, tm, tm=tm)
                    u_off = 0
                else:
                    u, u_off = u_next, g["off"]
                taps = _hyena_taps(g["seq"], hy_ffn_w1[o], hy_ffn_b1[o], hy_ffn_w2[o], hy_ffn_b2[o],
                                   hy_ffn_w3[o], hy_freq[o])
                blk = min(g["seq"], 512)
                n_seq = max(1, 1024 // g["seq"])
                z = _hyena_conv(u, hy_conv_w[o], hy_conv_b[o], taps, hy_filter_bias[o], g["batch"], g["seq"],
                                blk, n_seq, row_off=u_off)
                mixes.append((z,))

        x1_all, h2_all, info_all, counts = _post_mix(mixes, [(g["x"], g["x_off"]) for g in groups], w_out, mod, l, ln_w[l, 0],
                                                     ln_b[l, 0], rw_hi, rw_lo, rbias, cond_row_all, tm)
        packed = info_all
        counts_i = counts[:, 0].astype(jnp.int32)
        plan = _moe_plan(counts_i, n_tiles)
        hs = _dispatch(h2_all, packed, plan, counts_i, n_tiles)
        ys = _moe(hs, plan, l, rwt, moe_w_gate, moe_w_up, moe_w_down)
        if l + 1 < DEPTH and (l + 1) % 2 == 1:
            x_all, u_next = _fin_proj(x1_all, ys, packed, plan[3], mod, l, ln_w[l, 1], ln_b[l, 1],
                                      hy_w_in[(l + 1) // 2], cond_row_all, tm=tm)
            for g in groups:
                g["x"], g["x_off"] = x_all, g["off"]
        else:
            u_next = None
            xs = _fin(x1_all, ys, packed, plan[3], mod, l, ln_w[l, 1], ln_b[l, 1], cond_row_all, (TP, TS))
            for g, x in zip(groups, xs):
                g["x"], g["x_off"] = x, 0

    y_prompt = groups[0]["x"].reshape(B, S, D)
    y_sample = groups[1]["x"].reshape(DB, DS, D)
    cat = lambda xs: xs[0] if len(xs) == 1 else jnp.concatenate(xs, axis=1)
    return (y_prompt, y_sample, cat(outs["kd"]), cat(outs["vd"]), cat(outs["sf"]), cat(outs["sb"]))
```

```python
import functools
import math

import numpy as np
import jax
import jax.numpy as jnp
from jax import lax
from jax.experimental import pallas as pl
from jax.experimental.pallas import tpu as pltpu

F32 = jnp.float32
BF16 = jnp.bfloat16
HIGHEST = lax.Precision.HIGHEST

D_MODEL = 1024
DEPTH = 2
GRID_W = 64
H_RET = 4
RET_DK = 128
RET_DV = 128
RET_CHUNK = 256
H_DIFF = 4
DIFF_QK = 64
DIFF_V = 128
ROPE_BASE = 10000.0
HY_BANDS = 16
HY_FH = 64
HY_DECAY_TARGET = 1e-2
HY_FAST = 0.3
HY_SLOW = 1.5
HY_SHIFT = 0.05
N_EXPERTS = 16
N_GROUPS = 4
GROUP_SIZE = N_EXPERTS // N_GROUPS
D_FF_EXPERT = 512
ALPHA = (2 * DEPTH) ** 0.25
LN_EPS = 1e-5

LANES = 128
SEC = 512
COND_ROWS = 8
N_MOD = 6
VMEM_LIMIT = 50 * 1024 * 1024

PAIR_SLOTS = ((0, 1), (2, 1), (2, 3), (0, 3), (0, 2), (1, 3))
N_PAIRS = len(PAIR_SLOTS)
N_CLASSES = N_GROUPS * N_PAIRS
MOE_TM = 256
MOE_SHIFT = MOE_TM.bit_length() - 1
POST_SUB = 512
TOK_ROWS = D_MODEL // LANES


def _cparams(sem):
    return pltpu.CompilerParams(dimension_semantics=sem, vmem_limit_bytes=VMEM_LIMIT)


def _silu(x):
    return x * jax.nn.sigmoid(x)


def _store_token_tiles(ref, x):
    n = x.shape[0]
    for j in range(TOK_ROWS):
        ref[pl.ds(j, n, stride=TOK_ROWS), :] = x[:, j * LANES:(j + 1) * LANES]


def _load_token_tiles(ref, n, slot=None):
    idx = () if slot is None else (slot,)
    return jnp.concatenate([ref[idx + (pl.ds(j, n, stride=TOK_ROWS), slice(None))] for j in range(TOK_ROWS)],
                           axis=1)


def _layer_norm(x, w, b):
    mu = jnp.mean(x, axis=-1, keepdims=True)
    xc = x - mu
    var = jnp.mean(xc * xc, axis=-1, keepdims=True)
    return xc * lax.rsqrt(var + LN_EPS) * w + b


def _dot_3pass(a, w):
    a_hi = a.astype(BF16)
    a_lo = (a - a_hi.astype(F32)).astype(BF16)
    w_hi = w.astype(BF16)
    w_lo = (w - w_hi.astype(F32)).astype(BF16)
    return (jnp.dot(a_hi, w_hi, preferred_element_type=F32) + jnp.dot(a_lo, w_hi, preferred_element_type=F32)
            + jnp.dot(a_hi, w_lo, preferred_element_type=F32))


def _ada_kernel(c_ref, w_ref, b_ref, o_ref):
    o_ref[...] = _dot_3pass(_silu(c_ref[...]), w_ref[...]) + b_ref[...]


def _ada_mod(cond8, ada_w, ada_b):
    tn = 1024
    nj = ada_w.shape[2] // tn
    return pl.pallas_call(
        _ada_kernel,
        grid=(DEPTH, nj),
        in_specs=[
            pl.BlockSpec((COND_ROWS, D_MODEL), lambda l, j: (0, 0)),
            pl.BlockSpec((None, D_MODEL, tn), lambda l, j: (l, 0, j)),
            pl.BlockSpec((None, 1, tn), lambda l, j: (l, 0, j)),
        ],
        out_specs=pl.BlockSpec((None, COND_ROWS, tn), lambda l, j: (l, 0, j)),
        out_shape=jax.ShapeDtypeStruct((DEPTH, COND_ROWS, ada_w.shape[2]), F32),
        compiler_params=_cparams(("arbitrary", "arbitrary")),
        name="ada_mod",
    )(cond8, ada_w, ada_b.reshape(DEPTH, 1, -1))


def _mod_spec(layer, chunk, row_of_tile):
    def imap(i, *_):
        return ((layer * COND_ROWS + row_of_tile(i)) * N_MOD + chunk, 0, 0)
    return pl.BlockSpec((None, 1, D_MODEL), imap)


def _rope(a, tabs, quarter):
    c, sa, sb = tabs
    out = []
    for hb in range(a.shape[1] // LANES):
        blk = a[:, hb * LANES:(hb + 1) * LANES]
        up = pltpu.roll(blk, LANES - quarter, axis=1)
        dn = pltpu.roll(blk, quarter, axis=1)
        out.append(blk * c + up * sa + dn * sb)
    return jnp.concatenate(out, axis=1)


def _cast_once(w_ref, w_bf16_ref):
    @pl.when(pl.program_id(0) == 0)
    def _():
        for c in range(0, w_ref.shape[1], SEC):
            w_bf16_ref[:, c:c + SEC] = w_ref[:, c:c + SEC].astype(BF16)


def _in_kernel(*refs, secs, n_f32_out):
    x_ref, sh_ref, sc_ref, w_ref = refs[:4]
    pos = 4
    tabs = {}
    for kind in ("ret", "diff"):
        if any(s[0] == kind for s in secs):
            tabs[kind] = tuple(r[...] for r in refs[pos:pos + 3])
            pos += 3
    o_ref = refs[pos]
    f32_refs = refs[pos + 1:pos + 1 + n_f32_out]
    w_bf16_ref = refs[-1]
    _cast_once(w_ref, w_bf16_ref)
    h = (x_ref[...] * (1.0 + sc_ref[...]) + sh_ref[...]).astype(BF16)
    k32 = 0
    for s, (kind, scale, want_f32) in enumerate(secs):
        acc = jnp.dot(h, w_bf16_ref[:, s * SEC:(s + 1) * SEC], preferred_element_type=F32)
        if scale != 1.0:
            acc = acc * scale
        if kind == "ret":
            acc = _rope(acc, tabs["ret"], RET_DK // 4)
        elif kind == "diff":
            acc = _rope(acc, tabs["diff"], DIFF_QK // 4)
        o_ref[:, s * SEC:(s + 1) * SEC] = acc.astype(BF16)
        if want_f32:
            for hb in range(SEC // LANES):
                f32_refs[k32][:, hb, :] = acc[:, hb * LANES:(hb + 1) * LANES]
            k32 += 1
    assert k32 == n_f32_out


def _in_proj(x2d, mod, layer, w_f32, secs, row_of_tile, seq_len, rope_tabs=None, tm=512):
    T = x2d.shape[0]
    N = w_f32.shape[1]
    assert N == SEC * len(secs) and T % tm == 0 and seq_len % tm == 0
    tiles_per_seq = seq_len // tm
    in_specs = [
        pl.BlockSpec((tm, D_MODEL), lambda i: (i, 0)),
        _mod_spec(layer, 0, row_of_tile),
        _mod_spec(layer, 1, row_of_tile),
        pl.BlockSpec((D_MODEL, N), lambda i: (0, 0), pipeline_mode=pl.Buffered(1)),
    ]
    args = [x2d, mod, mod, w_f32]
    for kind in ("ret", "diff"):
        if any(s[0] == kind for s in secs):
            for t in rope_tabs[kind]:
                in_specs.append(pl.BlockSpec((tm, LANES), lambda i: (i % tiles_per_seq, 0)))
                args.append(t)
    n_f32 = sum(1 for s in secs if s[2])
    heads = SEC // LANES
    out_shape = [jax.ShapeDtypeStruct((T, N), BF16)] + [jax.ShapeDtypeStruct((T, heads, LANES), F32)] * n_f32
    out_specs = ([pl.BlockSpec((tm, N), lambda i: (i, 0))]
                 + [pl.BlockSpec((tm, heads, LANES), lambda i: (i, 0, 0))] * n_f32)
    return pl.pallas_call(
        functools.partial(_in_kernel, secs=secs, n_f32_out=n_f32),
        grid=(T // tm,),
        in_specs=in_specs,
        out_specs=out_specs,
        out_shape=out_shape,
        scratch_shapes=[pltpu.VMEM((D_MODEL, N), BF16)],
        compiler_params=_cparams(("arbitrary",)),
        name="in_proj",
    )(*args)


def _rope_tables(seq_len, d):
    half = d // 2
    quarter = half // 2
    t = np.arange(seq_len)
    inv = ROPE_BASE ** (-np.arange(quarter, dtype=np.float64) / quarter)
    ang_r = (t // GRID_W)[:, None] * inv[None, :]
    ang_c = (t % GRID_W)[:, None] * inv[None, :]
    zero = np.zeros_like(ang_r)
    cos = np.concatenate([np.cos(ang_r)] * 2 + [np.cos(ang_c)] * 2, axis=1)
    sa = np.concatenate([-np.sin(ang_r), zero, -np.sin(ang_c), zero], axis=1)
    sb = np.concatenate([zero, np.sin(ang_r), zero, np.sin(ang_c)], axis=1)
    reps = LANES // d
    return tuple(jnp.asarray(np.tile(a, (1, reps)).astype(np.float32)) for a in (cos, sa, sb))


def _ret_kernel(*refs, n_chunks, has_state, write_state, heads):
    lg_ref, q_ref, k_ref, v_ref, g_ref, gnw_ref = refs[:6]
    pos = 6
    if has_state:
        s0f_ref, s0b_ref = refs[pos:pos + 2]
        pos += 2
    o_ref = refs[pos]
    pos += 1
    if write_state:
        sf_ref, sb_ref = refs[pos:pos + 2]
        pos += 2
    acc_ref, kv_ref = refs[pos:pos + 2]
    for hh in range(heads):
        _ret_head(hh, heads, lg_ref, q_ref, k_ref, v_ref, g_ref, gnw_ref,
                  (s0f_ref, s0b_ref) if has_state else None, o_ref, (sf_ref, sb_ref) if write_state else None,
                  acc_ref, kv_ref, n_chunks)


def _ret_head(hh, heads, lg_ref, q_ref, k_ref, v_ref, g_ref, gnw_ref, s0_refs, o_ref, s_out_refs, acc_ref, kv_ref,
              n_chunks):
    C = q_ref.shape[0] // n_chunks
    cols = slice(hh * LANES, (hh + 1) * LANES)
    hd = pl.program_id(1) * heads + hh
    lgf = lg_ref[0, hd]
    lgb = lg_ref[1, hd]
    ii = lax.broadcasted_iota(jnp.int32, (C, C), 0).astype(F32)
    jj = lax.broadcasted_iota(jnp.int32, (C, C), 1).astype(F32)
    rel = ii - jj
    d_f = jnp.where(rel >= 0, jnp.exp(jnp.maximum(rel, 0.0) * lgf), 0.0)
    d_b = jnp.where(rel <= 0, jnp.exp(jnp.maximum(-rel, 0.0) * lgb), 0.0)
    d_sum = d_f + d_b
    idx = lax.broadcasted_iota(jnp.int32, (C, 1), 0).astype(F32)
    xi_f = jnp.exp((idx + 1.0) * lgf)
    zeta_f = jnp.exp((C - 1.0 - idx) * lgf)
    xi_b = jnp.exp((C - idx) * lgb)
    zeta_b = jnp.exp(idx * lgb)
    one = jnp.ones((1, 1), F32)
    gc_f = jnp.exp(one * (C * lgf))
    gc_b = jnp.exp(one * (C * lgb))

    nt = (((1,), (1,)), ((), ()))
    tn = (((0,), (0,)), ((), ()))

    if s0_refs is not None:
        s_f = s0_refs[0][hh]
        s_b = s0_refs[1][hh]
    else:
        s_f = jnp.zeros((RET_DK, RET_DV), F32)
        s_b = jnp.zeros((RET_DK, RET_DV), F32)

    for n in range(n_chunks):
        sl = slice(n * C, (n + 1) * C)
        qc, kc, vc = q_ref[sl, cols], k_ref[sl, cols], v_ref[sl, cols]
        scores = lax.dot_general(qc, kc, nt, preferred_element_type=F32) * d_sum
        acc_ref[sl, cols] = jnp.dot(scores.astype(BF16), vc, preferred_element_type=F32)
        kf = kc.astype(F32)
        kv_ref[0, hh, n] = lax.dot_general((kf * zeta_f).astype(BF16), vc, tn, preferred_element_type=F32)
        kv_ref[1, hh, n] = lax.dot_general((kf * zeta_b).astype(BF16), vc, tn, preferred_element_type=F32)

    for n in range(n_chunks):
        sl = slice(n * C, (n + 1) * C)
        qf = q_ref[sl, cols].astype(F32)
        acc_ref[sl, cols] += jnp.dot((qf * xi_f).astype(BF16), s_f.astype(BF16), preferred_element_type=F32)
        s_f = gc_f * s_f + kv_ref[0, hh, n]

    gnw = gnw_ref[:, cols]
    for n in reversed(range(n_chunks)):
        sl = slice(n * C, (n + 1) * C)
        qf = q_ref[sl, cols].astype(F32)
        cross = jnp.dot((qf * xi_b).astype(BF16), s_b.astype(BF16), preferred_element_type=F32)
        r = acc_ref[sl, cols] + cross
        mu = jnp.mean(r, axis=-1, keepdims=True)
        rc = r - mu
        var = jnp.mean(rc * rc, axis=-1, keepdims=True)
        rn = rc * lax.rsqrt(var + LN_EPS) * gnw
        o_ref[sl, cols] = (_silu(g_ref[sl, cols].astype(F32)) * rn).astype(BF16)
        s_b = gc_b * s_b + kv_ref[1, hh, n]

    if s_out_refs is not None:
        s_out_refs[0][hh] = s_f
        s_out_refs[1][hh] = s_b


def _retention(proj, lg, gn_w, batch, seq_len, states=None, write_state=False, heads=2):
    T = proj.shape[0]
    hp = heads
    width = hp * LANES
    per_sec = SEC // width
    blk = lambda sec: pl.BlockSpec((seq_len, width), lambda b, h: (b, sec * per_sec + h))
    in_specs = [pl.BlockSpec(memory_space=pltpu.SMEM), blk(0), blk(1), blk(2), blk(3),
                pl.BlockSpec((1, width), lambda b, h: (0, h))]
    args = [lg, proj, proj, proj, proj, gn_w.reshape(1, -1)]
    if states is not None:
        st = pl.BlockSpec((None, hp, RET_DK, RET_DV), lambda b, h: (b, h, 0, 0))
        in_specs += [st, st]
        args += list(states)
    n_chunks = max(1, seq_len // RET_CHUNK)
    out_shape = [jax.ShapeDtypeStruct((T, SEC), BF16)]
    out_specs = [pl.BlockSpec((seq_len, width), lambda b, h: (b, h))]
    if write_state:
        st_o = pl.BlockSpec((None, hp, RET_DK, RET_DV), lambda b, h: (b, h, 0, 0))
        out_shape += [jax.ShapeDtypeStruct((batch, H_RET, RET_DK, RET_DV), F32)] * 2
        out_specs += [st_o, st_o]
    return pl.pallas_call(
        functools.partial(_ret_kernel, n_chunks=n_chunks, has_state=states is not None,
                          write_state=write_state, heads=hp),
        grid=(batch, H_RET // hp),
        in_specs=in_specs,
        out_specs=out_specs,
        out_shape=out_shape,
        scratch_shapes=[pltpu.VMEM((seq_len, width), F32),
                        pltpu.VMEM((2, hp, n_chunks, RET_DK, RET_DV), F32)],
        compiler_params=_cparams(("arbitrary", "arbitrary")),
        name="retention",
    )(*args)


def _att_kernel(*refs, has_ctx, out_scale, key_chunk, heads):
    lam_ref, q_ref, k_ref, v_ref = refs[:4]
    pos = 4
    if has_ctx:
        ck_ref, cv_ref = refs[pos:pos + 2]
        pos += 2
    w_ref, o_ref = refs[pos:pos + 2]
    lam = lam_ref[0]
    tq = q_ref.shape[0]
    nt = (((1,), (1,)), ((), ()))
    for hh in range(heads):
        cols = slice(hh * LANES, (hh + 1) * LANES)
        q = q_ref[:, cols]
        lane = lax.broadcasted_iota(jnp.int32, q.shape, 1)
        zero = jnp.zeros_like(q)
        qq = jnp.concatenate([jnp.where(lane < DIFF_QK, q, zero), jnp.where(lane >= DIFF_QK, q, zero)], axis=0)
        qq = qq * jnp.asarray(DIFF_QK ** -0.5, BF16)
        chunks = [(k_ref, v_ref, c * key_chunk, key_chunk) for c in range(k_ref.shape[0] // key_chunk)]
        if has_ctx:
            chunks.append((ck_ref, cv_ref, 0, ck_ref.shape[0]))
        m = l = acc = None
        for kr, vr, off, n in chunks:
            kch = kr[off:off + n, cols].astype(BF16)
            vch = vr[off:off + n, cols].astype(BF16)
            s = lax.dot_general(qq, kch, nt, preferred_element_type=F32)
            cm = jnp.max(s, axis=-1, keepdims=True)
            m_new = cm if m is None else jnp.maximum(m, cm)
            p = jnp.exp(s - m_new)
            ps = jnp.sum(p, axis=-1, keepdims=True)
            pv = jnp.dot(p.astype(BF16), vch, preferred_element_type=F32)
            if m is None:
                l, acc = ps, pv
            else:
                alpha = jnp.exp(m - m_new)
                l = alpha * l + ps
                acc = alpha * acc + pv
            m = m_new
        o = acc / l
        att = o[:tq] - lam * o[tq:]
        att = att * lax.rsqrt(jnp.mean(att * att, axis=-1, keepdims=True) + LN_EPS)
        o_ref[:, cols] = (att * w_ref[...] * out_scale).astype(BF16)


def _diff_attention(proj, lam, subln_w, lam_init, batch, seq_len, ctx=None, tq=256, heads=1):
    T = proj.shape[0]
    width = heads * LANES
    per_sec = SEC // width
    nq = seq_len // tq
    in_specs = [
        pl.BlockSpec(memory_space=pltpu.SMEM),
        pl.BlockSpec((tq, width), lambda b, h, i: (b * nq + i, 4 * per_sec + h)),
        pl.BlockSpec((seq_len, width), lambda b, h, i: (b, 5 * per_sec + h)),
        pl.BlockSpec((seq_len, width), lambda b, h, i: (b, 6 * per_sec + h)),
    ]
    args = [lam, proj, proj, proj]
    if ctx is not None:
        ck, cv = ctx
        past = ck.shape[1]
        cspec = pl.BlockSpec((None, past, width), lambda b, h, i: (b, 0, h))
        in_specs += [cspec, cspec]
        args += [ck, cv]
    in_specs.append(pl.BlockSpec((1, LANES), lambda b, h, i: (0, 0)))
    args.append(subln_w.reshape(1, -1))
    return pl.pallas_call(
        functools.partial(_att_kernel, has_ctx=ctx is not None, out_scale=1.0 - lam_init,
                          key_chunk=min(512, seq_len), heads=heads),
        grid=(batch, H_DIFF // heads, nq),
        in_specs=in_specs,
        out_specs=pl.BlockSpec((tq, width), lambda b, h, i: (b * nq + i, h)),
        out_shape=jax.ShapeDtypeStruct((T, SEC), BF16),
        compiler_params=_cparams(("arbitrary", "arbitrary", "arbitrary")),
        name="diff_attention",
    )(*args)


def _route_class(lt, rb_ref):
    sel = [jax.nn.sigmoid(lt[e:e + 1, :]) + rb_ref[e] for e in range(N_EXPERTS)]
    gscore = []
    for g in range(N_GROUPS):
        mem = sel[g * GROUP_SIZE:(g + 1) * GROUP_SIZE]
        best = None
        for a in range(GROUP_SIZE):
            for b in range(a + 1, GROUP_SIZE):
                pair = mem[a] + mem[b]
                best = pair if best is None else jnp.maximum(best, pair)
        gscore.append(best)
    gbest = gscore[0]
    gidx = jnp.zeros_like(gbest)
    for g in range(1, N_GROUPS):
        upd = gscore[g] > gbest
        gidx = jnp.where(upd, float(g), gidx)
        gbest = jnp.where(upd, gscore[g], gbest)
    msel = []
    for j in range(GROUP_SIZE):
        out = sel[j]
        for g in range(1, N_GROUPS):
            out = jnp.where(gidx == float(g), sel[g * GROUP_SIZE + j], out)
        msel.append(out)
    one = jnp.ones_like(gbest)
    zero = jnp.zeros_like(gbest)
    chosen = []
    for j in range(GROUP_SIZE):
        rank = zero
        for k in range(GROUP_SIZE):
            if k < j:
                rank = rank + jnp.where(msel[k] >= msel[j], one, zero)
            elif k > j:
                rank = rank + jnp.where(msel[k] > msel[j], one, zero)
        chosen.append(jnp.where(rank < 2.0, one, zero))
    c0, c1, c2, c3 = chosen
    order = jnp.where(c0 * c1 > 0, 0.0, jnp.where(c1 * c2 > 0, 1.0, jnp.where(c2 * c3 > 0, 2.0,
            jnp.where(c0 * c3 > 0, 3.0, jnp.where(c0 * c2 > 0, 4.0, 5.0)))))
    return gidx * float(N_PAIRS) + order


def _post_kernel(*refs, n_mix, group_tiles):
    n_groups = len(group_tiles)
    per_group = n_mix + 1
    group_refs = [refs[g * per_group:(g + 1) * per_group] for g in range(n_groups)]
    refs = refs[n_groups * per_group:]
    (w_f32_ref, g_ref, sh_ref, sc_ref, lnw_ref, lnb_ref, rwh_ref, rwl_ref, rb_ref, tri_ref,
     x1_ref, h2_ref, info_ref, cout_ref, cnt_ref, pre_ref, w_ref) = refs
    _cast_once(w_f32_ref, w_ref)
    i = pl.program_id(0)

    @pl.when(i == 0)
    def _():
        cnt_ref[...] = jnp.zeros_like(cnt_ref)

    first = 0
    for g in range(n_groups):
        @pl.when(jnp.logical_and(i >= first, i < first + group_tiles[g]))
        def _(g=g):
            out = None
            off = 0
            for m_ref in group_refs[g][:n_mix]:
                width = m_ref.shape[1]
                part = jnp.dot(m_ref[...], w_ref[off:off + width, :], preferred_element_type=F32)
                out = part if out is None else out + part
                off += width
            pre_ref[...] = ALPHA * group_refs[g][n_mix][...] + g_ref[...] * out
        first += group_tiles[g]

    tm = pre_ref.shape[0]
    ts = tri_ref.shape[0]
    base = cnt_ref[:, 0:1]
    for part in range(tm // ts):
        rows = slice(part * ts, (part + 1) * ts)
        x1 = _layer_norm(pre_ref[rows, :], lnw_ref[...], lnb_ref[...])
        x1_ref[rows, :] = x1
        h2 = x1 * (1.0 + sc_ref[...]) + sh_ref[...]
        for j in range(TOK_ROWS):
            h2_ref[pl.ds(part * ts * TOK_ROWS + j, ts, stride=TOK_ROWS), :] = h2[:, j * LANES:(j + 1) * LANES]
        h_hi = h2.astype(BF16)
        h_lo = (h2 - h_hi.astype(F32)).astype(BF16)
        both = jnp.dot(h_hi, rwl_ref[...], preferred_element_type=F32)
        logits = (both[:, :LANES] + both[:, LANES:]
                  + jnp.dot(h_lo, rwh_ref[...], preferred_element_type=F32))
        cls = _route_class(logits.T, rb_ref)
        crow = lax.broadcasted_iota(jnp.int32, (32, ts), 0).astype(F32)
        onehot = jnp.where(crow == cls, 1.0, 0.0)
        prefix = jnp.dot(onehot.astype(BF16), tri_ref[...], preferred_element_type=F32)
        rank = jnp.sum(onehot * (prefix - 1.0 + base), axis=0, keepdims=True)
        base = base + jnp.sum(onehot, axis=1, keepdims=True)
        packed = cls.astype(jnp.int32) * 65536 + rank.astype(jnp.int32)
        info_ref[:, rows] = packed
    cnt_ref[...] = jnp.broadcast_to(base, cnt_ref.shape)

    @pl.when(i == pl.num_programs(0) - 1)
    def _():
        cout_ref[...] = cnt_ref[...]


def _tri(tm):
    return jnp.asarray(np.triu(np.ones((tm, tm), np.float32))).astype(BF16)


def _group_spec(width, tm, first_tile, n_tiles, array_tile_off=0):
    return pl.BlockSpec((tm, width),
                        lambda i, *_: (jnp.clip(i - first_tile, 0, n_tiles - 1) + array_tile_off, 0))


def _post_mix(group_mixes, group_x, w_out_bf16, mod, layer, ln_w, ln_b, rw_hi, rw_lo, router_bias, cond_row, tm):
    group_tiles = tuple(m[0].shape[0] // tm for m in group_mixes)
    n_tiles = sum(group_tiles)
    total_rows = n_tiles * tm
    row = pl.BlockSpec((tm, D_MODEL), lambda i: (i, 0))
    vec = pl.BlockSpec((1, D_MODEL), lambda i: (0, 0))
    cnt = pl.BlockSpec((32, LANES), lambda i: (0, 0))
    in_specs, args = [], []
    first = 0
    for mixes, (x, x_row_off), nt in zip(group_mixes, group_x, group_tiles):
        for m in mixes:
            in_specs.append(_group_spec(m.shape[1], tm, first, nt))
            args.append(m)
        in_specs.append(_group_spec(D_MODEL, tm, first, nt, x_row_off // tm))
        args.append(x)
        first += nt
    in_specs += [
        pl.BlockSpec((D_MODEL, D_MODEL), lambda i: (0, 0), pipeline_mode=pl.Buffered(1)),
        _mod_spec(layer, 2, lambda i: cond_row(i, tm)), _mod_spec(layer, 3, lambda i: cond_row(i, tm)),
        _mod_spec(layer, 4, lambda i: cond_row(i, tm)),
        vec, vec,
        pl.BlockSpec((D_MODEL, LANES), lambda i: (0, 0)),
        pl.BlockSpec((D_MODEL, 2 * LANES), lambda i: (0, 0)),
        pl.BlockSpec(memory_space=pltpu.SMEM),
        pl.BlockSpec((POST_SUB, POST_SUB), lambda i: (0, 0)),
    ]
    args += [w_out_bf16, mod, mod, mod, ln_w.reshape(1, -1), ln_b.reshape(1, -1), rw_hi, rw_lo, router_bias,
             _tri(POST_SUB)]
    return pl.pallas_call(
        functools.partial(_post_kernel, n_mix=len(group_mixes[0]), group_tiles=group_tiles),
        grid=(n_tiles,),
        in_specs=in_specs,
        out_specs=[row, pl.BlockSpec((tm * TOK_ROWS, LANES), lambda i: (i, 0)),
                   pl.BlockSpec((1, tm), lambda i: (0, i)), cnt],
        out_shape=[jax.ShapeDtypeStruct((total_rows, D_MODEL), F32),
                   jax.ShapeDtypeStruct((total_rows * TOK_ROWS, LANES), F32),
                   jax.ShapeDtypeStruct((1, total_rows), jnp.int32), jax.ShapeDtypeStruct((32, LANES), F32)],
        scratch_shapes=[pltpu.VMEM((32, LANES), F32), pltpu.VMEM((tm, D_MODEL), F32),
                        pltpu.VMEM((D_MODEL, D_MODEL), BF16)],
        compiler_params=_cparams(("arbitrary",)),
        name="post_mix",
    )(*args)


def _plan_kernel(cnt_ref, ea_ref, eb_ref, fl_ref, rs_ref, *, n_tiles):
    start = jnp.int32(0)
    prev_a = jnp.int32(-1)
    prev_b = jnp.int32(-1)
    par_a = jnp.int32(0)
    par_b = jnp.int32(0)
    for c in range(N_CLASSES):
        n = cnt_ref[c]
        tiles = lax.shift_right_logical(n + (MOE_TM - 1), MOE_SHIFT)
        row0 = start * MOE_TM
        rs_ref[c] = row0
        g, pr = divmod(c, N_PAIRS)
        a = g * GROUP_SIZE + PAIR_SLOTS[pr][0]
        b = g * GROUP_SIZE + PAIR_SLOTS[pr][1]
        new_a = (prev_a != a).astype(jnp.int32)
        new_b = (prev_b != b).astype(jnp.int32)
        has = tiles > 0
        par_a = jnp.where(has, par_a ^ new_a, par_a)
        par_b = jnp.where(has, par_b ^ new_b, par_b)
        first = 1 + 4 * new_a + 8 * new_b
        common = 16 * par_a + 32 * par_b

        def tile_body(k, _, start=start, a=a, b=b, first=first, n=n, common=common):
            t = start + k
            ea_ref[t] = a
            eb_ref[t] = b
            half = 2 * (n - k * MOE_TM <= MOE_TM // 2).astype(jnp.int32)
            fl_ref[t] = jnp.where(k == 0, first, 1) + half + common
            return 0

        lax.fori_loop(0, tiles, tile_body, 0)
        prev_a = jnp.where(has, a, prev_a)
        prev_b = jnp.where(has, b, prev_b)
        start = start + tiles
    for c in range(N_CLASSES, 31):
        rs_ref[c] = 0
    rs_ref[31] = start

    def idle_body(t, _):
        ea_ref[t] = prev_a
        eb_ref[t] = prev_b
        fl_ref[t] = 0
        return 0

    lax.fori_loop(start, n_tiles, idle_body, 0)


def _moe_plan(counts, n_tiles):
    smem = pl.BlockSpec(memory_space=pltpu.SMEM)
    i32 = lambda n: jax.ShapeDtypeStruct((n,), jnp.int32)
    return pl.pallas_call(
        functools.partial(_plan_kernel, n_tiles=n_tiles),
        in_specs=[smem],
        out_specs=[smem] * 4,
        out_shape=[i32(n_tiles), i32(n_tiles), i32(n_tiles), i32(32)],
        name="moe_plan",
    )(counts)


def _class_row(packed, rs_ref):
    return rs_ref[lax.shift_right_logical(packed, 16)] + (packed & 0xFFFF)


def _dispatch_kernel(packed_ref, rs_ref, cnt_ref, h_ref, hs_ref, sem, fill_sem, *, tm, n_tiles):
    i = pl.program_id(0)
    used = rs_ref[31]

    def tile_rows(row, n=1):
        return pl.ds(pl.multiple_of(row * TOK_ROWS, TOK_ROWS), n * TOK_ROWS)

    def fill_copy(row, n):
        return pltpu.make_async_copy(h_ref.at[tile_rows(0, n)], hs_ref.at[tile_rows(row, n)], fill_sem)

    def fill(act):
        for c in range(N_CLASSES):
            n = cnt_ref[c]
            row = rs_ref[c] + n
            pad = (-n) & (MOE_TM - 1)
            for bit in reversed(range(MOE_SHIFT)):
                size = 1 << bit

                @pl.when((pad & size) != 0)
                def _(row=row, size=size):
                    act(fill_copy(row, size))
                row = row + (pad & size)
        lax.fori_loop(used, n_tiles, lambda t, _: (act(fill_copy(t * MOE_TM, MOE_TM)), 0)[1], 0)

    @pl.when(i == 0)
    def _():
        fill(lambda cp: cp.start())

    def body(r8, _):
        for j in range(8):
            r = r8 * 8 + j
            row = _class_row(packed_ref[0, i * tm + r], rs_ref)
            pltpu.make_async_copy(h_ref.at[tile_rows(r)], hs_ref.at[tile_rows(row)], sem).start(priority=j % 2)
        return 0

    lax.fori_loop(0, tm // 8, body, 0)
    pltpu.make_async_copy(h_ref, hs_ref.at[tile_rows(0, tm)], sem).wait()

    @pl.when(i == 0)
    def _():
        fill(lambda cp: cp.wait())


def _dispatch(h2_tiles, packed, plan, counts, n_tiles, tm=2048):
    T = packed.shape[1]
    assert tm >= MOE_TM and T % tm == 0
    grid_spec = pltpu.PrefetchScalarGridSpec(
        num_scalar_prefetch=3,
        grid=(T // tm,),
        in_specs=[pl.BlockSpec((tm * TOK_ROWS, LANES), lambda i, *_: (i, 0))],
        out_specs=pl.BlockSpec(memory_space=pl.ANY),
        scratch_shapes=[pltpu.SemaphoreType.DMA(()), pltpu.SemaphoreType.DMA(())],
    )
    return pl.pallas_call(
        functools.partial(_dispatch_kernel, tm=tm, n_tiles=n_tiles),
        grid_spec=grid_spec,
        out_shape=jax.ShapeDtypeStruct((n_tiles * MOE_TM * TOK_ROWS, LANES), F32),
        compiler_params=_cparams(("arbitrary",)),
        name="moe_dispatch",
    )(packed, plan[3], counts, h2_tiles)


def _moe_kernel(ea_ref, eb_ref, fl_ref, h_ref, rwt_ref, wga, wua, wda, wgb, wub, wdb, o_ref,
                sga, sua, sda, sgb, sub, sdb):
    s = pl.program_id(0)
    n_tiles = pl.num_programs(0) - 1
    nxt = fl_ref[jnp.minimum(s, n_tiles - 1)]
    stage = s < n_tiles

    @pl.when(jnp.logical_and(stage, (nxt & 4) != 0))
    def _():
        p = (nxt >> 4) & 1
        sga[p] = wga[...].astype(BF16)
        sua[p] = wua[...].astype(BF16)
        sda[p] = wda[...].astype(BF16)

    @pl.when(jnp.logical_and(stage, (nxt & 8) != 0))
    def _():
        p = (nxt >> 5) & 1
        sgb[p] = wgb[...].astype(BF16)
        sub[p] = wub[...].astype(BF16)
        sdb[p] = wdb[...].astype(BF16)

    t = jnp.maximum(s - 1, 0)
    flags = fl_ref[t]
    run = s > 0
    valid = jnp.logical_and(run, (flags & 1) != 0)
    pa = (flags >> 4) & 1
    pb = (flags >> 5) & 1

    def experts(rows):
        h = _load_token_tiles(h_ref, rows).astype(BF16)
        nt = (((1,), (1,)), ((), ()))
        score = jax.nn.sigmoid(lax.dot_general(h, rwt_ref[...], nt, preferred_element_type=F32))
        lane = lax.broadcasted_iota(jnp.int32, score.shape, 1)
        s_a = jnp.sum(jnp.where(lane == ea_ref[t], score, 0.0), axis=1, keepdims=True)
        s_b = jnp.sum(jnp.where(lane == eb_ref[t], score, 0.0), axis=1, keepdims=True)
        tot = s_a + s_b
        act_a = _silu(jnp.dot(h, sga[pa], preferred_element_type=F32)) \
            * jnp.dot(h, sua[pa], preferred_element_type=F32) * (s_a / tot)
        act_b = _silu(jnp.dot(h, sgb[pb], preferred_element_type=F32)) \
            * jnp.dot(h, sub[pb], preferred_element_type=F32) * (s_b / tot)
        _store_token_tiles(o_ref, jnp.dot(act_a.astype(BF16), sda[pa], preferred_element_type=F32)
                           + jnp.dot(act_b.astype(BF16), sdb[pb], preferred_element_type=F32))
        if rows < MOE_TM:
            o_ref[rows * TOK_ROWS:, :] = jnp.zeros(((MOE_TM - rows) * TOK_ROWS, LANES), F32)

    half = (flags & 2) != 0

    @pl.when(jnp.logical_and(valid, jnp.logical_not(half)))
    def _():
        experts(MOE_TM)

    @pl.when(jnp.logical_and(valid, half))
    def _():
        experts(MOE_TM // 2)

    @pl.when(jnp.logical_and(run, jnp.logical_not(valid)))
    def _():
        o_ref[...] = jnp.zeros_like(o_ref)


def _moe(hs, plan, layer, rwt_bf16, w_gate, w_up, w_down):
    ea, eb, flags, _ = plan
    n_tiles = ea.shape[0]
    ahead = lambda s: jnp.minimum(s, n_tiles - 1)
    wspec = lambda shape, which: pl.BlockSpec(
        (None, None) + shape, (lambda s, ea, eb, fl: (layer, ea[ahead(s)], 0, 0)) if which == 0
        else (lambda s, ea, eb, fl: (layer, eb[ahead(s)], 0, 0)))
    up = (D_MODEL, D_FF_EXPERT)
    dn = (D_FF_EXPERT, D_MODEL)
    tile = pl.BlockSpec((MOE_TM * TOK_ROWS, LANES), lambda s, ea, eb, fl: (jnp.maximum(s - 1, 0), 0))
    grid_spec = pltpu.PrefetchScalarGridSpec(
        num_scalar_prefetch=3,
        grid=(n_tiles + 1,),
        in_specs=[
            tile,
            pl.BlockSpec((N_EXPERTS, D_MODEL), lambda s, ea, eb, fl: (0, 0)),
            wspec(up, 0), wspec(up, 0), wspec(dn, 0), wspec(up, 1), wspec(up, 1), wspec(dn, 1),
        ],
        out_specs=tile,
        scratch_shapes=[pltpu.VMEM((2,) + up, BF16), pltpu.VMEM((2,) + up, BF16), pltpu.VMEM((2,) + dn, BF16),
                        pltpu.VMEM((2,) + up, BF16), pltpu.VMEM((2,) + up, BF16), pltpu.VMEM((2,) + dn, BF16)],
    )
    return pl.pallas_call(
        _moe_kernel,
        grid_spec=grid_spec,
        out_shape=jax.ShapeDtypeStruct((n_tiles * MOE_TM * TOK_ROWS, LANES), F32),
        compiler_params=_cparams(("arbitrary",)),
        name="moe",
    )(ea, eb, flags, hs, rwt_bf16, w_gate, w_up, w_down, w_gate, w_up, w_down)


def _gather_expert_rows(packed_ref, rs_ref, ys_ref, ybuf, sem, tm):
    i = pl.program_id(0)
    n_i = pl.num_programs(0)
    slot = i % 2

    def gather(tile, buf_slot):
        def body(r8, _):
            for j in range(8):
                r = r8 * 8 + j
                d = _class_row(packed_ref[0, tile * tm + r], rs_ref)
                pltpu.make_async_copy(ys_ref.at[pl.ds(pl.multiple_of(d * TOK_ROWS, TOK_ROWS), TOK_ROWS)],
                                      ybuf.at[buf_slot, pl.ds(pl.multiple_of(r * TOK_ROWS, TOK_ROWS), TOK_ROWS)],
                                      sem.at[buf_slot]).start(priority=j % 2)
            return 0
        lax.fori_loop(0, tm // 8, body, 0)

    @pl.when(i == 0)
    def _():
        gather(0, 0)

    @pl.when(i + 1 < n_i)
    def _():
        gather(jnp.minimum(i + 1, n_i - 1), 1 - slot)

    pltpu.make_async_copy(ys_ref.at[pl.ds(0, tm * TOK_ROWS)], ybuf.at[slot], sem.at[slot]).wait()
    return _load_token_tiles(ybuf, tm, slot)


def _fin_proj_kernel(packed_ref, rs_ref, x_ref, ys_ref, g_ref, lnw_ref, lnb_ref, sh_ref, sc_ref, w_f32_ref,
                     x2_ref, u_ref, ybuf, sem, w_ref, *, tm):
    _cast_once(w_f32_ref, w_ref)
    y = _gather_expert_rows(packed_ref, rs_ref, ys_ref, ybuf, sem, tm)
    x2 = _layer_norm(ALPHA * x_ref[...] + g_ref[...] * y, lnw_ref[...], lnb_ref[...])
    x2_ref[...] = x2
    h = (x2 * (1.0 + sc_ref[...]) + sh_ref[...]).astype(BF16)
    for s in range(u_ref.shape[1] // SEC):
        u_ref[:, s * SEC:(s + 1) * SEC] = jnp.dot(h, w_ref[:, s * SEC:(s + 1) * SEC],
                                                  preferred_element_type=F32).astype(BF16)


def _fin_proj(x1, ys, packed, row_start, mod, layer, ln_w, ln_b, w_next, cond_row, tm=512):
    T = x1.shape[0]
    N = w_next.shape[1]
    row = pl.BlockSpec((tm, D_MODEL), lambda i, *_: (i, 0))
    vec = pl.BlockSpec((1, D_MODEL), lambda i, *_: (0, 0))
    crow = lambda i: cond_row(i, tm)
    grid_spec = pltpu.PrefetchScalarGridSpec(
        num_scalar_prefetch=2,
        grid=(T // tm,),
        in_specs=[row, pl.BlockSpec(memory_space=pl.ANY), _mod_spec(layer, 5, crow), vec, vec,
                  _mod_spec(layer + 1, 0, crow), _mod_spec(layer + 1, 1, crow),
                  pl.BlockSpec((D_MODEL, N), lambda i, *_: (0, 0), pipeline_mode=pl.Buffered(1))],
        out_specs=[row, pl.BlockSpec((tm, N), lambda i, *_: (i, 0))],
        scratch_shapes=[pltpu.VMEM((2, tm * TOK_ROWS, LANES), F32), pltpu.SemaphoreType.DMA((2,)),
                        pltpu.VMEM((D_MODEL, N), BF16)],
    )
    return pl.pallas_call(
        functools.partial(_fin_proj_kernel, tm=tm),
        grid_spec=grid_spec,
        out_shape=[jax.ShapeDtypeStruct((T, D_MODEL), F32), jax.ShapeDtypeStruct((T, N), BF16)],
        compiler_params=_cparams(("arbitrary",)),
        name="post_moe_in_proj",
    )(packed, row_start, x1, ys, mod, ln_w.reshape(1, -1), ln_b.reshape(1, -1), mod, mod, w_next)


def _fin_kernel(packed_ref, rs_ref, x_ref, ys_ref, g_ref, lnw_ref, lnb_ref, *rest, tm, group_tiles):
    o_refs = rest[:len(group_tiles)]
    ybuf, sem = rest[len(group_tiles):]
    i = pl.program_id(0)
    y = _layer_norm(ALPHA * x_ref[...] + g_ref[...] * _gather_expert_rows(packed_ref, rs_ref, ys_ref, ybuf, sem, tm),
                    lnw_ref[...], lnb_ref[...])
    first = 0
    for o_ref, nt in zip(o_refs, group_tiles):
        @pl.when(jnp.logical_and(i >= first, i < first + nt))
        def _(o_ref=o_ref):
            o_ref[...] = y
        first += nt


def _fin(x1, ys, packed, row_start, mod, layer, ln_w, ln_b, cond_row, group_rows, tm=512):
    group_tiles = tuple(n // tm for n in group_rows)
    row = pl.BlockSpec((tm, D_MODEL), lambda i, *_: (i, 0))
    vec = pl.BlockSpec((1, D_MODEL), lambda i, *_: (0, 0))
    out_specs, first = [], 0
    for nt in group_tiles:
        out_specs.append(_group_spec(D_MODEL, tm, first, nt))
        first += nt
    grid_spec = pltpu.PrefetchScalarGridSpec(
        num_scalar_prefetch=2,
        grid=(sum(group_tiles),),
        in_specs=[row, pl.BlockSpec(memory_space=pl.ANY), _mod_spec(layer, 5, lambda i: cond_row(i, tm)), vec, vec],
        out_specs=out_specs,
        scratch_shapes=[pltpu.VMEM((2, tm * TOK_ROWS, LANES), F32), pltpu.SemaphoreType.DMA((2,))],
    )
    return pl.pallas_call(
        functools.partial(_fin_kernel, tm=tm, group_tiles=group_tiles),
        grid_spec=grid_spec,
        out_shape=[jax.ShapeDtypeStruct((n, D_MODEL), F32) for n in group_rows],
        compiler_params=_cparams(("arbitrary",)),
        name="post_moe",
    )(packed, row_start, x1, ys, mod, ln_w.reshape(1, -1), ln_b.reshape(1, -1))


def _filt_kernel(z_ref, w1_ref, b1_ref, w2_ref, b2_ref, fr_ref, w3_ref, dl_ref, o_ref):
    i = pl.program_id(0)
    z = z_ref[...]
    fr = fr_ref[...]
    a = jnp.sin(fr * (jnp.dot(z, w1_ref[...], precision=HIGHEST, preferred_element_type=F32) + b1_ref[...]))
    a = jnp.sin(fr * (jnp.dot(a, w2_ref[...], precision=HIGHEST, preferred_element_type=F32) + b2_ref[...]))
    filt = _dot_3pass(a, w3_ref[...])
    window = jnp.exp(-z[:, 0:1] * dl_ref[...]) + HY_SHIFT
    rows = i * z.shape[0] + lax.broadcasted_iota(jnp.int32, filt.shape, 0)
    o_ref[...] = jnp.where(rows == 0, 0.0, filt * window)


def _hyena_taps(seq_len, w1, b1, w2, b2, w3, freq):
    L = seq_len
    t = np.linspace(0.0, 1.0, L)[:, None]
    bands = np.linspace(1e-4, HY_BANDS - 1, HY_BANDS)
    ang = 2.0 * math.pi * bands[None, :] * np.arange(L)[:, None] / L
    z = np.concatenate([t, np.cos(ang), -np.sin(ang)], axis=-1)
    offs = np.minimum(np.abs(np.arange(2 * L) - L), L - 1)
    z2_np = np.zeros((2 * L, LANES), np.float32)
    z2_np[:, :z.shape[1]] = z[offs]
    z2 = jnp.asarray(z2_np)
    pad_c = lambda a: jnp.zeros((a.shape[0], LANES), F32).at[:, :a.shape[1]].set(a)
    pad_r = lambda a: jnp.zeros((LANES, a.shape[1]), F32).at[:a.shape[0], :].set(a)
    w1p = pad_r(pad_c(w1))
    w2p = pad_r(pad_c(w2))
    w3p = pad_r(w3)
    b1p, b2p, frp = pad_c(b1[None, :]), pad_c(b2[None, :]), pad_c(freq[None, :])
    deltas = jnp.asarray(np.abs(np.linspace(math.log(HY_DECAY_TARGET) / HY_SLOW, math.log(HY_DECAY_TARGET) / HY_FAST,
                                            D_MODEL))[None, :].astype(np.float32))
    rb = min(1024, L)
    cbf = D_MODEL
    ncb = D_MODEL // cbf
    nrb_back = L // rb
    sq = pl.BlockSpec((LANES, LANES), lambda i, j: (0, 0))
    vec = pl.BlockSpec((1, LANES), lambda i, j: (0, 0))
    return pl.pallas_call(
        _filt_kernel,
        grid=(2 * L // rb, ncb),
        in_specs=[
            pl.BlockSpec((rb, LANES), lambda i, j: (i, 0)),
            sq, vec, sq, vec, vec,
            pl.BlockSpec((LANES, cbf), lambda i, j: (0, jnp.where(i < nrb_back, ncb + j, j))),
            pl.BlockSpec((1, cbf), lambda i, j: (0, j)),
        ],
        out_specs=pl.BlockSpec((rb, cbf), lambda i, j: (i, j)),
        out_shape=jax.ShapeDtypeStruct((2 * L, D_MODEL), F32),
        compiler_params=_cparams(("arbitrary", "arbitrary")),
        name="hyena_taps",
    )(z2, w1p, b1p, w2p, b2p, frp, w3p, deltas)


def _dft_mats(cb):
    n = 2 * cb
    m = np.arange(cb)
    f = np.arange(cb)
    ang = 2.0 * np.pi * ((f[:, None] * m[None, :]) % n) / n
    fwd = np.concatenate([np.cos(ang), -np.sin(ang)], axis=0)
    fwd[cb, :] = np.where(m % 2 == 0, 1.0, -1.0)
    coef = np.where(f == 0, 1.0, 2.0)[None, :] / n
    inv = np.concatenate([coef * np.cos(ang.T), -coef * np.sin(ang.T)], axis=1)
    inv[:, cb] = np.where(m % 2 == 0, 1.0, -1.0) / n
    return fwd.astype(np.float32), inv.astype(np.float32)


def _hconv_kernel(x0_ref, x1_ref, v_ref, cw0_ref, cw1_ref, cw2_ref, cb0_ref, cb1_ref, cb2_ref,
                  taps_ref, fb_ref, fwd_ref, inv_ref, o_ref,
                  hs_ref, stage_ref, w32_ref, w_ref, x0c_ref, u_ref, y_ref, *, seq_len, blk, n_seq):
    L = seq_len
    nb = L // blk
    cw = x0_ref.shape[1]
    RC = 256
    bi = pl.program_id(1)
    row0 = lax.broadcasted_iota(jnp.int32, (RC, cw), 0) == 0

    @pl.when(bi == 0)
    def _():
        rows = lax.broadcasted_iota(jnp.int32, (2 * blk, 1), 0)
        sign = jnp.where(rows % 2 == 0, 1.0, -1.0)
        real_row = rows <= blk
        prev = None
        for jb in range(2 * nb):
            cur = jnp.dot(fwd_ref[...], taps_ref[jb * blk:(jb + 1) * blk, :].astype(BF16),
                          preferred_element_type=F32)
            if prev is not None:
                first_tap = taps_ref[(jb - 1) * blk:(jb - 1) * blk + 1, :].astype(BF16).astype(F32)
                hs_ref[jb - 1] = cur + sign * (prev - jnp.where(real_row, first_tap, 0.0))
            prev = cur

    zeros8 = jnp.zeros((8, cw), F32)
    for s in range(n_seq):
        r0 = s * L

        def short_conv(k, src_ref, cw_ref, cb_ref, store):
            stage_ref[k, 0:8, :] = zeros8
            stage_ref[k, 8 + L:16 + L, :] = zeros8
            for c in range(L // RC):
                stage_ref[k, 8 + c * RC:8 + (c + 1) * RC, :] = \
                    src_ref[r0 + c * RC:r0 + (c + 1) * RC, :].astype(F32)
            w = cw_ref[...]
            for c in range(L // RC):
                lo = stage_ref[k, 7 + c * RC:7 + (c + 1) * RC, :]
                mid = stage_ref[k, 8 + c * RC:8 + (c + 1) * RC, :]
                hi = stage_ref[k, 9 + c * RC:9 + (c + 1) * RC, :]
                store(c, lo * w[0:1, :] + mid * w[1:2, :] + hi * w[2:3, :] + cb_ref[...])

        def st_x1(c, val):
            w32_ref[c * RC:(c + 1) * RC, :] = val

        def st_v(c, val):
            w_ref[c * RC:(c + 1) * RC, :] = (w32_ref[c * RC:(c + 1) * RC, :] * val).astype(BF16)

        def st_x0(c, val):
            x0c_ref[c * RC:(c + 1) * RC, :] = val.astype(BF16)

        short_conv(0, x1_ref, cw1_ref, cb1_ref, st_x1)
        short_conv(1, v_ref, cw2_ref, cb2_ref, st_v)
        short_conv(2, x0_ref, cw0_ref, cb0_ref, st_x0)

        for j in range(nb):
            u_ref[j] = jnp.dot(fwd_ref[...], w_ref[j * blk:(j + 1) * blk, :], preferred_element_type=F32)

        for i in range(nb):
            for c in range(blk // RC):
                re = None
                im = None
                for j in range(nb):
                    k = i - j + nb - 1
                    a = u_ref[j, c * RC:(c + 1) * RC, :]
                    b = u_ref[j, blk + c * RC:blk + (c + 1) * RC, :]
                    hr = hs_ref[k, c * RC:(c + 1) * RC, :]
                    hi = hs_ref[k, blk + c * RC:blk + (c + 1) * RC, :]
                    bb = b * hi
                    if c == 0:
                        t_re = a * hr - jnp.where(row0, 0.0, bb)
                        t_im = jnp.where(row0, bb, a * hi + b * hr)
                    else:
                        t_re = a * hr - bb
                        t_im = a * hi + b * hr
                    re = t_re if re is None else re + t_re
                    im = t_im if im is None else im + t_im
                y_ref[i, c * RC:(c + 1) * RC, :] = re.astype(BF16)
                y_ref[i, blk + c * RC:blk + (c + 1) * RC, :] = im.astype(BF16)
            conv = jnp.dot(inv_ref[...], y_ref[i], preferred_element_type=F32)
            sl = slice(i * blk, (i + 1) * blk)
            z = x0c_ref[sl, :].astype(F32) * (conv + w_ref[sl, :].astype(F32) * fb_ref[...])
            o_ref[r0 + i * blk:r0 + (i + 1) * blk, :] = z.astype(BF16)


def _hyena_conv(u, conv_w, conv_b, taps, filt_bias, batch, seq_len, blk, n_seq, row_off=0, cw=256):
    T = batch * seq_len
    L = seq_len
    nb = L // blk
    ncw = D_MODEL // cw
    fwd_np, inv_np = _dft_mats(blk)
    fwd = jnp.asarray(fwd_np).astype(BF16)
    inv = jnp.asarray(inv_np).astype(BF16)
    rows = n_seq * L
    assert row_off % rows == 0
    boff = row_off // rows
    sec = lambda s: pl.BlockSpec((rows, cw), lambda c, b: (b + boff, s * ncw + c))
    cws = lambda s: pl.BlockSpec((3, cw), lambda c, b: (0, s * ncw + c))
    cbs = lambda s: pl.BlockSpec((1, cw), lambda c, b: (0, s * ncw + c))
    return pl.pallas_call(
        functools.partial(_hconv_kernel, seq_len=L, blk=blk, n_seq=n_seq),
        grid=(ncw, batch // n_seq),
        in_specs=[sec(0), sec(1), sec(2), cws(0), cws(1), cws(2), cbs(0), cbs(1), cbs(2),
                  pl.BlockSpec((2 * L, cw), lambda c, b: (0, c), pipeline_mode=pl.Buffered(1)),
                  pl.BlockSpec((1, cw), lambda c, b: (0, c)),
                  pl.BlockSpec((2 * blk, blk), lambda c, b: (0, 0), pipeline_mode=pl.Buffered(1)),
                  pl.BlockSpec((blk, 2 * blk), lambda c, b: (0, 0), pipeline_mode=pl.Buffered(1))],
        out_specs=pl.BlockSpec((rows, cw), lambda c, b: (b, c)),
        out_shape=jax.ShapeDtypeStruct((T, D_MODEL), BF16),
        scratch_shapes=[
            pltpu.VMEM((2 * nb - 1, 2 * blk, cw), F32),
            pltpu.VMEM((3, L + 16, cw), F32),
            pltpu.VMEM((L, cw), F32),
            pltpu.VMEM((L, cw), BF16),
            pltpu.VMEM((L, cw), BF16),
            pltpu.VMEM((nb, 2 * blk, cw), F32),
            pltpu.VMEM((nb, 2 * blk, cw), BF16),
        ],
        compiler_params=_cparams(("arbitrary", "arbitrary")),
        name="hyena_conv",
    )(u, u, u, conv_w, conv_w, conv_w, conv_b.reshape(1, -1), conv_b.reshape(1, -1), conv_b.reshape(1, -1),
      taps, filt_bias.reshape(1, -1), fwd, inv)


def kernel(x_prompt, x_sample, cache_diff_k, cache_diff_v, state_ret_fwd, state_ret_bwd, c, c_ctx, ada_w, ada_b, ln_w, ln_b, ev_w_in, ev_w_out, ret_decay_fwd, ret_decay_bwd, ret_gn_w, diff_lambda, diff_subln_w, hy_w_in, hy_conv_w, hy_conv_b, hy_ffn_w1, hy_ffn_b1, hy_ffn_w2, hy_ffn_b2, hy_ffn_w3, hy_freq, hy_filter_bias, hy_w_out, router_w, router_bias, moe_w_gate, moe_w_up, moe_w_down):
    B, S, D = x_prompt.shape
    DB, DS, _ = x_sample.shape
    PAST = cache_diff_k.shape[2]
    TP, TS = B * S, DB * DS
    T_ALL = TP + TS
    assert D == D_MODEL and 1 + DB <= COND_ROWS and T_ALL < 65536

    cond8 = jnp.zeros((COND_ROWS, D), F32).at[0].set(c_ctx).at[1:1 + DB].set(c)
    mod = _ada_mod(cond8, ada_w, ada_b).reshape(DEPTH * COND_ROWS * N_MOD, 1, D)

    tm = 512
    groups = [
        dict(x=x_prompt.reshape(TP, D), x_off=0, batch=B, seq=S, off=0, cond_row=lambda i, tm: 0),
        dict(x=x_sample.reshape(TS, D), x_off=0, batch=DB, seq=DS, off=TP,
             cond_row=lambda i, tm: 1 + (i * tm) // DS),
    ]
    u_next = None

    def cond_row_all(i, tm):
        return jnp.where(i * tm < TP, 0, 1 + (i * tm - TP) // DS)

    rw_pad = jnp.zeros((D, LANES), F32).at[:, :N_EXPERTS].set(router_w.astype(F32))
    rw_hi = rw_pad.astype(BF16)
    rw_lo = jnp.concatenate([rw_hi, (rw_pad - rw_hi.astype(F32)).astype(BF16)], axis=1)
    rwt = router_w.T.astype(BF16)
    rbias = router_bias.astype(F32)
    n_tiles = T_ALL // MOE_TM + N_CLASSES
    outs = {}

    for l in range(DEPTH):
        mixes = []
        if l % 2 == 0:
            e = l // 2
            w_in = ev_w_in[e]
            w_out = ev_w_out[e]
            lg = jnp.stack([jnp.log1p(-jnp.exp2(ret_decay_fwd[e].astype(F32))),
                            jnp.log1p(-jnp.exp2(ret_decay_bwd[e].astype(F32)))])
            lam_init = 0.8 - 0.6 * math.exp(-0.3 * l)
            lq1, lk1, lq2, lk2 = diff_lambda[e].astype(F32)
            lam = (jnp.exp(jnp.sum(lq1 * lk1)) - jnp.exp(jnp.sum(lq2 * lk2)) + lam_init).reshape(1)
            kscale = RET_DK ** -0.5
            for gi, g in enumerate(groups):
                tm_in = tm
                crow = functools.partial(g["cond_row"], tm=tm_in)
                if gi == 0:
                    secs = (("none", 1.0, False), ("none", kscale, False), ("none", 1.0, False),
                            ("none", 1.0, False), ("none", 1.0, False), ("none", 1.0, True),
                            ("none", 1.0, True))
                    proj, kd, vd = _in_proj(g["x"], mod, l, w_in, secs, crow, tm_in, tm=tm_in)
                    outs.setdefault("kd", []).append(kd.reshape(B, 1, S, H_DIFF, 2 * DIFF_QK))
                    outs.setdefault("vd", []).append(vd.reshape(B, 1, S, H_DIFF, DIFF_V))
                    ret, sf, sb = _retention(proj, lg, ret_gn_w[e], g["batch"], g["seq"], write_state=True,
                                             heads=H_RET)
                    outs.setdefault("sf", []).append(sf.reshape(B, 1, H_RET, RET_DK, RET_DV))
                    outs.setdefault("sb", []).append(sb.reshape(B, 1, H_RET, RET_DK, RET_DV))
                    att = _diff_attention(proj, lam, diff_subln_w[e], lam_init, g["batch"], g["seq"],
                                          tq=min(256, g["seq"]), heads=H_DIFF)
                else:
                    secs = (("ret", 1.0, False), ("ret", kscale, False), ("none", 1.0, False),
                            ("none", 1.0, False), ("diff", 1.0, False), ("diff", 1.0, False),
                            ("none", 1.0, False))
                    tabs = {"ret": _rope_tables(g["seq"], RET_DK), "diff": _rope_tables(g["seq"], DIFF_QK)}
                    (proj,) = _in_proj(g["x"], mod, l, w_in, secs, crow, g["seq"], rope_tabs=tabs, tm=tm_in)
                    ret, = _retention(proj, lg, ret_gn_w[e], g["batch"], g["seq"],
                                      states=(state_ret_fwd[:, e], state_ret_bwd[:, e]))
                    ctx = (cache_diff_k[:, e].reshape(DB, PAST, H_DIFF * 2 * DIFF_QK),
                           cache_diff_v[:, e].reshape(DB, PAST, H_DIFF * DIFF_V))
                    att = _diff_attention(proj, lam, diff_subln_w[e], lam_init, g["batch"], g["seq"], ctx=ctx,
                                          tq=1024, heads=1)
                mixes.append((ret, att))
        else:
            o = l // 2
            w_in = hy_w_in[o]
            w_out = hy_w_out[o]
            secs = (("none", 1.0, False),) * (3 * D // SEC)
            for gi, g in enumerate(groups):
                if u_next is None:
                    crow = functools.partial(g["cond_row"], tm=tm)
                    (u,) = _in_proj(g["x"], mod, l, w_in, secs, crow, tm, tm=tm)
                    u_off = 0
                else:
                    u, u_off = u_next, g["off"]
                taps = _hyena_taps(g["seq"], hy_ffn_w1[o], hy_ffn_b1[o], hy_ffn_w2[o], hy_ffn_b2[o],
                                   hy_ffn_w3[o], hy_freq[o])
                blk = min(g["seq"], 512)
                n_seq = max(1, 1024 // g["seq"])
                z = _hyena_conv(u, hy_conv_w[o], hy_conv_b[o], taps, hy_filter_bias[o], g["batch"], g["seq"],
                                blk, n_seq, row_off=u_off)
                mixes.append((z,))

        x1_all, h2_all, info_all, counts = _post_mix(mixes, [(g["x"], g["x_off"]) for g in groups], w_out, mod, l, ln_w[l, 0],
                                                     ln_b[l, 0], rw_hi, rw_lo, rbias, cond_row_all, tm)
        packed = info_all
        counts_i = counts[:, 0].astype(jnp.int32)
        plan = _moe_plan(counts_i, n_tiles)
        hs = _dispatch(h2_all, packed, plan, counts_i, n_tiles)
        ys = _moe(hs, plan, l, rwt, moe_w_gate, moe_w_up, moe_w_down)
        if l + 1 < DEPTH and (l + 1) % 2 == 1:
            x_all, u_next = _fin_proj(x1_all, ys, packed, plan[3], mod, l, ln_w[l, 1], ln_b[l, 1],
                                      hy_w_in[(l + 1) // 2], cond_row_all, tm=tm)
            for g in groups:
                g["x"], g["x_off"] = x_all, g["off"]
        else:
            u_next = None
            xs = _fin(x1_all, ys, packed, plan[3], mod, l, ln_w[l, 1], ln_b[l, 1], cond_row_all, (TP, TS))
            for g, x in zip(groups, xs):
                g["x"], g["x_off"] = x, 0

    y_prompt = groups[0]["x"].reshape(B, S, D)
    y_sample = groups[1]["x"].reshape(DB, DS, D)
    cat = lambda xs: xs[0] if len(xs) == 1 else jnp.concatenate(xs, axis=1)
    return (y_prompt, y_sample, cat(outs["kd"]), cat(outs["vd"]), cat(outs["sf"]), cat(outs["sb"]))
```

```python
import functools
import math

import numpy as np
import jax
import jax.numpy as jnp
from jax import lax
from jax.experimental import pallas as pl
from jax.experimental.pallas import tpu as pltpu

F32 = jnp.float32
BF16 = jnp.bfloat16
HIGHEST = lax.Precision.HIGHEST

D_MODEL = 1024
DEPTH = 2
GRID_W = 64
H_RET = 4
RET_DK = 128
RET_DV = 128
RET_CHUNK = 256
H_DIFF = 4
DIFF_QK = 64
DIFF_V = 128
ROPE_BASE = 10000.0
HY_BANDS = 16
HY_FH = 64
HY_DECAY_TARGET = 1e-2
HY_FAST = 0.3
HY_SLOW = 1.5
HY_SHIFT = 0.05
N_EXPERTS = 16
N_GROUPS = 4
GROUP_SIZE = N_EXPERTS // N_GROUPS
D_FF_EXPERT = 512
ALPHA = (2 * DEPTH) ** 0.25
LN_EPS = 1e-5

LANES = 128
SEC = 512
COND_ROWS = 8
N_MOD = 6
VMEM_LIMIT = 50 * 1024 * 1024

PAIR_SLOTS = ((0, 1), (2, 1), (2, 3), (0, 3), (0, 2), (1, 3))
N_PAIRS = len(PAIR_SLOTS)
N_CLASSES = N_GROUPS * N_PAIRS
MOE_TM = 256
MOE_SHIFT = MOE_TM.bit_length() - 1
POST_SUB = 512
TOK_ROWS = D_MODEL // LANES


def _cparams(sem):
    return pltpu.CompilerParams(dimension_semantics=sem, vmem_limit_bytes=VMEM_LIMIT)


def _silu(x):
    return x * jax.nn.sigmoid(x)


def _store_token_tiles(ref, x):
    n = x.shape[0]
    for j in range(TOK_ROWS):
        ref[pl.ds(j, n, stride=TOK_ROWS), :] = x[:, j * LANES:(j + 1) * LANES]


def _load_token_tiles(ref, n, slot=None):
    idx = () if slot is None else (slot,)
    return jnp.concatenate([ref[idx + (pl.ds(j, n, stride=TOK_ROWS), slice(None))] for j in range(TOK_ROWS)],
                           axis=1)


def _layer_norm(x, w, b):
    mu = jnp.mean(x, axis=-1, keepdims=True)
    xc = x - mu
    var = jnp.mean(xc * xc, axis=-1, keepdims=True)
    return xc * lax.rsqrt(var + LN_EPS) * w + b


def _dot_3pass(a, w):
    a_hi = a.astype(BF16)
    a_lo = (a - a_hi.astype(F32)).astype(BF16)
    w_hi = w.astype(BF16)
    w_lo = (w - w_hi.astype(F32)).astype(BF16)
    return (jnp.dot(a_hi, w_hi, preferred_element_type=F32) + jnp.dot(a_lo, w_hi, preferred_element_type=F32)
            + jnp.dot(a_hi, w_lo, preferred_element_type=F32))


def _ada_kernel(c_ref, w_ref, b_ref, o_ref):
    o_ref[...] = _dot_3pass(_silu(c_ref[...]), w_ref[...]) + b_ref[...]


def _ada_mod(cond8, ada_w, ada_b):
    tn = 1024
    nj = ada_w.shape[2] // tn
    return pl.pallas_call(
        _ada_kernel,
        grid=(DEPTH, nj),
        in_specs=[
            pl.BlockSpec((COND_ROWS, D_MODEL), lambda l, j: (0, 0)),
            pl.BlockSpec((None, D_MODEL, tn), lambda l, j: (l, 0, j)),
            pl.BlockSpec((None, 1, tn), lambda l, j: (l, 0, j)),
        ],
        out_specs=pl.BlockSpec((None, COND_ROWS, tn), lambda l, j: (l, 0, j)),
        out_shape=jax.ShapeDtypeStruct((DEPTH, COND_ROWS, ada_w.shape[2]), F32),
        compiler_params=_cparams(("arbitrary", "arbitrary")),
        name="ada_mod",
    )(cond8, ada_w, ada_b.reshape(DEPTH, 1, -1))


def _mod_spec(layer, chunk, row_of_tile):
    def imap(i, *_):
        return ((layer * COND_ROWS + row_of_tile(i)) * N_MOD + chunk, 0, 0)
    return pl.BlockSpec((None, 1, D_MODEL), imap)


def _rope(a, tabs, quarter):
    c, sa, sb = tabs
    out = []
    for hb in range(a.shape[1] // LANES):
        blk = a[:, hb * LANES:(hb + 1) * LANES]
        up = pltpu.roll(blk, LANES - quarter, axis=1)
        dn = pltpu.roll(blk, quarter, axis=1)
        out.append(blk * c + up * sa + dn * sb)
    return jnp.concatenate(out, axis=1)


def _cast_once(w_ref, w_bf16_ref):
    @pl.when(pl.program_id(0) == 0)
    def _():
        for c in range(0, w_ref.shape[1], SEC):
            w_bf16_ref[:, c:c + SEC] = w_ref[:, c:c + SEC].astype(BF16)


def _in_kernel(*refs, secs, n_f32_out):
    x_ref, sh_ref, sc_ref, w_ref = refs[:4]
    pos = 4
    tabs = {}
    for kind in ("ret", "diff"):
        if any(s[0] == kind for s in secs):
            tabs[kind] = tuple(r[...] for r in refs[pos:pos + 3])
            pos += 3
    o_ref = refs[pos]
    f32_refs = refs[pos + 1:pos + 1 + n_f32_out]
    w_bf16_ref = refs[-1]
    _cast_once(w_ref, w_bf16_ref)
    h = (x_ref[...] * (1.0 + sc_ref[...]) + sh_ref[...]).astype(BF16)
    k32 = 0
    for s, (kind, scale, want_f32) in enumerate(secs):
        acc = jnp.dot(h, w_bf16_ref[:, s * SEC:(s + 1) * SEC], preferred_element_type=F32)
        if scale != 1.0:
            acc = acc * scale
        if kind == "ret":
            acc = _rope(acc, tabs["ret"], RET_DK // 4)
        elif kind == "diff":
            acc = _rope(acc, tabs["diff"], DIFF_QK // 4)
        o_ref[:, s * SEC:(s + 1) * SEC] = acc.astype(BF16)
        if want_f32:
            for hb in range(SEC // LANES):
                f32_refs[k32][:, hb, :] = acc[:, hb * LANES:(hb + 1) * LANES]
            k32 += 1
    assert k32 == n_f32_out


def _in_proj(x2d, mod, layer, w_f32, secs, row_of_tile, seq_len, rope_tabs=None, tm=512):
    T = x2d.shape[0]
    N = w_f32.shape[1]
    assert N == SEC * len(secs) and T % tm == 0 and seq_len % tm == 0
    tiles_per_seq = seq_len // tm
    in_specs = [
        pl.BlockSpec((tm, D_MODEL), lambda i: (i, 0)),
        _mod_spec(layer, 0, row_of_tile),
        _mod_spec(layer, 1, row_of_tile),
        pl.BlockSpec((D_MODEL, N), lambda i: (0, 0), pipeline_mode=pl.Buffered(1)),
    ]
    args = [x2d, mod, mod, w_f32]
    for kind in ("ret", "diff"):
        if any(s[0] == kind for s in secs):
            for t in rope_tabs[kind]:
                in_specs.append(pl.BlockSpec((tm, LANES), lambda i: (i % tiles_per_seq, 0)))
                args.append(t)
    n_f32 = sum(1 for s in secs if s[2])
    heads = SEC // LANES
    out_shape = [jax.ShapeDtypeStruct((T, N), BF16)] + [jax.ShapeDtypeStruct((T, heads, LANES), F32)] * n_f32
    out_specs = ([pl.BlockSpec((tm, N), lambda i: (i, 0))]
                 + [pl.BlockSpec((tm, heads, LANES), lambda i: (i, 0, 0))] * n_f32)
    return pl.pallas_call(
        functools.partial(_in_kernel, secs=secs, n_f32_out=n_f32),
        grid=(T // tm,),
        in_specs=in_specs,
        out_specs=out_specs,
        out_shape=out_shape,
        scratch_shapes=[pltpu.VMEM((D_MODEL, N), BF16)],
        compiler_params=_cparams(("arbitrary",)),
        name="in_proj",
    )(*args)


def _rope_tables(seq_len, d):
    half = d // 2
    quarter = half // 2
    t = np.arange(seq_len)
    inv = ROPE_BASE ** (-np.arange(quarter, dtype=np.float64) / quarter)
    ang_r = (t // GRID_W)[:, None] * inv[None, :]
    ang_c = (t % GRID_W)[:, None] * inv[None, :]
    zero = np.zeros_like(ang_r)
    cos = np.concatenate([np.cos(ang_r)] * 2 + [np.cos(ang_c)] * 2, axis=1)
    sa = np.concatenate([-np.sin(ang_r), zero, -np.sin(ang_c), zero], axis=1)
    sb = np.concatenate([zero, np.sin(ang_r), zero, np.sin(ang_c)], axis=1)
    reps = LANES // d
    return tuple(jnp.asarray(np.tile(a, (1, reps)).astype(np.float32)) for a in (cos, sa, sb))


def _ret_kernel(*refs, n_chunks, has_state, write_state, heads):
    lg_ref, q_ref, k_ref, v_ref, g_ref, gnw_ref = refs[:6]
    pos = 6
    if has_state:
        s0f_ref, s0b_ref = refs[pos:pos + 2]
        pos += 2
    o_ref = refs[pos]
    pos += 1
    if write_state:
        sf_ref, sb_ref = refs[pos:pos + 2]
        pos += 2
    acc_ref, kv_ref = refs[pos:pos + 2]
    for hh in range(heads):
        _ret_head(hh, heads, lg_ref, q_ref, k_ref, v_ref, g_ref, gnw_ref,
                  (s0f_ref, s0b_ref) if has_state else None, o_ref, (sf_ref, sb_ref) if write_state else None,
                  acc_ref, kv_ref, n_chunks)


def _ret_head(hh, heads, lg_ref, q_ref, k_ref, v_ref, g_ref, gnw_ref, s0_refs, o_ref, s_out_refs, acc_ref, kv_ref,
              n_chunks):
    C = q_ref.shape[0] // n_chunks
    cols = slice(hh * LANES, (hh + 1) * LANES)
    hd = pl.program_id(1) * heads + hh
    lgf = lg_ref[0, hd]
    lgb = lg_ref[1, hd]
    ii = lax.broadcasted_iota(jnp.int32, (C, C), 0).astype(F32)
    jj = lax.broadcasted_iota(jnp.int32, (C, C), 1).astype(F32)
    rel = ii - jj
    d_f = jnp.where(rel >= 0, jnp.exp(jnp.maximum(rel, 0.0) * lgf), 0.0)
    d_b = jnp.where(rel <= 0, jnp.exp(jnp.maximum(-rel, 0.0) * lgb), 0.0)
    d_sum = d_f + d_b
    idx = lax.broadcasted_iota(jnp.int32, (C, 1), 0).astype(F32)
    xi_f = jnp.exp((idx + 1.0) * lgf)
    zeta_f = jnp.exp((C - 1.0 - idx) * lgf)
    xi_b = jnp.exp((C - idx) * lgb)
    zeta_b = jnp.exp(idx * lgb)
    one = jnp.ones((1, 1), F32)
    gc_f = jnp.exp(one * (C * lgf))
    gc_b = jnp.exp(one * (C * lgb))

    nt = (((1,), (1,)), ((), ()))
    tn = (((0,), (0,)), ((), ()))

    if s0_refs is not None:
        s_f = s0_refs[0][hh]
        s_b = s0_refs[1][hh]
    else:
        s_f = jnp.zeros((RET_DK, RET_DV), F32)
        s_b = jnp.zeros((RET_DK, RET_DV), F32)

    for n in range(n_chunks):
        sl = slice(n * C, (n + 1) * C)
        qc, kc, vc = q_ref[sl, cols], k_ref[sl, cols], v_ref[sl, cols]
        scores = lax.dot_general(qc, kc, nt, preferred_element_type=F32) * d_sum
        acc_ref[sl, cols] = jnp.dot(scores.astype(BF16), vc, preferred_element_type=F32)
        kf = kc.astype(F32)
        kv_ref[0, hh, n] = lax.dot_general((kf * zeta_f).astype(BF16), vc, tn, preferred_element_type=F32)
        kv_ref[1, hh, n] = lax.dot_general((kf * zeta_b).astype(BF16), vc, tn, preferred_element_type=F32)

    for n in range(n_chunks):
        sl = slice(n * C, (n + 1) * C)
        qf = q_ref[sl, cols].astype(F32)
        acc_ref[sl, cols] += jnp.dot((qf * xi_f).astype(BF16), s_f.astype(BF16), preferred_element_type=F32)
        s_f = gc_f * s_f + kv_ref[0, hh, n]

    gnw = gnw_ref[:, cols]
    for n in reversed(range(n_chunks)):
        sl = slice(n * C, (n + 1) * C)
        qf = q_ref[sl, cols].astype(F32)
        cross = jnp.dot((qf * xi_b).astype(BF16), s_b.astype(BF16), preferred_element_type=F32)
        r = acc_ref[sl, cols] + cross
        mu = jnp.mean(r, axis=-1, keepdims=True)
        rc = r - mu
        var = jnp.mean(rc * rc, axis=-1, keepdims=True)
        rn = rc * lax.rsqrt(var + LN_EPS) * gnw
        o_ref[sl, cols] = (_silu(g_ref[sl, cols].astype(F32)) * rn).astype(BF16)
        s_b = gc_b * s_b + kv_ref[1, hh, n]

    if s_out_refs is not None:
        s_out_refs[0][hh] = s_f
        s_out_refs[1][hh] = s_b


def _retention(proj, lg, gn_w, batch, seq_len, states=None, write_state=False, heads=2):
    T = proj.shape[0]
    hp = heads
    width = hp * LANES
    per_sec = SEC // width
    blk = lambda sec: pl.BlockSpec((seq_len, width), lambda b, h: (b, sec * per_sec + h))
    in_specs = [pl.BlockSpec(memory_space=pltpu.SMEM), blk(0), blk(1), blk(2), blk(3),
                pl.BlockSpec((1, width), lambda b, h: (0, h))]
    args = [lg, proj, proj, proj, proj, gn_w.reshape(1, -1)]
    if states is not None:
        st = pl.BlockSpec((None, hp, RET_DK, RET_DV), lambda b, h: (b, h, 0, 0))
        in_specs += [st, st]
        args += list(states)
    n_chunks = max(1, seq_len // RET_CHUNK)
    out_shape = [jax.ShapeDtypeStruct((T, SEC), BF16)]
    out_specs = [pl.BlockSpec((seq_len, width), lambda b, h: (b, h))]
    if write_state:
        st_o = pl.BlockSpec((None, hp, RET_DK, RET_DV), lambda b, h: (b, h, 0, 0))
        out_shape += [jax.ShapeDtypeStruct((batch, H_RET, RET_DK, RET_DV), F32)] * 2
        out_specs += [st_o, st_o]
    return pl.pallas_call(
        functools.partial(_ret_kernel, n_chunks=n_chunks, has_state=states is not None,
                          write_state=write_state, heads=hp),
        grid=(batch, H_RET // hp),
        in_specs=in_specs,
        out_specs=out_specs,
        out_shape=out_shape,
        scratch_shapes=[pltpu.VMEM((seq_len, width), F32),
                        pltpu.VMEM((2, hp, n_chunks, RET_DK, RET_DV), F32)],
        compiler_params=_cparams(("arbitrary", "arbitrary")),
        name="retention",
    )(*args)


def _att_kernel(*refs, has_ctx, out_scale, key_chunk, heads):
    lam_ref, q_ref, k_ref, v_ref = refs[:4]
    pos = 4
    if has_ctx:
        ck_ref, cv_ref = refs[pos:pos + 2]
        pos += 2
    w_ref, o_ref = refs[pos:pos + 2]
    lam = lam_ref[0]
    tq = q_ref.shape[0]
    nt = (((1,), (1,)), ((), ()))
    for hh in range(heads):
        cols = slice(hh * LANES, (hh + 1) * LANES)
        q = q_ref[:, cols]
        lane = lax.broadcasted_iota(jnp.int32, q.shape, 1)
        zero = jnp.zeros_like(q)
        qq = jnp.concatenate([jnp.where(lane < DIFF_QK, q, zero), jnp.where(lane >= DIFF_QK, q, zero)], axis=0)
        qq = qq * jnp.asarray(DIFF_QK ** -0.5, BF16)
        chunks = [(k_ref, v_ref, c * key_chunk, key_chunk) for c in range(k_ref.shape[0] // key_chunk)]
        if has_ctx:
            chunks.append((ck_ref, cv_ref, 0, ck_ref.shape[0]))
        m = l = acc = None
        for kr, vr, off, n in chunks:
            kch = kr[off:off + n, cols].astype(BF16)
            vch = vr[off:off + n, cols].astype(BF16)
            s = lax.dot_general(qq, kch, nt, preferred_element_type=F32)
            cm = jnp.max(s, axis=-1, keepdims=True)
            m_new = cm if m is None else jnp.maximum(m, cm)
            p = jnp.exp(s - m_new)
            ps = jnp.sum(p, axis=-1, keepdims=True)
            pv = jnp.dot(p.astype(BF16), vch, preferred_element_type=F32)
            if m is None:
                l, acc = ps, pv
            else:
                alpha = jnp.exp(m - m_new)
                l = alpha * l + ps
                acc = alpha * acc + pv
            m = m_new
        o = acc / l
        att = o[:tq] - lam * o[tq:]
        att = att * lax.rsqrt(jnp.mean(att * att, axis=-1, keepdims=True) + LN_EPS)
        o_ref[:, cols] = (att * w_ref[...] * out_scale).astype(BF16)


def _diff_attention(proj, lam, subln_w, lam_init, batch, seq_len, ctx=None, tq=256, heads=1):
    T = proj.shape[0]
    width = heads * LANES
    per_sec = SEC // width
    nq = seq_len // tq
    in_specs = [
        pl.BlockSpec(memory_space=pltpu.SMEM),
        pl.BlockSpec((tq, width), lambda b, h, i: (b * nq + i, 4 * per_sec + h)),
        pl.BlockSpec((seq_len, width), lambda b, h, i: (b, 5 * per_sec + h)),
        pl.BlockSpec((seq_len, width), lambda b, h, i: (b, 6 * per_sec + h)),
    ]
    args = [lam, proj, proj, proj]
    if ctx is not None:
        ck, cv = ctx
        past = ck.shape[1]
        cspec = pl.BlockSpec((None, past, width), lambda b, h, i: (b, 0, h))
        in_specs += [cspec, cspec]
        args += [ck, cv]
    in_specs.append(pl.BlockSpec((1, LANES), lambda b, h, i: (0, 0)))
    args.append(subln_w.reshape(1, -1))
    return pl.pallas_call(
        functools.partial(_att_kernel, has_ctx=ctx is not None, out_scale=1.0 - lam_init,
                          key_chunk=min(512, seq_len), heads=heads),
        grid=(batch, H_DIFF // heads, nq),
        in_specs=in_specs,
        out_specs=pl.BlockSpec((tq, width), lambda b, h, i: (b * nq + i, h)),
        out_shape=jax.ShapeDtypeStruct((T, SEC), BF16),
        compiler_params=_cparams(("arbitrary", "arbitrary", "arbitrary")),
        name="diff_attention",
    )(*args)


def _route_class(lt, rb_ref):
    sel = [jax.nn.sigmoid(lt[e:e + 1, :]) + rb_ref[e] for e in range(N_EXPERTS)]
    gscore = []
    for g in range(N_GROUPS):
        mem = sel[g * GROUP_SIZE:(g + 1) * GROUP_SIZE]
        best = None
        for a in range(GROUP_SIZE):
            for b in range(a + 1, GROUP_SIZE):
                pair = mem[a] + mem[b]
                best = pair if best is None else jnp.maximum(best, pair)
        gscore.append(best)
    gbest = gscore[0]
    gidx = jnp.zeros_like(gbest)
    for g in range(1, N_GROUPS):
        upd = gscore[g] > gbest
        gidx = jnp.where(upd, float(g), gidx)
        gbest = jnp.where(upd, gscore[g], gbest)
    msel = []
    for j in range(GROUP_SIZE):
        out = sel[j]
        for g in range(1, N_GROUPS):
            out = jnp.where(gidx == float(g), sel[g * GROUP_SIZE + j], out)
        msel.append(out)
    one = jnp.ones_like(gbest)
    zero = jnp.zeros_like(gbest)
    chosen = []
    for j in range(GROUP_SIZE):
        rank = zero
        for k in range(GROUP_SIZE):
            if k < j:
                rank = rank + jnp.where(msel[k] >= msel[j], one, zero)
            elif k > j:
                rank = rank + jnp.where(msel[k] > msel[j], one, zero)
        chosen.append(jnp.where(rank < 2.0, one, zero))
    c0, c1, c2, c3 = chosen
    order = jnp.where(c0 * c1 > 0, 0.0, jnp.where(c1 * c2 > 0, 1.0, jnp.where(c2 * c3 > 0, 2.0,
            jnp.where(c0 * c3 > 0, 3.0, jnp.where(c0 * c2 > 0, 4.0, 5.0)))))
    return gidx * float(N_PAIRS) + order


def _post_kernel(*refs, n_mix, group_tiles):
    n_groups = len(group_tiles)
    per_group = n_mix + 1
    group_refs = [refs[g * per_group:(g + 1) * per_group] for g in range(n_groups)]
    refs = refs[n_groups * per_group:]
    (w_f32_ref, g_ref, sh_ref, sc_ref, lnw_ref, lnb_ref, rwh_ref, rwl_ref, rb_ref, tri_ref,
     x1_ref, h2_ref, info_ref, cout_ref, cnt_ref, pre_ref, w_ref) = refs
    _cast_once(w_f32_ref, w_ref)
    i = pl.program_id(0)

    @pl.when(i == 0)
    def _():
        cnt_ref[...] = jnp.zeros_like(cnt_ref)

    first = 0
    for g in range(n_groups):
        @pl.when(jnp.logical_and(i >= first, i < first + group_tiles[g]))
        def _(g=g):
            out = None
            off = 0
            for m_ref in group_refs[g][:n_mix]:
                width = m_ref.shape[1]
                part = jnp.dot(m_ref[...], w_ref[off:off + width, :], preferred_element_type=F32)
                out = part if out is None else out + part
                off += width
            pre_ref[...] = ALPHA * group_refs[g][n_mix][...] + g_ref[...] * out
        first += group_tiles[g]

    tm = pre_ref.shape[0]
    ts = tri_ref.shape[0]
    base = cnt_ref[:, 0:1]
    for part in range(tm // ts):
        rows = slice(part * ts, (part + 1) * ts)
        x1 = _layer_norm(pre_ref[rows, :], lnw_ref[...], lnb_ref[...])
        x1_ref[rows, :] = x1
        h2 = x1 * (1.0 + sc_ref[...]) + sh_ref[...]
        for j in range(TOK_ROWS):
            h2_ref[pl.ds(part * ts * TOK_ROWS + j, ts, stride=TOK_ROWS), :] = h2[:, j * LANES:(j + 1) * LANES]
        h_hi = h2.astype(BF16)
        h_lo = (h2 - h_hi.astype(F32)).astype(BF16)
        both = jnp.dot(h_hi, rwl_ref[...], preferred_element_type=F32)
        logits = (both[:, :LANES] + both[:, LANES:]
                  + jnp.dot(h_lo, rwh_ref[...], preferred_element_type=F32))
        cls = _route_class(logits.T, rb_ref)
        crow = lax.broadcasted_iota(jnp.int32, (32, ts), 0).astype(F32)
        onehot = jnp.where(crow == cls, 1.0, 0.0)
        prefix = jnp.dot(onehot.astype(BF16), tri_ref[...], preferred_element_type=F32)
        rank = jnp.sum(onehot * (prefix - 1.0 + base), axis=0, keepdims=True)
        base = base + jnp.sum(onehot, axis=1, keepdims=True)
        packed = cls.astype(jnp.int32) * 65536 + rank.astype(jnp.int32)
        info_ref[:, rows] = packed
    cnt_ref[...] = jnp.broadcast_to(base, cnt_ref.shape)

    @pl.when(i == pl.num_programs(0) - 1)
    def _():
        cout_ref[...] = cnt_ref[...]


def _tri(tm):
    return jnp.asarray(np.triu(np.ones((tm, tm), np.float32))).astype(BF16)


def _group_spec(width, tm, first_tile, n_tiles, array_tile_off=0):
    return pl.BlockSpec((tm, width),
                        lambda i, *_: (jnp.clip(i - first_tile, 0, n_tiles - 1) + array_tile_off, 0))


def _post_mix(group_mixes, group_x, w_out_bf16, mod, layer, ln_w, ln_b, rw_hi, rw_lo, router_bias, cond_row, tm):
    group_tiles = tuple(m[0].shape[0] // tm for m in group_mixes)
    n_tiles = sum(group_tiles)
    total_rows = n_tiles * tm
    row = pl.BlockSpec((tm, D_MODEL), lambda i: (i, 0))
    vec = pl.BlockSpec((1, D_MODEL), lambda i: (0, 0))
    cnt = pl.BlockSpec((32, LANES), lambda i: (0, 0))
    in_specs, args = [], []
    first = 0
    for mixes, (x, x_row_off), nt in zip(group_mixes, group_x, group_tiles):
        for m in mixes:
            in_specs.append(_group_spec(m.shape[1], tm, first, nt))
            args.append(m)
        in_specs.append(_group_spec(D_MODEL, tm, first, nt, x_row_off // tm))
        args.append(x)
        first += nt
    in_specs += [
        pl.BlockSpec((D_MODEL, D_MODEL), lambda i: (0, 0), pipeline_mode=pl.Buffered(1)),
        _mod_spec(layer, 2, lambda i: cond_row(i, tm)), _mod_spec(layer, 3, lambda i: cond_row(i, tm)),
        _mod_spec(layer, 4, lambda i: cond_row(i, tm)),
        vec, vec,
        pl.BlockSpec((D_MODEL, LANES), lambda i: (0, 0)),
        pl.BlockSpec((D_MODEL, 2 * LANES), lambda i: (0, 0)),
        pl.BlockSpec(memory_space=pltpu.SMEM),
        pl.BlockSpec((POST_SUB, POST_SUB), lambda i: (0, 0)),
    ]
    args += [w_out_bf16, mod, mod, mod, ln_w.reshape(1, -1), ln_b.reshape(1, -1), rw_hi, rw_lo, router_bias,
             _tri(POST_SUB)]
    return pl.pallas_call(
        functools.partial(_post_kernel, n_mix=len(group_mixes[0]), group_tiles=group_tiles),
        grid=(n_tiles,),
        in_specs=in_specs,
        out_specs=[row, pl.BlockSpec((tm * TOK_ROWS, LANES), lambda i: (i, 0)),
                   pl.BlockSpec((1, tm), lambda i: (0, i)), cnt],
        out_shape=[jax.ShapeDtypeStruct((total_rows, D_MODEL), F32),
                   jax.ShapeDtypeStruct((total_rows * TOK_ROWS, LANES), F32),
                   jax.ShapeDtypeStruct((1, total_rows), jnp.int32), jax.ShapeDtypeStruct((32, LANES), F32)],
        scratch_shapes=[pltpu.VMEM((32, LANES), F32), pltpu.VMEM((tm, D_MODEL), F32),
                        pltpu.VMEM((D_MODEL, D_MODEL), BF16)],
        compiler_params=_cparams(("arbitrary",)),
        name="post_mix",
    )(*args)


def _plan_kernel(cnt_ref, ea_ref, eb_ref, fl_ref, rs_ref, *, n_tiles):
    start = jnp.int32(0)
    prev_a = jnp.int32(-1)
    prev_b = jnp.int32(-1)
    par_a = jnp.int32(0)
    par_b = jnp.int32(0)
    for c in range(N_CLASSES):
        n = cnt_ref[c]
        tiles = lax.shift_right_logical(n + (MOE_TM - 1), MOE_SHIFT)
        row0 = start * MOE_TM
        rs_ref[c] = row0
        g, pr = divmod(c, N_PAIRS)
        a = g * GROUP_SIZE + PAIR_SLOTS[pr][0]
        b = g * GROUP_SIZE + PAIR_SLOTS[pr][1]
        new_a = (prev_a != a).astype(jnp.int32)
        new_b = (prev_b != b).astype(jnp.int32)
        has = tiles > 0
        par_a = jnp.where(has, par_a ^ new_a, par_a)
        par_b = jnp.where(has, par_b ^ new_b, par_b)
        first = 1 + 4 * new_a + 8 * new_b
        common = 16 * par_a + 32 * par_b

        def tile_body(k, _, start=start, a=a, b=b, first=first, n=n, common=common):
            t = start + k
            ea_ref[t] = a
            eb_ref[t] = b
            half = 2 * (n - k * MOE_TM <= MOE_TM // 2).astype(jnp.int32)
            fl_ref[t] = jnp.where(k == 0, first, 1) + half + common
            return 0

        lax.fori_loop(0, tiles, tile_body, 0)
        prev_a = jnp.where(has, a, prev_a)
        prev_b = jnp.where(has, b, prev_b)
        start = start + tiles
    for c in range(N_CLASSES, 31):
        rs_ref[c] = 0
    rs_ref[31] = start

    def idle_body(t, _):
        ea_ref[t] = prev_a
        eb_ref[t] = prev_b
        fl_ref[t] = 0
        return 0

    lax.fori_loop(start, n_tiles, idle_body, 0)


def _moe_plan(counts, n_tiles):
    smem = pl.BlockSpec(memory_space=pltpu.SMEM)
    i32 = lambda n: jax.ShapeDtypeStruct((n,), jnp.int32)
    return pl.pallas_call(
        functools.partial(_plan_kernel, n_tiles=n_tiles),
        in_specs=[smem],
        out_specs=[smem] * 4,
        out_shape=[i32(n_tiles), i32(n_tiles), i32(n_tiles), i32(32)],
        name="moe_plan",
    )(counts)


def _class_row(packed, rs_ref):
    return rs_ref[lax.shift_right_logical(packed, 16)] + (packed & 0xFFFF)


def _dispatch_kernel(packed_ref, rs_ref, cnt_ref, h_ref, hs_ref, sem, fill_sem, *, tm, n_tiles):
    i = pl.program_id(0)
    used = rs_ref[31]

    def tile_rows(row, n=1):
        return pl.ds(pl.multiple_of(row * TOK_ROWS, TOK_ROWS), n * TOK_ROWS)

    def fill_copy(row, n):
        return pltpu.make_async_copy(h_ref.at[tile_rows(0, n)], hs_ref.at[tile_rows(row, n)], fill_sem)

    def fill(act):
        for c in range(N_CLASSES):
            n = cnt_ref[c]
            row = rs_ref[c] + n
            pad = (-n) & (MOE_TM - 1)
            for bit in reversed(range(MOE_SHIFT)):
                size = 1 << bit

                @pl.when((pad & size) != 0)
                def _(row=row, size=size):
                    act(fill_copy(row, size))
                row = row + (pad & size)
        lax.fori_loop(used, n_tiles, lambda t, _: (act(fill_copy(t * MOE_TM, MOE_TM)), 0)[1], 0)

    @pl.when(i == 0)
    def _():
        fill(lambda cp: cp.start())

    def body(r8, _):
        for j in range(8):
            r = r8 * 8 + j
            row = _class_row(packed_ref[0, i * tm + r], rs_ref)
            pltpu.make_async_copy(h_ref.at[tile_rows(r)], hs_ref.at[tile_rows(row)], sem).start(priority=j % 2)
        return 0

    lax.fori_loop(0, tm // 8, body, 0)
    pltpu.make_async_copy(h_ref, hs_ref.at[tile_rows(0, tm)], sem).wait()

    @pl.when(i == 0)
    def _():
        fill(lambda cp: cp.wait())


def _dispatch(h2_tiles, packed, plan, counts, n_tiles, tm=2048):
    T = packed.shape[1]
    assert tm >= MOE_TM and T % tm == 0
    grid_spec = pltpu.PrefetchScalarGridSpec(
        num_scalar_prefetch=3,
        grid=(T // tm,),
        in_specs=[pl.BlockSpec((tm * TOK_ROWS, LANES), lambda i, *_: (i, 0))],
        out_specs=pl.BlockSpec(memory_space=pl.ANY),
        scratch_shapes=[pltpu.SemaphoreType.DMA(()), pltpu.SemaphoreType.DMA(())],
    )
    return pl.pallas_call(
        functools.partial(_dispatch_kernel, tm=tm, n_tiles=n_tiles),
        grid_spec=grid_spec,
        out_shape=jax.ShapeDtypeStruct((n_tiles * MOE_TM * TOK_ROWS, LANES), F32),
        compiler_params=_cparams(("arbitrary",)),
        name="moe_dispatch",
    )(packed, plan[3], counts, h2_tiles)


def _moe_kernel(ea_ref, eb_ref, fl_ref, h_ref, rwt_ref, wga, wua, wda, wgb, wub, wdb, o_ref,
                sga, sua, sda, sgb, sub, sdb):
    s = pl.program_id(0)
    n_tiles = pl.num_programs(0) - 1
    nxt = fl_ref[jnp.minimum(s, n_tiles - 1)]
    stage = s < n_tiles

    @pl.when(jnp.logical_and(stage, (nxt & 4) != 0))
    def _():
        p = (nxt >> 4) & 1
        sga[p] = wga[...].astype(BF16)
        sua[p] = wua[...].astype(BF16)
        sda[p] = wda[...].astype(BF16)

    @pl.when(jnp.logical_and(stage, (nxt & 8) != 0))
    def _():
        p = (nxt >> 5) & 1
        sgb[p] = wgb[...].astype(BF16)
        sub[p] = wub[...].astype(BF16)
        sdb[p] = wdb[...].astype(BF16)

    t = jnp.maximum(s - 1, 0)
    flags = fl_ref[t]
    run = s > 0
    valid = jnp.logical_and(run, (flags & 1) != 0)
    pa = (flags >> 4) & 1
    pb = (flags >> 5) & 1

    def experts(rows):
        h32 = _load_token_tiles(h_ref, rows)
        h = h32.astype(BF16)
        s_a = jax.nn.sigmoid(jnp.sum(h32 * rwt_ref[pl.ds(ea_ref[t], 1), :], axis=1, keepdims=True))
        s_b = jax.nn.sigmoid(jnp.sum(h32 * rwt_ref[pl.ds(eb_ref[t], 1), :], axis=1, keepdims=True))
        tot = s_a + s_b
        act_a = _silu(jnp.dot(h, sga[pa], preferred_element_type=F32)) \
            * jnp.dot(h, sua[pa], preferred_element_type=F32) * (s_a / tot)
        act_b = _silu(jnp.dot(h, sgb[pb], preferred_element_type=F32)) \
            * jnp.dot(h, sub[pb], preferred_element_type=F32) * (s_b / tot)
        _store_token_tiles(o_ref, jnp.dot(act_a.astype(BF16), sda[pa], preferred_element_type=F32)
                           + jnp.dot(act_b.astype(BF16), sdb[pb], preferred_element_type=F32))
        if rows < MOE_TM:
            o_ref[rows * TOK_ROWS:, :] = jnp.zeros(((MOE_TM - rows) * TOK_ROWS, LANES), F32)

    half = (flags & 2) != 0

    @pl.when(jnp.logical_and(valid, jnp.logical_not(half)))
    def _():
        experts(MOE_TM)

    @pl.when(jnp.logical_and(valid, half))
    def _():
        experts(MOE_TM // 2)

    @pl.when(jnp.logical_and(run, jnp.logical_not(valid)))
    def _():
        o_ref[...] = jnp.zeros_like(o_ref)


def _moe(hs, plan, layer, rwt, w_gate, w_up, w_down):
    ea, eb, flags, _ = plan
    n_tiles = ea.shape[0]
    ahead = lambda s: jnp.minimum(s, n_tiles - 1)
    wspec = lambda shape, which: pl.BlockSpec(
        (None, None) + shape, (lambda s, ea, eb, fl: (layer, ea[ahead(s)], 0, 0)) if which == 0
        else (lambda s, ea, eb, fl: (layer, eb[ahead(s)], 0, 0)))
    up = (D_MODEL, D_FF_EXPERT)
    dn = (D_FF_EXPERT, D_MODEL)
    tile = pl.BlockSpec((MOE_TM * TOK_ROWS, LANES), lambda s, ea, eb, fl: (jnp.maximum(s - 1, 0), 0))
    grid_spec = pltpu.PrefetchScalarGridSpec(
        num_scalar_prefetch=3,
        grid=(n_tiles + 1,),
        in_specs=[
            tile,
            pl.BlockSpec((N_EXPERTS, D_MODEL), lambda s, ea, eb, fl: (0, 0)),
            wspec(up, 0), wspec(up, 0), wspec(dn, 0), wspec(up, 1), wspec(up, 1), wspec(dn, 1),
        ],
        out_specs=tile,
        scratch_shapes=[pltpu.VMEM((2,) + up, BF16), pltpu.VMEM((2,) + up, BF16), pltpu.VMEM((2,) + dn, BF16),
                        pltpu.VMEM((2,) + up, BF16), pltpu.VMEM((2,) + up, BF16), pltpu.VMEM((2,) + dn, BF16)],
    )
    return pl.pallas_call(
        _moe_kernel,
        grid_spec=grid_spec,
        out_shape=jax.ShapeDtypeStruct((n_tiles * MOE_TM * TOK_ROWS, LANES), F32),
        compiler_params=_cparams(("arbitrary",)),
        name="moe",
    )(ea, eb, flags, hs, rwt, w_gate, w_up, w_down, w_gate, w_up, w_down)


def _gather_expert_rows(packed_ref, rs_ref, ys_ref, ybuf, sem, tm):
    i = pl.program_id(0)
    n_i = pl.num_programs(0)
    slot = i % 2

    def gather(tile, buf_slot):
        def body(r8, _):
            for j in range(8):
                r = r8 * 8 + j
                d = _class_row(packed_ref[0, tile * tm + r], rs_ref)
                pltpu.make_async_copy(ys_ref.at[pl.ds(pl.multiple_of(d * TOK_ROWS, TOK_ROWS), TOK_ROWS)],
                                      ybuf.at[buf_slot, pl.ds(pl.multiple_of(r * TOK_ROWS, TOK_ROWS), TOK_ROWS)],
                                      sem.at[buf_slot]).start(priority=j % 2)
            return 0
        lax.fori_loop(0, tm // 8, body, 0)

    @pl.when(i == 0)
    def _():
        gather(0, 0)

    @pl.when(i + 1 < n_i)
    def _():
        gather(jnp.minimum(i + 1, n_i - 1), 1 - slot)

    pltpu.make_async_copy(ys_ref.at[pl.ds(0, tm * TOK_ROWS)], ybuf.at[slot], sem.at[slot]).wait()
    return _load_token_tiles(ybuf, tm, slot)


def _fin_proj_kernel(packed_ref, rs_ref, x_ref, ys_ref, g_ref, lnw_ref, lnb_ref, sh_ref, sc_ref, w_f32_ref,
                     x2_ref, u_ref, ybuf, sem, w_ref, *, tm):
    _cast_once(w_f32_ref, w_ref)
    y = _gather_expert_rows(packed_ref, rs_ref, ys_ref, ybuf, sem, tm)
    x2 = _layer_norm(ALPHA * x_ref[...] + g_ref[...] * y, lnw_ref[...], lnb_ref[...])
    x2_ref[...] = x2
    h = (x2 * (1.0 + sc_ref[...]) + sh_ref[...]).astype(BF16)
    for s in range(u_ref.shape[1] // SEC):
        u_ref[:, s * SEC:(s + 1) * SEC] = jnp.dot(h, w_ref[:, s * SEC:(s + 1) * SEC],
                                                  preferred_element_type=F32).astype(BF16)


def _fin_proj(x1, ys, packed, row_start, mod, layer, ln_w, ln_b, w_next, cond_row, tm=512):
    T = x1.shape[0]
    N = w_next.shape[1]
    row = pl.BlockSpec((tm, D_MODEL), lambda i, *_: (i, 0))
    vec = pl.BlockSpec((1, D_MODEL), lambda i, *_: (0, 0))
    crow = lambda i: cond_row(i, tm)
    grid_spec = pltpu.PrefetchScalarGridSpec(
        num_scalar_prefetch=2,
        grid=(T // tm,),
        in_specs=[row, pl.BlockSpec(memory_space=pl.ANY), _mod_spec(layer, 5, crow), vec, vec,
                  _mod_spec(layer + 1, 0, crow), _mod_spec(layer + 1, 1, crow),
                  pl.BlockSpec((D_MODEL, N), lambda i, *_: (0, 0), pipeline_mode=pl.Buffered(1))],
        out_specs=[row, pl.BlockSpec((tm, N), lambda i, *_: (i, 0))],
        scratch_shapes=[pltpu.VMEM((2, tm * TOK_ROWS, LANES), F32), pltpu.SemaphoreType.DMA((2,)),
                        pltpu.VMEM((D_MODEL, N), BF16)],
    )
    return pl.pallas_call(
        functools.partial(_fin_proj_kernel, tm=tm),
        grid_spec=grid_spec,
        out_shape=[jax.ShapeDtypeStruct((T, D_MODEL), F32), jax.ShapeDtypeStruct((T, N), BF16)],
        compiler_params=_cparams(("arbitrary",)),
        name="post_moe_in_proj",
    )(packed, row_start, x1, ys, mod, ln_w.reshape(1, -1), ln_b.reshape(1, -1), mod, mod, w_next)


def _fin_kernel(packed_ref, rs_ref, x_ref, ys_ref, g_ref, lnw_ref, lnb_ref, *rest, tm, group_tiles):
    o_refs = rest[:len(group_tiles)]
    ybuf, sem = rest[len(group_tiles):]
    i = pl.program_id(0)
    y = _layer_norm(ALPHA * x_ref[...] + g_ref[...] * _gather_expert_rows(packed_ref, rs_ref, ys_ref, ybuf, sem, tm),
                    lnw_ref[...], lnb_ref[...])
    first = 0
    for o_ref, nt in zip(o_refs, group_tiles):
        @pl.when(jnp.logical_and(i >= first, i < first + nt))
        def _(o_ref=o_ref):
            o_ref[...] = y
        first += nt


def _fin(x1, ys, packed, row_start, mod, layer, ln_w, ln_b, cond_row, group_rows, tm=512):
    group_tiles = tuple(n // tm for n in group_rows)
    row = pl.BlockSpec((tm, D_MODEL), lambda i, *_: (i, 0))
    vec = pl.BlockSpec((1, D_MODEL), lambda i, *_: (0, 0))
    out_specs, first = [], 0
    for nt in group_tiles:
        out_specs.append(_group_spec(D_MODEL, tm, first, nt))
        first += nt
    grid_spec = pltpu.PrefetchScalarGridSpec(
        num_scalar_prefetch=2,
        grid=(sum(group_tiles),),
        in_specs=[row, pl.BlockSpec(memory_space=pl.ANY), _mod_spec(layer, 5, lambda i: cond_row(i, tm)), vec, vec],
        out_specs=out_specs,
        scratch_shapes=[pltpu.VMEM((2, tm * TOK_ROWS, LANES), F32), pltpu.SemaphoreType.DMA((2,))],
    )
    return pl.pallas_call(
        functools.partial(_fin_kernel, tm=tm, group_tiles=group_tiles),
        grid_spec=grid_spec,
        out_shape=[jax.ShapeDtypeStruct((n, D_MODEL), F32) for n in group_rows],
        compiler_params=_cparams(("arbitrary",)),
        name="post_moe",
    )(packed, row_start, x1, ys, mod, ln_w.reshape(1, -1), ln_b.reshape(1, -1))


def _filt_kernel(z_ref, w1_ref, b1_ref, w2_ref, b2_ref, fr_ref, w3_ref, dl_ref, o_ref):
    i = pl.program_id(0)
    z = z_ref[...]
    fr = fr_ref[...]
    a = jnp.sin(fr * (jnp.dot(z, w1_ref[...], precision=HIGHEST, preferred_element_type=F32) + b1_ref[...]))
    a = jnp.sin(fr * (jnp.dot(a, w2_ref[...], precision=HIGHEST, preferred_element_type=F32) + b2_ref[...]))
    filt = _dot_3pass(a, w3_ref[...])
    window = jnp.exp(-z[:, 0:1] * dl_ref[...]) + HY_SHIFT
    rows = i * z.shape[0] + lax.broadcasted_iota(jnp.int32, filt.shape, 0)
    o_ref[...] = jnp.where(rows == 0, 0.0, filt * window)


def _hyena_taps(seq_len, w1, b1, w2, b2, w3, freq):
    L = seq_len
    t = np.linspace(0.0, 1.0, L)[:, None]
    bands = np.linspace(1e-4, HY_BANDS - 1, HY_BANDS)
    ang = 2.0 * math.pi * bands[None, :] * np.arange(L)[:, None] / L
    z = np.concatenate([t, np.cos(ang), -np.sin(ang)], axis=-1)
    offs = np.minimum(np.abs(np.arange(2 * L) - L), L - 1)
    z2_np = np.zeros((2 * L, LANES), np.float32)
    z2_np[:, :z.shape[1]] = z[offs]
    z2 = jnp.asarray(z2_np)
    pad_c = lambda a: jnp.zeros((a.shape[0], LANES), F32).at[:, :a.shape[1]].set(a)
    pad_r = lambda a: jnp.zeros((LANES, a.shape[1]), F32).at[:a.shape[0], :].set(a)
    w1p = pad_r(pad_c(w1))
    w2p = pad_r(pad_c(w2))
    w3p = pad_r(w3)
    b1p, b2p, frp = pad_c(b1[None, :]), pad_c(b2[None, :]), pad_c(freq[None, :])
    deltas = jnp.asarray(np.abs(np.linspace(math.log(HY_DECAY_TARGET) / HY_SLOW, math.log(HY_DECAY_TARGET) / HY_FAST,
                                            D_MODEL))[None, :].astype(np.float32))
    rb = min(1024, L)
    cbf = D_MODEL
    ncb = D_MODEL // cbf
    nrb_back = L // rb
    sq = pl.BlockSpec((LANES, LANES), lambda i, j: (0, 0))
    vec = pl.BlockSpec((1, LANES), lambda i, j: (0, 0))
    return pl.pallas_call(
        _filt_kernel,
        grid=(2 * L // rb, ncb),
        in_specs=[
            pl.BlockSpec((rb, LANES), lambda i, j: (i, 0)),
            sq, vec, sq, vec, vec,
            pl.BlockSpec((LANES, cbf), lambda i, j: (0, jnp.where(i < nrb_back, ncb + j, j))),
            pl.BlockSpec((1, cbf), lambda i, j: (0, j)),
        ],
        out_specs=pl.BlockSpec((rb, cbf), lambda i, j: (i, j)),
        out_shape=jax.ShapeDtypeStruct((2 * L, D_MODEL), F32),
        compiler_params=_cparams(("arbitrary", "arbitrary")),
        name="hyena_taps",
    )(z2, w1p, b1p, w2p, b2p, frp, w3p, deltas)


def _dft_mats(cb):
    n = 2 * cb
    m = np.arange(cb)
    f = np.arange(cb)
    ang = 2.0 * np.pi * ((f[:, None] * m[None, :]) % n) / n
    fwd = np.concatenate([np.cos(ang), -np.sin(ang)], axis=0)
    fwd[cb, :] = np.where(m % 2 == 0, 1.0, -1.0)
    coef = np.where(f == 0, 1.0, 2.0)[None, :] / n
    inv = np.concatenate([coef * np.cos(ang.T), -coef * np.sin(ang.T)], axis=1)
    inv[:, cb] = np.where(m % 2 == 0, 1.0, -1.0) / n
    return fwd.astype(np.float32), inv.astype(np.float32)


def _hconv_kernel(x0_ref, x1_ref, v_ref, cw0_ref, cw1_ref, cw2_ref, cb0_ref, cb1_ref, cb2_ref,
                  taps_ref, fb_ref, fwd_ref, inv_ref, o_ref,
                  hs_ref, stage_ref, w32_ref, w_ref, x0c_ref, u_ref, y_ref, *, seq_len, blk, n_seq):
    L = seq_len
    nb = L // blk
    cw = x0_ref.shape[1]
    RC = 256
    bi = pl.program_id(1)
    row0 = lax.broadcasted_iota(jnp.int32, (RC, cw), 0) == 0

    @pl.when(bi == 0)
    def _():
        rows = lax.broadcasted_iota(jnp.int32, (2 * blk, 1), 0)
        sign = jnp.where(rows % 2 == 0, 1.0, -1.0)
        real_row = rows <= blk
        prev = None
        for jb in range(2 * nb):
            cur = jnp.dot(fwd_ref[...], taps_ref[jb * blk:(jb + 1) * blk, :].astype(BF16),
                          preferred_element_type=F32)
            if prev is not None:
                first_tap = taps_ref[(jb - 1) * blk:(jb - 1) * blk + 1, :].astype(BF16).astype(F32)
                hs_ref[jb - 1] = cur + sign * (prev - jnp.where(real_row, first_tap, 0.0))
            prev = cur

    zeros8 = jnp.zeros((8, cw), F32)
    for s in range(n_seq):
        r0 = s * L

        def short_conv(k, src_ref, cw_ref, cb_ref, store):
            stage_ref[k, 0:8, :] = zeros8
            stage_ref[k, 8 + L:16 + L, :] = zeros8
            for c in range(L // RC):
                stage_ref[k, 8 + c * RC:8 + (c + 1) * RC, :] = \
                    src_ref[r0 + c * RC:r0 + (c + 1) * RC, :].astype(F32)
            w = cw_ref[...]
            for c in range(L // RC):
                lo = stage_ref[k, 7 + c * RC:7 + (c + 1) * RC, :]
                mid = stage_ref[k, 8 + c * RC:8 + (c + 1) * RC, :]
                hi = stage_ref[k, 9 + c * RC:9 + (c + 1) * RC, :]
                store(c, lo * w[0:1, :] + mid * w[1:2, :] + hi * w[2:3, :] + cb_ref[...])

        def st_x1(c, val):
            w32_ref[c * RC:(c + 1) * RC, :] = val

        def st_v(c, val):
            w_ref[c * RC:(c + 1) * RC, :] = (w32_ref[c * RC:(c + 1) * RC, :] * val).astype(BF16)

        def st_x0(c, val):
            x0c_ref[c * RC:(c + 1) * RC, :] = val.astype(BF16)

        short_conv(0, x1_ref, cw1_ref, cb1_ref, st_x1)
        short_conv(1, v_ref, cw2_ref, cb2_ref, st_v)
        short_conv(2, x0_ref, cw0_ref, cb0_ref, st_x0)

        for j in range(nb):
            u_ref[j] = jnp.dot(fwd_ref[...], w_ref[j * blk:(j + 1) * blk, :], preferred_element_type=F32)

        for i in range(nb):
            for c in range(blk // RC):
                re = None
                im = None
                for j in range(nb):
                    k = i - j + nb - 1
                    a = u_ref[j, c * RC:(c + 1) * RC, :]
                    b = u_ref[j, blk + c * RC:blk + (c + 1) * RC, :]
                    hr = hs_ref[k, c * RC:(c + 1) * RC, :]
                    hi = hs_ref[k, blk + c * RC:blk + (c + 1) * RC, :]
                    bb = b * hi
                    if c == 0:
                        t_re = a * hr - jnp.where(row0, 0.0, bb)
                        t_im = jnp.where(row0, bb, a * hi + b * hr)
                    else:
                        t_re = a * hr - bb
                        t_im = a * hi + b * hr
                    re = t_re if re is None else re + t_re
                    im = t_im if im is None else im + t_im
                y_ref[i, c * RC:(c + 1) * RC, :] = re.astype(BF16)
                y_ref[i, blk + c * RC:blk + (c + 1) * RC, :] = im.astype(BF16)
            conv = jnp.dot(inv_ref[...], y_ref[i], preferred_element_type=F32)
            sl = slice(i * blk, (i + 1) * blk)
            z = x0c_ref[sl, :].astype(F32) * (conv + w_ref[sl, :].astype(F32) * fb_ref[...])
            o_ref[r0 + i * blk:r0 + (i + 1) * blk, :] = z.astype(BF16)


def _hyena_conv(u, conv_w, conv_b, taps, filt_bias, batch, seq_len, blk, n_seq, row_off=0, cw=256):
    T = batch * seq_len
    L = seq_len
    nb = L // blk
    ncw = D_MODEL // cw
    fwd_np, inv_np = _dft_mats(blk)
    fwd = jnp.asarray(fwd_np).astype(BF16)
    inv = jnp.asarray(inv_np).astype(BF16)
    rows = n_seq * L
    assert row_off % rows == 0
    boff = row_off // rows
    sec = lambda s: pl.BlockSpec((rows, cw), lambda c, b: (b + boff, s * ncw + c))
    cws = lambda s: pl.BlockSpec((3, cw), lambda c, b: (0, s * ncw + c))
    cbs = lambda s: pl.BlockSpec((1, cw), lambda c, b: (0, s * ncw + c))
    return pl.pallas_call(
        functools.partial(_hconv_kernel, seq_len=L, blk=blk, n_seq=n_seq),
        grid=(ncw, batch // n_seq),
        in_specs=[sec(0), sec(1), sec(2), cws(0), cws(1), cws(2), cbs(0), cbs(1), cbs(2),
                  pl.BlockSpec((2 * L, cw), lambda c, b: (0, c), pipeline_mode=pl.Buffered(1)),
                  pl.BlockSpec((1, cw), lambda c, b: (0, c)),
                  pl.BlockSpec((2 * blk, blk), lambda c, b: (0, 0), pipeline_mode=pl.Buffered(1)),
                  pl.BlockSpec((blk, 2 * blk), lambda c, b: (0, 0), pipeline_mode=pl.Buffered(1))],
        out_specs=pl.BlockSpec((rows, cw), lambda c, b: (b, c)),
        out_shape=jax.ShapeDtypeStruct((T, D_MODEL), BF16),
        scratch_shapes=[
            pltpu.VMEM((2 * nb - 1, 2 * blk, cw), F32),
            pltpu.VMEM((3, L + 16, cw), F32),
            pltpu.VMEM((L, cw), F32),
            pltpu.VMEM((L, cw), BF16),
            pltpu.VMEM((L, cw), BF16),
            pltpu.VMEM((nb, 2 * blk, cw), F32),
            pltpu.VMEM((nb, 2 * blk, cw), BF16),
        ],
        compiler_params=_cparams(("arbitrary", "arbitrary")),
        name="hyena_conv",
    )(u, u, u, conv_w, conv_w, conv_w, conv_b.reshape(1, -1), conv_b.reshape(1, -1), conv_b.reshape(1, -1),
      taps, filt_bias.reshape(1, -1), fwd, inv)


def kernel(x_prompt, x_sample, cache_diff_k, cache_diff_v, state_ret_fwd, state_ret_bwd, c, c_ctx, ada_w, ada_b, ln_w, ln_b, ev_w_in, ev_w_out, ret_decay_fwd, ret_decay_bwd, ret_gn_w, diff_lambda, diff_subln_w, hy_w_in, hy_conv_w, hy_conv_b, hy_ffn_w1, hy_ffn_b1, hy_ffn_w2, hy_ffn_b2, hy_ffn_w3, hy_freq, hy_filter_bias, hy_w_out, router_w, router_bias, moe_w_gate, moe_w_up, moe_w_down):
    B, S, D = x_prompt.shape
    DB, DS, _ = x_sample.shape
    PAST = cache_diff_k.shape[2]
    TP, TS = B * S, DB * DS
    T_ALL = TP + TS
    assert D == D_MODEL and 1 + DB <= COND_ROWS and T_ALL < 65536

    cond8 = jnp.zeros((COND_ROWS, D), F32).at[0].set(c_ctx).at[1:1 + DB].set(c)
    mod = _ada_mod(cond8, ada_w, ada_b).reshape(DEPTH * COND_ROWS * N_MOD, 1, D)

    tm = 512
    groups = [
        dict(x=x_prompt.reshape(TP, D), x_off=0, batch=B, seq=S, off=0, cond_row=lambda i, tm: 0),
        dict(x=x_sample.reshape(TS, D), x_off=0, batch=DB, seq=DS, off=TP,
             cond_row=lambda i, tm: 1 + (i * tm) // DS),
    ]
    u_next = None

    def cond_row_all(i, tm):
        return jnp.where(i * tm < TP, 0, 1 + (i * tm - TP) // DS)

    rw_pad = jnp.zeros((D, LANES), F32).at[:, :N_EXPERTS].set(router_w.astype(F32))
    rw_hi = rw_pad.astype(BF16)
    rw_lo = jnp.concatenate([rw_hi, (rw_pad - rw_hi.astype(F32)).astype(BF16)], axis=1)
    rwt = router_w.T.astype(F32)
    rbias = router_bias.astype(F32)
    n_tiles = T_ALL // MOE_TM + N_CLASSES
    outs = {}

    for l in range(DEPTH):
        mixes = []
        if l % 2 == 0:
            e = l // 2
            w_in = ev_w_in[e]
            w_out = ev_w_out[e]
            lg = jnp.stack([jnp.log1p(-jnp.exp2(ret_decay_fwd[e].astype(F32))),
                            jnp.log1p(-jnp.exp2(ret_decay_bwd[e].astype(F32)))])
            lam_init = 0.8 - 0.6 * math.exp(-0.3 * l)
            lq1, lk1, lq2, lk2 = diff_lambda[e].astype(F32)
            lam = (jnp.exp(jnp.sum(lq1 * lk1)) - jnp.exp(jnp.sum(lq2 * lk2)) + lam_init).reshape(1)
            kscale = RET_DK ** -0.5
            for gi, g in enumerate(groups):
                tm_in = tm
                crow = functools.partial(g["cond_row"], tm=tm_in)
                if gi == 0:
                    secs = (("none", 1.0, False), ("none", kscale, False), ("none", 1.0, False),
                            ("none", 1.0, False), ("none", 1.0, False), ("none", 1.0, True),
                            ("none", 1.0, True))
                    proj, kd, vd = _in_proj(g["x"], mod, l, w_in, secs, crow, tm_in, tm=tm_in)
                    outs.setdefault("kd", []).append(kd.reshape(B, 1, S, H_DIFF, 2 * DIFF_QK))
                    outs.setdefault("vd", []).append(vd.reshape(B, 1, S, H_DIFF, DIFF_V))
                    ret, sf, sb = _retention(proj, lg, ret_gn_w[e], g["batch"], g["seq"], write_state=True,
                                             heads=H_RET)
                    outs.setdefault("sf", []).append(sf.reshape(B, 1, H_RET, RET_DK, RET_DV))
                    outs.setdefault("sb", []).append(sb.reshape(B, 1, H_RET, RET_DK, RET_DV))
                    att = _diff_attention(proj, lam, diff_subln_w[e], lam_init, g["batch"], g["seq"],
                                          tq=min(256, g["seq"]), heads=H_DIFF)
                else:
                    secs = (("ret", 1.0, False), ("ret", kscale, False), ("none", 1.0, False),
                            ("none", 1.0, False), ("diff", 1.0, False), ("diff", 1.0, False),
                            ("none", 1.0, False))
                    tabs = {"ret": _rope_tables(g["seq"], RET_DK), "diff": _rope_tables(g["seq"], DIFF_QK)}
                    (proj,) = _in_proj(g["x"], mod, l, w_in, secs, crow, g["seq"], rope_tabs=tabs, tm=tm_in)
                    ret, = _retention(proj, lg, ret_gn_w[e], g["batch"], g["seq"],
                                      states=(state_ret_fwd[:, e], state_ret_bwd[:, e]))
                    ctx = (cache_diff_k[:, e].reshape(DB, PAST, H_DIFF * 2 * DIFF_QK),
                           cache_diff_v[:, e].reshape(DB, PAST, H_DIFF * DIFF_V))
                    att = _diff_attention(proj, lam, diff_subln_w[e], lam_init, g["batch"], g["seq"], ctx=ctx,
                                          tq=1024, heads=1)
                mixes.append((ret, att))
        else:
            o = l // 2
            w_in = hy_w_in[o]
            w_out = hy_w_out[o]
            secs = (("none", 1.0, False),) * (3 * D // SEC)
            for gi, g in enumerate(groups):
                if u_next is None:
                    crow = functools.partial(g["cond_row"], tm=tm)
                    (u,) = _in_proj(g["x"], mod, l, w_in, secs, crow, tm, tm=tm)
                    u_off = 0
                else:
                    u, u_off = u_next, g["off"]
                taps = _hyena_taps(g["seq"], hy_ffn_w1[o], hy_ffn_b1[o], hy_ffn_w2[o], hy_ffn_b2[o],
                                   hy_ffn_w3[o], hy_freq[o])
                blk = min(g["seq"], 512)
                n_seq = max(1, 1024 // g["seq"])
                z = _hyena_conv(u, hy_conv_w[o], hy_conv_b[o], taps, hy_filter_bias[o], g["batch"], g["seq"],
                                blk, n_seq, row_off=u_off)
                mixes.append((z,))

        x1_all, h2_all, info_all, counts = _post_mix(mixes, [(g["x"], g["x_off"]) for g in groups], w_out, mod, l, ln_w[l, 0],
                                                     ln_b[l, 0], rw_hi, rw_lo, rbias, cond_row_all, tm)
        packed = info_all
        counts_i = counts[:, 0].astype(jnp.int32)
        plan = _moe_plan(counts_i, n_tiles)
        hs = _dispatch(h2_all, packed, plan, counts_i, n_tiles)
        ys = _moe(hs, plan, l, rwt, moe_w_gate, moe_w_up, moe_w_down)
        if l + 1 < DEPTH and (l + 1) % 2 == 1:
            x_all, u_next = _fin_proj(x1_all, ys, packed, plan[3], mod, l, ln_w[l, 1], ln_b[l, 1],
                                      hy_w_in[(l + 1) // 2], cond_row_all, tm=tm)
            for g in groups:
                g["x"], g["x_off"] = x_all, g["off"]
        else:
            u_next = None
            xs = _fin(x1_all, ys, packed, plan[3], mod, l, ln_w[l, 1], ln_b[l, 1], cond_row_all, (TP, TS))
            for g, x in zip(groups, xs):
                g["x"], g["x_off"] = x, 0

    y_prompt = groups[0]["x"].reshape(B, S, D)
    y_sample = groups[1]["x"].reshape(DB, DS, D)
    cat = lambda xs: xs[0] if len(xs) == 1 else jnp.concatenate(xs, axis=1)
    return (y_prompt, y_sample, cat(outs["kd"]), cat(outs["vd"]), cat(outs["sf"]), cat(outs["sb"]))
```

```python
import functools
import math

import numpy as np
import jax
import jax.numpy as jnp
from jax import lax
from jax.experimental import pallas as pl
from jax.experimental.pallas import tpu as pltpu

F32 = jnp.float32
BF16 = jnp.bfloat16

D_MODEL = 1024
DEPTH = 2
GRID_W = 64
H_RET = 4
RET_DK = 128
RET_DV = 128
RET_CHUNK = 256
H_DIFF = 4
DIFF_QK = 64
DIFF_V = 128
ROPE_BASE = 10000.0
HY_BANDS = 16
HY_FH = 64
HY_DECAY_TARGET = 1e-2
HY_FAST = 0.3
HY_SLOW = 1.5
HY_SHIFT = 0.05
N_EXPERTS = 16
N_GROUPS = 4
GROUP_SIZE = N_EXPERTS // N_GROUPS
D_FF_EXPERT = 512
ALPHA = (2 * DEPTH) ** 0.25
LN_EPS = 1e-5

LANES = 128
SEC = 512
COND_ROWS = 8
N_MOD = 6
VMEM_LIMIT = 50 * 1024 * 1024

PAIR_SLOTS = ((0, 1), (2, 1), (2, 3), (0, 3), (0, 2), (1, 3))
N_PAIRS = len(PAIR_SLOTS)
N_CLASSES = N_GROUPS * N_PAIRS
MOE_TM = 256
MOE_SHIFT = MOE_TM.bit_length() - 1
POST_SUB = 512
TOK_ROWS = D_MODEL // LANES


def _cparams(sem):
    return pltpu.CompilerParams(dimension_semantics=sem, vmem_limit_bytes=VMEM_LIMIT)


def _silu(x):
    return x * jax.nn.sigmoid(x)


def _store_token_tiles(ref, x):
    n = x.shape[0]
    for j in range(TOK_ROWS):
        ref[pl.ds(j, n, stride=TOK_ROWS), :] = x[:, j * LANES:(j + 1) * LANES]


def _load_token_tiles(ref, n, slot=None):
    idx = () if slot is None else (slot,)
    return jnp.concatenate([ref[idx + (pl.ds(j, n, stride=TOK_ROWS), slice(None))] for j in range(TOK_ROWS)],
                           axis=1)


def _layer_norm(x, w, b):
    mu = jnp.mean(x, axis=-1, keepdims=True)
    xc = x - mu
    var = jnp.mean(xc * xc, axis=-1, keepdims=True)
    return xc * lax.rsqrt(var + LN_EPS) * w + b


def _dot_3pass(a, w):
    a_hi = a.astype(BF16)
    a_lo = (a - a_hi.astype(F32)).astype(BF16)
    w_hi = w.astype(BF16)
    w_lo = (w - w_hi.astype(F32)).astype(BF16)
    return (jnp.dot(a_hi, w_hi, preferred_element_type=F32) + jnp.dot(a_lo, w_hi, preferred_element_type=F32)
            + jnp.dot(a_hi, w_lo, preferred_element_type=F32))


def _ada_kernel(c_ref, w_ref, b_ref, o_ref):
    o_ref[...] = _dot_3pass(_silu(c_ref[...]), w_ref[...]) + b_ref[...]


def _ada_mod(cond8, ada_w, ada_b):
    tn = 1024
    nj = ada_w.shape[2] // tn
    return pl.pallas_call(
        _ada_kernel,
        grid=(DEPTH, nj),
        in_specs=[
            pl.BlockSpec((COND_ROWS, D_MODEL), lambda l, j: (0, 0)),
            pl.BlockSpec((None, D_MODEL, tn), lambda l, j: (l, 0, j)),
            pl.BlockSpec((None, 1, tn), lambda l, j: (l, 0, j)),
        ],
        out_specs=pl.BlockSpec((None, COND_ROWS, tn), lambda l, j: (l, 0, j)),
        out_shape=jax.ShapeDtypeStruct((DEPTH, COND_ROWS, ada_w.shape[2]), F32),
        compiler_params=_cparams(("arbitrary", "arbitrary")),
        name="ada_mod",
    )(cond8, ada_w, ada_b.reshape(DEPTH, 1, -1))


def _mod_spec(layer, chunk, row_of_tile):
    def imap(i, *_):
        return ((layer * COND_ROWS + row_of_tile(i)) * N_MOD + chunk, 0, 0)
    return pl.BlockSpec((None, 1, D_MODEL), imap)


def _rope(a, tabs, quarter):
    c, sa, sb = tabs
    out = []
    for hb in range(a.shape[1] // LANES):
        blk = a[:, hb * LANES:(hb + 1) * LANES]
        up = pltpu.roll(blk, LANES - quarter, axis=1)
        dn = pltpu.roll(blk, quarter, axis=1)
        out.append(blk * c + up * sa + dn * sb)
    return jnp.concatenate(out, axis=1)


def _cast_once(w_ref, w_bf16_ref):
    @pl.when(pl.program_id(0) == 0)
    def _():
        for c in range(0, w_ref.shape[1], SEC):
            w_bf16_ref[:, c:c + SEC] = w_ref[:, c:c + SEC].astype(BF16)


def _in_kernel(*refs, secs, n_f32_out):
    x_ref, sh_ref, sc_ref, w_ref = refs[:4]
    pos = 4
    tabs = {}
    for kind in ("ret", "diff"):
        if any(s[0] == kind for s in secs):
            tabs[kind] = tuple(r[...] for r in refs[pos:pos + 3])
            pos += 3
    o_ref = refs[pos]
    f32_refs = refs[pos + 1:pos + 1 + n_f32_out]
    w_bf16_ref = refs[-1]
    _cast_once(w_ref, w_bf16_ref)
    h = (x_ref[...] * (1.0 + sc_ref[...]) + sh_ref[...]).astype(BF16)
    k32 = 0
    for s, (kind, scale, want_f32) in enumerate(secs):
        acc = jnp.dot(h, w_bf16_ref[:, s * SEC:(s + 1) * SEC], preferred_element_type=F32)
        if scale != 1.0:
            acc = acc * scale
        if kind == "ret":
            acc = _rope(acc, tabs["ret"], RET_DK // 4)
        elif kind == "diff":
            acc = _rope(acc, tabs["diff"], DIFF_QK // 4)
        o_ref[:, s * SEC:(s + 1) * SEC] = acc.astype(BF16)
        if want_f32:
            for hb in range(SEC // LANES):
                f32_refs[k32][:, hb, :] = acc[:, hb * LANES:(hb + 1) * LANES]
            k32 += 1
    assert k32 == n_f32_out


def _in_proj(x2d, mod, layer, w_f32, secs, row_of_tile, seq_len, rope_tabs=None, tm=512):
    T = x2d.shape[0]
    N = w_f32.shape[1]
    assert N == SEC * len(secs) and T % tm == 0 and seq_len % tm == 0
    tiles_per_seq = seq_len // tm
    in_specs = [
        pl.BlockSpec((tm, D_MODEL), lambda i: (i, 0)),
        _mod_spec(layer, 0, row_of_tile),
        _mod_spec(layer, 1, row_of_tile),
        pl.BlockSpec((D_MODEL, N), lambda i: (0, 0), pipeline_mode=pl.Buffered(1)),
    ]
    args = [x2d, mod, mod, w_f32]
    for kind in ("ret", "diff"):
        if any(s[0] == kind for s in secs):
            for t in rope_tabs[kind]:
                in_specs.append(pl.BlockSpec((tm, LANES), lambda i: (i % tiles_per_seq, 0)))
                args.append(t)
    n_f32 = sum(1 for s in secs if s[2])
    heads = SEC // LANES
    out_shape = [jax.ShapeDtypeStruct((T, N), BF16)] + [jax.ShapeDtypeStruct((T, heads, LANES), F32)] * n_f32
    out_specs = ([pl.BlockSpec((tm, N), lambda i: (i, 0))]
                 + [pl.BlockSpec((tm, heads, LANES), lambda i: (i, 0, 0))] * n_f32)
    return pl.pallas_call(
        functools.partial(_in_kernel, secs=secs, n_f32_out=n_f32),
        grid=(T // tm,),
        in_specs=in_specs,
        out_specs=out_specs,
        out_shape=out_shape,
        scratch_shapes=[pltpu.VMEM((D_MODEL, N), BF16)],
        compiler_params=_cparams(("arbitrary",)),
        name="in_proj",
    )(*args)


def _rope_tables(seq_len, d):
    half = d // 2
    quarter = half // 2
    t = np.arange(seq_len)
    inv = ROPE_BASE ** (-np.arange(quarter, dtype=np.float64) / quarter)
    ang_r = (t // GRID_W)[:, None] * inv[None, :]
    ang_c = (t % GRID_W)[:, None] * inv[None, :]
    zero = np.zeros_like(ang_r)
    cos = np.concatenate([np.cos(ang_r)] * 2 + [np.cos(ang_c)] * 2, axis=1)
    sa = np.concatenate([-np.sin(ang_r), zero, -np.sin(ang_c), zero], axis=1)
    sb = np.concatenate([zero, np.sin(ang_r), zero, np.sin(ang_c)], axis=1)
    reps = LANES // d
    return tuple(jnp.asarray(np.tile(a, (1, reps)).astype(np.float32)) for a in (cos, sa, sb))


def _ret_kernel(*refs, n_chunks, has_state, write_state, heads):
    lg_ref, q_ref, k_ref, v_ref, g_ref, gnw_ref = refs[:6]
    pos = 6
    if has_state:
        s0f_ref, s0b_ref = refs[pos:pos + 2]
        pos += 2
    o_ref = refs[pos]
    pos += 1
    if write_state:
        sf_ref, sb_ref = refs[pos:pos + 2]
        pos += 2
    acc_ref, kv_ref = refs[pos:pos + 2]
    for hh in range(heads):
        _ret_head(hh, heads, lg_ref, q_ref, k_ref, v_ref, g_ref, gnw_ref,
                  (s0f_ref, s0b_ref) if has_state else None, o_ref, (sf_ref, sb_ref) if write_state else None,
                  acc_ref, kv_ref, n_chunks)


def _ret_head(hh, heads, lg_ref, q_ref, k_ref, v_ref, g_ref, gnw_ref, s0_refs, o_ref, s_out_refs, acc_ref, kv_ref,
              n_chunks):
    C = q_ref.shape[0] // n_chunks
    cols = slice(hh * LANES, (hh + 1) * LANES)
    hd = pl.program_id(1) * heads + hh
    lgf = lg_ref[0, hd]
    lgb = lg_ref[1, hd]
    ii = lax.broadcasted_iota(jnp.int32, (C, C), 0).astype(F32)
    jj = lax.broadcasted_iota(jnp.int32, (C, C), 1).astype(F32)
    rel = ii - jj
    d_f = jnp.where(rel >= 0, jnp.exp(jnp.maximum(rel, 0.0) * lgf), 0.0)
    d_b = jnp.where(rel <= 0, jnp.exp(jnp.maximum(-rel, 0.0) * lgb), 0.0)
    d_sum = d_f + d_b
    idx = lax.broadcasted_iota(jnp.int32, (C, 1), 0).astype(F32)
    xi_f = jnp.exp((idx + 1.0) * lgf)
    zeta_f = jnp.exp((C - 1.0 - idx) * lgf)
    xi_b = jnp.exp((C - idx) * lgb)
    zeta_b = jnp.exp(idx * lgb)
    one = jnp.ones((1, 1), F32)
    gc_f = jnp.exp(one * (C * lgf))
    gc_b = jnp.exp(one * (C * lgb))

    nt = (((1,), (1,)), ((), ()))
    tn = (((0,), (0,)), ((), ()))

    if s0_refs is not None:
        s_f = s0_refs[0][hh]
        s_b = s0_refs[1][hh]
    else:
        s_f = jnp.zeros((RET_DK, RET_DV), F32)
        s_b = jnp.zeros((RET_DK, RET_DV), F32)

    for n in range(n_chunks):
        sl = slice(n * C, (n + 1) * C)
        qc, kc, vc = q_ref[sl, cols], k_ref[sl, cols], v_ref[sl, cols]
        scores = lax.dot_general(qc, kc, nt, preferred_element_type=F32) * d_sum
        acc_ref[sl, cols] = jnp.dot(scores.astype(BF16), vc, preferred_element_type=F32)
        kf = kc.astype(F32)
        kv_ref[0, hh, n] = lax.dot_general((kf * zeta_f).astype(BF16), vc, tn, preferred_element_type=F32)
        kv_ref[1, hh, n] = lax.dot_general((kf * zeta_b).astype(BF16), vc, tn, preferred_element_type=F32)

    for n in range(n_chunks):
        sl = slice(n * C, (n + 1) * C)
        qf = q_ref[sl, cols].astype(F32)
        acc_ref[sl, cols] += jnp.dot((qf * xi_f).astype(BF16), s_f.astype(BF16), preferred_element_type=F32)
        s_f = gc_f * s_f + kv_ref[0, hh, n]

    gnw = gnw_ref[:, cols]
    for n in reversed(range(n_chunks)):
        sl = slice(n * C, (n + 1) * C)
        qf = q_ref[sl, cols].astype(F32)
        cross = jnp.dot((qf * xi_b).astype(BF16), s_b.astype(BF16), preferred_element_type=F32)
        r = acc_ref[sl, cols] + cross
        mu = jnp.mean(r, axis=-1, keepdims=True)
        rc = r - mu
        var = jnp.mean(rc * rc, axis=-1, keepdims=True)
        rn = rc * lax.rsqrt(var + LN_EPS) * gnw
        o_ref[sl, cols] = (_silu(g_ref[sl, cols].astype(F32)) * rn).astype(BF16)
        s_b = gc_b * s_b + kv_ref[1, hh, n]

    if s_out_refs is not None:
        s_out_refs[0][hh] = s_f
        s_out_refs[1][hh] = s_b


def _retention(proj, lg, gn_w, batch, seq_len, states=None, write_state=False, heads=2):
    T = proj.shape[0]
    hp = heads
    width = hp * LANES
    per_sec = SEC // width
    blk = lambda sec: pl.BlockSpec((seq_len, width), lambda b, h: (b, sec * per_sec + h))
    in_specs = [pl.BlockSpec(memory_space=pltpu.SMEM), blk(0), blk(1), blk(2), blk(3),
                pl.BlockSpec((1, width), lambda b, h: (0, h))]
    args = [lg, proj, proj, proj, proj, gn_w.reshape(1, -1)]
    if states is not None:
        st = pl.BlockSpec((None, hp, RET_DK, RET_DV), lambda b, h: (b, h, 0, 0))
        in_specs += [st, st]
        args += list(states)
    n_chunks = max(1, seq_len // RET_CHUNK)
    out_shape = [jax.ShapeDtypeStruct((T, SEC), BF16)]
    out_specs = [pl.BlockSpec((seq_len, width), lambda b, h: (b, h))]
    if write_state:
        st_o = pl.BlockSpec((None, hp, RET_DK, RET_DV), lambda b, h: (b, h, 0, 0))
        out_shape += [jax.ShapeDtypeStruct((batch, H_RET, RET_DK, RET_DV), F32)] * 2
        out_specs += [st_o, st_o]
    return pl.pallas_call(
        functools.partial(_ret_kernel, n_chunks=n_chunks, has_state=states is not None,
                          write_state=write_state, heads=hp),
        grid=(batch, H_RET // hp),
        in_specs=in_specs,
        out_specs=out_specs,
        out_shape=out_shape,
        scratch_shapes=[pltpu.VMEM((seq_len, width), F32),
                        pltpu.VMEM((2, hp, n_chunks, RET_DK, RET_DV), F32)],
        compiler_params=_cparams(("arbitrary", "arbitrary")),
        name="retention",
    )(*args)


def _att_kernel(*refs, has_ctx, out_scale, key_chunk, heads):
    lam_ref, q_ref, k_ref, v_ref = refs[:4]
    pos = 4
    if has_ctx:
        ck_ref, cv_ref = refs[pos:pos + 2]
        pos += 2
    w_ref, o_ref = refs[pos:pos + 2]
    lam = lam_ref[0]
    tq = q_ref.shape[0]
    nt = (((1,), (1,)), ((), ()))
    for hh in range(heads):
        cols = slice(hh * LANES, (hh + 1) * LANES)
        q = q_ref[:, cols]
        lane = lax.broadcasted_iota(jnp.int32, q.shape, 1)
        zero = jnp.zeros_like(q)
        qq = jnp.concatenate([jnp.where(lane < DIFF_QK, q, zero), jnp.where(lane >= DIFF_QK, q, zero)], axis=0)
        qq = qq * jnp.asarray(DIFF_QK ** -0.5, BF16)
        chunks = [(k_ref, v_ref, c * key_chunk, key_chunk) for c in range(k_ref.shape[0] // key_chunk)]
        if has_ctx:
            chunks.append((ck_ref, cv_ref, 0, ck_ref.shape[0]))
        m = l = acc = None
        for kr, vr, off, n in chunks:
            kch = kr[off:off + n, cols].astype(BF16)
            vch = vr[off:off + n, cols].astype(BF16)
            s = lax.dot_general(qq, kch, nt, preferred_element_type=F32)
            cm = jnp.max(s, axis=-1, keepdims=True)
            m_new = cm if m is None else jnp.maximum(m, cm)
            p = jnp.exp(s - m_new)
            ps = jnp.sum(p, axis=-1, keepdims=True)
            pv = jnp.dot(p.astype(BF16), vch, preferred_element_type=F32)
            if m is None:
                l, acc = ps, pv
            else:
                alpha = jnp.exp(m - m_new)
                l = alpha * l + ps
                acc = alpha * acc + pv
            m = m_new
        o = acc / l
        att = o[:tq] - lam * o[tq:]
        att = att * lax.rsqrt(jnp.mean(att * att, axis=-1, keepdims=True) + LN_EPS)
        o_ref[:, cols] = (att * w_ref[...] * out_scale).astype(BF16)


def _diff_attention(proj, lam, subln_w, lam_init, batch, seq_len, ctx=None, tq=256, heads=1):
    T = proj.shape[0]
    width = heads * LANES
    per_sec = SEC // width
    nq = seq_len // tq
    in_specs = [
        pl.BlockSpec(memory_space=pltpu.SMEM),
        pl.BlockSpec((tq, width), lambda b, h, i: (b * nq + i, 4 * per_sec + h)),
        pl.BlockSpec((seq_len, width), lambda b, h, i: (b, 5 * per_sec + h)),
        pl.BlockSpec((seq_len, width), lambda b, h, i: (b, 6 * per_sec + h)),
    ]
    args = [lam, proj, proj, proj]
    if ctx is not None:
        ck, cv = ctx
        past = ck.shape[1]
        cspec = pl.BlockSpec((None, past, width), lambda b, h, i: (b, 0, h))
        in_specs += [cspec, cspec]
        args += [ck, cv]
    in_specs.append(pl.BlockSpec((1, LANES), lambda b, h, i: (0, 0)))
    args.append(subln_w.reshape(1, -1))
    return pl.pallas_call(
        functools.partial(_att_kernel, has_ctx=ctx is not None, out_scale=1.0 - lam_init,
                          key_chunk=min(512, seq_len), heads=heads),
        grid=(batch, H_DIFF // heads, nq),
        in_specs=in_specs,
        out_specs=pl.BlockSpec((tq, width), lambda b, h, i: (b * nq + i, h)),
        out_shape=jax.ShapeDtypeStruct((T, SEC), BF16),
        compiler_params=_cparams(("arbitrary", "arbitrary", "arbitrary")),
        name="diff_attention",
    )(*args)


def _route_class(lt, rb_ref):
    sel = [jax.nn.sigmoid(lt[e:e + 1, :]) + rb_ref[e] for e in range(N_EXPERTS)]
    gscore = []
    for g in range(N_GROUPS):
        mem = sel[g * GROUP_SIZE:(g + 1) * GROUP_SIZE]
        best = None
        for a in range(GROUP_SIZE):
            for b in range(a + 1, GROUP_SIZE):
                pair = mem[a] + mem[b]
                best = pair if best is None else jnp.maximum(best, pair)
        gscore.append(best)
    gbest = gscore[0]
    gidx = jnp.zeros_like(gbest)
    for g in range(1, N_GROUPS):
        upd = gscore[g] > gbest
        gidx = jnp.where(upd, float(g), gidx)
        gbest = jnp.where(upd, gscore[g], gbest)
    msel = []
    for j in range(GROUP_SIZE):
        out = sel[j]
        for g in range(1, N_GROUPS):
            out = jnp.where(gidx == float(g), sel[g * GROUP_SIZE + j], out)
        msel.append(out)
    one = jnp.ones_like(gbest)
    zero = jnp.zeros_like(gbest)
    chosen = []
    for j in range(GROUP_SIZE):
        rank = zero
        for k in range(GROUP_SIZE):
            if k < j:
                rank = rank + jnp.where(msel[k] >= msel[j], one, zero)
            elif k > j:
                rank = rank + jnp.where(msel[k] > msel[j], one, zero)
        chosen.append(jnp.where(rank < 2.0, one, zero))
    c0, c1, c2, c3 = chosen
    order = jnp.where(c0 * c1 > 0, 0.0, jnp.where(c1 * c2 > 0, 1.0, jnp.where(c2 * c3 > 0, 2.0,
            jnp.where(c0 * c3 > 0, 3.0, jnp.where(c0 * c2 > 0, 4.0, 5.0)))))
    return gidx * float(N_PAIRS) + order


def _post_kernel(*refs, n_mix, group_tiles):
    n_groups = len(group_tiles)
    per_group = n_mix + 1
    group_refs = [refs[g * per_group:(g + 1) * per_group] for g in range(n_groups)]
    refs = refs[n_groups * per_group:]
    (w_f32_ref, g_ref, sh_ref, sc_ref, lnw_ref, lnb_ref, rwh_ref, rwl_ref, rb_ref, tri_ref,
     x1_ref, h2_ref, info_ref, cout_ref, cnt_ref, pre_ref, w_ref) = refs
    _cast_once(w_f32_ref, w_ref)
    i = pl.program_id(0)

    @pl.when(i == 0)
    def _():
        cnt_ref[...] = jnp.zeros_like(cnt_ref)

    first = 0
    for g in range(n_groups):
        @pl.when(jnp.logical_and(i >= first, i < first + group_tiles[g]))
        def _(g=g):
            out = None
            off = 0
            for m_ref in group_refs[g][:n_mix]:
                width = m_ref.shape[1]
                part = jnp.dot(m_ref[...], w_ref[off:off + width, :], preferred_element_type=F32)
                out = part if out is None else out + part
                off += width
            pre_ref[...] = ALPHA * group_refs[g][n_mix][...] + g_ref[...] * out
        first += group_tiles[g]

    tm = pre_ref.shape[0]
    ts = tri_ref.shape[0]
    base = cnt_ref[:, 0:1]
    for part in range(tm // ts):
        rows = slice(part * ts, (part + 1) * ts)
        x1 = _layer_norm(pre_ref[rows, :], lnw_ref[...], lnb_ref[...])
        x1_ref[rows, :] = x1
        h2 = x1 * (1.0 + sc_ref[...]) + sh_ref[...]
        for j in range(TOK_ROWS):
            h2_ref[pl.ds(part * ts * TOK_ROWS + j, ts, stride=TOK_ROWS), :] = h2[:, j * LANES:(j + 1) * LANES]
        h_hi = h2.astype(BF16)
        h_lo = (h2 - h_hi.astype(F32)).astype(BF16)
        both = jnp.dot(h_hi, rwl_ref[...], preferred_element_type=F32)
        logits = (both[:, :LANES] + both[:, LANES:]
                  + jnp.dot(h_lo, rwh_ref[...], preferred_element_type=F32))
        cls = _route_class(logits.T, rb_ref)
        crow = lax.broadcasted_iota(jnp.int32, (32, ts), 0).astype(F32)
        onehot = jnp.where(crow == cls, 1.0, 0.0)
        prefix = jnp.dot(onehot.astype(BF16), tri_ref[...], preferred_element_type=F32)
        rank = jnp.sum(onehot * (prefix - 1.0 + base), axis=0, keepdims=True)
        base = base + jnp.sum(onehot, axis=1, keepdims=True)
        packed = cls.astype(jnp.int32) * 65536 + rank.astype(jnp.int32)
        info_ref[:, rows] = packed
    cnt_ref[...] = jnp.broadcast_to(base, cnt_ref.shape)

    @pl.when(i == pl.num_programs(0) - 1)
    def _():
        cout_ref[...] = cnt_ref[...]


def _tri(tm):
    return jnp.asarray(np.triu(np.ones((tm, tm), np.float32))).astype(BF16)


def _group_spec(width, tm, first_tile, n_tiles, array_tile_off=0):
    return pl.BlockSpec((tm, width),
                        lambda i, *_: (jnp.clip(i - first_tile, 0, n_tiles - 1) + array_tile_off, 0))


def _post_mix(group_mixes, group_x, w_out_bf16, mod, layer, ln_w, ln_b, rw_hi, rw_lo, router_bias, cond_row, tm):
    group_tiles = tuple(m[0].shape[0] // tm for m in group_mixes)
    n_tiles = sum(group_tiles)
    total_rows = n_tiles * tm
    row = pl.BlockSpec((tm, D_MODEL), lambda i: (i, 0))
    vec = pl.BlockSpec((1, D_MODEL), lambda i: (0, 0))
    cnt = pl.BlockSpec((32, LANES), lambda i: (0, 0))
    in_specs, args = [], []
    first = 0
    for mixes, (x, x_row_off), nt in zip(group_mixes, group_x, group_tiles):
        for m in mixes:
            in_specs.append(_group_spec(m.shape[1], tm, first, nt))
            args.append(m)
        in_specs.append(_group_spec(D_MODEL, tm, first, nt, x_row_off // tm))
        args.append(x)
        first += nt
    in_specs += [
        pl.BlockSpec((D_MODEL, D_MODEL), lambda i: (0, 0), pipeline_mode=pl.Buffered(1)),
        _mod_spec(layer, 2, lambda i: cond_row(i, tm)), _mod_spec(layer, 3, lambda i: cond_row(i, tm)),
        _mod_spec(layer, 4, lambda i: cond_row(i, tm)),
        vec, vec,
        pl.BlockSpec((D_MODEL, LANES), lambda i: (0, 0)),
        pl.BlockSpec((D_MODEL, 2 * LANES), lambda i: (0, 0)),
        pl.BlockSpec(memory_space=pltpu.SMEM),
        pl.BlockSpec((POST_SUB, POST_SUB), lambda i: (0, 0)),
    ]
    args += [w_out_bf16, mod, mod, mod, ln_w.reshape(1, -1), ln_b.reshape(1, -1), rw_hi, rw_lo, router_bias,
             _tri(POST_SUB)]
    return pl.pallas_call(
        functools.partial(_post_kernel, n_mix=len(group_mixes[0]), group_tiles=group_tiles),
        grid=(n_tiles,),
        in_specs=in_specs,
        out_specs=[row, pl.BlockSpec((tm * TOK_ROWS, LANES), lambda i: (i, 0)),
                   pl.BlockSpec((1, tm), lambda i: (0, i)), cnt],
        out_shape=[jax.ShapeDtypeStruct((total_rows, D_MODEL), F32),
                   jax.ShapeDtypeStruct((total_rows * TOK_ROWS, LANES), F32),
                   jax.ShapeDtypeStruct((1, total_rows), jnp.int32), jax.ShapeDtypeStruct((32, LANES), F32)],
        scratch_shapes=[pltpu.VMEM((32, LANES), F32), pltpu.VMEM((tm, D_MODEL), F32),
                        pltpu.VMEM((D_MODEL, D_MODEL), BF16)],
        compiler_params=_cparams(("arbitrary",)),
        name="post_mix",
    )(*args)


def _plan_kernel(cnt_ref, ea_ref, eb_ref, fl_ref, rs_ref, *, n_tiles):
    start = jnp.int32(0)
    prev_a = jnp.int32(-1)
    prev_b = jnp.int32(-1)
    par_a = jnp.int32(0)
    par_b = jnp.int32(0)
    for c in range(N_CLASSES):
        n = cnt_ref[c]
        tiles = lax.shift_right_logical(n + (MOE_TM - 1), MOE_SHIFT)
        row0 = start * MOE_TM
        rs_ref[c] = row0
        g, pr = divmod(c, N_PAIRS)
        a = g * GROUP_SIZE + PAIR_SLOTS[pr][0]
        b = g * GROUP_SIZE + PAIR_SLOTS[pr][1]
        new_a = (prev_a != a).astype(jnp.int32)
        new_b = (prev_b != b).astype(jnp.int32)
        has = tiles > 0
        par_a = jnp.where(has, par_a ^ new_a, par_a)
        par_b = jnp.where(has, par_b ^ new_b, par_b)
        first = 1 + 4 * new_a + 8 * new_b
        common = 16 * par_a + 32 * par_b

        def tile_body(k, _, start=start, a=a, b=b, first=first, n=n, common=common):
            t = start + k
            ea_ref[t] = a
            eb_ref[t] = b
            half = 2 * (n - k * MOE_TM <= MOE_TM // 2).astype(jnp.int32)
            fl_ref[t] = jnp.where(k == 0, first, 1) + half + common
            return 0

        lax.fori_loop(0, tiles, tile_body, 0)
        prev_a = jnp.where(has, a, prev_a)
        prev_b = jnp.where(has, b, prev_b)
        start = start + tiles
    for c in range(N_CLASSES, 31):
        rs_ref[c] = 0
    rs_ref[31] = start

    def idle_body(t, _):
        ea_ref[t] = prev_a
        eb_ref[t] = prev_b
        fl_ref[t] = 0
        return 0

    lax.fori_loop(start, n_tiles, idle_body, 0)


def _moe_plan(counts, n_tiles):
    smem = pl.BlockSpec(memory_space=pltpu.SMEM)
    i32 = lambda n: jax.ShapeDtypeStruct((n,), jnp.int32)
    return pl.pallas_call(
        functools.partial(_plan_kernel, n_tiles=n_tiles),
        in_specs=[smem],
        out_specs=[smem] * 4,
        out_shape=[i32(n_tiles), i32(n_tiles), i32(n_tiles), i32(32)],
        name="moe_plan",
    )(counts)


def _class_row(packed, rs_ref):
    return rs_ref[lax.shift_right_logical(packed, 16)] + (packed & 0xFFFF)


def _dispatch_kernel(packed_ref, rs_ref, cnt_ref, h_ref, hs_ref, sem, fill_sem, *, tm, n_tiles):
    i = pl.program_id(0)
    used = rs_ref[31]

    def tile_rows(row, n=1):
        return pl.ds(pl.multiple_of(row * TOK_ROWS, TOK_ROWS), n * TOK_ROWS)

    def fill_copy(row, n):
        return pltpu.make_async_copy(h_ref.at[tile_rows(0, n)], hs_ref.at[tile_rows(row, n)], fill_sem)

    def fill(act):
        for c in range(N_CLASSES):
            n = cnt_ref[c]
            row = rs_ref[c] + n
            pad = (-n) & (MOE_TM - 1)
            for bit in reversed(range(MOE_SHIFT)):
                size = 1 << bit

                @pl.when((pad & size) != 0)
                def _(row=row, size=size):
                    act(fill_copy(row, size))
                row = row + (pad & size)
        lax.fori_loop(used, n_tiles, lambda t, _: (act(fill_copy(t * MOE_TM, MOE_TM)), 0)[1], 0)

    @pl.when(i == 0)
    def _():
        fill(lambda cp: cp.start())

    def body(r8, _):
        for j in range(8):
            r = r8 * 8 + j
            row = _class_row(packed_ref[0, i * tm + r], rs_ref)
            pltpu.make_async_copy(h_ref.at[tile_rows(r)], hs_ref.at[tile_rows(row)], sem).start(priority=j % 2)
        return 0

    lax.fori_loop(0, tm // 8, body, 0)
    pltpu.make_async_copy(h_ref, hs_ref.at[tile_rows(0, tm)], sem).wait()

    @pl.when(i == 0)
    def _():
        fill(lambda cp: cp.wait())


def _dispatch(h2_tiles, packed, plan, counts, n_tiles, tm=2048):
    T = packed.shape[1]
    assert tm >= MOE_TM and T % tm == 0
    grid_spec = pltpu.PrefetchScalarGridSpec(
        num_scalar_prefetch=3,
        grid=(T // tm,),
        in_specs=[pl.BlockSpec((tm * TOK_ROWS, LANES), lambda i, *_: (i, 0))],
        out_specs=pl.BlockSpec(memory_space=pl.ANY),
        scratch_shapes=[pltpu.SemaphoreType.DMA(()), pltpu.SemaphoreType.DMA(())],
    )
    return pl.pallas_call(
        functools.partial(_dispatch_kernel, tm=tm, n_tiles=n_tiles),
        grid_spec=grid_spec,
        out_shape=jax.ShapeDtypeStruct((n_tiles * MOE_TM * TOK_ROWS, LANES), F32),
        compiler_params=_cparams(("arbitrary",)),
        name="moe_dispatch",
    )(packed, plan[3], counts, h2_tiles)


def _moe_kernel(ea_ref, eb_ref, fl_ref, h_ref, rwt_ref, wga, wua, wda, wgb, wub, wdb, o_ref,
                sga, sua, sda, sgb, sub, sdb):
    s = pl.program_id(0)
    n_tiles = pl.num_programs(0) - 1
    nxt = fl_ref[jnp.minimum(s, n_tiles - 1)]
    stage = s < n_tiles

    @pl.when(jnp.logical_and(stage, (nxt & 4) != 0))
    def _():
        p = (nxt >> 4) & 1
        sga[p] = wga[...].astype(BF16)
        sua[p] = wua[...].astype(BF16)
        sda[p] = wda[...].astype(BF16)

    @pl.when(jnp.logical_and(stage, (nxt & 8) != 0))
    def _():
        p = (nxt >> 5) & 1
        sgb[p] = wgb[...].astype(BF16)
        sub[p] = wub[...].astype(BF16)
        sdb[p] = wdb[...].astype(BF16)

    t = jnp.maximum(s - 1, 0)
    flags = fl_ref[t]
    run = s > 0
    valid = jnp.logical_and(run, (flags & 1) != 0)
    pa = (flags >> 4) & 1
    pb = (flags >> 5) & 1

    def experts(rows):
        h32 = _load_token_tiles(h_ref, rows)
        h = h32.astype(BF16)
        s_a = jax.nn.sigmoid(jnp.sum(h32 * rwt_ref[pl.ds(ea_ref[t], 1), :], axis=1, keepdims=True))
        s_b = jax.nn.sigmoid(jnp.sum(h32 * rwt_ref[pl.ds(eb_ref[t], 1), :], axis=1, keepdims=True))
        tot = s_a + s_b
        act_a = _silu(jnp.dot(h, sga[pa], preferred_element_type=F32)) \
            * jnp.dot(h, sua[pa], preferred_element_type=F32) * (s_a / tot)
        act_b = _silu(jnp.dot(h, sgb[pb], preferred_element_type=F32)) \
            * jnp.dot(h, sub[pb], preferred_element_type=F32) * (s_b / tot)
        _store_token_tiles(o_ref, jnp.dot(act_a.astype(BF16), sda[pa], preferred_element_type=F32)
                           + jnp.dot(act_b.astype(BF16), sdb[pb], preferred_element_type=F32))
        if rows < MOE_TM:
            o_ref[rows * TOK_ROWS:, :] = jnp.zeros(((MOE_TM - rows) * TOK_ROWS, LANES), F32)

    half = (flags & 2) != 0

    @pl.when(jnp.logical_and(valid, jnp.logical_not(half)))
    def _():
        experts(MOE_TM)

    @pl.when(jnp.logical_and(valid, half))
    def _():
        experts(MOE_TM // 2)

    @pl.when(jnp.logical_and(run, jnp.logical_not(valid)))
    def _():
        o_ref[...] = jnp.zeros_like(o_ref)


def _moe(hs, plan, layer, rwt, w_gate, w_up, w_down):
    ea, eb, flags, _ = plan
    n_tiles = ea.shape[0]
    ahead = lambda s: jnp.minimum(s, n_tiles - 1)
    wspec = lambda shape, which: pl.BlockSpec(
        (None, None) + shape, (lambda s, ea, eb, fl: (layer, ea[ahead(s)], 0, 0)) if which == 0
        else (lambda s, ea, eb, fl: (layer, eb[ahead(s)], 0, 0)))
    up = (D_MODEL, D_FF_EXPERT)
    dn = (D_FF_EXPERT, D_MODEL)
    tile = pl.BlockSpec((MOE_TM * TOK_ROWS, LANES), lambda s, ea, eb, fl: (jnp.maximum(s - 1, 0), 0))
    grid_spec = pltpu.PrefetchScalarGridSpec(
        num_scalar_prefetch=3,
        grid=(n_tiles + 1,),
        in_specs=[
            tile,
            pl.BlockSpec((N_EXPERTS, D_MODEL), lambda s, ea, eb, fl: (0, 0)),
            wspec(up, 0), wspec(up, 0), wspec(dn, 0), wspec(up, 1), wspec(up, 1), wspec(dn, 1),
        ],
        out_specs=tile,
        scratch_shapes=[pltpu.VMEM((2,) + up, BF16), pltpu.VMEM((2,) + up, BF16), pltpu.VMEM((2,) + dn, BF16),
                        pltpu.VMEM((2,) + up, BF16), pltpu.VMEM((2,) + up, BF16), pltpu.VMEM((2,) + dn, BF16)],
    )
    return pl.pallas_call(
        _moe_kernel,
        grid_spec=grid_spec,
        out_shape=jax.ShapeDtypeStruct((n_tiles * MOE_TM * TOK_ROWS, LANES), F32),
        compiler_params=_cparams(("arbitrary",)),
        name="moe",
    )(ea, eb, flags, hs, rwt, w_gate, w_up, w_down, w_gate, w_up, w_down)


def _gather_expert_rows(packed_ref, rs_ref, ys_ref, ybuf, sem, tm):
    i = pl.program_id(0)
    n_i = pl.num_programs(0)
    slot = i % 2

    def gather(tile, buf_slot):
        def body(r8, _):
            for j in range(8):
                r = r8 * 8 + j
                d = _class_row(packed_ref[0, tile * tm + r], rs_ref)
                pltpu.make_async_copy(ys_ref.at[pl.ds(pl.multiple_of(d * TOK_ROWS, TOK_ROWS), TOK_ROWS)],
                                      ybuf.at[buf_slot, pl.ds(pl.multiple_of(r * TOK_ROWS, TOK_ROWS), TOK_ROWS)],
                                      sem.at[buf_slot]).start(priority=j % 2)
            return 0
        lax.fori_loop(0, tm // 8, body, 0)

    @pl.when(i == 0)
    def _():
        gather(0, 0)

    @pl.when(i + 1 < n_i)
    def _():
        gather(jnp.minimum(i + 1, n_i - 1), 1 - slot)

    pltpu.make_async_copy(ys_ref.at[pl.ds(0, tm * TOK_ROWS)], ybuf.at[slot], sem.at[slot]).wait()
    return _load_token_tiles(ybuf, tm, slot)


def _fin_proj_kernel(packed_ref, rs_ref, x_ref, ys_ref, g_ref, lnw_ref, lnb_ref, sh_ref, sc_ref, w_f32_ref,
                     x2_ref, u_ref, ybuf, sem, w_ref, *, tm):
    _cast_once(w_f32_ref, w_ref)
    y = _gather_expert_rows(packed_ref, rs_ref, ys_ref, ybuf, sem, tm)
    x2 = _layer_norm(ALPHA * x_ref[...] + g_ref[...] * y, lnw_ref[...], lnb_ref[...])
    x2_ref[...] = x2
    h = (x2 * (1.0 + sc_ref[...]) + sh_ref[...]).astype(BF16)
    for s in range(u_ref.shape[1] // SEC):
        u_ref[:, s * SEC:(s + 1) * SEC] = jnp.dot(h, w_ref[:, s * SEC:(s + 1) * SEC],
                                                  preferred_element_type=F32).astype(BF16)


def _fin_proj(x1, ys, packed, row_start, mod, layer, ln_w, ln_b, w_next, cond_row, tm=512):
    T = x1.shape[0]
    N = w_next.shape[1]
    row = pl.BlockSpec((tm, D_MODEL), lambda i, *_: (i, 0))
    vec = pl.BlockSpec((1, D_MODEL), lambda i, *_: (0, 0))
    crow = lambda i: cond_row(i, tm)
    grid_spec = pltpu.PrefetchScalarGridSpec(
        num_scalar_prefetch=2,
        grid=(T // tm,),
        in_specs=[row, pl.BlockSpec(memory_space=pl.ANY), _mod_spec(layer, 5, crow), vec, vec,
                  _mod_spec(layer + 1, 0, crow), _mod_spec(layer + 1, 1, crow),
                  pl.BlockSpec((D_MODEL, N), lambda i, *_: (0, 0), pipeline_mode=pl.Buffered(1))],
        out_specs=[row, pl.BlockSpec((tm, N), lambda i, *_: (i, 0))],
        scratch_shapes=[pltpu.VMEM((2, tm * TOK_ROWS, LANES), F32), pltpu.SemaphoreType.DMA((2,)),
                        pltpu.VMEM((D_MODEL, N), BF16)],
    )
    return pl.pallas_call(
        functools.partial(_fin_proj_kernel, tm=tm),
        grid_spec=grid_spec,
        out_shape=[jax.ShapeDtypeStruct((T, D_MODEL), F32), jax.ShapeDtypeStruct((T, N), BF16)],
        compiler_params=_cparams(("arbitrary",)),
        name="post_moe_in_proj",
    )(packed, row_start, x1, ys, mod, ln_w.reshape(1, -1), ln_b.reshape(1, -1), mod, mod, w_next)


def _fin_kernel(packed_ref, rs_ref, x_ref, ys_ref, g_ref, lnw_ref, lnb_ref, *rest, tm, group_tiles):
    o_refs = rest[:len(group_tiles)]
    ybuf, sem = rest[len(group_tiles):]
    i = pl.program_id(0)
    y = _layer_norm(ALPHA * x_ref[...] + g_ref[...] * _gather_expert_rows(packed_ref, rs_ref, ys_ref, ybuf, sem, tm),
                    lnw_ref[...], lnb_ref[...])
    first = 0
    for o_ref, nt in zip(o_refs, group_tiles):
        @pl.when(jnp.logical_and(i >= first, i < first + nt))
        def _(o_ref=o_ref):
            o_ref[...] = y
        first += nt


def _fin(x1, ys, packed, row_start, mod, layer, ln_w, ln_b, cond_row, group_rows, tm=512):
    group_tiles = tuple(n // tm for n in group_rows)
    row = pl.BlockSpec((tm, D_MODEL), lambda i, *_: (i, 0))
    vec = pl.BlockSpec((1, D_MODEL), lambda i, *_: (0, 0))
    out_specs, first = [], 0
    for nt in group_tiles:
        out_specs.append(_group_spec(D_MODEL, tm, first, nt))
        first += nt
    grid_spec = pltpu.PrefetchScalarGridSpec(
        num_scalar_prefetch=2,
        grid=(sum(group_tiles),),
        in_specs=[row, pl.BlockSpec(memory_space=pl.ANY), _mod_spec(layer, 5, lambda i: cond_row(i, tm)), vec, vec],
        out_specs=out_specs,
        scratch_shapes=[pltpu.VMEM((2, tm * TOK_ROWS, LANES), F32), pltpu.SemaphoreType.DMA((2,))],
    )
    return pl.pallas_call(
        functools.partial(_fin_kernel, tm=tm, group_tiles=group_tiles),
        grid_spec=grid_spec,
        out_shape=[jax.ShapeDtypeStruct((n, D_MODEL), F32) for n in group_rows],
        compiler_params=_cparams(("arbitrary",)),
        name="post_moe",
    )(packed, row_start, x1, ys, mod, ln_w.reshape(1, -1), ln_b.reshape(1, -1))


def _filt_kernel(z_ref, w1_ref, b1_ref, w2_ref, b2_ref, fr_ref, w3_ref, dl_ref, o_ref):
    i = pl.program_id(0)
    z = z_ref[...]
    fr = fr_ref[...]
    a = jnp.sin(fr * (_dot_3pass(z, w1_ref[...]) + b1_ref[...]))
    a = jnp.sin(fr * (_dot_3pass(a, w2_ref[...]) + b2_ref[...]))
    filt = _dot_3pass(a, w3_ref[...])
    window = jnp.exp(-z[:, 0:1] * dl_ref[...]) + HY_SHIFT
    rows = i * z.shape[0] + lax.broadcasted_iota(jnp.int32, filt.shape, 0)
    o_ref[...] = jnp.where(rows == 0, 0.0, filt * window)


def _hyena_taps(seq_len, w1, b1, w2, b2, w3, freq):
    L = seq_len
    t = np.linspace(0.0, 1.0, L)[:, None]
    bands = np.linspace(1e-4, HY_BANDS - 1, HY_BANDS)
    ang = 2.0 * math.pi * bands[None, :] * np.arange(L)[:, None] / L
    z = np.concatenate([t, np.cos(ang), -np.sin(ang)], axis=-1)
    offs = np.minimum(np.abs(np.arange(2 * L) - L), L - 1)
    z2_np = np.zeros((2 * L, LANES), np.float32)
    z2_np[:, :z.shape[1]] = z[offs]
    z2 = jnp.asarray(z2_np)
    pad_c = lambda a: jnp.zeros((a.shape[0], LANES), F32).at[:, :a.shape[1]].set(a)
    pad_r = lambda a: jnp.zeros((LANES, a.shape[1]), F32).at[:a.shape[0], :].set(a)
    w1p = pad_r(pad_c(w1))
    w2p = pad_r(pad_c(w2))
    w3p = pad_r(w3)
    b1p, b2p, frp = pad_c(b1[None, :]), pad_c(b2[None, :]), pad_c(freq[None, :])
    deltas = jnp.asarray(np.abs(np.linspace(math.log(HY_DECAY_TARGET) / HY_SLOW, math.log(HY_DECAY_TARGET) / HY_FAST,
                                            D_MODEL))[None, :].astype(np.float32))
    rb = min(1024, L)
    cbf = D_MODEL
    ncb = D_MODEL // cbf
    nrb_back = L // rb
    sq = pl.BlockSpec((LANES, LANES), lambda i, j: (0, 0))
    vec = pl.BlockSpec((1, LANES), lambda i, j: (0, 0))
    return pl.pallas_call(
        _filt_kernel,
        grid=(2 * L // rb, ncb),
        in_specs=[
            pl.BlockSpec((rb, LANES), lambda i, j: (i, 0)),
            sq, vec, sq, vec, vec,
            pl.BlockSpec((LANES, cbf), lambda i, j: (0, jnp.where(i < nrb_back, ncb + j, j))),
            pl.BlockSpec((1, cbf), lambda i, j: (0, j)),
        ],
        out_specs=pl.BlockSpec((rb, cbf), lambda i, j: (i, j)),
        out_shape=jax.ShapeDtypeStruct((2 * L, D_MODEL), F32),
        compiler_params=_cparams(("arbitrary", "arbitrary")),
        name="hyena_taps",
    )(z2, w1p, b1p, w2p, b2p, frp, w3p, deltas)


def _dft_mats(cb):
    n = 2 * cb
    m = np.arange(cb)
    f = np.arange(cb)
    ang = 2.0 * np.pi * ((f[:, None] * m[None, :]) % n) / n
    fwd = np.concatenate([np.cos(ang), -np.sin(ang)], axis=0)
    fwd[cb, :] = np.where(m % 2 == 0, 1.0, -1.0)
    coef = np.where(f == 0, 1.0, 2.0)[None, :] / n
    inv = np.concatenate([coef * np.cos(ang.T), -coef * np.sin(ang.T)], axis=1)
    inv[:, cb] = np.where(m % 2 == 0, 1.0, -1.0) / n
    return fwd.astype(np.float32), inv.astype(np.float32)


def _hconv_kernel(x0_ref, x1_ref, v_ref, cw0_ref, cw1_ref, cw2_ref, cb0_ref, cb1_ref, cb2_ref,
                  taps_ref, fb_ref, fwd_ref, inv_ref, o_ref,
                  hs_ref, stage_ref, w32_ref, w_ref, x0c_ref, u_ref, y_ref, *, seq_len, blk, n_seq):
    L = seq_len
    nb = L // blk
    cw = x0_ref.shape[1]
    RC = 256
    bi = pl.program_id(1)
    row0 = lax.broadcasted_iota(jnp.int32, (RC, cw), 0) == 0

    @pl.when(bi == 0)
    def _():
        rows = lax.broadcasted_iota(jnp.int32, (2 * blk, 1), 0)
        sign = jnp.where(rows % 2 == 0, 1.0, -1.0)
        real_row = rows <= blk
        prev = None
        for jb in range(2 * nb):
            cur = jnp.dot(fwd_ref[...], taps_ref[jb * blk:(jb + 1) * blk, :].astype(BF16),
                          preferred_element_type=F32)
            if prev is not None:
                first_tap = taps_ref[(jb - 1) * blk:(jb - 1) * blk + 1, :].astype(BF16).astype(F32)
                hs_ref[jb - 1] = cur + sign * (prev - jnp.where(real_row, first_tap, 0.0))
            prev = cur

    zeros8 = jnp.zeros((8, cw), F32)
    for s in range(n_seq):
        r0 = s * L

        def short_conv(k, src_ref, cw_ref, cb_ref, store):
            stage_ref[k, 0:8, :] = zeros8
            stage_ref[k, 8 + L:16 + L, :] = zeros8
            for c in range(L // RC):
                stage_ref[k, 8 + c * RC:8 + (c + 1) * RC, :] = \
                    src_ref[r0 + c * RC:r0 + (c + 1) * RC, :].astype(F32)
            w = cw_ref[...]
            for c in range(L // RC):
                lo = stage_ref[k, 7 + c * RC:7 + (c + 1) * RC, :]
                mid = stage_ref[k, 8 + c * RC:8 + (c + 1) * RC, :]
                hi = stage_ref[k, 9 + c * RC:9 + (c + 1) * RC, :]
                store(c, lo * w[0:1, :] + mid * w[1:2, :] + hi * w[2:3, :] + cb_ref[...])

        def st_x1(c, val):
            w32_ref[c * RC:(c + 1) * RC, :] = val

        def st_v(c, val):
            w_ref[c * RC:(c + 1) * RC, :] = (w32_ref[c * RC:(c + 1) * RC, :] * val).astype(BF16)

        def st_x0(c, val):
            x0c_ref[c * RC:(c + 1) * RC, :] = val.astype(BF16)

        short_conv(0, x1_ref, cw1_ref, cb1_ref, st_x1)
        short_conv(1, v_ref, cw2_ref, cb2_ref, st_v)
        short_conv(2, x0_ref, cw0_ref, cb0_ref, st_x0)

        for j in range(nb):
            u_ref[j] = jnp.dot(fwd_ref[...], w_ref[j * blk:(j + 1) * blk, :], preferred_element_type=F32)

        for i in range(nb):
            for c in range(blk // RC):
                re = None
                im = None
                for j in range(nb):
                    k = i - j + nb - 1
                    a = u_ref[j, c * RC:(c + 1) * RC, :]
                    b = u_ref[j, blk + c * RC:blk + (c + 1) * RC, :]
                    hr = hs_ref[k, c * RC:(c + 1) * RC, :]
                    hi = hs_ref[k, blk + c * RC:blk + (c + 1) * RC, :]
                    bb = b * hi
                    if c == 0:
                        t_re = a * hr - jnp.where(row0, 0.0, bb)
                        t_im = jnp.where(row0, bb, a * hi + b * hr)
                    else:
                        t_re = a * hr - bb
                        t_im = a * hi + b * hr
                    re = t_re if re is None else re + t_re
                    im = t_im if im is None else im + t_im
                y_ref[i, c * RC:(c + 1) * RC, :] = re.astype(BF16)
                y_ref[i, blk + c * RC:blk + (c + 1) * RC, :] = im.astype(BF16)
            conv = jnp.dot(inv_ref[...], y_ref[i], preferred_element_type=F32)
            sl = slice(i * blk, (i + 1) * blk)
            z = x0c_ref[sl, :].astype(F32) * (conv + w_ref[sl, :].astype(F32) * fb_ref[...])
            o_ref[r0 + i * blk:r0 + (i + 1) * blk, :] = z.astype(BF16)


def _hyena_conv(u, conv_w, conv_b, taps, filt_bias, batch, seq_len, blk, n_seq, row_off=0, cw=256):
    T = batch * seq_len
    L = seq_len
    nb = L // blk
    ncw = D_MODEL // cw
    fwd_np, inv_np = _dft_mats(blk)
    fwd = jnp.asarray(fwd_np).astype(BF16)
    inv = jnp.asarray(inv_np).astype(BF16)
    rows = n_seq * L
    assert row_off % rows == 0
    boff = row_off // rows
    sec = lambda s: pl.BlockSpec((rows, cw), lambda c, b: (b + boff, s * ncw + c))
    cws = lambda s: pl.BlockSpec((3, cw), lambda c, b: (0, s * ncw + c))
    cbs = lambda s: pl.BlockSpec((1, cw), lambda c, b: (0, s * ncw + c))
    return pl.pallas_call(
        functools.partial(_hconv_kernel, seq_len=L, blk=blk, n_seq=n_seq),
        grid=(ncw, batch // n_seq),
        in_specs=[sec(0), sec(1), sec(2), cws(0), cws(1), cws(2), cbs(0), cbs(1), cbs(2),
                  pl.BlockSpec((2 * L, cw), lambda c, b: (0, c), pipeline_mode=pl.Buffered(1)),
                  pl.BlockSpec((1, cw), lambda c, b: (0, c)),
                  pl.BlockSpec((2 * blk, blk), lambda c, b: (0, 0), pipeline_mode=pl.Buffered(1)),
                  pl.BlockSpec((blk, 2 * blk), lambda c, b: (0, 0), pipeline_mode=pl.Buffered(1))],
        out_specs=pl.BlockSpec((rows, cw), lambda c, b: (b, c)),
        out_shape=jax.ShapeDtypeStruct((T, D_MODEL), BF16),
        scratch_shapes=[
            pltpu.VMEM((2 * nb - 1, 2 * blk, cw), F32),
            pltpu.VMEM((3, L + 16, cw), F32),
            pltpu.VMEM((L, cw), F32),
            pltpu.VMEM((L, cw), BF16),
            pltpu.VMEM((L, cw), BF16),
            pltpu.VMEM((nb, 2 * blk, cw), F32),
            pltpu.VMEM((nb, 2 * blk, cw), BF16),
        ],
        compiler_params=_cparams(("arbitrary", "arbitrary")),
        name="hyena_conv",
    )(u, u, u, conv_w, conv_w, conv_w, conv_b.reshape(1, -1), conv_b.reshape(1, -1), conv_b.reshape(1, -1),
      taps, filt_bias.reshape(1, -1), fwd, inv)


def kernel(x_prompt, x_sample, cache_diff_k, cache_diff_v, state_ret_fwd, state_ret_bwd, c, c_ctx, ada_w, ada_b, ln_w, ln_b, ev_w_in, ev_w_out, ret_decay_fwd, ret_decay_bwd, ret_gn_w, diff_lambda, diff_subln_w, hy_w_in, hy_conv_w, hy_conv_b, hy_ffn_w1, hy_ffn_b1, hy_ffn_w2, hy_ffn_b2, hy_ffn_w3, hy_freq, hy_filter_bias, hy_w_out, router_w, router_bias, moe_w_gate, moe_w_up, moe_w_down):
    B, S, D = x_prompt.shape
    DB, DS, _ = x_sample.shape
    PAST = cache_diff_k.shape[2]
    TP, TS = B * S, DB * DS
    T_ALL = TP + TS
    assert D == D_MODEL and 1 + DB <= COND_ROWS and T_ALL < 65536

    cond8 = jnp.zeros((COND_ROWS, D), F32).at[0].set(c_ctx).at[1:1 + DB].set(c)
    mod = _ada_mod(cond8, ada_w, ada_b).reshape(DEPTH * COND_ROWS * N_MOD, 1, D)

    tm = 512
    groups = [
        dict(x=x_prompt.reshape(TP, D), x_off=0, batch=B, seq=S, off=0, cond_row=lambda i, tm: 0),
        dict(x=x_sample.reshape(TS, D), x_off=0, batch=DB, seq=DS, off=TP,
             cond_row=lambda i, tm: 1 + (i * tm) // DS),
    ]
    u_next = None

    def cond_row_all(i, tm):
        return jnp.where(i * tm < TP, 0, 1 + (i * tm - TP) // DS)

    rw_pad = jnp.zeros((D, LANES), F32).at[:, :N_EXPERTS].set(router_w.astype(F32))
    rw_hi = rw_pad.astype(BF16)
    rw_lo = jnp.concatenate([rw_hi, (rw_pad - rw_hi.astype(F32)).astype(BF16)], axis=1)
    rwt = router_w.T.astype(F32)
    rbias = router_bias.astype(F32)
    n_tiles = T_ALL // MOE_TM + N_CLASSES
    outs = {}

    for l in range(DEPTH):
        mixes = []
        if l % 2 == 0:
            e = l // 2
            w_in = ev_w_in[e]
            w_out = ev_w_out[e]
            lg = jnp.stack([jnp.log1p(-jnp.exp2(ret_decay_fwd[e].astype(F32))),
                            jnp.log1p(-jnp.exp2(ret_decay_bwd[e].astype(F32)))])
            lam_init = 0.8 - 0.6 * math.exp(-0.3 * l)
            lq1, lk1, lq2, lk2 = diff_lambda[e].astype(F32)
            lam = (jnp.exp(jnp.sum(lq1 * lk1)) - jnp.exp(jnp.sum(lq2 * lk2)) + lam_init).reshape(1)
            kscale = RET_DK ** -0.5
            for gi, g in enumerate(groups):
                tm_in = tm
                crow = functools.partial(g["cond_row"], tm=tm_in)
                if gi == 0:
                    secs = (("none", 1.0, False), ("none", kscale, False), ("none", 1.0, False),
                            ("none", 1.0, False), ("none", 1.0, False), ("none", 1.0, True),
                            ("none", 1.0, True))
                    proj, kd, vd = _in_proj(g["x"], mod, l, w_in, secs, crow, tm_in, tm=tm_in)
                    outs.setdefault("kd", []).append(kd.reshape(B, 1, S, H_DIFF, 2 * DIFF_QK))
                    outs.setdefault("vd", []).append(vd.reshape(B, 1, S, H_DIFF, DIFF_V))
                    ret, sf, sb = _retention(proj, lg, ret_gn_w[e], g["batch"], g["seq"], write_state=True,
                                             heads=H_RET)
                    outs.setdefault("sf", []).append(sf.reshape(B, 1, H_RET, RET_DK, RET_DV))
                    outs.setdefault("sb", []).append(sb.reshape(B, 1, H_RET, RET_DK, RET_DV))
                    att = _diff_attention(proj, lam, diff_subln_w[e], lam_init, g["batch"], g["seq"],
                                          tq=min(256, g["seq"]), heads=H_DIFF)
                else:
                    secs = (("ret", 1.0, False), ("ret", kscale, False), ("none", 1.0, False),
                            ("none", 1.0, False), ("diff", 1.0, False), ("diff", 1.0, False),
                            ("none", 1.0, False))
                    tabs = {"ret": _rope_tables(g["seq"], RET_DK), "diff": _rope_tables(g["seq"], DIFF_QK)}
                    (proj,) = _in_proj(g["x"], mod, l, w_in, secs, crow, g["seq"], rope_tabs=tabs, tm=tm_in)
                    ret, = _retention(proj, lg, ret_gn_w[e], g["batch"], g["seq"],
                                      states=(state_ret_fwd[:, e], state_ret_bwd[:, e]))
                    ctx = (cache_diff_k[:, e].reshape(DB, PAST, H_DIFF * 2 * DIFF_QK),
                           cache_diff_v[:, e].reshape(DB, PAST, H_DIFF * DIFF_V))
                    att = _diff_attention(proj, lam, diff_subln_w[e], lam_init, g["batch"], g["seq"], ctx=ctx,
                                          tq=1024, heads=1)
                mixes.append((ret, att))
        else:
            o = l // 2
            w_in = hy_w_in[o]
            w_out = hy_w_out[o]
            secs = (("none", 1.0, False),) * (3 * D // SEC)
            for gi, g in enumerate(groups):
                if u_next is None:
                    crow = functools.partial(g["cond_row"], tm=tm)
                    (u,) = _in_proj(g["x"], mod, l, w_in, secs, crow, tm, tm=tm)
                    u_off = 0
                else:
                    u, u_off = u_next, g["off"]
                taps = _hyena_taps(g["seq"], hy_ffn_w1[o], hy_ffn_b1[o], hy_ffn_w2[o], hy_ffn_b2[o],
                                   hy_ffn_w3[o], hy_freq[o])
                blk = min(g["seq"], 512)
                n_seq = max(1, 1024 // g["seq"])
                z = _hyena_conv(u, hy_conv_w[o], hy_conv_b[o], taps, hy_filter_bias[o], g["batch"], g["seq"],
                                blk, n_seq, row_off=u_off)
                mixes.append((z,))

        x1_all, h2_all, info_all, counts = _post_mix(mixes, [(g["x"], g["x_off"]) for g in groups], w_out, mod, l, ln_w[l, 0],
                                                     ln_b[l, 0], rw_hi, rw_lo, rbias, cond_row_all, tm)
        packed = info_all
        counts_i = counts[:, 0].astype(jnp.int32)
        plan = _moe_plan(counts_i, n_tiles)
        hs = _dispatch(h2_all, packed, plan, counts_i, n_tiles)
        ys = _moe(hs, plan, l, rwt, moe_w_gate, moe_w_up, moe_w_down)
        if l + 1 < DEPTH and (l + 1) % 2 == 1:
            x_all, u_next = _fin_proj(x1_all, ys, packed, plan[3], mod, l, ln_w[l, 1], ln_b[l, 1],
                                      hy_w_in[(l + 1) // 2], cond_row_all, tm=tm)
            for g in groups:
                g["x"], g["x_off"] = x_all, g["off"]
        else:
            u_next = None
            xs = _fin(x1_all, ys, packed, plan[3], mod, l, ln_w[l, 1], ln_b[l, 1], cond_row_all, (TP, TS))
            for g, x in zip(groups, xs):
                g["x"], g["x_off"] = x, 0

    y_prompt = groups[0]["x"].reshape(B, S, D)
    y_sample = groups[1]["x"].reshape(DB, DS, D)
    cat = lambda xs: xs[0] if len(xs) == 1 else jnp.concatenate(xs, axis=1)
    return (y_prompt, y_sample, cat(outs["kd"]), cat(outs["vd"]), cat(outs["sf"]), cat(outs["sb"]))
```

```python
import functools
import math

import numpy as np
import jax
import jax.numpy as jnp
from jax import lax
from jax.experimental import pallas as pl
from jax.experimental.pallas import tpu as pltpu

F32 = jnp.float32
BF16 = jnp.bfloat16

D_MODEL = 1024
DEPTH = 2
GRID_W = 64
H_RET = 4
RET_DK = 128
RET_DV = 128
RET_CHUNK = 256
H_DIFF = 4
DIFF_QK = 64
DIFF_V = 128
ROPE_BASE = 10000.0
HY_BANDS = 16
HY_FH = 64
HY_DECAY_TARGET = 1e-2
HY_FAST = 0.3
HY_SLOW = 1.5
HY_SHIFT = 0.05
N_EXPERTS = 16
N_GROUPS = 4
GROUP_SIZE = N_EXPERTS // N_GROUPS
D_FF_EXPERT = 512
ALPHA = (2 * DEPTH) ** 0.25
LN_EPS = 1e-5

LANES = 128
SEC = 512
COND_ROWS = 8
N_MOD = 6
VMEM_LIMIT = 50 * 1024 * 1024

PAIR_SLOTS = ((0, 1), (2, 1), (2, 3), (0, 3), (0, 2), (1, 3))
N_PAIRS = len(PAIR_SLOTS)
N_CLASSES = N_GROUPS * N_PAIRS
MOE_TM = 256
MOE_SHIFT = MOE_TM.bit_length() - 1
POST_SUB = 512
TOK_ROWS = D_MODEL // LANES


def _cparams(sem):
    return pltpu.CompilerParams(dimension_semantics=sem, vmem_limit_bytes=VMEM_LIMIT)


def _silu(x):
    return x * jax.nn.sigmoid(x)


def _store_token_tiles(ref, x):
    n = x.shape[0]
    for j in range(TOK_ROWS):
        ref[pl.ds(j, n, stride=TOK_ROWS), :] = x[:, j * LANES:(j + 1) * LANES]


def _load_token_tiles(ref, n, slot=None):
    idx = () if slot is None else (slot,)
    return jnp.concatenate([ref[idx + (pl.ds(j, n, stride=TOK_ROWS), slice(None))] for j in range(TOK_ROWS)],
                           axis=1)


def _layer_norm(x, w, b):
    mu = jnp.mean(x, axis=-1, keepdims=True)
    xc = x - mu
    var = jnp.mean(xc * xc, axis=-1, keepdims=True)
    return xc * lax.rsqrt(var + LN_EPS) * w + b


def _dot_3pass(a, w):
    a_hi = a.astype(BF16)
    a_lo = (a - a_hi.astype(F32)).astype(BF16)
    w_hi = w.astype(BF16)
    w_lo = (w - w_hi.astype(F32)).astype(BF16)
    return (jnp.dot(a_hi, w_hi, preferred_element_type=F32) + jnp.dot(a_lo, w_hi, preferred_element_type=F32)
            + jnp.dot(a_hi, w_lo, preferred_element_type=F32))


def _ada_kernel(c_ref, w_ref, b_ref, o_ref):
    o_ref[...] = _dot_3pass(_silu(c_ref[...]), w_ref[...]) + b_ref[...]


def _ada_mod(cond8, ada_w, ada_b):
    tn = 1024
    nj = ada_w.shape[2] // tn
    return pl.pallas_call(
        _ada_kernel,
        grid=(DEPTH, nj),
        in_specs=[
            pl.BlockSpec((COND_ROWS, D_MODEL), lambda l, j: (0, 0)),
            pl.BlockSpec((None, D_MODEL, tn), lambda l, j: (l, 0, j)),
            pl.BlockSpec((None, 1, tn), lambda l, j: (l, 0, j)),
        ],
        out_specs=pl.BlockSpec((None, COND_ROWS, tn), lambda l, j: (l, 0, j)),
        out_shape=jax.ShapeDtypeStruct((DEPTH, COND_ROWS, ada_w.shape[2]), F32),
        compiler_params=_cparams(("arbitrary", "arbitrary")),
        name="ada_mod",
    )(cond8, ada_w, ada_b.reshape(DEPTH, 1, -1))


def _mod_spec(layer, chunk, row_of_tile):
    def imap(i, *_):
        return ((layer * COND_ROWS + row_of_tile(i)) * N_MOD + chunk, 0, 0)
    return pl.BlockSpec((None, 1, D_MODEL), imap)


def _rope(a, tabs, quarter):
    c, sa, sb = tabs
    out = []
    for hb in range(a.shape[1] // LANES):
        blk = a[:, hb * LANES:(hb + 1) * LANES]
        up = pltpu.roll(blk, LANES - quarter, axis=1)
        dn = pltpu.roll(blk, quarter, axis=1)
        out.append(blk * c + up * sa + dn * sb)
    return jnp.concatenate(out, axis=1)


def _cast_once(w_ref, w_bf16_ref):
    @pl.when(pl.program_id(0) == 0)
    def _():
        for c in range(0, w_ref.shape[1], SEC):
            w_bf16_ref[:, c:c + SEC] = w_ref[:, c:c + SEC].astype(BF16)


def _in_kernel(*refs, secs, n_f32_out):
    x_ref, sh_ref, sc_ref, w_ref = refs[:4]
    pos = 4
    tabs = {}
    for kind in ("ret", "diff"):
        if any(s[0] == kind for s in secs):
            tabs[kind] = tuple(r[...] for r in refs[pos:pos + 3])
            pos += 3
    o_ref = refs[pos]
    f32_refs = refs[pos + 1:pos + 1 + n_f32_out]
    w_bf16_ref = refs[-1]
    _cast_once(w_ref, w_bf16_ref)
    h = (x_ref[...] * (1.0 + sc_ref[...]) + sh_ref[...]).astype(BF16)
    k32 = 0
    for s, (kind, scale, want_f32) in enumerate(secs):
        acc = jnp.dot(h, w_bf16_ref[:, s * SEC:(s + 1) * SEC], preferred_element_type=F32)
        if scale != 1.0:
            acc = acc * scale
        if kind == "ret":
            acc = _rope(acc, tabs["ret"], RET_DK // 4)
        elif kind == "diff":
            acc = _rope(acc, tabs["diff"], DIFF_QK // 4)
        o_ref[:, s * SEC:(s + 1) * SEC] = acc.astype(BF16)
        if want_f32:
            for hb in range(SEC // LANES):
                f32_refs[k32][:, hb, :] = acc[:, hb * LANES:(hb + 1) * LANES]
            k32 += 1
    assert k32 == n_f32_out


def _in_proj(x2d, mod, layer, w_f32, secs, row_of_tile, seq_len, rope_tabs=None, tm=512):
    T = x2d.shape[0]
    N = w_f32.shape[1]
    assert N == SEC * len(secs) and T % tm == 0 and seq_len % tm == 0
    tiles_per_seq = seq_len // tm
    in_specs = [
        pl.BlockSpec((tm, D_MODEL), lambda i: (i, 0)),
        _mod_spec(layer, 0, row_of_tile),
        _mod_spec(layer, 1, row_of_tile),
        pl.BlockSpec((D_MODEL, N), lambda i: (0, 0), pipeline_mode=pl.Buffered(1)),
    ]
    args = [x2d, mod, mod, w_f32]
    for kind in ("ret", "diff"):
        if any(s[0] == kind for s in secs):
            for t in rope_tabs[kind]:
                in_specs.append(pl.BlockSpec((tm, LANES), lambda i: (i % tiles_per_seq, 0)))
                args.append(t)
    n_f32 = sum(1 for s in secs if s[2])
    heads = SEC // LANES
    out_shape = [jax.ShapeDtypeStruct((T, N), BF16)] + [jax.ShapeDtypeStruct((T, heads, LANES), F32)] * n_f32
    out_specs = ([pl.BlockSpec((tm, N), lambda i: (i, 0))]
                 + [pl.BlockSpec((tm, heads, LANES), lambda i: (i, 0, 0))] * n_f32)
    return pl.pallas_call(
        functools.partial(_in_kernel, secs=secs, n_f32_out=n_f32),
        grid=(T // tm,),
        in_specs=in_specs,
        out_specs=out_specs,
        out_shape=out_shape,
        scratch_shapes=[pltpu.VMEM((D_MODEL, N), BF16)],
        compiler_params=_cparams(("arbitrary",)),
        name="in_proj",
    )(*args)


def _rope_tables(seq_len, d):
    half = d // 2
    quarter = half // 2
    t = np.arange(seq_len)
    inv = ROPE_BASE ** (-np.arange(quarter, dtype=np.float64) / quarter)
    ang_r = (t // GRID_W)[:, None] * inv[None, :]
    ang_c = (t % GRID_W)[:, None] * inv[None, :]
    zero = np.zeros_like(ang_r)
    cos = np.concatenate([np.cos(ang_r)] * 2 + [np.cos(ang_c)] * 2, axis=1)
    sa = np.concatenate([-np.sin(ang_r), zero, -np.sin(ang_c), zero], axis=1)
    sb = np.concatenate([zero, np.sin(ang_r), zero, np.sin(ang_c)], axis=1)
    reps = LANES // d
    return tuple(jnp.asarray(np.tile(a, (1, reps)).astype(np.float32)) for a in (cos, sa, sb))


def _ret_kernel(*refs, n_chunks, has_state, write_state, heads):
    lg_ref, q_ref, k_ref, v_ref, g_ref, gnw_ref = refs[:6]
    pos = 6
    if has_state:
        s0f_ref, s0b_ref = refs[pos:pos + 2]
        pos += 2
    o_ref = refs[pos]
    pos += 1
    if write_state:
        sf_ref, sb_ref = refs[pos:pos + 2]
        pos += 2
    acc_ref, kv_ref = refs[pos:pos + 2]
    for hh in range(heads):
        _ret_head(hh, heads, lg_ref, q_ref, k_ref, v_ref, g_ref, gnw_ref,
                  (s0f_ref, s0b_ref) if has_state else None, o_ref, (sf_ref, sb_ref) if write_state else None,
                  acc_ref, kv_ref, n_chunks)


def _ret_head(hh, heads, lg_ref, q_ref, k_ref, v_ref, g_ref, gnw_ref, s0_refs, o_ref, s_out_refs, acc_ref, kv_ref,
              n_chunks):
    C = q_ref.shape[0] // n_chunks
    cols = slice(hh * LANES, (hh + 1) * LANES)
    hd = pl.program_id(1) * heads + hh
    lgf = lg_ref[0, hd]
    lgb = lg_ref[1, hd]
    ii = lax.broadcasted_iota(jnp.int32, (C, C), 0).astype(F32)
    jj = lax.broadcasted_iota(jnp.int32, (C, C), 1).astype(F32)
    rel = ii - jj
    d_f = jnp.where(rel >= 0, jnp.exp(jnp.maximum(rel, 0.0) * lgf), 0.0)
    d_b = jnp.where(rel <= 0, jnp.exp(jnp.maximum(-rel, 0.0) * lgb), 0.0)
    d_sum = d_f + d_b
    idx = lax.broadcasted_iota(jnp.int32, (C, 1), 0).astype(F32)
    xi_f = jnp.exp((idx + 1.0) * lgf)
    zeta_f = jnp.exp((C - 1.0 - idx) * lgf)
    xi_b = jnp.exp((C - idx) * lgb)
    zeta_b = jnp.exp(idx * lgb)
    one = jnp.ones((1, 1), F32)
    gc_f = jnp.exp(one * (C * lgf))
    gc_b = jnp.exp(one * (C * lgb))

    nt = (((1,), (1,)), ((), ()))
    tn = (((0,), (0,)), ((), ()))

    if s0_refs is not None:
        s_f = s0_refs[0][hh]
        s_b = s0_refs[1][hh]
    else:
        s_f = jnp.zeros((RET_DK, RET_DV), F32)
        s_b = jnp.zeros((RET_DK, RET_DV), F32)

    for n in range(n_chunks):
        sl = slice(n * C, (n + 1) * C)
        qc, kc, vc = q_ref[sl, cols], k_ref[sl, cols], v_ref[sl, cols]
        scores = lax.dot_general(qc, kc, nt, preferred_element_type=F32) * d_sum
        acc_ref[sl, cols] = jnp.dot(scores.astype(BF16), vc, preferred_element_type=F32)
        kf = kc.astype(F32)
        kv_ref[0, hh, n] = lax.dot_general((kf * zeta_f).astype(BF16), vc, tn, preferred_element_type=F32)
        kv_ref[1, hh, n] = lax.dot_general((kf * zeta_b).astype(BF16), vc, tn, preferred_element_type=F32)

    for n in range(n_chunks):
        sl = slice(n * C, (n + 1) * C)
        qf = q_ref[sl, cols].astype(F32)
        acc_ref[sl, cols] += jnp.dot((qf * xi_f).astype(BF16), s_f.astype(BF16), preferred_element_type=F32)
        s_f = gc_f * s_f + kv_ref[0, hh, n]

    gnw = gnw_ref[:, cols]
    for n in reversed(range(n_chunks)):
        sl = slice(n * C, (n + 1) * C)
        qf = q_ref[sl, cols].astype(F32)
        cross = jnp.dot((qf * xi_b).astype(BF16), s_b.astype(BF16), preferred_element_type=F32)
        r = acc_ref[sl, cols] + cross
        mu = jnp.mean(r, axis=-1, keepdims=True)
        rc = r - mu
        var = jnp.mean(rc * rc, axis=-1, keepdims=True)
        rn = rc * lax.rsqrt(var + LN_EPS) * gnw
        o_ref[sl, cols] = (_silu(g_ref[sl, cols].astype(F32)) * rn).astype(BF16)
        s_b = gc_b * s_b + kv_ref[1, hh, n]

    if s_out_refs is not None:
        s_out_refs[0][hh] = s_f
        s_out_refs[1][hh] = s_b


def _retention(proj, lg, gn_w, batch, seq_len, states=None, write_state=False, heads=2):
    T = proj.shape[0]
    hp = heads
    width = hp * LANES
    per_sec = SEC // width
    blk = lambda sec: pl.BlockSpec((seq_len, width), lambda b, h: (b, sec * per_sec + h))
    in_specs = [pl.BlockSpec(memory_space=pltpu.SMEM), blk(0), blk(1), blk(2), blk(3),
                pl.BlockSpec((1, width), lambda b, h: (0, h))]
    args = [lg, proj, proj, proj, proj, gn_w.reshape(1, -1)]
    if states is not None:
        st = pl.BlockSpec((None, hp, RET_DK, RET_DV), lambda b, h: (b, h, 0, 0))
        in_specs += [st, st]
        args += list(states)
    n_chunks = max(1, seq_len // RET_CHUNK)
    out_shape = [jax.ShapeDtypeStruct((T, SEC), BF16)]
    out_specs = [pl.BlockSpec((seq_len, width), lambda b, h: (b, h))]
    if write_state:
        st_o = pl.BlockSpec((None, hp, RET_DK, RET_DV), lambda b, h: (b, h, 0, 0))
        out_shape += [jax.ShapeDtypeStruct((batch, H_RET, RET_DK, RET_DV), F32)] * 2
        out_specs += [st_o, st_o]
    return pl.pallas_call(
        functools.partial(_ret_kernel, n_chunks=n_chunks, has_state=states is not None,
                          write_state=write_state, heads=hp),
        grid=(batch, H_RET // hp),
        in_specs=in_specs,
        out_specs=out_specs,
        out_shape=out_shape,
        scratch_shapes=[pltpu.VMEM((seq_len, width), F32),
                        pltpu.VMEM((2, hp, n_chunks, RET_DK, RET_DV), F32)],
        compiler_params=_cparams(("arbitrary", "arbitrary")),
        name="retention",
    )(*args)


def _att_kernel(*refs, has_ctx, out_scale, key_chunk, heads):
    lam_ref, q_ref, k_ref, v_ref = refs[:4]
    pos = 4
    if has_ctx:
        ck_ref, cv_ref = refs[pos:pos + 2]
        pos += 2
    w_ref, o_ref = refs[pos:pos + 2]
    lam = lam_ref[0]
    tq = q_ref.shape[0]
    nt = (((1,), (1,)), ((), ()))
    for hh in range(heads):
        cols = slice(hh * LANES, (hh + 1) * LANES)
        q = q_ref[:, cols]
        lane = lax.broadcasted_iota(jnp.int32, q.shape, 1)
        zero = jnp.zeros_like(q)
        qq = jnp.concatenate([jnp.where(lane < DIFF_QK, q, zero), jnp.where(lane >= DIFF_QK, q, zero)], axis=0)
        qq = qq * jnp.asarray(DIFF_QK ** -0.5, BF16)
        chunks = [(k_ref, v_ref, c * key_chunk, key_chunk) for c in range(k_ref.shape[0] // key_chunk)]
        if has_ctx:
            chunks.append((ck_ref, cv_ref, 0, ck_ref.shape[0]))
        m = l = acc = None
        for kr, vr, off, n in chunks:
            kch = kr[off:off + n, cols].astype(BF16)
            vch = vr[off:off + n, cols].astype(BF16)
            s = lax.dot_general(qq, kch, nt, preferred_element_type=F32)
            cm = jnp.max(s, axis=-1, keepdims=True)
            m_new = cm if m is None else jnp.maximum(m, cm)
            p = jnp.exp(s - m_new)
            ps = jnp.sum(p, axis=-1, keepdims=True)
            pv = jnp.dot(p.astype(BF16), vch, preferred_element_type=F32)
            if m is None:
                l, acc = ps, pv
            else:
                alpha = jnp.exp(m - m_new)
                l = alpha * l + ps
                acc = alpha * acc + pv
            m = m_new
        o = acc / l
        att = o[:tq] - lam * o[tq:]
        att = att * lax.rsqrt(jnp.mean(att * att, axis=-1, keepdims=True) + LN_EPS)
        o_ref[:, cols] = (att * w_ref[...] * out_scale).astype(BF16)


def _diff_attention(proj, lam, subln_w, lam_init, batch, seq_len, ctx=None, tq=256, heads=1):
    T = proj.shape[0]
    width = heads * LANES
    per_sec = SEC // width
    nq = seq_len // tq
    in_specs = [
        pl.BlockSpec(memory_space=pltpu.SMEM),
        pl.BlockSpec((tq, width), lambda b, h, i: (b * nq + i, 4 * per_sec + h)),
        pl.BlockSpec((seq_len, width), lambda b, h, i: (b, 5 * per_sec + h)),
        pl.BlockSpec((seq_len, width), lambda b, h, i: (b, 6 * per_sec + h)),
    ]
    args = [lam, proj, proj, proj]
    if ctx is not None:
        ck, cv = ctx
        past = ck.shape[1]
        cspec = pl.BlockSpec((None, past, width), lambda b, h, i: (b, 0, h))
        in_specs += [cspec, cspec]
        args += [ck, cv]
    in_specs.append(pl.BlockSpec((1, LANES), lambda b, h, i: (0, 0)))
    args.append(subln_w.reshape(1, -1))
    return pl.pallas_call(
        functools.partial(_att_kernel, has_ctx=ctx is not None, out_scale=1.0 - lam_init,
                          key_chunk=min(512, seq_len), heads=heads),
        grid=(batch, H_DIFF // heads, nq),
        in_specs=in_specs,
        out_specs=pl.BlockSpec((tq, width), lambda b, h, i: (b * nq + i, h)),
        out_shape=jax.ShapeDtypeStruct((T, SEC), BF16),
        compiler_params=_cparams(("arbitrary", "arbitrary", "arbitrary")),
        name="diff_attention",
    )(*args)


def _route_class(lt, rb_ref):
    sel = [jax.nn.sigmoid(lt[e:e + 1, :]) + rb_ref[e] for e in range(N_EXPERTS)]
    gscore = []
    for g in range(N_GROUPS):
        mem = sel[g * GROUP_SIZE:(g + 1) * GROUP_SIZE]
        best = None
        for a in range(GROUP_SIZE):
            for b in range(a + 1, GROUP_SIZE):
                pair = mem[a] + mem[b]
                best = pair if best is None else jnp.maximum(best, pair)
        gscore.append(best)
    gbest = gscore[0]
    gidx = jnp.zeros_like(gbest)
    for g in range(1, N_GROUPS):
        upd = gscore[g] > gbest
        gidx = jnp.where(upd, float(g), gidx)
        gbest = jnp.where(upd, gscore[g], gbest)
    msel = []
    for j in range(GROUP_SIZE):
        out = sel[j]
        for g in range(1, N_GROUPS):
            out = jnp.where(gidx == float(g), sel[g * GROUP_SIZE + j], out)
        msel.append(out)
    one = jnp.ones_like(gbest)
    zero = jnp.zeros_like(gbest)
    chosen = []
    for j in range(GROUP_SIZE):
        rank = zero
        for k in range(GROUP_SIZE):
            if k < j:
                rank = rank + jnp.where(msel[k] >= msel[j], one, zero)
            elif k > j:
                rank = rank + jnp.where(msel[k] > msel[j], one, zero)
        chosen.append(jnp.where(rank < 2.0, one, zero))
    c0, c1, c2, c3 = chosen
    order = jnp.where(c0 * c1 > 0, 0.0, jnp.where(c1 * c2 > 0, 1.0, jnp.where(c2 * c3 > 0, 2.0,
            jnp.where(c0 * c3 > 0, 3.0, jnp.where(c0 * c2 > 0, 4.0, 5.0)))))
    return gidx * float(N_PAIRS) + order


def _post_kernel(*refs, n_mix, group_tiles):
    n_groups = len(group_tiles)
    per_group = n_mix + 1
    group_refs = [refs[g * per_group:(g + 1) * per_group] for g in range(n_groups)]
    refs = refs[n_groups * per_group:]
    (w_f32_ref, g_ref, sh_ref, sc_ref, lnw_ref, lnb_ref, rwh_ref, rwl_ref, rb_ref, tri_ref,
     x1_ref, h2_ref, info_ref, cout_ref, cnt_ref, pre_ref, w_ref) = refs
    _cast_once(w_f32_ref, w_ref)
    i = pl.program_id(0)

    @pl.when(i == 0)
    def _():
        cnt_ref[...] = jnp.zeros_like(cnt_ref)

    first = 0
    for g in range(n_groups):
        @pl.when(jnp.logical_and(i >= first, i < first + group_tiles[g]))
        def _(g=g):
            out = None
            off = 0
            for m_ref in group_refs[g][:n_mix]:
                width = m_ref.shape[1]
                part = jnp.dot(m_ref[...], w_ref[off:off + width, :], preferred_element_type=F32)
                out = part if out is None else out + part
                off += width
            pre_ref[...] = ALPHA * group_refs[g][n_mix][...] + g_ref[...] * out
        first += group_tiles[g]

    tm = pre_ref.shape[0]
    ts = tri_ref.shape[0]
    base = cnt_ref[:, 0:1]
    for part in range(tm // ts):
        rows = slice(part * ts, (part + 1) * ts)
        x1 = _layer_norm(pre_ref[rows, :], lnw_ref[...], lnb_ref[...])
        x1_ref[rows, :] = x1
        h2 = x1 * (1.0 + sc_ref[...]) + sh_ref[...]
        for j in range(TOK_ROWS):
            h2_ref[pl.ds(part * ts * TOK_ROWS + j, ts, stride=TOK_ROWS), :] = h2[:, j * LANES:(j + 1) * LANES]
        h_hi = h2.astype(BF16)
        h_lo = (h2 - h_hi.astype(F32)).astype(BF16)
        both = jnp.dot(h_hi, rwl_ref[...], preferred_element_type=F32)
        logits = (both[:, :LANES] + both[:, LANES:]
                  + jnp.dot(h_lo, rwh_ref[...], preferred_element_type=F32))
        cls = _route_class(logits.T, rb_ref)
        crow = lax.broadcasted_iota(jnp.int32, (32, ts), 0).astype(F32)
        onehot = jnp.where(crow == cls, 1.0, 0.0)
        prefix = jnp.dot(onehot.astype(BF16), tri_ref[...], preferred_element_type=F32)
        rank = jnp.sum(onehot * (prefix - 1.0 + base), axis=0, keepdims=True)
        base = base + jnp.sum(onehot, axis=1, keepdims=True)
        packed = cls.astype(jnp.int32) * 65536 + rank.astype(jnp.int32)
        info_ref[:, rows] = packed
    cnt_ref[...] = jnp.broadcast_to(base, cnt_ref.shape)

    @pl.when(i == pl.num_programs(0) - 1)
    def _():
        cout_ref[...] = cnt_ref[...]


def _tri(tm):
    return jnp.asarray(np.triu(np.ones((tm, tm), np.float32))).astype(BF16)


def _group_spec(width, tm, first_tile, n_tiles, array_tile_off=0):
    return pl.BlockSpec((tm, width),
                        lambda i, *_: (jnp.clip(i - first_tile, 0, n_tiles - 1) + array_tile_off, 0))


def _post_mix(group_mixes, group_x, w_out_bf16, mod, layer, ln_w, ln_b, rw_hi, rw_lo, router_bias, cond_row, tm):
    group_tiles = tuple(m[0].shape[0] // tm for m in group_mixes)
    n_tiles = sum(group_tiles)
    total_rows = n_tiles * tm
    row = pl.BlockSpec((tm, D_MODEL), lambda i: (i, 0))
    vec = pl.BlockSpec((1, D_MODEL), lambda i: (0, 0))
    cnt = pl.BlockSpec((32, LANES), lambda i: (0, 0))
    in_specs, args = [], []
    first = 0
    for mixes, (x, x_row_off), nt in zip(group_mixes, group_x, group_tiles):
        for m in mixes:
            in_specs.append(_group_spec(m.shape[1], tm, first, nt))
            args.append(m)
        in_specs.append(_group_spec(D_MODEL, tm, first, nt, x_row_off // tm))
        args.append(x)
        first += nt
    in_specs += [
        pl.BlockSpec((D_MODEL, D_MODEL), lambda i: (0, 0), pipeline_mode=pl.Buffered(1)),
        _mod_spec(layer, 2, lambda i: cond_row(i, tm)), _mod_spec(layer, 3, lambda i: cond_row(i, tm)),
        _mod_spec(layer, 4, lambda i: cond_row(i, tm)),
        vec, vec,
        pl.BlockSpec((D_MODEL, LANES), lambda i: (0, 0)),
        pl.BlockSpec((D_MODEL, 2 * LANES), lambda i: (0, 0)),
        pl.BlockSpec(memory_space=pltpu.SMEM),
        pl.BlockSpec((POST_SUB, POST_SUB), lambda i: (0, 0)),
    ]
    args += [w_out_bf16, mod, mod, mod, ln_w.reshape(1, -1), ln_b.reshape(1, -1), rw_hi, rw_lo, router_bias,
             _tri(POST_SUB)]
    return pl.pallas_call(
        functools.partial(_post_kernel, n_mix=len(group_mixes[0]), group_tiles=group_tiles),
        grid=(n_tiles,),
        in_specs=in_specs,
        out_specs=[row, pl.BlockSpec((tm * TOK_ROWS, LANES), lambda i: (i, 0)),
                   pl.BlockSpec((1, tm), lambda i: (0, i)), cnt],
        out_shape=[jax.ShapeDtypeStruct((total_rows, D_MODEL), F32),
                   jax.ShapeDtypeStruct((total_rows * TOK_ROWS, LANES), F32),
                   jax.ShapeDtypeStruct((1, total_rows), jnp.int32), jax.ShapeDtypeStruct((32, LANES), F32)],
        scratch_shapes=[pltpu.VMEM((32, LANES), F32), pltpu.VMEM((tm, D_MODEL), F32),
                        pltpu.VMEM((D_MODEL, D_MODEL), BF16)],
        compiler_params=_cparams(("arbitrary",)),
        name="post_mix",
    )(*args)


def _plan_kernel(cnt_ref, ea_ref, eb_ref, fl_ref, rs_ref, *, n_tiles):
    start = jnp.int32(0)
    prev_a = jnp.int32(-1)
    prev_b = jnp.int32(-1)
    par_a = jnp.int32(0)
    par_b = jnp.int32(0)
    for c in range(N_CLASSES):
        n = cnt_ref[c]
        tiles = lax.shift_right_logical(n + (MOE_TM - 1), MOE_SHIFT)
        row0 = start * MOE_TM
        rs_ref[c] = row0
        g, pr = divmod(c, N_PAIRS)
        a = g * GROUP_SIZE + PAIR_SLOTS[pr][0]
        b = g * GROUP_SIZE + PAIR_SLOTS[pr][1]
        new_a = (prev_a != a).astype(jnp.int32)
        new_b = (prev_b != b).astype(jnp.int32)
        has = tiles > 0
        par_a = jnp.where(has, par_a ^ new_a, par_a)
        par_b = jnp.where(has, par_b ^ new_b, par_b)
        first = 1 + 4 * new_a + 8 * new_b
        common = 16 * par_a + 32 * par_b

        def tile_body(k, _, start=start, a=a, b=b, first=first, n=n, common=common):
            t = start + k
            ea_ref[t] = a
            eb_ref[t] = b
            real = jnp.minimum(n - k * MOE_TM, MOE_TM)
            quarters = lax.shift_right_logical(real + (MOE_TM // 4 - 1), MOE_SHIFT - 2)
            fl_ref[t] = jnp.where(k == 0, first, 1) + 64 * (quarters - 1) + common
            return 0

        lax.fori_loop(0, tiles, tile_body, 0)
        prev_a = jnp.where(has, a, prev_a)
        prev_b = jnp.where(has, b, prev_b)
        start = start + tiles
    for c in range(N_CLASSES, 31):
        rs_ref[c] = 0
    rs_ref[31] = start

    def idle_body(t, _):
        ea_ref[t] = prev_a
        eb_ref[t] = prev_b
        fl_ref[t] = 0
        return 0

    lax.fori_loop(start, n_tiles, idle_body, 0)


def _moe_plan(counts, n_tiles):
    smem = pl.BlockSpec(memory_space=pltpu.SMEM)
    i32 = lambda n: jax.ShapeDtypeStruct((n,), jnp.int32)
    return pl.pallas_call(
        functools.partial(_plan_kernel, n_tiles=n_tiles),
        in_specs=[smem],
        out_specs=[smem] * 4,
        out_shape=[i32(n_tiles), i32(n_tiles), i32(n_tiles), i32(32)],
        name="moe_plan",
    )(counts)


def _class_row(packed, rs_ref):
    return rs_ref[lax.shift_right_logical(packed, 16)] + (packed & 0xFFFF)


def _dispatch_kernel(packed_ref, rs_ref, cnt_ref, h_ref, hs_ref, sem, fill_sem, *, tm, n_tiles):
    i = pl.program_id(0)
    used = rs_ref[31]

    def tile_rows(row, n=1):
        return pl.ds(pl.multiple_of(row * TOK_ROWS, TOK_ROWS), n * TOK_ROWS)

    def fill_copy(row, n):
        return pltpu.make_async_copy(h_ref.at[tile_rows(0, n)], hs_ref.at[tile_rows(row, n)], fill_sem)

    def fill(act):
        for c in range(N_CLASSES):
            n = cnt_ref[c]
            row = rs_ref[c] + n
            pad = (-n) & (MOE_TM - 1)
            for bit in reversed(range(MOE_SHIFT)):
                size = 1 << bit

                @pl.when((pad & size) != 0)
                def _(row=row, size=size):
                    act(fill_copy(row, size))
                row = row + (pad & size)
        lax.fori_loop(used, n_tiles, lambda t, _: (act(fill_copy(t * MOE_TM, MOE_TM)), 0)[1], 0)

    @pl.when(i == 0)
    def _():
        fill(lambda cp: cp.start())

    def body(r8, _):
        for j in range(8):
            r = r8 * 8 + j
            row = _class_row(packed_ref[0, i * tm + r], rs_ref)
            pltpu.make_async_copy(h_ref.at[tile_rows(r)], hs_ref.at[tile_rows(row)], sem).start(priority=j % 2)
        return 0

    lax.fori_loop(0, tm // 8, body, 0)
    pltpu.make_async_copy(h_ref, hs_ref.at[tile_rows(0, tm)], sem).wait()

    @pl.when(i == 0)
    def _():
        fill(lambda cp: cp.wait())


def _dispatch(h2_tiles, packed, plan, counts, n_tiles, tm=2048):
    T = packed.shape[1]
    assert tm >= MOE_TM and T % tm == 0
    grid_spec = pltpu.PrefetchScalarGridSpec(
        num_scalar_prefetch=3,
        grid=(T // tm,),
        in_specs=[pl.BlockSpec((tm * TOK_ROWS, LANES), lambda i, *_: (i, 0))],
        out_specs=pl.BlockSpec(memory_space=pl.ANY),
        scratch_shapes=[pltpu.SemaphoreType.DMA(()), pltpu.SemaphoreType.DMA(())],
    )
    return pl.pallas_call(
        functools.partial(_dispatch_kernel, tm=tm, n_tiles=n_tiles),
        grid_spec=grid_spec,
        out_shape=jax.ShapeDtypeStruct((n_tiles * MOE_TM * TOK_ROWS, LANES), F32),
        compiler_params=_cparams(("arbitrary",)),
        name="moe_dispatch",
    )(packed, plan[3], counts, h2_tiles)


def _moe_kernel(ea_ref, eb_ref, fl_ref, h_ref, rwt_ref, wga, wua, wda, wgb, wub, wdb, o_ref,
                sga, sua, sda, sgb, sub, sdb):
    s = pl.program_id(0)
    n_tiles = pl.num_programs(0) - 1
    nxt = fl_ref[jnp.minimum(s, n_tiles - 1)]
    stage = s < n_tiles

    @pl.when(jnp.logical_and(stage, (nxt & 4) != 0))
    def _():
        p = (nxt >> 4) & 1
        sga[p] = wga[...].astype(BF16)
        sua[p] = wua[...].astype(BF16)
        sda[p] = wda[...].astype(BF16)

    @pl.when(jnp.logical_and(stage, (nxt & 8) != 0))
    def _():
        p = (nxt >> 5) & 1
        sgb[p] = wgb[...].astype(BF16)
        sub[p] = wub[...].astype(BF16)
        sdb[p] = wdb[...].astype(BF16)

    t = jnp.maximum(s - 1, 0)
    flags = fl_ref[t]
    run = s > 0
    valid = jnp.logical_and(run, (flags & 1) != 0)
    pa = (flags >> 4) & 1
    pb = (flags >> 5) & 1

    def experts(rows):
        h32 = _load_token_tiles(h_ref, rows)
        h = h32.astype(BF16)
        s_a = jax.nn.sigmoid(jnp.sum(h32 * rwt_ref[pl.ds(ea_ref[t], 1), :], axis=1, keepdims=True))
        s_b = jax.nn.sigmoid(jnp.sum(h32 * rwt_ref[pl.ds(eb_ref[t], 1), :], axis=1, keepdims=True))
        tot = s_a + s_b
        act_a = _silu(jnp.dot(h, sga[pa], preferred_element_type=F32)) \
            * jnp.dot(h, sua[pa], preferred_element_type=F32) * (s_a / tot)
        act_b = _silu(jnp.dot(h, sgb[pb], preferred_element_type=F32)) \
            * jnp.dot(h, sub[pb], preferred_element_type=F32) * (s_b / tot)
        _store_token_tiles(o_ref, jnp.dot(act_a.astype(BF16), sda[pa], preferred_element_type=F32)
                           + jnp.dot(act_b.astype(BF16), sdb[pb], preferred_element_type=F32))
        if rows < MOE_TM:
            o_ref[rows * TOK_ROWS:, :] = jnp.zeros(((MOE_TM - rows) * TOK_ROWS, LANES), F32)

    quarters = (flags >> 6) & 3
    for q in range(4):
        @pl.when(jnp.logical_and(valid, quarters == q))
        def _(q=q):
            experts((q + 1) * (MOE_TM // 4))

    @pl.when(jnp.logical_and(run, jnp.logical_not(valid)))
    def _():
        o_ref[...] = jnp.zeros_like(o_ref)


def _moe(hs, plan, layer, rwt, w_gate, w_up, w_down):
    ea, eb, flags, _ = plan
    n_tiles = ea.shape[0]
    ahead = lambda s: jnp.minimum(s, n_tiles - 1)
    wspec = lambda shape, which: pl.BlockSpec(
        (None, None) + shape, (lambda s, ea, eb, fl: (layer, ea[ahead(s)], 0, 0)) if which == 0
        else (lambda s, ea, eb, fl: (layer, eb[ahead(s)], 0, 0)))
    up = (D_MODEL, D_FF_EXPERT)
    dn = (D_FF_EXPERT, D_MODEL)
    tile = pl.BlockSpec((MOE_TM * TOK_ROWS, LANES), lambda s, ea, eb, fl: (jnp.maximum(s - 1, 0), 0))
    grid_spec = pltpu.PrefetchScalarGridSpec(
        num_scalar_prefetch=3,
        grid=(n_tiles + 1,),
        in_specs=[
            tile,
            pl.BlockSpec((N_EXPERTS, D_MODEL), lambda s, ea, eb, fl: (0, 0)),
            wspec(up, 0), wspec(up, 0), wspec(dn, 0), wspec(up, 1), wspec(up, 1), wspec(dn, 1),
        ],
        out_specs=tile,
        scratch_shapes=[pltpu.VMEM((2,) + up, BF16), pltpu.VMEM((2,) + up, BF16), pltpu.VMEM((2,) + dn, BF16),
                        pltpu.VMEM((2,) + up, BF16), pltpu.VMEM((2,) + up, BF16), pltpu.VMEM((2,) + dn, BF16)],
    )
    return pl.pallas_call(
        _moe_kernel,
        grid_spec=grid_spec,
        out_shape=jax.ShapeDtypeStruct((n_tiles * MOE_TM * TOK_ROWS, LANES), F32),
        compiler_params=_cparams(("arbitrary",)),
        name="moe",
    )(ea, eb, flags, hs, rwt, w_gate, w_up, w_down, w_gate, w_up, w_down)


def _gather_expert_rows(packed_ref, rs_ref, ys_ref, ybuf, sem, tm):
    i = pl.program_id(0)
    n_i = pl.num_programs(0)
    slot = i % 2

    def gather(tile, buf_slot):
        def body(r8, _):
            for j in range(8):
                r = r8 * 8 + j
                d = _class_row(packed_ref[0, tile * tm + r], rs_ref)
                pltpu.make_async_copy(ys_ref.at[pl.ds(pl.multiple_of(d * TOK_ROWS, TOK_ROWS), TOK_ROWS)],
                                      ybuf.at[buf_slot, pl.ds(pl.multiple_of(r * TOK_ROWS, TOK_ROWS), TOK_ROWS)],
                                      sem.at[buf_slot]).start(priority=j % 2)
            return 0
        lax.fori_loop(0, tm // 8, body, 0)

    @pl.when(i == 0)
    def _():
        gather(0, 0)

    @pl.when(i + 1 < n_i)
    def _():
        gather(jnp.minimum(i + 1, n_i - 1), 1 - slot)

    pltpu.make_async_copy(ys_ref.at[pl.ds(0, tm * TOK_ROWS)], ybuf.at[slot], sem.at[slot]).wait()
    return _load_token_tiles(ybuf, tm, slot)


def _fin_proj_kernel(packed_ref, rs_ref, x_ref, ys_ref, g_ref, lnw_ref, lnb_ref, sh_ref, sc_ref, w_f32_ref,
                     x2_ref, u_ref, ybuf, sem, w_ref, *, tm):
    _cast_once(w_f32_ref, w_ref)
    y = _gather_expert_rows(packed_ref, rs_ref, ys_ref, ybuf, sem, tm)
    x2 = _layer_norm(ALPHA * x_ref[...] + g_ref[...] * y, lnw_ref[...], lnb_ref[...])
    x2_ref[...] = x2
    h = (x2 * (1.0 + sc_ref[...]) + sh_ref[...]).astype(BF16)
    for s in range(u_ref.shape[1] // SEC):
        u_ref[:, s * SEC:(s + 1) * SEC] = jnp.dot(h, w_ref[:, s * SEC:(s + 1) * SEC],
                                                  preferred_element_type=F32).astype(BF16)


def _fin_proj(x1, ys, packed, row_start, mod, layer, ln_w, ln_b, w_next, cond_row, tm=512):
    T = x1.shape[0]
    N = w_next.shape[1]
    row = pl.BlockSpec((tm, D_MODEL), lambda i, *_: (i, 0))
    vec = pl.BlockSpec((1, D_MODEL), lambda i, *_: (0, 0))
    crow = lambda i: cond_row(i, tm)
    grid_spec = pltpu.PrefetchScalarGridSpec(
        num_scalar_prefetch=2,
        grid=(T // tm,),
        in_specs=[row, pl.BlockSpec(memory_space=pl.ANY), _mod_spec(layer, 5, crow), vec, vec,
                  _mod_spec(layer + 1, 0, crow), _mod_spec(layer + 1, 1, crow),
                  pl.BlockSpec((D_MODEL, N), lambda i, *_: (0, 0), pipeline_mode=pl.Buffered(1))],
        out_specs=[row, pl.BlockSpec((tm, N), lambda i, *_: (i, 0))],
        scratch_shapes=[pltpu.VMEM((2, tm * TOK_ROWS, LANES), F32), pltpu.SemaphoreType.DMA((2,)),
                        pltpu.VMEM((D_MODEL, N), BF16)],
    )
    return pl.pallas_call(
        functools.partial(_fin_proj_kernel, tm=tm),
        grid_spec=grid_spec,
        out_shape=[jax.ShapeDtypeStruct((T, D_MODEL), F32), jax.ShapeDtypeStruct((T, N), BF16)],
        compiler_params=_cparams(("arbitrary",)),
        name="post_moe_in_proj",
    )(packed, row_start, x1, ys, mod, ln_w.reshape(1, -1), ln_b.reshape(1, -1), mod, mod, w_next)


def _fin_kernel(packed_ref, rs_ref, x_ref, ys_ref, g_ref, lnw_ref, lnb_ref, *rest, tm, group_tiles):
    o_refs = rest[:len(group_tiles)]
    ybuf, sem = rest[len(group_tiles):]
    i = pl.program_id(0)
    y = _layer_norm(ALPHA * x_ref[...] + g_ref[...] * _gather_expert_rows(packed_ref, rs_ref, ys_ref, ybuf, sem, tm),
                    lnw_ref[...], lnb_ref[...])
    first = 0
    for o_ref, nt in zip(o_refs, group_tiles):
        @pl.when(jnp.logical_and(i >= first, i < first + nt))
        def _(o_ref=o_ref):
            o_ref[...] = y
        first += nt


def _fin(x1, ys, packed, row_start, mod, layer, ln_w, ln_b, cond_row, group_rows, tm=512):
    group_tiles = tuple(n // tm for n in group_rows)
    row = pl.BlockSpec((tm, D_MODEL), lambda i, *_: (i, 0))
    vec = pl.BlockSpec((1, D_MODEL), lambda i, *_: (0, 0))
    out_specs, first = [], 0
    for nt in group_tiles:
        out_specs.append(_group_spec(D_MODEL, tm, first, nt))
        first += nt
    grid_spec = pltpu.PrefetchScalarGridSpec(
        num_scalar_prefetch=2,
        grid=(sum(group_tiles),),
        in_specs=[row, pl.BlockSpec(memory_space=pl.ANY), _mod_spec(layer, 5, lambda i: cond_row(i, tm)), vec, vec],
        out_specs=out_specs,
        scratch_shapes=[pltpu.VMEM((2, tm * TOK_ROWS, LANES), F32), pltpu.SemaphoreType.DMA((2,))],
    )
    return pl.pallas_call(
        functools.partial(_fin_kernel, tm=tm, group_tiles=group_tiles),
        grid_spec=grid_spec,
        out_shape=[jax.ShapeDtypeStruct((n, D_MODEL), F32) for n in group_rows],
        compiler_params=_cparams(("arbitrary",)),
        name="post_moe",
    )(packed, row_start, x1, ys, mod, ln_w.reshape(1, -1), ln_b.reshape(1, -1))


def _filt_kernel(z_ref, w1_ref, b1_ref, w2_ref, b2_ref, fr_ref, w3_ref, dl_ref, o_ref):
    i = pl.program_id(0)
    z = z_ref[...]
    fr = fr_ref[...]
    a = jnp.sin(fr * (_dot_3pass(z, w1_ref[...]) + b1_ref[...]))
    a = jnp.sin(fr * (_dot_3pass(a, w2_ref[...]) + b2_ref[...]))
    filt = _dot_3pass(a, w3_ref[...])
    window = jnp.exp(-z[:, 0:1] * dl_ref[...]) + HY_SHIFT
    rows = i * z.shape[0] + lax.broadcasted_iota(jnp.int32, filt.shape, 0)
    o_ref[...] = jnp.where(rows == 0, 0.0, filt * window)


def _hyena_taps(seq_len, w1, b1, w2, b2, w3, freq):
    L = seq_len
    t = np.linspace(0.0, 1.0, L)[:, None]
    bands = np.linspace(1e-4, HY_BANDS - 1, HY_BANDS)
    ang = 2.0 * math.pi * bands[None, :] * np.arange(L)[:, None] / L
    z = np.concatenate([t, np.cos(ang), -np.sin(ang)], axis=-1)
    offs = np.minimum(np.abs(np.arange(2 * L) - L), L - 1)
    z2_np = np.zeros((2 * L, LANES), np.float32)
    z2_np[:, :z.shape[1]] = z[offs]
    z2 = jnp.asarray(z2_np)
    pad_c = lambda a: jnp.zeros((a.shape[0], LANES), F32).at[:, :a.shape[1]].set(a)
    pad_r = lambda a: jnp.zeros((LANES, a.shape[1]), F32).at[:a.shape[0], :].set(a)
    w1p = pad_r(pad_c(w1))
    w2p = pad_r(pad_c(w2))
    w3p = pad_r(w3)
    b1p, b2p, frp = pad_c(b1[None, :]), pad_c(b2[None, :]), pad_c(freq[None, :])
    deltas = jnp.asarray(np.abs(np.linspace(math.log(HY_DECAY_TARGET) / HY_SLOW, math.log(HY_DECAY_TARGET) / HY_FAST,
                                            D_MODEL))[None, :].astype(np.float32))
    rb = min(1024, L)
    cbf = D_MODEL
    ncb = D_MODEL // cbf
    nrb_back = L // rb
    sq = pl.BlockSpec((LANES, LANES), lambda i, j: (0, 0))
    vec = pl.BlockSpec((1, LANES), lambda i, j: (0, 0))
    return pl.pallas_call(
        _filt_kernel,
        grid=(2 * L // rb, ncb),
        in_specs=[
            pl.BlockSpec((rb, LANES), lambda i, j: (i, 0)),
            sq, vec, sq, vec, vec,
            pl.BlockSpec((LANES, cbf), lambda i, j: (0, jnp.where(i < nrb_back, ncb + j, j))),
            pl.BlockSpec((1, cbf), lambda i, j: (0, j)),
        ],
        out_specs=pl.BlockSpec((rb, cbf), lambda i, j: (i, j)),
        out_shape=jax.ShapeDtypeStruct((2 * L, D_MODEL), F32),
        compiler_params=_cparams(("arbitrary", "arbitrary")),
        name="hyena_taps",
    )(z2, w1p, b1p, w2p, b2p, frp, w3p, deltas)


def _dft_mats(cb):
    n = 2 * cb
    m = np.arange(cb)
    f = np.arange(cb)
    ang = 2.0 * np.pi * ((f[:, None] * m[None, :]) % n) / n
    fwd = np.concatenate([np.cos(ang), -np.sin(ang)], axis=0)
    fwd[cb, :] = np.where(m % 2 == 0, 1.0, -1.0)
    coef = np.where(f == 0, 1.0, 2.0)[None, :] / n
    inv = np.concatenate([coef * np.cos(ang.T), -coef * np.sin(ang.T)], axis=1)
    inv[:, cb] = np.where(m % 2 == 0, 1.0, -1.0) / n
    return fwd.astype(np.float32), inv.astype(np.float32)


def _hconv_kernel(x0_ref, x1_ref, v_ref, cw0_ref, cw1_ref, cw2_ref, cb0_ref, cb1_ref, cb2_ref,
                  taps_ref, fb_ref, fwd_ref, inv_ref, o_ref,
                  hs_ref, stage_ref, w32_ref, w_ref, x0c_ref, u_ref, y_ref, *, seq_len, blk, n_seq):
    L = seq_len
    nb = L // blk
    cw = x0_ref.shape[1]
    RC = 256
    bi = pl.program_id(1)
    row0 = lax.broadcasted_iota(jnp.int32, (RC, cw), 0) == 0

    @pl.when(bi == 0)
    def _():
        rows = lax.broadcasted_iota(jnp.int32, (2 * blk, 1), 0)
        sign = jnp.where(rows % 2 == 0, 1.0, -1.0)
        real_row = rows <= blk
        prev = None
        for jb in range(2 * nb):
            cur = jnp.dot(fwd_ref[...], taps_ref[jb * blk:(jb + 1) * blk, :].astype(BF16),
                          preferred_element_type=F32)
            if prev is not None:
                first_tap = taps_ref[(jb - 1) * blk:(jb - 1) * blk + 1, :].astype(BF16).astype(F32)
                hs_ref[jb - 1] = cur + sign * (prev - jnp.where(real_row, first_tap, 0.0))
            prev = cur

    zeros8 = jnp.zeros((8, cw), F32)
    for s in range(n_seq):
        r0 = s * L

        def short_conv(k, src_ref, cw_ref, cb_ref, store):
            stage_ref[k, 0:8, :] = zeros8
            stage_ref[k, 8 + L:16 + L, :] = zeros8
            for c in range(L // RC):
                stage_ref[k, 8 + c * RC:8 + (c + 1) * RC, :] = \
                    src_ref[r0 + c * RC:r0 + (c + 1) * RC, :].astype(F32)
            w = cw_ref[...]
            for c in range(L // RC):
                lo = stage_ref[k, 7 + c * RC:7 + (c + 1) * RC, :]
                mid = stage_ref[k, 8 + c * RC:8 + (c + 1) * RC, :]
                hi = stage_ref[k, 9 + c * RC:9 + (c + 1) * RC, :]
                store(c, lo * w[0:1, :] + mid * w[1:2, :] + hi * w[2:3, :] + cb_ref[...])

        def st_x1(c, val):
            w32_ref[c * RC:(c + 1) * RC, :] = val

        def st_v(c, val):
            w_ref[c * RC:(c + 1) * RC, :] = (w32_ref[c * RC:(c + 1) * RC, :] * val).astype(BF16)

        def st_x0(c, val):
            x0c_ref[c * RC:(c + 1) * RC, :] = val.astype(BF16)

        short_conv(0, x1_ref, cw1_ref, cb1_ref, st_x1)
        short_conv(1, v_ref, cw2_ref, cb2_ref, st_v)
        short_conv(2, x0_ref, cw0_ref, cb0_ref, st_x0)

        for j in range(nb):
            u_ref[j] = jnp.dot(fwd_ref[...], w_ref[j * blk:(j + 1) * blk, :], preferred_element_type=F32)

        for i in range(nb):
            for c in range(blk // RC):
                re = None
                im = None
                for j in range(nb):
                    k = i - j + nb - 1
                    a = u_ref[j, c * RC:(c + 1) * RC, :]
                    b = u_ref[j, blk + c * RC:blk + (c + 1) * RC, :]
                    hr = hs_ref[k, c * RC:(c + 1) * RC, :]
                    hi = hs_ref[k, blk + c * RC:blk + (c + 1) * RC, :]
                    bb = b * hi
                    if c == 0:
                        t_re = a * hr - jnp.where(row0, 0.0, bb)
                        t_im = jnp.where(row0, bb, a * hi + b * hr)
                    else:
                        t_re = a * hr - bb
                        t_im = a * hi + b * hr
                    re = t_re if re is None else re + t_re
                    im = t_im if im is None else im + t_im
                y_ref[i, c * RC:(c + 1) * RC, :] = re.astype(BF16)
                y_ref[i, blk + c * RC:blk + (c + 1) * RC, :] = im.astype(BF16)
            conv = jnp.dot(inv_ref[...], y_ref[i], preferred_element_type=F32)
            sl = slice(i * blk, (i + 1) * blk)
            z = x0c_ref[sl, :].astype(F32) * (conv + w_ref[sl, :].astype(F32) * fb_ref[...])
            o_ref[r0 + i * blk:r0 + (i + 1) * blk, :] = z.astype(BF16)


def _hyena_conv(u, conv_w, conv_b, taps, filt_bias, batch, seq_len, blk, n_seq, row_off=0, cw=256):
    T = batch * seq_len
    L = seq_len
    nb = L // blk
    ncw = D_MODEL // cw
    fwd_np, inv_np = _dft_mats(blk)
    fwd = jnp.asarray(fwd_np).astype(BF16)
    inv = jnp.asarray(inv_np).astype(BF16)
    rows = n_seq * L
    assert row_off % rows == 0
    boff = row_off // rows
    sec = lambda s: pl.BlockSpec((rows, cw), lambda c, b: (b + boff, s * ncw + c))
    cws = lambda s: pl.BlockSpec((3, cw), lambda c, b: (0, s * ncw + c))
    cbs = lambda s: pl.BlockSpec((1, cw), lambda c, b: (0, s * ncw + c))
    return pl.pallas_call(
        functools.partial(_hconv_kernel, seq_len=L, blk=blk, n_seq=n_seq),
        grid=(ncw, batch // n_seq),
        in_specs=[sec(0), sec(1), sec(2), cws(0), cws(1), cws(2), cbs(0), cbs(1), cbs(2),
                  pl.BlockSpec((2 * L, cw), lambda c, b: (0, c), pipeline_mode=pl.Buffered(1)),
                  pl.BlockSpec((1, cw), lambda c, b: (0, c)),
                  pl.BlockSpec((2 * blk, blk), lambda c, b: (0, 0), pipeline_mode=pl.Buffered(1)),
                  pl.BlockSpec((blk, 2 * blk), lambda c, b: (0, 0), pipeline_mode=pl.Buffered(1))],
        out_specs=pl.BlockSpec((rows, cw), lambda c, b: (b, c)),
        out_shape=jax.ShapeDtypeStruct((T, D_MODEL), BF16),
        scratch_shapes=[
            pltpu.VMEM((2 * nb - 1, 2 * blk, cw), F32),
            pltpu.VMEM((3, L + 16, cw), F32),
            pltpu.VMEM((L, cw), F32),
            pltpu.VMEM((L, cw), BF16),
            pltpu.VMEM((L, cw), BF16),
            pltpu.VMEM((nb, 2 * blk, cw), F32),
            pltpu.VMEM((nb, 2 * blk, cw), BF16),
        ],
        compiler_params=_cparams(("arbitrary", "arbitrary")),
        name="hyena_conv",
    )(u, u, u, conv_w, conv_w, conv_w, conv_b.reshape(1, -1), conv_b.reshape(1, -1), conv_b.reshape(1, -1),
      taps, filt_bias.reshape(1, -1), fwd, inv)


def kernel(x_prompt, x_sample, cache_diff_k, cache_diff_v, state_ret_fwd, state_ret_bwd, c, c_ctx, ada_w, ada_b, ln_w, ln_b, ev_w_in, ev_w_out, ret_decay_fwd, ret_decay_bwd, ret_gn_w, diff_lambda, diff_subln_w, hy_w_in, hy_conv_w, hy_conv_b, hy_ffn_w1, hy_ffn_b1, hy_ffn_w2, hy_ffn_b2, hy_ffn_w3, hy_freq, hy_filter_bias, hy_w_out, router_w, router_bias, moe_w_gate, moe_w_up, moe_w_down):
    B, S, D = x_prompt.shape
    DB, DS, _ = x_sample.shape
    PAST = cache_diff_k.shape[2]
    TP, TS = B * S, DB * DS
    T_ALL = TP + TS
    assert D == D_MODEL and 1 + DB <= COND_ROWS and T_ALL < 65536

    cond8 = jnp.zeros((COND_ROWS, D), F32).at[0].set(c_ctx).at[1:1 + DB].set(c)
    mod = _ada_mod(cond8, ada_w, ada_b).reshape(DEPTH * COND_ROWS * N_MOD, 1, D)

    tm = 512
    groups = [
        dict(x=x_prompt.reshape(TP, D), x_off=0, batch=B, seq=S, off=0, cond_row=lambda i, tm: 0),
        dict(x=x_sample.reshape(TS, D), x_off=0, batch=DB, seq=DS, off=TP,
             cond_row=lambda i, tm: 1 + (i * tm) // DS),
    ]
    u_next = None

    def cond_row_all(i, tm):
        return jnp.where(i * tm < TP, 0, 1 + (i * tm - TP) // DS)

    rw_pad = jnp.zeros((D, LANES), F32).at[:, :N_EXPERTS].set(router_w.astype(F32))
    rw_hi = rw_pad.astype(BF16)
    rw_lo = jnp.concatenate([rw_hi, (rw_pad - rw_hi.astype(F32)).astype(BF16)], axis=1)
    rwt = router_w.T.astype(F32)
    rbias = router_bias.astype(F32)
    n_tiles = T_ALL // MOE_TM + N_CLASSES
    outs = {}

    for l in range(DEPTH):
        mixes = []
        if l % 2 == 0:
            e = l // 2
            w_in = ev_w_in[e]
            w_out = ev_w_out[e]
            lg = jnp.stack([jnp.log1p(-jnp.exp2(ret_decay_fwd[e].astype(F32))),
                            jnp.log1p(-jnp.exp2(ret_decay_bwd[e].astype(F32)))])
            lam_init = 0.8 - 0.6 * math.exp(-0.3 * l)
            lq1, lk1, lq2, lk2 = diff_lambda[e].astype(F32)
            lam = (jnp.exp(jnp.sum(lq1 * lk1)) - jnp.exp(jnp.sum(lq2 * lk2)) + lam_init).reshape(1)
            kscale = RET_DK ** -0.5
            for gi, g in enumerate(groups):
                tm_in = tm
                crow = functools.partial(g["cond_row"], tm=tm_in)
                if gi == 0:
                    secs = (("none", 1.0, False), ("none", kscale, False), ("none", 1.0, False),
                            ("none", 1.0, False), ("none", 1.0, False), ("none", 1.0, True),
                            ("none", 1.0, True))
                    proj, kd, vd = _in_proj(g["x"], mod, l, w_in, secs, crow, tm_in, tm=tm_in)
                    outs.setdefault("kd", []).append(kd.reshape(B, 1, S, H_DIFF, 2 * DIFF_QK))
                    outs.setdefault("vd", []).append(vd.reshape(B, 1, S, H_DIFF, DIFF_V))
                    ret, sf, sb = _retention(proj, lg, ret_gn_w[e], g["batch"], g["seq"], write_state=True,
                                             heads=H_RET)
                    outs.setdefault("sf", []).append(sf.reshape(B, 1, H_RET, RET_DK, RET_DV))
                    outs.setdefault("sb", []).append(sb.reshape(B, 1, H_RET, RET_DK, RET_DV))
                    att = _diff_attention(proj, lam, diff_subln_w[e], lam_init, g["batch"], g["seq"],
                                          tq=min(256, g["seq"]), heads=H_DIFF)
                else:
                    secs = (("ret", 1.0, False), ("ret", kscale, False), ("none", 1.0, False),
                            ("none", 1.0, False), ("diff", 1.0, False), ("diff", 1.0, False),
                            ("none", 1.0, False))
                    tabs = {"ret": _rope_tables(g["seq"], RET_DK), "diff": _rope_tables(g["seq"], DIFF_QK)}
                    (proj,) = _in_proj(g["x"], mod, l, w_in, secs, crow, g["seq"], rope_tabs=tabs, tm=tm_in)
                    ret, = _retention(proj, lg, ret_gn_w[e], g["batch"], g["seq"],
                                      states=(state_ret_fwd[:, e], state_ret_bwd[:, e]))
                    ctx = (cache_diff_k[:, e].reshape(DB, PAST, H_DIFF * 2 * DIFF_QK),
                           cache_diff_v[:, e].reshape(DB, PAST, H_DIFF * DIFF_V))
                    att = _diff_attention(proj, lam, diff_subln_w[e], lam_init, g["batch"], g["seq"], ctx=ctx,
                                          tq=1024, heads=1)
                mixes.append((ret, att))
        else:
            o = l // 2
            w_in = hy_w_in[o]
            w_out = hy_w_out[o]
            secs = (("none", 1.0, False),) * (3 * D // SEC)
            for gi, g in enumerate(groups):
                if u_next is None:
                    crow = functools.partial(g["cond_row"], tm=tm)
                    (u,) = _in_proj(g["x"], mod, l, w_in, secs, crow, tm, tm=tm)
                    u_off = 0
                else:
                    u, u_off = u_next, g["off"]
                taps = _hyena_taps(g["seq"], hy_ffn_w1[o], hy_ffn_b1[o], hy_ffn_w2[o], hy_ffn_b2[o],
                                   hy_ffn_w3[o], hy_freq[o])
                blk = min(g["seq"], 512)
                n_seq = max(1, 1024 // g["seq"])
                z = _hyena_conv(u, hy_conv_w[o], hy_conv_b[o], taps, hy_filter_bias[o], g["batch"], g["seq"],
                                blk, n_seq, row_off=u_off)
                mixes.append((z,))

        x1_all, h2_all, info_all, counts = _post_mix(mixes, [(g["x"], g["x_off"]) for g in groups], w_out, mod, l, ln_w[l, 0],
                                                     ln_b[l, 0], rw_hi, rw_lo, rbias, cond_row_all, tm)
        packed = info_all
        counts_i = counts[:, 0].astype(jnp.int32)
        plan = _moe_plan(counts_i, n_tiles)
        hs = _dispatch(h2_all, packed, plan, counts_i, n_tiles)
        ys = _moe(hs, plan, l, rwt, moe_w_gate, moe_w_up, moe_w_down)
        if l + 1 < DEPTH and (l + 1) % 2 == 1:
            x_all, u_next = _fin_proj(x1_all, ys, packed, plan[3], mod, l, ln_w[l, 1], ln_b[l, 1],
                                      hy_w_in[(l + 1) // 2], cond_row_all, tm=tm)
            for g in groups:
                g["x"], g["x_off"] = x_all, g["off"]
        else:
            u_next = None
            xs = _fin(x1_all, ys, packed, plan[3], mod, l, ln_w[l, 1], ln_b[l, 1], cond_row_all, (TP, TS))
            for g, x in zip(groups, xs):
                g["x"], g["x_off"] = x, 0

    y_prompt = groups[0]["x"].reshape(B, S, D)
    y_sample = groups[1]["x"].reshape(DB, DS, D)
    cat = lambda xs: xs[0] if len(xs) == 1 else jnp.concatenate(xs, axis=1)
    return (y_prompt, y_sample, cat(outs["kd"]), cat(outs["vd"]), cat(outs["sf"]), cat(outs["sb"]))
```
